```python
import math
import jax, jax.numpy as jnp
from jax import lax
import numpy as np

D_MODEL = 2048
BATCH = 4
SEQ = 2048
DEPTH = 1
DEC_BATCH = 128
DEC_SEQ = 1
PAST_LEN = 16384
PAGE_SIZE = 128

D_MIX = D_MODEL
RG_WIDTH = D_MIX // 2
RG_BLOCKS = 8
RG_BLOCK = RG_WIDTH // RG_BLOCKS
CONV_W = 4
RG_C = 8.0
ML_HEADS = 4
ML_WIDTH = D_MIX - RG_WIDTH
ML_DV = ML_WIDTH // ML_HEADS
ML_DK = ML_DV // 2
ML_CHUNK = 64
N_MEM = 256
XA_HEADS = 4
XA_DH = D_MODEL // XA_HEADS
D_FF = -(-8 * D_MODEL // (3 * 256)) * 256
EPS = 1e-6
NEG = -1e30

OFF_RGX = 0
OFF_RGG = OFF_RGX + RG_WIDTH
OFF_Q = OFF_RGG + RG_WIDTH
OFF_K = OFF_Q + ML_HEADS * ML_DK
OFF_V = OFF_K + ML_HEADS * ML_DK
OFF_O = OFF_V + ML_WIDTH
OFF_I = OFF_O + ML_WIDTH
OFF_F = OFF_I + ML_HEADS
IN_W = OFF_F + ML_HEADS

kernel_name = "hymba_rglru_mlstm_memxattn_decode_step"


def rmsnorm(x, g):
    xf = x.astype(jnp.float32)
    y = xf * lax.rsqrt(jnp.mean(xf * xf, axis=-1, keepdims=True) + EPS)
    return (y * g.astype(jnp.float32)).astype(x.dtype)


def causal_conv(x, buf, w, b):
    L = x.shape[1]
    xp = jnp.concatenate([buf.astype(x.dtype), x], axis=1)
    y = b + sum(xp[:, j:j + L] * w[j] for j in range(CONV_W))
    return y, xp[:, L:]


def block_diag(x, w, b):
    B, L, _ = x.shape
    xb = x.reshape(B, L, RG_BLOCKS, RG_BLOCK)
    return jnp.einsum('blnc,ncd->blnd', xb, w).reshape(B, L, RG_WIDTH) + b


def rglru(xr, r, i, lam, h0):
    log_a = -RG_C * r * jax.nn.softplus(-lam)
    a = jnp.exp(log_a)
    mult = jnp.sqrt(jnp.maximum(1.0 - jnp.exp(2.0 * log_a), 0.0))
    bx = mult * (i * xr)
    bx = bx.at[:, 0].add(a[:, 0] * h0)

    def comb(c1, c2):
        a1, b1 = c1
        a2, b2 = c2
        return a1 * a2, a2 * b1 + b2

    _, h = lax.associative_scan(comb, (a, bx), axis=1)
    return h, h[:, -1]


def mlstm(q, k, v, logi, logf, C0, n0, m0):
    B, L = q.shape[0], q.shape[1]
    cs = min(ML_CHUNK, L)
    pad = (-L) % cs
    nc = (L + pad) // cs

    def prep(t, fill):
        t = jnp.moveaxis(t.astype(jnp.float32), 2, 1)
        widths = [(0, 0), (0, 0), (0, pad)] + [(0, 0)] * (t.ndim - 3)
        t = jnp.pad(t, widths, constant_values=fill)
        t = t.reshape(t.shape[:2] + (nc, cs) + t.shape[3:])
        return jnp.moveaxis(t, 2, 0)

    qs, ks, vs = prep(q, 0.0), prep(k, 0.0), prep(v, 0.0)
    lis, lfs = prep(logi, NEG), prep(logf, 0.0)
    causal = jnp.tril(jnp.ones((cs, cs), dtype=bool))

    def step(carry, inp):
        C, n, m = carry
        qc, kc, vc, li, lf = inp
        bcum = jnp.cumsum(lf, axis=-1)
        logD = bcum[..., :, None] - bcum[..., None, :] + li[..., None, :]
        logD = jnp.where(causal, logD, NEG)
        inter = bcum + m[..., None]
        m_t = jnp.maximum(inter, jnp.max(logD, axis=-1))
        D = jnp.exp(logD - m_t[..., None])
        sc = jnp.exp(inter - m_t)
        qk = jnp.einsum('bhtd,bhsd->bhts', qc, kc) * D
        num = sc[..., None] * jnp.einsum('bhtd,bhdv->bhtv', qc, C) + jnp.einsum('bhts,bhsv->bhtv', qk, vc)
        den = sc * jnp.einsum('bhtd,bhd->bht', qc, n) + jnp.sum(qk, axis=-1)
        den = jnp.maximum(jnp.abs(den), jnp.exp(-m_t))
        h = num / den[..., None]
        m_new = m_t[..., -1]
        w_end = jnp.exp(bcum[..., -1:] - bcum + li - m_new[..., None])
        dec = jnp.exp(bcum[..., -1] + m - m_new)
        C_new = dec[..., None, None] * C + jnp.einsum('bhs,bhsd,bhsv->bhdv', w_end, kc, vc)
        n_new = dec[..., None] * n + jnp.einsum('bhs,bhsd->bhd', w_end, kc)
        return (C_new, n_new, m_new), h

    carry0 = (C0.astype(jnp.float32), n0.astype(jnp.float32), m0.astype(jnp.float32))
    (C, n, m), hs = lax.scan(step, carry0, (qs, ks, vs, lis, lfs))
    hs = jnp.moveaxis(hs, 0, 2).reshape(B, ML_HEADS, nc * cs, ML_DV)[:, :, :L]
    return jnp.moveaxis(hs, 1, 2), C, n, m


def mem_kv(mem, g_mem, w_k, w_v):
    B = mem.shape[0]
    mn = rmsnorm(mem, g_mem)
    mk = (mn @ w_k).reshape(B, N_MEM, XA_HEADS, XA_DH)
    mv = (mn @ w_v).reshape(B, N_MEM, XA_HEADS, XA_DH)
    return mk, mv


def layer(x, conv_buf, h0, C0, n0, m0, mk, mv,
          g_mix, w_in, conv_w, conv_b, w_rg_a, b_rg_a, w_rg_x, b_rg_x, rg_lambda,
          b_ml_i, b_ml_f, g_rg_out, g_ml_out, w_out,
          g_xa, w_xa_q, w_xa_o, g_ffn, w_ffn_gate, w_ffn_up, w_ffn_down):
    B, L, _ = x.shape
    dt = x.dtype
    xn = rmsnorm(x, g_mix)
    z = xn @ w_in
    xr, conv_new = causal_conv(z[..., OFF_RGX:OFF_RGG], conv_buf, conv_w, conv_b)
    r = jax.nn.sigmoid(block_diag(xr, w_rg_a, b_rg_a).astype(jnp.float32))
    ig = jax.nn.sigmoid(block_diag(xr, w_rg_x, b_rg_x).astype(jnp.float32))
    h_rg, h_last = rglru(xr.astype(jnp.float32), r, ig, rg_lambda.astype(jnp.float32),
                         h0.astype(jnp.float32))
    y_rg = rmsnorm((h_rg.astype(dt) * jax.nn.gelu(z[..., OFF_RGG:OFF_Q])), g_rg_out)
    q = z[..., OFF_Q:OFF_K].reshape(B, L, ML_HEADS, ML_DK)
    k = z[..., OFF_K:OFF_V].reshape(B, L, ML_HEADS, ML_DK) * (ML_DK ** -0.5)
    v = z[..., OFF_V:OFF_O].reshape(B, L, ML_HEADS, ML_DV)
    logi = (z[..., OFF_I:OFF_F] + b_ml_i).astype(jnp.float32)
    logf = jax.nn.log_sigmoid((z[..., OFF_F:IN_W] + b_ml_f).astype(jnp.float32))
    h_ml, C_new, n_new, m_new = mlstm(q, k, v, logi, logf, C0, n0, m0)
    h_ml = rmsnorm(h_ml.astype(dt), g_ml_out.reshape(ML_HEADS, ML_DV)).reshape(B, L, ML_WIDTH)
    y_ml = h_ml * jax.nn.sigmoid(z[..., OFF_O:OFF_I])
    x = x + jnp.concatenate([y_rg, y_ml], axis=-1) @ w_out
    xq = (rmsnorm(x, g_xa) @ w_xa_q).reshape(B, L, XA_HEADS, XA_DH)
    s = jnp.einsum('blhd,bmhd->bhlm', xq, mk.astype(dt)).astype(jnp.float32) * (XA_DH ** -0.5)
    p = jax.nn.softmax(s, axis=-1).astype(dt)
    o = jnp.einsum('bhlm,bmhd->blhd', p, mv.astype(dt)).reshape(B, L, D_MODEL)
    x = x + o @ w_xa_o
    xf = rmsnorm(x, g_ffn)
    x = x + (jax.nn.silu(xf @ w_ffn_gate) * (xf @ w_ffn_up)) @ w_ffn_down
    return x, (h_last.astype(dt), conv_new.astype(dt), C_new.astype(dt), n_new.astype(dt), m_new.astype(dt))


def setup_inputs(seed: int = 0) -> dict:
    key = jax.random.key(seed)
    ks = jax.random.split(key, 40)
    f32 = jnp.float32
    nrm = lambda i, shape, s: jax.random.normal(ks[i], shape, f32) * s
    gain = lambda i, shape: 1.0 + 0.05 * jax.random.normal(ks[i], shape, f32)
    a0 = jax.random.uniform(ks[10], (DEPTH, RG_WIDTH), f32, 0.9, 0.999)
    return {
        "x_prompt": nrm(0, (BATCH, SEQ, D_MODEL), 1.0),
        "x_sample": nrm(1, (DEC_BATCH, DEC_SEQ, D_MODEL), 1.0),
        "mem_prompt": nrm(2, (BATCH, N_MEM, D_MODEL), 1.0),
        "state_rg_h": nrm(3, (DEPTH, DEC_BATCH, RG_WIDTH), 0.5),
        "state_rg_conv": nrm(4, (DEPTH, DEC_BATCH, CONV_W - 1, RG_WIDTH), 1.0),
        "state_ml_C": nrm(5, (DEPTH, DEC_BATCH, ML_HEADS, ML_DK, ML_DV), 0.1),
        "state_ml_n": nrm(6, (DEPTH, DEC_BATCH, ML_HEADS, ML_DK), 0.1),
        "state_ml_m": nrm(7, (DEPTH, DEC_BATCH, ML_HEADS), 1.0),
        "cache_mem_k": nrm(8, (DEPTH, DEC_BATCH, N_MEM, XA_HEADS, XA_DH), 1.0),
        "cache_mem_v": nrm(9, (DEPTH, DEC_BATCH, N_MEM, XA_HEADS, XA_DH), 1.0),
        "g_mix": gain(11, (DEPTH, D_MODEL)),
        "w_in": nrm(12, (DEPTH, D_MODEL, IN_W), D_MODEL ** -0.5),
        "conv_w": nrm(13, (DEPTH, CONV_W, RG_WIDTH), CONV_W ** -0.5),
        "conv_b": nrm(14, (DEPTH, RG_WIDTH), 0.02),
        "w_rg_a": nrm(15, (DEPTH, RG_BLOCKS, RG_BLOCK, RG_BLOCK), RG_BLOCK ** -0.5),
        "b_rg_a": nrm(16, (DEPTH, RG_WIDTH), 0.02),
        "w_rg_x": nrm(17, (DEPTH, RG_BLOCKS, RG_BLOCK, RG_BLOCK), RG_BLOCK ** -0.5),
        "b_rg_x": nrm(18, (DEPTH, RG_WIDTH), 0.02),
        "rg_lambda": jnp.log(a0) - jnp.log1p(-a0),
        "b_ml_i": nrm(19, (DEPTH, ML_HEADS), 0.1),
        "b_ml_f": 3.0 + jax.random.uniform(ks[20], (DEPTH, ML_HEADS), f32, 0.0, 3.0),
        "g_rg_out": gain(21, (DEPTH, RG_WIDTH)),
        "g_ml_out": gain(22, (DEPTH, ML_WIDTH)),
        "w_out": nrm(23, (DEPTH, D_MIX, D_MODEL), D_MIX ** -0.5),
        "g_xa": gain(24, (DEPTH, D_MODEL)),
        "g_mem": gain(25, (DEPTH, D_MODEL)),
        "w_xa_q": nrm(26, (DEPTH, D_MODEL, D_MODEL), D_MODEL ** -0.5),
        "w_xa_k": nrm(27, (DEPTH, D_MODEL, D_MODEL), D_MODEL ** -0.5),
        "w_xa_v": nrm(28, (DEPTH, D_MODEL, D_MODEL), D_MODEL ** -0.5),
        "w_xa_o": nrm(29, (DEPTH, D_MODEL, D_MODEL), D_MODEL ** -0.5),
        "g_ffn": gain(30, (DEPTH, D_MODEL)),
        "w_ffn_gate": nrm(31, (DEPTH, D_MODEL, D_FF), D_MODEL ** -0.5),
        "w_ffn_up": nrm(32, (DEPTH, D_MODEL, D_FF), D_MODEL ** -0.5),
        "w_ffn_down": nrm(33, (DEPTH, D_FF, D_MODEL), D_FF ** -0.5),
        "g_final": gain(34, (D_MODEL,)),
    }


def reference(x_prompt, x_sample, mem_prompt, state_rg_h, state_rg_conv, state_ml_C, state_ml_n,
              state_ml_m, cache_mem_k, cache_mem_v,
              g_mix, w_in, conv_w, conv_b, w_rg_a, b_rg_a, w_rg_x, b_rg_x, rg_lambda,
              b_ml_i, b_ml_f, g_rg_out, g_ml_out, w_out,
              g_xa, g_mem, w_xa_q, w_xa_k, w_xa_v, w_xa_o,
              g_ffn, w_ffn_gate, w_ffn_up, w_ffn_down, g_final):
    dt = x_prompt.dtype
    xp, xs = x_prompt, x_sample
    Bp = xp.shape[0]
    p_st, s_st, p_mk, p_mv = [], [], [], []
    for l in range(DEPTH):
        w = (g_mix[l], w_in[l], conv_w[l], conv_b[l], w_rg_a[l], b_rg_a[l], w_rg_x[l], b_rg_x[l],
             rg_lambda[l], b_ml_i[l], b_ml_f[l], g_rg_out[l], g_ml_out[l], w_out[l],
             g_xa[l], w_xa_q[l], w_xa_o[l], g_ffn[l], w_ffn_gate[l], w_ffn_up[l], w_ffn_down[l])
        mk, mv = mem_kv(mem_prompt, g_mem[l], w_xa_k[l], w_xa_v[l])
        xp, st_p = layer(xp,
                         jnp.zeros((Bp, CONV_W - 1, RG_WIDTH), dt),
                         jnp.zeros((Bp, RG_WIDTH), dt),
                         jnp.zeros((Bp, ML_HEADS, ML_DK, ML_DV), dt),
                         jnp.zeros((Bp, ML_HEADS, ML_DK), dt),
                         jnp.zeros((Bp, ML_HEADS), dt),
                         mk, mv, *w)
        p_st.append(st_p)
        p_mk.append(mk)
        p_mv.append(mv)
        xs, st_s = layer(xs, state_rg_conv[l], state_rg_h[l], state_ml_C[l], state_ml_n[l],
                         state_ml_m[l], cache_mem_k[l], cache_mem_v[l], *w)
        s_st.append(st_s)
    y_prompt = rmsnorm(xp, g_final)
    y_sample = rmsnorm(xs, g_final)
    stk = lambda lst, j: jnp.stack([s[j] for s in lst], axis=0)
    return (y_prompt, y_sample,
            stk(p_st, 0), stk(p_st, 1), stk(p_st, 2), stk(p_st, 3), stk(p_st, 4),
            jnp.stack(p_mk, axis=0), jnp.stack(p_mv, axis=0),
            stk(s_st, 0), stk(s_st, 1), stk(s_st, 2), stk(s_st, 3), stk(s_st, 4))
```

```python
import functools

import jax
import jax.numpy as jnp
from jax import lax
from jax.experimental import pallas as pl
from jax.experimental.pallas import tpu as pltpu

F32 = jnp.float32
BF16 = jnp.bfloat16

D_MODEL = 2048
RG_WIDTH = D_MODEL // 2
RG_BLOCKS = 8
RG_BLOCK = RG_WIDTH // RG_BLOCKS
CONV_W = 4
RG_C = 8.0
ML_HEADS = 4
ML_WIDTH = D_MODEL - RG_WIDTH
ML_DV = ML_WIDTH // ML_HEADS
ML_DK = ML_DV // 2
N_MEM = 256
XA_HEADS = 4
XA_DH = D_MODEL // XA_HEADS
EPS = 1e-6
NEG = -1e30

OFF_RGX = 0
OFF_RGG = OFF_RGX + RG_WIDTH
OFF_Q = OFF_RGG + RG_WIDTH
OFF_K = OFF_Q + ML_HEADS * ML_DK
OFF_V = OFF_K + ML_HEADS * ML_DK
OFF_O = OFF_V + ML_WIDTH
OFF_I = OFF_O + ML_WIDTH
IN_MAIN = OFF_I
N_GATE = 2 * ML_HEADS

V7X_LANES = 128
V7X_SUBLANES = 8
V7X_VMEM_BYTES = 64 * 2**20

ML_CHUNK = 256


def _params(n_axes, vmem_mib):
    assert vmem_mib * 2**20 <= V7X_VMEM_BYTES
    return pltpu.CompilerParams(
        dimension_semantics=("arbitrary",) * n_axes,
        vmem_limit_bytes=vmem_mib * 2**20,
    )


def _rms(x, g):
    ms = jnp.mean(x * x, axis=-1, keepdims=True)
    return x * lax.rsqrt(ms + EPS) * g


def _softplus(u):
    return jnp.maximum(u, 0.0) + jnp.log1p(jnp.exp(-jnp.abs(u)))


def _log_sigmoid(u):
    return -_softplus(-u)


def _gelu_tanh(x):
    return x * (0.5 * (1.0 + jnp.tanh(0.7978845608028654 * (x + 0.044715 * (x * x * x)))))


def _mm(a, b):
    return jnp.dot(a.astype(BF16), b.astype(BF16), preferred_element_type=F32)


def _norm_linear_kernel(*refs, with_gate):
    if with_gate:
        x_ref, g_ref, w_ref, wg_ref, o_ref, og_ref, xn_ref = refs
    else:
        x_ref, g_ref, w_ref, o_ref, xn_ref = refs

    @pl.when(pl.program_id(1) == 0)
    def _():
        xn = _rms(x_ref[...], g_ref[...]).astype(BF16)
        xn_ref[...] = xn
        if with_gate:
            og_ref[...] = jnp.dot(xn, wg_ref[...].astype(BF16), preferred_element_type=F32)

    o_ref[...] = jnp.dot(
        xn_ref[...], w_ref[...].astype(BF16), preferred_element_type=F32
    ).astype(o_ref.dtype)


def _norm_linear(x, g, w, *, n_out, tm, tn, out_dtype=F32, w_gate=None, vmem_mib=48, name):
    m, k = x.shape
    assert m % tm == 0 and n_out % tn == 0 and w.shape[0] == k and n_out <= w.shape[1]
    with_gate = w_gate is not None
    in_specs = [
        pl.BlockSpec((tm, k), lambda i, j: (i, 0)),
        pl.BlockSpec((1, k), lambda i, j: (0, 0)),
        pl.BlockSpec((k, tn), lambda i, j: (0, j)),
    ]
    out_shape = [jax.ShapeDtypeStruct((m, n_out), out_dtype)]
    out_specs = [pl.BlockSpec((tm, tn), lambda i, j: (i, j))]
    args = [x, g.reshape(1, k), w]
    if with_gate:
        ng = w_gate.shape[1]
        in_specs.append(pl.BlockSpec((k, ng), lambda i, j: (0, 0)))
        out_shape.append(jax.ShapeDtypeStruct((m, ng), F32))
        out_specs.append(pl.BlockSpec((tm, ng), lambda i, j: (i, 0)))
        args.append(w_gate)
    out = pl.pallas_call(
        functools.partial(_norm_linear_kernel, with_gate=with_gate),
        grid=(m // tm, n_out // tn),
        in_specs=in_specs,
        out_specs=out_specs,
        out_shape=out_shape,
        scratch_shapes=[pltpu.VMEM((tm, k), BF16)],
        compiler_params=_params(2, vmem_mib),
        name=name,
    )(*args)
    return out if with_gate else out[0]


def _linear_res_kernel(*refs, n_in):
    a_refs = refs[:n_in]
    w_refs = refs[n_in:2 * n_in]
    res_ref = refs[2 * n_in]
    o_ref = refs[2 * n_in + 1]
    acc = res_ref[...]
    for a_ref, w_ref in zip(a_refs, w_refs):
        acc = acc + _mm(a_ref[...], w_ref[...])
    o_ref[...] = acc


def _linear_residual(parts, w, res, *, tm, tn, vmem_mib=48, name):
    m, n = res.shape
    kp = parts[0].shape[1]
    assert all(p.shape == (m, kp) for p in parts) and w.shape == (kp * len(parts), n)
    assert m % tm == 0 and n % tn == 0
    n_in = len(parts)
    in_specs = [pl.BlockSpec((tm, kp), lambda i, j: (i, 0)) for _ in parts]
    in_specs += [pl.BlockSpec((kp, tn), lambda i, j, p=p: (p, j)) for p in range(n_in)]
    in_specs.append(pl.BlockSpec((tm, tn), lambda i, j: (i, j)))
    return pl.pallas_call(
        functools.partial(_linear_res_kernel, n_in=n_in),
        grid=(m // tm, n // tn),
        in_specs=in_specs,
        out_specs=pl.BlockSpec((tm, tn), lambda i, j: (i, j)),
        out_shape=jax.ShapeDtypeStruct((m, n), F32),
        compiler_params=_params(2, vmem_mib),
        name=name,
    )(*parts, *([w] * n_in), res)


def _ffn_kernel(x_ref, g_ref, wg_ref, wu_ref, wd_ref, gf_ref, o_ref, xf_ref):
    f = pl.program_id(1)

    @pl.when(f == 0)
    def _():
        x = x_ref[...]
        xf_ref[...] = _rms(x, g_ref[...]).astype(BF16)
        o_ref[...] = x

    xf = xf_ref[...]
    gate = jnp.dot(xf, wg_ref[...].astype(BF16), preferred_element_type=F32)
    up = jnp.dot(xf, wu_ref[...].astype(BF16), preferred_element_type=F32)
    hidden = (gate * jax.nn.sigmoid(gate)) * up
    o_ref[...] += _mm(hidden, wd_ref[...])

    @pl.when(f == pl.num_programs(1) - 1)
    def _():
        o_ref[...] = _rms(o_ref[...], gf_ref[...])


def _ffn(x, g, w_gate, w_up, w_down, g_final, *, tm, tf, vmem_mib, name):
    m, d = x.shape
    dff = w_gate.shape[1]
    assert m % tm == 0 and dff % tf == 0
    return pl.pallas_call(
        _ffn_kernel,
        grid=(m // tm, dff // tf),
        in_specs=[
            pl.BlockSpec((tm, d), lambda i, f: (i, 0)),
            pl.BlockSpec((1, d), lambda i, f: (0, 0)),
            pl.BlockSpec((d, tf), lambda i, f: (0, f)),
            pl.BlockSpec((d, tf), lambda i, f: (0, f)),
            pl.BlockSpec((tf, d), lambda i, f: (f, 0)),
            pl.BlockSpec((1, d), lambda i, f: (0, 0)),
        ],
        out_specs=pl.BlockSpec((tm, d), lambda i, f: (i, 0)),
        out_shape=jax.ShapeDtypeStruct((m, d), F32),
        scratch_shapes=[pltpu.VMEM((tm, d), BF16)],
        compiler_params=_params(2, vmem_mib),
        name=name,
    )(x, g.reshape(1, d), w_gate, w_up, w_down, g_final.reshape(1, d))


def _rg_gates(xr, wgate_ref, ba_ref, bx_ref, lam_ref, a_ref, b_ref):
    for n in range(RG_BLOCKS):
        sl = slice(n * RG_BLOCK, (n + 1) * RG_BLOCK)
        xn = xr[:, sl]
        g = _mm(xn, wgate_ref[n])
        r = jax.nn.sigmoid(g[:, :RG_BLOCK] + ba_ref[:, sl])
        ig = jax.nn.sigmoid(g[:, RG_BLOCK:] + bx_ref[:, sl])
        log_a = -RG_C * r * _softplus(-lam_ref[:, sl])
        a_ref[:, sl] = jnp.exp(log_a)
        mult = jnp.sqrt(jnp.maximum(1.0 - jnp.exp(2.0 * log_a), 0.0))
        b_ref[:, sl] = mult * (ig * xn)


def _rglru_seq_kernel(zx_ref, zg_ref, conv0_ref, h0_ref, cw_ref, cb_ref, wgate_ref, ba_ref,
                      bx_ref, lam_ref, gout_ref, y_ref, hlast_ref, convn_ref,
                      xe_ref, a_ref, b_ref, h_ref, hc_ref, *, tl):
    t = pl.program_id(1)
    pad = V7X_SUBLANES

    @pl.when(t == 0)
    def _():
        xe_ref[pad - 3:pad, :] = conv0_ref[...]
        hc_ref[...] = h0_ref[...]

    @pl.when(t > 0)
    def _():
        xe_ref[pad - 3:pad, :] = xe_ref[tl + pad - 3:tl + pad, :]

    x = zx_ref[...]
    xe_ref[pad:tl + pad, :] = x
    xr = (xe_ref[pad - 3:tl + pad - 3, :] * cw_ref[0:1, :]
          + xe_ref[pad - 2:tl + pad - 2, :] * cw_ref[1:2, :]
          + xe_ref[pad - 1:tl + pad - 1, :] * cw_ref[2:3, :]
          + x * cw_ref[3:4, :]) + cb_ref[...]
    _rg_gates(xr, wgate_ref, ba_ref, bx_ref, lam_ref, a_ref, b_ref)

    row = lax.broadcasted_iota(jnp.int32, (V7X_SUBLANES, RG_WIDTH), 0)

    def group(gi, hc):
        r0 = pl.multiple_of(gi * V7X_SUBLANES, V7X_SUBLANES)
        a8 = a_ref[pl.ds(r0, V7X_SUBLANES), :]
        b8 = b_ref[pl.ds(r0, V7X_SUBLANES), :]
        for d in (1, 2, 4):
            keep = row >= d
            b8 = jnp.where(keep, a8 * pltpu.roll(b8, d, axis=0) + b8, b8)
            a8 = jnp.where(keep, a8 * pltpu.roll(a8, d, axis=0), a8)
        h8 = a8 * hc + b8
        h_ref[pl.ds(r0, V7X_SUBLANES), :] = h8
        return h8[V7X_SUBLANES - 1:V7X_SUBLANES, :]

    hc = lax.fori_loop(0, tl // V7X_SUBLANES, group, hc_ref[...], unroll=4)
    hc_ref[...] = hc
    hlast_ref[...] = hc
    convn_ref[...] = xe_ref[tl + pad - 3:tl + pad, :]
    y = h_ref[...] * _gelu_tanh(zg_ref[...])
    y_ref[...] = _rms(y, gout_ref[...]).astype(y_ref.dtype)


def _rglru_seq(z, conv0, h0, cw, cb, wgate, ba, bx, lam, gout, *, batch, seq, tl, name):
    nt = seq // tl
    assert seq % tl == 0
    w = RG_WIDTH
    row = lambda v: v.reshape(1, w)
    const2 = lambda b, t: (0, 0)
    return pl.pallas_call(
        functools.partial(_rglru_seq_kernel, tl=tl),
        grid=(batch, nt),
        in_specs=[
            pl.BlockSpec((tl, w), lambda b, t: (b * nt + t, OFF_RGX // w)),
            pl.BlockSpec((tl, w), lambda b, t: (b * nt + t, OFF_RGG // w)),
            pl.BlockSpec((None, CONV_W - 1, w), lambda b, t: (b, 0, 0)),
            pl.BlockSpec((None, 1, w), lambda b, t: (b, 0, 0)),
            pl.BlockSpec((CONV_W, w), const2),
            pl.BlockSpec((1, w), const2),
            pl.BlockSpec((RG_BLOCKS, RG_BLOCK, 2 * RG_BLOCK), lambda b, t: (0, 0, 0)),
            pl.BlockSpec((1, w), const2),
            pl.BlockSpec((1, w), const2),
            pl.BlockSpec((1, w), const2),
            pl.BlockSpec((1, w), const2),
        ],
        out_specs=[
            pl.BlockSpec((tl, w), lambda b, t: (b * nt + t, 0)),
            pl.BlockSpec((None, 1, w), lambda b, t: (b, 0, 0)),
            pl.BlockSpec((None, CONV_W - 1, w), lambda b, t: (b, 0, 0)),
        ],
        out_shape=[
            jax.ShapeDtypeStruct((batch * seq, w), BF16),
            jax.ShapeDtypeStruct((batch, 1, w), F32),
            jax.ShapeDtypeStruct((batch, CONV_W - 1, w), F32),
        ],
        scratch_shapes=[
            pltpu.VMEM((tl + V7X_SUBLANES, w), F32),
            pltpu.VMEM((tl, w), F32),
            pltpu.VMEM((tl, w), F32),
            pltpu.VMEM((tl, w), F32),
            pltpu.VMEM((1, w), F32),
        ],
        compiler_params=_params(2, 32),
        name=name,
    )(z, z, conv0, h0.reshape(batch, 1, w), cw, row(cb), wgate, row(ba), row(bx), row(lam), row(gout))


def _rglru_step_kernel(zx_ref, zg_ref, conv_ref, h0_ref, cw_ref, cb_ref, wgate_ref, ba_ref,
                       bx_ref, lam_ref, gout_ref, y_ref, hn_ref, convn_ref, a_ref, b_ref):
    w = RG_WIDTH
    x = zx_ref[...]
    xr = (conv_ref[:, 0:w] * cw_ref[0:1, :] + conv_ref[:, w:2 * w] * cw_ref[1:2, :]
          + conv_ref[:, 2 * w:3 * w] * cw_ref[2:3, :] + x * cw_ref[3:4, :]) + cb_ref[...]
    _rg_gates(xr, wgate_ref, ba_ref, bx_ref, lam_ref, a_ref, b_ref)
    h = a_ref[...] * h0_ref[...] + b_ref[...]
    hn_ref[...] = h
    convn_ref[:, 0:2 * w] = conv_ref[:, w:3 * w]
    convn_ref[:, 2 * w:3 * w] = x
    y_ref[...] = _rms(h * _gelu_tanh(zg_ref[...]), gout_ref[...]).astype(y_ref.dtype)


def _rglru_step(z, conv, h0, cw, cb, wgate, ba, bx, lam, gout, *, name):
    nb = z.shape[0]
    w = RG_WIDTH
    row = lambda v: v.reshape(1, w)
    c0 = lambda i: (0, 0)
    return pl.pallas_call(
        _rglru_step_kernel,
        grid=(1,),
        in_specs=[
            pl.BlockSpec((nb, w), lambda i: (0, OFF_RGX // w)),
            pl.BlockSpec((nb, w), lambda i: (0, OFF_RGG // w)),
            pl.BlockSpec((nb, (CONV_W - 1) * w), c0),
            pl.BlockSpec((nb, w), c0),
            pl.BlockSpec((CONV_W, w), c0),
            pl.BlockSpec((1, w), c0),
            pl.BlockSpec((RG_BLOCKS, RG_BLOCK, 2 * RG_BLOCK), lambda i: (0, 0, 0)),
            pl.BlockSpec((1, w), c0),
            pl.BlockSpec((1, w), c0),
            pl.BlockSpec((1, w), c0),
            pl.BlockSpec((1, w), c0),
        ],
        out_specs=[
            pl.BlockSpec((nb, w), c0),
            pl.BlockSpec((nb, w), c0),
            pl.BlockSpec((nb, (CONV_W - 1) * w), c0),
        ],
        out_shape=[
            jax.ShapeDtypeStruct((nb, w), BF16),
            jax.ShapeDtypeStruct((nb, w), F32),
            jax.ShapeDtypeStruct((nb, (CONV_W - 1) * w), F32),
        ],
        scratch_shapes=[pltpu.VMEM((nb, w), F32), pltpu.VMEM((nb, w), F32)],
        compiler_params=_params(1, 32),
        name=name,
    )(z, z, conv.reshape(nb, (CONV_W - 1) * w), h0, cw, row(cb), wgate, row(ba), row(bx),
      row(lam), row(gout))


def _mlstm_seq_kernel(bi_ref, bf_ref, q_ref, k_ref, v_ref, o_ref, zg_ref, g_ref,
                      y_ref, c_ref, n_ref, m_ref, zgt_ref, cs_ref, ns_ref, *, seq, cs):
    h = pl.program_id(1)
    nc = seq // cs
    bi = bi_ref[h]
    bf = bf_ref[h]
    for c in range(nc):
        zgt_ref[c] = zg_ref[c * cs:(c + 1) * cs, :].T
    cs_ref[...] = jnp.zeros_like(cs_ref)
    ns_ref[...] = jnp.zeros_like(ns_ref)

    t_idx = lax.broadcasted_iota(jnp.int32, (cs, cs), 0)
    s_idx = lax.broadcasted_iota(jnp.int32, (cs, cs), 1)
    causal = s_idx <= t_idx
    lane = lax.broadcasted_iota(jnp.int32, (cs, V7X_LANES), 1)

    def chunk(c, m):
        r0 = pl.multiple_of(c * cs, cs)
        q = q_ref[pl.ds(r0, cs), :]
        k = k_ref[pl.ds(r0, cs), :] * (ML_DK ** -0.5)
        v = v_ref[pl.ds(r0, cs), :]
        zg = zg_ref[pl.ds(r0, cs), :]
        li_col = jnp.sum(jnp.where(lane == h, zg, 0.0), axis=1, keepdims=True) + bi
        lf_col = _log_sigmoid(
            jnp.sum(jnp.where(lane == h + ML_HEADS, zg, 0.0), axis=1, keepdims=True) + bf)
        li_row = zgt_ref[c, pl.ds(h, 1), :] + bi
        lf_row = _log_sigmoid(zgt_ref[c, pl.ds(h + ML_HEADS, 1), :] + bf)
        bcum_col = jnp.sum(jnp.where(causal, lf_row, 0.0), axis=1, keepdims=True)
        bcum_row = jnp.sum(jnp.where(t_idx <= s_idx, lf_col, 0.0), axis=0, keepdims=True)
        log_d = jnp.where(causal, bcum_col - bcum_row + li_row, NEG)
        inter = bcum_col + m
        m_t = jnp.maximum(inter, jnp.max(log_d, axis=1, keepdims=True))
        dmat = jnp.exp(log_d - m_t)
        sc = jnp.exp(inter - m_t)
        qb = q.astype(BF16)
        kb = k.astype(BF16)
        vb = v.astype(BF16)
        qk = lax.dot_general(qb, kb, (((1,), (1,)), ((), ())), preferred_element_type=F32) * dmat
        c_old = cs_ref[...]
        n_old = ns_ref[...]
        num = sc * jnp.dot(qb, c_old.astype(BF16), preferred_element_type=F32) + _mm(qk, vb)
        den = sc * jnp.sum(q * n_old, axis=1, keepdims=True) + jnp.sum(qk, axis=1, keepdims=True)
        den = jnp.maximum(jnp.abs(den), jnp.exp(-m_t))
        hh = num / den
        m_new = m_t[cs - 1:cs, :]
        b_last = bcum_col[cs - 1:cs, :]
        w_end = jnp.exp(b_last - bcum_col + li_col - m_new)
        dec = jnp.exp(b_last + m - m_new)
        wk = w_end * k
        cs_ref[...] = dec * c_old + lax.dot_general(
            wk.astype(BF16), vb, (((0,), (0,)), ((), ())), preferred_element_type=F32)
        ns_ref[...] = dec * n_old + jnp.sum(wk, axis=0, keepdims=True)
        y = _rms(hh, g_ref[...]) * jax.nn.sigmoid(o_ref[pl.ds(r0, cs), :])
        y_ref[pl.ds(r0, cs), :] = y.astype(y_ref.dtype)
        return m_new

    m_fin = lax.fori_loop(0, nc, chunk, jnp.zeros((1, 1), F32))
    c_ref[...] = cs_ref[...]
    n_ref[pl.ds(h, 1), :] = ns_ref[...]
    m_ref[pl.ds(h, 1), :] = jnp.broadcast_to(m_fin, (1, V7X_LANES))


def _mlstm_seq(z, zg, b_i, b_f, g_out, *, batch, seq, cs, name):
    assert seq % cs == 0 and zg.shape[1] == V7X_LANES
    dk, dv, nh = ML_DK, ML_DV, ML_HEADS
    smem = pl.BlockSpec(memory_space=pltpu.SMEM)
    return pl.pallas_call(
        functools.partial(_mlstm_seq_kernel, seq=seq, cs=cs),
        grid=(batch, nh),
        in_specs=[
            smem, smem,
            pl.BlockSpec((seq, dk), lambda b, h: (b, OFF_Q // dk + h)),
            pl.BlockSpec((seq, dk), lambda b, h: (b, OFF_K // dk + h)),
            pl.BlockSpec((seq, dv), lambda b, h: (b, OFF_V // dv + h)),
            pl.BlockSpec((seq, dv), lambda b, h: (b, OFF_O // dv + h)),
            pl.BlockSpec((seq, V7X_LANES), lambda b, h: (b, 0)),
            pl.BlockSpec((1, dv), lambda b, h: (0, h)),
        ],
        out_specs=[
            pl.BlockSpec((seq, dv), lambda b, h: (b, h)),
            pl.BlockSpec((None, None, dk, dv), lambda b, h: (b, h, 0, 0)),
            pl.BlockSpec((None, nh, dk), lambda b, h: (b, 0, 0)),
            pl.BlockSpec((None, nh, V7X_LANES), lambda b, h: (b, 0, 0)),
        ],
        out_shape=[
            jax.ShapeDtypeStruct((batch * seq, nh * dv), BF16),
            jax.ShapeDtypeStruct((batch, nh, dk, dv), F32),
            jax.ShapeDtypeStruct((batch, nh, dk), F32),
            jax.ShapeDtypeStruct((batch, nh, V7X_LANES), F32),
        ],
        scratch_shapes=[
            pltpu.VMEM((seq // cs, V7X_LANES, cs), F32),
            pltpu.VMEM((dk, dv), F32),
            pltpu.VMEM((1, dk), F32),
        ],
        compiler_params=_params(2, 40),
        name=name,
    )(b_i, b_f, z, z, z, z, zg, g_out.reshape(1, nh * dv))


def _mlstm_step_kernel(bi_ref, bf_ref, q_ref, k_ref, v_ref, o_ref, zg_ref, g_ref, c0_ref, n0_ref,
                       m0_ref, y_ref, c_ref, n_ref, m_ref, qc_ref, *, bs):
    dk, dv = ML_DK, ML_DV
    eye = (lax.broadcasted_iota(jnp.int32, (dk, dk), 0)
           == lax.broadcasted_iota(jnp.int32, (dk, dk), 1))
    zg = zg_ref[...]
    for h in range(ML_HEADS):
        li = zg[:, h:h + 1] + bi_ref[h]
        lf = _log_sigmoid(zg[:, ML_HEADS + h:ML_HEADS + h + 1] + bf_ref[h])
        m = m0_ref[:, h:h + 1]
        inter = lf + m
        m_t = jnp.maximum(inter, li)
        dgate = jnp.exp(li - m_t)
        sc = jnp.exp(inter - m_t)
        q = q_ref[:, h * dk:(h + 1) * dk]
        k = k_ref[:, h * dk:(h + 1) * dk] * (ML_DK ** -0.5)
        v = v_ref[:, h * dv:(h + 1) * dv]
        n_old = n0_ref[:, h, :]
        qk = jnp.sum(q * k, axis=1, keepdims=True) * dgate
        w_end = jnp.exp(li - m_t)
        dec = jnp.exp(inter - m_t)
        wk = w_end * k
        for j in range(bs):
            c_old = c0_ref[j, h]
            qc_ref[j:j + 1, :] = _mm(q[j:j + 1, :], c_old)
            wk_col = jnp.sum(
                jnp.where(eye, jnp.broadcast_to(wk[j:j + 1, :], (dk, dk)), 0.0),
                axis=1, keepdims=True)
            c_ref[j, h] = dec[j:j + 1, :] * c_old + wk_col * v[j:j + 1, :]
        num = sc * qc_ref[...] + qk * v
        den = sc * jnp.sum(q * n_old, axis=1, keepdims=True) + qk
        den = jnp.maximum(jnp.abs(den), jnp.exp(-m_t))
        hh = num / den
        n_ref[:, h, :] = dec * n_old + wk
        m_ref[:, h:h + 1] = m_t
        y = _rms(hh, g_ref[:, h * dv:(h + 1) * dv]) * jax.nn.sigmoid(o_ref[:, h * dv:(h + 1) * dv])
        y_ref[:, h * dv:(h + 1) * dv] = y.astype(y_ref.dtype)


def _mlstm_step(z, zg, c0, n0, m0, b_i, b_f, g_out, *, bs, name):
    nb = z.shape[0]
    assert nb % bs == 0
    dk, dv, nh = ML_DK, ML_DV, ML_HEADS
    smem = pl.BlockSpec(memory_space=pltpu.SMEM)
    return pl.pallas_call(
        functools.partial(_mlstm_step_kernel, bs=bs),
        grid=(nb // bs,),
        in_specs=[
            smem, smem,
            pl.BlockSpec((bs, nh * dk), lambda i: (i, OFF_Q // (nh * dk))),
            pl.BlockSpec((bs, nh * dk), lambda i: (i, OFF_K // (nh * dk))),
            pl.BlockSpec((bs, nh * dv), lambda i: (i, OFF_V // (nh * dv))),
            pl.BlockSpec((bs, nh * dv), lambda i: (i, OFF_O // (nh * dv))),
            pl.BlockSpec((bs, V7X_LANES), lambda i: (i, 0)),
            pl.BlockSpec((1, nh * dv), lambda i: (0, 0)),
            pl.BlockSpec((bs, nh, dk, dv), lambda i: (i, 0, 0, 0)),
            pl.BlockSpec((bs, nh, dk), lambda i: (i, 0, 0)),
            pl.BlockSpec((bs, nh), lambda i: (i, 0)),
        ],
        out_specs=[
            pl.BlockSpec((bs, nh * dv), lambda i: (i, 0)),
            pl.BlockSpec((bs, nh, dk, dv), lambda i: (i, 0, 0, 0)),
            pl.BlockSpec((bs, nh, dk), lambda i: (i, 0, 0)),
            pl.BlockSpec((bs, nh), lambda i: (i, 0)),
        ],
        out_shape=[
            jax.ShapeDtypeStruct((nb, nh * dv), F32),
            jax.ShapeDtypeStruct((nb, nh, dk, dv), F32),
            jax.ShapeDtypeStruct((nb, nh, dk), F32),
            jax.ShapeDtypeStruct((nb, nh), F32),
        ],
        scratch_shapes=[pltpu.VMEM((bs, dv), F32)],
        compiler_params=_params(1, 32),
        name=name,
    )(b_i, b_f, z, z, z, z, zg, g_out.reshape(1, nh * dv), c0, n0, m0)


def _softmax_rows(s):
    e = jnp.exp(s - jnp.max(s, axis=-1, keepdims=True))
    return e / jnp.sum(e, axis=-1, keepdims=True)


def _xattn_seq_kernel(q_ref, k_ref, v_ref, o_ref):
    for h in range(XA_HEADS):
        sl = slice(h * XA_DH, (h + 1) * XA_DH)
        s = lax.dot_general(q_ref[:, sl].astype(BF16), k_ref[:, sl].astype(BF16),
                            (((1,), (1,)), ((), ())), preferred_element_type=F32)
        p = _softmax_rows(s * (XA_DH ** -0.5))
        o_ref[:, sl] = _mm(p, v_ref[:, sl]).astype(o_ref.dtype)


def _xattn_seq(xq, mk, mv, *, batch, seq, tq, name):
    nt = seq // tq
    assert seq % tq == 0
    d = D_MODEL
    return pl.pallas_call(
        _xattn_seq_kernel,
        grid=(batch, nt),
        in_specs=[
            pl.BlockSpec((tq, d), lambda b, t: (b * nt + t, 0)),
            pl.BlockSpec((N_MEM, d), lambda b, t: (b, 0)),
            pl.BlockSpec((N_MEM, d), lambda b, t: (b, 0)),
        ],
        out_specs=pl.BlockSpec((tq, d), lambda b, t: (b * nt + t, 0)),
        out_shape=jax.ShapeDtypeStruct((batch * seq, d), BF16),
        compiler_params=_params(2, 40),
        name=name,
    )(xq, mk, mv)


def _xattn_step_kernel(q_ref, k_ref, v_ref, o_ref, *, sb):
    d = D_MODEL
    rows = V7X_SUBLANES
    lane = lax.broadcasted_iota(jnp.int32, (rows, d), 1)
    head_start = lax.broadcasted_iota(jnp.int32, (rows, d), 0) * XA_DH
    own = (lane >= head_start) & (lane < head_start + XA_DH)
    for j in range(sb):
        qbd = jnp.where(own, jnp.broadcast_to(q_ref[j], (rows, d)), 0.0)
        s = lax.dot_general(qbd.astype(BF16), k_ref[j].astype(BF16),
                            (((1,), (1,)), ((), ())), preferred_element_type=F32)
        p = _softmax_rows(s * (XA_DH ** -0.5))
        full = _mm(p, v_ref[j])
        o_ref[j] = jnp.sum(jnp.where(own, full, 0.0), axis=0, keepdims=True)


def _xattn_step(xq, ck, cv, *, sb, name):
    nb, d = xq.shape
    assert nb % sb == 0 and ck.shape == (nb, N_MEM, d)
    cache_spec = pl.BlockSpec((sb, N_MEM, d), lambda i: (i, 0, 0))
    return pl.pallas_call(
        functools.partial(_xattn_step_kernel, sb=sb),
        grid=(nb // sb,),
        in_specs=[pl.BlockSpec((sb, 1, d), lambda i: (i, 0, 0)), cache_spec, cache_spec],
        out_specs=pl.BlockSpec((sb, 1, d), lambda i: (i, 0, 0)),
        out_shape=jax.ShapeDtypeStruct((nb, 1, d), F32),
        compiler_params=_params(1, 40),
        name=name,
    )(xq.reshape(nb, 1, d), ck, cv).reshape(nb, d)


def _tiles(rows):
    tm = min(rows, 1024)
    assert rows % tm == 0
    return tm


def kernel(x_prompt, x_sample, mem_prompt, state_rg_h, state_rg_conv, state_ml_C, state_ml_n, state_ml_m, cache_mem_k, cache_mem_v, g_mix, w_in, conv_w, conv_b, w_rg_a, b_rg_a, w_rg_x, b_rg_x, rg_lambda, b_ml_i, b_ml_f, g_rg_out, g_ml_out, w_out, g_xa, g_mem, w_xa_q, w_xa_k, w_xa_v, w_xa_o, g_ffn, w_ffn_gate, w_ffn_up, w_ffn_down, g_final):
    depth = g_mix.shape[0]
    assert depth == 1, "single trunk layer"
    bp, seq, d = x_prompt.shape
    bs_, dec_seq, _ = x_sample.shape
    assert d == D_MODEL and dec_seq == 1
    n_mem = mem_prompt.shape[1]
    assert n_mem == N_MEM
    dff = w_ffn_gate.shape[-1]
    in_w = w_in.shape[-1]
    assert in_w == IN_MAIN + N_GATE

    w_in2 = w_in.reshape(d, in_w)
    w_gate_pad = jnp.pad(w_in2[:, IN_MAIN:], ((0, 0), (0, V7X_LANES - N_GATE)))
    cw = conv_w.reshape(CONV_W, RG_WIDTH)
    wgate = jnp.concatenate([w_rg_a.reshape(RG_BLOCKS, RG_BLOCK, RG_BLOCK),
                             w_rg_x.reshape(RG_BLOCKS, RG_BLOCK, RG_BLOCK)], axis=-1)
    rg_args = (cw, conv_b.reshape(-1), wgate, b_rg_a.reshape(-1), b_rg_x.reshape(-1),
               rg_lambda.reshape(-1), g_rg_out.reshape(-1))
    b_i = b_ml_i.reshape(ML_HEADS)
    b_f = b_ml_f.reshape(ML_HEADS)
    g_ml = g_ml_out.reshape(-1)
    w_out2 = w_out.reshape(d, d)
    w_q = w_xa_q.reshape(d, d)
    w_k = w_xa_k.reshape(d, d)
    w_v = w_xa_v.reshape(d, d)
    w_o = w_xa_o.reshape(d, d)
    w_fg = w_ffn_gate.reshape(d, dff)
    w_fu = w_ffn_up.reshape(d, dff)
    w_fd = w_ffn_down.reshape(dff, d)

    def trunk_in(x2, tag):
        tm = _tiles(x2.shape[0])
        return _norm_linear(x2, g_mix.reshape(-1), w_in2, n_out=IN_MAIN, tm=tm, tn=1024,
                            w_gate=w_gate_pad, vmem_mib=56, name=f"in_proj_{tag}")

    def trunk_out(x2, y_rg, y_ml, q_fn, tag):
        tm = _tiles(x2.shape[0])
        x1 = _linear_residual([y_rg, y_ml], w_out2, x2, tm=tm, tn=1024, name=f"mix_out_{tag}")
        xq = _norm_linear(x1, g_xa.reshape(-1), w_q, n_out=d, tm=tm, tn=1024,
                          out_dtype=q_fn[1], name=f"xa_q_{tag}")
        o = q_fn[0](xq)
        x3 = _linear_residual([o], w_o, x1, tm=tm, tn=1024, name=f"xa_out_{tag}")
        return _ffn(x3, g_ffn.reshape(-1), w_fg, w_fu, w_fd, g_final, tm=tm, tf=256,
                    vmem_mib=60, name=f"ffn_{tag}")

    tp = bp * seq
    xp = x_prompt.reshape(tp, d)
    z_p, zg_p = trunk_in(xp, "p")
    y_rg_p, p_h, p_conv = _rglru_seq(
        z_p, jnp.zeros((bp, CONV_W - 1, RG_WIDTH), F32), jnp.zeros((bp, RG_WIDTH), F32), *rg_args,
        batch=bp, seq=seq, tl=min(seq, 256), name="rglru_p")
    y_ml_p, p_c, p_n, p_m = _mlstm_seq(z_p, zg_p, b_i, b_f, g_ml, batch=bp, seq=seq,
                                       cs=min(seq, ML_CHUNK), name="mlstm_p")
    mem2 = mem_prompt.reshape(bp * n_mem, d)
    tmem = _tiles(bp * n_mem)
    mk = _norm_linear(mem2, g_mem.reshape(-1), w_k, n_out=d, tm=tmem, tn=1024, name="mem_k")
    mv = _norm_linear(mem2, g_mem.reshape(-1), w_v, n_out=d, tm=tmem, tn=1024, name="mem_v")
    attn_p = lambda xq: _xattn_seq(xq, mk, mv, batch=bp, seq=seq, tq=min(seq, 512), name="xattn_p")
    y_p = trunk_out(xp, y_rg_p, y_ml_p, (attn_p, BF16), "p")

    xs = x_sample.reshape(bs_, d)
    z_s, zg_s = trunk_in(xs, "s")
    y_rg_s, s_h, s_conv = _rglru_step(
        z_s, state_rg_conv.reshape(bs_, CONV_W - 1, RG_WIDTH), state_rg_h.reshape(bs_, RG_WIDTH),
        *rg_args, name="rglru_s")
    y_ml_s, s_c, s_n, s_m = _mlstm_step(
        z_s, zg_s, state_ml_C.reshape(bs_, ML_HEADS, ML_DK, ML_DV),
        state_ml_n.reshape(bs_, ML_HEADS, ML_DK), state_ml_m.reshape(bs_, ML_HEADS),
        b_i, b_f, g_ml, bs=V7X_SUBLANES, name="mlstm_s")
    ck = cache_mem_k.reshape(bs_, n_mem, d)
    cv = cache_mem_v.reshape(bs_, n_mem, d)
    attn_s = lambda xq: _xattn_step(xq, ck, cv, sb=2, name="xattn_s")
    y_s = trunk_out(xs, y_rg_s, y_ml_s, (attn_s, F32), "s")

    return (
        y_p.reshape(bp, seq, d),
        y_s.reshape(bs_, 1, d),
        p_h.reshape(1, bp, RG_WIDTH),
        p_conv.reshape(1, bp, CONV_W - 1, RG_WIDTH),
        p_c.reshape(1, bp, ML_HEADS, ML_DK, ML_DV),
        p_n.reshape(1, bp, ML_HEADS, ML_DK),
        p_m[:, :, 0].reshape(1, bp, ML_HEADS),
        mk.reshape(1, bp, n_mem, XA_HEADS, XA_DH),
        mv.reshape(1, bp, n_mem, XA_HEADS, XA_DH),
        s_h.reshape(1, bs_, RG_WIDTH),
        s_conv.reshape(1, bs_, CONV_W - 1, RG_WIDTH),
        s_c.reshape(1, bs_, ML_HEADS, ML_DK, ML_DV),
        s_n.reshape(1, bs_, ML_HEADS, ML_DK),
        s_m.reshape(1, bs_, ML_HEADS),
    )
```

```python
import functools

import jax
import jax.numpy as jnp
from jax import lax
from jax.experimental import pallas as pl
from jax.experimental.pallas import tpu as pltpu

F32 = jnp.float32
BF16 = jnp.bfloat16

D_MODEL = 2048
RG_WIDTH = D_MODEL // 2
RG_BLOCKS = 8
RG_BLOCK = RG_WIDTH // RG_BLOCKS
CONV_W = 4
RG_C = 8.0
ML_HEADS = 4
ML_WIDTH = D_MODEL - RG_WIDTH
ML_DV = ML_WIDTH // ML_HEADS
ML_DK = ML_DV // 2
N_MEM = 256
XA_HEADS = 4
XA_DH = D_MODEL // XA_HEADS
EPS = 1e-6
NEG = -1e30

OFF_RGX = 0
OFF_RGG = OFF_RGX + RG_WIDTH
OFF_Q = OFF_RGG + RG_WIDTH
OFF_K = OFF_Q + ML_HEADS * ML_DK
OFF_V = OFF_K + ML_HEADS * ML_DK
OFF_O = OFF_V + ML_WIDTH
OFF_I = OFF_O + ML_WIDTH
IN_MAIN = OFF_I
N_GATE = 2 * ML_HEADS

V7X_LANES = 128
V7X_SUBLANES = 8
V7X_VMEM_BYTES = 64 * 2**20

ML_CHUNK = 256


def _params(n_axes, vmem_mib):
    assert vmem_mib * 2**20 <= V7X_VMEM_BYTES
    return pltpu.CompilerParams(
        dimension_semantics=("arbitrary",) * n_axes,
        vmem_limit_bytes=vmem_mib * 2**20,
    )


def _rms(x, g):
    ms = jnp.mean(x * x, axis=-1, keepdims=True)
    return x * lax.rsqrt(ms + EPS) * g


def _softplus(u):
    return jnp.maximum(u, 0.0) + jnp.log1p(jnp.exp(-jnp.abs(u)))


def _log_sigmoid(u):
    return -_softplus(-u)


def _gelu_tanh(x):
    return x * (0.5 * (1.0 + jnp.tanh(0.7978845608028654 * (x + 0.044715 * (x * x * x)))))


def _mm(a, b):
    return jnp.dot(a.astype(BF16), b.astype(BF16), preferred_element_type=F32)


def _dot_w(a, w_ref, w_is_nk):
    w = w_ref[...].astype(BF16)
    if w_is_nk:
        return lax.dot_general(a, w, (((1,), (1,)), ((), ())), preferred_element_type=F32)
    return jnp.dot(a, w, preferred_element_type=F32)


def _norm_linear_kernel(*refs, with_gate, w_is_nk):
    if with_gate:
        x_ref, g_ref, w_ref, wg_ref, o_ref, og_ref, xn_ref = refs
    else:
        x_ref, g_ref, w_ref, o_ref, xn_ref = refs

    @pl.when(pl.program_id(1) == 0)
    def _():
        xn = _rms(x_ref[...], g_ref[...]).astype(BF16)
        xn_ref[...] = xn
        if with_gate:
            og_ref[...] = _dot_w(xn, wg_ref, w_is_nk)

    o_ref[...] = _dot_w(xn_ref[...], w_ref, w_is_nk).astype(o_ref.dtype)


def _norm_linear(x, g, w, *, n_out, tm, tn, out_dtype=F32, w_gate=None, w_is_nk=False,
                 vmem_mib=48, name):
    m, k = x.shape
    k_ax, n_ax = (1, 0) if w_is_nk else (0, 1)
    assert m % tm == 0 and n_out % tn == 0 and w.shape[k_ax] == k and n_out <= w.shape[n_ax]
    with_gate = w_gate is not None
    w_spec = (pl.BlockSpec((tn, k), lambda i, j: (j, 0)) if w_is_nk
              else pl.BlockSpec((k, tn), lambda i, j: (0, j)))
    in_specs = [
        pl.BlockSpec((tm, k), lambda i, j: (i, 0)),
        pl.BlockSpec((1, k), lambda i, j: (0, 0)),
        w_spec,
    ]
    out_shape = [jax.ShapeDtypeStruct((m, n_out), out_dtype)]
    out_specs = [pl.BlockSpec((tm, tn), lambda i, j: (i, j))]
    args = [x, g.reshape(1, k), w]
    if with_gate:
        ng = w_gate.shape[n_ax]
        in_specs.append(pl.BlockSpec(w_gate.shape, lambda i, j: (0, 0)))
        out_shape.append(jax.ShapeDtypeStruct((m, ng), F32))
        out_specs.append(pl.BlockSpec((tm, ng), lambda i, j: (i, 0)))
        args.append(w_gate)
    out = pl.pallas_call(
        functools.partial(_norm_linear_kernel, with_gate=with_gate, w_is_nk=w_is_nk),
        grid=(m // tm, n_out // tn),
        in_specs=in_specs,
        out_specs=out_specs,
        out_shape=out_shape,
        scratch_shapes=[pltpu.VMEM((tm, k), BF16)],
        compiler_params=_params(2, vmem_mib),
        name=name,
    )(*args)
    return out if with_gate else out[0]


def _linear_res_kernel(*refs, n_in):
    a_refs = refs[:n_in]
    w_refs = refs[n_in:2 * n_in]
    res_ref = refs[2 * n_in]
    o_ref = refs[2 * n_in + 1]
    acc = res_ref[...]
    for a_ref, w_ref in zip(a_refs, w_refs):
        acc = acc + _mm(a_ref[...], w_ref[...])
    o_ref[...] = acc


def _linear_residual(parts, w, res, *, tm, tn, vmem_mib=48, name):
    m, n = res.shape
    kp = parts[0].shape[1]
    assert all(p.shape == (m, kp) for p in parts) and w.shape == (kp * len(parts), n)
    assert m % tm == 0 and n % tn == 0
    n_in = len(parts)
    in_specs = [pl.BlockSpec((tm, kp), lambda i, j: (i, 0)) for _ in parts]
    in_specs += [pl.BlockSpec((kp, tn), lambda i, j, p=p: (p, j)) for p in range(n_in)]
    in_specs.append(pl.BlockSpec((tm, tn), lambda i, j: (i, j)))
    return pl.pallas_call(
        functools.partial(_linear_res_kernel, n_in=n_in),
        grid=(m // tm, n // tn),
        in_specs=in_specs,
        out_specs=pl.BlockSpec((tm, tn), lambda i, j: (i, j)),
        out_shape=jax.ShapeDtypeStruct((m, n), F32),
        compiler_params=_params(2, vmem_mib),
        name=name,
    )(*parts, *([w] * n_in), res)


def _ffn_kernel(x_ref, g_ref, wg_ref, wu_ref, wd_ref, gf_ref, o_ref, xf_ref):
    f = pl.program_id(1)

    @pl.when(f == 0)
    def _():
        x = x_ref[...]
        xf_ref[...] = _rms(x, g_ref[...]).astype(BF16)
        o_ref[...] = x

    xf = xf_ref[...]
    gate = jnp.dot(xf, wg_ref[...].astype(BF16), preferred_element_type=F32)
    up = jnp.dot(xf, wu_ref[...].astype(BF16), preferred_element_type=F32)
    hidden = (gate * jax.nn.sigmoid(gate)) * up
    o_ref[...] += _mm(hidden, wd_ref[...])

    @pl.when(f == pl.num_programs(1) - 1)
    def _():
        o_ref[...] = _rms(o_ref[...], gf_ref[...])


def _ffn(x, g, w_gate, w_up, w_down, g_final, *, tm, tf, vmem_mib, name):
    m, d = x.shape
    dff = w_gate.shape[1]
    assert m % tm == 0 and dff % tf == 0
    return pl.pallas_call(
        _ffn_kernel,
        grid=(m // tm, dff // tf),
        in_specs=[
            pl.BlockSpec((tm, d), lambda i, f: (i, 0)),
            pl.BlockSpec((1, d), lambda i, f: (0, 0)),
            pl.BlockSpec((d, tf), lambda i, f: (0, f)),
            pl.BlockSpec((d, tf), lambda i, f: (0, f)),
            pl.BlockSpec((tf, d), lambda i, f: (f, 0)),
            pl.BlockSpec((1, d), lambda i, f: (0, 0)),
        ],
        out_specs=pl.BlockSpec((tm, d), lambda i, f: (i, 0)),
        out_shape=jax.ShapeDtypeStruct((m, d), F32),
        scratch_shapes=[pltpu.VMEM((tm, d), BF16)],
        compiler_params=_params(2, vmem_mib),
        name=name,
    )(x, g.reshape(1, d), w_gate, w_up, w_down, g_final.reshape(1, d))


def _rg_gates(xr, wgate_ref, ba_ref, bx_ref, lam_ref, a_ref, b_ref):
    for n in range(RG_BLOCKS):
        sl = slice(n * RG_BLOCK, (n + 1) * RG_BLOCK)
        xn = xr[:, sl]
        g = _mm(xn, wgate_ref[n])
        r = jax.nn.sigmoid(g[:, :RG_BLOCK] + ba_ref[:, sl])
        ig = jax.nn.sigmoid(g[:, RG_BLOCK:] + bx_ref[:, sl])
        log_a = -RG_C * r * _softplus(-lam_ref[:, sl])
        a_ref[:, sl] = jnp.exp(log_a)
        mult = jnp.sqrt(jnp.maximum(1.0 - jnp.exp(2.0 * log_a), 0.0))
        b_ref[:, sl] = mult * (ig * xn)


def _rglru_seq_kernel(zx_ref, zg_ref, conv0_ref, h0_ref, cw_ref, cb_ref, wgate_ref, ba_ref,
                      bx_ref, lam_ref, gout_ref, y_ref, hlast_ref, convn_ref,
                      xe_ref, a_ref, b_ref, h_ref, hc_ref, *, tl):
    t = pl.program_id(1)
    pad = V7X_SUBLANES

    @pl.when(t == 0)
    def _():
        xe_ref[pad - 3:pad, :] = conv0_ref[...]
        hc_ref[...] = h0_ref[...]

    @pl.when(t > 0)
    def _():
        xe_ref[pad - 3:pad, :] = xe_ref[tl + pad - 3:tl + pad, :]

    x = zx_ref[...]
    xe_ref[pad:tl + pad, :] = x
    xr = (xe_ref[pad - 3:tl + pad - 3, :] * cw_ref[0:1, :]
          + xe_ref[pad - 2:tl + pad - 2, :] * cw_ref[1:2, :]
          + xe_ref[pad - 1:tl + pad - 1, :] * cw_ref[2:3, :]
          + x * cw_ref[3:4, :]) + cb_ref[...]
    _rg_gates(xr, wgate_ref, ba_ref, bx_ref, lam_ref, a_ref, b_ref)

    row = lax.broadcasted_iota(jnp.int32, (V7X_SUBLANES, RG_WIDTH), 0)

    def group(gi, hc):
        r0 = pl.multiple_of(gi * V7X_SUBLANES, V7X_SUBLANES)
        a8 = a_ref[pl.ds(r0, V7X_SUBLANES), :]
        b8 = b_ref[pl.ds(r0, V7X_SUBLANES), :]
        for d in (1, 2, 4):
            keep = row >= d
            b8 = jnp.where(keep, a8 * pltpu.roll(b8, d, axis=0) + b8, b8)
            a8 = jnp.where(keep, a8 * pltpu.roll(a8, d, axis=0), a8)
        h8 = a8 * hc + b8
        h_ref[pl.ds(r0, V7X_SUBLANES), :] = h8
        return h8[V7X_SUBLANES - 1:V7X_SUBLANES, :]

    hc = lax.fori_loop(0, tl // V7X_SUBLANES, group, hc_ref[...], unroll=4)
    hc_ref[...] = hc
    hlast_ref[...] = hc
    convn_ref[...] = xe_ref[tl + pad - 3:tl + pad, :]
    y = h_ref[...] * _gelu_tanh(zg_ref[...])
    y_ref[...] = _rms(y, gout_ref[...]).astype(y_ref.dtype)


def _rglru_seq(z, conv0, h0, cw, cb, wgate, ba, bx, lam, gout, *, batch, seq, tl, name):
    nt = seq // tl
    assert seq % tl == 0
    w = RG_WIDTH
    row = lambda v: v.reshape(1, w)
    const2 = lambda b, t: (0, 0)
    return pl.pallas_call(
        functools.partial(_rglru_seq_kernel, tl=tl),
        grid=(batch, nt),
        in_specs=[
            pl.BlockSpec((tl, w), lambda b, t: (b * nt + t, OFF_RGX // w)),
            pl.BlockSpec((tl, w), lambda b, t: (b * nt + t, OFF_RGG // w)),
            pl.BlockSpec((None, CONV_W - 1, w), lambda b, t: (b, 0, 0)),
            pl.BlockSpec((None, 1, w), lambda b, t: (b, 0, 0)),
            pl.BlockSpec((CONV_W, w), const2),
            pl.BlockSpec((1, w), const2),
            pl.BlockSpec((RG_BLOCKS, RG_BLOCK, 2 * RG_BLOCK), lambda b, t: (0, 0, 0)),
            pl.BlockSpec((1, w), const2),
            pl.BlockSpec((1, w), const2),
            pl.BlockSpec((1, w), const2),
            pl.BlockSpec((1, w), const2),
        ],
        out_specs=[
            pl.BlockSpec((tl, w), lambda b, t: (b * nt + t, 0)),
            pl.BlockSpec((None, 1, w), lambda b, t: (b, 0, 0)),
            pl.BlockSpec((None, CONV_W - 1, w), lambda b, t: (b, 0, 0)),
        ],
        out_shape=[
            jax.ShapeDtypeStruct((batch * seq, w), BF16),
            jax.ShapeDtypeStruct((batch, 1, w), F32),
            jax.ShapeDtypeStruct((batch, CONV_W - 1, w), F32),
        ],
        scratch_shapes=[
            pltpu.VMEM((tl + V7X_SUBLANES, w), F32),
            pltpu.VMEM((tl, w), F32),
            pltpu.VMEM((tl, w), F32),
            pltpu.VMEM((tl, w), F32),
            pltpu.VMEM((1, w), F32),
        ],
        compiler_params=_params(2, 32),
        name=name,
    )(z, z, conv0, h0.reshape(batch, 1, w), cw, row(cb), wgate, row(ba), row(bx), row(lam), row(gout))


def _rglru_step_kernel(zx_ref, zg_ref, conv_ref, h0_ref, cw_ref, cb_ref, wgate_ref, ba_ref,
                       bx_ref, lam_ref, gout_ref, y_ref, hn_ref, convn_ref, a_ref, b_ref):
    w = RG_WIDTH
    x = zx_ref[...]
    xr = (conv_ref[:, 0:w] * cw_ref[0:1, :] + conv_ref[:, w:2 * w] * cw_ref[1:2, :]
          + conv_ref[:, 2 * w:3 * w] * cw_ref[2:3, :] + x * cw_ref[3:4, :]) + cb_ref[...]
    _rg_gates(xr, wgate_ref, ba_ref, bx_ref, lam_ref, a_ref, b_ref)
    h = a_ref[...] * h0_ref[...] + b_ref[...]
    hn_ref[...] = h
    convn_ref[:, 0:2 * w] = conv_ref[:, w:3 * w]
    convn_ref[:, 2 * w:3 * w] = x
    y_ref[...] = _rms(h * _gelu_tanh(zg_ref[...]), gout_ref[...]).astype(y_ref.dtype)


def _rglru_step(z, conv, h0, cw, cb, wgate, ba, bx, lam, gout, *, name):
    nb = z.shape[0]
    w = RG_WIDTH
    row = lambda v: v.reshape(1, w)
    c0 = lambda i: (0, 0)
    return pl.pallas_call(
        _rglru_step_kernel,
        grid=(1,),
        in_specs=[
            pl.BlockSpec((nb, w), lambda i: (0, OFF_RGX // w)),
            pl.BlockSpec((nb, w), lambda i: (0, OFF_RGG // w)),
            pl.BlockSpec((nb, (CONV_W - 1) * w), c0),
            pl.BlockSpec((nb, w), c0),
            pl.BlockSpec((CONV_W, w), c0),
            pl.BlockSpec((1, w), c0),
            pl.BlockSpec((RG_BLOCKS, RG_BLOCK, 2 * RG_BLOCK), lambda i: (0, 0, 0)),
            pl.BlockSpec((1, w), c0),
            pl.BlockSpec((1, w), c0),
            pl.BlockSpec((1, w), c0),
            pl.BlockSpec((1, w), c0),
        ],
        out_specs=[
            pl.BlockSpec((nb, w), c0),
            pl.BlockSpec((nb, w), c0),
            pl.BlockSpec((nb, (CONV_W - 1) * w), c0),
        ],
        out_shape=[
            jax.ShapeDtypeStruct((nb, w), BF16),
            jax.ShapeDtypeStruct((nb, w), F32),
            jax.ShapeDtypeStruct((nb, (CONV_W - 1) * w), F32),
        ],
        scratch_shapes=[pltpu.VMEM((nb, w), F32), pltpu.VMEM((nb, w), F32)],
        compiler_params=_params(1, 32),
        name=name,
    )(z, z, conv.reshape(nb, (CONV_W - 1) * w), h0, cw, row(cb), wgate, row(ba), row(bx),
      row(lam), row(gout))


def _mlstm_seq_kernel(bi_ref, bf_ref, q_ref, k_ref, v_ref, o_ref, zg_ref, g_ref,
                      y_ref, c_ref, n_ref, m_ref, zgt_ref, cs_ref, ns_ref, *, seq, cs):
    h = pl.program_id(1)
    nc = seq // cs
    bi = bi_ref[h]
    bf = bf_ref[h]
    for c in range(nc):
        zgt_ref[c] = zg_ref[c * cs:(c + 1) * cs, :].T
    cs_ref[...] = jnp.zeros_like(cs_ref)
    ns_ref[...] = jnp.zeros_like(ns_ref)

    t_idx = lax.broadcasted_iota(jnp.int32, (cs, cs), 0)
    s_idx = lax.broadcasted_iota(jnp.int32, (cs, cs), 1)
    causal = s_idx <= t_idx
    lane = lax.broadcasted_iota(jnp.int32, (cs, V7X_LANES), 1)

    def chunk(c, m):
        r0 = pl.multiple_of(c * cs, cs)
        q = q_ref[pl.ds(r0, cs), :]
        k = k_ref[pl.ds(r0, cs), :] * (ML_DK ** -0.5)
        v = v_ref[pl.ds(r0, cs), :]
        zg = zg_ref[pl.ds(r0, cs), :]
        li_col = jnp.sum(jnp.where(lane == h, zg, 0.0), axis=1, keepdims=True) + bi
        lf_col = _log_sigmoid(
            jnp.sum(jnp.where(lane == h + ML_HEADS, zg, 0.0), axis=1, keepdims=True) + bf)
        li_row = zgt_ref[c, pl.ds(h, 1), :] + bi
        lf_row = _log_sigmoid(zgt_ref[c, pl.ds(h + ML_HEADS, 1), :] + bf)
        bcum_col = jnp.sum(jnp.where(causal, lf_row, 0.0), axis=1, keepdims=True)
        bcum_row = jnp.sum(jnp.where(t_idx <= s_idx, lf_col, 0.0), axis=0, keepdims=True)
        log_d = jnp.where(causal, bcum_col - bcum_row + li_row, NEG)
        inter = bcum_col + m
        m_t = jnp.maximum(inter, jnp.max(log_d, axis=1, keepdims=True))
        dmat = jnp.exp(log_d - m_t)
        sc = jnp.exp(inter - m_t)
        qb = q.astype(BF16)
        kb = k.astype(BF16)
        vb = v.astype(BF16)
        qk = lax.dot_general(qb, kb, (((1,), (1,)), ((), ())), preferred_element_type=F32) * dmat
        c_old = cs_ref[...]
        n_old = ns_ref[...]
        num = sc * jnp.dot(qb, c_old.astype(BF16), preferred_element_type=F32) + _mm(qk, vb)
        den = sc * jnp.sum(q * n_old, axis=1, keepdims=True) + jnp.sum(qk, axis=1, keepdims=True)
        den = jnp.maximum(jnp.abs(den), jnp.exp(-m_t))
        hh = num / den
        m_new = m_t[cs - 1:cs, :]
        b_last = bcum_col[cs - 1:cs, :]
        w_end = jnp.exp(b_last - bcum_col + li_col - m_new)
        dec = jnp.exp(b_last + m - m_new)
        wk = w_end * k
        cs_ref[...] = dec * c_old + lax.dot_general(
            wk.astype(BF16), vb, (((0,), (0,)), ((), ())), preferred_element_type=F32)
        ns_ref[...] = dec * n_old + jnp.sum(wk, axis=0, keepdims=True)
        y = _rms(hh, g_ref[...]) * jax.nn.sigmoid(o_ref[pl.ds(r0, cs), :])
        y_ref[pl.ds(r0, cs), :] = y.astype(y_ref.dtype)
        return m_new

    m_fin = lax.fori_loop(0, nc, chunk, jnp.zeros((1, 1), F32))
    c_ref[...] = cs_ref[...]
    n_ref[pl.ds(h, 1), :] = ns_ref[...]
    m_ref[pl.ds(h, 1), :] = jnp.broadcast_to(m_fin, (1, V7X_LANES))


def _mlstm_seq(z, zg, b_i, b_f, g_out, *, batch, seq, cs, name):
    assert seq % cs == 0 and zg.shape[1] == V7X_LANES
    dk, dv, nh = ML_DK, ML_DV, ML_HEADS
    smem = pl.BlockSpec(memory_space=pltpu.SMEM)
    return pl.pallas_call(
        functools.partial(_mlstm_seq_kernel, seq=seq, cs=cs),
        grid=(batch, nh),
        in_specs=[
            smem, smem,
            pl.BlockSpec((seq, dk), lambda b, h: (b, OFF_Q // dk + h)),
            pl.BlockSpec((seq, dk), lambda b, h: (b, OFF_K // dk + h)),
            pl.BlockSpec((seq, dv), lambda b, h: (b, OFF_V // dv + h)),
            pl.BlockSpec((seq, dv), lambda b, h: (b, OFF_O // dv + h)),
            pl.BlockSpec((seq, V7X_LANES), lambda b, h: (b, 0)),
            pl.BlockSpec((1, dv), lambda b, h: (0, h)),
        ],
        out_specs=[
            pl.BlockSpec((seq, dv), lambda b, h: (b, h)),
            pl.BlockSpec((None, None, dk, dv), lambda b, h: (b, h, 0, 0)),
            pl.BlockSpec((None, nh, dk), lambda b, h: (b, 0, 0)),
            pl.BlockSpec((None, nh, V7X_LANES), lambda b, h: (b, 0, 0)),
        ],
        out_shape=[
            jax.ShapeDtypeStruct((batch * seq, nh * dv), BF16),
            jax.ShapeDtypeStruct((batch, nh, dk, dv), F32),
            jax.ShapeDtypeStruct((batch, nh, dk), F32),
            jax.ShapeDtypeStruct((batch, nh, V7X_LANES), F32),
        ],
        scratch_shapes=[
            pltpu.VMEM((seq // cs, V7X_LANES, cs), F32),
            pltpu.VMEM((dk, dv), F32),
            pltpu.VMEM((1, dk), F32),
        ],
        compiler_params=_params(2, 40),
        name=name,
    )(b_i, b_f, z, z, z, z, zg, g_out.reshape(1, nh * dv))


def _mlstm_step_kernel(bi_ref, bf_ref, q_ref, k_ref, v_ref, o_ref, zg_ref, g_ref, c0_ref, n0_ref,
                       m0_ref, y_ref, c_ref, n_ref, m_ref, qc_ref, *, bs):
    dk, dv = ML_DK, ML_DV
    eye = (lax.broadcasted_iota(jnp.int32, (dk, dk), 0)
           == lax.broadcasted_iota(jnp.int32, (dk, dk), 1))
    zg = zg_ref[...]
    for h in range(ML_HEADS):
        li = zg[:, h:h + 1] + bi_ref[h]
        lf = _log_sigmoid(zg[:, ML_HEADS + h:ML_HEADS + h + 1] + bf_ref[h])
        m = m0_ref[:, h:h + 1]
        inter = lf + m
        m_t = jnp.maximum(inter, li)
        dgate = jnp.exp(li - m_t)
        sc = jnp.exp(inter - m_t)
        q = q_ref[:, h * dk:(h + 1) * dk]
        k = k_ref[:, h * dk:(h + 1) * dk] * (ML_DK ** -0.5)
        v = v_ref[:, h * dv:(h + 1) * dv]
        n_old = n0_ref[:, h, :]
        qk = jnp.sum(q * k, axis=1, keepdims=True) * dgate
        w_end = jnp.exp(li - m_t)
        dec = jnp.exp(inter - m_t)
        wk = w_end * k
        for j in range(bs):
            c_old = c0_ref[j, h]
            qc_ref[j:j + 1, :] = _mm(q[j:j + 1, :], c_old)
            wk_col = jnp.sum(
                jnp.where(eye, jnp.broadcast_to(wk[j:j + 1, :], (dk, dk)), 0.0),
                axis=1, keepdims=True)
            c_ref[j, h] = dec[j:j + 1, :] * c_old + wk_col * v[j:j + 1, :]
        num = sc * qc_ref[...] + qk * v
        den = sc * jnp.sum(q * n_old, axis=1, keepdims=True) + qk
        den = jnp.maximum(jnp.abs(den), jnp.exp(-m_t))
        hh = num / den
        n_ref[:, h, :] = dec * n_old + wk
        m_ref[:, h:h + 1] = m_t
        y = _rms(hh, g_ref[:, h * dv:(h + 1) * dv]) * jax.nn.sigmoid(o_ref[:, h * dv:(h + 1) * dv])
        y_ref[:, h * dv:(h + 1) * dv] = y.astype(y_ref.dtype)


def _mlstm_step(z, zg, c0, n0, m0, b_i, b_f, g_out, *, bs, name):
    nb = z.shape[0]
    assert nb % bs == 0
    dk, dv, nh = ML_DK, ML_DV, ML_HEADS
    smem = pl.BlockSpec(memory_space=pltpu.SMEM)
    return pl.pallas_call(
        functools.partial(_mlstm_step_kernel, bs=bs),
        grid=(nb // bs,),
        in_specs=[
            smem, smem,
            pl.BlockSpec((bs, nh * dk), lambda i: (i, OFF_Q // (nh * dk))),
            pl.BlockSpec((bs, nh * dk), lambda i: (i, OFF_K // (nh * dk))),
            pl.BlockSpec((bs, nh * dv), lambda i: (i, OFF_V // (nh * dv))),
            pl.BlockSpec((bs, nh * dv), lambda i: (i, OFF_O // (nh * dv))),
            pl.BlockSpec((bs, V7X_LANES), lambda i: (i, 0)),
            pl.BlockSpec((1, nh * dv), lambda i: (0, 0)),
            pl.BlockSpec((bs, nh, dk, dv), lambda i: (i, 0, 0, 0)),
            pl.BlockSpec((bs, nh, dk), lambda i: (i, 0, 0)),
            pl.BlockSpec((bs, nh), lambda i: (i, 0)),
        ],
        out_specs=[
            pl.BlockSpec((bs, nh * dv), lambda i: (i, 0)),
            pl.BlockSpec((bs, nh, dk, dv), lambda i: (i, 0, 0, 0)),
            pl.BlockSpec((bs, nh, dk), lambda i: (i, 0, 0)),
            pl.BlockSpec((bs, nh), lambda i: (i, 0)),
        ],
        out_shape=[
            jax.ShapeDtypeStruct((nb, nh * dv), F32),
            jax.ShapeDtypeStruct((nb, nh, dk, dv), F32),
            jax.ShapeDtypeStruct((nb, nh, dk), F32),
            jax.ShapeDtypeStruct((nb, nh), F32),
        ],
        scratch_shapes=[pltpu.VMEM((bs, dv), F32)],
        compiler_params=_params(1, 32),
        name=name,
    )(b_i, b_f, z, z, z, z, zg, g_out.reshape(1, nh * dv), c0, n0, m0)


def _softmax_rows(s):
    e = jnp.exp(s - jnp.max(s, axis=-1, keepdims=True))
    return e / jnp.sum(e, axis=-1, keepdims=True)


def _xattn_seq_kernel(q_ref, k_ref, v_ref, o_ref):
    for h in range(XA_HEADS):
        sl = slice(h * XA_DH, (h + 1) * XA_DH)
        s = lax.dot_general(q_ref[:, sl].astype(BF16), k_ref[:, sl].astype(BF16),
                            (((1,), (1,)), ((), ())), preferred_element_type=F32)
        p = _softmax_rows(s * (XA_DH ** -0.5))
        o_ref[:, sl] = _mm(p, v_ref[:, sl]).astype(o_ref.dtype)


def _xattn_seq(xq, mk, mv, *, batch, seq, tq, name):
    nt = seq // tq
    assert seq % tq == 0
    d = D_MODEL
    return pl.pallas_call(
        _xattn_seq_kernel,
        grid=(batch, nt),
        in_specs=[
            pl.BlockSpec((tq, d), lambda b, t: (b * nt + t, 0)),
            pl.BlockSpec((N_MEM, d), lambda b, t: (b, 0)),
            pl.BlockSpec((N_MEM, d), lambda b, t: (b, 0)),
        ],
        out_specs=pl.BlockSpec((tq, d), lambda b, t: (b * nt + t, 0)),
        out_shape=jax.ShapeDtypeStruct((batch * seq, d), BF16),
        compiler_params=_params(2, 40),
        name=name,
    )(xq, mk, mv)


def _xattn_step_kernel(q_ref, k_ref, v_ref, o_ref, *, sb):
    nrow = N_MEM * XA_HEADS
    col_head = lax.broadcasted_iota(jnp.int32, (V7X_SUBLANES, nrow), 1) & (XA_HEADS - 1)
    row_head = lax.broadcasted_iota(jnp.int32, (V7X_SUBLANES, nrow), 0) & (XA_HEADS - 1)
    own = col_head == row_head
    for j in range(sb):
        kf = k_ref[j].reshape(nrow, XA_DH)
        vf = v_ref[j].reshape(nrow, XA_DH)
        q8 = jnp.concatenate([q_ref[j]] * (V7X_SUBLANES // XA_HEADS), axis=0)
        s = lax.dot_general(q8.astype(BF16), kf.astype(BF16), (((1,), (1,)), ((), ())),
                            preferred_element_type=F32)
        s = jnp.where(own, s * (XA_DH ** -0.5), NEG)
        e = jnp.where(own, jnp.exp(s - jnp.max(s, axis=-1, keepdims=True)), 0.0)
        p = e / jnp.sum(e, axis=-1, keepdims=True)
        o_ref[j] = _mm(p, vf)[0:XA_HEADS, :]


def _xattn_step(xq, ck, cv, *, sb, name):
    nb = xq.shape[0]
    assert XA_HEADS & (XA_HEADS - 1) == 0
    assert nb % sb == 0 and ck.shape == (nb, N_MEM, XA_HEADS, XA_DH)
    cache_spec = pl.BlockSpec((sb, N_MEM, XA_HEADS, XA_DH), lambda i: (i, 0, 0, 0))
    q_spec = pl.BlockSpec((sb, XA_HEADS, XA_DH), lambda i: (i, 0, 0))
    return pl.pallas_call(
        functools.partial(_xattn_step_kernel, sb=sb),
        grid=(nb // sb,),
        in_specs=[q_spec, cache_spec, cache_spec],
        out_specs=q_spec,
        out_shape=jax.ShapeDtypeStruct((nb, XA_HEADS, XA_DH), F32),
        compiler_params=_params(1, 40),
        name=name,
    )(xq.reshape(nb, XA_HEADS, XA_DH), ck, cv).reshape(nb, XA_HEADS * XA_DH)


def _tiles(rows):
    tm = min(rows, 1024)
    assert rows % tm == 0
    return tm


def kernel(x_prompt, x_sample, mem_prompt, state_rg_h, state_rg_conv, state_ml_C, state_ml_n, state_ml_m, cache_mem_k, cache_mem_v, g_mix, w_in, conv_w, conv_b, w_rg_a, b_rg_a, w_rg_x, b_rg_x, rg_lambda, b_ml_i, b_ml_f, g_rg_out, g_ml_out, w_out, g_xa, g_mem, w_xa_q, w_xa_k, w_xa_v, w_xa_o, g_ffn, w_ffn_gate, w_ffn_up, w_ffn_down, g_final):
    depth = g_mix.shape[0]
    assert depth == 1, "single trunk layer"
    bp, seq, d = x_prompt.shape
    bs_, dec_seq, _ = x_sample.shape
    assert d == D_MODEL and dec_seq == 1
    n_mem = mem_prompt.shape[1]
    assert n_mem == N_MEM
    dff = w_ffn_gate.shape[-1]
    in_w = w_in.shape[-1]
    assert in_w == IN_MAIN + N_GATE

    w_in_t = jnp.swapaxes(w_in, 1, 2).reshape(in_w, d)
    w_gate_pad = jnp.pad(w_in_t[IN_MAIN:], ((0, V7X_LANES - N_GATE), (0, 0)))
    cw = conv_w.reshape(CONV_W, RG_WIDTH)
    wgate = jnp.concatenate([w_rg_a.reshape(RG_BLOCKS, RG_BLOCK, RG_BLOCK),
                             w_rg_x.reshape(RG_BLOCKS, RG_BLOCK, RG_BLOCK)], axis=-1)
    rg_args = (cw, conv_b.reshape(-1), wgate, b_rg_a.reshape(-1), b_rg_x.reshape(-1),
               rg_lambda.reshape(-1), g_rg_out.reshape(-1))
    b_i = b_ml_i.reshape(ML_HEADS)
    b_f = b_ml_f.reshape(ML_HEADS)
    g_ml = g_ml_out.reshape(-1)
    w_out2 = w_out.reshape(d, d)
    w_q = w_xa_q.reshape(d, d)
    w_k = w_xa_k.reshape(d, d)
    w_v = w_xa_v.reshape(d, d)
    w_o = w_xa_o.reshape(d, d)
    w_fg = w_ffn_gate.reshape(d, dff)
    w_fu = w_ffn_up.reshape(d, dff)
    w_fd = w_ffn_down.reshape(dff, d)

    def trunk_in(x2, tag):
        tm = _tiles(x2.shape[0])
        return _norm_linear(x2, g_mix.reshape(-1), w_in_t, n_out=IN_MAIN, tm=tm, tn=1024,
                            w_gate=w_gate_pad, w_is_nk=True, vmem_mib=56, name=f"in_proj_{tag}")

    def trunk_out(x2, y_rg, y_ml, q_fn, tag):
        tm = _tiles(x2.shape[0])
        x1 = _linear_residual([y_rg, y_ml], w_out2, x2, tm=tm, tn=1024, name=f"mix_out_{tag}")
        xq = _norm_linear(x1, g_xa.reshape(-1), w_q, n_out=d, tm=tm, tn=1024,
                          out_dtype=q_fn[1], name=f"xa_q_{tag}")
        o = q_fn[0](xq)
        x3 = _linear_residual([o], w_o, x1, tm=tm, tn=1024, name=f"xa_out_{tag}")
        return _ffn(x3, g_ffn.reshape(-1), w_fg, w_fu, w_fd, g_final, tm=tm, tf=256,
                    vmem_mib=60, name=f"ffn_{tag}")

    tp = bp * seq
    xp = x_prompt.reshape(tp, d)
    z_p, zg_p = trunk_in(xp, "p")
    y_rg_p, p_h, p_conv = _rglru_seq(
        z_p, jnp.zeros((bp, CONV_W - 1, RG_WIDTH), F32), jnp.zeros((bp, RG_WIDTH), F32), *rg_args,
        batch=bp, seq=seq, tl=min(seq, 256), name="rglru_p")
    y_ml_p, p_c, p_n, p_m = _mlstm_seq(z_p, zg_p, b_i, b_f, g_ml, batch=bp, seq=seq,
                                       cs=min(seq, ML_CHUNK), name="mlstm_p")
    mem2 = mem_prompt.reshape(bp * n_mem, d)
    tmem = _tiles(bp * n_mem)
    mk = _norm_linear(mem2, g_mem.reshape(-1), w_k, n_out=d, tm=tmem, tn=1024, name="mem_k")
    mv = _norm_linear(mem2, g_mem.reshape(-1), w_v, n_out=d, tm=tmem, tn=1024, name="mem_v")
    attn_p = lambda xq: _xattn_seq(xq, mk, mv, batch=bp, seq=seq, tq=min(seq, 512), name="xattn_p")
    y_p = trunk_out(xp, y_rg_p, y_ml_p, (attn_p, BF16), "p")

    xs = x_sample.reshape(bs_, d)
    z_s, zg_s = trunk_in(xs, "s")
    y_rg_s, s_h, s_conv = _rglru_step(
        z_s, state_rg_conv.reshape(bs_, CONV_W - 1, RG_WIDTH), state_rg_h.reshape(bs_, RG_WIDTH),
        *rg_args, name="rglru_s")
    y_ml_s, s_c, s_n, s_m = _mlstm_step(
        z_s, zg_s, state_ml_C.reshape(bs_, ML_HEADS, ML_DK, ML_DV),
        state_ml_n.reshape(bs_, ML_HEADS, ML_DK), state_ml_m.reshape(bs_, ML_HEADS),
        b_i, b_f, g_ml, bs=V7X_SUBLANES, name="mlstm_s")
    ck = cache_mem_k.reshape(bs_, n_mem, XA_HEADS, XA_DH)
    cv = cache_mem_v.reshape(bs_, n_mem, XA_HEADS, XA_DH)
    attn_s = lambda xq: _xattn_step(xq, ck, cv, sb=2, name="xattn_s")
    y_s = trunk_out(xs, y_rg_s, y_ml_s, (attn_s, F32), "s")

    return (
        y_p.reshape(bp, seq, d),
        y_s.reshape(bs_, 1, d),
        p_h.reshape(1, bp, RG_WIDTH),
        p_conv.reshape(1, bp, CONV_W - 1, RG_WIDTH),
        p_c.reshape(1, bp, ML_HEADS, ML_DK, ML_DV),
        p_n.reshape(1, bp, ML_HEADS, ML_DK),
        p_m[:, :, 0].reshape(1, bp, ML_HEADS),
        mk.reshape(1, bp, n_mem, XA_HEADS, XA_DH),
        mv.reshape(1, bp, n_mem, XA_HEADS, XA_DH),
        s_h.reshape(1, bs_, RG_WIDTH),
        s_conv.reshape(1, bs_, CONV_W - 1, RG_WIDTH),
        s_c.reshape(1, bs_, ML_HEADS, ML_DK, ML_DV),
        s_n.reshape(1, bs_, ML_HEADS, ML_DK),
        s_m.reshape(1, bs_, ML_HEADS),
    )
```

```python
import functools

import jax
import jax.numpy as jnp
from jax import lax
from jax.experimental import pallas as pl
from jax.experimental.pallas import tpu as pltpu

F32 = jnp.float32
BF16 = jnp.bfloat16

D_MODEL = 2048
RG_WIDTH = D_MODEL // 2
RG_BLOCKS = 8
RG_BLOCK = RG_WIDTH // RG_BLOCKS
CONV_W = 4
RG_C = 8.0
ML_HEADS = 4
ML_WIDTH = D_MODEL - RG_WIDTH
ML_DV = ML_WIDTH // ML_HEADS
ML_DK = ML_DV // 2
N_MEM = 256
XA_HEADS = 4
XA_DH = D_MODEL // XA_HEADS
EPS = 1e-6
NEG = -1e30

OFF_RGX = 0
OFF_RGG = OFF_RGX + RG_WIDTH
OFF_Q = OFF_RGG + RG_WIDTH
OFF_K = OFF_Q + ML_HEADS * ML_DK
OFF_V = OFF_K + ML_HEADS * ML_DK
OFF_O = OFF_V + ML_WIDTH
OFF_I = OFF_O + ML_WIDTH
IN_MAIN = OFF_I
N_GATE = 2 * ML_HEADS

V7X_LANES = 128
V7X_SUBLANES = 8
V7X_VMEM_BYTES = 64 * 2**20

ML_CHUNK = 256


def _params(n_axes, vmem_mib):
    assert vmem_mib * 2**20 <= V7X_VMEM_BYTES
    return pltpu.CompilerParams(
        dimension_semantics=("arbitrary",) * n_axes,
        vmem_limit_bytes=vmem_mib * 2**20,
    )


def _rms(x, g):
    ms = jnp.mean(x * x, axis=-1, keepdims=True)
    return x * lax.rsqrt(ms + EPS) * g


def _softplus(u):
    return jnp.maximum(u, 0.0) + jnp.log1p(jnp.exp(-jnp.abs(u)))


def _log_sigmoid(u):
    return -_softplus(-u)


def _gelu_tanh(x):
    return x * (0.5 * (1.0 + jnp.tanh(0.7978845608028654 * (x + 0.044715 * (x * x * x)))))


def _mm(a, b):
    return jnp.dot(a.astype(BF16), b.astype(BF16), preferred_element_type=F32)


def _dot_w(a, w_ref, w_is_nk):
    w = w_ref[...].astype(BF16)
    if w_is_nk:
        return lax.dot_general(a, w, (((1,), (1,)), ((), ())), preferred_element_type=F32)
    return jnp.dot(a, w, preferred_element_type=F32)


def _norm_linear_kernel(*refs, with_gate, w_is_nk, emit_w16):
    refs = list(refs)
    x_ref, g_ref, w_ref = refs[:3]
    wg_ref = refs[3] if with_gate else None
    outs = refs[3 + with_gate:-1]
    xn_ref = refs[-1]
    o_ref = outs[0]
    og_ref = outs[1] if with_gate else None

    @pl.when(pl.program_id(1) == 0)
    def _():
        xn = _rms(x_ref[...], g_ref[...]).astype(BF16)
        xn_ref[...] = xn
        if with_gate:
            og_ref[...] = _dot_w(xn, wg_ref, w_is_nk)

    o_ref[...] = _dot_w(xn_ref[...], w_ref, w_is_nk).astype(o_ref.dtype)
    if emit_w16:
        outs[-1][...] = w_ref[...].astype(BF16)


def _norm_linear(x, g, w, *, n_out, tm, tn, out_dtype=F32, w_gate=None, w_is_nk=False,
                 emit_w16=False, vmem_mib=48, name):
    m, k = x.shape
    k_ax, n_ax = (1, 0) if w_is_nk else (0, 1)
    assert m % tm == 0 and n_out % tn == 0 and w.shape[k_ax] == k and n_out <= w.shape[n_ax]
    assert not emit_w16 or m == tm
    with_gate = w_gate is not None
    w_spec = (pl.BlockSpec((tn, k), lambda i, j: (j, 0)) if w_is_nk
              else pl.BlockSpec((k, tn), lambda i, j: (0, j)))
    in_specs = [
        pl.BlockSpec((tm, k), lambda i, j: (i, 0)),
        pl.BlockSpec((1, k), lambda i, j: (0, 0)),
        w_spec,
    ]
    out_shape = [jax.ShapeDtypeStruct((m, n_out), out_dtype)]
    out_specs = [pl.BlockSpec((tm, tn), lambda i, j: (i, j))]
    args = [x, g.reshape(1, k), w]
    if with_gate:
        ng = w_gate.shape[n_ax]
        in_specs.append(pl.BlockSpec(w_gate.shape, lambda i, j: (0, 0)))
        out_shape.append(jax.ShapeDtypeStruct((m, ng), F32))
        out_specs.append(pl.BlockSpec((tm, ng), lambda i, j: (i, 0)))
        args.append(w_gate)
    if emit_w16:
        out_shape.append(jax.ShapeDtypeStruct((n_out, k) if w_is_nk else (k, n_out), BF16))
        out_specs.append(w_spec)
    out = pl.pallas_call(
        functools.partial(_norm_linear_kernel, with_gate=with_gate, w_is_nk=w_is_nk,
                          emit_w16=emit_w16),
        grid=(m // tm, n_out // tn),
        in_specs=in_specs,
        out_specs=out_specs,
        out_shape=out_shape,
        scratch_shapes=[pltpu.VMEM((tm, k), BF16)],
        compiler_params=_params(2, vmem_mib),
        name=name,
    )(*args)
    return out if len(out) > 1 else out[0]


def _linear_res_kernel(*refs, n_in, emit_w16):
    a_refs = refs[:n_in]
    w_refs = refs[n_in:2 * n_in]
    res_ref = refs[2 * n_in]
    o_ref = refs[2 * n_in + 1]
    acc = res_ref[...]
    for a_ref, w_ref in zip(a_refs, w_refs):
        acc = acc + _mm(a_ref[...], w_ref[...])
    o_ref[...] = acc
    if emit_w16:
        for w_ref, wc_ref in zip(w_refs, refs[2 * n_in + 2:]):
            wc_ref[...] = w_ref[...].astype(BF16)


def _linear_residual(parts, weights, res, *, tm, tn, emit_w16=False, vmem_mib=48, name):
    m, n = res.shape
    kp = parts[0].shape[1]
    assert all(p.shape == (m, kp) for p in parts) and len(weights) == len(parts)
    assert m % tm == 0 and n % tn == 0 and (not emit_w16 or m == tm)
    n_in = len(parts)
    in_specs = [pl.BlockSpec((tm, kp), lambda i, j: (i, 0)) for _ in parts]
    in_specs += [pl.BlockSpec((kp, tn), lambda i, j, rb=rb: (rb, j)) for _, rb in weights]
    in_specs.append(pl.BlockSpec((tm, tn), lambda i, j: (i, j)))
    out_specs = [pl.BlockSpec((tm, tn), lambda i, j: (i, j))]
    out_shape = [jax.ShapeDtypeStruct((m, n), F32)]
    if emit_w16:
        out_specs += [pl.BlockSpec((kp, tn), lambda i, j: (0, j)) for _ in parts]
        out_shape += [jax.ShapeDtypeStruct((kp, n), BF16) for _ in parts]
    out = pl.pallas_call(
        functools.partial(_linear_res_kernel, n_in=n_in, emit_w16=emit_w16),
        grid=(m // tm, n // tn),
        in_specs=in_specs,
        out_specs=out_specs,
        out_shape=out_shape,
        compiler_params=_params(2, vmem_mib),
        name=name,
    )(*parts, *[w for w, _ in weights], res)
    return out if emit_w16 else out[0]


def _ffn_kernel(x_ref, g_ref, wg_ref, wu_ref, wd_ref, gf_ref, o_ref, *rest, emit_w16):
    xf_ref = rest[-1]
    f = pl.program_id(1)

    @pl.when(f == 0)
    def _():
        x = x_ref[...]
        xf_ref[...] = _rms(x, g_ref[...]).astype(BF16)
        o_ref[...] = x

    xf = xf_ref[...]
    wg = wg_ref[...].astype(BF16)
    wu = wu_ref[...].astype(BF16)
    wd = wd_ref[...].astype(BF16)
    gate = jnp.dot(xf, wg, preferred_element_type=F32)
    up = jnp.dot(xf, wu, preferred_element_type=F32)
    hidden = (gate * jax.nn.sigmoid(gate)) * up
    o_ref[...] += jnp.dot(hidden.astype(BF16), wd, preferred_element_type=F32)
    if emit_w16:
        for dst, val in zip(rest[:3], (wg, wu, wd)):
            dst[...] = val

    @pl.when(f == pl.num_programs(1) - 1)
    def _():
        o_ref[...] = _rms(o_ref[...], gf_ref[...])


def _ffn(x, g, w_gate, w_up, w_down, g_final, *, tm, tf, emit_w16=False, vmem_mib, name):
    m, d = x.shape
    dff = w_gate.shape[1]
    assert m % tm == 0 and dff % tf == 0 and (not emit_w16 or m == tm)
    up_spec = pl.BlockSpec((d, tf), lambda i, f: (0, f))
    down_spec = pl.BlockSpec((tf, d), lambda i, f: (f, 0))
    out_specs = [pl.BlockSpec((tm, d), lambda i, f: (i, 0))]
    out_shape = [jax.ShapeDtypeStruct((m, d), F32)]
    if emit_w16:
        out_specs += [up_spec, up_spec, down_spec]
        out_shape += [jax.ShapeDtypeStruct(w.shape, BF16) for w in (w_gate, w_up, w_down)]
    out = pl.pallas_call(
        functools.partial(_ffn_kernel, emit_w16=emit_w16),
        grid=(m // tm, dff // tf),
        in_specs=[
            pl.BlockSpec((tm, d), lambda i, f: (i, 0)),
            pl.BlockSpec((1, d), lambda i, f: (0, 0)),
            up_spec,
            up_spec,
            down_spec,
            pl.BlockSpec((1, d), lambda i, f: (0, 0)),
        ],
        out_specs=out_specs,
        out_shape=out_shape,
        scratch_shapes=[pltpu.VMEM((tm, d), BF16)],
        compiler_params=_params(2, vmem_mib),
        name=name,
    )(x, g.reshape(1, d), w_gate, w_up, w_down, g_final.reshape(1, d))
    return out if emit_w16 else out[0]


def _rg_gates(xr, wgate_ref, ba_ref, bx_ref, lam_ref, a_ref, b_ref):
    for n in range(RG_BLOCKS):
        sl = slice(n * RG_BLOCK, (n + 1) * RG_BLOCK)
        xn = xr[:, sl]
        g = _mm(xn, wgate_ref[n])
        r = jax.nn.sigmoid(g[:, :RG_BLOCK] + ba_ref[:, sl])
        ig = jax.nn.sigmoid(g[:, RG_BLOCK:] + bx_ref[:, sl])
        log_a = -RG_C * r * _softplus(-lam_ref[:, sl])
        a_ref[:, sl] = jnp.exp(log_a)
        mult = jnp.sqrt(jnp.maximum(1.0 - jnp.exp(2.0 * log_a), 0.0))
        b_ref[:, sl] = mult * (ig * xn)


def _rglru_seq_kernel(zx_ref, zg_ref, conv0_ref, h0_ref, cw_ref, cb_ref, wgate_ref, ba_ref,
                      bx_ref, lam_ref, gout_ref, y_ref, hlast_ref, convn_ref,
                      xe_ref, a_ref, b_ref, h_ref, hc_ref, *, tl):
    t = pl.program_id(1)
    pad = V7X_SUBLANES

    @pl.when(t == 0)
    def _():
        xe_ref[pad - 3:pad, :] = conv0_ref[...]
        hc_ref[...] = h0_ref[...]

    @pl.when(t > 0)
    def _():
        xe_ref[pad - 3:pad, :] = xe_ref[tl + pad - 3:tl + pad, :]

    x = zx_ref[...]
    xe_ref[pad:tl + pad, :] = x
    xr = (xe_ref[pad - 3:tl + pad - 3, :] * cw_ref[0:1, :]
          + xe_ref[pad - 2:tl + pad - 2, :] * cw_ref[1:2, :]
          + xe_ref[pad - 1:tl + pad - 1, :] * cw_ref[2:3, :]
          + x * cw_ref[3:4, :]) + cb_ref[...]
    _rg_gates(xr, wgate_ref, ba_ref, bx_ref, lam_ref, a_ref, b_ref)

    row = lax.broadcasted_iota(jnp.int32, (V7X_SUBLANES, RG_WIDTH), 0)

    def group(gi, hc):
        r0 = pl.multiple_of(gi * V7X_SUBLANES, V7X_SUBLANES)
        a8 = a_ref[pl.ds(r0, V7X_SUBLANES), :]
        b8 = b_ref[pl.ds(r0, V7X_SUBLANES), :]
        for d in (1, 2, 4):
            keep = row >= d
            b8 = jnp.where(keep, a8 * pltpu.roll(b8, d, axis=0) + b8, b8)
            a8 = jnp.where(keep, a8 * pltpu.roll(a8, d, axis=0), a8)
        h8 = a8 * hc + b8
        h_ref[pl.ds(r0, V7X_SUBLANES), :] = h8
        return h8[V7X_SUBLANES - 1:V7X_SUBLANES, :]

    hc = lax.fori_loop(0, tl // V7X_SUBLANES, group, hc_ref[...], unroll=4)
    hc_ref[...] = hc
    hlast_ref[...] = hc
    convn_ref[...] = xe_ref[tl + pad - 3:tl + pad, :]
    y = h_ref[...] * _gelu_tanh(zg_ref[...])
    y_ref[...] = _rms(y, gout_ref[...]).astype(y_ref.dtype)


def _rglru_seq(z, conv0, h0, cw, cb, wgate, ba, bx, lam, gout, *, batch, seq, tl, name):
    nt = seq // tl
    assert seq % tl == 0
    w = RG_WIDTH
    row = lambda v: v.reshape(1, w)
    const2 = lambda b, t: (0, 0)
    return pl.pallas_call(
        functools.partial(_rglru_seq_kernel, tl=tl),
        grid=(batch, nt),
        in_specs=[
            pl.BlockSpec((tl, w), lambda b, t: (b * nt + t, OFF_RGX // w)),
            pl.BlockSpec((tl, w), lambda b, t: (b * nt + t, OFF_RGG // w)),
            pl.BlockSpec((None, CONV_W - 1, w), lambda b, t: (b, 0, 0)),
            pl.BlockSpec((None, 1, w), lambda b, t: (b, 0, 0)),
            pl.BlockSpec((CONV_W, w), const2),
            pl.BlockSpec((1, w), const2),
            pl.BlockSpec((RG_BLOCKS, RG_BLOCK, 2 * RG_BLOCK), lambda b, t: (0, 0, 0)),
            pl.BlockSpec((1, w), const2),
            pl.BlockSpec((1, w), const2),
            pl.BlockSpec((1, w), const2),
            pl.BlockSpec((1, w), const2),
        ],
        out_specs=[
            pl.BlockSpec((tl, w), lambda b, t: (b * nt + t, 0)),
            pl.BlockSpec((None, 1, w), lambda b, t: (b, 0, 0)),
            pl.BlockSpec((None, CONV_W - 1, w), lambda b, t: (b, 0, 0)),
        ],
        out_shape=[
            jax.ShapeDtypeStruct((batch * seq, w), BF16),
            jax.ShapeDtypeStruct((batch, 1, w), F32),
            jax.ShapeDtypeStruct((batch, CONV_W - 1, w), F32),
        ],
        scratch_shapes=[
            pltpu.VMEM((tl + V7X_SUBLANES, w), F32),
            pltpu.VMEM((tl, w), F32),
            pltpu.VMEM((tl, w), F32),
            pltpu.VMEM((tl, w), F32),
            pltpu.VMEM((1, w), F32),
        ],
        compiler_params=_params(2, 32),
        name=name,
    )(z, z, conv0, h0.reshape(batch, 1, w), cw, row(cb), wgate, row(ba), row(bx), row(lam), row(gout))


def _rglru_step_kernel(zx_ref, zg_ref, conv_ref, h0_ref, cw_ref, cb_ref, wgate_ref, ba_ref,
                       bx_ref, lam_ref, gout_ref, y_ref, hn_ref, convn_ref, a_ref, b_ref):
    w = RG_WIDTH
    x = zx_ref[...]
    xr = (conv_ref[:, 0:w] * cw_ref[0:1, :] + conv_ref[:, w:2 * w] * cw_ref[1:2, :]
          + conv_ref[:, 2 * w:3 * w] * cw_ref[2:3, :] + x * cw_ref[3:4, :]) + cb_ref[...]
    _rg_gates(xr, wgate_ref, ba_ref, bx_ref, lam_ref, a_ref, b_ref)
    h = a_ref[...] * h0_ref[...] + b_ref[...]
    hn_ref[...] = h
    convn_ref[:, 0:2 * w] = conv_ref[:, w:3 * w]
    convn_ref[:, 2 * w:3 * w] = x
    y_ref[...] = _rms(h * _gelu_tanh(zg_ref[...]), gout_ref[...]).astype(y_ref.dtype)


def _rglru_step(z, conv, h0, cw, cb, wgate, ba, bx, lam, gout, *, name):
    nb = z.shape[0]
    w = RG_WIDTH
    row = lambda v: v.reshape(1, w)
    c0 = lambda i: (0, 0)
    return pl.pallas_call(
        _rglru_step_kernel,
        grid=(1,),
        in_specs=[
            pl.BlockSpec((nb, w), lambda i: (0, OFF_RGX // w)),
            pl.BlockSpec((nb, w), lambda i: (0, OFF_RGG // w)),
            pl.BlockSpec((nb, (CONV_W - 1) * w), c0),
            pl.BlockSpec((nb, w), c0),
            pl.BlockSpec((CONV_W, w), c0),
            pl.BlockSpec((1, w), c0),
            pl.BlockSpec((RG_BLOCKS, RG_BLOCK, 2 * RG_BLOCK), lambda i: (0, 0, 0)),
            pl.BlockSpec((1, w), c0),
            pl.BlockSpec((1, w), c0),
            pl.BlockSpec((1, w), c0),
            pl.BlockSpec((1, w), c0),
        ],
        out_specs=[
            pl.BlockSpec((nb, w), c0),
            pl.BlockSpec((nb, w), c0),
            pl.BlockSpec((nb, (CONV_W - 1) * w), c0),
        ],
        out_shape=[
            jax.ShapeDtypeStruct((nb, w), BF16),
            jax.ShapeDtypeStruct((nb, w), F32),
            jax.ShapeDtypeStruct((nb, (CONV_W - 1) * w), F32),
        ],
        scratch_shapes=[pltpu.VMEM((nb, w), F32), pltpu.VMEM((nb, w), F32)],
        compiler_params=_params(1, 32),
        name=name,
    )(z, z, conv.reshape(nb, (CONV_W - 1) * w), h0, cw, row(cb), wgate, row(ba), row(bx),
      row(lam), row(gout))


def _mlstm_seq_kernel(bi_ref, bf_ref, q_ref, k_ref, v_ref, o_ref, zg_ref, g_ref,
                      y_ref, c_ref, n_ref, m_ref, zgt_ref, cs_ref, ns_ref, *, seq, cs):
    h = pl.program_id(1)
    nc = seq // cs
    bi = bi_ref[h]
    bf = bf_ref[h]
    for c in range(nc):
        zgt_ref[c] = zg_ref[c * cs:(c + 1) * cs, :].T
    cs_ref[...] = jnp.zeros_like(cs_ref)
    ns_ref[...] = jnp.zeros_like(ns_ref)

    t_idx = lax.broadcasted_iota(jnp.int32, (cs, cs), 0)
    s_idx = lax.broadcasted_iota(jnp.int32, (cs, cs), 1)
    causal = s_idx <= t_idx
    lane = lax.broadcasted_iota(jnp.int32, (cs, V7X_LANES), 1)

    def chunk(c, m):
        r0 = pl.multiple_of(c * cs, cs)
        q = q_ref[pl.ds(r0, cs), :]
        k = k_ref[pl.ds(r0, cs), :] * (ML_DK ** -0.5)
        v = v_ref[pl.ds(r0, cs), :]
        zg = zg_ref[pl.ds(r0, cs), :]
        li_col = jnp.sum(jnp.where(lane == h, zg, 0.0), axis=1, keepdims=True) + bi
        lf_col = _log_sigmoid(
            jnp.sum(jnp.where(lane == h + ML_HEADS, zg, 0.0), axis=1, keepdims=True) + bf)
        li_row = zgt_ref[c, pl.ds(h, 1), :] + bi
        lf_row = _log_sigmoid(zgt_ref[c, pl.ds(h + ML_HEADS, 1), :] + bf)
        bcum_col = jnp.sum(jnp.where(causal, lf_row, 0.0), axis=1, keepdims=True)
        bcum_row = jnp.sum(jnp.where(t_idx <= s_idx, lf_col, 0.0), axis=0, keepdims=True)
        log_d = jnp.where(causal, bcum_col - bcum_row + li_row, NEG)
        inter = bcum_col + m
        m_t = jnp.maximum(inter, jnp.max(log_d, axis=1, keepdims=True))
        dmat = jnp.exp(log_d - m_t)
        sc = jnp.exp(inter - m_t)
        qb = q.astype(BF16)
        kb = k.astype(BF16)
        vb = v.astype(BF16)
        qk = lax.dot_general(qb, kb, (((1,), (1,)), ((), ())), preferred_element_type=F32) * dmat
        c_old = cs_ref[...]
        n_old = ns_ref[...]
        num = sc * jnp.dot(qb, c_old.astype(BF16), preferred_element_type=F32) + _mm(qk, vb)
        den = sc * jnp.sum(q * n_old, axis=1, keepdims=True) + jnp.sum(qk, axis=1, keepdims=True)
        den = jnp.maximum(jnp.abs(den), jnp.exp(-m_t))
        hh = num / den
        m_new = m_t[cs - 1:cs, :]
        b_last = bcum_col[cs - 1:cs, :]
        w_end = jnp.exp(b_last - bcum_col + li_col - m_new)
        dec = jnp.exp(b_last + m - m_new)
        wk = w_end * k
        cs_ref[...] = dec * c_old + lax.dot_general(
            wk.astype(BF16), vb, (((0,), (0,)), ((), ())), preferred_element_type=F32)
        ns_ref[...] = dec * n_old + jnp.sum(wk, axis=0, keepdims=True)
        y = _rms(hh, g_ref[...]) * jax.nn.sigmoid(o_ref[pl.ds(r0, cs), :])
        y_ref[pl.ds(r0, cs), :] = y.astype(y_ref.dtype)
        return m_new

    m_fin = lax.fori_loop(0, nc, chunk, jnp.zeros((1, 1), F32))
    c_ref[...] = cs_ref[...]
    n_ref[pl.ds(h, 1), :] = ns_ref[...]
    m_ref[pl.ds(h, 1), :] = jnp.broadcast_to(m_fin, (1, V7X_LANES))


def _mlstm_seq(z, zg, b_i, b_f, g_out, *, batch, seq, cs, name):
    assert seq % cs == 0 and zg.shape[1] == V7X_LANES
    dk, dv, nh = ML_DK, ML_DV, ML_HEADS
    smem = pl.BlockSpec(memory_space=pltpu.SMEM)
    return pl.pallas_call(
        functools.partial(_mlstm_seq_kernel, seq=seq, cs=cs),
        grid=(batch, nh),
        in_specs=[
            smem, smem,
            pl.BlockSpec((seq, dk), lambda b, h: (b, OFF_Q // dk + h)),
            pl.BlockSpec((seq, dk), lambda b, h: (b, OFF_K // dk + h)),
            pl.BlockSpec((seq, dv), lambda b, h: (b, OFF_V // dv + h)),
            pl.BlockSpec((seq, dv), lambda b, h: (b, OFF_O // dv + h)),
            pl.BlockSpec((seq, V7X_LANES), lambda b, h: (b, 0)),
            pl.BlockSpec((1, dv), lambda b, h: (0, h)),
        ],
        out_specs=[
            pl.BlockSpec((seq, dv), lambda b, h: (b, h)),
            pl.BlockSpec((None, None, dk, dv), lambda b, h: (b, h, 0, 0)),
            pl.BlockSpec((None, nh, dk), lambda b, h: (b, 0, 0)),
            pl.BlockSpec((None, nh, V7X_LANES), lambda b, h: (b, 0, 0)),
        ],
        out_shape=[
            jax.ShapeDtypeStruct((batch * seq, nh * dv), BF16),
            jax.ShapeDtypeStruct((batch, nh, dk, dv), F32),
            jax.ShapeDtypeStruct((batch, nh, dk), F32),
            jax.ShapeDtypeStruct((batch, nh, V7X_LANES), F32),
        ],
        scratch_shapes=[
            pltpu.VMEM((seq // cs, V7X_LANES, cs), F32),
            pltpu.VMEM((dk, dv), F32),
            pltpu.VMEM((1, dk), F32),
        ],
        compiler_params=_params(2, 40),
        name=name,
    )(b_i, b_f, z, z, z, z, zg, g_out.reshape(1, nh * dv))


def _mlstm_step_kernel(bi_ref, bf_ref, q_ref, k_ref, v_ref, o_ref, zg_ref, g_ref, c0_ref, n0_ref,
                       m0_ref, y_ref, c_ref, n_ref, m_ref, qc_ref, *, bs):
    dk, dv = ML_DK, ML_DV
    eye = (lax.broadcasted_iota(jnp.int32, (dk, dk), 0)
           == lax.broadcasted_iota(jnp.int32, (dk, dk), 1))
    zg = zg_ref[...]
    for h in range(ML_HEADS):
        li = zg[:, h:h + 1] + bi_ref[h]
        lf = _log_sigmoid(zg[:, ML_HEADS + h:ML_HEADS + h + 1] + bf_ref[h])
        m = m0_ref[:, h:h + 1]
        inter = lf + m
        m_t = jnp.maximum(inter, li)
        dgate = jnp.exp(li - m_t)
        sc = jnp.exp(inter - m_t)
        q = q_ref[:, h * dk:(h + 1) * dk]
        k = k_ref[:, h * dk:(h + 1) * dk] * (ML_DK ** -0.5)
        v = v_ref[:, h * dv:(h + 1) * dv]
        n_old = n0_ref[:, h, :]
        qk = jnp.sum(q * k, axis=1, keepdims=True) * dgate
        w_end = jnp.exp(li - m_t)
        dec = jnp.exp(inter - m_t)
        wk = w_end * k
        for j in range(bs):
            c_old = c0_ref[j, h]
            qc_ref[j:j + 1, :] = _mm(q[j:j + 1, :], c_old)
            wk_col = jnp.sum(
                jnp.where(eye, jnp.broadcast_to(wk[j:j + 1, :], (dk, dk)), 0.0),
                axis=1, keepdims=True)
            c_ref[j, h] = dec[j:j + 1, :] * c_old + wk_col * v[j:j + 1, :]
        num = sc * qc_ref[...] + qk * v
        den = sc * jnp.sum(q * n_old, axis=1, keepdims=True) + qk
        den = jnp.maximum(jnp.abs(den), jnp.exp(-m_t))
        hh = num / den
        n_ref[:, h, :] = dec * n_old + wk
        m_ref[:, h:h + 1] = m_t
        y = _rms(hh, g_ref[:, h * dv:(h + 1) * dv]) * jax.nn.sigmoid(o_ref[:, h * dv:(h + 1) * dv])
        y_ref[:, h * dv:(h + 1) * dv] = y.astype(y_ref.dtype)


def _mlstm_step(z, zg, c0, n0, m0, b_i, b_f, g_out, *, bs, name):
    nb = z.shape[0]
    assert nb % bs == 0
    dk, dv, nh = ML_DK, ML_DV, ML_HEADS
    smem = pl.BlockSpec(memory_space=pltpu.SMEM)
    return pl.pallas_call(
        functools.partial(_mlstm_step_kernel, bs=bs),
        grid=(nb // bs,),
        in_specs=[
            smem, smem,
            pl.BlockSpec((bs, nh * dk), lambda i: (i, OFF_Q // (nh * dk))),
            pl.BlockSpec((bs, nh * dk), lambda i: (i, OFF_K // (nh * dk))),
            pl.BlockSpec((bs, nh * dv), lambda i: (i, OFF_V // (nh * dv))),
            pl.BlockSpec((bs, nh * dv), lambda i: (i, OFF_O // (nh * dv))),
            pl.BlockSpec((bs, V7X_LANES), lambda i: (i, 0)),
            pl.BlockSpec((1, nh * dv), lambda i: (0, 0)),
            pl.BlockSpec((bs, nh, dk, dv), lambda i: (i, 0, 0, 0)),
            pl.BlockSpec((bs, nh, dk), lambda i: (i, 0, 0)),
            pl.BlockSpec((bs, nh), lambda i: (i, 0)),
        ],
        out_specs=[
            pl.BlockSpec((bs, nh * dv), lambda i: (i, 0)),
            pl.BlockSpec((bs, nh, dk, dv), lambda i: (i, 0, 0, 0)),
            pl.BlockSpec((bs, nh, dk), lambda i: (i, 0, 0)),
            pl.BlockSpec((bs, nh), lambda i: (i, 0)),
        ],
        out_shape=[
            jax.ShapeDtypeStruct((nb, nh * dv), F32),
            jax.ShapeDtypeStruct((nb, nh, dk, dv), F32),
            jax.ShapeDtypeStruct((nb, nh, dk), F32),
            jax.ShapeDtypeStruct((nb, nh), F32),
        ],
        scratch_shapes=[pltpu.VMEM((bs, dv), F32)],
        compiler_params=_params(1, 32),
        name=name,
    )(b_i, b_f, z, z, z, z, zg, g_out.reshape(1, nh * dv), c0, n0, m0)


def _softmax_rows(s):
    e = jnp.exp(s - jnp.max(s, axis=-1, keepdims=True))
    return e / jnp.sum(e, axis=-1, keepdims=True)


def _xattn_seq_kernel(q_ref, k_ref, v_ref, o_ref):
    for h in range(XA_HEADS):
        sl = slice(h * XA_DH, (h + 1) * XA_DH)
        s = lax.dot_general(q_ref[:, sl].astype(BF16), k_ref[:, sl].astype(BF16),
                            (((1,), (1,)), ((), ())), preferred_element_type=F32)
        p = _softmax_rows(s * (XA_DH ** -0.5))
        o_ref[:, sl] = _mm(p, v_ref[:, sl]).astype(o_ref.dtype)


def _xattn_seq(xq, mk, mv, *, batch, seq, tq, name):
    nt = seq // tq
    assert seq % tq == 0
    d = D_MODEL
    return pl.pallas_call(
        _xattn_seq_kernel,
        grid=(batch, nt),
        in_specs=[
            pl.BlockSpec((tq, d), lambda b, t: (b * nt + t, 0)),
            pl.BlockSpec((N_MEM, d), lambda b, t: (b, 0)),
            pl.BlockSpec((N_MEM, d), lambda b, t: (b, 0)),
        ],
        out_specs=pl.BlockSpec((tq, d), lambda b, t: (b * nt + t, 0)),
        out_shape=jax.ShapeDtypeStruct((batch * seq, d), BF16),
        compiler_params=_params(2, 40),
        name=name,
    )(xq, mk, mv)


def _xattn_step_kernel(q_ref, k_ref, v_ref, o_ref, *, sb):
    nrow = N_MEM * XA_HEADS
    col_head = lax.broadcasted_iota(jnp.int32, (V7X_SUBLANES, nrow), 1) & (XA_HEADS - 1)
    row_head = lax.broadcasted_iota(jnp.int32, (V7X_SUBLANES, nrow), 0) & (XA_HEADS - 1)
    own = col_head == row_head
    for j in range(sb):
        kf = k_ref[j].reshape(nrow, XA_DH)
        vf = v_ref[j].reshape(nrow, XA_DH)
        q8 = jnp.concatenate([q_ref[j]] * (V7X_SUBLANES // XA_HEADS), axis=0)
        s = lax.dot_general(q8.astype(BF16), kf.astype(BF16), (((1,), (1,)), ((), ())),
                            preferred_element_type=F32)
        s = jnp.where(own, s * (XA_DH ** -0.5), NEG)
        e = jnp.where(own, jnp.exp(s - jnp.max(s, axis=-1, keepdims=True)), 0.0)
        p = e / jnp.sum(e, axis=-1, keepdims=True)
        o_ref[j] = _mm(p, vf)[0:XA_HEADS, :]


def _xattn_step(xq, ck, cv, *, sb, name):
    nb = xq.shape[0]
    assert XA_HEADS & (XA_HEADS - 1) == 0
    assert nb % sb == 0 and ck.shape == (nb, N_MEM, XA_HEADS, XA_DH)
    cache_spec = pl.BlockSpec((sb, N_MEM, XA_HEADS, XA_DH), lambda i: (i, 0, 0, 0))
    q_spec = pl.BlockSpec((sb, XA_HEADS, XA_DH), lambda i: (i, 0, 0))
    return pl.pallas_call(
        functools.partial(_xattn_step_kernel, sb=sb),
        grid=(nb // sb,),
        in_specs=[q_spec, cache_spec, cache_spec],
        out_specs=q_spec,
        out_shape=jax.ShapeDtypeStruct((nb, XA_HEADS, XA_DH), F32),
        compiler_params=_params(1, 40),
        name=name,
    )(xq.reshape(nb, XA_HEADS, XA_DH), ck, cv).reshape(nb, XA_HEADS * XA_DH)


def _tiles(rows):
    tm = min(rows, 1024)
    assert rows % tm == 0
    return tm


def kernel(x_prompt, x_sample, mem_prompt, state_rg_h, state_rg_conv, state_ml_C, state_ml_n, state_ml_m, cache_mem_k, cache_mem_v, g_mix, w_in, conv_w, conv_b, w_rg_a, b_rg_a, w_rg_x, b_rg_x, rg_lambda, b_ml_i, b_ml_f, g_rg_out, g_ml_out, w_out, g_xa, g_mem, w_xa_q, w_xa_k, w_xa_v, w_xa_o, g_ffn, w_ffn_gate, w_ffn_up, w_ffn_down, g_final):
    depth = g_mix.shape[0]
    assert depth == 1, "single trunk layer"
    bp, seq, d = x_prompt.shape
    bs_, dec_seq, _ = x_sample.shape
    assert d == D_MODEL and dec_seq == 1
    n_mem = mem_prompt.shape[1]
    assert n_mem == N_MEM
    dff = w_ffn_gate.shape[-1]
    in_w = w_in.shape[-1]
    assert in_w == IN_MAIN + N_GATE

    w_in_t = jnp.swapaxes(w_in, 1, 2).reshape(in_w, d)
    w_gate_pad = jnp.pad(w_in_t[IN_MAIN:], ((0, V7X_LANES - N_GATE), (0, 0)))
    cw = conv_w.reshape(CONV_W, RG_WIDTH)
    wgate = jnp.concatenate([w_rg_a.reshape(RG_BLOCKS, RG_BLOCK, RG_BLOCK),
                             w_rg_x.reshape(RG_BLOCKS, RG_BLOCK, RG_BLOCK)], axis=-1)
    rg_args = (cw, conv_b.reshape(-1), wgate, b_rg_a.reshape(-1), b_rg_x.reshape(-1),
               rg_lambda.reshape(-1), g_rg_out.reshape(-1))
    b_i = b_ml_i.reshape(ML_HEADS)
    b_f = b_ml_f.reshape(ML_HEADS)
    g_ml = g_ml_out.reshape(-1)
    w_out2 = w_out.reshape(d, d)
    w_q = w_xa_q.reshape(d, d)
    w_k = w_xa_k.reshape(d, d)
    w_v = w_xa_v.reshape(d, d)
    w_o = w_xa_o.reshape(d, d)
    w_fg = w_ffn_gate.reshape(d, dff)
    w_fu = w_ffn_up.reshape(d, dff)
    w_fd = w_ffn_down.reshape(dff, d)

    row_tile = _tiles(bp * seq)
    rg_zero_conv = jnp.zeros((bp, CONV_W - 1, RG_WIDTH), F32)
    rg_zero_h = jnp.zeros((bp, RG_WIDTH), F32)

    xs = x_sample.reshape(bs_, d)
    z_s, zg_s, w_in16 = _norm_linear(
        xs, g_mix.reshape(-1), w_in_t, n_out=IN_MAIN, tm=bs_, tn=1024, w_gate=w_gate_pad,
        w_is_nk=True, emit_w16=True, name="in_proj_s")
    y_rg_s, s_h, s_conv = _rglru_step(
        z_s, state_rg_conv.reshape(bs_, CONV_W - 1, RG_WIDTH), state_rg_h.reshape(bs_, RG_WIDTH),
        *rg_args, name="rglru_s")
    y_ml_s, s_c, s_n, s_m = _mlstm_step(
        z_s, zg_s, state_ml_C.reshape(bs_, ML_HEADS, ML_DK, ML_DV),
        state_ml_n.reshape(bs_, ML_HEADS, ML_DK), state_ml_m.reshape(bs_, ML_HEADS),
        b_i, b_f, g_ml, bs=V7X_SUBLANES, name="mlstm_s")
    x1_s, w_out16_rg, w_out16_ml = _linear_residual(
        [y_rg_s, y_ml_s], [(w_out2, 0), (w_out2, 1)], xs, tm=bs_, tn=1024, emit_w16=True,
        name="mix_out_s")
    xq_s, w_q16 = _norm_linear(x1_s, g_xa.reshape(-1), w_q, n_out=d, tm=bs_, tn=1024,
                               emit_w16=True, name="xa_q_s")
    o_s = _xattn_step(xq_s, cache_mem_k.reshape(bs_, n_mem, XA_HEADS, XA_DH),
                      cache_mem_v.reshape(bs_, n_mem, XA_HEADS, XA_DH), sb=2, name="xattn_s")
    x3_s, w_o16 = _linear_residual([o_s], [(w_o, 0)], x1_s, tm=bs_, tn=1024, emit_w16=True,
                                   name="xa_out_s")
    y_s, w_fg16, w_fu16, w_fd16 = _ffn(x3_s, g_ffn.reshape(-1), w_fg, w_fu, w_fd, g_final,
                                       tm=bs_, tf=512, emit_w16=True, vmem_mib=48, name="ffn_s")

    tp = bp * seq
    xp = x_prompt.reshape(tp, d)
    z_p, zg_p = _norm_linear(xp, g_mix.reshape(-1), w_in16, n_out=IN_MAIN, tm=row_tile, tn=1024,
                             w_gate=w_gate_pad, w_is_nk=True, name="in_proj_p")
    y_rg_p, p_h, p_conv = _rglru_seq(z_p, rg_zero_conv, rg_zero_h, *rg_args, batch=bp, seq=seq,
                                     tl=min(seq, 256), name="rglru_p")
    y_ml_p, p_c, p_n, p_m = _mlstm_seq(z_p, zg_p, b_i, b_f, g_ml, batch=bp, seq=seq,
                                       cs=min(seq, ML_CHUNK), name="mlstm_p")
    mem2 = mem_prompt.reshape(bp * n_mem, d)
    tmem = _tiles(bp * n_mem)
    mk = _norm_linear(mem2, g_mem.reshape(-1), w_k, n_out=d, tm=tmem, tn=1024, name="mem_k")
    mv = _norm_linear(mem2, g_mem.reshape(-1), w_v, n_out=d, tm=tmem, tn=1024, name="mem_v")
    x1_p = _linear_residual([y_rg_p, y_ml_p], [(w_out16_rg, 0), (w_out16_ml, 0)], xp,
                            tm=row_tile, tn=1024, name="mix_out_p")
    xq_p = _norm_linear(x1_p, g_xa.reshape(-1), w_q16, n_out=d, tm=row_tile, tn=1024,
                        out_dtype=BF16, name="xa_q_p")
    o_p = _xattn_seq(xq_p, mk, mv, batch=bp, seq=seq, tq=min(seq, 512), name="xattn_p")
    x3_p = _linear_residual([o_p], [(w_o16, 0)], x1_p, tm=row_tile, tn=1024, name="xa_out_p")
    y_p = _ffn(x3_p, g_ffn.reshape(-1), w_fg16, w_fu16, w_fd16, g_final, tm=row_tile, tf=512,
               vmem_mib=60, name="ffn_p")

    return (
        y_p.reshape(bp, seq, d),
        y_s.reshape(bs_, 1, d),
        p_h.reshape(1, bp, RG_WIDTH),
        p_conv.reshape(1, bp, CONV_W - 1, RG_WIDTH),
        p_c.reshape(1, bp, ML_HEADS, ML_DK, ML_DV),
        p_n.reshape(1, bp, ML_HEADS, ML_DK),
        p_m[:, :, 0].reshape(1, bp, ML_HEADS),
        mk.reshape(1, bp, n_mem, XA_HEADS, XA_DH),
        mv.reshape(1, bp, n_mem, XA_HEADS, XA_DH),
        s_h.reshape(1, bs_, RG_WIDTH),
        s_conv.reshape(1, bs_, CONV_W - 1, RG_WIDTH),
        s_c.reshape(1, bs_, ML_HEADS, ML_DK, ML_DV),
        s_n.reshape(1, bs_, ML_HEADS, ML_DK),
        s_m.reshape(1, bs_, ML_HEADS),
    )
```

```python
import functools

import jax
import jax.numpy as jnp
from jax import lax
from jax.experimental import pallas as pl
from jax.experimental.pallas import tpu as pltpu

F32 = jnp.float32
BF16 = jnp.bfloat16

D_MODEL = 2048
RG_WIDTH = D_MODEL // 2
RG_BLOCKS = 8
RG_BLOCK = RG_WIDTH // RG_BLOCKS
CONV_W = 4
RG_C = 8.0
ML_HEADS = 4
ML_WIDTH = D_MODEL - RG_WIDTH
ML_DV = ML_WIDTH // ML_HEADS
ML_DK = ML_DV // 2
N_MEM = 256
XA_HEADS = 4
XA_DH = D_MODEL // XA_HEADS
EPS = 1e-6
NEG = -1e30

OFF_RGX = 0
OFF_RGG = OFF_RGX + RG_WIDTH
OFF_Q = OFF_RGG + RG_WIDTH
OFF_K = OFF_Q + ML_HEADS * ML_DK
OFF_V = OFF_K + ML_HEADS * ML_DK
OFF_O = OFF_V + ML_WIDTH
OFF_I = OFF_O + ML_WIDTH
IN_MAIN = OFF_I
N_GATE = 2 * ML_HEADS

V7X_LANES = 128
V7X_SUBLANES = 8
V7X_VMEM_BYTES = 64 * 2**20

ML_CHUNK = 256


def _params(n_axes, vmem_mib):
    assert vmem_mib * 2**20 <= V7X_VMEM_BYTES
    return pltpu.CompilerParams(
        dimension_semantics=("arbitrary",) * n_axes,
        vmem_limit_bytes=vmem_mib * 2**20,
    )


def _rms(x, g):
    ms = jnp.mean(x * x, axis=-1, keepdims=True)
    return x * lax.rsqrt(ms + EPS) * g


def _softplus(u):
    return jnp.maximum(u, 0.0) + jnp.log1p(jnp.exp(-jnp.abs(u)))


def _log_sigmoid(u):
    return -_softplus(-u)


def _gelu_tanh(x):
    return x * (0.5 * (1.0 + jnp.tanh(0.7978845608028654 * (x + 0.044715 * (x * x * x)))))


def _mm(a, b):
    return jnp.dot(a.astype(BF16), b.astype(BF16), preferred_element_type=F32)


def _dot_w(a, w_ref, w_is_nk):
    w = w_ref[...].astype(BF16)
    if w_is_nk:
        return lax.dot_general(a, w, (((1,), (1,)), ((), ())), preferred_element_type=F32)
    return jnp.dot(a, w, preferred_element_type=F32)


def _norm_linear_kernel(*refs, with_gate, w_is_nk, emit_w16):
    refs = list(refs)
    x_ref, g_ref, w_ref = refs[:3]
    wg_ref = refs[3] if with_gate else None
    outs = refs[3 + with_gate:-1]
    xn_ref = refs[-1]
    o_ref = outs[0]
    og_ref = outs[1] if with_gate else None

    def column_tile(xn):
        o_ref[...] = _dot_w(xn, w_ref, w_is_nk).astype(o_ref.dtype)
        if emit_w16:
            outs[-1][...] = w_ref[...].astype(BF16)

    @pl.when(pl.program_id(1) == 0)
    def _():
        xn = _rms(x_ref[...], g_ref[...]).astype(BF16)
        xn_ref[...] = xn
        if with_gate:
            og_ref[...] = _dot_w(xn, wg_ref, w_is_nk)
        column_tile(xn)

    @pl.when(pl.program_id(1) > 0)
    def _():
        column_tile(xn_ref[...])


def _norm_linear(x, g, w, *, n_out, tm, tn, out_dtype=F32, w_gate=None, w_is_nk=False,
                 emit_w16=False, vmem_mib=48, name):
    m, k = x.shape
    k_ax, n_ax = (1, 0) if w_is_nk else (0, 1)
    assert m % tm == 0 and n_out % tn == 0 and w.shape[k_ax] == k and n_out <= w.shape[n_ax]
    assert not emit_w16 or m == tm
    with_gate = w_gate is not None
    w_spec = (pl.BlockSpec((tn, k), lambda i, j: (j, 0)) if w_is_nk
              else pl.BlockSpec((k, tn), lambda i, j: (0, j)))
    in_specs = [
        pl.BlockSpec((tm, k), lambda i, j: (i, 0)),
        pl.BlockSpec((1, k), lambda i, j: (0, 0)),
        w_spec,
    ]
    out_shape = [jax.ShapeDtypeStruct((m, n_out), out_dtype)]
    out_specs = [pl.BlockSpec((tm, tn), lambda i, j: (i, j))]
    args = [x, g.reshape(1, k), w]
    if with_gate:
        ng = w_gate.shape[n_ax]
        in_specs.append(pl.BlockSpec(w_gate.shape, lambda i, j: (0, 0)))
        out_shape.append(jax.ShapeDtypeStruct((m, ng), F32))
        out_specs.append(pl.BlockSpec((tm, ng), lambda i, j: (i, 0)))
        args.append(w_gate)
    if emit_w16:
        out_shape.append(jax.ShapeDtypeStruct((n_out, k) if w_is_nk else (k, n_out), BF16))
        out_specs.append(w_spec)
    out = pl.pallas_call(
        functools.partial(_norm_linear_kernel, with_gate=with_gate, w_is_nk=w_is_nk,
                          emit_w16=emit_w16),
        grid=(m // tm, n_out // tn),
        in_specs=in_specs,
        out_specs=out_specs,
        out_shape=out_shape,
        scratch_shapes=[pltpu.VMEM((tm, k), BF16)],
        compiler_params=_params(2, vmem_mib),
        name=name,
    )(*args)
    return out if len(out) > 1 else out[0]


def _linear_res_kernel(*refs, n_in, emit_w16):
    a_refs = refs[:n_in]
    w_refs = refs[n_in:2 * n_in]
    res_ref = refs[2 * n_in]
    o_ref = refs[2 * n_in + 1]
    acc = res_ref[...]
    for a_ref, w_ref in zip(a_refs, w_refs):
        acc = acc + _mm(a_ref[...], w_ref[...])
    o_ref[...] = acc
    if emit_w16:
        for w_ref, wc_ref in zip(w_refs, refs[2 * n_in + 2:]):
            wc_ref[...] = w_ref[...].astype(BF16)


def _linear_residual(parts, weights, res, *, tm, tn, emit_w16=False, vmem_mib=48, name):
    m, n = res.shape
    kp = parts[0].shape[1]
    assert all(p.shape == (m, kp) for p in parts) and len(weights) == len(parts)
    assert m % tm == 0 and n % tn == 0 and (not emit_w16 or m == tm)
    n_in = len(parts)
    in_specs = [pl.BlockSpec((tm, kp), lambda i, j: (i, 0)) for _ in parts]
    in_specs += [pl.BlockSpec((kp, tn), lambda i, j, rb=rb: (rb, j)) for _, rb in weights]
    in_specs.append(pl.BlockSpec((tm, tn), lambda i, j: (i, j)))
    out_specs = [pl.BlockSpec((tm, tn), lambda i, j: (i, j))]
    out_shape = [jax.ShapeDtypeStruct((m, n), F32)]
    if emit_w16:
        out_specs += [pl.BlockSpec((kp, tn), lambda i, j: (0, j)) for _ in parts]
        out_shape += [jax.ShapeDtypeStruct((kp, n), BF16) for _ in parts]
    out = pl.pallas_call(
        functools.partial(_linear_res_kernel, n_in=n_in, emit_w16=emit_w16),
        grid=(m // tm, n // tn),
        in_specs=in_specs,
        out_specs=out_specs,
        out_shape=out_shape,
        compiler_params=_params(2, vmem_mib),
        name=name,
    )(*parts, *[w for w, _ in weights], res)
    return out if emit_w16 else out[0]


def _ffn_kernel(x_ref, g_ref, wg_ref, wu_ref, wd_ref, gf_ref, o_ref, *rest, emit_w16):
    xf_ref = rest[-1]
    f = pl.program_id(1)
    last = pl.num_programs(1) - 1

    def hidden_tile(xf):
        wg = wg_ref[...].astype(BF16)
        wu = wu_ref[...].astype(BF16)
        wd = wd_ref[...].astype(BF16)
        if emit_w16:
            for dst, val in zip(rest[:3], (wg, wu, wd)):
                dst[...] = val
        gate = jnp.dot(xf, wg, preferred_element_type=F32)
        up = jnp.dot(xf, wu, preferred_element_type=F32)
        hidden = (gate * jax.nn.sigmoid(gate)) * up
        return jnp.dot(hidden.astype(BF16), wd, preferred_element_type=F32)

    @pl.when(f == 0)
    def _():
        x = x_ref[...]
        xf = _rms(x, g_ref[...]).astype(BF16)
        xf_ref[...] = xf
        o_ref[...] = x + hidden_tile(xf)

    @pl.when((f > 0) & (f < last))
    def _():
        o_ref[...] += hidden_tile(xf_ref[...])

    @pl.when(f == last)
    def _():
        o_ref[...] = _rms(o_ref[...] + hidden_tile(xf_ref[...]), gf_ref[...])


def _ffn(x, g, w_gate, w_up, w_down, g_final, *, tm, tf, emit_w16=False, vmem_mib, name):
    m, d = x.shape
    dff = w_gate.shape[1]
    assert m % tm == 0 and dff % tf == 0 and dff // tf >= 2 and (not emit_w16 or m == tm)
    up_spec = pl.BlockSpec((d, tf), lambda i, f: (0, f))
    down_spec = pl.BlockSpec((tf, d), lambda i, f: (f, 0))
    out_specs = [pl.BlockSpec((tm, d), lambda i, f: (i, 0))]
    out_shape = [jax.ShapeDtypeStruct((m, d), F32)]
    if emit_w16:
        out_specs += [up_spec, up_spec, down_spec]
        out_shape += [jax.ShapeDtypeStruct(w.shape, BF16) for w in (w_gate, w_up, w_down)]
    out = pl.pallas_call(
        functools.partial(_ffn_kernel, emit_w16=emit_w16),
        grid=(m // tm, dff // tf),
        in_specs=[
            pl.BlockSpec((tm, d), lambda i, f: (i, 0)),
            pl.BlockSpec((1, d), lambda i, f: (0, 0)),
            up_spec,
            up_spec,
            down_spec,
            pl.BlockSpec((1, d), lambda i, f: (0, 0)),
        ],
        out_specs=out_specs,
        out_shape=out_shape,
        scratch_shapes=[pltpu.VMEM((tm, d), BF16)],
        compiler_params=_params(2, vmem_mib),
        name=name,
    )(x, g.reshape(1, d), w_gate, w_up, w_down, g_final.reshape(1, d))
    return out if emit_w16 else out[0]


def _rg_gates(xr, wgate_ref, ba_ref, bx_ref, lam_ref, a_ref, b_ref):
    for n in range(RG_BLOCKS):
        sl = slice(n * RG_BLOCK, (n + 1) * RG_BLOCK)
        xn = xr[:, sl]
        g = _mm(xn, wgate_ref[n])
        r = jax.nn.sigmoid(g[:, :RG_BLOCK] + ba_ref[:, sl])
        ig = jax.nn.sigmoid(g[:, RG_BLOCK:] + bx_ref[:, sl])
        log_a = -RG_C * r * _softplus(-lam_ref[:, sl])
        a_ref[:, sl] = jnp.exp(log_a)
        mult = jnp.sqrt(jnp.maximum(1.0 - jnp.exp(2.0 * log_a), 0.0))
        b_ref[:, sl] = mult * (ig * xn)


def _rglru_seq_kernel(zx_ref, zg_ref, conv0_ref, h0_ref, cw_ref, cb_ref, wgate_ref, ba_ref,
                      bx_ref, lam_ref, gout_ref, y_ref, hlast_ref, convn_ref,
                      xe_ref, a_ref, b_ref, h_ref, hc_ref, *, tl):
    t = pl.program_id(1)
    pad = V7X_SUBLANES

    @pl.when(t == 0)
    def _():
        xe_ref[pad - 3:pad, :] = conv0_ref[...]
        hc_ref[...] = h0_ref[...]

    @pl.when(t > 0)
    def _():
        xe_ref[pad - 3:pad, :] = xe_ref[tl + pad - 3:tl + pad, :]

    x = zx_ref[...]
    xe_ref[pad:tl + pad, :] = x
    xr = (xe_ref[pad - 3:tl + pad - 3, :] * cw_ref[0:1, :]
          + xe_ref[pad - 2:tl + pad - 2, :] * cw_ref[1:2, :]
          + xe_ref[pad - 1:tl + pad - 1, :] * cw_ref[2:3, :]
          + x * cw_ref[3:4, :]) + cb_ref[...]
    _rg_gates(xr, wgate_ref, ba_ref, bx_ref, lam_ref, a_ref, b_ref)

    row = lax.broadcasted_iota(jnp.int32, (V7X_SUBLANES, RG_WIDTH), 0)

    def group(gi, hc):
        r0 = pl.multiple_of(gi * V7X_SUBLANES, V7X_SUBLANES)
        a8 = a_ref[pl.ds(r0, V7X_SUBLANES), :]
        b8 = b_ref[pl.ds(r0, V7X_SUBLANES), :]
        for d in (1, 2, 4):
            keep = row >= d
            b8 = jnp.where(keep, a8 * pltpu.roll(b8, d, axis=0) + b8, b8)
            a8 = jnp.where(keep, a8 * pltpu.roll(a8, d, axis=0), a8)
        h8 = a8 * hc + b8
        h_ref[pl.ds(r0, V7X_SUBLANES), :] = h8
        return h8[V7X_SUBLANES - 1:V7X_SUBLANES, :]

    hc = lax.fori_loop(0, tl // V7X_SUBLANES, group, hc_ref[...], unroll=4)
    hc_ref[...] = hc
    hlast_ref[...] = hc
    convn_ref[...] = xe_ref[tl + pad - 3:tl + pad, :]
    y = h_ref[...] * _gelu_tanh(zg_ref[...])
    y_ref[...] = _rms(y, gout_ref[...]).astype(y_ref.dtype)


def _rglru_seq(z, conv0, h0, cw, cb, wgate, ba, bx, lam, gout, *, batch, seq, tl, name):
    nt = seq // tl
    assert seq % tl == 0
    w = RG_WIDTH
    row = lambda v: v.reshape(1, w)
    const2 = lambda b, t: (0, 0)
    return pl.pallas_call(
        functools.partial(_rglru_seq_kernel, tl=tl),
        grid=(batch, nt),
        in_specs=[
            pl.BlockSpec((tl, w), lambda b, t: (b * nt + t, OFF_RGX // w)),
            pl.BlockSpec((tl, w), lambda b, t: (b * nt + t, OFF_RGG // w)),
            pl.BlockSpec((None, CONV_W - 1, w), lambda b, t: (b, 0, 0)),
            pl.BlockSpec((None, 1, w), lambda b, t: (b, 0, 0)),
            pl.BlockSpec((CONV_W, w), const2),
            pl.BlockSpec((1, w), const2),
            pl.BlockSpec((RG_BLOCKS, RG_BLOCK, 2 * RG_BLOCK), lambda b, t: (0, 0, 0)),
            pl.BlockSpec((1, w), const2),
            pl.BlockSpec((1, w), const2),
            pl.BlockSpec((1, w), const2),
            pl.BlockSpec((1, w), const2),
        ],
        out_specs=[
            pl.BlockSpec((tl, w), lambda b, t: (b * nt + t, 0)),
            pl.BlockSpec((None, 1, w), lambda b, t: (b, 0, 0)),
            pl.BlockSpec((None, CONV_W - 1, w), lambda b, t: (b, 0, 0)),
        ],
        out_shape=[
            jax.ShapeDtypeStruct((batch * seq, w), BF16),
            jax.ShapeDtypeStruct((batch, 1, w), F32),
            jax.ShapeDtypeStruct((batch, CONV_W - 1, w), F32),
        ],
        scratch_shapes=[
            pltpu.VMEM((tl + V7X_SUBLANES, w), F32),
            pltpu.VMEM((tl, w), F32),
            pltpu.VMEM((tl, w), F32),
            pltpu.VMEM((tl, w), F32),
            pltpu.VMEM((1, w), F32),
        ],
        compiler_params=_params(2, 32),
        name=name,
    )(z, z, conv0, h0.reshape(batch, 1, w), cw, row(cb), wgate, row(ba), row(bx), row(lam), row(gout))


def _rglru_step_kernel(zx_ref, zg_ref, conv_ref, h0_ref, cw_ref, cb_ref, wgate_ref, ba_ref,
                       bx_ref, lam_ref, gout_ref, y_ref, hn_ref, convn_ref, a_ref, b_ref):
    w = RG_WIDTH
    x = zx_ref[...]
    xr = (conv_ref[:, 0:w] * cw_ref[0:1, :] + conv_ref[:, w:2 * w] * cw_ref[1:2, :]
          + conv_ref[:, 2 * w:3 * w] * cw_ref[2:3, :] + x * cw_ref[3:4, :]) + cb_ref[...]
    _rg_gates(xr, wgate_ref, ba_ref, bx_ref, lam_ref, a_ref, b_ref)
    h = a_ref[...] * h0_ref[...] + b_ref[...]
    hn_ref[...] = h
    convn_ref[:, 0:2 * w] = conv_ref[:, w:3 * w]
    convn_ref[:, 2 * w:3 * w] = x
    y_ref[...] = _rms(h * _gelu_tanh(zg_ref[...]), gout_ref[...]).astype(y_ref.dtype)


def _rglru_step(z, conv, h0, cw, cb, wgate, ba, bx, lam, gout, *, name):
    nb = z.shape[0]
    w = RG_WIDTH
    row = lambda v: v.reshape(1, w)
    c0 = lambda i: (0, 0)
    return pl.pallas_call(
        _rglru_step_kernel,
        grid=(1,),
        in_specs=[
            pl.BlockSpec((nb, w), lambda i: (0, OFF_RGX // w)),
            pl.BlockSpec((nb, w), lambda i: (0, OFF_RGG // w)),
            pl.BlockSpec((nb, (CONV_W - 1) * w), c0),
            pl.BlockSpec((nb, w), c0),
            pl.BlockSpec((CONV_W, w), c0),
            pl.BlockSpec((1, w), c0),
            pl.BlockSpec((RG_BLOCKS, RG_BLOCK, 2 * RG_BLOCK), lambda i: (0, 0, 0)),
            pl.BlockSpec((1, w), c0),
            pl.BlockSpec((1, w), c0),
            pl.BlockSpec((1, w), c0),
            pl.BlockSpec((1, w), c0),
        ],
        out_specs=[
            pl.BlockSpec((nb, w), c0),
            pl.BlockSpec((nb, w), c0),
            pl.BlockSpec((nb, (CONV_W - 1) * w), c0),
        ],
        out_shape=[
            jax.ShapeDtypeStruct((nb, w), BF16),
            jax.ShapeDtypeStruct((nb, w), F32),
            jax.ShapeDtypeStruct((nb, (CONV_W - 1) * w), F32),
        ],
        scratch_shapes=[pltpu.VMEM((nb, w), F32), pltpu.VMEM((nb, w), F32)],
        compiler_params=_params(1, 32),
        name=name,
    )(z, z, conv.reshape(nb, (CONV_W - 1) * w), h0, cw, row(cb), wgate, row(ba), row(bx),
      row(lam), row(gout))


def _mlstm_seq_kernel(bi_ref, bf_ref, q_ref, k_ref, v_ref, o_ref, zg_ref, g_ref,
                      y_ref, c_ref, n_ref, m_ref, zgt_ref, cs_ref, ns_ref, *, seq, cs):
    h = pl.program_id(1)
    nc = seq // cs
    bi = bi_ref[h]
    bf = bf_ref[h]
    for c in range(nc):
        zgt_ref[c] = zg_ref[c * cs:(c + 1) * cs, :].T
    cs_ref[...] = jnp.zeros_like(cs_ref)
    ns_ref[...] = jnp.zeros_like(ns_ref)

    t_idx = lax.broadcasted_iota(jnp.int32, (cs, cs), 0)
    s_idx = lax.broadcasted_iota(jnp.int32, (cs, cs), 1)
    causal = s_idx <= t_idx
    lane = lax.broadcasted_iota(jnp.int32, (cs, V7X_LANES), 1)

    def chunk(c, m):
        r0 = pl.multiple_of(c * cs, cs)
        q = q_ref[pl.ds(r0, cs), :]
        k = k_ref[pl.ds(r0, cs), :] * (ML_DK ** -0.5)
        v = v_ref[pl.ds(r0, cs), :]
        zg = zg_ref[pl.ds(r0, cs), :]
        li_col = jnp.sum(jnp.where(lane == h, zg, 0.0), axis=1, keepdims=True) + bi
        lf_col = _log_sigmoid(
            jnp.sum(jnp.where(lane == h + ML_HEADS, zg, 0.0), axis=1, keepdims=True) + bf)
        li_row = zgt_ref[c, pl.ds(h, 1), :] + bi
        lf_row = _log_sigmoid(zgt_ref[c, pl.ds(h + ML_HEADS, 1), :] + bf)
        bcum_col = jnp.sum(jnp.where(causal, lf_row, 0.0), axis=1, keepdims=True)
        bcum_row = jnp.sum(jnp.where(t_idx <= s_idx, lf_col, 0.0), axis=0, keepdims=True)
        log_d = jnp.where(causal, bcum_col - bcum_row + li_row, NEG)
        inter = bcum_col + m
        m_t = jnp.maximum(inter, jnp.max(log_d, axis=1, keepdims=True))
        dmat = jnp.exp(log_d - m_t)
        sc = jnp.exp(inter - m_t)
        qb = q.astype(BF16)
        kb = k.astype(BF16)
        vb = v.astype(BF16)
        qk = lax.dot_general(qb, kb, (((1,), (1,)), ((), ())), preferred_element_type=F32) * dmat
        c_old = cs_ref[...]
        n_old = ns_ref[...]
        num = sc * jnp.dot(qb, c_old.astype(BF16), preferred_element_type=F32) + _mm(qk, vb)
        den = sc * jnp.sum(q * n_old, axis=1, keepdims=True) + jnp.sum(qk, axis=1, keepdims=True)
        den = jnp.maximum(jnp.abs(den), jnp.exp(-m_t))
        hh = num / den
        m_new = m_t[cs - 1:cs, :]
        b_last = bcum_col[cs - 1:cs, :]
        w_end = jnp.exp(b_last - bcum_col + li_col - m_new)
        dec = jnp.exp(b_last + m - m_new)
        wk = w_end * k
        cs_ref[...] = dec * c_old + lax.dot_general(
            wk.astype(BF16), vb, (((0,), (0,)), ((), ())), preferred_element_type=F32)
        ns_ref[...] = dec * n_old + jnp.sum(wk, axis=0, keepdims=True)
        y = _rms(hh, g_ref[...]) * jax.nn.sigmoid(o_ref[pl.ds(r0, cs), :])
        y_ref[pl.ds(r0, cs), :] = y.astype(y_ref.dtype)
        return m_new

    m_fin = lax.fori_loop(0, nc, chunk, jnp.zeros((1, 1), F32))
    c_ref[...] = cs_ref[...]
    n_ref[pl.ds(h, 1), :] = ns_ref[...]
    m_ref[pl.ds(h, 1), :] = jnp.broadcast_to(m_fin, (1, V7X_LANES))


def _mlstm_seq(z, zg, b_i, b_f, g_out, *, batch, seq, cs, name):
    assert seq % cs == 0 and zg.shape[1] == V7X_LANES
    dk, dv, nh = ML_DK, ML_DV, ML_HEADS
    smem = pl.BlockSpec(memory_space=pltpu.SMEM)
    return pl.pallas_call(
        functools.partial(_mlstm_seq_kernel, seq=seq, cs=cs),
        grid=(batch, nh),
        in_specs=[
            smem, smem,
            pl.BlockSpec((seq, dk), lambda b, h: (b, OFF_Q // dk + h)),
            pl.BlockSpec((seq, dk), lambda b, h: (b, OFF_K // dk + h)),
            pl.BlockSpec((seq, dv), lambda b, h: (b, OFF_V // dv + h)),
            pl.BlockSpec((seq, dv), lambda b, h: (b, OFF_O // dv + h)),
            pl.BlockSpec((seq, V7X_LANES), lambda b, h: (b, 0)),
            pl.BlockSpec((1, dv), lambda b, h: (0, h)),
        ],
        out_specs=[
            pl.BlockSpec((seq, dv), lambda b, h: (b, h)),
            pl.BlockSpec((None, None, dk, dv), lambda b, h: (b, h, 0, 0)),
            pl.BlockSpec((None, nh, dk), lambda b, h: (b, 0, 0)),
            pl.BlockSpec((None, nh, V7X_LANES), lambda b, h: (b, 0, 0)),
        ],
        out_shape=[
            jax.ShapeDtypeStruct((batch * seq, nh * dv), BF16),
            jax.ShapeDtypeStruct((batch, nh, dk, dv), F32),
            jax.ShapeDtypeStruct((batch, nh, dk), F32),
            jax.ShapeDtypeStruct((batch, nh, V7X_LANES), F32),
        ],
        scratch_shapes=[
            pltpu.VMEM((seq // cs, V7X_LANES, cs), F32),
            pltpu.VMEM((dk, dv), F32),
            pltpu.VMEM((1, dk), F32),
        ],
        compiler_params=_params(2, 40),
        name=name,
    )(b_i, b_f, z, z, z, z, zg, g_out.reshape(1, nh * dv))


def _mlstm_step_kernel(bi_ref, bf_ref, q_ref, k_ref, v_ref, o_ref, zg_ref, g_ref, c0_ref, n0_ref,
                       m0_ref, y_ref, c_ref, n_ref, m_ref, qc_ref, *, bs):
    dk, dv = ML_DK, ML_DV
    eye = (lax.broadcasted_iota(jnp.int32, (dk, dk), 0)
           == lax.broadcasted_iota(jnp.int32, (dk, dk), 1))
    zg = zg_ref[...]
    for h in range(ML_HEADS):
        li = zg[:, h:h + 1] + bi_ref[h]
        lf = _log_sigmoid(zg[:, ML_HEADS + h:ML_HEADS + h + 1] + bf_ref[h])
        m = m0_ref[:, h:h + 1]
        inter = lf + m
        m_t = jnp.maximum(inter, li)
        dgate = jnp.exp(li - m_t)
        sc = jnp.exp(inter - m_t)
        q = q_ref[:, h * dk:(h + 1) * dk]
        k = k_ref[:, h * dk:(h + 1) * dk] * (ML_DK ** -0.5)
        v = v_ref[:, h * dv:(h + 1) * dv]
        n_old = n0_ref[:, h, :]
        qk = jnp.sum(q * k, axis=1, keepdims=True) * dgate
        w_end = jnp.exp(li - m_t)
        dec = jnp.exp(inter - m_t)
        wk = w_end * k
        for j in range(bs):
            c_old = c0_ref[j, h]
            qc_ref[j:j + 1, :] = _mm(q[j:j + 1, :], c_old)
            wk_col = jnp.sum(
                jnp.where(eye, jnp.broadcast_to(wk[j:j + 1, :], (dk, dk)), 0.0),
                axis=1, keepdims=True)
            c_ref[j, h] = dec[j:j + 1, :] * c_old + wk_col * v[j:j + 1, :]
        num = sc * qc_ref[...] + qk * v
        den = sc * jnp.sum(q * n_old, axis=1, keepdims=True) + qk
        den = jnp.maximum(jnp.abs(den), jnp.exp(-m_t))
        hh = num / den
        n_ref[:, h, :] = dec * n_old + wk
        m_ref[:, h:h + 1] = m_t
        y = _rms(hh, g_ref[:, h * dv:(h + 1) * dv]) * jax.nn.sigmoid(o_ref[:, h * dv:(h + 1) * dv])
        y_ref[:, h * dv:(h + 1) * dv] = y.astype(y_ref.dtype)


def _mlstm_step(z, zg, c0, n0, m0, b_i, b_f, g_out, *, bs, name):
    nb = z.shape[0]
    assert nb % bs == 0
    dk, dv, nh = ML_DK, ML_DV, ML_HEADS
    smem = pl.BlockSpec(memory_space=pltpu.SMEM)
    return pl.pallas_call(
        functools.partial(_mlstm_step_kernel, bs=bs),
        grid=(nb // bs,),
        in_specs=[
            smem, smem,
            pl.BlockSpec((bs, nh * dk), lambda i: (i, OFF_Q // (nh * dk))),
            pl.BlockSpec((bs, nh * dk), lambda i: (i, OFF_K // (nh * dk))),
            pl.BlockSpec((bs, nh * dv), lambda i: (i, OFF_V // (nh * dv))),
            pl.BlockSpec((bs, nh * dv), lambda i: (i, OFF_O // (nh * dv))),
            pl.BlockSpec((bs, V7X_LANES), lambda i: (i, 0)),
            pl.BlockSpec((1, nh * dv), lambda i: (0, 0)),
            pl.BlockSpec((bs, nh, dk, dv), lambda i: (i, 0, 0, 0)),
            pl.BlockSpec((bs, nh, dk), lambda i: (i, 0, 0)),
            pl.BlockSpec((bs, nh), lambda i: (i, 0)),
        ],
        out_specs=[
            pl.BlockSpec((bs, nh * dv), lambda i: (i, 0)),
            pl.BlockSpec((bs, nh, dk, dv), lambda i: (i, 0, 0, 0)),
            pl.BlockSpec((bs, nh, dk), lambda i: (i, 0, 0)),
            pl.BlockSpec((bs, nh), lambda i: (i, 0)),
        ],
        out_shape=[
            jax.ShapeDtypeStruct((nb, nh * dv), F32),
            jax.ShapeDtypeStruct((nb, nh, dk, dv), F32),
            jax.ShapeDtypeStruct((nb, nh, dk), F32),
            jax.ShapeDtypeStruct((nb, nh), F32),
        ],
        scratch_shapes=[pltpu.VMEM((bs, dv), F32)],
        compiler_params=_params(1, 32),
        name=name,
    )(b_i, b_f, z, z, z, z, zg, g_out.reshape(1, nh * dv), c0, n0, m0)


def _softmax_rows(s):
    e = jnp.exp(s - jnp.max(s, axis=-1, keepdims=True))
    return e / jnp.sum(e, axis=-1, keepdims=True)


def _xattn_seq_kernel(q_ref, k_ref, v_ref, o_ref):
    for h in range(XA_HEADS):
        sl = slice(h * XA_DH, (h + 1) * XA_DH)
        s = lax.dot_general(q_ref[:, sl].astype(BF16), k_ref[:, sl].astype(BF16),
                            (((1,), (1,)), ((), ())), preferred_element_type=F32)
        p = _softmax_rows(s * (XA_DH ** -0.5))
        o_ref[:, sl] = _mm(p, v_ref[:, sl]).astype(o_ref.dtype)


def _xattn_seq(xq, mk, mv, *, batch, seq, tq, name):
    nt = seq // tq
    assert seq % tq == 0
    d = D_MODEL
    return pl.pallas_call(
        _xattn_seq_kernel,
        grid=(batch, nt),
        in_specs=[
            pl.BlockSpec((tq, d), lambda b, t: (b * nt + t, 0)),
            pl.BlockSpec((N_MEM, d), lambda b, t: (b, 0)),
            pl.BlockSpec((N_MEM, d), lambda b, t: (b, 0)),
        ],
        out_specs=pl.BlockSpec((tq, d), lambda b, t: (b * nt + t, 0)),
        out_shape=jax.ShapeDtypeStruct((batch * seq, d), BF16),
        compiler_params=_params(2, 40),
        name=name,
    )(xq, mk, mv)


def _xattn_step_kernel(q_ref, k_ref, v_ref, o_ref, *, sb):
    nrow = N_MEM * XA_HEADS
    col_head = lax.broadcasted_iota(jnp.int32, (V7X_SUBLANES, nrow), 1) & (XA_HEADS - 1)
    row_head = lax.broadcasted_iota(jnp.int32, (V7X_SUBLANES, nrow), 0) & (XA_HEADS - 1)
    own = col_head == row_head
    for j in range(sb):
        kf = k_ref[j].reshape(nrow, XA_DH)
        vf = v_ref[j].reshape(nrow, XA_DH)
        q8 = jnp.concatenate([q_ref[j]] * (V7X_SUBLANES // XA_HEADS), axis=0)
        s = lax.dot_general(q8.astype(BF16), kf.astype(BF16), (((1,), (1,)), ((), ())),
                            preferred_element_type=F32)
        s = jnp.where(own, s * (XA_DH ** -0.5), NEG)
        e = jnp.where(own, jnp.exp(s - jnp.max(s, axis=-1, keepdims=True)), 0.0)
        p = e / jnp.sum(e, axis=-1, keepdims=True)
        o_ref[j] = _mm(p, vf)[0:XA_HEADS, :]


def _xattn_step(xq, ck, cv, *, sb, name):
    nb = xq.shape[0]
    assert XA_HEADS & (XA_HEADS - 1) == 0
    assert nb % sb == 0 and ck.shape == (nb, N_MEM, XA_HEADS, XA_DH)
    cache_spec = pl.BlockSpec((sb, N_MEM, XA_HEADS, XA_DH), lambda i: (i, 0, 0, 0))
    q_spec = pl.BlockSpec((sb, XA_HEADS, XA_DH), lambda i: (i, 0, 0))
    return pl.pallas_call(
        functools.partial(_xattn_step_kernel, sb=sb),
        grid=(nb // sb,),
        in_specs=[q_spec, cache_spec, cache_spec],
        out_specs=q_spec,
        out_shape=jax.ShapeDtypeStruct((nb, XA_HEADS, XA_DH), F32),
        compiler_params=_params(1, 40),
        name=name,
    )(xq.reshape(nb, XA_HEADS, XA_DH), ck, cv).reshape(nb, XA_HEADS * XA_DH)


def _tiles(rows):
    tm = min(rows, 1024)
    assert rows % tm == 0
    return tm


def kernel(x_prompt, x_sample, mem_prompt, state_rg_h, state_rg_conv, state_ml_C, state_ml_n, state_ml_m, cache_mem_k, cache_mem_v, g_mix, w_in, conv_w, conv_b, w_rg_a, b_rg_a, w_rg_x, b_rg_x, rg_lambda, b_ml_i, b_ml_f, g_rg_out, g_ml_out, w_out, g_xa, g_mem, w_xa_q, w_xa_k, w_xa_v, w_xa_o, g_ffn, w_ffn_gate, w_ffn_up, w_ffn_down, g_final):
    depth = g_mix.shape[0]
    assert depth == 1, "single trunk layer"
    bp, seq, d = x_prompt.shape
    bs_, dec_seq, _ = x_sample.shape
    assert d == D_MODEL and dec_seq == 1
    n_mem = mem_prompt.shape[1]
    assert n_mem == N_MEM
    dff = w_ffn_gate.shape[-1]
    in_w = w_in.shape[-1]
    assert in_w == IN_MAIN + N_GATE

    w_in_t = jnp.swapaxes(w_in, 1, 2).reshape(in_w, d)
    w_gate_pad = jnp.pad(w_in_t[IN_MAIN:], ((0, V7X_LANES - N_GATE), (0, 0)))
    cw = conv_w.reshape(CONV_W, RG_WIDTH)
    wgate = jnp.concatenate([w_rg_a.reshape(RG_BLOCKS, RG_BLOCK, RG_BLOCK),
                             w_rg_x.reshape(RG_BLOCKS, RG_BLOCK, RG_BLOCK)], axis=-1)
    rg_args = (cw, conv_b.reshape(-1), wgate, b_rg_a.reshape(-1), b_rg_x.reshape(-1),
               rg_lambda.reshape(-1), g_rg_out.reshape(-1))
    b_i = b_ml_i.reshape(ML_HEADS)
    b_f = b_ml_f.reshape(ML_HEADS)
    g_ml = g_ml_out.reshape(-1)
    w_out2 = w_out.reshape(d, d)
    w_q = w_xa_q.reshape(d, d)
    w_k = w_xa_k.reshape(d, d)
    w_v = w_xa_v.reshape(d, d)
    w_o = w_xa_o.reshape(d, d)
    w_fg = w_ffn_gate.reshape(d, dff)
    w_fu = w_ffn_up.reshape(d, dff)
    w_fd = w_ffn_down.reshape(dff, d)

    row_tile = _tiles(bp * seq)
    rg_zero_conv = jnp.zeros((bp, CONV_W - 1, RG_WIDTH), F32)
    rg_zero_h = jnp.zeros((bp, RG_WIDTH), F32)

    xs = x_sample.reshape(bs_, d)
    z_s, zg_s, w_in16 = _norm_linear(
        xs, g_mix.reshape(-1), w_in_t, n_out=IN_MAIN, tm=bs_, tn=1024, w_gate=w_gate_pad,
        w_is_nk=True, emit_w16=True, name="in_proj_s")
    y_rg_s, s_h, s_conv = _rglru_step(
        z_s, state_rg_conv.reshape(bs_, CONV_W - 1, RG_WIDTH), state_rg_h.reshape(bs_, RG_WIDTH),
        *rg_args, name="rglru_s")
    y_ml_s, s_c, s_n, s_m = _mlstm_step(
        z_s, zg_s, state_ml_C.reshape(bs_, ML_HEADS, ML_DK, ML_DV),
        state_ml_n.reshape(bs_, ML_HEADS, ML_DK), state_ml_m.reshape(bs_, ML_HEADS),
        b_i, b_f, g_ml, bs=V7X_SUBLANES, name="mlstm_s")
    x1_s, w_out16_rg, w_out16_ml = _linear_residual(
        [y_rg_s, y_ml_s], [(w_out2, 0), (w_out2, 1)], xs, tm=bs_, tn=1024, emit_w16=True,
        name="mix_out_s")
    xq_s, w_q16 = _norm_linear(x1_s, g_xa.reshape(-1), w_q, n_out=d, tm=bs_, tn=1024,
                               emit_w16=True, name="xa_q_s")
    o_s = _xattn_step(xq_s, cache_mem_k.reshape(bs_, n_mem, XA_HEADS, XA_DH),
                      cache_mem_v.reshape(bs_, n_mem, XA_HEADS, XA_DH), sb=2, name="xattn_s")
    x3_s, w_o16 = _linear_residual([o_s], [(w_o, 0)], x1_s, tm=bs_, tn=1024, emit_w16=True,
                                   name="xa_out_s")
    y_s, w_fg16, w_fu16, w_fd16 = _ffn(x3_s, g_ffn.reshape(-1), w_fg, w_fu, w_fd, g_final,
                                       tm=bs_, tf=512, emit_w16=True, vmem_mib=48, name="ffn_s")

    tp = bp * seq
    xp = x_prompt.reshape(tp, d)
    z_p, zg_p = _norm_linear(xp, g_mix.reshape(-1), w_in16, n_out=IN_MAIN, tm=row_tile, tn=1024,
                             w_gate=w_gate_pad, w_is_nk=True, name="in_proj_p")
    y_rg_p, p_h, p_conv = _rglru_seq(z_p, rg_zero_conv, rg_zero_h, *rg_args, batch=bp, seq=seq,
                                     tl=min(seq, 256), name="rglru_p")
    y_ml_p, p_c, p_n, p_m = _mlstm_seq(z_p, zg_p, b_i, b_f, g_ml, batch=bp, seq=seq,
                                       cs=min(seq, ML_CHUNK), name="mlstm_p")
    mem2 = mem_prompt.reshape(bp * n_mem, d)
    tmem = _tiles(bp * n_mem)
    mk = _norm_linear(mem2, g_mem.reshape(-1), w_k, n_out=d, tm=tmem, tn=1024, name="mem_k")
    mv = _norm_linear(mem2, g_mem.reshape(-1), w_v, n_out=d, tm=tmem, tn=1024, name="mem_v")
    proj_tile = min(row_tile, 512)
    x1_p = _linear_residual([y_rg_p, y_ml_p], [(w_out16_rg, 0), (w_out16_ml, 0)], xp,
                            tm=proj_tile, tn=d, name="mix_out_p")
    xq_p = _norm_linear(x1_p, g_xa.reshape(-1), w_q16, n_out=d, tm=proj_tile, tn=d,
                        out_dtype=BF16, name="xa_q_p")
    o_p = _xattn_seq(xq_p, mk, mv, batch=bp, seq=seq, tq=min(seq, 512), name="xattn_p")
    x3_p = _linear_residual([o_p], [(w_o16, 0)], x1_p, tm=proj_tile, tn=d, name="xa_out_p")
    y_p = _ffn(x3_p, g_ffn.reshape(-1), w_fg16, w_fu16, w_fd16, g_final, tm=row_tile, tf=512,
               vmem_mib=60, name="ffn_p")

    return (
        y_p.reshape(bp, seq, d),
        y_s.reshape(bs_, 1, d),
        p_h.reshape(1, bp, RG_WIDTH),
        p_conv.reshape(1, bp, CONV_W - 1, RG_WIDTH),
        p_c.reshape(1, bp, ML_HEADS, ML_DK, ML_DV),
        p_n.reshape(1, bp, ML_HEADS, ML_DK),
        p_m[:, :, 0].reshape(1, bp, ML_HEADS),
        mk.reshape(1, bp, n_mem, XA_HEADS, XA_DH),
        mv.reshape(1, bp, n_mem, XA_HEADS, XA_DH),
        s_h.reshape(1, bs_, RG_WIDTH),
        s_conv.reshape(1, bs_, CONV_W - 1, RG_WIDTH),
        s_c.reshape(1, bs_, ML_HEADS, ML_DK, ML_DV),
        s_n.reshape(1, bs_, ML_HEADS, ML_DK),
        s_m.reshape(1, bs_, ML_HEADS),
    )
```

```python
import functools

import jax
import jax.numpy as jnp
from jax import lax
from jax.experimental import pallas as pl
from jax.experimental.pallas import tpu as pltpu

F32 = jnp.float32
BF16 = jnp.bfloat16

D_MODEL = 2048
RG_WIDTH = D_MODEL // 2
RG_BLOCKS = 8
RG_BLOCK = RG_WIDTH // RG_BLOCKS
CONV_W = 4
RG_C = 8.0
ML_HEADS = 4
ML_WIDTH = D_MODEL - RG_WIDTH
ML_DV = ML_WIDTH // ML_HEADS
ML_DK = ML_DV // 2
N_MEM = 256
XA_HEADS = 4
XA_DH = D_MODEL // XA_HEADS
EPS = 1e-6
NEG = -1e30

OFF_RGX = 0
OFF_RGG = OFF_RGX + RG_WIDTH
OFF_Q = OFF_RGG + RG_WIDTH
OFF_K = OFF_Q + ML_HEADS * ML_DK
OFF_V = OFF_K + ML_HEADS * ML_DK
OFF_O = OFF_V + ML_WIDTH
OFF_I = OFF_O + ML_WIDTH
IN_MAIN = OFF_I
N_GATE = 2 * ML_HEADS

V7X_LANES = 128
V7X_SUBLANES = 8
V7X_VMEM_BYTES = 64 * 2**20

ML_CHUNK = 256


def _params(n_axes, vmem_mib):
    assert vmem_mib * 2**20 <= V7X_VMEM_BYTES
    return pltpu.CompilerParams(
        dimension_semantics=("arbitrary",) * n_axes,
        vmem_limit_bytes=vmem_mib * 2**20,
    )


def _rms(x, g):
    ms = jnp.mean(x * x, axis=-1, keepdims=True)
    return x * lax.rsqrt(ms + EPS) * g


def _softplus(u):
    return jnp.maximum(u, 0.0) + jnp.log1p(jnp.exp(-jnp.abs(u)))


def _log_sigmoid(u):
    return -_softplus(-u)


def _gelu_tanh(x):
    return x * (0.5 * (1.0 + jnp.tanh(0.7978845608028654 * (x + 0.044715 * (x * x * x)))))


def _mm(a, b):
    return jnp.dot(a.astype(BF16), b.astype(BF16), preferred_element_type=F32)


def _dot_w(a, w_ref, w_is_nk):
    w = w_ref[...].astype(BF16)
    if w_is_nk:
        return lax.dot_general(a, w, (((1,), (1,)), ((), ())), preferred_element_type=F32)
    return jnp.dot(a, w, preferred_element_type=F32)


def _norm_linear_kernel(*refs, with_gate, w_is_nk, emit_w16):
    refs = list(refs)
    x_ref, g_ref, w_ref = refs[:3]
    wg_ref = refs[3] if with_gate else None
    outs = refs[3 + with_gate:-1]
    xn_ref = refs[-1]
    o_ref = outs[0]
    og_ref = outs[1] if with_gate else None

    def column_tile(xn):
        o_ref[...] = _dot_w(xn, w_ref, w_is_nk).astype(o_ref.dtype)
        if emit_w16:
            outs[-1][...] = w_ref[...].astype(BF16)

    @pl.when(pl.program_id(1) == 0)
    def _():
        xn = _rms(x_ref[...], g_ref[...]).astype(BF16)
        xn_ref[...] = xn
        if with_gate:
            og_ref[...] = _dot_w(xn, wg_ref, w_is_nk)
        column_tile(xn)

    @pl.when(pl.program_id(1) > 0)
    def _():
        column_tile(xn_ref[...])


def _norm_linear(x, g, w, *, n_out, tm, tn, out_dtype=F32, w_gate=None, w_is_nk=False,
                 emit_w16=False, vmem_mib=48, name):
    m, k = x.shape
    k_ax, n_ax = (1, 0) if w_is_nk else (0, 1)
    assert m % tm == 0 and n_out % tn == 0 and w.shape[k_ax] == k and n_out <= w.shape[n_ax]
    assert not emit_w16 or m == tm
    with_gate = w_gate is not None
    w_spec = (pl.BlockSpec((tn, k), lambda i, j: (j, 0)) if w_is_nk
              else pl.BlockSpec((k, tn), lambda i, j: (0, j)))
    in_specs = [
        pl.BlockSpec((tm, k), lambda i, j: (i, 0)),
        pl.BlockSpec((1, k), lambda i, j: (0, 0)),
        w_spec,
    ]
    out_shape = [jax.ShapeDtypeStruct((m, n_out), out_dtype)]
    out_specs = [pl.BlockSpec((tm, tn), lambda i, j: (i, j))]
    args = [x, g.reshape(1, k), w]
    if with_gate:
        ng = w_gate.shape[n_ax]
        in_specs.append(pl.BlockSpec(w_gate.shape, lambda i, j: (0, 0)))
        out_shape.append(jax.ShapeDtypeStruct((m, ng), F32))
        out_specs.append(pl.BlockSpec((tm, ng), lambda i, j: (i, 0)))
        args.append(w_gate)
    if emit_w16:
        out_shape.append(jax.ShapeDtypeStruct((n_out, k) if w_is_nk else (k, n_out), BF16))
        out_specs.append(w_spec)
    out = pl.pallas_call(
        functools.partial(_norm_linear_kernel, with_gate=with_gate, w_is_nk=w_is_nk,
                          emit_w16=emit_w16),
        grid=(m // tm, n_out // tn),
        in_specs=in_specs,
        out_specs=out_specs,
        out_shape=out_shape,
        scratch_shapes=[pltpu.VMEM((tm, k), BF16)],
        compiler_params=_params(2, vmem_mib),
        name=name,
    )(*args)
    return out if len(out) > 1 else out[0]


def _linear_res_kernel(*refs, n_in, emit_w16):
    a_refs = refs[:n_in]
    w_refs = refs[n_in:2 * n_in]
    res_ref = refs[2 * n_in]
    o_ref = refs[2 * n_in + 1]
    acc = res_ref[...]
    for a_ref, w_ref in zip(a_refs, w_refs):
        acc = acc + _mm(a_ref[...], w_ref[...])
    o_ref[...] = acc
    if emit_w16:
        for w_ref, wc_ref in zip(w_refs, refs[2 * n_in + 2:]):
            wc_ref[...] = w_ref[...].astype(BF16)


def _linear_residual(parts, weights, res, *, tm, tn, emit_w16=False, vmem_mib=48, name):
    m, n = res.shape
    kp = parts[0].shape[1]
    assert all(p.shape == (m, kp) for p in parts) and len(weights) == len(parts)
    assert m % tm == 0 and n % tn == 0 and (not emit_w16 or m == tm)
    n_in = len(parts)
    in_specs = [pl.BlockSpec((tm, kp), lambda i, j: (i, 0)) for _ in parts]
    in_specs += [pl.BlockSpec((kp, tn), lambda i, j, rb=rb: (rb, j)) for _, rb in weights]
    in_specs.append(pl.BlockSpec((tm, tn), lambda i, j: (i, j)))
    out_specs = [pl.BlockSpec((tm, tn), lambda i, j: (i, j))]
    out_shape = [jax.ShapeDtypeStruct((m, n), F32)]
    if emit_w16:
        out_specs += [pl.BlockSpec((kp, tn), lambda i, j: (0, j)) for _ in parts]
        out_shape += [jax.ShapeDtypeStruct((kp, n), BF16) for _ in parts]
    out = pl.pallas_call(
        functools.partial(_linear_res_kernel, n_in=n_in, emit_w16=emit_w16),
        grid=(m // tm, n // tn),
        in_specs=in_specs,
        out_specs=out_specs,
        out_shape=out_shape,
        compiler_params=_params(2, vmem_mib),
        name=name,
    )(*parts, *[w for w, _ in weights], res)
    return out if emit_w16 else out[0]


def _ffn_kernel(x_ref, g_ref, wg_ref, wu_ref, wd_ref, gf_ref, o_ref, *rest, emit_w16):
    xf_ref = rest[-1]
    f = pl.program_id(1)
    last = pl.num_programs(1) - 1

    def hidden_tile(xf):
        wg = wg_ref[...].astype(BF16)
        wu = wu_ref[...].astype(BF16)
        wd = wd_ref[...].astype(BF16)
        if emit_w16:
            for dst, val in zip(rest[:3], (wg, wu, wd)):
                dst[...] = val
        gate = jnp.dot(xf, wg, preferred_element_type=F32)
        up = jnp.dot(xf, wu, preferred_element_type=F32)
        hidden = (gate * jax.nn.sigmoid(gate)) * up
        return jnp.dot(hidden.astype(BF16), wd, preferred_element_type=F32)

    @pl.when(f == 0)
    def _():
        x = x_ref[...]
        xf = _rms(x, g_ref[...]).astype(BF16)
        xf_ref[...] = xf
        o_ref[...] = x + hidden_tile(xf)

    @pl.when((f > 0) & (f < last))
    def _():
        o_ref[...] += hidden_tile(xf_ref[...])

    @pl.when(f == last)
    def _():
        o_ref[...] = _rms(o_ref[...] + hidden_tile(xf_ref[...]), gf_ref[...])


def _ffn(x, g, w_gate, w_up, w_down, g_final, *, tm, tf, emit_w16=False, vmem_mib, name):
    m, d = x.shape
    dff = w_gate.shape[1]
    assert m % tm == 0 and dff % tf == 0 and dff // tf >= 2 and (not emit_w16 or m == tm)
    up_spec = pl.BlockSpec((d, tf), lambda i, f: (0, f))
    down_spec = pl.BlockSpec((tf, d), lambda i, f: (f, 0))
    out_specs = [pl.BlockSpec((tm, d), lambda i, f: (i, 0))]
    out_shape = [jax.ShapeDtypeStruct((m, d), F32)]
    if emit_w16:
        out_specs += [up_spec, up_spec, down_spec]
        out_shape += [jax.ShapeDtypeStruct(w.shape, BF16) for w in (w_gate, w_up, w_down)]
    out = pl.pallas_call(
        functools.partial(_ffn_kernel, emit_w16=emit_w16),
        grid=(m // tm, dff // tf),
        in_specs=[
            pl.BlockSpec((tm, d), lambda i, f: (i, 0)),
            pl.BlockSpec((1, d), lambda i, f: (0, 0)),
            up_spec,
            up_spec,
            down_spec,
            pl.BlockSpec((1, d), lambda i, f: (0, 0)),
        ],
        out_specs=out_specs,
        out_shape=out_shape,
        scratch_shapes=[pltpu.VMEM((tm, d), BF16)],
        compiler_params=_params(2, vmem_mib),
        name=name,
    )(x, g.reshape(1, d), w_gate, w_up, w_down, g_final.reshape(1, d))
    return out if emit_w16 else out[0]


def _rg_gates(xr, wgate_ref, ba_ref, bx_ref, lam_ref, a_ref, b_ref, before_block=None):
    for n in range(RG_BLOCKS):
        if before_block is not None:
            before_block(n)
        sl = slice(n * RG_BLOCK, (n + 1) * RG_BLOCK)
        xn = xr[:, sl]
        g = _mm(xn, wgate_ref[n])
        r = jax.nn.sigmoid(g[:, :RG_BLOCK] + ba_ref[:, sl])
        ig = jax.nn.sigmoid(g[:, RG_BLOCK:] + bx_ref[:, sl])
        log_a = -RG_C * r * _softplus(-lam_ref[:, sl])
        a_ref[:, sl] = jnp.exp(log_a)
        mult = jnp.sqrt(jnp.maximum(1.0 - jnp.exp(2.0 * log_a), 0.0))
        b_ref[:, sl] = mult * (ig * xn)


def _rglru_seq_kernel(*refs, tl, side_rows):
    (zx_ref, zg_ref, conv0_ref, h0_ref, cw_ref, cb_ref, wgate_ref, ba_ref, bx_ref, lam_ref,
     gout_ref) = refs[:11]
    xe_ref, a_ref, b_ref, h_ref, hc_ref = refs[-5:]
    if side_rows:
        sq_ref, sk_ref, sv_ref, y_ref, hlast_ref, convn_ref, so_ref = refs[11:-5]
    else:
        y_ref, hlast_ref, convn_ref = refs[11:-5]
    t = pl.program_id(1)
    pad = V7X_SUBLANES

    @pl.when(t == 0)
    def _():
        xe_ref[pad - 3:pad, :] = conv0_ref[...]
        hc_ref[...] = h0_ref[...]

    @pl.when(t > 0)
    def _():
        xe_ref[pad - 3:pad, :] = xe_ref[tl + pad - 3:tl + pad, :]

    x = zx_ref[...]
    xe_ref[pad:tl + pad, :] = x
    xr = (xe_ref[pad - 3:tl + pad - 3, :] * cw_ref[0:1, :]
          + xe_ref[pad - 2:tl + pad - 2, :] * cw_ref[1:2, :]
          + xe_ref[pad - 1:tl + pad - 1, :] * cw_ref[2:3, :]
          + x * cw_ref[3:4, :]) + cb_ref[...]
    side = None
    if side_rows:
        own = _xattn_own_mask()

        def side(n):
            for j in range(side_rows):
                if j * RG_BLOCKS // side_rows == n:
                    _xattn_one_row(sq_ref, sk_ref, sv_ref, so_ref, j, own)

    _rg_gates(xr, wgate_ref, ba_ref, bx_ref, lam_ref, a_ref, b_ref, before_block=side)

    row = lax.broadcasted_iota(jnp.int32, (V7X_SUBLANES, RG_WIDTH), 0)

    def group(gi, hc):
        r0 = pl.multiple_of(gi * V7X_SUBLANES, V7X_SUBLANES)
        a8 = a_ref[pl.ds(r0, V7X_SUBLANES), :]
        b8 = b_ref[pl.ds(r0, V7X_SUBLANES), :]
        for d in (1, 2, 4):
            keep = row >= d
            b8 = jnp.where(keep, a8 * pltpu.roll(b8, d, axis=0) + b8, b8)
            a8 = jnp.where(keep, a8 * pltpu.roll(a8, d, axis=0), a8)
        h8 = a8 * hc + b8
        h_ref[pl.ds(r0, V7X_SUBLANES), :] = h8
        return h8[V7X_SUBLANES - 1:V7X_SUBLANES, :]

    hc = lax.fori_loop(0, tl // V7X_SUBLANES, group, hc_ref[...], unroll=4)
    hc_ref[...] = hc
    hlast_ref[...] = hc
    convn_ref[...] = xe_ref[tl + pad - 3:tl + pad, :]
    y = h_ref[...] * _gelu_tanh(zg_ref[...])
    y_ref[...] = _rms(y, gout_ref[...]).astype(y_ref.dtype)


def _rglru_seq(z, conv0, h0, cw, cb, wgate, ba, bx, lam, gout, *, batch, seq, tl, name,
               side_xattn=None):
    nt = seq // tl
    assert seq % tl == 0
    w = RG_WIDTH
    row = lambda v: v.reshape(1, w)
    const2 = lambda b, t: (0, 0)
    in_specs = [
        pl.BlockSpec((tl, w), lambda b, t: (b * nt + t, OFF_RGX // w)),
        pl.BlockSpec((tl, w), lambda b, t: (b * nt + t, OFF_RGG // w)),
        pl.BlockSpec((None, CONV_W - 1, w), lambda b, t: (b, 0, 0)),
        pl.BlockSpec((None, 1, w), lambda b, t: (b, 0, 0)),
        pl.BlockSpec((CONV_W, w), const2),
        pl.BlockSpec((1, w), const2),
        pl.BlockSpec((RG_BLOCKS, RG_BLOCK, 2 * RG_BLOCK), lambda b, t: (0, 0, 0)),
        pl.BlockSpec((1, w), const2),
        pl.BlockSpec((1, w), const2),
        pl.BlockSpec((1, w), const2),
        pl.BlockSpec((1, w), const2),
    ]
    out_specs = [
        pl.BlockSpec((tl, w), lambda b, t: (b * nt + t, 0)),
        pl.BlockSpec((None, 1, w), lambda b, t: (b, 0, 0)),
        pl.BlockSpec((None, CONV_W - 1, w), lambda b, t: (b, 0, 0)),
    ]
    out_shape = [
        jax.ShapeDtypeStruct((batch * seq, w), BF16),
        jax.ShapeDtypeStruct((batch, 1, w), F32),
        jax.ShapeDtypeStruct((batch, CONV_W - 1, w), F32),
    ]
    args = [z, z, conv0, h0.reshape(batch, 1, w), cw, row(cb), wgate, row(ba), row(bx), row(lam),
            row(gout)]
    side_rows = 0
    vmem_mib = 32
    if side_xattn is not None:
        xq, ck, cv = side_xattn
        nb = xq.shape[0]
        assert nb % (batch * nt) == 0 and ck.shape == (nb, N_MEM, XA_HEADS, XA_DH)
        side_rows = nb // (batch * nt)
        q_spec = pl.BlockSpec((side_rows, XA_HEADS, XA_DH), lambda b, t: (b * nt + t, 0, 0))
        cache_spec = pl.BlockSpec((side_rows, N_MEM, XA_HEADS, XA_DH),
                                  lambda b, t: (b * nt + t, 0, 0, 0))
        in_specs += [q_spec, cache_spec, cache_spec]
        out_specs.append(q_spec)
        out_shape.append(jax.ShapeDtypeStruct((nb, XA_HEADS, XA_DH), F32))
        args += [xq.reshape(nb, XA_HEADS, XA_DH), ck, cv]
        vmem_mib = 56
    return pl.pallas_call(
        functools.partial(_rglru_seq_kernel, tl=tl, side_rows=side_rows),
        grid=(batch, nt),
        in_specs=in_specs,
        out_specs=out_specs,
        out_shape=out_shape,
        scratch_shapes=[
            pltpu.VMEM((tl + V7X_SUBLANES, w), F32),
            pltpu.VMEM((tl, w), F32),
            pltpu.VMEM((tl, w), F32),
            pltpu.VMEM((tl, w), F32),
            pltpu.VMEM((1, w), F32),
        ],
        compiler_params=_params(2, vmem_mib),
        name=name,
    )(*args)


def _rglru_step_kernel(zx_ref, zg_ref, conv_ref, h0_ref, cw_ref, cb_ref, wgate_ref, ba_ref,
                       bx_ref, lam_ref, gout_ref, y_ref, hn_ref, convn_ref, a_ref, b_ref):
    w = RG_WIDTH
    x = zx_ref[...]
    xr = (conv_ref[:, 0:w] * cw_ref[0:1, :] + conv_ref[:, w:2 * w] * cw_ref[1:2, :]
          + conv_ref[:, 2 * w:3 * w] * cw_ref[2:3, :] + x * cw_ref[3:4, :]) + cb_ref[...]
    _rg_gates(xr, wgate_ref, ba_ref, bx_ref, lam_ref, a_ref, b_ref)
    h = a_ref[...] * h0_ref[...] + b_ref[...]
    hn_ref[...] = h
    convn_ref[:, 0:2 * w] = conv_ref[:, w:3 * w]
    convn_ref[:, 2 * w:3 * w] = x
    y_ref[...] = _rms(h * _gelu_tanh(zg_ref[...]), gout_ref[...]).astype(y_ref.dtype)


def _rglru_step(z, conv, h0, cw, cb, wgate, ba, bx, lam, gout, *, name):
    nb = z.shape[0]
    w = RG_WIDTH
    row = lambda v: v.reshape(1, w)
    c0 = lambda i: (0, 0)
    return pl.pallas_call(
        _rglru_step_kernel,
        grid=(1,),
        in_specs=[
            pl.BlockSpec((nb, w), lambda i: (0, OFF_RGX // w)),
            pl.BlockSpec((nb, w), lambda i: (0, OFF_RGG // w)),
            pl.BlockSpec((nb, (CONV_W - 1) * w), c0),
            pl.BlockSpec((nb, w), c0),
            pl.BlockSpec((CONV_W, w), c0),
            pl.BlockSpec((1, w), c0),
            pl.BlockSpec((RG_BLOCKS, RG_BLOCK, 2 * RG_BLOCK), lambda i: (0, 0, 0)),
            pl.BlockSpec((1, w), c0),
            pl.BlockSpec((1, w), c0),
            pl.BlockSpec((1, w), c0),
            pl.BlockSpec((1, w), c0),
        ],
        out_specs=[
            pl.BlockSpec((nb, w), c0),
            pl.BlockSpec((nb, w), c0),
            pl.BlockSpec((nb, (CONV_W - 1) * w), c0),
        ],
        out_shape=[
            jax.ShapeDtypeStruct((nb, w), BF16),
            jax.ShapeDtypeStruct((nb, w), F32),
            jax.ShapeDtypeStruct((nb, (CONV_W - 1) * w), F32),
        ],
        scratch_shapes=[pltpu.VMEM((nb, w), F32), pltpu.VMEM((nb, w), F32)],
        compiler_params=_params(1, 32),
        name=name,
    )(z, z, conv.reshape(nb, (CONV_W - 1) * w), h0, cw, row(cb), wgate, row(ba), row(bx),
      row(lam), row(gout))


def _mlstm_seq_kernel(bi_ref, bf_ref, q_ref, k_ref, v_ref, o_ref, zg_ref, g_ref,
                      y_ref, c_ref, n_ref, m_ref, zgt_ref, cs_ref, ns_ref, *, seq, cs):
    h = pl.program_id(1)
    nc = seq // cs
    bi = bi_ref[h]
    bf = bf_ref[h]
    for c in range(nc):
        zgt_ref[c] = zg_ref[c * cs:(c + 1) * cs, :].T
    cs_ref[...] = jnp.zeros_like(cs_ref)
    ns_ref[...] = jnp.zeros_like(ns_ref)

    t_idx = lax.broadcasted_iota(jnp.int32, (cs, cs), 0)
    s_idx = lax.broadcasted_iota(jnp.int32, (cs, cs), 1)
    causal = s_idx <= t_idx
    lane = lax.broadcasted_iota(jnp.int32, (cs, V7X_LANES), 1)

    def chunk(c, m):
        r0 = pl.multiple_of(c * cs, cs)
        q = q_ref[pl.ds(r0, cs), :]
        k = k_ref[pl.ds(r0, cs), :] * (ML_DK ** -0.5)
        v = v_ref[pl.ds(r0, cs), :]
        zg = zg_ref[pl.ds(r0, cs), :]
        li_col = jnp.sum(jnp.where(lane == h, zg, 0.0), axis=1, keepdims=True) + bi
        lf_col = _log_sigmoid(
            jnp.sum(jnp.where(lane == h + ML_HEADS, zg, 0.0), axis=1, keepdims=True) + bf)
        li_row = zgt_ref[c, pl.ds(h, 1), :] + bi
        lf_row = _log_sigmoid(zgt_ref[c, pl.ds(h + ML_HEADS, 1), :] + bf)
        bcum_col = jnp.sum(jnp.where(causal, lf_row, 0.0), axis=1, keepdims=True)
        bcum_row = jnp.sum(jnp.where(t_idx <= s_idx, lf_col, 0.0), axis=0, keepdims=True)
        log_d = jnp.where(causal, bcum_col - bcum_row + li_row, NEG)
        inter = bcum_col + m
        m_t = jnp.maximum(inter, jnp.max(log_d, axis=1, keepdims=True))
        dmat = jnp.exp(log_d - m_t)
        sc = jnp.exp(inter - m_t)
        qb = q.astype(BF16)
        kb = k.astype(BF16)
        vb = v.astype(BF16)
        qk = lax.dot_general(qb, kb, (((1,), (1,)), ((), ())), preferred_element_type=F32) * dmat
        c_old = cs_ref[...]
        n_old = ns_ref[...]
        num = sc * jnp.dot(qb, c_old.astype(BF16), preferred_element_type=F32) + _mm(qk, vb)
        den = sc * jnp.sum(q * n_old, axis=1, keepdims=True) + jnp.sum(qk, axis=1, keepdims=True)
        den = jnp.maximum(jnp.abs(den), jnp.exp(-m_t))
        hh = num / den
        m_new = m_t[cs - 1:cs, :]
        b_last = bcum_col[cs - 1:cs, :]
        w_end = jnp.exp(b_last - bcum_col + li_col - m_new)
        dec = jnp.exp(b_last + m - m_new)
        wk = w_end * k
        cs_ref[...] = dec * c_old + lax.dot_general(
            wk.astype(BF16), vb, (((0,), (0,)), ((), ())), preferred_element_type=F32)
        ns_ref[...] = dec * n_old + jnp.sum(wk, axis=0, keepdims=True)
        y = _rms(hh, g_ref[...]) * jax.nn.sigmoid(o_ref[pl.ds(r0, cs), :])
        y_ref[pl.ds(r0, cs), :] = y.astype(y_ref.dtype)
        return m_new

    m_fin = lax.fori_loop(0, nc, chunk, jnp.zeros((1, 1), F32))
    c_ref[...] = cs_ref[...]
    n_ref[pl.ds(h, 1), :] = ns_ref[...]
    m_ref[pl.ds(h, 1), :] = jnp.broadcast_to(m_fin, (1, V7X_LANES))


def _mlstm_seq(z, zg, b_i, b_f, g_out, *, batch, seq, cs, name):
    assert seq % cs == 0 and zg.shape[1] == V7X_LANES
    dk, dv, nh = ML_DK, ML_DV, ML_HEADS
    smem = pl.BlockSpec(memory_space=pltpu.SMEM)
    return pl.pallas_call(
        functools.partial(_mlstm_seq_kernel, seq=seq, cs=cs),
        grid=(batch, nh),
        in_specs=[
            smem, smem,
            pl.BlockSpec((seq, dk), lambda b, h: (b, OFF_Q // dk + h)),
            pl.BlockSpec((seq, dk), lambda b, h: (b, OFF_K // dk + h)),
            pl.BlockSpec((seq, dv), lambda b, h: (b, OFF_V // dv + h)),
            pl.BlockSpec((seq, dv), lambda b, h: (b, OFF_O // dv + h)),
            pl.BlockSpec((seq, V7X_LANES), lambda b, h: (b, 0)),
            pl.BlockSpec((1, dv), lambda b, h: (0, h)),
        ],
        out_specs=[
            pl.BlockSpec((seq, dv), lambda b, h: (b, h)),
            pl.BlockSpec((None, None, dk, dv), lambda b, h: (b, h, 0, 0)),
            pl.BlockSpec((None, nh, dk), lambda b, h: (b, 0, 0)),
            pl.BlockSpec((None, nh, V7X_LANES), lambda b, h: (b, 0, 0)),
        ],
        out_shape=[
            jax.ShapeDtypeStruct((batch * seq, nh * dv), BF16),
            jax.ShapeDtypeStruct((batch, nh, dk, dv), F32),
            jax.ShapeDtypeStruct((batch, nh, dk), F32),
            jax.ShapeDtypeStruct((batch, nh, V7X_LANES), F32),
        ],
        scratch_shapes=[
            pltpu.VMEM((seq // cs, V7X_LANES, cs), F32),
            pltpu.VMEM((dk, dv), F32),
            pltpu.VMEM((1, dk), F32),
        ],
        compiler_params=_params(2, 40),
        name=name,
    )(b_i, b_f, z, z, z, z, zg, g_out.reshape(1, nh * dv))


def _mlstm_step_kernel(bi_ref, bf_ref, q_ref, k_ref, v_ref, o_ref, zg_ref, g_ref, c0_ref, n0_ref,
                       m0_ref, y_ref, c_ref, n_ref, m_ref, qc_ref, *, bs):
    dk, dv = ML_DK, ML_DV
    eye = (lax.broadcasted_iota(jnp.int32, (dk, dk), 0)
           == lax.broadcasted_iota(jnp.int32, (dk, dk), 1))
    zg = zg_ref[...]
    for h in range(ML_HEADS):
        li = zg[:, h:h + 1] + bi_ref[h]
        lf = _log_sigmoid(zg[:, ML_HEADS + h:ML_HEADS + h + 1] + bf_ref[h])
        m = m0_ref[:, h:h + 1]
        inter = lf + m
        m_t = jnp.maximum(inter, li)
        dgate = jnp.exp(li - m_t)
        sc = jnp.exp(inter - m_t)
        q = q_ref[:, h * dk:(h + 1) * dk]
        k = k_ref[:, h * dk:(h + 1) * dk] * (ML_DK ** -0.5)
        v = v_ref[:, h * dv:(h + 1) * dv]
        n_old = n0_ref[:, h, :]
        qk = jnp.sum(q * k, axis=1, keepdims=True) * dgate
        w_end = jnp.exp(li - m_t)
        dec = jnp.exp(inter - m_t)
        wk = w_end * k
        for j in range(bs):
            c_old = c0_ref[j, h]
            qc_ref[j:j + 1, :] = _mm(q[j:j + 1, :], c_old)
            wk_col = jnp.sum(
                jnp.where(eye, jnp.broadcast_to(wk[j:j + 1, :], (dk, dk)), 0.0),
                axis=1, keepdims=True)
            c_ref[j, h] = dec[j:j + 1, :] * c_old + wk_col * v[j:j + 1, :]
        num = sc * qc_ref[...] + qk * v
        den = sc * jnp.sum(q * n_old, axis=1, keepdims=True) + qk
        den = jnp.maximum(jnp.abs(den), jnp.exp(-m_t))
        hh = num / den
        n_ref[:, h, :] = dec * n_old + wk
        m_ref[:, h:h + 1] = m_t
        y = _rms(hh, g_ref[:, h * dv:(h + 1) * dv]) * jax.nn.sigmoid(o_ref[:, h * dv:(h + 1) * dv])
        y_ref[:, h * dv:(h + 1) * dv] = y.astype(y_ref.dtype)


def _mlstm_step(z, zg, c0, n0, m0, b_i, b_f, g_out, *, bs, name):
    nb = z.shape[0]
    assert nb % bs == 0
    dk, dv, nh = ML_DK, ML_DV, ML_HEADS
    smem = pl.BlockSpec(memory_space=pltpu.SMEM)
    return pl.pallas_call(
        functools.partial(_mlstm_step_kernel, bs=bs),
        grid=(nb // bs,),
        in_specs=[
            smem, smem,
            pl.BlockSpec((bs, nh * dk), lambda i: (i, OFF_Q // (nh * dk))),
            pl.BlockSpec((bs, nh * dk), lambda i: (i, OFF_K // (nh * dk))),
            pl.BlockSpec((bs, nh * dv), lambda i: (i, OFF_V // (nh * dv))),
            pl.BlockSpec((bs, nh * dv), lambda i: (i, OFF_O // (nh * dv))),
            pl.BlockSpec((bs, V7X_LANES), lambda i: (i, 0)),
            pl.BlockSpec((1, nh * dv), lambda i: (0, 0)),
            pl.BlockSpec((bs, nh, dk, dv), lambda i: (i, 0, 0, 0)),
            pl.BlockSpec((bs, nh, dk), lambda i: (i, 0, 0)),
            pl.BlockSpec((bs, nh), lambda i: (i, 0)),
        ],
        out_specs=[
            pl.BlockSpec((bs, nh * dv), lambda i: (i, 0)),
            pl.BlockSpec((bs, nh, dk, dv), lambda i: (i, 0, 0, 0)),
            pl.BlockSpec((bs, nh, dk), lambda i: (i, 0, 0)),
            pl.BlockSpec((bs, nh), lambda i: (i, 0)),
        ],
        out_shape=[
            jax.ShapeDtypeStruct((nb, nh * dv), F32),
            jax.ShapeDtypeStruct((nb, nh, dk, dv), F32),
            jax.ShapeDtypeStruct((nb, nh, dk), F32),
            jax.ShapeDtypeStruct((nb, nh), F32),
        ],
        scratch_shapes=[pltpu.VMEM((bs, dv), F32)],
        compiler_params=_params(1, 32),
        name=name,
    )(b_i, b_f, z, z, z, z, zg, g_out.reshape(1, nh * dv), c0, n0, m0)


def _softmax_rows(s):
    e = jnp.exp(s - jnp.max(s, axis=-1, keepdims=True))
    return e / jnp.sum(e, axis=-1, keepdims=True)


def _xattn_seq_kernel(q_ref, k_ref, v_ref, o_ref):
    for h in range(XA_HEADS):
        sl = slice(h * XA_DH, (h + 1) * XA_DH)
        s = lax.dot_general(q_ref[:, sl].astype(BF16), k_ref[:, sl].astype(BF16),
                            (((1,), (1,)), ((), ())), preferred_element_type=F32)
        p = _softmax_rows(s * (XA_DH ** -0.5))
        o_ref[:, sl] = _mm(p, v_ref[:, sl]).astype(o_ref.dtype)


def _xattn_seq(xq, mk, mv, *, batch, seq, tq, name):
    nt = seq // tq
    assert seq % tq == 0
    d = D_MODEL
    return pl.pallas_call(
        _xattn_seq_kernel,
        grid=(batch, nt),
        in_specs=[
            pl.BlockSpec((tq, d), lambda b, t: (b * nt + t, 0)),
            pl.BlockSpec((N_MEM, d), lambda b, t: (b, 0)),
            pl.BlockSpec((N_MEM, d), lambda b, t: (b, 0)),
        ],
        out_specs=pl.BlockSpec((tq, d), lambda b, t: (b * nt + t, 0)),
        out_shape=jax.ShapeDtypeStruct((batch * seq, d), BF16),
        compiler_params=_params(2, 40),
        name=name,
    )(xq, mk, mv)


def _xattn_own_mask():
    nrow = N_MEM * XA_HEADS
    col_head = lax.broadcasted_iota(jnp.int32, (V7X_SUBLANES, nrow), 1) & (XA_HEADS - 1)
    row_head = lax.broadcasted_iota(jnp.int32, (V7X_SUBLANES, nrow), 0) & (XA_HEADS - 1)
    return col_head == row_head


def _xattn_one_row(q_ref, k_ref, v_ref, o_ref, j, own):
    nrow = N_MEM * XA_HEADS
    kf = k_ref[j].reshape(nrow, XA_DH)
    vf = v_ref[j].reshape(nrow, XA_DH)
    q8 = jnp.concatenate([q_ref[j]] * (V7X_SUBLANES // XA_HEADS), axis=0)
    s = lax.dot_general(q8.astype(BF16), kf.astype(BF16), (((1,), (1,)), ((), ())),
                        preferred_element_type=F32)
    s = jnp.where(own, s * (XA_DH ** -0.5), NEG)
    e = jnp.where(own, jnp.exp(s - jnp.max(s, axis=-1, keepdims=True)), 0.0)
    p = e / jnp.sum(e, axis=-1, keepdims=True)
    o_ref[j] = _mm(p, vf)[0:XA_HEADS, :]


def _xattn_step_kernel(q_ref, k_ref, v_ref, o_ref, *, sb):
    own = _xattn_own_mask()
    for j in range(sb):
        _xattn_one_row(q_ref, k_ref, v_ref, o_ref, j, own)


def _xattn_step(xq, ck, cv, *, sb, name):
    nb = xq.shape[0]
    assert XA_HEADS & (XA_HEADS - 1) == 0
    assert nb % sb == 0 and ck.shape == (nb, N_MEM, XA_HEADS, XA_DH)
    cache_spec = pl.BlockSpec((sb, N_MEM, XA_HEADS, XA_DH), lambda i: (i, 0, 0, 0))
    q_spec = pl.BlockSpec((sb, XA_HEADS, XA_DH), lambda i: (i, 0, 0))
    return pl.pallas_call(
        functools.partial(_xattn_step_kernel, sb=sb),
        grid=(nb // sb,),
        in_specs=[q_spec, cache_spec, cache_spec],
        out_specs=q_spec,
        out_shape=jax.ShapeDtypeStruct((nb, XA_HEADS, XA_DH), F32),
        compiler_params=_params(1, 40),
        name=name,
    )(xq.reshape(nb, XA_HEADS, XA_DH), ck, cv).reshape(nb, XA_HEADS * XA_DH)


def _tiles(rows):
    tm = min(rows, 1024)
    assert rows % tm == 0
    return tm


def kernel(x_prompt, x_sample, mem_prompt, state_rg_h, state_rg_conv, state_ml_C, state_ml_n, state_ml_m, cache_mem_k, cache_mem_v, g_mix, w_in, conv_w, conv_b, w_rg_a, b_rg_a, w_rg_x, b_rg_x, rg_lambda, b_ml_i, b_ml_f, g_rg_out, g_ml_out, w_out, g_xa, g_mem, w_xa_q, w_xa_k, w_xa_v, w_xa_o, g_ffn, w_ffn_gate, w_ffn_up, w_ffn_down, g_final):
    depth = g_mix.shape[0]
    assert depth == 1, "single trunk layer"
    bp, seq, d = x_prompt.shape
    bs_, dec_seq, _ = x_sample.shape
    assert d == D_MODEL and dec_seq == 1
    n_mem = mem_prompt.shape[1]
    assert n_mem == N_MEM
    dff = w_ffn_gate.shape[-1]
    in_w = w_in.shape[-1]
    assert in_w == IN_MAIN + N_GATE

    w_in_t = jnp.swapaxes(w_in, 1, 2).reshape(in_w, d)
    w_gate_pad = jnp.pad(w_in_t[IN_MAIN:], ((0, V7X_LANES - N_GATE), (0, 0)))
    cw = conv_w.reshape(CONV_W, RG_WIDTH)
    wgate = jnp.concatenate([w_rg_a.reshape(RG_BLOCKS, RG_BLOCK, RG_BLOCK),
                             w_rg_x.reshape(RG_BLOCKS, RG_BLOCK, RG_BLOCK)], axis=-1)
    rg_args = (cw, conv_b.reshape(-1), wgate, b_rg_a.reshape(-1), b_rg_x.reshape(-1),
               rg_lambda.reshape(-1), g_rg_out.reshape(-1))
    b_i = b_ml_i.reshape(ML_HEADS)
    b_f = b_ml_f.reshape(ML_HEADS)
    g_ml = g_ml_out.reshape(-1)
    w_out2 = w_out.reshape(d, d)
    w_q = w_xa_q.reshape(d, d)
    w_k = w_xa_k.reshape(d, d)
    w_v = w_xa_v.reshape(d, d)
    w_o = w_xa_o.reshape(d, d)
    w_fg = w_ffn_gate.reshape(d, dff)
    w_fu = w_ffn_up.reshape(d, dff)
    w_fd = w_ffn_down.reshape(dff, d)

    row_tile = _tiles(bp * seq)
    rg_zero_conv = jnp.zeros((bp, CONV_W - 1, RG_WIDTH), F32)
    rg_zero_h = jnp.zeros((bp, RG_WIDTH), F32)

    xs = x_sample.reshape(bs_, d)
    z_s, zg_s, w_in16 = _norm_linear(
        xs, g_mix.reshape(-1), w_in_t, n_out=IN_MAIN, tm=bs_, tn=1024, w_gate=w_gate_pad,
        w_is_nk=True, emit_w16=True, name="in_proj_s")
    y_rg_s, s_h, s_conv = _rglru_step(
        z_s, state_rg_conv.reshape(bs_, CONV_W - 1, RG_WIDTH), state_rg_h.reshape(bs_, RG_WIDTH),
        *rg_args, name="rglru_s")
    y_ml_s, s_c, s_n, s_m = _mlstm_step(
        z_s, zg_s, state_ml_C.reshape(bs_, ML_HEADS, ML_DK, ML_DV),
        state_ml_n.reshape(bs_, ML_HEADS, ML_DK), state_ml_m.reshape(bs_, ML_HEADS),
        b_i, b_f, g_ml, bs=V7X_SUBLANES, name="mlstm_s")
    x1_s, w_out16_rg, w_out16_ml = _linear_residual(
        [y_rg_s, y_ml_s], [(w_out2, 0), (w_out2, 1)], xs, tm=bs_, tn=1024, emit_w16=True,
        name="mix_out_s")
    xq_s, w_q16 = _norm_linear(x1_s, g_xa.reshape(-1), w_q, n_out=d, tm=bs_, tn=1024,
                               emit_w16=True, name="xa_q_s")
    ck = cache_mem_k.reshape(bs_, n_mem, XA_HEADS, XA_DH)
    cv = cache_mem_v.reshape(bs_, n_mem, XA_HEADS, XA_DH)

    tp = bp * seq
    xp = x_prompt.reshape(tp, d)
    z_p, zg_p = _norm_linear(xp, g_mix.reshape(-1), w_in16, n_out=IN_MAIN, tm=row_tile, tn=1024,
                             w_gate=w_gate_pad, w_is_nk=True, name="in_proj_p")
    rg_tl = min(seq, 256)
    if bs_ % (bp * (seq // rg_tl)) == 0:
        y_rg_p, p_h, p_conv, o_s = _rglru_seq(
            z_p, rg_zero_conv, rg_zero_h, *rg_args, batch=bp, seq=seq, tl=rg_tl,
            side_xattn=(xq_s, ck, cv), name="rglru_p")
        o_s = o_s.reshape(bs_, d)
    else:
        y_rg_p, p_h, p_conv = _rglru_seq(z_p, rg_zero_conv, rg_zero_h, *rg_args, batch=bp,
                                         seq=seq, tl=rg_tl, name="rglru_p")
        o_s = _xattn_step(xq_s, ck, cv, sb=2, name="xattn_s")

    x3_s, w_o16 = _linear_residual([o_s], [(w_o, 0)], x1_s, tm=bs_, tn=1024, emit_w16=True,
                                   name="xa_out_s")
    y_s, w_fg16, w_fu16, w_fd16 = _ffn(x3_s, g_ffn.reshape(-1), w_fg, w_fu, w_fd, g_final,
                                       tm=bs_, tf=512, emit_w16=True, vmem_mib=48, name="ffn_s")

    y_ml_p, p_c, p_n, p_m = _mlstm_seq(z_p, zg_p, b_i, b_f, g_ml, batch=bp, seq=seq,
                                       cs=min(seq, ML_CHUNK), name="mlstm_p")
    mem2 = mem_prompt.reshape(bp * n_mem, d)
    tmem = _tiles(bp * n_mem)
    mk = _norm_linear(mem2, g_mem.reshape(-1), w_k, n_out=d, tm=tmem, tn=1024, name="mem_k")
    mv = _norm_linear(mem2, g_mem.reshape(-1), w_v, n_out=d, tm=tmem, tn=1024, name="mem_v")
    proj_tile = min(row_tile, 512)
    x1_p = _linear_residual([y_rg_p, y_ml_p], [(w_out16_rg, 0), (w_out16_ml, 0)], xp,
                            tm=proj_tile, tn=d, name="mix_out_p")
    xq_p = _norm_linear(x1_p, g_xa.reshape(-1), w_q16, n_out=d, tm=proj_tile, tn=d,
                        out_dtype=BF16, name="xa_q_p")
    o_p = _xattn_seq(xq_p, mk, mv, batch=bp, seq=seq, tq=min(seq, 512), name="xattn_p")
    x3_p = _linear_residual([o_p], [(w_o16, 0)], x1_p, tm=proj_tile, tn=d, name="xa_out_p")
    y_p = _ffn(x3_p, g_ffn.reshape(-1), w_fg16, w_fu16, w_fd16, g_final, tm=row_tile, tf=512,
               vmem_mib=60, name="ffn_p")

    return (
        y_p.reshape(bp, seq, d),
        y_s.reshape(bs_, 1, d),
        p_h.reshape(1, bp, RG_WIDTH),
        p_conv.reshape(1, bp, CONV_W - 1, RG_WIDTH),
        p_c.reshape(1, bp, ML_HEADS, ML_DK, ML_DV),
        p_n.reshape(1, bp, ML_HEADS, ML_DK),
        p_m[:, :, 0].reshape(1, bp, ML_HEADS),
        mk.reshape(1, bp, n_mem, XA_HEADS, XA_DH),
        mv.reshape(1, bp, n_mem, XA_HEADS, XA_DH),
        s_h.reshape(1, bs_, RG_WIDTH),
        s_conv.reshape(1, bs_, CONV_W - 1, RG_WIDTH),
        s_c.reshape(1, bs_, ML_HEADS, ML_DK, ML_DV),
        s_n.reshape(1, bs_, ML_HEADS, ML_DK),
        s_m.reshape(1, bs_, ML_HEADS),
    )
```

```python
import functools

import jax
import jax.numpy as jnp
from jax import lax
from jax.experimental import pallas as pl
from jax.experimental.pallas import tpu as pltpu

F32 = jnp.float32
BF16 = jnp.bfloat16

D_MODEL = 2048
RG_WIDTH = D_MODEL // 2
RG_BLOCKS = 8
RG_BLOCK = RG_WIDTH // RG_BLOCKS
CONV_W = 4
RG_C = 8.0
ML_HEADS = 4
ML_WIDTH = D_MODEL - RG_WIDTH
ML_DV = ML_WIDTH // ML_HEADS
ML_DK = ML_DV // 2
N_MEM = 256
XA_HEADS = 4
XA_DH = D_MODEL // XA_HEADS
EPS = 1e-6
NEG = -1e30

OFF_RGX = 0
OFF_RGG = OFF_RGX + RG_WIDTH
OFF_Q = OFF_RGG + RG_WIDTH
OFF_K = OFF_Q + ML_HEADS * ML_DK
OFF_V = OFF_K + ML_HEADS * ML_DK
OFF_O = OFF_V + ML_WIDTH
OFF_I = OFF_O + ML_WIDTH
IN_MAIN = OFF_I
N_GATE = 2 * ML_HEADS

V7X_LANES = 128
V7X_SUBLANES = 8
V7X_VMEM_BYTES = 64 * 2**20

ML_CHUNK = 256


def _params(n_axes, vmem_mib):
    assert vmem_mib * 2**20 <= V7X_VMEM_BYTES
    return pltpu.CompilerParams(
        dimension_semantics=("arbitrary",) * n_axes,
        vmem_limit_bytes=vmem_mib * 2**20,
    )


def _rms(x, g):
    ms = jnp.mean(x * x, axis=-1, keepdims=True)
    return x * lax.rsqrt(ms + EPS) * g


def _softplus(u):
    return jnp.maximum(u, 0.0) + jnp.log1p(jnp.exp(-jnp.abs(u)))


def _log_sigmoid(u):
    return -_softplus(-u)


def _gelu_tanh(x):
    return x * (0.5 * (1.0 + jnp.tanh(0.7978845608028654 * (x + 0.044715 * (x * x * x)))))


def _mm(a, b):
    return jnp.dot(a.astype(BF16), b.astype(BF16), preferred_element_type=F32)


def _dot_w(a, w_ref, w_is_nk):
    w = w_ref[...].astype(BF16)
    if w_is_nk:
        return lax.dot_general(a, w, (((1,), (1,)), ((), ())), preferred_element_type=F32)
    return jnp.dot(a, w, preferred_element_type=F32)


def _norm_linear_kernel(*refs, with_gate, w_is_nk, emit_w16, heads_out):
    refs = list(refs)
    x_ref, g_ref, w_ref = refs[:3]
    wg_ref = refs[3] if with_gate else None
    outs = refs[3 + with_gate:-1]
    xn_ref = refs[-1]
    o_ref = outs[0]
    oh_ref = outs[1] if heads_out else None
    og_ref = outs[1 + bool(heads_out)] if with_gate else None

    def column_tile(xn):
        acc = _dot_w(xn, w_ref, w_is_nk)
        o_ref[...] = acc.astype(o_ref.dtype)
        if heads_out:
            oh_ref[...] = acc.reshape(oh_ref.shape)
        if emit_w16:
            outs[-1][...] = w_ref[...].astype(BF16)

    @pl.when(pl.program_id(1) == 0)
    def _():
        xn = _rms(x_ref[...], g_ref[...]).astype(BF16)
        xn_ref[...] = xn
        if with_gate:
            og_ref[...] = _dot_w(xn, wg_ref, w_is_nk)
        column_tile(xn)

    @pl.when(pl.program_id(1) > 0)
    def _():
        column_tile(xn_ref[...])


def _norm_linear(x, g, w, *, n_out, tm, tn, out_dtype=F32, w_gate=None, w_is_nk=False,
                 emit_w16=False, heads_out=None, vmem_mib=48, name):
    m, k = x.shape
    k_ax, n_ax = (1, 0) if w_is_nk else (0, 1)
    assert m % tm == 0 and n_out % tn == 0 and w.shape[k_ax] == k and n_out <= w.shape[n_ax]
    assert not emit_w16 or m == tm
    assert heads_out is None or (tn == n_out and heads_out[0] * heads_out[1] == n_out)
    with_gate = w_gate is not None
    w_mode = dict(pipeline_mode=pl.Buffered(1)) if tn == n_out else {}
    w_spec = (pl.BlockSpec((tn, k), lambda i, j: (j, 0), **w_mode) if w_is_nk
              else pl.BlockSpec((k, tn), lambda i, j: (0, j), **w_mode))
    in_specs = [
        pl.BlockSpec((tm, k), lambda i, j: (i, 0)),
        pl.BlockSpec((1, k), lambda i, j: (0, 0)),
        w_spec,
    ]
    out_shape = [jax.ShapeDtypeStruct((m, n_out), out_dtype)]
    out_specs = [pl.BlockSpec((tm, tn), lambda i, j: (i, j))]
    args = [x, g.reshape(1, k), w]
    if heads_out:
        out_shape.append(jax.ShapeDtypeStruct((m, *heads_out), F32))
        out_specs.append(pl.BlockSpec((tm, *heads_out), lambda i, j: (i, 0, 0)))
    if with_gate:
        ng = w_gate.shape[n_ax]
        in_specs.append(pl.BlockSpec(w_gate.shape, lambda i, j: (0, 0)))
        out_shape.append(jax.ShapeDtypeStruct((m, ng), F32))
        out_specs.append(pl.BlockSpec((tm, ng), lambda i, j: (i, 0)))
        args.append(w_gate)
    if emit_w16:
        out_shape.append(jax.ShapeDtypeStruct((n_out, k) if w_is_nk else (k, n_out), BF16))
        out_specs.append(pl.BlockSpec((tn, k), lambda i, j: (j, 0)) if w_is_nk
                         else pl.BlockSpec((k, tn), lambda i, j: (0, j)))
    out = pl.pallas_call(
        functools.partial(_norm_linear_kernel, with_gate=with_gate, w_is_nk=w_is_nk,
                          emit_w16=emit_w16, heads_out=heads_out),
        grid=(m // tm, n_out // tn),
        in_specs=in_specs,
        out_specs=out_specs,
        out_shape=out_shape,
        scratch_shapes=[pltpu.VMEM((tm, k), BF16)],
        compiler_params=_params(2, vmem_mib),
        name=name,
    )(*args)
    return out if len(out) > 1 else out[0]


def _linear_res_kernel(*refs, n_in, emit_w16):
    a_refs = refs[:n_in]
    w_refs = refs[n_in:2 * n_in]
    res_ref = refs[2 * n_in]
    o_ref = refs[2 * n_in + 1]
    acc = res_ref[...]
    for a_ref, w_ref in zip(a_refs, w_refs):
        acc = acc + _mm(a_ref[...], w_ref[...])
    o_ref[...] = acc
    if emit_w16:
        for w_ref, wc_ref in zip(w_refs, refs[2 * n_in + 2:]):
            wc_ref[...] = w_ref[...].astype(BF16)


def _linear_residual(parts, weights, res, *, tm, tn, emit_w16=False, vmem_mib=48, name):
    m, n = res.shape
    kp = parts[0].shape[1]
    assert all(p.shape == (m, kp) for p in parts) and len(weights) == len(parts)
    assert m % tm == 0 and n % tn == 0 and (not emit_w16 or m == tm)
    n_in = len(parts)
    in_specs = [pl.BlockSpec((tm, kp), lambda i, j: (i, 0)) for _ in parts]
    w_mode = dict(pipeline_mode=pl.Buffered(1)) if tn == n else {}
    in_specs += [pl.BlockSpec((kp, tn), lambda i, j, rb=rb: (rb, j), **w_mode)
                 for _, rb in weights]
    in_specs.append(pl.BlockSpec((tm, tn), lambda i, j: (i, j)))
    out_specs = [pl.BlockSpec((tm, tn), lambda i, j: (i, j))]
    out_shape = [jax.ShapeDtypeStruct((m, n), F32)]
    if emit_w16:
        out_specs += [pl.BlockSpec((kp, tn), lambda i, j: (0, j)) for _ in parts]
        out_shape += [jax.ShapeDtypeStruct((kp, n), BF16) for _ in parts]
    out = pl.pallas_call(
        functools.partial(_linear_res_kernel, n_in=n_in, emit_w16=emit_w16),
        grid=(m // tm, n // tn),
        in_specs=in_specs,
        out_specs=out_specs,
        out_shape=out_shape,
        compiler_params=_params(2, vmem_mib),
        name=name,
    )(*parts, *[w for w, _ in weights], res)
    return out if emit_w16 else out[0]


def _ffn_kernel(x_ref, g_ref, wg_ref, wu_ref, wd_ref, gf_ref, o_ref, *rest, emit_w16):
    xf_ref = rest[-1]
    f = pl.program_id(1)
    last = pl.num_programs(1) - 1

    def hidden_tile(xf):
        wg = wg_ref[...].astype(BF16)
        wu = wu_ref[...].astype(BF16)
        wd = wd_ref[...].astype(BF16)
        if emit_w16:
            for dst, val in zip(rest[:3], (wg, wu, wd)):
                dst[...] = val
        gate = jnp.dot(xf, wg, preferred_element_type=F32)
        up = jnp.dot(xf, wu, preferred_element_type=F32)
        hidden = (gate * jax.nn.sigmoid(gate)) * up
        return jnp.dot(hidden.astype(BF16), wd, preferred_element_type=F32)

    @pl.when(f == 0)
    def _():
        x = x_ref[...]
        xf = _rms(x, g_ref[...]).astype(BF16)
        xf_ref[...] = xf
        o_ref[...] = x + hidden_tile(xf)

    @pl.when((f > 0) & (f < last))
    def _():
        o_ref[...] += hidden_tile(xf_ref[...])

    @pl.when(f == last)
    def _():
        o_ref[...] = _rms(o_ref[...] + hidden_tile(xf_ref[...]), gf_ref[...])


def _ffn(x, g, w_gate, w_up, w_down, g_final, *, tm, tf, emit_w16=False, vmem_mib, name):
    m, d = x.shape
    dff = w_gate.shape[1]
    assert m % tm == 0 and dff % tf == 0 and dff // tf >= 2 and (not emit_w16 or m == tm)
    up_spec = pl.BlockSpec((d, tf), lambda i, f: (0, f))
    down_spec = pl.BlockSpec((tf, d), lambda i, f: (f, 0))
    out_specs = [pl.BlockSpec((tm, d), lambda i, f: (i, 0))]
    out_shape = [jax.ShapeDtypeStruct((m, d), F32)]
    if emit_w16:
        out_specs += [up_spec, up_spec, down_spec]
        out_shape += [jax.ShapeDtypeStruct(w.shape, BF16) for w in (w_gate, w_up, w_down)]
    out = pl.pallas_call(
        functools.partial(_ffn_kernel, emit_w16=emit_w16),
        grid=(m // tm, dff // tf),
        in_specs=[
            pl.BlockSpec((tm, d), lambda i, f: (i, 0)),
            pl.BlockSpec((1, d), lambda i, f: (0, 0)),
            up_spec,
            up_spec,
            down_spec,
            pl.BlockSpec((1, d), lambda i, f: (0, 0)),
        ],
        out_specs=out_specs,
        out_shape=out_shape,
        scratch_shapes=[pltpu.VMEM((tm, d), BF16)],
        compiler_params=_params(2, vmem_mib),
        name=name,
    )(x, g.reshape(1, d), w_gate, w_up, w_down, g_final.reshape(1, d))
    return out if emit_w16 else out[0]


def _rg_gates(xr, wgate_ref, ba_ref, bx_ref, lam_ref, a_ref, b_ref, before_block=None):
    for n in range(RG_BLOCKS):
        if before_block is not None:
            before_block(n)
        sl = slice(n * RG_BLOCK, (n + 1) * RG_BLOCK)
        xn = xr[:, sl]
        g = _mm(xn, wgate_ref[n])
        r = jax.nn.sigmoid(g[:, :RG_BLOCK] + ba_ref[:, sl])
        ig = jax.nn.sigmoid(g[:, RG_BLOCK:] + bx_ref[:, sl])
        log_a = -RG_C * r * _softplus(-lam_ref[:, sl])
        a_ref[:, sl] = jnp.exp(log_a)
        mult = jnp.sqrt(jnp.maximum(1.0 - jnp.exp(2.0 * log_a), 0.0))
        b_ref[:, sl] = mult * (ig * xn)


def _rglru_seq_kernel(*refs, tl, side_rows):
    (zx_ref, zg_ref, conv0_ref, h0_ref, cw_ref, cb_ref, wgate_ref, ba_ref, bx_ref, lam_ref,
     gout_ref) = refs[:11]
    xe_ref, a_ref, b_ref, h_ref, hc_ref = refs[-5:]
    if side_rows:
        sq_ref, sk_ref, sv_ref, y_ref, hlast_ref, convn_ref, so_ref = refs[11:-5]
    else:
        y_ref, hlast_ref, convn_ref = refs[11:-5]
    t = pl.program_id(1)
    pad = V7X_SUBLANES

    @pl.when(t == 0)
    def _():
        xe_ref[pad - 3:pad, :] = conv0_ref[...]
        hc_ref[...] = h0_ref[...]

    @pl.when(t > 0)
    def _():
        xe_ref[pad - 3:pad, :] = xe_ref[tl + pad - 3:tl + pad, :]

    x = zx_ref[...]
    xe_ref[pad:tl + pad, :] = x
    xr = (xe_ref[pad - 3:tl + pad - 3, :] * cw_ref[0:1, :]
          + xe_ref[pad - 2:tl + pad - 2, :] * cw_ref[1:2, :]
          + xe_ref[pad - 1:tl + pad - 1, :] * cw_ref[2:3, :]
          + x * cw_ref[3:4, :]) + cb_ref[...]
    side = None
    if side_rows:
        own = _xattn_own_mask()

        def side(n):
            for j in range(side_rows):
                if j * RG_BLOCKS // side_rows == n:
                    _xattn_one_row(sq_ref, sk_ref, sv_ref, so_ref, j, own)

    _rg_gates(xr, wgate_ref, ba_ref, bx_ref, lam_ref, a_ref, b_ref, before_block=side)

    row = lax.broadcasted_iota(jnp.int32, (V7X_SUBLANES, RG_WIDTH), 0)

    def group(gi, hc):
        r0 = pl.multiple_of(gi * V7X_SUBLANES, V7X_SUBLANES)
        a8 = a_ref[pl.ds(r0, V7X_SUBLANES), :]
        b8 = b_ref[pl.ds(r0, V7X_SUBLANES), :]
        for d in (1, 2, 4):
            keep = row >= d
            b8 = jnp.where(keep, a8 * pltpu.roll(b8, d, axis=0) + b8, b8)
            a8 = jnp.where(keep, a8 * pltpu.roll(a8, d, axis=0), a8)
        h8 = a8 * hc + b8
        h_ref[pl.ds(r0, V7X_SUBLANES), :] = h8
        return h8[V7X_SUBLANES - 1:V7X_SUBLANES, :]

    hc = lax.fori_loop(0, tl // V7X_SUBLANES, group, hc_ref[...], unroll=4)
    hc_ref[...] = hc
    hlast_ref[...] = hc
    convn_ref[...] = xe_ref[tl + pad - 3:tl + pad, :]
    y = h_ref[...] * _gelu_tanh(zg_ref[...])
    y_ref[...] = _rms(y, gout_ref[...]).astype(y_ref.dtype)


def _rglru_seq(z, conv0, h0, cw, cb, wgate, ba, bx, lam, gout, *, batch, seq, tl, name,
               side_xattn=None):
    nt = seq // tl
    assert seq % tl == 0
    w = RG_WIDTH
    row = lambda v: v.reshape(1, w)
    const2 = lambda b, t: (0, 0)
    in_specs = [
        pl.BlockSpec((tl, w), lambda b, t: (b * nt + t, OFF_RGX // w)),
        pl.BlockSpec((tl, w), lambda b, t: (b * nt + t, OFF_RGG // w)),
        pl.BlockSpec((None, CONV_W - 1, w), lambda b, t: (b, 0, 0)),
        pl.BlockSpec((None, 1, w), lambda b, t: (b, 0, 0)),
        pl.BlockSpec((CONV_W, w), const2),
        pl.BlockSpec((1, w), const2),
        pl.BlockSpec((RG_BLOCKS, RG_BLOCK, 2 * RG_BLOCK), lambda b, t: (0, 0, 0)),
        pl.BlockSpec((1, w), const2),
        pl.BlockSpec((1, w), const2),
        pl.BlockSpec((1, w), const2),
        pl.BlockSpec((1, w), const2),
    ]
    out_specs = [
        pl.BlockSpec((tl, w), lambda b, t: (b * nt + t, 0)),
        pl.BlockSpec((None, 1, w), lambda b, t: (b, 0, 0)),
        pl.BlockSpec((None, CONV_W - 1, w), lambda b, t: (b, 0, 0)),
    ]
    out_shape = [
        jax.ShapeDtypeStruct((batch * seq, w), BF16),
        jax.ShapeDtypeStruct((batch, 1, w), F32),
        jax.ShapeDtypeStruct((batch, CONV_W - 1, w), F32),
    ]
    args = [z, z, conv0, h0.reshape(batch, 1, w), cw, row(cb), wgate, row(ba), row(bx), row(lam),
            row(gout)]
    side_rows = 0
    vmem_mib = 32
    if side_xattn is not None:
        xq, ck, cv = side_xattn
        nb = xq.shape[0]
        assert nb % (batch * nt) == 0 and ck.shape == (nb, N_MEM, XA_HEADS, XA_DH)
        side_rows = nb // (batch * nt)
        q_spec = pl.BlockSpec((side_rows, XA_HEADS, XA_DH), lambda b, t: (b * nt + t, 0, 0))
        cache_spec = pl.BlockSpec((side_rows, N_MEM, XA_HEADS, XA_DH),
                                  lambda b, t: (b * nt + t, 0, 0, 0))
        in_specs += [q_spec, cache_spec, cache_spec]
        out_specs.append(q_spec)
        out_shape.append(jax.ShapeDtypeStruct((nb, XA_HEADS, XA_DH), F32))
        args += [xq.reshape(nb, XA_HEADS, XA_DH), ck, cv]
        vmem_mib = 56
    return pl.pallas_call(
        functools.partial(_rglru_seq_kernel, tl=tl, side_rows=side_rows),
        grid=(batch, nt),
        in_specs=in_specs,
        out_specs=out_specs,
        out_shape=out_shape,
        scratch_shapes=[
            pltpu.VMEM((tl + V7X_SUBLANES, w), F32),
            pltpu.VMEM((tl, w), F32),
            pltpu.VMEM((tl, w), F32),
            pltpu.VMEM((tl, w), F32),
            pltpu.VMEM((1, w), F32),
        ],
        compiler_params=_params(2, vmem_mib),
        name=name,
    )(*args)


def _rglru_step_kernel(zx_ref, zg_ref, conv_ref, h0_ref, cw_ref, cb_ref, wgate_ref, ba_ref,
                       bx_ref, lam_ref, gout_ref, y_ref, hn_ref, convn_ref, a_ref, b_ref):
    w = RG_WIDTH
    x = zx_ref[...]
    xr = (conv_ref[:, 0:w] * cw_ref[0:1, :] + conv_ref[:, w:2 * w] * cw_ref[1:2, :]
          + conv_ref[:, 2 * w:3 * w] * cw_ref[2:3, :] + x * cw_ref[3:4, :]) + cb_ref[...]
    _rg_gates(xr, wgate_ref, ba_ref, bx_ref, lam_ref, a_ref, b_ref)
    h = a_ref[...] * h0_ref[...] + b_ref[...]
    hn_ref[...] = h
    convn_ref[:, 0:2 * w] = conv_ref[:, w:3 * w]
    convn_ref[:, 2 * w:3 * w] = x
    y_ref[...] = _rms(h * _gelu_tanh(zg_ref[...]), gout_ref[...]).astype(y_ref.dtype)


def _rglru_step(z, conv, h0, cw, cb, wgate, ba, bx, lam, gout, *, name):
    nb = z.shape[0]
    w = RG_WIDTH
    row = lambda v: v.reshape(1, w)
    c0 = lambda i: (0, 0)
    return pl.pallas_call(
        _rglru_step_kernel,
        grid=(1,),
        in_specs=[
            pl.BlockSpec((nb, w), lambda i: (0, OFF_RGX // w)),
            pl.BlockSpec((nb, w), lambda i: (0, OFF_RGG // w)),
            pl.BlockSpec((nb, (CONV_W - 1) * w), c0),
            pl.BlockSpec((nb, w), c0),
            pl.BlockSpec((CONV_W, w), c0),
            pl.BlockSpec((1, w), c0),
            pl.BlockSpec((RG_BLOCKS, RG_BLOCK, 2 * RG_BLOCK), lambda i: (0, 0, 0)),
            pl.BlockSpec((1, w), c0),
            pl.BlockSpec((1, w), c0),
            pl.BlockSpec((1, w), c0),
            pl.BlockSpec((1, w), c0),
        ],
        out_specs=[
            pl.BlockSpec((nb, w), c0),
            pl.BlockSpec((nb, w), c0),
            pl.BlockSpec((nb, (CONV_W - 1) * w), c0),
        ],
        out_shape=[
            jax.ShapeDtypeStruct((nb, w), BF16),
            jax.ShapeDtypeStruct((nb, w), F32),
            jax.ShapeDtypeStruct((nb, (CONV_W - 1) * w), F32),
        ],
        scratch_shapes=[pltpu.VMEM((nb, w), F32), pltpu.VMEM((nb, w), F32)],
        compiler_params=_params(1, 32),
        name=name,
    )(z, z, conv.reshape(nb, (CONV_W - 1) * w), h0, cw, row(cb), wgate, row(ba), row(bx),
      row(lam), row(gout))


def _mlstm_seq_kernel(bi_ref, bf_ref, q_ref, k_ref, v_ref, o_ref, zg_ref, g_ref,
                      y_ref, c_ref, n_ref, m_ref, zgt_ref, cs_ref, ns_ref, *, seq, cs):
    h = pl.program_id(1)
    nc = seq // cs
    bi = bi_ref[h]
    bf = bf_ref[h]
    for c in range(nc):
        zgt_ref[c] = zg_ref[c * cs:(c + 1) * cs, :].T
    cs_ref[...] = jnp.zeros_like(cs_ref)
    ns_ref[...] = jnp.zeros_like(ns_ref)

    t_idx = lax.broadcasted_iota(jnp.int32, (cs, cs), 0)
    s_idx = lax.broadcasted_iota(jnp.int32, (cs, cs), 1)
    causal = s_idx <= t_idx
    lane = lax.broadcasted_iota(jnp.int32, (cs, V7X_LANES), 1)

    def chunk(c, m):
        r0 = pl.multiple_of(c * cs, cs)
        q = q_ref[pl.ds(r0, cs), :]
        k = k_ref[pl.ds(r0, cs), :] * (ML_DK ** -0.5)
        v = v_ref[pl.ds(r0, cs), :]
        zg = zg_ref[pl.ds(r0, cs), :]
        li_col = jnp.sum(jnp.where(lane == h, zg, 0.0), axis=1, keepdims=True) + bi
        lf_col = _log_sigmoid(
            jnp.sum(jnp.where(lane == h + ML_HEADS, zg, 0.0), axis=1, keepdims=True) + bf)
        li_row = zgt_ref[c, pl.ds(h, 1), :] + bi
        lf_row = _log_sigmoid(zgt_ref[c, pl.ds(h + ML_HEADS, 1), :] + bf)
        bcum_col = jnp.sum(jnp.where(causal, lf_row, 0.0), axis=1, keepdims=True)
        bcum_row = jnp.sum(jnp.where(t_idx <= s_idx, lf_col, 0.0), axis=0, keepdims=True)
        log_d = jnp.where(causal, bcum_col - bcum_row + li_row, NEG)
        inter = bcum_col + m
        m_t = jnp.maximum(inter, jnp.max(log_d, axis=1, keepdims=True))
        dmat = jnp.exp(log_d - m_t)
        sc = jnp.exp(inter - m_t)
        qb = q.astype(BF16)
        kb = k.astype(BF16)
        vb = v.astype(BF16)
        qk = lax.dot_general(qb, kb, (((1,), (1,)), ((), ())), preferred_element_type=F32) * dmat
        c_old = cs_ref[...]
        n_old = ns_ref[...]
        num = sc * jnp.dot(qb, c_old.astype(BF16), preferred_element_type=F32) + _mm(qk, vb)
        den = sc * jnp.sum(q * n_old, axis=1, keepdims=True) + jnp.sum(qk, axis=1, keepdims=True)
        den = jnp.maximum(jnp.abs(den), jnp.exp(-m_t))
        hh = num / den
        m_new = m_t[cs - 1:cs, :]
        b_last = bcum_col[cs - 1:cs, :]
        w_end = jnp.exp(b_last - bcum_col + li_col - m_new)
        dec = jnp.exp(b_last + m - m_new)
        wk = w_end * k
        cs_ref[...] = dec * c_old + lax.dot_general(
            wk.astype(BF16), vb, (((0,), (0,)), ((), ())), preferred_element_type=F32)
        ns_ref[...] = dec * n_old + jnp.sum(wk, axis=0, keepdims=True)
        y = _rms(hh, g_ref[...]) * jax.nn.sigmoid(o_ref[pl.ds(r0, cs), :])
        y_ref[pl.ds(r0, cs), :] = y.astype(y_ref.dtype)
        return m_new

    m_fin = lax.fori_loop(0, nc, chunk, jnp.zeros((1, 1), F32))
    c_ref[...] = cs_ref[...]
    n_ref[pl.ds(h, 1), :] = ns_ref[...]
    m_ref[pl.ds(h, 1), :] = jnp.broadcast_to(m_fin, (1, V7X_LANES))


def _mlstm_seq(z, zg, b_i, b_f, g_out, *, batch, seq, cs, name):
    assert seq % cs == 0 and zg.shape[1] == V7X_LANES
    dk, dv, nh = ML_DK, ML_DV, ML_HEADS
    smem = pl.BlockSpec(memory_space=pltpu.SMEM)
    return pl.pallas_call(
        functools.partial(_mlstm_seq_kernel, seq=seq, cs=cs),
        grid=(batch, nh),
        in_specs=[
            smem, smem,
            pl.BlockSpec((seq, dk), lambda b, h: (b, OFF_Q // dk + h)),
            pl.BlockSpec((seq, dk), lambda b, h: (b, OFF_K // dk + h)),
            pl.BlockSpec((seq, dv), lambda b, h: (b, OFF_V // dv + h)),
            pl.BlockSpec((seq, dv), lambda b, h: (b, OFF_O // dv + h)),
            pl.BlockSpec((seq, V7X_LANES), lambda b, h: (b, 0)),
            pl.BlockSpec((1, dv), lambda b, h: (0, h)),
        ],
        out_specs=[
            pl.BlockSpec((seq, dv), lambda b, h: (b, h)),
            pl.BlockSpec((None, None, dk, dv), lambda b, h: (b, h, 0, 0)),
            pl.BlockSpec((None, nh, dk), lambda b, h: (b, 0, 0)),
            pl.BlockSpec((None, nh, V7X_LANES), lambda b, h: (b, 0, 0)),
        ],
        out_shape=[
            jax.ShapeDtypeStruct((batch * seq, nh * dv), BF16),
            jax.ShapeDtypeStruct((batch, nh, dk, dv), F32),
            jax.ShapeDtypeStruct((batch, nh, dk), F32),
            jax.ShapeDtypeStruct((batch, nh, V7X_LANES), F32),
        ],
        scratch_shapes=[
            pltpu.VMEM((seq // cs, V7X_LANES, cs), F32),
            pltpu.VMEM((dk, dv), F32),
            pltpu.VMEM((1, dk), F32),
        ],
        compiler_params=_params(2, 40),
        name=name,
    )(b_i, b_f, z, z, z, z, zg, g_out.reshape(1, nh * dv))


def _mlstm_step_kernel(bi_ref, bf_ref, q_ref, k_ref, v_ref, o_ref, zg_ref, g_ref, c0_ref, n0_ref,
                       m0_ref, y_ref, c_ref, n_ref, m_ref, qc_ref, *, bs):
    dk, dv = ML_DK, ML_DV
    eye = (lax.broadcasted_iota(jnp.int32, (dk, dk), 0)
           == lax.broadcasted_iota(jnp.int32, (dk, dk), 1))
    zg = zg_ref[...]
    for h in range(ML_HEADS):
        li = zg[:, h:h + 1] + bi_ref[h]
        lf = _log_sigmoid(zg[:, ML_HEADS + h:ML_HEADS + h + 1] + bf_ref[h])
        m = m0_ref[:, h:h + 1]
        inter = lf + m
        m_t = jnp.maximum(inter, li)
        dgate = jnp.exp(li - m_t)
        sc = jnp.exp(inter - m_t)
        q = q_ref[:, h * dk:(h + 1) * dk]
        k = k_ref[:, h * dk:(h + 1) * dk] * (ML_DK ** -0.5)
        v = v_ref[:, h * dv:(h + 1) * dv]
        n_old = n0_ref[:, h, :]
        qk = jnp.sum(q * k, axis=1, keepdims=True) * dgate
        w_end = jnp.exp(li - m_t)
        dec = jnp.exp(inter - m_t)
        wk = w_end * k
        for j in range(bs):
            c_old = c0_ref[j, h]
            qc_ref[j:j + 1, :] = _mm(q[j:j + 1, :], c_old)
            wk_col = jnp.sum(
                jnp.where(eye, jnp.broadcast_to(wk[j:j + 1, :], (dk, dk)), 0.0),
                axis=1, keepdims=True)
            c_ref[j, h] = dec[j:j + 1, :] * c_old + wk_col * v[j:j + 1, :]
        num = sc * qc_ref[...] + qk * v
        den = sc * jnp.sum(q * n_old, axis=1, keepdims=True) + qk
        den = jnp.maximum(jnp.abs(den), jnp.exp(-m_t))
        hh = num / den
        n_ref[:, h, :] = dec * n_old + wk
        m_ref[:, h:h + 1] = m_t
        y = _rms(hh, g_ref[:, h * dv:(h + 1) * dv]) * jax.nn.sigmoid(o_ref[:, h * dv:(h + 1) * dv])
        y_ref[:, h * dv:(h + 1) * dv] = y.astype(y_ref.dtype)


def _mlstm_step(z, zg, c0, n0, m0, b_i, b_f, g_out, *, bs, name):
    nb = z.shape[0]
    assert nb % bs == 0
    dk, dv, nh = ML_DK, ML_DV, ML_HEADS
    smem = pl.BlockSpec(memory_space=pltpu.SMEM)
    return pl.pallas_call(
        functools.partial(_mlstm_step_kernel, bs=bs),
        grid=(nb // bs,),
        in_specs=[
            smem, smem,
            pl.BlockSpec((bs, nh * dk), lambda i: (i, OFF_Q // (nh * dk))),
            pl.BlockSpec((bs, nh * dk), lambda i: (i, OFF_K // (nh * dk))),
            pl.BlockSpec((bs, nh * dv), lambda i: (i, OFF_V // (nh * dv))),
            pl.BlockSpec((bs, nh * dv), lambda i: (i, OFF_O // (nh * dv))),
            pl.BlockSpec((bs, V7X_LANES), lambda i: (i, 0)),
            pl.BlockSpec((1, nh * dv), lambda i: (0, 0)),
            pl.BlockSpec((bs, nh, dk, dv), lambda i: (i, 0, 0, 0)),
            pl.BlockSpec((bs, nh, dk), lambda i: (i, 0, 0)),
            pl.BlockSpec((bs, nh), lambda i: (i, 0)),
        ],
        out_specs=[
            pl.BlockSpec((bs, nh * dv), lambda i: (i, 0)),
            pl.BlockSpec((bs, nh, dk, dv), lambda i: (i, 0, 0, 0)),
            pl.BlockSpec((bs, nh, dk), lambda i: (i, 0, 0)),
            pl.BlockSpec((bs, nh), lambda i: (i, 0)),
        ],
        out_shape=[
            jax.ShapeDtypeStruct((nb, nh * dv), F32),
            jax.ShapeDtypeStruct((nb, nh, dk, dv), F32),
            jax.ShapeDtypeStruct((nb, nh, dk), F32),
            jax.ShapeDtypeStruct((nb, nh), F32),
        ],
        scratch_shapes=[pltpu.VMEM((bs, dv), F32)],
        compiler_params=_params(1, 32),
        name=name,
    )(b_i, b_f, z, z, z, z, zg, g_out.reshape(1, nh * dv), c0, n0, m0)


def _softmax_rows(s):
    e = jnp.exp(s - jnp.max(s, axis=-1, keepdims=True))
    return e / jnp.sum(e, axis=-1, keepdims=True)


def _xattn_seq_kernel(q_ref, k_ref, v_ref, o_ref):
    for h in range(XA_HEADS):
        sl = slice(h * XA_DH, (h + 1) * XA_DH)
        s = lax.dot_general(q_ref[:, sl].astype(BF16), k_ref[:, sl].astype(BF16),
                            (((1,), (1,)), ((), ())), preferred_element_type=F32)
        p = _softmax_rows(s * (XA_DH ** -0.5))
        o_ref[:, sl] = _mm(p, v_ref[:, sl]).astype(o_ref.dtype)


def _xattn_seq(xq, mk, mv, *, batch, seq, tq, name):
    nt = seq // tq
    assert seq % tq == 0
    d = D_MODEL
    return pl.pallas_call(
        _xattn_seq_kernel,
        grid=(batch, nt),
        in_specs=[
            pl.BlockSpec((tq, d), lambda b, t: (b * nt + t, 0)),
            pl.BlockSpec((N_MEM, d), lambda b, t: (b, 0)),
            pl.BlockSpec((N_MEM, d), lambda b, t: (b, 0)),
        ],
        out_specs=pl.BlockSpec((tq, d), lambda b, t: (b * nt + t, 0)),
        out_shape=jax.ShapeDtypeStruct((batch * seq, d), BF16),
        compiler_params=_params(2, 40),
        name=name,
    )(xq, mk, mv)


def _xattn_own_mask():
    nrow = N_MEM * XA_HEADS
    col_head = lax.broadcasted_iota(jnp.int32, (V7X_SUBLANES, nrow), 1) & (XA_HEADS - 1)
    row_head = lax.broadcasted_iota(jnp.int32, (V7X_SUBLANES, nrow), 0) & (XA_HEADS - 1)
    return col_head == row_head


def _xattn_one_row(q_ref, k_ref, v_ref, o_ref, j, own):
    nrow = N_MEM * XA_HEADS
    kf = k_ref[j].reshape(nrow, XA_DH)
    vf = v_ref[j].reshape(nrow, XA_DH)
    q8 = jnp.concatenate([q_ref[j]] * (V7X_SUBLANES // XA_HEADS), axis=0)
    s = lax.dot_general(q8.astype(BF16), kf.astype(BF16), (((1,), (1,)), ((), ())),
                        preferred_element_type=F32)
    s = jnp.where(own, s * (XA_DH ** -0.5), NEG)
    e = jnp.where(own, jnp.exp(s - jnp.max(s, axis=-1, keepdims=True)), 0.0)
    p = e / jnp.sum(e, axis=-1, keepdims=True)
    o_ref[j] = _mm(p, vf)[0:XA_HEADS, :]


def _xattn_step_kernel(q_ref, k_ref, v_ref, o_ref, *, sb):
    own = _xattn_own_mask()
    for j in range(sb):
        _xattn_one_row(q_ref, k_ref, v_ref, o_ref, j, own)


def _xattn_step(xq, ck, cv, *, sb, name):
    nb = xq.shape[0]
    assert XA_HEADS & (XA_HEADS - 1) == 0
    assert nb % sb == 0 and ck.shape == (nb, N_MEM, XA_HEADS, XA_DH)
    cache_spec = pl.BlockSpec((sb, N_MEM, XA_HEADS, XA_DH), lambda i: (i, 0, 0, 0))
    q_spec = pl.BlockSpec((sb, XA_HEADS, XA_DH), lambda i: (i, 0, 0))
    return pl.pallas_call(
        functools.partial(_xattn_step_kernel, sb=sb),
        grid=(nb // sb,),
        in_specs=[q_spec, cache_spec, cache_spec],
        out_specs=q_spec,
        out_shape=jax.ShapeDtypeStruct((nb, XA_HEADS, XA_DH), F32),
        compiler_params=_params(1, 40),
        name=name,
    )(xq.reshape(nb, XA_HEADS, XA_DH), ck, cv).reshape(nb, XA_HEADS * XA_DH)


def _tiles(rows):
    tm = min(rows, 1024)
    assert rows % tm == 0
    return tm


def kernel(x_prompt, x_sample, mem_prompt, state_rg_h, state_rg_conv, state_ml_C, state_ml_n, state_ml_m, cache_mem_k, cache_mem_v, g_mix, w_in, conv_w, conv_b, w_rg_a, b_rg_a, w_rg_x, b_rg_x, rg_lambda, b_ml_i, b_ml_f, g_rg_out, g_ml_out, w_out, g_xa, g_mem, w_xa_q, w_xa_k, w_xa_v, w_xa_o, g_ffn, w_ffn_gate, w_ffn_up, w_ffn_down, g_final):
    depth = g_mix.shape[0]
    assert depth == 1, "single trunk layer"
    bp, seq, d = x_prompt.shape
    bs_, dec_seq, _ = x_sample.shape
    assert d == D_MODEL and dec_seq == 1
    n_mem = mem_prompt.shape[1]
    assert n_mem == N_MEM
    dff = w_ffn_gate.shape[-1]
    in_w = w_in.shape[-1]
    assert in_w == IN_MAIN + N_GATE

    w_in_t = jnp.swapaxes(w_in, 1, 2).reshape(in_w, d)
    w_gate_pad = jnp.pad(w_in_t[IN_MAIN:], ((0, V7X_LANES - N_GATE), (0, 0)))
    cw = conv_w.reshape(CONV_W, RG_WIDTH)
    wgate = jnp.concatenate([w_rg_a.reshape(RG_BLOCKS, RG_BLOCK, RG_BLOCK),
                             w_rg_x.reshape(RG_BLOCKS, RG_BLOCK, RG_BLOCK)], axis=-1)
    rg_args = (cw, conv_b.reshape(-1), wgate, b_rg_a.reshape(-1), b_rg_x.reshape(-1),
               rg_lambda.reshape(-1), g_rg_out.reshape(-1))
    b_i = b_ml_i.reshape(ML_HEADS)
    b_f = b_ml_f.reshape(ML_HEADS)
    g_ml = g_ml_out.reshape(-1)
    w_out2 = w_out.reshape(d, d)
    w_q = w_xa_q.reshape(d, d)
    w_k = w_xa_k.reshape(d, d)
    w_v = w_xa_v.reshape(d, d)
    w_o = w_xa_o.reshape(d, d)
    w_fg = w_ffn_gate.reshape(d, dff)
    w_fu = w_ffn_up.reshape(d, dff)
    w_fd = w_ffn_down.reshape(dff, d)

    row_tile = _tiles(bp * seq)
    rg_zero_conv = jnp.zeros((bp, CONV_W - 1, RG_WIDTH), F32)
    rg_zero_h = jnp.zeros((bp, RG_WIDTH), F32)

    xs = x_sample.reshape(bs_, d)
    z_s, zg_s, w_in16 = _norm_linear(
        xs, g_mix.reshape(-1), w_in_t, n_out=IN_MAIN, tm=bs_, tn=1024, w_gate=w_gate_pad,
        w_is_nk=True, emit_w16=True, name="in_proj_s")
    y_rg_s, s_h, s_conv = _rglru_step(
        z_s, state_rg_conv.reshape(bs_, CONV_W - 1, RG_WIDTH), state_rg_h.reshape(bs_, RG_WIDTH),
        *rg_args, name="rglru_s")
    y_ml_s, s_c, s_n, s_m = _mlstm_step(
        z_s, zg_s, state_ml_C.reshape(bs_, ML_HEADS, ML_DK, ML_DV),
        state_ml_n.reshape(bs_, ML_HEADS, ML_DK), state_ml_m.reshape(bs_, ML_HEADS),
        b_i, b_f, g_ml, bs=V7X_SUBLANES, name="mlstm_s")
    x1_s, w_out16_rg, w_out16_ml = _linear_residual(
        [y_rg_s, y_ml_s], [(w_out2, 0), (w_out2, 1)], xs, tm=bs_, tn=1024, emit_w16=True,
        name="mix_out_s")
    xq_s, w_q16 = _norm_linear(x1_s, g_xa.reshape(-1), w_q, n_out=d, tm=bs_, tn=1024,
                               emit_w16=True, name="xa_q_s")
    ck = cache_mem_k.reshape(bs_, n_mem, XA_HEADS, XA_DH)
    cv = cache_mem_v.reshape(bs_, n_mem, XA_HEADS, XA_DH)

    tp = bp * seq
    xp = x_prompt.reshape(tp, d)
    z_p, zg_p = _norm_linear(xp, g_mix.reshape(-1), w_in16, n_out=IN_MAIN, tm=row_tile, tn=1024,
                             w_gate=w_gate_pad, w_is_nk=True, name="in_proj_p")
    rg_tl = min(seq, 256)
    if bs_ % (bp * (seq // rg_tl)) == 0:
        y_rg_p, p_h, p_conv, o_s = _rglru_seq(
            z_p, rg_zero_conv, rg_zero_h, *rg_args, batch=bp, seq=seq, tl=rg_tl,
            side_xattn=(xq_s, ck, cv), name="rglru_p")
        o_s = o_s.reshape(bs_, d)
    else:
        y_rg_p, p_h, p_conv = _rglru_seq(z_p, rg_zero_conv, rg_zero_h, *rg_args, batch=bp,
                                         seq=seq, tl=rg_tl, name="rglru_p")
        o_s = _xattn_step(xq_s, ck, cv, sb=2, name="xattn_s")

    x3_s, w_o16 = _linear_residual([o_s], [(w_o, 0)], x1_s, tm=bs_, tn=1024, emit_w16=True,
                                   name="xa_out_s")
    y_s, w_fg16, w_fu16, w_fd16 = _ffn(x3_s, g_ffn.reshape(-1), w_fg, w_fu, w_fd, g_final,
                                       tm=bs_, tf=512, emit_w16=True, vmem_mib=48, name="ffn_s")

    y_ml_p, p_c, p_n, p_m = _mlstm_seq(z_p, zg_p, b_i, b_f, g_ml, batch=bp, seq=seq,
                                       cs=min(seq, ML_CHUNK), name="mlstm_p")
    mem2 = mem_prompt.reshape(bp * n_mem, d)
    tmem = min(bp * n_mem, 256)
    mk, mk_heads = _norm_linear(mem2, g_mem.reshape(-1), w_k, n_out=d, tm=tmem, tn=d,
                                heads_out=(XA_HEADS, XA_DH), name="mem_k")
    mv, mv_heads = _norm_linear(mem2, g_mem.reshape(-1), w_v, n_out=d, tm=tmem, tn=d,
                                heads_out=(XA_HEADS, XA_DH), name="mem_v")
    proj_tile = min(row_tile, 512)
    x1_p = _linear_residual([y_rg_p, y_ml_p], [(w_out16_rg, 0), (w_out16_ml, 0)], xp,
                            tm=proj_tile, tn=d, name="mix_out_p")
    xq_p = _norm_linear(x1_p, g_xa.reshape(-1), w_q16, n_out=d, tm=proj_tile, tn=d,
                        out_dtype=BF16, name="xa_q_p")
    o_p = _xattn_seq(xq_p, mk, mv, batch=bp, seq=seq, tq=min(seq, 1024), name="xattn_p")
    x3_p = _linear_residual([o_p], [(w_o16, 0)], x1_p, tm=proj_tile, tn=d, name="xa_out_p")
    y_p = _ffn(x3_p, g_ffn.reshape(-1), w_fg16, w_fu16, w_fd16, g_final, tm=row_tile, tf=512,
               vmem_mib=60, name="ffn_p")

    return (
        y_p.reshape(bp, seq, d),
        y_s.reshape(bs_, 1, d),
        p_h.reshape(1, bp, RG_WIDTH),
        p_conv.reshape(1, bp, CONV_W - 1, RG_WIDTH),
        p_c.reshape(1, bp, ML_HEADS, ML_DK, ML_DV),
        p_n.reshape(1, bp, ML_HEADS, ML_DK),
        p_m[:, :, 0].reshape(1, bp, ML_HEADS),
        mk_heads.reshape(1, bp, n_mem, XA_HEADS, XA_DH),
        mv_heads.reshape(1, bp, n_mem, XA_HEADS, XA_DH),
        s_h.reshape(1, bs_, RG_WIDTH),
        s_conv.reshape(1, bs_, CONV_W - 1, RG_WIDTH),
        s_c.reshape(1, bs_, ML_HEADS, ML_DK, ML_DV),
        s_n.reshape(1, bs_, ML_HEADS, ML_DK),
        s_m.reshape(1, bs_, ML_HEADS),
    )
```

```python
import functools

import jax
import jax.numpy as jnp
from jax import lax
from jax.experimental import pallas as pl
from jax.experimental.pallas import tpu as pltpu

F32 = jnp.float32
BF16 = jnp.bfloat16

D_MODEL = 2048
RG_WIDTH = D_MODEL // 2
RG_BLOCKS = 8
RG_BLOCK = RG_WIDTH // RG_BLOCKS
CONV_W = 4
RG_C = 8.0
ML_HEADS = 4
ML_WIDTH = D_MODEL - RG_WIDTH
ML_DV = ML_WIDTH // ML_HEADS
ML_DK = ML_DV // 2
N_MEM = 256
XA_HEADS = 4
XA_DH = D_MODEL // XA_HEADS
EPS = 1e-6
NEG = -1e30

OFF_RGX = 0
OFF_RGG = OFF_RGX + RG_WIDTH
OFF_Q = OFF_RGG + RG_WIDTH
OFF_K = OFF_Q + ML_HEADS * ML_DK
OFF_V = OFF_K + ML_HEADS * ML_DK
OFF_O = OFF_V + ML_WIDTH
OFF_I = OFF_O + ML_WIDTH
IN_MAIN = OFF_I
N_GATE = 2 * ML_HEADS

V7X_LANES = 128
V7X_SUBLANES = 8
V7X_VMEM_BYTES = 64 * 2**20

ML_CHUNK = 256


def _params(n_axes, vmem_mib):
    assert vmem_mib * 2**20 <= V7X_VMEM_BYTES
    return pltpu.CompilerParams(
        dimension_semantics=("arbitrary",) * n_axes,
        vmem_limit_bytes=vmem_mib * 2**20,
    )


def _rms(x, g):
    ms = jnp.mean(x * x, axis=-1, keepdims=True)
    return x * lax.rsqrt(ms + EPS) * g


def _softplus(u):
    return jnp.maximum(u, 0.0) + jnp.log1p(jnp.exp(-jnp.abs(u)))


def _log_sigmoid(u):
    return -_softplus(-u)


def _sigmoid(u):
    return 0.5 * jnp.tanh(0.5 * u) + 0.5


def _gelu_tanh(x):
    c = 0.7978845608028654
    half_x = 0.5 * x
    return half_x + half_x * jnp.tanh(x * (c + (c * 0.044715) * (x * x)))


def _sqrt_nonneg(v):
    return jnp.where(v > 0.0, v * lax.rsqrt(v), 0.0)


def _mm(a, b):
    return jnp.dot(a.astype(BF16), b.astype(BF16), preferred_element_type=F32)


def _dot_w(a, w_ref, w_is_nk):
    w = w_ref[...].astype(BF16)
    if w_is_nk:
        return lax.dot_general(a, w, (((1,), (1,)), ((), ())), preferred_element_type=F32)
    return jnp.dot(a, w, preferred_element_type=F32)


def _norm_linear_kernel(*refs, with_gate, w_is_nk, emit_w16, heads_out):
    refs = list(refs)
    x_ref, g_ref, w_ref = refs[:3]
    wg_ref = refs[3] if with_gate else None
    outs = refs[3 + with_gate:-1]
    xn_ref = refs[-1]
    o_ref = outs[0]
    oh_ref = outs[1] if heads_out else None
    og_ref = outs[1 + bool(heads_out)] if with_gate else None

    def column_tile(xn):
        acc = _dot_w(xn, w_ref, w_is_nk)
        o_ref[...] = acc.astype(o_ref.dtype)
        if heads_out:
            oh_ref[...] = acc.reshape(oh_ref.shape)
        if emit_w16:
            outs[-1][...] = w_ref[...].astype(BF16)

    @pl.when(pl.program_id(1) == 0)
    def _():
        xn = _rms(x_ref[...], g_ref[...]).astype(BF16)
        xn_ref[...] = xn
        if with_gate:
            og_ref[...] = _dot_w(xn, wg_ref, w_is_nk)
        column_tile(xn)

    @pl.when(pl.program_id(1) > 0)
    def _():
        column_tile(xn_ref[...])


def _norm_linear(x, g, w, *, n_out, tm, tn, out_dtype=F32, w_gate=None, w_is_nk=False,
                 emit_w16=False, heads_out=None, vmem_mib=48, name):
    m, k = x.shape
    k_ax, n_ax = (1, 0) if w_is_nk else (0, 1)
    assert m % tm == 0 and n_out % tn == 0 and w.shape[k_ax] == k and n_out <= w.shape[n_ax]
    assert not emit_w16 or m == tm
    assert heads_out is None or (tn == n_out and heads_out[0] * heads_out[1] == n_out)
    with_gate = w_gate is not None
    w_mode = dict(pipeline_mode=pl.Buffered(1)) if tn == n_out else {}
    w_spec = (pl.BlockSpec((tn, k), lambda i, j: (j, 0), **w_mode) if w_is_nk
              else pl.BlockSpec((k, tn), lambda i, j: (0, j), **w_mode))
    in_specs = [
        pl.BlockSpec((tm, k), lambda i, j: (i, 0)),
        pl.BlockSpec((1, k), lambda i, j: (0, 0)),
        w_spec,
    ]
    out_shape = [jax.ShapeDtypeStruct((m, n_out), out_dtype)]
    out_specs = [pl.BlockSpec((tm, tn), lambda i, j: (i, j))]
    args = [x, g.reshape(1, k), w]
    if heads_out:
        out_shape.append(jax.ShapeDtypeStruct((m, *heads_out), F32))
        out_specs.append(pl.BlockSpec((tm, *heads_out), lambda i, j: (i, 0, 0)))
    if with_gate:
        ng = w_gate.shape[n_ax]
        in_specs.append(pl.BlockSpec(w_gate.shape, lambda i, j: (0, 0)))
        out_shape.append(jax.ShapeDtypeStruct((m, ng), F32))
        out_specs.append(pl.BlockSpec((tm, ng), lambda i, j: (i, 0)))
        args.append(w_gate)
    if emit_w16:
        out_shape.append(jax.ShapeDtypeStruct((n_out, k) if w_is_nk else (k, n_out), BF16))
        out_specs.append(pl.BlockSpec((tn, k), lambda i, j: (j, 0)) if w_is_nk
                         else pl.BlockSpec((k, tn), lambda i, j: (0, j)))
    out = pl.pallas_call(
        functools.partial(_norm_linear_kernel, with_gate=with_gate, w_is_nk=w_is_nk,
                          emit_w16=emit_w16, heads_out=heads_out),
        grid=(m // tm, n_out // tn),
        in_specs=in_specs,
        out_specs=out_specs,
        out_shape=out_shape,
        scratch_shapes=[pltpu.VMEM((tm, k), BF16)],
        compiler_params=_params(2, vmem_mib),
        name=name,
    )(*args)
    return out if len(out) > 1 else out[0]


def _linear_res_kernel(*refs, n_in, emit_w16):
    a_refs = refs[:n_in]
    w_refs = refs[n_in:2 * n_in]
    res_ref = refs[2 * n_in]
    o_ref = refs[2 * n_in + 1]
    acc = res_ref[...]
    for a_ref, w_ref in zip(a_refs, w_refs):
        acc = acc + _mm(a_ref[...], w_ref[...])
    o_ref[...] = acc
    if emit_w16:
        for w_ref, wc_ref in zip(w_refs, refs[2 * n_in + 2:]):
            wc_ref[...] = w_ref[...].astype(BF16)


def _linear_residual(parts, weights, res, *, tm, tn, emit_w16=False, vmem_mib=48, name):
    m, n = res.shape
    kp = parts[0].shape[1]
    assert all(p.shape == (m, kp) for p in parts) and len(weights) == len(parts)
    assert m % tm == 0 and n % tn == 0 and (not emit_w16 or m == tm)
    n_in = len(parts)
    in_specs = [pl.BlockSpec((tm, kp), lambda i, j: (i, 0)) for _ in parts]
    w_mode = dict(pipeline_mode=pl.Buffered(1)) if tn == n else {}
    in_specs += [pl.BlockSpec((kp, tn), lambda i, j, rb=rb: (rb, j), **w_mode)
                 for _, rb in weights]
    in_specs.append(pl.BlockSpec((tm, tn), lambda i, j: (i, j)))
    out_specs = [pl.BlockSpec((tm, tn), lambda i, j: (i, j))]
    out_shape = [jax.ShapeDtypeStruct((m, n), F32)]
    if emit_w16:
        out_specs += [pl.BlockSpec((kp, tn), lambda i, j: (0, j)) for _ in parts]
        out_shape += [jax.ShapeDtypeStruct((kp, n), BF16) for _ in parts]
    out = pl.pallas_call(
        functools.partial(_linear_res_kernel, n_in=n_in, emit_w16=emit_w16),
        grid=(m // tm, n // tn),
        in_specs=in_specs,
        out_specs=out_specs,
        out_shape=out_shape,
        compiler_params=_params(2, vmem_mib),
        name=name,
    )(*parts, *[w for w, _ in weights], res)
    return out if emit_w16 else out[0]


def _ffn_kernel(x_ref, g_ref, wg_ref, wu_ref, wd_ref, gf_ref, o_ref, *rest, emit_w16):
    xf_ref = rest[-1]
    f = pl.program_id(1)
    last = pl.num_programs(1) - 1

    def hidden_tile(xf):
        wg = wg_ref[...].astype(BF16)
        wu = wu_ref[...].astype(BF16)
        wd = wd_ref[...].astype(BF16)
        if emit_w16:
            for dst, val in zip(rest[:3], (wg, wu, wd)):
                dst[...] = val
        gate = jnp.dot(xf, wg, preferred_element_type=F32)
        up = jnp.dot(xf, wu, preferred_element_type=F32)
        hidden = (gate * _sigmoid(gate)) * up
        return jnp.dot(hidden.astype(BF16), wd, preferred_element_type=F32)

    @pl.when(f == 0)
    def _():
        x = x_ref[...]
        xf = _rms(x, g_ref[...]).astype(BF16)
        xf_ref[...] = xf
        o_ref[...] = x + hidden_tile(xf)

    @pl.when((f > 0) & (f < last))
    def _():
        o_ref[...] += hidden_tile(xf_ref[...])

    @pl.when(f == last)
    def _():
        o_ref[...] = _rms(o_ref[...] + hidden_tile(xf_ref[...]), gf_ref[...])


def _ffn(x, g, w_gate, w_up, w_down, g_final, *, tm, tf, emit_w16=False, vmem_mib, name):
    m, d = x.shape
    dff = w_gate.shape[1]
    assert m % tm == 0 and dff % tf == 0 and dff // tf >= 2 and (not emit_w16 or m == tm)
    up_spec = pl.BlockSpec((d, tf), lambda i, f: (0, f))
    down_spec = pl.BlockSpec((tf, d), lambda i, f: (f, 0))
    out_specs = [pl.BlockSpec((tm, d), lambda i, f: (i, 0))]
    out_shape = [jax.ShapeDtypeStruct((m, d), F32)]
    if emit_w16:
        out_specs += [up_spec, up_spec, down_spec]
        out_shape += [jax.ShapeDtypeStruct(w.shape, BF16) for w in (w_gate, w_up, w_down)]
    out = pl.pallas_call(
        functools.partial(_ffn_kernel, emit_w16=emit_w16),
        grid=(m // tm, dff // tf),
        in_specs=[
            pl.BlockSpec((tm, d), lambda i, f: (i, 0)),
            pl.BlockSpec((1, d), lambda i, f: (0, 0)),
            up_spec,
            up_spec,
            down_spec,
            pl.BlockSpec((1, d), lambda i, f: (0, 0)),
        ],
        out_specs=out_specs,
        out_shape=out_shape,
        scratch_shapes=[pltpu.VMEM((tm, d), BF16)],
        compiler_params=_params(2, vmem_mib),
        name=name,
    )(x, g.reshape(1, d), w_gate, w_up, w_down, g_final.reshape(1, d))
    return out if emit_w16 else out[0]


RG_GATE_LOOKAHEAD = 2


def _rg_gates(xr, wgate_ref, ba_ref, bx_ref, lam_ref, a_ref, b_ref, side=None):
    blocks = [slice(n * RG_BLOCK, (n + 1) * RG_BLOCK) for n in range(RG_BLOCKS)]
    gates = {}

    def issue_gate(n):
        if n < RG_BLOCKS:
            gates[n] = _mm(xr[:, blocks[n]], wgate_ref[n])

    rows, score_phase, value_phase = side if side else (0, None, None)
    per = RG_BLOCKS // rows if rows else RG_BLOCKS
    scores = {}
    for n in range(RG_GATE_LOOKAHEAD):
        issue_gate(n)
    for n, sl in enumerate(blocks):
        if rows and n % per == 0:
            j = n // per
            scores[j] = score_phase(j)
            if j > 0:
                value_phase(j - 1, scores.pop(j - 1))
        xn = xr[:, sl]
        g = gates.pop(n)
        issue_gate(n + RG_GATE_LOOKAHEAD)
        r = _sigmoid(g[:, :RG_BLOCK] + ba_ref[:, sl])
        ig = _sigmoid(g[:, RG_BLOCK:] + bx_ref[:, sl])
        a = jnp.exp(r * (-RG_C * _softplus(-lam_ref[:, sl])))
        a_ref[:, sl] = a
        mult = _sqrt_nonneg(jnp.maximum(1.0 - a * a, 0.0))
        b_ref[:, sl] = mult * (ig * xn)
    if rows:
        value_phase(rows - 1, scores.pop(rows - 1))


def _rglru_seq_kernel(*refs, tl, side_rows):
    (zx_ref, zg_ref, conv0_ref, h0_ref, cw_ref, cb_ref, wgate_ref, ba_ref, bx_ref, lam_ref,
     gout_ref) = refs[:11]
    xe_ref, a_ref, b_ref, h_ref, hc_ref = refs[-5:]
    if side_rows:
        sq_ref, sk_ref, sv_ref, y_ref, hlast_ref, convn_ref, so_ref = refs[11:-5]
    else:
        y_ref, hlast_ref, convn_ref = refs[11:-5]
    t = pl.program_id(1)
    pad = V7X_SUBLANES

    @pl.when(t == 0)
    def _():
        xe_ref[pad - 3:pad, :] = conv0_ref[...]
        hc_ref[...] = h0_ref[...]

    @pl.when(t > 0)
    def _():
        xe_ref[pad - 3:pad, :] = xe_ref[tl + pad - 3:tl + pad, :]

    x = zx_ref[...]
    xe_ref[pad:tl + pad, :] = x
    xr = (xe_ref[pad - 3:tl + pad - 3, :] * cw_ref[0:1, :]
          + xe_ref[pad - 2:tl + pad - 2, :] * cw_ref[1:2, :]
          + xe_ref[pad - 1:tl + pad - 1, :] * cw_ref[2:3, :]
          + x * cw_ref[3:4, :]) + cb_ref[...]
    side = None
    if side_rows:
        own = _xattn_own_mask()
        side = (side_rows,
                lambda j: _xattn_probs(sq_ref, sk_ref, j, own),
                lambda j, p: _xattn_values(sv_ref, so_ref, j, p))

    _rg_gates(xr, wgate_ref, ba_ref, bx_ref, lam_ref, a_ref, b_ref, side=side)

    row = lax.broadcasted_iota(jnp.int32, (V7X_SUBLANES, RG_WIDTH), 0)

    def group(gi, hc):
        r0 = pl.multiple_of(gi * V7X_SUBLANES, V7X_SUBLANES)
        a8 = a_ref[pl.ds(r0, V7X_SUBLANES), :]
        b8 = b_ref[pl.ds(r0, V7X_SUBLANES), :]
        for d in (1, 2, 4):
            keep = row >= d
            b8 = jnp.where(keep, a8 * pltpu.roll(b8, d, axis=0) + b8, b8)
            a8 = jnp.where(keep, a8 * pltpu.roll(a8, d, axis=0), a8)
        h8 = a8 * hc + b8
        h_ref[pl.ds(r0, V7X_SUBLANES), :] = h8
        return h8[V7X_SUBLANES - 1:V7X_SUBLANES, :]

    hc = lax.fori_loop(0, tl // V7X_SUBLANES, group, hc_ref[...], unroll=4)
    hc_ref[...] = hc
    hlast_ref[...] = hc
    convn_ref[...] = xe_ref[tl + pad - 3:tl + pad, :]
    y = h_ref[...] * _gelu_tanh(zg_ref[...])
    y_ref[...] = _rms(y, gout_ref[...]).astype(y_ref.dtype)


def _rglru_seq(z, conv0, h0, cw, cb, wgate, ba, bx, lam, gout, *, batch, seq, tl, name,
               side_xattn=None):
    nt = seq // tl
    assert seq % tl == 0
    w = RG_WIDTH
    row = lambda v: v.reshape(1, w)
    const2 = lambda b, t: (0, 0)
    in_specs = [
        pl.BlockSpec((tl, w), lambda b, t: (b * nt + t, OFF_RGX // w)),
        pl.BlockSpec((tl, w), lambda b, t: (b * nt + t, OFF_RGG // w)),
        pl.BlockSpec((None, CONV_W - 1, w), lambda b, t: (b, 0, 0)),
        pl.BlockSpec((None, 1, w), lambda b, t: (b, 0, 0)),
        pl.BlockSpec((CONV_W, w), const2),
        pl.BlockSpec((1, w), const2),
        pl.BlockSpec((RG_BLOCKS, RG_BLOCK, 2 * RG_BLOCK), lambda b, t: (0, 0, 0)),
        pl.BlockSpec((1, w), const2),
        pl.BlockSpec((1, w), const2),
        pl.BlockSpec((1, w), const2),
        pl.BlockSpec((1, w), const2),
    ]
    out_specs = [
        pl.BlockSpec((tl, w), lambda b, t: (b * nt + t, 0)),
        pl.BlockSpec((None, 1, w), lambda b, t: (b, 0, 0)),
        pl.BlockSpec((None, CONV_W - 1, w), lambda b, t: (b, 0, 0)),
    ]
    out_shape = [
        jax.ShapeDtypeStruct((batch * seq, w), BF16),
        jax.ShapeDtypeStruct((batch, 1, w), F32),
        jax.ShapeDtypeStruct((batch, CONV_W - 1, w), F32),
    ]
    args = [z, z, conv0, h0.reshape(batch, 1, w), cw, row(cb), wgate, row(ba), row(bx), row(lam),
            row(gout)]
    side_rows = 0
    vmem_mib = 32
    if side_xattn is not None:
        xq, ck, cv = side_xattn
        nb = xq.shape[0]
        assert nb % (batch * nt) == 0 and ck.shape == (nb, N_MEM, XA_HEADS, XA_DH)
        side_rows = nb // (batch * nt)
        q_spec = pl.BlockSpec((side_rows, XA_HEADS, XA_DH), lambda b, t: (b * nt + t, 0, 0))
        cache_spec = pl.BlockSpec((side_rows, N_MEM, XA_HEADS, XA_DH),
                                  lambda b, t: (b * nt + t, 0, 0, 0))
        in_specs += [q_spec, cache_spec, cache_spec]
        out_specs.append(q_spec)
        out_shape.append(jax.ShapeDtypeStruct((nb, XA_HEADS, XA_DH), F32))
        args += [xq.reshape(nb, XA_HEADS, XA_DH), ck, cv]
        vmem_mib = 56
    return pl.pallas_call(
        functools.partial(_rglru_seq_kernel, tl=tl, side_rows=side_rows),
        grid=(batch, nt),
        in_specs=in_specs,
        out_specs=out_specs,
        out_shape=out_shape,
        scratch_shapes=[
            pltpu.VMEM((tl + V7X_SUBLANES, w), F32),
            pltpu.VMEM((tl, w), F32),
            pltpu.VMEM((tl, w), F32),
            pltpu.VMEM((tl, w), F32),
            pltpu.VMEM((1, w), F32),
        ],
        compiler_params=_params(2, vmem_mib),
        name=name,
    )(*args)


def _rglru_step_kernel(zx_ref, zg_ref, conv_ref, h0_ref, cw_ref, cb_ref, wgate_ref, ba_ref,
                       bx_ref, lam_ref, gout_ref, y_ref, hn_ref, convn_ref, a_ref, b_ref):
    w = RG_WIDTH
    x = zx_ref[...]
    xr = (conv_ref[:, 0:w] * cw_ref[0:1, :] + conv_ref[:, w:2 * w] * cw_ref[1:2, :]
          + conv_ref[:, 2 * w:3 * w] * cw_ref[2:3, :] + x * cw_ref[3:4, :]) + cb_ref[...]
    _rg_gates(xr, wgate_ref, ba_ref, bx_ref, lam_ref, a_ref, b_ref)
    h = a_ref[...] * h0_ref[...] + b_ref[...]
    hn_ref[...] = h
    convn_ref[:, 0:2 * w] = conv_ref[:, w:3 * w]
    convn_ref[:, 2 * w:3 * w] = x
    y_ref[...] = _rms(h * _gelu_tanh(zg_ref[...]), gout_ref[...]).astype(y_ref.dtype)


def _rglru_step(z, conv, h0, cw, cb, wgate, ba, bx, lam, gout, *, name):
    nb = z.shape[0]
    w = RG_WIDTH
    row = lambda v: v.reshape(1, w)
    c0 = lambda i: (0, 0)
    return pl.pallas_call(
        _rglru_step_kernel,
        grid=(1,),
        in_specs=[
            pl.BlockSpec((nb, w), lambda i: (0, OFF_RGX // w)),
            pl.BlockSpec((nb, w), lambda i: (0, OFF_RGG // w)),
            pl.BlockSpec((nb, (CONV_W - 1) * w), c0),
            pl.BlockSpec((nb, w), c0),
            pl.BlockSpec((CONV_W, w), c0),
            pl.BlockSpec((1, w), c0),
            pl.BlockSpec((RG_BLOCKS, RG_BLOCK, 2 * RG_BLOCK), lambda i: (0, 0, 0)),
            pl.BlockSpec((1, w), c0),
            pl.BlockSpec((1, w), c0),
            pl.BlockSpec((1, w), c0),
            pl.BlockSpec((1, w), c0),
        ],
        out_specs=[
            pl.BlockSpec((nb, w), c0),
            pl.BlockSpec((nb, w), c0),
            pl.BlockSpec((nb, (CONV_W - 1) * w), c0),
        ],
        out_shape=[
            jax.ShapeDtypeStruct((nb, w), BF16),
            jax.ShapeDtypeStruct((nb, w), F32),
            jax.ShapeDtypeStruct((nb, (CONV_W - 1) * w), F32),
        ],
        scratch_shapes=[pltpu.VMEM((nb, w), F32), pltpu.VMEM((nb, w), F32)],
        compiler_params=_params(1, 32),
        name=name,
    )(z, z, conv.reshape(nb, (CONV_W - 1) * w), h0, cw, row(cb), wgate, row(ba), row(bx),
      row(lam), row(gout))


def _mlstm_seq_kernel(bi_ref, bf_ref, q_ref, k_ref, v_ref, o_ref, zg_ref, g_ref,
                      y_ref, c_ref, n_ref, m_ref, zgt_ref, cs_ref, ns_ref, *, seq, cs):
    h = pl.program_id(1)
    nc = seq // cs
    bi = bi_ref[h]
    bf = bf_ref[h]
    for c in range(nc):
        zgt_ref[c] = zg_ref[c * cs:(c + 1) * cs, :].T
    cs_ref[...] = jnp.zeros_like(cs_ref)
    ns_ref[...] = jnp.zeros_like(ns_ref)

    t_idx = lax.broadcasted_iota(jnp.int32, (cs, cs), 0)
    s_idx = lax.broadcasted_iota(jnp.int32, (cs, cs), 1)
    causal = s_idx <= t_idx
    lane = lax.broadcasted_iota(jnp.int32, (cs, V7X_LANES), 1)

    def chunk(c, m):
        r0 = pl.multiple_of(c * cs, cs)
        q = q_ref[pl.ds(r0, cs), :]
        k = k_ref[pl.ds(r0, cs), :] * (ML_DK ** -0.5)
        v = v_ref[pl.ds(r0, cs), :]
        zg = zg_ref[pl.ds(r0, cs), :]
        li_col = jnp.sum(jnp.where(lane == h, zg, 0.0), axis=1, keepdims=True) + bi
        lf_col = _log_sigmoid(
            jnp.sum(jnp.where(lane == h + ML_HEADS, zg, 0.0), axis=1, keepdims=True) + bf)
        li_row = zgt_ref[c, pl.ds(h, 1), :] + bi
        lf_row = _log_sigmoid(zgt_ref[c, pl.ds(h + ML_HEADS, 1), :] + bf)
        bcum_col = jnp.sum(jnp.where(causal, lf_row, 0.0), axis=1, keepdims=True)
        bcum_row = jnp.sum(jnp.where(t_idx <= s_idx, lf_col, 0.0), axis=0, keepdims=True)
        log_d = jnp.where(causal, bcum_col - bcum_row + li_row, NEG)
        inter = bcum_col + m
        m_t = jnp.maximum(inter, jnp.max(log_d, axis=1, keepdims=True))
        dmat = jnp.exp(log_d - m_t)
        sc = jnp.exp(inter - m_t)
        qb = q.astype(BF16)
        kb = k.astype(BF16)
        vb = v.astype(BF16)
        qk = lax.dot_general(qb, kb, (((1,), (1,)), ((), ())), preferred_element_type=F32) * dmat
        c_old = cs_ref[...]
        n_old = ns_ref[...]
        num = sc * jnp.dot(qb, c_old.astype(BF16), preferred_element_type=F32) + _mm(qk, vb)
        den = sc * jnp.sum(q * n_old, axis=1, keepdims=True) + jnp.sum(qk, axis=1, keepdims=True)
        den = jnp.maximum(jnp.abs(den), jnp.exp(-m_t))
        hh = num / den
        m_new = m_t[cs - 1:cs, :]
        b_last = bcum_col[cs - 1:cs, :]
        w_end = jnp.exp(b_last - bcum_col + li_col - m_new)
        dec = jnp.exp(b_last + m - m_new)
        wk = w_end * k
        cs_ref[...] = dec * c_old + lax.dot_general(
            wk.astype(BF16), vb, (((0,), (0,)), ((), ())), preferred_element_type=F32)
        ns_ref[...] = dec * n_old + jnp.sum(wk, axis=0, keepdims=True)
        y = _rms(hh, g_ref[...]) * _sigmoid(o_ref[pl.ds(r0, cs), :])
        y_ref[pl.ds(r0, cs), :] = y.astype(y_ref.dtype)
        return m_new

    m_fin = lax.fori_loop(0, nc, chunk, jnp.zeros((1, 1), F32))
    c_ref[...] = cs_ref[...]
    n_ref[pl.ds(h, 1), :] = ns_ref[...]
    m_ref[pl.ds(h, 1), :] = jnp.broadcast_to(m_fin, (1, V7X_LANES))


def _mlstm_seq(z, zg, b_i, b_f, g_out, *, batch, seq, cs, name):
    assert seq % cs == 0 and zg.shape[1] == V7X_LANES
    dk, dv, nh = ML_DK, ML_DV, ML_HEADS
    smem = pl.BlockSpec(memory_space=pltpu.SMEM)
    return pl.pallas_call(
        functools.partial(_mlstm_seq_kernel, seq=seq, cs=cs),
        grid=(batch, nh),
        in_specs=[
            smem, smem,
            pl.BlockSpec((seq, dk), lambda b, h: (b, OFF_Q // dk + h)),
            pl.BlockSpec((seq, dk), lambda b, h: (b, OFF_K // dk + h)),
            pl.BlockSpec((seq, dv), lambda b, h: (b, OFF_V // dv + h)),
            pl.BlockSpec((seq, dv), lambda b, h: (b, OFF_O // dv + h)),
            pl.BlockSpec((seq, V7X_LANES), lambda b, h: (b, 0)),
            pl.BlockSpec((1, dv), lambda b, h: (0, h)),
        ],
        out_specs=[
            pl.BlockSpec((seq, dv), lambda b, h: (b, h)),
            pl.BlockSpec((None, None, dk, dv), lambda b, h: (b, h, 0, 0)),
            pl.BlockSpec((None, nh, dk), lambda b, h: (b, 0, 0)),
            pl.BlockSpec((None, nh, V7X_LANES), lambda b, h: (b, 0, 0)),
        ],
        out_shape=[
            jax.ShapeDtypeStruct((batch * seq, nh * dv), BF16),
            jax.ShapeDtypeStruct((batch, nh, dk, dv), F32),
            jax.ShapeDtypeStruct((batch, nh, dk), F32),
            jax.ShapeDtypeStruct((batch, nh, V7X_LANES), F32),
        ],
        scratch_shapes=[
            pltpu.VMEM((seq // cs, V7X_LANES, cs), F32),
            pltpu.VMEM((dk, dv), F32),
            pltpu.VMEM((1, dk), F32),
        ],
        compiler_params=_params(2, 40),
        name=name,
    )(b_i, b_f, z, z, z, z, zg, g_out.reshape(1, nh * dv))


def _mlstm_step_kernel(bi_ref, bf_ref, q_ref, k_ref, v_ref, o_ref, zg_ref, g_ref, c0_ref, n0_ref,
                       m0_ref, y_ref, c_ref, n_ref, m_ref, qc_ref, *, bs):
    dk, dv = ML_DK, ML_DV
    eye = (lax.broadcasted_iota(jnp.int32, (dk, dk), 0)
           == lax.broadcasted_iota(jnp.int32, (dk, dk), 1))
    zg = zg_ref[...]
    for h in range(ML_HEADS):
        li = zg[:, h:h + 1] + bi_ref[h]
        lf = _log_sigmoid(zg[:, ML_HEADS + h:ML_HEADS + h + 1] + bf_ref[h])
        m = m0_ref[:, h:h + 1]
        inter = lf + m
        m_t = jnp.maximum(inter, li)
        dgate = jnp.exp(li - m_t)
        sc = jnp.exp(inter - m_t)
        q = q_ref[:, h * dk:(h + 1) * dk]
        k = k_ref[:, h * dk:(h + 1) * dk] * (ML_DK ** -0.5)
        v = v_ref[:, h * dv:(h + 1) * dv]
        n_old = n0_ref[:, h, :]
        qk = jnp.sum(q * k, axis=1, keepdims=True) * dgate
        w_end = jnp.exp(li - m_t)
        dec = jnp.exp(inter - m_t)
        wk = w_end * k
        for j in range(bs):
            c_old = c0_ref[j, h]
            qc_ref[j:j + 1, :] = _mm(q[j:j + 1, :], c_old)
            wk_col = jnp.sum(
                jnp.where(eye, jnp.broadcast_to(wk[j:j + 1, :], (dk, dk)), 0.0),
                axis=1, keepdims=True)
            c_ref[j, h] = dec[j:j + 1, :] * c_old + wk_col * v[j:j + 1, :]
        num = sc * qc_ref[...] + qk * v
        den = sc * jnp.sum(q * n_old, axis=1, keepdims=True) + qk
        den = jnp.maximum(jnp.abs(den), jnp.exp(-m_t))
        hh = num / den
        n_ref[:, h, :] = dec * n_old + wk
        m_ref[:, h:h + 1] = m_t
        y = _rms(hh, g_ref[:, h * dv:(h + 1) * dv]) * _sigmoid(o_ref[:, h * dv:(h + 1) * dv])
        y_ref[:, h * dv:(h + 1) * dv] = y.astype(y_ref.dtype)


def _mlstm_step(z, zg, c0, n0, m0, b_i, b_f, g_out, *, bs, name):
    nb = z.shape[0]
    assert nb % bs == 0
    dk, dv, nh = ML_DK, ML_DV, ML_HEADS
    smem = pl.BlockSpec(memory_space=pltpu.SMEM)
    return pl.pallas_call(
        functools.partial(_mlstm_step_kernel, bs=bs),
        grid=(nb // bs,),
        in_specs=[
            smem, smem,
            pl.BlockSpec((bs, nh * dk), lambda i: (i, OFF_Q // (nh * dk))),
            pl.BlockSpec((bs, nh * dk), lambda i: (i, OFF_K // (nh * dk))),
            pl.BlockSpec((bs, nh * dv), lambda i: (i, OFF_V // (nh * dv))),
            pl.BlockSpec((bs, nh * dv), lambda i: (i, OFF_O // (nh * dv))),
            pl.BlockSpec((bs, V7X_LANES), lambda i: (i, 0)),
            pl.BlockSpec((1, nh * dv), lambda i: (0, 0)),
            pl.BlockSpec((bs, nh, dk, dv), lambda i: (i, 0, 0, 0)),
            pl.BlockSpec((bs, nh, dk), lambda i: (i, 0, 0)),
            pl.BlockSpec((bs, nh), lambda i: (i, 0)),
        ],
        out_specs=[
            pl.BlockSpec((bs, nh * dv), lambda i: (i, 0)),
            pl.BlockSpec((bs, nh, dk, dv), lambda i: (i, 0, 0, 0)),
            pl.BlockSpec((bs, nh, dk), lambda i: (i, 0, 0)),
            pl.BlockSpec((bs, nh), lambda i: (i, 0)),
        ],
        out_shape=[
            jax.ShapeDtypeStruct((nb, nh * dv), F32),
            jax.ShapeDtypeStruct((nb, nh, dk, dv), F32),
            jax.ShapeDtypeStruct((nb, nh, dk), F32),
            jax.ShapeDtypeStruct((nb, nh), F32),
        ],
        scratch_shapes=[pltpu.VMEM((bs, dv), F32)],
        compiler_params=_params(1, 32),
        name=name,
    )(b_i, b_f, z, z, z, z, zg, g_out.reshape(1, nh * dv), c0, n0, m0)


def _softmax_rows(s):
    e = jnp.exp(s - jnp.max(s, axis=-1, keepdims=True))
    return e / jnp.sum(e, axis=-1, keepdims=True)


def _xattn_seq_kernel(q_ref, k_ref, v_ref, o_ref):
    for h in range(XA_HEADS):
        sl = slice(h * XA_DH, (h + 1) * XA_DH)
        s = lax.dot_general(q_ref[:, sl].astype(BF16), k_ref[:, sl].astype(BF16),
                            (((1,), (1,)), ((), ())), preferred_element_type=F32)
        p = _softmax_rows(s * (XA_DH ** -0.5))
        o_ref[:, sl] = _mm(p, v_ref[:, sl]).astype(o_ref.dtype)


def _xattn_seq(xq, mk, mv, *, batch, seq, tq, name):
    nt = seq // tq
    assert seq % tq == 0
    d = D_MODEL
    return pl.pallas_call(
        _xattn_seq_kernel,
        grid=(batch, nt),
        in_specs=[
            pl.BlockSpec((tq, d), lambda b, t: (b * nt + t, 0)),
            pl.BlockSpec((N_MEM, d), lambda b, t: (b, 0)),
            pl.BlockSpec((N_MEM, d), lambda b, t: (b, 0)),
        ],
        out_specs=pl.BlockSpec((tq, d), lambda b, t: (b * nt + t, 0)),
        out_shape=jax.ShapeDtypeStruct((batch * seq, d), BF16),
        compiler_params=_params(2, 40),
        name=name,
    )(xq, mk, mv)


def _xattn_own_mask():
    nrow = N_MEM * XA_HEADS
    col_head = lax.broadcasted_iota(jnp.int32, (V7X_SUBLANES, nrow), 1) & (XA_HEADS - 1)
    row_head = lax.broadcasted_iota(jnp.int32, (V7X_SUBLANES, nrow), 0) & (XA_HEADS - 1)
    return col_head == row_head


def _xattn_probs(q_ref, k_ref, j, own):
    kf = k_ref[j].reshape(N_MEM * XA_HEADS, XA_DH)
    q8 = jnp.concatenate([q_ref[j]] * (V7X_SUBLANES // XA_HEADS), axis=0)
    s = lax.dot_general(q8.astype(BF16), kf.astype(BF16), (((1,), (1,)), ((), ())),
                        preferred_element_type=F32)
    s = jnp.where(own, s * (XA_DH ** -0.5), NEG)
    e = jnp.where(own, jnp.exp(s - jnp.max(s, axis=-1, keepdims=True)), 0.0)
    return e / jnp.sum(e, axis=-1, keepdims=True)


def _xattn_values(v_ref, o_ref, j, p):
    vf = v_ref[j].reshape(N_MEM * XA_HEADS, XA_DH)
    o_ref[j] = _mm(p, vf)[0:XA_HEADS, :]


def _xattn_step_kernel(q_ref, k_ref, v_ref, o_ref, *, sb):
    own = _xattn_own_mask()
    for j in range(sb):
        _xattn_values(v_ref, o_ref, j, _xattn_probs(q_ref, k_ref, j, own))


def _xattn_step(xq, ck, cv, *, sb, name):
    nb = xq.shape[0]
    assert XA_HEADS & (XA_HEADS - 1) == 0
    assert nb % sb == 0 and ck.shape == (nb, N_MEM, XA_HEADS, XA_DH)
    cache_spec = pl.BlockSpec((sb, N_MEM, XA_HEADS, XA_DH), lambda i: (i, 0, 0, 0))
    q_spec = pl.BlockSpec((sb, XA_HEADS, XA_DH), lambda i: (i, 0, 0))
    return pl.pallas_call(
        functools.partial(_xattn_step_kernel, sb=sb),
        grid=(nb // sb,),
        in_specs=[q_spec, cache_spec, cache_spec],
        out_specs=q_spec,
        out_shape=jax.ShapeDtypeStruct((nb, XA_HEADS, XA_DH), F32),
        compiler_params=_params(1, 40),
        name=name,
    )(xq.reshape(nb, XA_HEADS, XA_DH), ck, cv).reshape(nb, XA_HEADS * XA_DH)


def _tiles(rows):
    tm = min(rows, 1024)
    assert rows % tm == 0
    return tm


def kernel(x_prompt, x_sample, mem_prompt, state_rg_h, state_rg_conv, state_ml_C, state_ml_n, state_ml_m, cache_mem_k, cache_mem_v, g_mix, w_in, conv_w, conv_b, w_rg_a, b_rg_a, w_rg_x, b_rg_x, rg_lambda, b_ml_i, b_ml_f, g_rg_out, g_ml_out, w_out, g_xa, g_mem, w_xa_q, w_xa_k, w_xa_v, w_xa_o, g_ffn, w_ffn_gate, w_ffn_up, w_ffn_down, g_final):
    depth = g_mix.shape[0]
    assert depth == 1, "single trunk layer"
    bp, seq, d = x_prompt.shape
    bs_, dec_seq, _ = x_sample.shape
    assert d == D_MODEL and dec_seq == 1
    n_mem = mem_prompt.shape[1]
    assert n_mem == N_MEM
    dff = w_ffn_gate.shape[-1]
    in_w = w_in.shape[-1]
    assert in_w == IN_MAIN + N_GATE

    w_in_t = jnp.swapaxes(w_in, 1, 2).reshape(in_w, d)
    w_gate_pad = jnp.pad(w_in_t[IN_MAIN:], ((0, V7X_LANES - N_GATE), (0, 0)))
    cw = conv_w.reshape(CONV_W, RG_WIDTH)
    wgate = jnp.concatenate([w_rg_a.reshape(RG_BLOCKS, RG_BLOCK, RG_BLOCK),
                             w_rg_x.reshape(RG_BLOCKS, RG_BLOCK, RG_BLOCK)], axis=-1)
    rg_args = (cw, conv_b.reshape(-1), wgate, b_rg_a.reshape(-1), b_rg_x.reshape(-1),
               rg_lambda.reshape(-1), g_rg_out.reshape(-1))
    b_i = b_ml_i.reshape(ML_HEADS)
    b_f = b_ml_f.reshape(ML_HEADS)
    g_ml = g_ml_out.reshape(-1)
    w_out2 = w_out.reshape(d, d)
    w_q = w_xa_q.reshape(d, d)
    w_k = w_xa_k.reshape(d, d)
    w_v = w_xa_v.reshape(d, d)
    w_o = w_xa_o.reshape(d, d)
    w_fg = w_ffn_gate.reshape(d, dff)
    w_fu = w_ffn_up.reshape(d, dff)
    w_fd = w_ffn_down.reshape(dff, d)

    row_tile = _tiles(bp * seq)
    rg_zero_conv = jnp.zeros((bp, CONV_W - 1, RG_WIDTH), F32)
    rg_zero_h = jnp.zeros((bp, RG_WIDTH), F32)

    xs = x_sample.reshape(bs_, d)
    z_s, zg_s, w_in16 = _norm_linear(
        xs, g_mix.reshape(-1), w_in_t, n_out=IN_MAIN, tm=bs_, tn=1024, w_gate=w_gate_pad,
        w_is_nk=True, emit_w16=True, name="in_proj_s")
    y_rg_s, s_h, s_conv = _rglru_step(
        z_s, state_rg_conv.reshape(bs_, CONV_W - 1, RG_WIDTH), state_rg_h.reshape(bs_, RG_WIDTH),
        *rg_args, name="rglru_s")
    y_ml_s, s_c, s_n, s_m = _mlstm_step(
        z_s, zg_s, state_ml_C.reshape(bs_, ML_HEADS, ML_DK, ML_DV),
        state_ml_n.reshape(bs_, ML_HEADS, ML_DK), state_ml_m.reshape(bs_, ML_HEADS),
        b_i, b_f, g_ml, bs=V7X_SUBLANES, name="mlstm_s")
    x1_s, w_out16_rg, w_out16_ml = _linear_residual(
        [y_rg_s, y_ml_s], [(w_out2, 0), (w_out2, 1)], xs, tm=bs_, tn=1024, emit_w16=True,
        name="mix_out_s")
    xq_s, w_q16 = _norm_linear(x1_s, g_xa.reshape(-1), w_q, n_out=d, tm=bs_, tn=1024,
                               emit_w16=True, name="xa_q_s")
    ck = cache_mem_k.reshape(bs_, n_mem, XA_HEADS, XA_DH)
    cv = cache_mem_v.reshape(bs_, n_mem, XA_HEADS, XA_DH)

    tp = bp * seq
    xp = x_prompt.reshape(tp, d)
    z_p, zg_p = _norm_linear(xp, g_mix.reshape(-1), w_in16, n_out=IN_MAIN, tm=row_tile, tn=1024,
                             w_gate=w_gate_pad, w_is_nk=True, name="in_proj_p")
    rg_tl = min(seq, 256)
    rg_steps = bp * (seq // rg_tl)
    if bs_ % rg_steps == 0 and RG_BLOCKS % (bs_ // rg_steps) == 0:
        y_rg_p, p_h, p_conv, o_s = _rglru_seq(
            z_p, rg_zero_conv, rg_zero_h, *rg_args, batch=bp, seq=seq, tl=rg_tl,
            side_xattn=(xq_s, ck, cv), name="rglru_p")
        o_s = o_s.reshape(bs_, d)
    else:
        y_rg_p, p_h, p_conv = _rglru_seq(z_p, rg_zero_conv, rg_zero_h, *rg_args, batch=bp,
                                         seq=seq, tl=rg_tl, name="rglru_p")
        o_s = _xattn_step(xq_s, ck, cv, sb=2, name="xattn_s")

    x3_s, w_o16 = _linear_residual([o_s], [(w_o, 0)], x1_s, tm=bs_, tn=1024, emit_w16=True,
                                   name="xa_out_s")
    y_s, w_fg16, w_fu16, w_fd16 = _ffn(x3_s, g_ffn.reshape(-1), w_fg, w_fu, w_fd, g_final,
                                       tm=bs_, tf=512, emit_w16=True, vmem_mib=48, name="ffn_s")

    y_ml_p, p_c, p_n, p_m = _mlstm_seq(z_p, zg_p, b_i, b_f, g_ml, batch=bp, seq=seq,
                                       cs=min(seq, ML_CHUNK), name="mlstm_p")
    mem2 = mem_prompt.reshape(bp * n_mem, d)
    tmem = min(bp * n_mem, 256)
    mk, mk_heads = _norm_linear(mem2, g_mem.reshape(-1), w_k, n_out=d, tm=tmem, tn=d,
                                heads_out=(XA_HEADS, XA_DH), name="mem_k")
    mv, mv_heads = _norm_linear(mem2, g_mem.reshape(-1), w_v, n_out=d, tm=tmem, tn=d,
                                heads_out=(XA_HEADS, XA_DH), name="mem_v")
    proj_tile = min(row_tile, 512)
    x1_p = _linear_residual([y_rg_p, y_ml_p], [(w_out16_rg, 0), (w_out16_ml, 0)], xp,
                            tm=proj_tile, tn=d, name="mix_out_p")
    xq_p = _norm_linear(x1_p, g_xa.reshape(-1), w_q16, n_out=d, tm=proj_tile, tn=d,
                        out_dtype=BF16, name="xa_q_p")
    o_p = _xattn_seq(xq_p, mk, mv, batch=bp, seq=seq, tq=min(seq, 1024), name="xattn_p")
    x3_p = _linear_residual([o_p], [(w_o16, 0)], x1_p, tm=proj_tile, tn=d, name="xa_out_p")
    y_p = _ffn(x3_p, g_ffn.reshape(-1), w_fg16, w_fu16, w_fd16, g_final, tm=row_tile, tf=512,
               vmem_mib=60, name="ffn_p")

    return (
        y_p.reshape(bp, seq, d),
        y_s.reshape(bs_, 1, d),
        p_h.reshape(1, bp, RG_WIDTH),
        p_conv.reshape(1, bp, CONV_W - 1, RG_WIDTH),
        p_c.reshape(1, bp, ML_HEADS, ML_DK, ML_DV),
        p_n.reshape(1, bp, ML_HEADS, ML_DK),
        p_m[:, :, 0].reshape(1, bp, ML_HEADS),
        mk_heads.reshape(1, bp, n_mem, XA_HEADS, XA_DH),
        mv_heads.reshape(1, bp, n_mem, XA_HEADS, XA_DH),
        s_h.reshape(1, bs_, RG_WIDTH),
        s_conv.reshape(1, bs_, CONV_W - 1, RG_WIDTH),
        s_c.reshape(1, bs_, ML_HEADS, ML_DK, ML_DV),
        s_n.reshape(1, bs_, ML_HEADS, ML_DK),
        s_m.reshape(1, bs_, ML_HEADS),
    )
```

```python
import functools

import jax
import jax.numpy as jnp
from jax import lax
from jax.experimental import pallas as pl
from jax.experimental.pallas import tpu as pltpu

F32 = jnp.float32
BF16 = jnp.bfloat16

D_MODEL = 2048
RG_WIDTH = D_MODEL // 2
RG_BLOCKS = 8
RG_BLOCK = RG_WIDTH // RG_BLOCKS
CONV_W = 4
RG_C = 8.0
ML_HEADS = 4
ML_WIDTH = D_MODEL - RG_WIDTH
ML_DV = ML_WIDTH // ML_HEADS
ML_DK = ML_DV // 2
N_MEM = 256
XA_HEADS = 4
XA_DH = D_MODEL // XA_HEADS
EPS = 1e-6
NEG = -1e30

OFF_RGX = 0
OFF_RGG = OFF_RGX + RG_WIDTH
OFF_Q = OFF_RGG + RG_WIDTH
OFF_K = OFF_Q + ML_HEADS * ML_DK
OFF_V = OFF_K + ML_HEADS * ML_DK
OFF_O = OFF_V + ML_WIDTH
OFF_I = OFF_O + ML_WIDTH
IN_MAIN = OFF_I
N_GATE = 2 * ML_HEADS

V7X_LANES = 128
V7X_SUBLANES = 8
V7X_VMEM_BYTES = 64 * 2**20

ML_CHUNK = 256


def _params(n_axes, vmem_mib):
    assert vmem_mib * 2**20 <= V7X_VMEM_BYTES
    return pltpu.CompilerParams(
        dimension_semantics=("arbitrary",) * n_axes,
        vmem_limit_bytes=vmem_mib * 2**20,
    )


def _rms(x, g):
    ms = jnp.mean(x * x, axis=-1, keepdims=True)
    return x * lax.rsqrt(ms + EPS) * g


def _softplus(u):
    return jnp.maximum(u, 0.0) + jnp.log1p(jnp.exp(-jnp.abs(u)))


def _log_sigmoid(u):
    return -_softplus(-u)


def _sigmoid(u):
    return 0.5 * jnp.tanh(0.5 * u) + 0.5


def _gelu_tanh(x):
    c = 0.7978845608028654
    half_x = 0.5 * x
    return half_x + half_x * jnp.tanh(x * (c + (c * 0.044715) * (x * x)))


def _sqrt_nonneg(v):
    return jnp.where(v > 0.0, v * lax.rsqrt(v), 0.0)


def _mm(a, b):
    return jnp.dot(a.astype(BF16), b.astype(BF16), preferred_element_type=F32)


def _dot_w(a, w_ref, w_is_nk):
    w = w_ref[...].astype(BF16)
    if w_is_nk:
        return lax.dot_general(a, w, (((1,), (1,)), ((), ())), preferred_element_type=F32)
    return jnp.dot(a, w, preferred_element_type=F32)


def _norm_linear_kernel(*refs, with_gate, w_is_nk, emit_w16, heads_out):
    refs = list(refs)
    x_ref, g_ref, w_ref = refs[:3]
    wg_ref = refs[3] if with_gate else None
    outs = refs[3 + with_gate:-1]
    xn_ref = refs[-1]
    o_ref = outs[0]
    oh_ref = outs[1] if heads_out else None
    og_ref = outs[1 + bool(heads_out)] if with_gate else None

    def column_tile(xn):
        acc = _dot_w(xn, w_ref, w_is_nk)
        o_ref[...] = acc.astype(o_ref.dtype)
        if heads_out:
            oh_ref[...] = acc.reshape(oh_ref.shape)
        if emit_w16:
            outs[-1][...] = w_ref[...].astype(BF16)

    @pl.when(pl.program_id(1) == 0)
    def _():
        xn = _rms(x_ref[...], g_ref[...]).astype(BF16)
        xn_ref[...] = xn
        if with_gate:
            og_ref[...] = _dot_w(xn, wg_ref, w_is_nk)
        column_tile(xn)

    @pl.when(pl.program_id(1) > 0)
    def _():
        column_tile(xn_ref[...])


def _norm_linear(x, g, w, *, n_out, tm, tn, out_dtype=F32, w_gate=None, w_is_nk=False,
                 emit_w16=False, heads_out=None, vmem_mib=48, name):
    m, k = x.shape
    k_ax, n_ax = (1, 0) if w_is_nk else (0, 1)
    assert m % tm == 0 and n_out % tn == 0 and w.shape[k_ax] == k and n_out <= w.shape[n_ax]
    assert not emit_w16 or m == tm
    assert heads_out is None or (tn == n_out and heads_out[0] * heads_out[1] == n_out)
    with_gate = w_gate is not None
    w_mode = dict(pipeline_mode=pl.Buffered(1)) if tn == n_out else {}
    w_spec = (pl.BlockSpec((tn, k), lambda i, j: (j, 0), **w_mode) if w_is_nk
              else pl.BlockSpec((k, tn), lambda i, j: (0, j), **w_mode))
    in_specs = [
        pl.BlockSpec((tm, k), lambda i, j: (i, 0)),
        pl.BlockSpec((1, k), lambda i, j: (0, 0)),
        w_spec,
    ]
    out_shape = [jax.ShapeDtypeStruct((m, n_out), out_dtype)]
    out_specs = [pl.BlockSpec((tm, tn), lambda i, j: (i, j))]
    args = [x, g.reshape(1, k), w]
    if heads_out:
        out_shape.append(jax.ShapeDtypeStruct((m, *heads_out), F32))
        out_specs.append(pl.BlockSpec((tm, *heads_out), lambda i, j: (i, 0, 0)))
    if with_gate:
        ng = w_gate.shape[n_ax]
        in_specs.append(pl.BlockSpec(w_gate.shape, lambda i, j: (0, 0)))
        out_shape.append(jax.ShapeDtypeStruct((m, ng), F32))
        out_specs.append(pl.BlockSpec((tm, ng), lambda i, j: (i, 0)))
        args.append(w_gate)
    if emit_w16:
        out_shape.append(jax.ShapeDtypeStruct((n_out, k) if w_is_nk else (k, n_out), BF16))
        out_specs.append(pl.BlockSpec((tn, k), lambda i, j: (j, 0)) if w_is_nk
                         else pl.BlockSpec((k, tn), lambda i, j: (0, j)))
    out = pl.pallas_call(
        functools.partial(_norm_linear_kernel, with_gate=with_gate, w_is_nk=w_is_nk,
                          emit_w16=emit_w16, heads_out=heads_out),
        grid=(m // tm, n_out // tn),
        in_specs=in_specs,
        out_specs=out_specs,
        out_shape=out_shape,
        scratch_shapes=[pltpu.VMEM((tm, k), BF16)],
        compiler_params=_params(2, vmem_mib),
        name=name,
    )(*args)
    return out if len(out) > 1 else out[0]


def _linear_res_kernel(*refs, n_in, emit_w16):
    a_refs = refs[:n_in]
    w_refs = refs[n_in:2 * n_in]
    res_ref = refs[2 * n_in]
    o_ref = refs[2 * n_in + 1]
    acc = res_ref[...]
    for a_ref, w_ref in zip(a_refs, w_refs):
        acc = acc + _mm(a_ref[...], w_ref[...])
    o_ref[...] = acc
    if emit_w16:
        for w_ref, wc_ref in zip(w_refs, refs[2 * n_in + 2:]):
            wc_ref[...] = w_ref[...].astype(BF16)


def _linear_residual(parts, weights, res, *, tm, tn, emit_w16=False, vmem_mib=48, name):
    m, n = res.shape
    kp = parts[0].shape[1]
    assert all(p.shape == (m, kp) for p in parts) and len(weights) == len(parts)
    assert m % tm == 0 and n % tn == 0 and (not emit_w16 or m == tm)
    n_in = len(parts)
    in_specs = [pl.BlockSpec((tm, kp), lambda i, j: (i, 0)) for _ in parts]
    w_mode = dict(pipeline_mode=pl.Buffered(1)) if tn == n else {}
    in_specs += [pl.BlockSpec((kp, tn), lambda i, j, rb=rb: (rb, j), **w_mode)
                 for _, rb in weights]
    in_specs.append(pl.BlockSpec((tm, tn), lambda i, j: (i, j)))
    out_specs = [pl.BlockSpec((tm, tn), lambda i, j: (i, j))]
    out_shape = [jax.ShapeDtypeStruct((m, n), F32)]
    if emit_w16:
        out_specs += [pl.BlockSpec((kp, tn), lambda i, j: (0, j)) for _ in parts]
        out_shape += [jax.ShapeDtypeStruct((kp, n), BF16) for _ in parts]
    out = pl.pallas_call(
        functools.partial(_linear_res_kernel, n_in=n_in, emit_w16=emit_w16),
        grid=(m // tm, n // tn),
        in_specs=in_specs,
        out_specs=out_specs,
        out_shape=out_shape,
        compiler_params=_params(2, vmem_mib),
        name=name,
    )(*parts, *[w for w, _ in weights], res)
    return out if emit_w16 else out[0]


def _ffn_kernel(x_ref, g_ref, wg_ref, wu_ref, wd_ref, gf_ref, o_ref, *rest, emit_w16):
    xf_ref = rest[-1]
    f = pl.program_id(1)
    last = pl.num_programs(1) - 1

    def hidden_tile(xf):
        wg = wg_ref[...].astype(BF16)
        wu = wu_ref[...].astype(BF16)
        wd = wd_ref[...].astype(BF16)
        if emit_w16:
            for dst, val in zip(rest[:3], (wg, wu, wd)):
                dst[...] = val
        gate = jnp.dot(xf, wg, preferred_element_type=F32)
        up = jnp.dot(xf, wu, preferred_element_type=F32)
        hidden = (gate * _sigmoid(gate)) * up
        return jnp.dot(hidden.astype(BF16), wd, preferred_element_type=F32)

    @pl.when(f == 0)
    def _():
        x = x_ref[...]
        xf = _rms(x, g_ref[...]).astype(BF16)
        xf_ref[...] = xf
        o_ref[...] = x + hidden_tile(xf)

    @pl.when((f > 0) & (f < last))
    def _():
        o_ref[...] += hidden_tile(xf_ref[...])

    @pl.when(f == last)
    def _():
        o_ref[...] = _rms(o_ref[...] + hidden_tile(xf_ref[...]), gf_ref[...])


def _ffn(x, g, w_gate, w_up, w_down, g_final, *, tm, tf, emit_w16=False, vmem_mib, name):
    m, d = x.shape
    dff = w_gate.shape[1]
    assert m % tm == 0 and dff % tf == 0 and dff // tf >= 2 and (not emit_w16 or m == tm)
    up_spec = pl.BlockSpec((d, tf), lambda i, f: (0, f))
    down_spec = pl.BlockSpec((tf, d), lambda i, f: (f, 0))
    out_specs = [pl.BlockSpec((tm, d), lambda i, f: (i, 0))]
    out_shape = [jax.ShapeDtypeStruct((m, d), F32)]
    if emit_w16:
        out_specs += [up_spec, up_spec, down_spec]
        out_shape += [jax.ShapeDtypeStruct(w.shape, BF16) for w in (w_gate, w_up, w_down)]
    out = pl.pallas_call(
        functools.partial(_ffn_kernel, emit_w16=emit_w16),
        grid=(m // tm, dff // tf),
        in_specs=[
            pl.BlockSpec((tm, d), lambda i, f: (i, 0)),
            pl.BlockSpec((1, d), lambda i, f: (0, 0)),
            up_spec,
            up_spec,
            down_spec,
            pl.BlockSpec((1, d), lambda i, f: (0, 0)),
        ],
        out_specs=out_specs,
        out_shape=out_shape,
        scratch_shapes=[pltpu.VMEM((tm, d), BF16)],
        compiler_params=_params(2, vmem_mib),
        name=name,
    )(x, g.reshape(1, d), w_gate, w_up, w_down, g_final.reshape(1, d))
    return out if emit_w16 else out[0]


RG_GATE_LOOKAHEAD = 2


def _rg_gates(xr, wgate_ref, ba_ref, bx_ref, lam_ref, a_ref, b_ref, side=None):
    blocks = [slice(n * RG_BLOCK, (n + 1) * RG_BLOCK) for n in range(RG_BLOCKS)]
    gates = {}

    def issue_gate(n):
        if n < RG_BLOCKS:
            gates[n] = _mm(xr[:, blocks[n]], wgate_ref[n])

    rows, score_phase, value_phase = side if side else (0, None, None)
    per = RG_BLOCKS // rows if rows else RG_BLOCKS
    scores = {}
    for n in range(RG_GATE_LOOKAHEAD):
        issue_gate(n)
    for n, sl in enumerate(blocks):
        if rows and n % per == 0:
            j = n // per
            scores[j] = score_phase(j)
            if j > 0:
                value_phase(j - 1, scores.pop(j - 1))
        xn = xr[:, sl]
        g = gates.pop(n)
        issue_gate(n + RG_GATE_LOOKAHEAD)
        r = _sigmoid(g[:, :RG_BLOCK] + ba_ref[:, sl])
        ig = _sigmoid(g[:, RG_BLOCK:] + bx_ref[:, sl])
        a = jnp.exp(r * (-RG_C * _softplus(-lam_ref[:, sl])))
        a_ref[:, sl] = a
        mult = _sqrt_nonneg(jnp.maximum(1.0 - a * a, 0.0))
        b_ref[:, sl] = mult * (ig * xn)
    if rows:
        value_phase(rows - 1, scores.pop(rows - 1))


def _rglru_seq_kernel(*refs, tl, side_rows):
    (zx_ref, zg_ref, conv0_ref, h0_ref, cw_ref, cb_ref, wgate_ref, ba_ref, bx_ref, lam_ref,
     gout_ref) = refs[:11]
    xe_ref, a_ref, b_ref, h_ref, hc_ref = refs[-5:]
    if side_rows:
        sq_ref, sk_ref, sv_ref, y_ref, hlast_ref, convn_ref, so_ref = refs[11:-5]
    else:
        y_ref, hlast_ref, convn_ref = refs[11:-5]
    t = pl.program_id(1)
    pad = V7X_SUBLANES

    @pl.when(t == 0)
    def _():
        xe_ref[pad - 3:pad, :] = conv0_ref[...]
        hc_ref[...] = h0_ref[...]

    @pl.when(t > 0)
    def _():
        xe_ref[pad - 3:pad, :] = xe_ref[tl + pad - 3:tl + pad, :]

    x = zx_ref[...]
    xe_ref[pad:tl + pad, :] = x
    xr = (xe_ref[pad - 3:tl + pad - 3, :] * cw_ref[0:1, :]
          + xe_ref[pad - 2:tl + pad - 2, :] * cw_ref[1:2, :]
          + xe_ref[pad - 1:tl + pad - 1, :] * cw_ref[2:3, :]
          + x * cw_ref[3:4, :]) + cb_ref[...]
    side = None
    if side_rows:
        own = _xattn_own_mask()
        side = (side_rows,
                lambda j: _xattn_probs(sq_ref, sk_ref, j, own),
                lambda j, p: _xattn_values(sv_ref, so_ref, j, p))

    _rg_gates(xr, wgate_ref, ba_ref, bx_ref, lam_ref, a_ref, b_ref, side=side)

    row = lax.broadcasted_iota(jnp.int32, (V7X_SUBLANES, RG_WIDTH), 0)

    def group(gi, hc):
        r0 = pl.multiple_of(gi * V7X_SUBLANES, V7X_SUBLANES)
        a8 = a_ref[pl.ds(r0, V7X_SUBLANES), :]
        b8 = b_ref[pl.ds(r0, V7X_SUBLANES), :]
        for d in (1, 2, 4):
            keep = row >= d
            b8 = jnp.where(keep, a8 * pltpu.roll(b8, d, axis=0) + b8, b8)
            a8 = jnp.where(keep, a8 * pltpu.roll(a8, d, axis=0), a8)
        h8 = a8 * hc + b8
        h_ref[pl.ds(r0, V7X_SUBLANES), :] = h8
        return h8[V7X_SUBLANES - 1:V7X_SUBLANES, :]

    hc = lax.fori_loop(0, tl // V7X_SUBLANES, group, hc_ref[...], unroll=4)
    hc_ref[...] = hc
    hlast_ref[...] = hc
    convn_ref[...] = xe_ref[tl + pad - 3:tl + pad, :]
    y = h_ref[...] * _gelu_tanh(zg_ref[...])
    y_ref[...] = _rms(y, gout_ref[...]).astype(y_ref.dtype)


def _rglru_seq(z, conv0, h0, cw, cb, wgate, ba, bx, lam, gout, *, batch, seq, tl, name,
               side_xattn=None):
    nt = seq // tl
    assert seq % tl == 0
    w = RG_WIDTH
    row = lambda v: v.reshape(1, w)
    const2 = lambda b, t: (0, 0)
    in_specs = [
        pl.BlockSpec((tl, w), lambda b, t: (b * nt + t, OFF_RGX // w)),
        pl.BlockSpec((tl, w), lambda b, t: (b * nt + t, OFF_RGG // w)),
        pl.BlockSpec((None, CONV_W - 1, w), lambda b, t: (b, 0, 0)),
        pl.BlockSpec((None, 1, w), lambda b, t: (b, 0, 0)),
        pl.BlockSpec((CONV_W, w), const2),
        pl.BlockSpec((1, w), const2),
        pl.BlockSpec((RG_BLOCKS, RG_BLOCK, 2 * RG_BLOCK), lambda b, t: (0, 0, 0)),
        pl.BlockSpec((1, w), const2),
        pl.BlockSpec((1, w), const2),
        pl.BlockSpec((1, w), const2),
        pl.BlockSpec((1, w), const2),
    ]
    out_specs = [
        pl.BlockSpec((tl, w), lambda b, t: (b * nt + t, 0)),
        pl.BlockSpec((None, 1, w), lambda b, t: (b, 0, 0)),
        pl.BlockSpec((None, CONV_W - 1, w), lambda b, t: (b, 0, 0)),
    ]
    out_shape = [
        jax.ShapeDtypeStruct((batch * seq, w), BF16),
        jax.ShapeDtypeStruct((batch, 1, w), F32),
        jax.ShapeDtypeStruct((batch, CONV_W - 1, w), F32),
    ]
    args = [z, z, conv0, h0.reshape(batch, 1, w), cw, row(cb), wgate, row(ba), row(bx), row(lam),
            row(gout)]
    side_rows = 0
    vmem_mib = 32
    if side_xattn is not None:
        xq, ck, cv = side_xattn
        nb = xq.shape[0]
        assert nb % (batch * nt) == 0 and ck.shape == (nb, N_MEM, XA_HEADS, XA_DH)
        side_rows = nb // (batch * nt)
        q_spec = pl.BlockSpec((side_rows, XA_HEADS, XA_DH), lambda b, t: (b * nt + t, 0, 0))
        cache_spec = pl.BlockSpec((side_rows, N_MEM, XA_HEADS, XA_DH),
                                  lambda b, t: (b * nt + t, 0, 0, 0))
        in_specs += [q_spec, cache_spec, cache_spec]
        out_specs.append(q_spec)
        out_shape.append(jax.ShapeDtypeStruct((nb, XA_HEADS, XA_DH), F32))
        args += [xq.reshape(nb, XA_HEADS, XA_DH), ck, cv]
        vmem_mib = 56
    return pl.pallas_call(
        functools.partial(_rglru_seq_kernel, tl=tl, side_rows=side_rows),
        grid=(batch, nt),
        in_specs=in_specs,
        out_specs=out_specs,
        out_shape=out_shape,
        scratch_shapes=[
            pltpu.VMEM((tl + V7X_SUBLANES, w), F32),
            pltpu.VMEM((tl, w), F32),
            pltpu.VMEM((tl, w), F32),
            pltpu.VMEM((tl, w), F32),
            pltpu.VMEM((1, w), F32),
        ],
        compiler_params=_params(2, vmem_mib),
        name=name,
    )(*args)


def _rglru_step_kernel(zx_ref, zg_ref, conv_ref, h0_ref, cw_ref, cb_ref, wgate_ref, ba_ref,
                       bx_ref, lam_ref, gout_ref, y_ref, hn_ref, convn_ref, a_ref, b_ref):
    w = RG_WIDTH
    x = zx_ref[...]
    xr = (conv_ref[:, 0:w] * cw_ref[0:1, :] + conv_ref[:, w:2 * w] * cw_ref[1:2, :]
          + conv_ref[:, 2 * w:3 * w] * cw_ref[2:3, :] + x * cw_ref[3:4, :]) + cb_ref[...]
    _rg_gates(xr, wgate_ref, ba_ref, bx_ref, lam_ref, a_ref, b_ref)
    h = a_ref[...] * h0_ref[...] + b_ref[...]
    hn_ref[...] = h
    convn_ref[:, 0:2 * w] = conv_ref[:, w:3 * w]
    convn_ref[:, 2 * w:3 * w] = x
    y_ref[...] = _rms(h * _gelu_tanh(zg_ref[...]), gout_ref[...]).astype(y_ref.dtype)


def _rglru_step(z, conv, h0, cw, cb, wgate, ba, bx, lam, gout, *, name):
    nb = z.shape[0]
    w = RG_WIDTH
    row = lambda v: v.reshape(1, w)
    c0 = lambda i: (0, 0)
    return pl.pallas_call(
        _rglru_step_kernel,
        grid=(1,),
        in_specs=[
            pl.BlockSpec((nb, w), lambda i: (0, OFF_RGX // w)),
            pl.BlockSpec((nb, w), lambda i: (0, OFF_RGG // w)),
            pl.BlockSpec((nb, (CONV_W - 1) * w), c0),
            pl.BlockSpec((nb, w), c0),
            pl.BlockSpec((CONV_W, w), c0),
            pl.BlockSpec((1, w), c0),
            pl.BlockSpec((RG_BLOCKS, RG_BLOCK, 2 * RG_BLOCK), lambda i: (0, 0, 0)),
            pl.BlockSpec((1, w), c0),
            pl.BlockSpec((1, w), c0),
            pl.BlockSpec((1, w), c0),
            pl.BlockSpec((1, w), c0),
        ],
        out_specs=[
            pl.BlockSpec((nb, w), c0),
            pl.BlockSpec((nb, w), c0),
            pl.BlockSpec((nb, (CONV_W - 1) * w), c0),
        ],
        out_shape=[
            jax.ShapeDtypeStruct((nb, w), BF16),
            jax.ShapeDtypeStruct((nb, w), F32),
            jax.ShapeDtypeStruct((nb, (CONV_W - 1) * w), F32),
        ],
        scratch_shapes=[pltpu.VMEM((nb, w), F32), pltpu.VMEM((nb, w), F32)],
        compiler_params=_params(1, 32),
        name=name,
    )(z, z, conv.reshape(nb, (CONV_W - 1) * w), h0, cw, row(cb), wgate, row(ba), row(bx),
      row(lam), row(gout))


def _mlstm_seq_kernel(bi_ref, bf_ref, q_ref, k_ref, v_ref, o_ref, zg_ref, g_ref,
                      y_ref, c_ref, n_ref, m_ref, zgt_ref, cs_ref, ns_ref, *, seq, cs):
    h = pl.program_id(1)
    nc = seq // cs
    bi = bi_ref[h]
    bf = bf_ref[h]
    for c in range(nc):
        zgt_ref[c] = zg_ref[c * cs:(c + 1) * cs, :].T
    cs_ref[...] = jnp.zeros_like(cs_ref)
    ns_ref[...] = jnp.zeros_like(ns_ref)

    t_idx = lax.broadcasted_iota(jnp.int32, (cs, cs), 0)
    s_idx = lax.broadcasted_iota(jnp.int32, (cs, cs), 1)
    causal = s_idx <= t_idx
    lane = lax.broadcasted_iota(jnp.int32, (cs, V7X_LANES), 1)

    def chunk(c, m):
        r0 = pl.multiple_of(c * cs, cs)
        q = q_ref[pl.ds(r0, cs), :]
        k = k_ref[pl.ds(r0, cs), :] * (ML_DK ** -0.5)
        v = v_ref[pl.ds(r0, cs), :]
        zg = zg_ref[pl.ds(r0, cs), :]
        li_col = jnp.sum(jnp.where(lane == h, zg, 0.0), axis=1, keepdims=True) + bi
        lf_col = _log_sigmoid(
            jnp.sum(jnp.where(lane == h + ML_HEADS, zg, 0.0), axis=1, keepdims=True) + bf)
        li_row = zgt_ref[c, pl.ds(h, 1), :] + bi
        lf_row = _log_sigmoid(zgt_ref[c, pl.ds(h + ML_HEADS, 1), :] + bf)
        bcum_col = jnp.sum(jnp.where(causal, lf_row, 0.0), axis=1, keepdims=True)
        bcum_row = jnp.sum(jnp.where(t_idx <= s_idx, lf_col, 0.0), axis=0, keepdims=True)
        log_d = jnp.where(causal, bcum_col - bcum_row + li_row, NEG)
        inter = bcum_col + m
        m_t = jnp.maximum(inter, jnp.max(log_d, axis=1, keepdims=True))
        dmat = jnp.exp(log_d - m_t)
        sc = jnp.exp(inter - m_t)
        qb = q.astype(BF16)
        kb = k.astype(BF16)
        vb = v.astype(BF16)
        qk = lax.dot_general(qb, kb, (((1,), (1,)), ((), ())), preferred_element_type=F32) * dmat
        c_old = cs_ref[...]
        n_old = ns_ref[...]
        num = sc * jnp.dot(qb, c_old.astype(BF16), preferred_element_type=F32) + _mm(qk, vb)
        den = sc * jnp.sum(q * n_old, axis=1, keepdims=True) + jnp.sum(qk, axis=1, keepdims=True)
        den = jnp.maximum(jnp.abs(den), jnp.exp(-m_t))
        hh = num / den
        m_new = m_t[cs - 1:cs, :]
        b_last = bcum_col[cs - 1:cs, :]
        w_end = jnp.exp(b_last - bcum_col + li_col - m_new)
        dec = jnp.exp(b_last + m - m_new)
        wk = w_end * k
        cs_ref[...] = dec * c_old + lax.dot_general(
            wk.astype(BF16), vb, (((0,), (0,)), ((), ())), preferred_element_type=F32)
        ns_ref[...] = dec * n_old + jnp.sum(wk, axis=0, keepdims=True)
        y = _rms(hh, g_ref[...]) * _sigmoid(o_ref[pl.ds(r0, cs), :])
        y_ref[pl.ds(r0, cs), :] = y.astype(y_ref.dtype)
        return m_new

    m_fin = lax.fori_loop(0, nc, chunk, jnp.zeros((1, 1), F32))
    c_ref[...] = cs_ref[...]
    n_ref[pl.ds(h, 1), :] = ns_ref[...]
    m_ref[pl.ds(h, 1), :] = jnp.broadcast_to(m_fin, (1, V7X_LANES))


def _mlstm_seq(z, zg, b_i, b_f, g_out, *, batch, seq, cs, name):
    assert seq % cs == 0 and zg.shape[1] == V7X_LANES
    dk, dv, nh = ML_DK, ML_DV, ML_HEADS
    smem = pl.BlockSpec(memory_space=pltpu.SMEM)
    return pl.pallas_call(
        functools.partial(_mlstm_seq_kernel, seq=seq, cs=cs),
        grid=(batch, nh),
        in_specs=[
            smem, smem,
            pl.BlockSpec((seq, dk), lambda b, h: (b, OFF_Q // dk + h)),
            pl.BlockSpec((seq, dk), lambda b, h: (b, OFF_K // dk + h)),
            pl.BlockSpec((seq, dv), lambda b, h: (b, OFF_V // dv + h)),
            pl.BlockSpec((seq, dv), lambda b, h: (b, OFF_O // dv + h)),
            pl.BlockSpec((seq, V7X_LANES), lambda b, h: (b, 0)),
            pl.BlockSpec((1, dv), lambda b, h: (0, h)),
        ],
        out_specs=[
            pl.BlockSpec((seq, dv), lambda b, h: (b, h)),
            pl.BlockSpec((None, None, dk, dv), lambda b, h: (b, h, 0, 0)),
            pl.BlockSpec((None, nh, dk), lambda b, h: (b, 0, 0)),
            pl.BlockSpec((None, nh, V7X_LANES), lambda b, h: (b, 0, 0)),
        ],
        out_shape=[
            jax.ShapeDtypeStruct((batch * seq, nh * dv), BF16),
            jax.ShapeDtypeStruct((batch, nh, dk, dv), F32),
            jax.ShapeDtypeStruct((batch, nh, dk), F32),
            jax.ShapeDtypeStruct((batch, nh, V7X_LANES), F32),
        ],
        scratch_shapes=[
            pltpu.VMEM((seq // cs, V7X_LANES, cs), F32),
            pltpu.VMEM((dk, dv), F32),
            pltpu.VMEM((1, dk), F32),
        ],
        compiler_params=_params(2, 40),
        name=name,
    )(b_i, b_f, z, z, z, z, zg, g_out.reshape(1, nh * dv))


def _mlstm_step_kernel(bi_ref, bf_ref, q_ref, k_ref, v_ref, o_ref, zg_ref, g_ref, c0_ref, n0_ref,
                       m0_ref, y_ref, c_ref, n_ref, m_ref, qc_ref, *, bs):
    dk, dv = ML_DK, ML_DV
    eye = (lax.broadcasted_iota(jnp.int32, (dk, dk), 0)
           == lax.broadcasted_iota(jnp.int32, (dk, dk), 1))
    zg = zg_ref[...]
    for h in range(ML_HEADS):
        li = zg[:, h:h + 1] + bi_ref[h]
        lf = _log_sigmoid(zg[:, ML_HEADS + h:ML_HEADS + h + 1] + bf_ref[h])
        m = m0_ref[:, h:h + 1]
        inter = lf + m
        m_t = jnp.maximum(inter, li)
        dgate = jnp.exp(li - m_t)
        sc = jnp.exp(inter - m_t)
        q = q_ref[:, h * dk:(h + 1) * dk]
        k = k_ref[:, h * dk:(h + 1) * dk] * (ML_DK ** -0.5)
        v = v_ref[:, h * dv:(h + 1) * dv]
        n_old = n0_ref[:, h, :]
        qk = jnp.sum(q * k, axis=1, keepdims=True) * dgate
        w_end = jnp.exp(li - m_t)
        dec = jnp.exp(inter - m_t)
        wk = w_end * k
        for j in range(bs):
            c_old = c0_ref[j, h]
            qc_ref[j:j + 1, :] = _mm(q[j:j + 1, :], c_old)
            wk_col = jnp.sum(
                jnp.where(eye, jnp.broadcast_to(wk[j:j + 1, :], (dk, dk)), 0.0),
                axis=1, keepdims=True)
            c_ref[j, h] = dec[j:j + 1, :] * c_old + wk_col * v[j:j + 1, :]
        num = sc * qc_ref[...] + qk * v
        den = sc * jnp.sum(q * n_old, axis=1, keepdims=True) + qk
        den = jnp.maximum(jnp.abs(den), jnp.exp(-m_t))
        hh = num / den
        n_ref[:, h, :] = dec * n_old + wk
        m_ref[:, h:h + 1] = m_t
        y = _rms(hh, g_ref[:, h * dv:(h + 1) * dv]) * _sigmoid(o_ref[:, h * dv:(h + 1) * dv])
        y_ref[:, h * dv:(h + 1) * dv] = y.astype(y_ref.dtype)


def _mlstm_step(z, zg, c0, n0, m0, b_i, b_f, g_out, *, bs, name):
    nb = z.shape[0]
    assert nb % bs == 0
    dk, dv, nh = ML_DK, ML_DV, ML_HEADS
    smem = pl.BlockSpec(memory_space=pltpu.SMEM)
    return pl.pallas_call(
        functools.partial(_mlstm_step_kernel, bs=bs),
        grid=(nb // bs,),
        in_specs=[
            smem, smem,
            pl.BlockSpec((bs, nh * dk), lambda i: (i, OFF_Q // (nh * dk))),
            pl.BlockSpec((bs, nh * dk), lambda i: (i, OFF_K // (nh * dk))),
            pl.BlockSpec((bs, nh * dv), lambda i: (i, OFF_V // (nh * dv))),
            pl.BlockSpec((bs, nh * dv), lambda i: (i, OFF_O // (nh * dv))),
            pl.BlockSpec((bs, V7X_LANES), lambda i: (i, 0)),
            pl.BlockSpec((1, nh * dv), lambda i: (0, 0)),
            pl.BlockSpec((bs, nh, dk, dv), lambda i: (i, 0, 0, 0)),
            pl.BlockSpec((bs, nh, dk), lambda i: (i, 0, 0)),
            pl.BlockSpec((bs, nh), lambda i: (i, 0)),
        ],
        out_specs=[
            pl.BlockSpec((bs, nh * dv), lambda i: (i, 0)),
            pl.BlockSpec((bs, nh, dk, dv), lambda i: (i, 0, 0, 0)),
            pl.BlockSpec((bs, nh, dk), lambda i: (i, 0, 0)),
            pl.BlockSpec((bs, nh), lambda i: (i, 0)),
        ],
        out_shape=[
            jax.ShapeDtypeStruct((nb, nh * dv), F32),
            jax.ShapeDtypeStruct((nb, nh, dk, dv), F32),
            jax.ShapeDtypeStruct((nb, nh, dk), F32),
            jax.ShapeDtypeStruct((nb, nh), F32),
        ],
        scratch_shapes=[pltpu.VMEM((bs, dv), F32)],
        compiler_params=_params(1, 32),
        name=name,
    )(b_i, b_f, z, z, z, z, zg, g_out.reshape(1, nh * dv), c0, n0, m0)


def _softmax_rows(s):
    e = jnp.exp(s - jnp.max(s, axis=-1, keepdims=True))
    return e / jnp.sum(e, axis=-1, keepdims=True)


def _xattn_block_kernel(x_ref, g_ref, wq_ref, k_ref, v_ref, wo_ref, o_ref):
    x = x_ref[...]
    xq = jnp.dot(_rms(x, g_ref[...]).astype(BF16), wq_ref[...],
                 preferred_element_type=F32).astype(BF16)
    heads = []
    for h in range(XA_HEADS):
        sl = slice(h * XA_DH, (h + 1) * XA_DH)
        s = lax.dot_general(xq[:, sl], k_ref[:, sl].astype(BF16),
                            (((1,), (1,)), ((), ())), preferred_element_type=F32)
        p = _softmax_rows(s * (XA_DH ** -0.5))
        heads.append(_mm(p, v_ref[:, sl]).astype(BF16))
    o_ref[...] = x + jnp.dot(jnp.concatenate(heads, axis=1), wo_ref[...],
                             preferred_element_type=F32)


def _xattn_block(x, g, wq16, mk, mv, wo16, *, batch, seq, tq, name):
    nt = seq // tq
    assert seq % tq == 0 and wq16.dtype == BF16 and wo16.dtype == BF16
    d = D_MODEL
    resident = dict(pipeline_mode=pl.Buffered(1))
    return pl.pallas_call(
        _xattn_block_kernel,
        grid=(batch, nt),
        in_specs=[
            pl.BlockSpec((tq, d), lambda b, t: (b * nt + t, 0)),
            pl.BlockSpec((1, d), lambda b, t: (0, 0)),
            pl.BlockSpec((d, d), lambda b, t: (0, 0), **resident),
            pl.BlockSpec((N_MEM, d), lambda b, t: (b, 0)),
            pl.BlockSpec((N_MEM, d), lambda b, t: (b, 0)),
            pl.BlockSpec((d, d), lambda b, t: (0, 0), **resident),
        ],
        out_specs=pl.BlockSpec((tq, d), lambda b, t: (b * nt + t, 0)),
        out_shape=jax.ShapeDtypeStruct((batch * seq, d), F32),
        compiler_params=_params(2, 56),
        name=name,
    )(x, g.reshape(1, d), wq16, mk, mv, wo16)


def _xattn_own_mask():
    nrow = N_MEM * XA_HEADS
    col_head = lax.broadcasted_iota(jnp.int32, (V7X_SUBLANES, nrow), 1) & (XA_HEADS - 1)
    row_head = lax.broadcasted_iota(jnp.int32, (V7X_SUBLANES, nrow), 0) & (XA_HEADS - 1)
    return col_head == row_head


def _xattn_probs(q_ref, k_ref, j, own):
    kf = k_ref[j].reshape(N_MEM * XA_HEADS, XA_DH)
    q8 = jnp.concatenate([q_ref[j]] * (V7X_SUBLANES // XA_HEADS), axis=0)
    s = lax.dot_general(q8.astype(BF16), kf.astype(BF16), (((1,), (1,)), ((), ())),
                        preferred_element_type=F32)
    s = jnp.where(own, s * (XA_DH ** -0.5), NEG)
    e = jnp.where(own, jnp.exp(s - jnp.max(s, axis=-1, keepdims=True)), 0.0)
    return e / jnp.sum(e, axis=-1, keepdims=True)


def _xattn_values(v_ref, o_ref, j, p):
    vf = v_ref[j].reshape(N_MEM * XA_HEADS, XA_DH)
    o_ref[j] = _mm(p, vf)[0:XA_HEADS, :]


def _xattn_step_kernel(q_ref, k_ref, v_ref, o_ref, *, sb):
    own = _xattn_own_mask()
    for j in range(sb):
        _xattn_values(v_ref, o_ref, j, _xattn_probs(q_ref, k_ref, j, own))


def _xattn_step(xq, ck, cv, *, sb, name):
    nb = xq.shape[0]
    assert XA_HEADS & (XA_HEADS - 1) == 0
    assert nb % sb == 0 and ck.shape == (nb, N_MEM, XA_HEADS, XA_DH)
    cache_spec = pl.BlockSpec((sb, N_MEM, XA_HEADS, XA_DH), lambda i: (i, 0, 0, 0))
    q_spec = pl.BlockSpec((sb, XA_HEADS, XA_DH), lambda i: (i, 0, 0))
    return pl.pallas_call(
        functools.partial(_xattn_step_kernel, sb=sb),
        grid=(nb // sb,),
        in_specs=[q_spec, cache_spec, cache_spec],
        out_specs=q_spec,
        out_shape=jax.ShapeDtypeStruct((nb, XA_HEADS, XA_DH), F32),
        compiler_params=_params(1, 40),
        name=name,
    )(xq.reshape(nb, XA_HEADS, XA_DH), ck, cv).reshape(nb, XA_HEADS * XA_DH)


def _tiles(rows):
    tm = min(rows, 1024)
    assert rows % tm == 0
    return tm


def kernel(x_prompt, x_sample, mem_prompt, state_rg_h, state_rg_conv, state_ml_C, state_ml_n, state_ml_m, cache_mem_k, cache_mem_v, g_mix, w_in, conv_w, conv_b, w_rg_a, b_rg_a, w_rg_x, b_rg_x, rg_lambda, b_ml_i, b_ml_f, g_rg_out, g_ml_out, w_out, g_xa, g_mem, w_xa_q, w_xa_k, w_xa_v, w_xa_o, g_ffn, w_ffn_gate, w_ffn_up, w_ffn_down, g_final):
    depth = g_mix.shape[0]
    assert depth == 1, "single trunk layer"
    bp, seq, d = x_prompt.shape
    bs_, dec_seq, _ = x_sample.shape
    assert d == D_MODEL and dec_seq == 1
    n_mem = mem_prompt.shape[1]
    assert n_mem == N_MEM
    dff = w_ffn_gate.shape[-1]
    in_w = w_in.shape[-1]
    assert in_w == IN_MAIN + N_GATE

    w_in_t = jnp.swapaxes(w_in, 1, 2).reshape(in_w, d)
    w_gate_pad = jnp.pad(w_in_t[IN_MAIN:], ((0, V7X_LANES - N_GATE), (0, 0)))
    cw = conv_w.reshape(CONV_W, RG_WIDTH)
    wgate = jnp.concatenate([w_rg_a.reshape(RG_BLOCKS, RG_BLOCK, RG_BLOCK),
                             w_rg_x.reshape(RG_BLOCKS, RG_BLOCK, RG_BLOCK)], axis=-1)
    rg_args = (cw, conv_b.reshape(-1), wgate, b_rg_a.reshape(-1), b_rg_x.reshape(-1),
               rg_lambda.reshape(-1), g_rg_out.reshape(-1))
    b_i = b_ml_i.reshape(ML_HEADS)
    b_f = b_ml_f.reshape(ML_HEADS)
    g_ml = g_ml_out.reshape(-1)
    w_out2 = w_out.reshape(d, d)
    w_q = w_xa_q.reshape(d, d)
    w_k = w_xa_k.reshape(d, d)
    w_v = w_xa_v.reshape(d, d)
    w_o = w_xa_o.reshape(d, d)
    w_fg = w_ffn_gate.reshape(d, dff)
    w_fu = w_ffn_up.reshape(d, dff)
    w_fd = w_ffn_down.reshape(dff, d)

    row_tile = _tiles(bp * seq)
    rg_zero_conv = jnp.zeros((bp, CONV_W - 1, RG_WIDTH), F32)
    rg_zero_h = jnp.zeros((bp, RG_WIDTH), F32)

    xs = x_sample.reshape(bs_, d)
    z_s, zg_s, w_in16 = _norm_linear(
        xs, g_mix.reshape(-1), w_in_t, n_out=IN_MAIN, tm=bs_, tn=1024, w_gate=w_gate_pad,
        w_is_nk=True, emit_w16=True, name="in_proj_s")
    y_rg_s, s_h, s_conv = _rglru_step(
        z_s, state_rg_conv.reshape(bs_, CONV_W - 1, RG_WIDTH), state_rg_h.reshape(bs_, RG_WIDTH),
        *rg_args, name="rglru_s")
    y_ml_s, s_c, s_n, s_m = _mlstm_step(
        z_s, zg_s, state_ml_C.reshape(bs_, ML_HEADS, ML_DK, ML_DV),
        state_ml_n.reshape(bs_, ML_HEADS, ML_DK), state_ml_m.reshape(bs_, ML_HEADS),
        b_i, b_f, g_ml, bs=V7X_SUBLANES, name="mlstm_s")
    x1_s, w_out16_rg, w_out16_ml = _linear_residual(
        [y_rg_s, y_ml_s], [(w_out2, 0), (w_out2, 1)], xs, tm=bs_, tn=1024, emit_w16=True,
        name="mix_out_s")
    xq_s, w_q16 = _norm_linear(x1_s, g_xa.reshape(-1), w_q, n_out=d, tm=bs_, tn=1024,
                               emit_w16=True, name="xa_q_s")
    ck = cache_mem_k.reshape(bs_, n_mem, XA_HEADS, XA_DH)
    cv = cache_mem_v.reshape(bs_, n_mem, XA_HEADS, XA_DH)

    tp = bp * seq
    xp = x_prompt.reshape(tp, d)
    z_p, zg_p = _norm_linear(xp, g_mix.reshape(-1), w_in16, n_out=IN_MAIN, tm=row_tile, tn=1024,
                             w_gate=w_gate_pad, w_is_nk=True, name="in_proj_p")
    rg_tl = min(seq, 256)
    rg_steps = bp * (seq // rg_tl)
    if bs_ % rg_steps == 0 and RG_BLOCKS % (bs_ // rg_steps) == 0:
        y_rg_p, p_h, p_conv, o_s = _rglru_seq(
            z_p, rg_zero_conv, rg_zero_h, *rg_args, batch=bp, seq=seq, tl=rg_tl,
            side_xattn=(xq_s, ck, cv), name="rglru_p")
        o_s = o_s.reshape(bs_, d)
    else:
        y_rg_p, p_h, p_conv = _rglru_seq(z_p, rg_zero_conv, rg_zero_h, *rg_args, batch=bp,
                                         seq=seq, tl=rg_tl, name="rglru_p")
        o_s = _xattn_step(xq_s, ck, cv, sb=2, name="xattn_s")

    x3_s, w_o16 = _linear_residual([o_s], [(w_o, 0)], x1_s, tm=bs_, tn=1024, emit_w16=True,
                                   name="xa_out_s")
    y_s, w_fg16, w_fu16, w_fd16 = _ffn(x3_s, g_ffn.reshape(-1), w_fg, w_fu, w_fd, g_final,
                                       tm=bs_, tf=512, emit_w16=True, vmem_mib=48, name="ffn_s")

    y_ml_p, p_c, p_n, p_m = _mlstm_seq(z_p, zg_p, b_i, b_f, g_ml, batch=bp, seq=seq,
                                       cs=min(seq, ML_CHUNK), name="mlstm_p")
    mem2 = mem_prompt.reshape(bp * n_mem, d)
    tmem = min(bp * n_mem, 256)
    mk, mk_heads = _norm_linear(mem2, g_mem.reshape(-1), w_k, n_out=d, tm=tmem, tn=d,
                                heads_out=(XA_HEADS, XA_DH), name="mem_k")
    mv, mv_heads = _norm_linear(mem2, g_mem.reshape(-1), w_v, n_out=d, tm=tmem, tn=d,
                                heads_out=(XA_HEADS, XA_DH), name="mem_v")
    proj_tile = min(row_tile, 512)
    x1_p = _linear_residual([y_rg_p, y_ml_p], [(w_out16_rg, 0), (w_out16_ml, 0)], xp,
                            tm=proj_tile, tn=d, name="mix_out_p")
    x3_p = _xattn_block(x1_p, g_xa.reshape(-1), w_q16, mk, mv, w_o16, batch=bp, seq=seq,
                        tq=min(seq, proj_tile), name="xattn_p")
    y_p = _ffn(x3_p, g_ffn.reshape(-1), w_fg16, w_fu16, w_fd16, g_final, tm=row_tile, tf=512,
               vmem_mib=60, name="ffn_p")

    return (
        y_p.reshape(bp, seq, d),
        y_s.reshape(bs_, 1, d),
        p_h.reshape(1, bp, RG_WIDTH),
        p_conv.reshape(1, bp, CONV_W - 1, RG_WIDTH),
        p_c.reshape(1, bp, ML_HEADS, ML_DK, ML_DV),
        p_n.reshape(1, bp, ML_HEADS, ML_DK),
        p_m[:, :, 0].reshape(1, bp, ML_HEADS),
        mk_heads.reshape(1, bp, n_mem, XA_HEADS, XA_DH),
        mv_heads.reshape(1, bp, n_mem, XA_HEADS, XA_DH),
        s_h.reshape(1, bs_, RG_WIDTH),
        s_conv.reshape(1, bs_, CONV_W - 1, RG_WIDTH),
        s_c.reshape(1, bs_, ML_HEADS, ML_DK, ML_DV),
        s_n.reshape(1, bs_, ML_HEADS, ML_DK),
        s_m.reshape(1, bs_, ML_HEADS),
    )
```

```python
import functools
from typing import Callable, NamedTuple

import jax
import jax.numpy as jnp
from jax import lax
from jax.experimental import pallas as pl
from jax.experimental.pallas import tpu as pltpu

F32 = jnp.float32
BF16 = jnp.bfloat16

D_MODEL = 2048
RG_WIDTH = D_MODEL // 2
RG_BLOCKS = 8
RG_BLOCK = RG_WIDTH // RG_BLOCKS
CONV_W = 4
RG_C = 8.0
ML_HEADS = 4
ML_WIDTH = D_MODEL - RG_WIDTH
ML_DV = ML_WIDTH // ML_HEADS
ML_DK = ML_DV // 2
N_MEM = 256
XA_HEADS = 4
XA_DH = D_MODEL // XA_HEADS
EPS = 1e-6
NEG = -1e30

OFF_RGX = 0
OFF_RGG = OFF_RGX + RG_WIDTH
OFF_Q = OFF_RGG + RG_WIDTH
OFF_K = OFF_Q + ML_HEADS * ML_DK
OFF_V = OFF_K + ML_HEADS * ML_DK
OFF_O = OFF_V + ML_WIDTH
OFF_I = OFF_O + ML_WIDTH
IN_MAIN = OFF_I
N_GATE = 2 * ML_HEADS

V7X_LANES = 128
V7X_SUBLANES = 8
V7X_VMEM_BYTES = 64 * 2**20

ML_CHUNK = 256


class SideJob(NamedTuple):
    n_blocks: int
    args: tuple
    in_specs: Callable
    out_shape: tuple
    out_specs: Callable
    scratch_shapes: tuple
    body: Callable


def _params(n_axes, vmem_mib):
    assert vmem_mib * 2**20 <= V7X_VMEM_BYTES
    return pltpu.CompilerParams(
        dimension_semantics=("arbitrary",) * n_axes,
        vmem_limit_bytes=vmem_mib * 2**20,
    )


def _rms(x, g):
    ms = jnp.mean(x * x, axis=-1, keepdims=True)
    return x * lax.rsqrt(ms + EPS) * g


def _softplus(u):
    return jnp.maximum(u, 0.0) + jnp.log1p(jnp.exp(-jnp.abs(u)))


def _log_sigmoid(u):
    return -_softplus(-u)


def _sigmoid(u):
    return 0.5 * jnp.tanh(0.5 * u) + 0.5


def _gelu_tanh(x):
    c = 0.7978845608028654
    half_x = 0.5 * x
    return half_x + half_x * jnp.tanh(x * (c + (c * 0.044715) * (x * x)))


def _sqrt_nonneg(v):
    return jnp.where(v > 0.0, v * lax.rsqrt(v), 0.0)


def _mm(a, b):
    return jnp.dot(a.astype(BF16), b.astype(BF16), preferred_element_type=F32)


def _dot_w(a, w_ref, w_is_nk):
    w = w_ref[...].astype(BF16)
    if w_is_nk:
        return lax.dot_general(a, w, (((1,), (1,)), ((), ())), preferred_element_type=F32)
    return jnp.dot(a, w, preferred_element_type=F32)


def _norm_linear_kernel(*refs, with_gate, w_is_nk, emit_w16, heads_out, side, side_cols):
    refs = list(refs)
    n_si, n_so, n_ss = ((len(side.args), len(side.out_shape), len(side.scratch_shapes))
                        if side else (0, 0, 0))
    n_in = 3 + with_gate + n_si
    n_out = 1 + bool(heads_out) + with_gate + emit_w16 + n_so
    x_ref, g_ref, w_ref = refs[:3]
    wg_ref = refs[3] if with_gate else None
    side_in = refs[3 + with_gate:n_in]
    outs = refs[n_in:n_in + n_out]
    xn_ref = refs[n_in + n_out]
    side_scratch = refs[n_in + n_out + 1:]
    o_ref = outs[0]
    oh_ref = outs[1] if heads_out else None
    og_ref = outs[1 + bool(heads_out)] if with_gate else None
    w16_ref = outs[n_out - n_so - 1] if emit_w16 else None
    side_out = outs[n_out - n_so:]
    col = pl.program_id(1)

    def column_tile(xn):
        acc = _dot_w(xn, w_ref, w_is_nk)
        o_ref[...] = acc.astype(o_ref.dtype)
        if heads_out:
            oh_ref[...] = acc.reshape(oh_ref.shape)
        if emit_w16:
            w16_ref[...] = w_ref[...].astype(BF16)

    @pl.when(col == 0)
    def _():
        xn = _rms(x_ref[...], g_ref[...]).astype(BF16)
        xn_ref[...] = xn
        if with_gate:
            og_ref[...] = _dot_w(xn, wg_ref, w_is_nk)
        column_tile(xn)

    if side:
        @pl.when((col > 0) & (col <= side_cols))
        def _():
            column_tile(xn_ref[...])
            side.body(side_in, side_out, side_scratch)

    @pl.when(col > (side_cols if side else 0))
    def _():
        column_tile(xn_ref[...])


def _norm_linear(x, g, w, *, n_out, tm, tn, out_dtype=F32, w_gate=None, w_is_nk=False,
                 emit_w16=False, heads_out=None, side=None, vmem_mib=48, name):
    m, k = x.shape
    k_ax, n_ax = (1, 0) if w_is_nk else (0, 1)
    assert m % tm == 0 and n_out % tn == 0 and w.shape[k_ax] == k and n_out <= w.shape[n_ax]
    assert not emit_w16 or m == tm
    assert heads_out is None or (tn == n_out and heads_out[0] * heads_out[1] == n_out)
    n_rows, n_cols = m // tm, n_out // tn
    with_gate = w_gate is not None
    w_mode = dict(pipeline_mode=pl.Buffered(1)) if tn == n_out else {}
    w_spec = (pl.BlockSpec((tn, k), lambda i, j: (j, 0), **w_mode) if w_is_nk
              else pl.BlockSpec((k, tn), lambda i, j: (0, j), **w_mode))
    in_specs = [
        pl.BlockSpec((tm, k), lambda i, j: (i, 0)),
        pl.BlockSpec((1, k), lambda i, j: (0, 0)),
        w_spec,
    ]
    out_shape = [jax.ShapeDtypeStruct((m, n_out), out_dtype)]
    out_specs = [pl.BlockSpec((tm, tn), lambda i, j: (i, j))]
    scratch_shapes = [pltpu.VMEM((tm, k), BF16)]
    args = [x, g.reshape(1, k), w]
    if heads_out:
        out_shape.append(jax.ShapeDtypeStruct((m, *heads_out), F32))
        out_specs.append(pl.BlockSpec((tm, *heads_out), lambda i, j: (i, 0, 0)))
    if with_gate:
        ng = w_gate.shape[n_ax]
        in_specs.append(pl.BlockSpec(w_gate.shape, lambda i, j: (0, 0)))
        out_shape.append(jax.ShapeDtypeStruct((m, ng), F32))
        out_specs.append(pl.BlockSpec((tm, ng), lambda i, j: (i, 0)))
        args.append(w_gate)
    if emit_w16:
        out_shape.append(jax.ShapeDtypeStruct((n_out, k) if w_is_nk else (k, n_out), BF16))
        out_specs.append(pl.BlockSpec((tn, k), lambda i, j: (j, 0)) if w_is_nk
                         else pl.BlockSpec((k, tn), lambda i, j: (0, j)))
    side_cols = 0
    if side:
        assert side.n_blocks % n_rows == 0 and 0 < side.n_blocks // n_rows < n_cols
        side_cols = side.n_blocks // n_rows
        block_of = lambda i, j: i * side_cols + jnp.clip(j - 1, 0, side_cols - 1)
        in_specs += side.in_specs(block_of)
        out_specs += side.out_specs(block_of)
        out_shape += list(side.out_shape)
        scratch_shapes += list(side.scratch_shapes)
        args += list(side.args)
    out = pl.pallas_call(
        functools.partial(_norm_linear_kernel, with_gate=with_gate, w_is_nk=w_is_nk,
                          emit_w16=emit_w16, heads_out=heads_out, side=side, side_cols=side_cols),
        grid=(n_rows, n_cols),
        in_specs=in_specs,
        out_specs=out_specs,
        out_shape=out_shape,
        scratch_shapes=scratch_shapes,
        compiler_params=_params(2, vmem_mib),
        name=name,
    )(*args)
    return out if len(out) > 1 else out[0]


def _linear_res_kernel(*refs, n_in, emit_w16):
    a_refs = refs[:n_in]
    w_refs = refs[n_in:2 * n_in]
    res_ref = refs[2 * n_in]
    o_ref = refs[2 * n_in + 1]
    acc = res_ref[...]
    for a_ref, w_ref in zip(a_refs, w_refs):
        acc = acc + _mm(a_ref[...], w_ref[...])
    o_ref[...] = acc
    if emit_w16:
        for w_ref, wc_ref in zip(w_refs, refs[2 * n_in + 2:]):
            wc_ref[...] = w_ref[...].astype(BF16)


def _linear_residual(parts, weights, res, *, tm, tn, emit_w16=False, vmem_mib=48, name):
    m, n = res.shape
    kp = parts[0].shape[1]
    assert all(p.shape == (m, kp) for p in parts) and len(weights) == len(parts)
    assert m % tm == 0 and n % tn == 0 and (not emit_w16 or m == tm)
    n_in = len(parts)
    in_specs = [pl.BlockSpec((tm, kp), lambda i, j: (i, 0)) for _ in parts]
    w_mode = dict(pipeline_mode=pl.Buffered(1)) if tn == n else {}
    in_specs += [pl.BlockSpec((kp, tn), lambda i, j, rb=rb: (rb, j), **w_mode)
                 for _, rb in weights]
    in_specs.append(pl.BlockSpec((tm, tn), lambda i, j: (i, j)))
    out_specs = [pl.BlockSpec((tm, tn), lambda i, j: (i, j))]
    out_shape = [jax.ShapeDtypeStruct((m, n), F32)]
    if emit_w16:
        out_specs += [pl.BlockSpec((kp, tn), lambda i, j: (0, j)) for _ in parts]
        out_shape += [jax.ShapeDtypeStruct((kp, n), BF16) for _ in parts]
    out = pl.pallas_call(
        functools.partial(_linear_res_kernel, n_in=n_in, emit_w16=emit_w16),
        grid=(m // tm, n // tn),
        in_specs=in_specs,
        out_specs=out_specs,
        out_shape=out_shape,
        compiler_params=_params(2, vmem_mib),
        name=name,
    )(*parts, *[w for w, _ in weights], res)
    return out if emit_w16 else out[0]


def _ffn_kernel(x_ref, g_ref, wg_ref, wu_ref, wd_ref, gf_ref, o_ref, *rest, emit_w16):
    xf_ref = rest[-1]
    f = pl.program_id(1)
    last = pl.num_programs(1) - 1

    def hidden_tile(xf):
        wg = wg_ref[...].astype(BF16)
        wu = wu_ref[...].astype(BF16)
        wd = wd_ref[...].astype(BF16)
        if emit_w16:
            for dst, val in zip(rest[:3], (wg, wu, wd)):
                dst[...] = val
        gate = jnp.dot(xf, wg, preferred_element_type=F32)
        up = jnp.dot(xf, wu, preferred_element_type=F32)
        hidden = (gate * _sigmoid(gate)) * up
        return jnp.dot(hidden.astype(BF16), wd, preferred_element_type=F32)

    @pl.when(f == 0)
    def _():
        x = x_ref[...]
        xf = _rms(x, g_ref[...]).astype(BF16)
        xf_ref[...] = xf
        o_ref[...] = x + hidden_tile(xf)

    @pl.when((f > 0) & (f < last))
    def _():
        o_ref[...] += hidden_tile(xf_ref[...])

    @pl.when(f == last)
    def _():
        o_ref[...] = _rms(o_ref[...] + hidden_tile(xf_ref[...]), gf_ref[...])


def _ffn(x, g, w_gate, w_up, w_down, g_final, *, tm, tf, emit_w16=False, vmem_mib, name):
    m, d = x.shape
    dff = w_gate.shape[1]
    assert m % tm == 0 and dff % tf == 0 and dff // tf >= 2 and (not emit_w16 or m == tm)
    up_spec = pl.BlockSpec((d, tf), lambda i, f: (0, f))
    down_spec = pl.BlockSpec((tf, d), lambda i, f: (f, 0))
    out_specs = [pl.BlockSpec((tm, d), lambda i, f: (i, 0))]
    out_shape = [jax.ShapeDtypeStruct((m, d), F32)]
    if emit_w16:
        out_specs += [up_spec, up_spec, down_spec]
        out_shape += [jax.ShapeDtypeStruct(w.shape, BF16) for w in (w_gate, w_up, w_down)]
    out = pl.pallas_call(
        functools.partial(_ffn_kernel, emit_w16=emit_w16),
        grid=(m // tm, dff // tf),
        in_specs=[
            pl.BlockSpec((tm, d), lambda i, f: (i, 0)),
            pl.BlockSpec((1, d), lambda i, f: (0, 0)),
            up_spec,
            up_spec,
            down_spec,
            pl.BlockSpec((1, d), lambda i, f: (0, 0)),
        ],
        out_specs=out_specs,
        out_shape=out_shape,
        scratch_shapes=[pltpu.VMEM((tm, d), BF16)],
        compiler_params=_params(2, vmem_mib),
        name=name,
    )(x, g.reshape(1, d), w_gate, w_up, w_down, g_final.reshape(1, d))
    return out if emit_w16 else out[0]


RG_GATE_LOOKAHEAD = 2


def _rg_gates(xr, wgate_ref, ba_ref, bx_ref, lam_ref, a_ref, b_ref, side=None):
    blocks = [slice(n * RG_BLOCK, (n + 1) * RG_BLOCK) for n in range(RG_BLOCKS)]
    gates = {}

    def issue_gate(n):
        if n < RG_BLOCKS:
            gates[n] = _mm(xr[:, blocks[n]], wgate_ref[n])

    rows, score_phase, value_phase = side if side else (0, None, None)
    per = RG_BLOCKS // rows if rows else RG_BLOCKS
    scores = {}
    for n in range(RG_GATE_LOOKAHEAD):
        issue_gate(n)
    for n, sl in enumerate(blocks):
        if rows and n % per == 0:
            j = n // per
            scores[j] = score_phase(j)
            if j > 0:
                value_phase(j - 1, scores.pop(j - 1))
        xn = xr[:, sl]
        g = gates.pop(n)
        issue_gate(n + RG_GATE_LOOKAHEAD)
        r = _sigmoid(g[:, :RG_BLOCK] + ba_ref[:, sl])
        ig = _sigmoid(g[:, RG_BLOCK:] + bx_ref[:, sl])
        a = jnp.exp(r * (-RG_C * _softplus(-lam_ref[:, sl])))
        a_ref[:, sl] = a
        mult = _sqrt_nonneg(jnp.maximum(1.0 - a * a, 0.0))
        b_ref[:, sl] = mult * (ig * xn)
    if rows:
        value_phase(rows - 1, scores.pop(rows - 1))


def _rglru_seq_kernel(*refs, tl, side_rows):
    (zx_ref, zg_ref, conv0_ref, h0_ref, cw_ref, cb_ref, wgate_ref, ba_ref, bx_ref, lam_ref,
     gout_ref) = refs[:11]
    xe_ref, a_ref, b_ref, h_ref, hc_ref = refs[-5:]
    if side_rows:
        sq_ref, sk_ref, sv_ref, y_ref, hlast_ref, convn_ref, so_ref = refs[11:-5]
    else:
        y_ref, hlast_ref, convn_ref = refs[11:-5]
    t = pl.program_id(1)
    pad = V7X_SUBLANES

    @pl.when(t == 0)
    def _():
        xe_ref[pad - 3:pad, :] = conv0_ref[...]
        hc_ref[...] = h0_ref[...]

    @pl.when(t > 0)
    def _():
        xe_ref[pad - 3:pad, :] = xe_ref[tl + pad - 3:tl + pad, :]

    x = zx_ref[...]
    xe_ref[pad:tl + pad, :] = x
    xr = (xe_ref[pad - 3:tl + pad - 3, :] * cw_ref[0:1, :]
          + xe_ref[pad - 2:tl + pad - 2, :] * cw_ref[1:2, :]
          + xe_ref[pad - 1:tl + pad - 1, :] * cw_ref[2:3, :]
          + x * cw_ref[3:4, :]) + cb_ref[...]
    side = None
    if side_rows:
        own = _xattn_own_mask()
        side = (side_rows,
                lambda j: _xattn_probs(sq_ref, sk_ref, j, own),
                lambda j, p: _xattn_values(sv_ref, so_ref, j, p))

    _rg_gates(xr, wgate_ref, ba_ref, bx_ref, lam_ref, a_ref, b_ref, side=side)

    row = lax.broadcasted_iota(jnp.int32, (V7X_SUBLANES, RG_WIDTH), 0)

    def group(gi, hc):
        r0 = pl.multiple_of(gi * V7X_SUBLANES, V7X_SUBLANES)
        a8 = a_ref[pl.ds(r0, V7X_SUBLANES), :]
        b8 = b_ref[pl.ds(r0, V7X_SUBLANES), :]
        for d in (1, 2, 4):
            keep = row >= d
            b8 = jnp.where(keep, a8 * pltpu.roll(b8, d, axis=0) + b8, b8)
            a8 = jnp.where(keep, a8 * pltpu.roll(a8, d, axis=0), a8)
        h8 = a8 * hc + b8
        h_ref[pl.ds(r0, V7X_SUBLANES), :] = h8
        return h8[V7X_SUBLANES - 1:V7X_SUBLANES, :]

    hc = lax.fori_loop(0, tl // V7X_SUBLANES, group, hc_ref[...], unroll=4)
    hc_ref[...] = hc
    hlast_ref[...] = hc
    convn_ref[...] = xe_ref[tl + pad - 3:tl + pad, :]
    y = h_ref[...] * _gelu_tanh(zg_ref[...])
    y_ref[...] = _rms(y, gout_ref[...]).astype(y_ref.dtype)


def _rglru_seq(z, conv0, h0, cw, cb, wgate, ba, bx, lam, gout, *, batch, seq, tl, name,
               side_xattn=None):
    nt = seq // tl
    assert seq % tl == 0
    w = RG_WIDTH
    row = lambda v: v.reshape(1, w)
    const2 = lambda b, t: (0, 0)
    in_specs = [
        pl.BlockSpec((tl, w), lambda b, t: (b * nt + t, OFF_RGX // w)),
        pl.BlockSpec((tl, w), lambda b, t: (b * nt + t, OFF_RGG // w)),
        pl.BlockSpec((None, CONV_W - 1, w), lambda b, t: (b, 0, 0)),
        pl.BlockSpec((None, 1, w), lambda b, t: (b, 0, 0)),
        pl.BlockSpec((CONV_W, w), const2),
        pl.BlockSpec((1, w), const2),
        pl.BlockSpec((RG_BLOCKS, RG_BLOCK, 2 * RG_BLOCK), lambda b, t: (0, 0, 0)),
        pl.BlockSpec((1, w), const2),
        pl.BlockSpec((1, w), const2),
        pl.BlockSpec((1, w), const2),
        pl.BlockSpec((1, w), const2),
    ]
    out_specs = [
        pl.BlockSpec((tl, w), lambda b, t: (b * nt + t, 0)),
        pl.BlockSpec((None, 1, w), lambda b, t: (b, 0, 0)),
        pl.BlockSpec((None, CONV_W - 1, w), lambda b, t: (b, 0, 0)),
    ]
    out_shape = [
        jax.ShapeDtypeStruct((batch * seq, w), BF16),
        jax.ShapeDtypeStruct((batch, 1, w), F32),
        jax.ShapeDtypeStruct((batch, CONV_W - 1, w), F32),
    ]
    args = [z, z, conv0, h0.reshape(batch, 1, w), cw, row(cb), wgate, row(ba), row(bx), row(lam),
            row(gout)]
    side_rows = 0
    vmem_mib = 32
    if side_xattn is not None:
        xq, ck, cv = side_xattn
        nb = xq.shape[0]
        assert nb % (batch * nt) == 0 and ck.shape == (nb, N_MEM, XA_HEADS, XA_DH)
        side_rows = nb // (batch * nt)
        q_spec = pl.BlockSpec((side_rows, XA_HEADS, XA_DH), lambda b, t: (b * nt + t, 0, 0))
        cache_spec = pl.BlockSpec((side_rows, N_MEM, XA_HEADS, XA_DH),
                                  lambda b, t: (b * nt + t, 0, 0, 0))
        in_specs += [q_spec, cache_spec, cache_spec]
        out_specs.append(q_spec)
        out_shape.append(jax.ShapeDtypeStruct((nb, XA_HEADS, XA_DH), F32))
        args += [xq.reshape(nb, XA_HEADS, XA_DH), ck, cv]
        vmem_mib = 56
    return pl.pallas_call(
        functools.partial(_rglru_seq_kernel, tl=tl, side_rows=side_rows),
        grid=(batch, nt),
        in_specs=in_specs,
        out_specs=out_specs,
        out_shape=out_shape,
        scratch_shapes=[
            pltpu.VMEM((tl + V7X_SUBLANES, w), F32),
            pltpu.VMEM((tl, w), F32),
            pltpu.VMEM((tl, w), F32),
            pltpu.VMEM((tl, w), F32),
            pltpu.VMEM((1, w), F32),
        ],
        compiler_params=_params(2, vmem_mib),
        name=name,
    )(*args)


def _rglru_step_kernel(zx_ref, zg_ref, conv_ref, h0_ref, cw_ref, cb_ref, wgate_ref, ba_ref,
                       bx_ref, lam_ref, gout_ref, y_ref, hn_ref, convn_ref, a_ref, b_ref):
    w = RG_WIDTH
    x = zx_ref[...]
    xr = (conv_ref[:, 0:w] * cw_ref[0:1, :] + conv_ref[:, w:2 * w] * cw_ref[1:2, :]
          + conv_ref[:, 2 * w:3 * w] * cw_ref[2:3, :] + x * cw_ref[3:4, :]) + cb_ref[...]
    _rg_gates(xr, wgate_ref, ba_ref, bx_ref, lam_ref, a_ref, b_ref)
    h = a_ref[...] * h0_ref[...] + b_ref[...]
    hn_ref[...] = h
    convn_ref[:, 0:2 * w] = conv_ref[:, w:3 * w]
    convn_ref[:, 2 * w:3 * w] = x
    y_ref[...] = _rms(h * _gelu_tanh(zg_ref[...]), gout_ref[...]).astype(y_ref.dtype)


def _rglru_step(z, conv, h0, cw, cb, wgate, ba, bx, lam, gout, *, name):
    nb = z.shape[0]
    w = RG_WIDTH
    row = lambda v: v.reshape(1, w)
    c0 = lambda i: (0, 0)
    return pl.pallas_call(
        _rglru_step_kernel,
        grid=(1,),
        in_specs=[
            pl.BlockSpec((nb, w), lambda i: (0, OFF_RGX // w)),
            pl.BlockSpec((nb, w), lambda i: (0, OFF_RGG // w)),
            pl.BlockSpec((nb, (CONV_W - 1) * w), c0),
            pl.BlockSpec((nb, w), c0),
            pl.BlockSpec((CONV_W, w), c0),
            pl.BlockSpec((1, w), c0),
            pl.BlockSpec((RG_BLOCKS, RG_BLOCK, 2 * RG_BLOCK), lambda i: (0, 0, 0)),
            pl.BlockSpec((1, w), c0),
            pl.BlockSpec((1, w), c0),
            pl.BlockSpec((1, w), c0),
            pl.BlockSpec((1, w), c0),
        ],
        out_specs=[
            pl.BlockSpec((nb, w), c0),
            pl.BlockSpec((nb, w), c0),
            pl.BlockSpec((nb, (CONV_W - 1) * w), c0),
        ],
        out_shape=[
            jax.ShapeDtypeStruct((nb, w), BF16),
            jax.ShapeDtypeStruct((nb, w), F32),
            jax.ShapeDtypeStruct((nb, (CONV_W - 1) * w), F32),
        ],
        scratch_shapes=[pltpu.VMEM((nb, w), F32), pltpu.VMEM((nb, w), F32)],
        compiler_params=_params(1, 32),
        name=name,
    )(z, z, conv.reshape(nb, (CONV_W - 1) * w), h0, cw, row(cb), wgate, row(ba), row(bx),
      row(lam), row(gout))


def _mlstm_seq_kernel(bi_ref, bf_ref, q_ref, k_ref, v_ref, o_ref, zg_ref, g_ref,
                      y_ref, c_ref, n_ref, m_ref, zgt_ref, cs_ref, ns_ref, *, seq, cs):
    h = pl.program_id(1)
    nc = seq // cs
    bi = bi_ref[h]
    bf = bf_ref[h]
    for c in range(nc):
        zgt_ref[c] = zg_ref[c * cs:(c + 1) * cs, :].T
    cs_ref[...] = jnp.zeros_like(cs_ref)
    ns_ref[...] = jnp.zeros_like(ns_ref)

    t_idx = lax.broadcasted_iota(jnp.int32, (cs, cs), 0)
    s_idx = lax.broadcasted_iota(jnp.int32, (cs, cs), 1)
    causal = s_idx <= t_idx
    lane = lax.broadcasted_iota(jnp.int32, (cs, V7X_LANES), 1)

    def chunk(c, m):
        r0 = pl.multiple_of(c * cs, cs)
        q = q_ref[pl.ds(r0, cs), :]
        k = k_ref[pl.ds(r0, cs), :] * (ML_DK ** -0.5)
        v = v_ref[pl.ds(r0, cs), :]
        zg = zg_ref[pl.ds(r0, cs), :]
        li_col = jnp.sum(jnp.where(lane == h, zg, 0.0), axis=1, keepdims=True) + bi
        lf_col = _log_sigmoid(
            jnp.sum(jnp.where(lane == h + ML_HEADS, zg, 0.0), axis=1, keepdims=True) + bf)
        li_row = zgt_ref[c, pl.ds(h, 1), :] + bi
        lf_row = _log_sigmoid(zgt_ref[c, pl.ds(h + ML_HEADS, 1), :] + bf)
        bcum_col = jnp.sum(jnp.where(causal, lf_row, 0.0), axis=1, keepdims=True)
        bcum_row = jnp.sum(jnp.where(t_idx <= s_idx, lf_col, 0.0), axis=0, keepdims=True)
        log_d = jnp.where(causal, bcum_col - bcum_row + li_row, NEG)
        inter = bcum_col + m
        m_t = jnp.maximum(inter, jnp.max(log_d, axis=1, keepdims=True))
        dmat = jnp.exp(log_d - m_t)
        sc = jnp.exp(inter - m_t)
        qb = q.astype(BF16)
        kb = k.astype(BF16)
        vb = v.astype(BF16)
        qk = lax.dot_general(qb, kb, (((1,), (1,)), ((), ())), preferred_element_type=F32) * dmat
        c_old = cs_ref[...]
        n_old = ns_ref[...]
        num = sc * jnp.dot(qb, c_old.astype(BF16), preferred_element_type=F32) + _mm(qk, vb)
        den = sc * jnp.sum(q * n_old, axis=1, keepdims=True) + jnp.sum(qk, axis=1, keepdims=True)
        den = jnp.maximum(jnp.abs(den), jnp.exp(-m_t))
        hh = num / den
        m_new = m_t[cs - 1:cs, :]
        b_last = bcum_col[cs - 1:cs, :]
        w_end = jnp.exp(b_last - bcum_col + li_col - m_new)
        dec = jnp.exp(b_last + m - m_new)
        wk = w_end * k
        cs_ref[...] = dec * c_old + lax.dot_general(
            wk.astype(BF16), vb, (((0,), (0,)), ((), ())), preferred_element_type=F32)
        ns_ref[...] = dec * n_old + jnp.sum(wk, axis=0, keepdims=True)
        y = _rms(hh, g_ref[...]) * _sigmoid(o_ref[pl.ds(r0, cs), :])
        y_ref[pl.ds(r0, cs), :] = y.astype(y_ref.dtype)
        return m_new

    m_fin = lax.fori_loop(0, nc, chunk, jnp.zeros((1, 1), F32))
    c_ref[...] = cs_ref[...]
    n_ref[pl.ds(h, 1), :] = ns_ref[...]
    m_ref[pl.ds(h, 1), :] = jnp.broadcast_to(m_fin, (1, V7X_LANES))


def _mlstm_seq(z, zg, b_i, b_f, g_out, *, batch, seq, cs, name):
    assert seq % cs == 0 and zg.shape[1] == V7X_LANES
    dk, dv, nh = ML_DK, ML_DV, ML_HEADS
    smem = pl.BlockSpec(memory_space=pltpu.SMEM)
    return pl.pallas_call(
        functools.partial(_mlstm_seq_kernel, seq=seq, cs=cs),
        grid=(batch, nh),
        in_specs=[
            smem, smem,
            pl.BlockSpec((seq, dk), lambda b, h: (b, OFF_Q // dk + h)),
            pl.BlockSpec((seq, dk), lambda b, h: (b, OFF_K // dk + h)),
            pl.BlockSpec((seq, dv), lambda b, h: (b, OFF_V // dv + h)),
            pl.BlockSpec((seq, dv), lambda b, h: (b, OFF_O // dv + h)),
            pl.BlockSpec((seq, V7X_LANES), lambda b, h: (b, 0)),
            pl.BlockSpec((1, dv), lambda b, h: (0, h)),
        ],
        out_specs=[
            pl.BlockSpec((seq, dv), lambda b, h: (b, h)),
            pl.BlockSpec((None, None, dk, dv), lambda b, h: (b, h, 0, 0)),
            pl.BlockSpec((None, nh, dk), lambda b, h: (b, 0, 0)),
            pl.BlockSpec((None, nh, V7X_LANES), lambda b, h: (b, 0, 0)),
        ],
        out_shape=[
            jax.ShapeDtypeStruct((batch * seq, nh * dv), BF16),
            jax.ShapeDtypeStruct((batch, nh, dk, dv), F32),
            jax.ShapeDtypeStruct((batch, nh, dk), F32),
            jax.ShapeDtypeStruct((batch, nh, V7X_LANES), F32),
        ],
        scratch_shapes=[
            pltpu.VMEM((seq // cs, V7X_LANES, cs), F32),
            pltpu.VMEM((dk, dv), F32),
            pltpu.VMEM((1, dk), F32),
        ],
        compiler_params=_params(2, 40),
        name=name,
    )(b_i, b_f, z, z, z, z, zg, g_out.reshape(1, nh * dv))


def _mlstm_step_body(in_refs, out_refs, scratch_refs, *, bs):
    bi_ref, bf_ref, q_ref, k_ref, v_ref, o_ref, zg_ref, g_ref, c0_ref, n0_ref, m0_ref = in_refs
    y_ref, c_ref, n_ref, m_ref = out_refs
    qc_ref, = scratch_refs
    dk, dv = ML_DK, ML_DV
    eye = (lax.broadcasted_iota(jnp.int32, (dk, dk), 0)
           == lax.broadcasted_iota(jnp.int32, (dk, dk), 1))

    def as_column(row):
        return jnp.sum(jnp.where(eye, jnp.broadcast_to(row, (dk, dk)), 0.0), axis=1, keepdims=True)

    zg = zg_ref[...]
    for h in range(ML_HEADS):
        li = zg[:, h:h + 1] + bi_ref[h]
        lf = _log_sigmoid(zg[:, ML_HEADS + h:ML_HEADS + h + 1] + bf_ref[h])
        m = m0_ref[:, h:h + 1]
        inter = lf + m
        m_t = jnp.maximum(inter, li)
        dgate = jnp.exp(li - m_t)
        sc = jnp.exp(inter - m_t)
        q = q_ref[:, h * dk:(h + 1) * dk]
        k = k_ref[:, h * dk:(h + 1) * dk] * (ML_DK ** -0.5)
        v = v_ref[:, h * dv:(h + 1) * dv]
        n_old = n0_ref[:, h, :]
        qk = jnp.sum(q * k, axis=1, keepdims=True) * dgate
        w_end = jnp.exp(li - m_t)
        dec = jnp.exp(inter - m_t)
        wk = w_end * k
        for j in range(bs):
            c_old = c0_ref[j, h]
            qc_ref[j:j + 1, :] = jnp.sum(as_column(q[j:j + 1, :]) * c_old, axis=0, keepdims=True)
            c_ref[j, h] = dec[j:j + 1, :] * c_old + as_column(wk[j:j + 1, :]) * v[j:j + 1, :]
        num = sc * qc_ref[...] + qk * v
        den = sc * jnp.sum(q * n_old, axis=1, keepdims=True) + qk
        den = jnp.maximum(jnp.abs(den), jnp.exp(-m_t))
        hh = num / den
        n_ref[:, h, :] = dec * n_old + wk
        m_ref[:, h:h + 1] = m_t
        y = _rms(hh, g_ref[:, h * dv:(h + 1) * dv]) * _sigmoid(o_ref[:, h * dv:(h + 1) * dv])
        y_ref[:, h * dv:(h + 1) * dv] = y.astype(y_ref.dtype)


def _mlstm_step_job(z, zg, c0, n0, m0, b_i, b_f, g_out, *, bs):
    nb = z.shape[0]
    assert nb % bs == 0
    dk, dv, nh = ML_DK, ML_DV, ML_HEADS
    smem = pl.BlockSpec(memory_space=pltpu.SMEM)

    def in_specs(blk):
        return [
            smem, smem,
            pl.BlockSpec((bs, nh * dk), lambda *g: (blk(*g), OFF_Q // (nh * dk))),
            pl.BlockSpec((bs, nh * dk), lambda *g: (blk(*g), OFF_K // (nh * dk))),
            pl.BlockSpec((bs, nh * dv), lambda *g: (blk(*g), OFF_V // (nh * dv))),
            pl.BlockSpec((bs, nh * dv), lambda *g: (blk(*g), OFF_O // (nh * dv))),
            pl.BlockSpec((bs, V7X_LANES), lambda *g: (blk(*g), 0)),
            pl.BlockSpec((1, nh * dv), lambda *g: (0, 0)),
            pl.BlockSpec((bs, nh, dk, dv), lambda *g: (blk(*g), 0, 0, 0)),
            pl.BlockSpec((bs, nh, dk), lambda *g: (blk(*g), 0, 0)),
            pl.BlockSpec((bs, nh), lambda *g: (blk(*g), 0)),
        ]

    def out_specs(blk):
        return [
            pl.BlockSpec((bs, nh * dv), lambda *g: (blk(*g), 0)),
            pl.BlockSpec((bs, nh, dk, dv), lambda *g: (blk(*g), 0, 0, 0)),
            pl.BlockSpec((bs, nh, dk), lambda *g: (blk(*g), 0, 0)),
            pl.BlockSpec((bs, nh), lambda *g: (blk(*g), 0)),
        ]

    return SideJob(
        n_blocks=nb // bs,
        args=(b_i, b_f, z, z, z, z, zg, g_out.reshape(1, nh * dv), c0, n0, m0),
        in_specs=in_specs,
        out_shape=(
            jax.ShapeDtypeStruct((nb, nh * dv), F32),
            jax.ShapeDtypeStruct((nb, nh, dk, dv), F32),
            jax.ShapeDtypeStruct((nb, nh, dk), F32),
            jax.ShapeDtypeStruct((nb, nh), F32),
        ),
        out_specs=out_specs,
        scratch_shapes=(pltpu.VMEM((bs, dv), F32),),
        body=functools.partial(_mlstm_step_body, bs=bs),
    )


def _run_side_job(job, *, vmem_mib, name):
    n_in, n_out = len(job.args), len(job.out_shape)

    def body(*refs):
        job.body(refs[:n_in], refs[n_in:n_in + n_out], refs[n_in + n_out:])

    block_of = lambda i: i
    return pl.pallas_call(
        body,
        grid=(job.n_blocks,),
        in_specs=job.in_specs(block_of),
        out_specs=job.out_specs(block_of),
        out_shape=list(job.out_shape),
        scratch_shapes=list(job.scratch_shapes),
        compiler_params=_params(1, vmem_mib),
        name=name,
    )(*job.args)


def _softmax_rows(s):
    e = jnp.exp(s - jnp.max(s, axis=-1, keepdims=True))
    return e / jnp.sum(e, axis=-1, keepdims=True)


def _xattn_block_kernel(x_ref, g_ref, wq_ref, k_ref, v_ref, wo_ref, o_ref):
    x = x_ref[...]
    xq = jnp.dot(_rms(x, g_ref[...]).astype(BF16), wq_ref[...],
                 preferred_element_type=F32).astype(BF16)
    heads = []
    for h in range(XA_HEADS):
        sl = slice(h * XA_DH, (h + 1) * XA_DH)
        s = lax.dot_general(xq[:, sl], k_ref[:, sl].astype(BF16),
                            (((1,), (1,)), ((), ())), preferred_element_type=F32)
        p = _softmax_rows(s * (XA_DH ** -0.5))
        heads.append(_mm(p, v_ref[:, sl]).astype(BF16))
    o_ref[...] = x + jnp.dot(jnp.concatenate(heads, axis=1), wo_ref[...],
                             preferred_element_type=F32)


def _xattn_block(x, g, wq16, mk, mv, wo16, *, batch, seq, tq, name):
    nt = seq // tq
    assert seq % tq == 0 and wq16.dtype == BF16 and wo16.dtype == BF16
    d = D_MODEL
    resident = dict(pipeline_mode=pl.Buffered(1))
    return pl.pallas_call(
        _xattn_block_kernel,
        grid=(batch, nt),
        in_specs=[
            pl.BlockSpec((tq, d), lambda b, t: (b * nt + t, 0)),
            pl.BlockSpec((1, d), lambda b, t: (0, 0)),
            pl.BlockSpec((d, d), lambda b, t: (0, 0), **resident),
            pl.BlockSpec((N_MEM, d), lambda b, t: (b, 0)),
            pl.BlockSpec((N_MEM, d), lambda b, t: (b, 0)),
            pl.BlockSpec((d, d), lambda b, t: (0, 0), **resident),
        ],
        out_specs=pl.BlockSpec((tq, d), lambda b, t: (b * nt + t, 0)),
        out_shape=jax.ShapeDtypeStruct((batch * seq, d), F32),
        compiler_params=_params(2, 56),
        name=name,
    )(x, g.reshape(1, d), wq16, mk, mv, wo16)


def _xattn_own_mask():
    nrow = N_MEM * XA_HEADS
    col_head = lax.broadcasted_iota(jnp.int32, (V7X_SUBLANES, nrow), 1) & (XA_HEADS - 1)
    row_head = lax.broadcasted_iota(jnp.int32, (V7X_SUBLANES, nrow), 0) & (XA_HEADS - 1)
    return col_head == row_head


def _xattn_probs(q_ref, k_ref, j, own):
    kf = k_ref[j].reshape(N_MEM * XA_HEADS, XA_DH)
    q8 = jnp.concatenate([q_ref[j]] * (V7X_SUBLANES // XA_HEADS), axis=0)
    s = lax.dot_general(q8.astype(BF16), kf.astype(BF16), (((1,), (1,)), ((), ())),
                        preferred_element_type=F32)
    s = jnp.where(own, s * (XA_DH ** -0.5), NEG)
    e = jnp.where(own, jnp.exp(s - jnp.max(s, axis=-1, keepdims=True)), 0.0)
    return e / jnp.sum(e, axis=-1, keepdims=True)


def _xattn_values(v_ref, o_ref, j, p):
    vf = v_ref[j].reshape(N_MEM * XA_HEADS, XA_DH)
    o_ref[j] = _mm(p, vf)[0:XA_HEADS, :]


def _xattn_step_kernel(q_ref, k_ref, v_ref, o_ref, *, sb):
    own = _xattn_own_mask()
    for j in range(sb):
        _xattn_values(v_ref, o_ref, j, _xattn_probs(q_ref, k_ref, j, own))


def _xattn_step(xq, ck, cv, *, sb, name):
    nb = xq.shape[0]
    assert XA_HEADS & (XA_HEADS - 1) == 0
    assert nb % sb == 0 and ck.shape == (nb, N_MEM, XA_HEADS, XA_DH)
    cache_spec = pl.BlockSpec((sb, N_MEM, XA_HEADS, XA_DH), lambda i: (i, 0, 0, 0))
    q_spec = pl.BlockSpec((sb, XA_HEADS, XA_DH), lambda i: (i, 0, 0))
    return pl.pallas_call(
        functools.partial(_xattn_step_kernel, sb=sb),
        grid=(nb // sb,),
        in_specs=[q_spec, cache_spec, cache_spec],
        out_specs=q_spec,
        out_shape=jax.ShapeDtypeStruct((nb, XA_HEADS, XA_DH), F32),
        compiler_params=_params(1, 40),
        name=name,
    )(xq.reshape(nb, XA_HEADS, XA_DH), ck, cv).reshape(nb, XA_HEADS * XA_DH)


def _tiles(rows):
    tm = min(rows, 1024)
    assert rows % tm == 0
    return tm


def kernel(x_prompt, x_sample, mem_prompt, state_rg_h, state_rg_conv, state_ml_C, state_ml_n, state_ml_m, cache_mem_k, cache_mem_v, g_mix, w_in, conv_w, conv_b, w_rg_a, b_rg_a, w_rg_x, b_rg_x, rg_lambda, b_ml_i, b_ml_f, g_rg_out, g_ml_out, w_out, g_xa, g_mem, w_xa_q, w_xa_k, w_xa_v, w_xa_o, g_ffn, w_ffn_gate, w_ffn_up, w_ffn_down, g_final):
    depth = g_mix.shape[0]
    assert depth == 1, "single trunk layer"
    bp, seq, d = x_prompt.shape
    bs_, dec_seq, _ = x_sample.shape
    assert d == D_MODEL and dec_seq == 1
    n_mem = mem_prompt.shape[1]
    assert n_mem == N_MEM
    dff = w_ffn_gate.shape[-1]
    in_w = w_in.shape[-1]
    assert in_w == IN_MAIN + N_GATE

    w_in_t = jnp.swapaxes(w_in, 1, 2).reshape(in_w, d)
    w_gate_pad = jnp.pad(w_in_t[IN_MAIN:], ((0, V7X_LANES - N_GATE), (0, 0)))
    cw = conv_w.reshape(CONV_W, RG_WIDTH)
    wgate = jnp.concatenate([w_rg_a.reshape(RG_BLOCKS, RG_BLOCK, RG_BLOCK),
                             w_rg_x.reshape(RG_BLOCKS, RG_BLOCK, RG_BLOCK)], axis=-1)
    rg_args = (cw, conv_b.reshape(-1), wgate, b_rg_a.reshape(-1), b_rg_x.reshape(-1),
               rg_lambda.reshape(-1), g_rg_out.reshape(-1))
    b_i = b_ml_i.reshape(ML_HEADS)
    b_f = b_ml_f.reshape(ML_HEADS)
    g_ml = g_ml_out.reshape(-1)
    w_out2 = w_out.reshape(d, d)
    w_q = w_xa_q.reshape(d, d)
    w_k = w_xa_k.reshape(d, d)
    w_v = w_xa_v.reshape(d, d)
    w_o = w_xa_o.reshape(d, d)
    w_fg = w_ffn_gate.reshape(d, dff)
    w_fu = w_ffn_up.reshape(d, dff)
    w_fd = w_ffn_down.reshape(dff, d)

    row_tile = _tiles(bp * seq)
    rg_zero_conv = jnp.zeros((bp, CONV_W - 1, RG_WIDTH), F32)
    rg_zero_h = jnp.zeros((bp, RG_WIDTH), F32)

    xs = x_sample.reshape(bs_, d)
    z_s, zg_s, w_in16 = _norm_linear(
        xs, g_mix.reshape(-1), w_in_t, n_out=IN_MAIN, tm=bs_, tn=1024, w_gate=w_gate_pad,
        w_is_nk=True, emit_w16=True, name="in_proj_s")
    y_rg_s, s_h, s_conv = _rglru_step(
        z_s, state_rg_conv.reshape(bs_, CONV_W - 1, RG_WIDTH), state_rg_h.reshape(bs_, RG_WIDTH),
        *rg_args, name="rglru_s")
    mlstm_s_job = _mlstm_step_job(
        z_s, zg_s, state_ml_C.reshape(bs_, ML_HEADS, ML_DK, ML_DV),
        state_ml_n.reshape(bs_, ML_HEADS, ML_DK), state_ml_m.reshape(bs_, ML_HEADS),
        b_i, b_f, g_ml, bs=V7X_SUBLANES)
    tp = bp * seq
    xp = x_prompt.reshape(tp, d)
    in_cols, in_rows = IN_MAIN // 1024, tp // row_tile
    if mlstm_s_job.n_blocks % in_rows == 0 and mlstm_s_job.n_blocks // in_rows < in_cols:
        z_p, zg_p, y_ml_s, s_c, s_n, s_m = _norm_linear(
            xp, g_mix.reshape(-1), w_in16, n_out=IN_MAIN, tm=row_tile, tn=1024, w_gate=w_gate_pad,
            w_is_nk=True, side=mlstm_s_job, vmem_mib=60, name="in_proj_p")
    else:
        z_p, zg_p = _norm_linear(xp, g_mix.reshape(-1), w_in16, n_out=IN_MAIN, tm=row_tile,
                                 tn=1024, w_gate=w_gate_pad, w_is_nk=True, name="in_proj_p")
        y_ml_s, s_c, s_n, s_m = _run_side_job(mlstm_s_job, vmem_mib=32, name="mlstm_s")
    x1_s, w_out16_rg, w_out16_ml = _linear_residual(
        [y_rg_s, y_ml_s], [(w_out2, 0), (w_out2, 1)], xs, tm=bs_, tn=1024, emit_w16=True,
        name="mix_out_s")
    xq_s, w_q16 = _norm_linear(x1_s, g_xa.reshape(-1), w_q, n_out=d, tm=bs_, tn=1024,
                               emit_w16=True, name="xa_q_s")
    ck = cache_mem_k.reshape(bs_, n_mem, XA_HEADS, XA_DH)
    cv = cache_mem_v.reshape(bs_, n_mem, XA_HEADS, XA_DH)

    rg_tl = min(seq, 256)
    rg_steps = bp * (seq // rg_tl)
    if bs_ % rg_steps == 0 and RG_BLOCKS % (bs_ // rg_steps) == 0:
        y_rg_p, p_h, p_conv, o_s = _rglru_seq(
            z_p, rg_zero_conv, rg_zero_h, *rg_args, batch=bp, seq=seq, tl=rg_tl,
            side_xattn=(xq_s, ck, cv), name="rglru_p")
        o_s = o_s.reshape(bs_, d)
    else:
        y_rg_p, p_h, p_conv = _rglru_seq(z_p, rg_zero_conv, rg_zero_h, *rg_args, batch=bp,
                                         seq=seq, tl=rg_tl, name="rglru_p")
        o_s = _xattn_step(xq_s, ck, cv, sb=2, name="xattn_s")

    x3_s, w_o16 = _linear_residual([o_s], [(w_o, 0)], x1_s, tm=bs_, tn=1024, emit_w16=True,
                                   name="xa_out_s")
    y_s, w_fg16, w_fu16, w_fd16 = _ffn(x3_s, g_ffn.reshape(-1), w_fg, w_fu, w_fd, g_final,
                                       tm=bs_, tf=512, emit_w16=True, vmem_mib=48, name="ffn_s")

    y_ml_p, p_c, p_n, p_m = _mlstm_seq(z_p, zg_p, b_i, b_f, g_ml, batch=bp, seq=seq,
                                       cs=min(seq, ML_CHUNK), name="mlstm_p")
    mem2 = mem_prompt.reshape(bp * n_mem, d)
    tmem = min(bp * n_mem, 256)
    mk, mk_heads = _norm_linear(mem2, g_mem.reshape(-1), w_k, n_out=d, tm=tmem, tn=d,
                                heads_out=(XA_HEADS, XA_DH), name="mem_k")
    mv, mv_heads = _norm_linear(mem2, g_mem.reshape(-1), w_v, n_out=d, tm=tmem, tn=d,
                                heads_out=(XA_HEADS, XA_DH), name="mem_v")
    proj_tile = min(row_tile, 512)
    x1_p = _linear_residual([y_rg_p, y_ml_p], [(w_out16_rg, 0), (w_out16_ml, 0)], xp,
                            tm=proj_tile, tn=d, name="mix_out_p")
    x3_p = _xattn_block(x1_p, g_xa.reshape(-1), w_q16, mk, mv, w_o16, batch=bp, seq=seq,
                        tq=min(seq, proj_tile), name="xattn_p")
    y_p = _ffn(x3_p, g_ffn.reshape(-1), w_fg16, w_fu16, w_fd16, g_final, tm=row_tile, tf=512,
               vmem_mib=60, name="ffn_p")

    return (
        y_p.reshape(bp, seq, d),
        y_s.reshape(bs_, 1, d),
        p_h.reshape(1, bp, RG_WIDTH),
        p_conv.reshape(1, bp, CONV_W - 1, RG_WIDTH),
        p_c.reshape(1, bp, ML_HEADS, ML_DK, ML_DV),
        p_n.reshape(1, bp, ML_HEADS, ML_DK),
        p_m[:, :, 0].reshape(1, bp, ML_HEADS),
        mk_heads.reshape(1, bp, n_mem, XA_HEADS, XA_DH),
        mv_heads.reshape(1, bp, n_mem, XA_HEADS, XA_DH),
        s_h.reshape(1, bs_, RG_WIDTH),
        s_conv.reshape(1, bs_, CONV_W - 1, RG_WIDTH),
        s_c.reshape(1, bs_, ML_HEADS, ML_DK, ML_DV),
        s_n.reshape(1, bs_, ML_HEADS, ML_DK),
        s_m.reshape(1, bs_, ML_HEADS),
    )
```

```python
import functools
from typing import Callable, NamedTuple

import jax
import jax.numpy as jnp
from jax import lax
from jax.experimental import pallas as pl
from jax.experimental.pallas import tpu as pltpu

F32 = jnp.float32
BF16 = jnp.bfloat16

D_MODEL = 2048
RG_WIDTH = D_MODEL // 2
RG_BLOCKS = 8
RG_BLOCK = RG_WIDTH // RG_BLOCKS
CONV_W = 4
RG_C = 8.0
ML_HEADS = 4
ML_WIDTH = D_MODEL - RG_WIDTH
ML_DV = ML_WIDTH // ML_HEADS
ML_DK = ML_DV // 2
N_MEM = 256
XA_HEADS = 4
XA_DH = D_MODEL // XA_HEADS
EPS = 1e-6
NEG = -1e30

OFF_RGX = 0
OFF_RGG = OFF_RGX + RG_WIDTH
OFF_Q = OFF_RGG + RG_WIDTH
OFF_K = OFF_Q + ML_HEADS * ML_DK
OFF_V = OFF_K + ML_HEADS * ML_DK
OFF_O = OFF_V + ML_WIDTH
OFF_I = OFF_O + ML_WIDTH
IN_MAIN = OFF_I
N_GATE = 2 * ML_HEADS

V7X_LANES = 128
V7X_SUBLANES = 8
V7X_VMEM_BYTES = 64 * 2**20

ML_CHUNK = 256


class SideJob(NamedTuple):
    n_blocks: int
    parts: int
    args: tuple
    in_specs: Callable
    out_shape: tuple
    out_specs: Callable
    scratch_shapes: tuple
    body: Callable


def _params(n_axes, vmem_mib):
    assert vmem_mib * 2**20 <= V7X_VMEM_BYTES
    return pltpu.CompilerParams(
        dimension_semantics=("arbitrary",) * n_axes,
        vmem_limit_bytes=vmem_mib * 2**20,
    )


def _rms(x, g):
    ms = jnp.mean(x * x, axis=-1, keepdims=True)
    return x * lax.rsqrt(ms + EPS) * g


def _softplus(u):
    return jnp.maximum(u, 0.0) + jnp.log1p(jnp.exp(-jnp.abs(u)))


def _log_sigmoid(u):
    return -_softplus(-u)


def _sigmoid(u):
    return 0.5 * jnp.tanh(0.5 * u) + 0.5


def _gelu_tanh(x):
    c = 0.7978845608028654
    half_x = 0.5 * x
    return half_x + half_x * jnp.tanh(x * (c + (c * 0.044715) * (x * x)))


def _sqrt_nonneg(v):
    return jnp.where(v > 0.0, v * lax.rsqrt(v), 0.0)


def _mm(a, b):
    return jnp.dot(a.astype(BF16), b.astype(BF16), preferred_element_type=F32)


def _dot_w(a, w_ref, w_is_nk):
    w = w_ref[...].astype(BF16)
    if w_is_nk:
        return lax.dot_general(a, w, (((1,), (1,)), ((), ())), preferred_element_type=F32)
    return jnp.dot(a, w, preferred_element_type=F32)


def _norm_linear_kernel(*refs, with_gate, w_is_nk, emit_w16, heads_out, side, side_cols):
    refs = list(refs)
    n_si, n_so, n_ss = ((len(side.args), len(side.out_shape), len(side.scratch_shapes))
                        if side else (0, 0, 0))
    n_in = 3 + with_gate + n_si
    n_out = 1 + bool(heads_out) + with_gate + emit_w16 + n_so
    x_ref, g_ref, w_ref = refs[:3]
    wg_ref = refs[3] if with_gate else None
    side_in = refs[3 + with_gate:n_in]
    outs = refs[n_in:n_in + n_out]
    xn_ref = refs[n_in + n_out]
    side_scratch = refs[n_in + n_out + 1:]
    o_ref = outs[0]
    oh_ref = outs[1] if heads_out else None
    og_ref = outs[1 + bool(heads_out)] if with_gate else None
    w16_ref = outs[n_out - n_so - 1] if emit_w16 else None
    side_out = outs[n_out - n_so:]
    col = pl.program_id(1)

    def column_tile(xn):
        acc = _dot_w(xn, w_ref, w_is_nk)
        o_ref[...] = acc.astype(o_ref.dtype)
        if heads_out:
            oh_ref[...] = acc.reshape(oh_ref.shape)
        if emit_w16:
            w16_ref[...] = w_ref[...].astype(BF16)

    @pl.when(col == 0)
    def _():
        xn = _rms(x_ref[...], g_ref[...]).astype(BF16)
        xn_ref[...] = xn
        if with_gate:
            og_ref[...] = _dot_w(xn, wg_ref, w_is_nk)
        column_tile(xn)

    if side:
        for part in range(side.parts):
            @pl.when((col > 0) & (col <= side_cols) & ((col - 1) % side.parts == part))
            def _(part=part):
                column_tile(xn_ref[...])
                side.body(side_in, side_out, side_scratch, part)

    @pl.when(col > (side_cols if side else 0))
    def _():
        column_tile(xn_ref[...])


def _norm_linear(x, g, w, *, n_out, tm, tn, out_dtype=F32, w_gate=None, w_is_nk=False,
                 emit_w16=False, heads_out=None, side=None, vmem_mib=48, name):
    m, k = x.shape
    k_ax, n_ax = (1, 0) if w_is_nk else (0, 1)
    assert m % tm == 0 and n_out % tn == 0 and w.shape[k_ax] == k and n_out <= w.shape[n_ax]
    assert not emit_w16 or m == tm
    assert heads_out is None or (tn == n_out and heads_out[0] * heads_out[1] == n_out)
    n_rows, n_cols = m // tm, n_out // tn
    with_gate = w_gate is not None
    w_mode = dict(pipeline_mode=pl.Buffered(1)) if tn == n_out else {}
    w_spec = (pl.BlockSpec((tn, k), lambda i, j: (j, 0), **w_mode) if w_is_nk
              else pl.BlockSpec((k, tn), lambda i, j: (0, j), **w_mode))
    in_specs = [
        pl.BlockSpec((tm, k), lambda i, j: (i, 0)),
        pl.BlockSpec((1, k), lambda i, j: (0, 0)),
        w_spec,
    ]
    out_shape = [jax.ShapeDtypeStruct((m, n_out), out_dtype)]
    out_specs = [pl.BlockSpec((tm, tn), lambda i, j: (i, j))]
    scratch_shapes = [pltpu.VMEM((tm, k), BF16)]
    args = [x, g.reshape(1, k), w]
    if heads_out:
        out_shape.append(jax.ShapeDtypeStruct((m, *heads_out), F32))
        out_specs.append(pl.BlockSpec((tm, *heads_out), lambda i, j: (i, 0, 0)))
    if with_gate:
        ng = w_gate.shape[n_ax]
        in_specs.append(pl.BlockSpec(w_gate.shape, lambda i, j: (0, 0)))
        out_shape.append(jax.ShapeDtypeStruct((m, ng), F32))
        out_specs.append(pl.BlockSpec((tm, ng), lambda i, j: (i, 0)))
        args.append(w_gate)
    if emit_w16:
        out_shape.append(jax.ShapeDtypeStruct((n_out, k) if w_is_nk else (k, n_out), BF16))
        out_specs.append(pl.BlockSpec((tn, k), lambda i, j: (j, 0)) if w_is_nk
                         else pl.BlockSpec((k, tn), lambda i, j: (0, j)))
    side_cols = 0
    if side:
        assert side.n_blocks % n_rows == 0
        blocks_per_row = side.n_blocks // n_rows
        side_cols = blocks_per_row * side.parts
        assert 0 < side_cols < n_cols
        block_of = lambda i, j: (i * blocks_per_row
                                 + jnp.clip(j - 1, 0, side_cols - 1) // side.parts)
        in_specs += side.in_specs(block_of)
        out_specs += side.out_specs(block_of)
        out_shape += list(side.out_shape)
        scratch_shapes += list(side.scratch_shapes)
        args += list(side.args)
    out = pl.pallas_call(
        functools.partial(_norm_linear_kernel, with_gate=with_gate, w_is_nk=w_is_nk,
                          emit_w16=emit_w16, heads_out=heads_out, side=side, side_cols=side_cols),
        grid=(n_rows, n_cols),
        in_specs=in_specs,
        out_specs=out_specs,
        out_shape=out_shape,
        scratch_shapes=scratch_shapes,
        compiler_params=_params(2, vmem_mib),
        name=name,
    )(*args)
    return out if len(out) > 1 else out[0]


def _linear_res_kernel(*refs, n_in, emit_w16):
    a_refs = refs[:n_in]
    w_refs = refs[n_in:2 * n_in]
    res_ref = refs[2 * n_in]
    o_ref = refs[2 * n_in + 1]
    acc = res_ref[...]
    for a_ref, w_ref in zip(a_refs, w_refs):
        acc = acc + _mm(a_ref[...], w_ref[...])
    o_ref[...] = acc
    if emit_w16:
        for w_ref, wc_ref in zip(w_refs, refs[2 * n_in + 2:]):
            wc_ref[...] = w_ref[...].astype(BF16)


def _linear_residual(parts, weights, res, *, tm, tn, emit_w16=False, vmem_mib=48, name):
    m, n = res.shape
    kp = parts[0].shape[1]
    assert all(p.shape == (m, kp) for p in parts) and len(weights) == len(parts)
    assert m % tm == 0 and n % tn == 0 and (not emit_w16 or m == tm)
    n_in = len(parts)
    in_specs = [pl.BlockSpec((tm, kp), lambda i, j: (i, 0)) for _ in parts]
    w_mode = dict(pipeline_mode=pl.Buffered(1)) if tn == n else {}
    in_specs += [pl.BlockSpec((kp, tn), lambda i, j, rb=rb: (rb, j), **w_mode)
                 for _, rb in weights]
    in_specs.append(pl.BlockSpec((tm, tn), lambda i, j: (i, j)))
    out_specs = [pl.BlockSpec((tm, tn), lambda i, j: (i, j))]
    out_shape = [jax.ShapeDtypeStruct((m, n), F32)]
    if emit_w16:
        out_specs += [pl.BlockSpec((kp, tn), lambda i, j: (0, j)) for _ in parts]
        out_shape += [jax.ShapeDtypeStruct((kp, n), BF16) for _ in parts]
    out = pl.pallas_call(
        functools.partial(_linear_res_kernel, n_in=n_in, emit_w16=emit_w16),
        grid=(m // tm, n // tn),
        in_specs=in_specs,
        out_specs=out_specs,
        out_shape=out_shape,
        compiler_params=_params(2, vmem_mib),
        name=name,
    )(*parts, *[w for w, _ in weights], res)
    return out if emit_w16 else out[0]


def _ffn_kernel(x_ref, g_ref, wg_ref, wu_ref, wd_ref, gf_ref, o_ref, *rest, emit_w16):
    xf_ref = rest[-1]
    f = pl.program_id(1)
    last = pl.num_programs(1) - 1

    def hidden_tile(xf):
        wg = wg_ref[...].astype(BF16)
        wu = wu_ref[...].astype(BF16)
        wd = wd_ref[...].astype(BF16)
        if emit_w16:
            for dst, val in zip(rest[:3], (wg, wu, wd)):
                dst[...] = val
        gate = jnp.dot(xf, wg, preferred_element_type=F32)
        up = jnp.dot(xf, wu, preferred_element_type=F32)
        hidden = (gate * _sigmoid(gate)) * up
        return jnp.dot(hidden.astype(BF16), wd, preferred_element_type=F32)

    @pl.when(f == 0)
    def _():
        x = x_ref[...]
        xf = _rms(x, g_ref[...]).astype(BF16)
        xf_ref[...] = xf
        o_ref[...] = x + hidden_tile(xf)

    @pl.when((f > 0) & (f < last))
    def _():
        o_ref[...] += hidden_tile(xf_ref[...])

    @pl.when(f == last)
    def _():
        o_ref[...] = _rms(o_ref[...] + hidden_tile(xf_ref[...]), gf_ref[...])


def _ffn(x, g, w_gate, w_up, w_down, g_final, *, tm, tf, emit_w16=False, vmem_mib, name):
    m, d = x.shape
    dff = w_gate.shape[1]
    assert m % tm == 0 and dff % tf == 0 and dff // tf >= 2 and (not emit_w16 or m == tm)
    up_spec = pl.BlockSpec((d, tf), lambda i, f: (0, f))
    down_spec = pl.BlockSpec((tf, d), lambda i, f: (f, 0))
    out_specs = [pl.BlockSpec((tm, d), lambda i, f: (i, 0))]
    out_shape = [jax.ShapeDtypeStruct((m, d), F32)]
    if emit_w16:
        out_specs += [up_spec, up_spec, down_spec]
        out_shape += [jax.ShapeDtypeStruct(w.shape, BF16) for w in (w_gate, w_up, w_down)]
    out = pl.pallas_call(
        functools.partial(_ffn_kernel, emit_w16=emit_w16),
        grid=(m // tm, dff // tf),
        in_specs=[
            pl.BlockSpec((tm, d), lambda i, f: (i, 0)),
            pl.BlockSpec((1, d), lambda i, f: (0, 0)),
            up_spec,
            up_spec,
            down_spec,
            pl.BlockSpec((1, d), lambda i, f: (0, 0)),
        ],
        out_specs=out_specs,
        out_shape=out_shape,
        scratch_shapes=[pltpu.VMEM((tm, d), BF16)],
        compiler_params=_params(2, vmem_mib),
        name=name,
    )(x, g.reshape(1, d), w_gate, w_up, w_down, g_final.reshape(1, d))
    return out if emit_w16 else out[0]


RG_GATE_LOOKAHEAD = 2


def _rg_gates(xr, wgate_ref, ba_ref, bx_ref, lam_ref, a_ref, b_ref, side=None):
    blocks = [slice(n * RG_BLOCK, (n + 1) * RG_BLOCK) for n in range(RG_BLOCKS)]
    gates = {}

    def issue_gate(n):
        if n < RG_BLOCKS:
            gates[n] = _mm(xr[:, blocks[n]], wgate_ref[n])

    rows, score_phase, value_phase = side if side else (0, None, None)
    per = RG_BLOCKS // rows if rows else RG_BLOCKS
    scores = {}
    for n in range(RG_GATE_LOOKAHEAD):
        issue_gate(n)
    for n, sl in enumerate(blocks):
        if rows and n % per == 0:
            j = n // per
            scores[j] = score_phase(j)
            if j > 0:
                value_phase(j - 1, scores.pop(j - 1))
        xn = xr[:, sl]
        g = gates.pop(n)
        issue_gate(n + RG_GATE_LOOKAHEAD)
        r = _sigmoid(g[:, :RG_BLOCK] + ba_ref[:, sl])
        ig = _sigmoid(g[:, RG_BLOCK:] + bx_ref[:, sl])
        a = jnp.exp(r * (-RG_C * _softplus(-lam_ref[:, sl])))
        a_ref[:, sl] = a
        mult = _sqrt_nonneg(jnp.maximum(1.0 - a * a, 0.0))
        b_ref[:, sl] = mult * (ig * xn)
    if rows:
        value_phase(rows - 1, scores.pop(rows - 1))


def _rglru_seq_kernel(*refs, tl, side_rows):
    (zx_ref, zg_ref, conv0_ref, h0_ref, cw_ref, cb_ref, wgate_ref, ba_ref, bx_ref, lam_ref,
     gout_ref) = refs[:11]
    xe_ref, a_ref, b_ref, h_ref, hc_ref = refs[-5:]
    if side_rows:
        sq_ref, sk_ref, sv_ref, y_ref, hlast_ref, convn_ref, so_ref = refs[11:-5]
    else:
        y_ref, hlast_ref, convn_ref = refs[11:-5]
    t = pl.program_id(1)
    pad = V7X_SUBLANES

    @pl.when(t == 0)
    def _():
        xe_ref[pad - 3:pad, :] = conv0_ref[...]
        hc_ref[...] = h0_ref[...]

    @pl.when(t > 0)
    def _():
        xe_ref[pad - 3:pad, :] = xe_ref[tl + pad - 3:tl + pad, :]

    x = zx_ref[...]
    xe_ref[pad:tl + pad, :] = x
    xr = (xe_ref[pad - 3:tl + pad - 3, :] * cw_ref[0:1, :]
          + xe_ref[pad - 2:tl + pad - 2, :] * cw_ref[1:2, :]
          + xe_ref[pad - 1:tl + pad - 1, :] * cw_ref[2:3, :]
          + x * cw_ref[3:4, :]) + cb_ref[...]
    side = None
    if side_rows:
        own = _xattn_own_mask()
        side = (side_rows,
                lambda j: _xattn_probs(sq_ref, sk_ref, j, own),
                lambda j, p: _xattn_values(sv_ref, so_ref, j, p))

    _rg_gates(xr, wgate_ref, ba_ref, bx_ref, lam_ref, a_ref, b_ref, side=side)

    row = lax.broadcasted_iota(jnp.int32, (V7X_SUBLANES, RG_WIDTH), 0)

    def group(gi, hc):
        r0 = pl.multiple_of(gi * V7X_SUBLANES, V7X_SUBLANES)
        a8 = a_ref[pl.ds(r0, V7X_SUBLANES), :]
        b8 = b_ref[pl.ds(r0, V7X_SUBLANES), :]
        for d in (1, 2, 4):
            keep = row >= d
            b8 = jnp.where(keep, a8 * pltpu.roll(b8, d, axis=0) + b8, b8)
            a8 = jnp.where(keep, a8 * pltpu.roll(a8, d, axis=0), a8)
        h8 = a8 * hc + b8
        h_ref[pl.ds(r0, V7X_SUBLANES), :] = h8
        return h8[V7X_SUBLANES - 1:V7X_SUBLANES, :]

    hc = lax.fori_loop(0, tl // V7X_SUBLANES, group, hc_ref[...], unroll=4)
    hc_ref[...] = hc
    hlast_ref[...] = hc
    convn_ref[...] = xe_ref[tl + pad - 3:tl + pad, :]
    y = h_ref[...] * _gelu_tanh(zg_ref[...])
    y_ref[...] = _rms(y, gout_ref[...]).astype(y_ref.dtype)


def _rglru_seq(z, conv0, h0, cw, cb, wgate, ba, bx, lam, gout, *, batch, seq, tl, name,
               side_xattn=None):
    nt = seq // tl
    assert seq % tl == 0
    w = RG_WIDTH
    row = lambda v: v.reshape(1, w)
    const2 = lambda b, t: (0, 0)
    in_specs = [
        pl.BlockSpec((tl, w), lambda b, t: (b * nt + t, OFF_RGX // w)),
        pl.BlockSpec((tl, w), lambda b, t: (b * nt + t, OFF_RGG // w)),
        pl.BlockSpec((None, CONV_W - 1, w), lambda b, t: (b, 0, 0)),
        pl.BlockSpec((None, 1, w), lambda b, t: (b, 0, 0)),
        pl.BlockSpec((CONV_W, w), const2),
        pl.BlockSpec((1, w), const2),
        pl.BlockSpec((RG_BLOCKS, RG_BLOCK, 2 * RG_BLOCK), lambda b, t: (0, 0, 0)),
        pl.BlockSpec((1, w), const2),
        pl.BlockSpec((1, w), const2),
        pl.BlockSpec((1, w), const2),
        pl.BlockSpec((1, w), const2),
    ]
    out_specs = [
        pl.BlockSpec((tl, w), lambda b, t: (b * nt + t, 0)),
        pl.BlockSpec((None, 1, w), lambda b, t: (b, 0, 0)),
        pl.BlockSpec((None, CONV_W - 1, w), lambda b, t: (b, 0, 0)),
    ]
    out_shape = [
        jax.ShapeDtypeStruct((batch * seq, w), BF16),
        jax.ShapeDtypeStruct((batch, 1, w), F32),
        jax.ShapeDtypeStruct((batch, CONV_W - 1, w), F32),
    ]
    args = [z, z, conv0, h0.reshape(batch, 1, w), cw, row(cb), wgate, row(ba), row(bx), row(lam),
            row(gout)]
    side_rows = 0
    vmem_mib = 32
    if side_xattn is not None:
        xq, ck, cv = side_xattn
        nb = xq.shape[0]
        assert nb % (batch * nt) == 0 and ck.shape == (nb, N_MEM, XA_HEADS, XA_DH)
        side_rows = nb // (batch * nt)
        q_spec = pl.BlockSpec((side_rows, XA_HEADS, XA_DH), lambda b, t: (b * nt + t, 0, 0))
        cache_spec = pl.BlockSpec((side_rows, N_MEM, XA_HEADS, XA_DH),
                                  lambda b, t: (b * nt + t, 0, 0, 0))
        in_specs += [q_spec, cache_spec, cache_spec]
        out_specs.append(q_spec)
        out_shape.append(jax.ShapeDtypeStruct((nb, XA_HEADS, XA_DH), F32))
        args += [xq.reshape(nb, XA_HEADS, XA_DH), ck, cv]
        vmem_mib = 56
    return pl.pallas_call(
        functools.partial(_rglru_seq_kernel, tl=tl, side_rows=side_rows),
        grid=(batch, nt),
        in_specs=in_specs,
        out_specs=out_specs,
        out_shape=out_shape,
        scratch_shapes=[
            pltpu.VMEM((tl + V7X_SUBLANES, w), F32),
            pltpu.VMEM((tl, w), F32),
            pltpu.VMEM((tl, w), F32),
            pltpu.VMEM((tl, w), F32),
            pltpu.VMEM((1, w), F32),
        ],
        compiler_params=_params(2, vmem_mib),
        name=name,
    )(*args)


def _rglru_step_kernel(zx_ref, zg_ref, conv_ref, h0_ref, cw_ref, cb_ref, wgate_ref, ba_ref,
                       bx_ref, lam_ref, gout_ref, y_ref, hn_ref, convn_ref, a_ref, b_ref):
    w = RG_WIDTH
    x = zx_ref[...]
    xr = (conv_ref[:, 0:w] * cw_ref[0:1, :] + conv_ref[:, w:2 * w] * cw_ref[1:2, :]
          + conv_ref[:, 2 * w:3 * w] * cw_ref[2:3, :] + x * cw_ref[3:4, :]) + cb_ref[...]
    _rg_gates(xr, wgate_ref, ba_ref, bx_ref, lam_ref, a_ref, b_ref)
    h = a_ref[...] * h0_ref[...] + b_ref[...]
    hn_ref[...] = h
    convn_ref[:, 0:2 * w] = conv_ref[:, w:3 * w]
    convn_ref[:, 2 * w:3 * w] = x
    y_ref[...] = _rms(h * _gelu_tanh(zg_ref[...]), gout_ref[...]).astype(y_ref.dtype)


def _rglru_step(z, conv, h0, cw, cb, wgate, ba, bx, lam, gout, *, name):
    nb = z.shape[0]
    w = RG_WIDTH
    row = lambda v: v.reshape(1, w)
    c0 = lambda i: (0, 0)
    return pl.pallas_call(
        _rglru_step_kernel,
        grid=(1,),
        in_specs=[
            pl.BlockSpec((nb, w), lambda i: (0, OFF_RGX // w)),
            pl.BlockSpec((nb, w), lambda i: (0, OFF_RGG // w)),
            pl.BlockSpec((nb, (CONV_W - 1) * w), c0),
            pl.BlockSpec((nb, w), c0),
            pl.BlockSpec((CONV_W, w), c0),
            pl.BlockSpec((1, w), c0),
            pl.BlockSpec((RG_BLOCKS, RG_BLOCK, 2 * RG_BLOCK), lambda i: (0, 0, 0)),
            pl.BlockSpec((1, w), c0),
            pl.BlockSpec((1, w), c0),
            pl.BlockSpec((1, w), c0),
            pl.BlockSpec((1, w), c0),
        ],
        out_specs=[
            pl.BlockSpec((nb, w), c0),
            pl.BlockSpec((nb, w), c0),
            pl.BlockSpec((nb, (CONV_W - 1) * w), c0),
        ],
        out_shape=[
            jax.ShapeDtypeStruct((nb, w), BF16),
            jax.ShapeDtypeStruct((nb, w), F32),
            jax.ShapeDtypeStruct((nb, (CONV_W - 1) * w), F32),
        ],
        scratch_shapes=[pltpu.VMEM((nb, w), F32), pltpu.VMEM((nb, w), F32)],
        compiler_params=_params(1, 32),
        name=name,
    )(z, z, conv.reshape(nb, (CONV_W - 1) * w), h0, cw, row(cb), wgate, row(ba), row(bx),
      row(lam), row(gout))


def _mlstm_seq_kernel(bi_ref, bf_ref, q_ref, k_ref, v_ref, o_ref, zg_ref, g_ref,
                      y_ref, c_ref, n_ref, m_ref, zgt_ref, cs_ref, ns_ref, *, seq, cs):
    h = pl.program_id(1)
    nc = seq // cs
    bi = bi_ref[h]
    bf = bf_ref[h]
    for c in range(nc):
        zgt_ref[c] = zg_ref[c * cs:(c + 1) * cs, :].T
    cs_ref[...] = jnp.zeros_like(cs_ref)
    ns_ref[...] = jnp.zeros_like(ns_ref)

    t_idx = lax.broadcasted_iota(jnp.int32, (cs, cs), 0)
    s_idx = lax.broadcasted_iota(jnp.int32, (cs, cs), 1)
    causal = s_idx <= t_idx
    lane = lax.broadcasted_iota(jnp.int32, (cs, V7X_LANES), 1)

    def chunk(c, m):
        r0 = pl.multiple_of(c * cs, cs)
        q = q_ref[pl.ds(r0, cs), :]
        k = k_ref[pl.ds(r0, cs), :] * (ML_DK ** -0.5)
        v = v_ref[pl.ds(r0, cs), :]
        zg = zg_ref[pl.ds(r0, cs), :]
        li_col = jnp.sum(jnp.where(lane == h, zg, 0.0), axis=1, keepdims=True) + bi
        lf_col = _log_sigmoid(
            jnp.sum(jnp.where(lane == h + ML_HEADS, zg, 0.0), axis=1, keepdims=True) + bf)
        li_row = zgt_ref[c, pl.ds(h, 1), :] + bi
        lf_row = _log_sigmoid(zgt_ref[c, pl.ds(h + ML_HEADS, 1), :] + bf)
        bcum_col = jnp.sum(jnp.where(causal, lf_row, 0.0), axis=1, keepdims=True)
        bcum_row = jnp.sum(jnp.where(t_idx <= s_idx, lf_col, 0.0), axis=0, keepdims=True)
        log_d = jnp.where(causal, bcum_col - bcum_row + li_row, NEG)
        inter = bcum_col + m
        m_t = jnp.maximum(inter, jnp.max(log_d, axis=1, keepdims=True))
        dmat = jnp.exp(log_d - m_t)
        sc = jnp.exp(inter - m_t)
        qb = q.astype(BF16)
        kb = k.astype(BF16)
        vb = v.astype(BF16)
        qk = lax.dot_general(qb, kb, (((1,), (1,)), ((), ())), preferred_element_type=F32) * dmat
        c_old = cs_ref[...]
        n_old = ns_ref[...]
        num = sc * jnp.dot(qb, c_old.astype(BF16), preferred_element_type=F32) + _mm(qk, vb)
        den = sc * jnp.sum(q * n_old, axis=1, keepdims=True) + jnp.sum(qk, axis=1, keepdims=True)
        den = jnp.maximum(jnp.abs(den), jnp.exp(-m_t))
        hh = num / den
        m_new = m_t[cs - 1:cs, :]
        b_last = bcum_col[cs - 1:cs, :]
        w_end = jnp.exp(b_last - bcum_col + li_col - m_new)
        dec = jnp.exp(b_last + m - m_new)
        wk = w_end * k
        cs_ref[...] = dec * c_old + lax.dot_general(
            wk.astype(BF16), vb, (((0,), (0,)), ((), ())), preferred_element_type=F32)
        ns_ref[...] = dec * n_old + jnp.sum(wk, axis=0, keepdims=True)
        y = _rms(hh, g_ref[...]) * _sigmoid(o_ref[pl.ds(r0, cs), :])
        y_ref[pl.ds(r0, cs), :] = y.astype(y_ref.dtype)
        return m_new

    m_fin = lax.fori_loop(0, nc, chunk, jnp.zeros((1, 1), F32))
    c_ref[...] = cs_ref[...]
    n_ref[pl.ds(h, 1), :] = ns_ref[...]
    m_ref[pl.ds(h, 1), :] = jnp.broadcast_to(m_fin, (1, V7X_LANES))


def _mlstm_seq(z, zg, b_i, b_f, g_out, *, batch, seq, cs, name):
    assert seq % cs == 0 and zg.shape[1] == V7X_LANES
    dk, dv, nh = ML_DK, ML_DV, ML_HEADS
    smem = pl.BlockSpec(memory_space=pltpu.SMEM)
    return pl.pallas_call(
        functools.partial(_mlstm_seq_kernel, seq=seq, cs=cs),
        grid=(batch, nh),
        in_specs=[
            smem, smem,
            pl.BlockSpec((seq, dk), lambda b, h: (b, OFF_Q // dk + h)),
            pl.BlockSpec((seq, dk), lambda b, h: (b, OFF_K // dk + h)),
            pl.BlockSpec((seq, dv), lambda b, h: (b, OFF_V // dv + h)),
            pl.BlockSpec((seq, dv), lambda b, h: (b, OFF_O // dv + h)),
            pl.BlockSpec((seq, V7X_LANES), lambda b, h: (b, 0)),
            pl.BlockSpec((1, dv), lambda b, h: (0, h)),
        ],
        out_specs=[
            pl.BlockSpec((seq, dv), lambda b, h: (b, h)),
            pl.BlockSpec((None, None, dk, dv), lambda b, h: (b, h, 0, 0)),
            pl.BlockSpec((None, nh, dk), lambda b, h: (b, 0, 0)),
            pl.BlockSpec((None, nh, V7X_LANES), lambda b, h: (b, 0, 0)),
        ],
        out_shape=[
            jax.ShapeDtypeStruct((batch * seq, nh * dv), BF16),
            jax.ShapeDtypeStruct((batch, nh, dk, dv), F32),
            jax.ShapeDtypeStruct((batch, nh, dk), F32),
            jax.ShapeDtypeStruct((batch, nh, V7X_LANES), F32),
        ],
        scratch_shapes=[
            pltpu.VMEM((seq // cs, V7X_LANES, cs), F32),
            pltpu.VMEM((dk, dv), F32),
            pltpu.VMEM((1, dk), F32),
        ],
        compiler_params=_params(2, 40),
        name=name,
    )(b_i, b_f, z, z, z, z, zg, g_out.reshape(1, nh * dv))


def _mlstm_step_body(in_refs, out_refs, scratch_refs, part, *, bs, parts):
    bi_ref, bf_ref, q_ref, k_ref, v_ref, o_ref, zg_ref, g_ref, c0_ref, n0_ref, m0_ref = in_refs
    y_ref, c_ref, n_ref, m_ref = out_refs
    qc_ref, = scratch_refs
    dk, dv = ML_DK, ML_DV
    nr = bs // parts
    rows = slice(part * nr, (part + 1) * nr)
    eye = (lax.broadcasted_iota(jnp.int32, (dk, dk), 0)
           == lax.broadcasted_iota(jnp.int32, (dk, dk), 1))

    def as_column(row):
        return jnp.sum(jnp.where(eye, jnp.broadcast_to(row, (dk, dk)), 0.0), axis=1, keepdims=True)

    zg = zg_ref[rows, :]
    for h in range(ML_HEADS):
        li = zg[:, h:h + 1] + bi_ref[h]
        lf = _log_sigmoid(zg[:, ML_HEADS + h:ML_HEADS + h + 1] + bf_ref[h])
        m = m0_ref[rows, h:h + 1]
        inter = lf + m
        m_t = jnp.maximum(inter, li)
        dgate = jnp.exp(li - m_t)
        sc = jnp.exp(inter - m_t)
        q = q_ref[rows, h * dk:(h + 1) * dk]
        k = k_ref[rows, h * dk:(h + 1) * dk] * (ML_DK ** -0.5)
        v = v_ref[rows, h * dv:(h + 1) * dv]
        n_old = n0_ref[rows, h, :]
        qk = jnp.sum(q * k, axis=1, keepdims=True) * dgate
        w_end = jnp.exp(li - m_t)
        dec = jnp.exp(inter - m_t)
        wk = w_end * k
        for j in range(nr):
            c_old = c0_ref[part * nr + j, h]
            qc_ref[j:j + 1, :] = jnp.sum(as_column(q[j:j + 1, :]) * c_old, axis=0, keepdims=True)
            c_ref[part * nr + j, h] = (dec[j:j + 1, :] * c_old
                                       + as_column(wk[j:j + 1, :]) * v[j:j + 1, :])
        num = sc * qc_ref[0:nr, :] + qk * v
        den = sc * jnp.sum(q * n_old, axis=1, keepdims=True) + qk
        den = jnp.maximum(jnp.abs(den), jnp.exp(-m_t))
        hh = num / den
        n_ref[rows, h, :] = dec * n_old + wk
        m_ref[rows, h:h + 1] = m_t
        y = (_rms(hh, g_ref[:, h * dv:(h + 1) * dv])
             * _sigmoid(o_ref[rows, h * dv:(h + 1) * dv]))
        y_ref[rows, h * dv:(h + 1) * dv] = y.astype(y_ref.dtype)


def _mlstm_step_job(z, zg, c0, n0, m0, b_i, b_f, g_out, *, bs, parts):
    nb = z.shape[0]
    assert nb % bs == 0 and bs % parts == 0
    dk, dv, nh = ML_DK, ML_DV, ML_HEADS
    smem = pl.BlockSpec(memory_space=pltpu.SMEM)

    def in_specs(blk):
        return [
            smem, smem,
            pl.BlockSpec((bs, nh * dk), lambda *g: (blk(*g), OFF_Q // (nh * dk))),
            pl.BlockSpec((bs, nh * dk), lambda *g: (blk(*g), OFF_K // (nh * dk))),
            pl.BlockSpec((bs, nh * dv), lambda *g: (blk(*g), OFF_V // (nh * dv))),
            pl.BlockSpec((bs, nh * dv), lambda *g: (blk(*g), OFF_O // (nh * dv))),
            pl.BlockSpec((bs, V7X_LANES), lambda *g: (blk(*g), 0)),
            pl.BlockSpec((1, nh * dv), lambda *g: (0, 0)),
            pl.BlockSpec((bs, nh, dk, dv), lambda *g: (blk(*g), 0, 0, 0)),
            pl.BlockSpec((bs, nh, dk), lambda *g: (blk(*g), 0, 0)),
            pl.BlockSpec((bs, nh), lambda *g: (blk(*g), 0)),
        ]

    def out_specs(blk):
        return [
            pl.BlockSpec((bs, nh * dv), lambda *g: (blk(*g), 0)),
            pl.BlockSpec((bs, nh, dk, dv), lambda *g: (blk(*g), 0, 0, 0)),
            pl.BlockSpec((bs, nh, dk), lambda *g: (blk(*g), 0, 0)),
            pl.BlockSpec((bs, nh), lambda *g: (blk(*g), 0)),
        ]

    return SideJob(
        n_blocks=nb // bs,
        parts=parts,
        args=(b_i, b_f, z, z, z, z, zg, g_out.reshape(1, nh * dv), c0, n0, m0),
        in_specs=in_specs,
        out_shape=(
            jax.ShapeDtypeStruct((nb, nh * dv), F32),
            jax.ShapeDtypeStruct((nb, nh, dk, dv), F32),
            jax.ShapeDtypeStruct((nb, nh, dk), F32),
            jax.ShapeDtypeStruct((nb, nh), F32),
        ),
        out_specs=out_specs,
        scratch_shapes=(pltpu.VMEM((bs, dv), F32),),
        body=functools.partial(_mlstm_step_body, bs=bs, parts=parts),
    )


def _run_side_job(job, *, vmem_mib, name):
    n_in, n_out = len(job.args), len(job.out_shape)

    def body(*refs):
        for part in range(job.parts):
            job.body(refs[:n_in], refs[n_in:n_in + n_out], refs[n_in + n_out:], part)

    block_of = lambda i: i
    return pl.pallas_call(
        body,
        grid=(job.n_blocks,),
        in_specs=job.in_specs(block_of),
        out_specs=job.out_specs(block_of),
        out_shape=list(job.out_shape),
        scratch_shapes=list(job.scratch_shapes),
        compiler_params=_params(1, vmem_mib),
        name=name,
    )(*job.args)


def _softmax_rows(s):
    e = jnp.exp(s - jnp.max(s, axis=-1, keepdims=True))
    return e / jnp.sum(e, axis=-1, keepdims=True)


def _xattn_block_kernel(x_ref, g_ref, wq_ref, k_ref, v_ref, wo_ref, o_ref):
    x = x_ref[...]
    xq = jnp.dot(_rms(x, g_ref[...]).astype(BF16), wq_ref[...],
                 preferred_element_type=F32).astype(BF16)
    heads = []
    for h in range(XA_HEADS):
        sl = slice(h * XA_DH, (h + 1) * XA_DH)
        s = lax.dot_general(xq[:, sl], k_ref[:, sl].astype(BF16),
                            (((1,), (1,)), ((), ())), preferred_element_type=F32)
        p = _softmax_rows(s * (XA_DH ** -0.5))
        heads.append(_mm(p, v_ref[:, sl]).astype(BF16))
    o_ref[...] = x + jnp.dot(jnp.concatenate(heads, axis=1), wo_ref[...],
                             preferred_element_type=F32)


def _xattn_block(x, g, wq16, mk, mv, wo16, *, batch, seq, tq, name):
    nt = seq // tq
    assert seq % tq == 0 and wq16.dtype == BF16 and wo16.dtype == BF16
    d = D_MODEL
    resident = dict(pipeline_mode=pl.Buffered(1))
    return pl.pallas_call(
        _xattn_block_kernel,
        grid=(batch, nt),
        in_specs=[
            pl.BlockSpec((tq, d), lambda b, t: (b * nt + t, 0)),
            pl.BlockSpec((1, d), lambda b, t: (0, 0)),
            pl.BlockSpec((d, d), lambda b, t: (0, 0), **resident),
            pl.BlockSpec((N_MEM, d), lambda b, t: (b, 0)),
            pl.BlockSpec((N_MEM, d), lambda b, t: (b, 0)),
            pl.BlockSpec((d, d), lambda b, t: (0, 0), **resident),
        ],
        out_specs=pl.BlockSpec((tq, d), lambda b, t: (b * nt + t, 0)),
        out_shape=jax.ShapeDtypeStruct((batch * seq, d), F32),
        compiler_params=_params(2, 56),
        name=name,
    )(x, g.reshape(1, d), wq16, mk, mv, wo16)


def _xattn_own_mask():
    nrow = N_MEM * XA_HEADS
    col_head = lax.broadcasted_iota(jnp.int32, (V7X_SUBLANES, nrow), 1) & (XA_HEADS - 1)
    row_head = lax.broadcasted_iota(jnp.int32, (V7X_SUBLANES, nrow), 0) & (XA_HEADS - 1)
    return col_head == row_head


def _xattn_probs(q_ref, k_ref, j, own):
    kf = k_ref[j].reshape(N_MEM * XA_HEADS, XA_DH)
    q8 = jnp.concatenate([q_ref[j]] * (V7X_SUBLANES // XA_HEADS), axis=0)
    s = lax.dot_general(q8.astype(BF16), kf.astype(BF16), (((1,), (1,)), ((), ())),
                        preferred_element_type=F32)
    s = jnp.where(own, s * (XA_DH ** -0.5), NEG)
    e = jnp.where(own, jnp.exp(s - jnp.max(s, axis=-1, keepdims=True)), 0.0)
    return e / jnp.sum(e, axis=-1, keepdims=True)


def _xattn_values(v_ref, o_ref, j, p):
    vf = v_ref[j].reshape(N_MEM * XA_HEADS, XA_DH)
    o_ref[j] = _mm(p, vf)[0:XA_HEADS, :]


def _xattn_step_kernel(q_ref, k_ref, v_ref, o_ref, *, sb):
    own = _xattn_own_mask()
    for j in range(sb):
        _xattn_values(v_ref, o_ref, j, _xattn_probs(q_ref, k_ref, j, own))


def _xattn_step(xq, ck, cv, *, sb, name):
    nb = xq.shape[0]
    assert XA_HEADS & (XA_HEADS - 1) == 0
    assert nb % sb == 0 and ck.shape == (nb, N_MEM, XA_HEADS, XA_DH)
    cache_spec = pl.BlockSpec((sb, N_MEM, XA_HEADS, XA_DH), lambda i: (i, 0, 0, 0))
    q_spec = pl.BlockSpec((sb, XA_HEADS, XA_DH), lambda i: (i, 0, 0))
    return pl.pallas_call(
        functools.partial(_xattn_step_kernel, sb=sb),
        grid=(nb // sb,),
        in_specs=[q_spec, cache_spec, cache_spec],
        out_specs=q_spec,
        out_shape=jax.ShapeDtypeStruct((nb, XA_HEADS, XA_DH), F32),
        compiler_params=_params(1, 40),
        name=name,
    )(xq.reshape(nb, XA_HEADS, XA_DH), ck, cv).reshape(nb, XA_HEADS * XA_DH)


def _tiles(rows):
    tm = min(rows, 1024)
    assert rows % tm == 0
    return tm


def kernel(x_prompt, x_sample, mem_prompt, state_rg_h, state_rg_conv, state_ml_C, state_ml_n, state_ml_m, cache_mem_k, cache_mem_v, g_mix, w_in, conv_w, conv_b, w_rg_a, b_rg_a, w_rg_x, b_rg_x, rg_lambda, b_ml_i, b_ml_f, g_rg_out, g_ml_out, w_out, g_xa, g_mem, w_xa_q, w_xa_k, w_xa_v, w_xa_o, g_ffn, w_ffn_gate, w_ffn_up, w_ffn_down, g_final):
    depth = g_mix.shape[0]
    assert depth == 1, "single trunk layer"
    bp, seq, d = x_prompt.shape
    bs_, dec_seq, _ = x_sample.shape
    assert d == D_MODEL and dec_seq == 1
    n_mem = mem_prompt.shape[1]
    assert n_mem == N_MEM
    dff = w_ffn_gate.shape[-1]
    in_w = w_in.shape[-1]
    assert in_w == IN_MAIN + N_GATE

    w_in_t = jnp.swapaxes(w_in, 1, 2).reshape(in_w, d)
    w_gate_pad = jnp.pad(w_in_t[IN_MAIN:], ((0, V7X_LANES - N_GATE), (0, 0)))
    cw = conv_w.reshape(CONV_W, RG_WIDTH)
    wgate = jnp.concatenate([w_rg_a.reshape(RG_BLOCKS, RG_BLOCK, RG_BLOCK),
                             w_rg_x.reshape(RG_BLOCKS, RG_BLOCK, RG_BLOCK)], axis=-1)
    rg_args = (cw, conv_b.reshape(-1), wgate, b_rg_a.reshape(-1), b_rg_x.reshape(-1),
               rg_lambda.reshape(-1), g_rg_out.reshape(-1))
    b_i = b_ml_i.reshape(ML_HEADS)
    b_f = b_ml_f.reshape(ML_HEADS)
    g_ml = g_ml_out.reshape(-1)
    w_out2 = w_out.reshape(d, d)
    w_q = w_xa_q.reshape(d, d)
    w_k = w_xa_k.reshape(d, d)
    w_v = w_xa_v.reshape(d, d)
    w_o = w_xa_o.reshape(d, d)
    w_fg = w_ffn_gate.reshape(d, dff)
    w_fu = w_ffn_up.reshape(d, dff)
    w_fd = w_ffn_down.reshape(dff, d)

    row_tile = _tiles(bp * seq)
    rg_zero_conv = jnp.zeros((bp, CONV_W - 1, RG_WIDTH), F32)
    rg_zero_h = jnp.zeros((bp, RG_WIDTH), F32)

    xs = x_sample.reshape(bs_, d)
    z_s, zg_s, w_in16 = _norm_linear(
        xs, g_mix.reshape(-1), w_in_t, n_out=IN_MAIN, tm=bs_, tn=1024, w_gate=w_gate_pad,
        w_is_nk=True, emit_w16=True, name="in_proj_s")
    y_rg_s, s_h, s_conv = _rglru_step(
        z_s, state_rg_conv.reshape(bs_, CONV_W - 1, RG_WIDTH), state_rg_h.reshape(bs_, RG_WIDTH),
        *rg_args, name="rglru_s")
    mlstm_s_job = _mlstm_step_job(
        z_s, zg_s, state_ml_C.reshape(bs_, ML_HEADS, ML_DK, ML_DV),
        state_ml_n.reshape(bs_, ML_HEADS, ML_DK), state_ml_m.reshape(bs_, ML_HEADS),
        b_i, b_f, g_ml, bs=V7X_SUBLANES, parts=2)
    tp = bp * seq
    xp = x_prompt.reshape(tp, d)
    in_cols, in_rows = IN_MAIN // 1024, tp // row_tile
    if (mlstm_s_job.n_blocks % in_rows == 0
            and mlstm_s_job.n_blocks // in_rows * mlstm_s_job.parts < in_cols):
        z_p, zg_p, y_ml_s, s_c, s_n, s_m = _norm_linear(
            xp, g_mix.reshape(-1), w_in16, n_out=IN_MAIN, tm=row_tile, tn=1024, w_gate=w_gate_pad,
            w_is_nk=True, side=mlstm_s_job, vmem_mib=60, name="in_proj_p")
    else:
        z_p, zg_p = _norm_linear(xp, g_mix.reshape(-1), w_in16, n_out=IN_MAIN, tm=row_tile,
                                 tn=1024, w_gate=w_gate_pad, w_is_nk=True, name="in_proj_p")
        y_ml_s, s_c, s_n, s_m = _run_side_job(mlstm_s_job, vmem_mib=32, name="mlstm_s")
    x1_s, w_out16_rg, w_out16_ml = _linear_residual(
        [y_rg_s, y_ml_s], [(w_out2, 0), (w_out2, 1)], xs, tm=bs_, tn=1024, emit_w16=True,
        name="mix_out_s")
    xq_s, w_q16 = _norm_linear(x1_s, g_xa.reshape(-1), w_q, n_out=d, tm=bs_, tn=1024,
                               emit_w16=True, name="xa_q_s")
    ck = cache_mem_k.reshape(bs_, n_mem, XA_HEADS, XA_DH)
    cv = cache_mem_v.reshape(bs_, n_mem, XA_HEADS, XA_DH)

    rg_tl = min(seq, 256)
    rg_steps = bp * (seq // rg_tl)
    if bs_ % rg_steps == 0 and RG_BLOCKS % (bs_ // rg_steps) == 0:
        y_rg_p, p_h, p_conv, o_s = _rglru_seq(
            z_p, rg_zero_conv, rg_zero_h, *rg_args, batch=bp, seq=seq, tl=rg_tl,
            side_xattn=(xq_s, ck, cv), name="rglru_p")
        o_s = o_s.reshape(bs_, d)
    else:
        y_rg_p, p_h, p_conv = _rglru_seq(z_p, rg_zero_conv, rg_zero_h, *rg_args, batch=bp,
                                         seq=seq, tl=rg_tl, name="rglru_p")
        o_s = _xattn_step(xq_s, ck, cv, sb=2, name="xattn_s")

    x3_s, w_o16 = _linear_residual([o_s], [(w_o, 0)], x1_s, tm=bs_, tn=1024, emit_w16=True,
                                   name="xa_out_s")
    y_s, w_fg16, w_fu16, w_fd16 = _ffn(x3_s, g_ffn.reshape(-1), w_fg, w_fu, w_fd, g_final,
                                       tm=bs_, tf=512, emit_w16=True, vmem_mib=48, name="ffn_s")

    y_ml_p, p_c, p_n, p_m = _mlstm_seq(z_p, zg_p, b_i, b_f, g_ml, batch=bp, seq=seq,
                                       cs=min(seq, ML_CHUNK), name="mlstm_p")
    mem2 = mem_prompt.reshape(bp * n_mem, d)
    tmem = min(bp * n_mem, 256)
    mk, mk_heads = _norm_linear(mem2, g_mem.reshape(-1), w_k, n_out=d, tm=tmem, tn=d,
                                heads_out=(XA_HEADS, XA_DH), name="mem_k")
    mv, mv_heads = _norm_linear(mem2, g_mem.reshape(-1), w_v, n_out=d, tm=tmem, tn=d,
                                heads_out=(XA_HEADS, XA_DH), name="mem_v")
    proj_tile = min(row_tile, 512)
    x1_p = _linear_residual([y_rg_p, y_ml_p], [(w_out16_rg, 0), (w_out16_ml, 0)], xp,
                            tm=proj_tile, tn=d, name="mix_out_p")
    x3_p = _xattn_block(x1_p, g_xa.reshape(-1), w_q16, mk, mv, w_o16, batch=bp, seq=seq,
                        tq=min(seq, proj_tile), name="xattn_p")
    y_p = _ffn(x3_p, g_ffn.reshape(-1), w_fg16, w_fu16, w_fd16, g_final, tm=row_tile, tf=512,
               vmem_mib=60, name="ffn_p")

    return (
        y_p.reshape(bp, seq, d),
        y_s.reshape(bs_, 1, d),
        p_h.reshape(1, bp, RG_WIDTH),
        p_conv.reshape(1, bp, CONV_W - 1, RG_WIDTH),
        p_c.reshape(1, bp, ML_HEADS, ML_DK, ML_DV),
        p_n.reshape(1, bp, ML_HEADS, ML_DK),
        p_m[:, :, 0].reshape(1, bp, ML_HEADS),
        mk_heads.reshape(1, bp, n_mem, XA_HEADS, XA_DH),
        mv_heads.reshape(1, bp, n_mem, XA_HEADS, XA_DH),
        s_h.reshape(1, bs_, RG_WIDTH),
        s_conv.reshape(1, bs_, CONV_W - 1, RG_WIDTH),
        s_c.reshape(1, bs_, ML_HEADS, ML_DK, ML_DV),
        s_n.reshape(1, bs_, ML_HEADS, ML_DK),
        s_m.reshape(1, bs_, ML_HEADS),
    )
```

```python
import functools
from typing import Callable, NamedTuple

import jax
import jax.numpy as jnp
from jax import lax
from jax.experimental import pallas as pl
from jax.experimental.pallas import tpu as pltpu

F32 = jnp.float32
BF16 = jnp.bfloat16

D_MODEL = 2048
RG_WIDTH = D_MODEL // 2
RG_BLOCKS = 8
RG_BLOCK = RG_WIDTH // RG_BLOCKS
CONV_W = 4
RG_C = 8.0
ML_HEADS = 4
ML_WIDTH = D_MODEL - RG_WIDTH
ML_DV = ML_WIDTH // ML_HEADS
ML_DK = ML_DV // 2
N_MEM = 256
XA_HEADS = 4
XA_DH = D_MODEL // XA_HEADS
EPS = 1e-6
NEG = -1e30

OFF_RGX = 0
OFF_RGG = OFF_RGX + RG_WIDTH
OFF_Q = OFF_RGG + RG_WIDTH
OFF_K = OFF_Q + ML_HEADS * ML_DK
OFF_V = OFF_K + ML_HEADS * ML_DK
OFF_O = OFF_V + ML_WIDTH
OFF_I = OFF_O + ML_WIDTH
IN_MAIN = OFF_I
N_GATE = 2 * ML_HEADS

V7X_LANES = 128
V7X_SUBLANES = 8
V7X_VMEM_BYTES = 64 * 2**20

ML_CHUNK = 256


class SideJob(NamedTuple):
    n_blocks: int
    parts: int
    args: tuple
    in_specs: Callable
    out_shape: tuple
    out_specs: Callable
    scratch_shapes: tuple
    body: Callable


def _params(n_axes, vmem_mib):
    assert vmem_mib * 2**20 <= V7X_VMEM_BYTES
    return pltpu.CompilerParams(
        dimension_semantics=("arbitrary",) * n_axes,
        vmem_limit_bytes=vmem_mib * 2**20,
    )


def _rms(x, g):
    ms = jnp.mean(x * x, axis=-1, keepdims=True)
    return x * lax.rsqrt(ms + EPS) * g


def _softplus(u):
    return jnp.maximum(u, 0.0) + jnp.log1p(jnp.exp(-jnp.abs(u)))


def _log_sigmoid(u):
    return -_softplus(-u)


def _sigmoid(u):
    return 0.5 * jnp.tanh(0.5 * u) + 0.5


def _gelu_tanh(x):
    c = 0.7978845608028654
    half_x = 0.5 * x
    return half_x + half_x * jnp.tanh(x * (c + (c * 0.044715) * (x * x)))


def _sqrt_nonneg(v):
    return jnp.where(v > 0.0, v * lax.rsqrt(v), 0.0)


def _mm(a, b):
    return jnp.dot(a.astype(BF16), b.astype(BF16), preferred_element_type=F32)


def _exact_cumsum_lanes(x):
    n = x.shape[-1]
    upper = jnp.where(lax.broadcasted_iota(jnp.int32, (n, n), 0)
                      <= lax.broadcasted_iota(jnp.int32, (n, n), 1), 1.0, 0.0).astype(BF16)
    hi = x.astype(BF16)
    rest = x - hi.astype(F32)
    mid = rest.astype(BF16)
    lo = (rest - mid.astype(F32)).astype(BF16)
    return (jnp.dot(hi, upper, preferred_element_type=F32)
            + jnp.dot(mid, upper, preferred_element_type=F32)
            + jnp.dot(lo, upper, preferred_element_type=F32))


def _dot_w(a, w_ref, w_is_nk):
    w = w_ref[...].astype(BF16)
    if w_is_nk:
        return lax.dot_general(a, w, (((1,), (1,)), ((), ())), preferred_element_type=F32)
    return jnp.dot(a, w, preferred_element_type=F32)


def _norm_linear_kernel(*refs, with_gate, w_is_nk, emit_w16, heads_out, side, side_cols):
    refs = list(refs)
    n_si, n_so, n_ss = ((len(side.args), len(side.out_shape), len(side.scratch_shapes))
                        if side else (0, 0, 0))
    n_in = 3 + with_gate + n_si
    n_out = 1 + bool(heads_out) + with_gate + emit_w16 + n_so
    x_ref, g_ref, w_ref = refs[:3]
    wg_ref = refs[3] if with_gate else None
    side_in = refs[3 + with_gate:n_in]
    outs = refs[n_in:n_in + n_out]
    xn_ref = refs[n_in + n_out]
    side_scratch = refs[n_in + n_out + 1:]
    o_ref = outs[0]
    oh_ref = outs[1] if heads_out else None
    og_ref = outs[1 + bool(heads_out)] if with_gate else None
    w16_ref = outs[n_out - n_so - 1] if emit_w16 else None
    side_out = outs[n_out - n_so:]
    col = pl.program_id(1)

    def column_tile(xn):
        acc = _dot_w(xn, w_ref, w_is_nk)
        o_ref[...] = acc.astype(o_ref.dtype)
        if heads_out:
            oh_ref[...] = acc.reshape(oh_ref.shape)
        if emit_w16:
            w16_ref[...] = w_ref[...].astype(BF16)

    @pl.when(col == 0)
    def _():
        xn = _rms(x_ref[...], g_ref[...]).astype(BF16)
        xn_ref[...] = xn
        if with_gate:
            og_ref[...] = _dot_w(xn, wg_ref, w_is_nk)
        column_tile(xn)

    if side:
        for part in range(side.parts):
            @pl.when((col > 0) & (col <= side_cols) & ((col - 1) % side.parts == part))
            def _(part=part):
                column_tile(xn_ref[...])
                side.body(side_in, side_out, side_scratch, part)

    @pl.when(col > (side_cols if side else 0))
    def _():
        column_tile(xn_ref[...])


def _norm_linear(x, g, w, *, n_out, tm, tn, out_dtype=F32, w_gate=None, w_is_nk=False,
                 emit_w16=False, heads_out=None, side=None, vmem_mib=48, name):
    m, k = x.shape
    k_ax, n_ax = (1, 0) if w_is_nk else (0, 1)
    assert m % tm == 0 and n_out % tn == 0 and w.shape[k_ax] == k and n_out <= w.shape[n_ax]
    assert not emit_w16 or m == tm
    assert heads_out is None or (tn == n_out and heads_out[0] * heads_out[1] == n_out)
    n_rows, n_cols = m // tm, n_out // tn
    with_gate = w_gate is not None
    w_mode = dict(pipeline_mode=pl.Buffered(1)) if tn == n_out else {}
    w_spec = (pl.BlockSpec((tn, k), lambda i, j: (j, 0), **w_mode) if w_is_nk
              else pl.BlockSpec((k, tn), lambda i, j: (0, j), **w_mode))
    in_specs = [
        pl.BlockSpec((tm, k), lambda i, j: (i, 0)),
        pl.BlockSpec((1, k), lambda i, j: (0, 0)),
        w_spec,
    ]
    out_shape = [jax.ShapeDtypeStruct((m, n_out), out_dtype)]
    out_specs = [pl.BlockSpec((tm, tn), lambda i, j: (i, j))]
    scratch_shapes = [pltpu.VMEM((tm, k), BF16)]
    args = [x, g.reshape(1, k), w]
    if heads_out:
        out_shape.append(jax.ShapeDtypeStruct((m, *heads_out), F32))
        out_specs.append(pl.BlockSpec((tm, *heads_out), lambda i, j: (i, 0, 0)))
    if with_gate:
        ng = w_gate.shape[n_ax]
        in_specs.append(pl.BlockSpec(w_gate.shape, lambda i, j: (0, 0)))
        out_shape.append(jax.ShapeDtypeStruct((m, ng), F32))
        out_specs.append(pl.BlockSpec((tm, ng), lambda i, j: (i, 0)))
        args.append(w_gate)
    if emit_w16:
        out_shape.append(jax.ShapeDtypeStruct((n_out, k) if w_is_nk else (k, n_out), BF16))
        out_specs.append(pl.BlockSpec((tn, k), lambda i, j: (j, 0)) if w_is_nk
                         else pl.BlockSpec((k, tn), lambda i, j: (0, j)))
    side_cols = 0
    if side:
        assert side.n_blocks % n_rows == 0
        blocks_per_row = side.n_blocks // n_rows
        side_cols = blocks_per_row * side.parts
        assert 0 < side_cols < n_cols
        block_of = lambda i, j: (i * blocks_per_row
                                 + jnp.clip(j - 1, 0, side_cols - 1) // side.parts)
        in_specs += side.in_specs(block_of)
        out_specs += side.out_specs(block_of)
        out_shape += list(side.out_shape)
        scratch_shapes += list(side.scratch_shapes)
        args += list(side.args)
    out = pl.pallas_call(
        functools.partial(_norm_linear_kernel, with_gate=with_gate, w_is_nk=w_is_nk,
                          emit_w16=emit_w16, heads_out=heads_out, side=side, side_cols=side_cols),
        grid=(n_rows, n_cols),
        in_specs=in_specs,
        out_specs=out_specs,
        out_shape=out_shape,
        scratch_shapes=scratch_shapes,
        compiler_params=_params(2, vmem_mib),
        name=name,
    )(*args)
    return out if len(out) > 1 else out[0]


def _linear_res_kernel(*refs, n_in, emit_w16):
    a_refs = refs[:n_in]
    w_refs = refs[n_in:2 * n_in]
    res_ref = refs[2 * n_in]
    o_ref = refs[2 * n_in + 1]
    acc = res_ref[...]
    for a_ref, w_ref in zip(a_refs, w_refs):
        acc = acc + _mm(a_ref[...], w_ref[...])
    o_ref[...] = acc
    if emit_w16:
        for w_ref, wc_ref in zip(w_refs, refs[2 * n_in + 2:]):
            wc_ref[...] = w_ref[...].astype(BF16)


def _linear_residual(parts, weights, res, *, tm, tn, emit_w16=False, vmem_mib=48, name):
    m, n = res.shape
    kp = parts[0].shape[1]
    assert all(p.shape == (m, kp) for p in parts) and len(weights) == len(parts)
    assert m % tm == 0 and n % tn == 0 and (not emit_w16 or m == tm)
    n_in = len(parts)
    in_specs = [pl.BlockSpec((tm, kp), lambda i, j: (i, 0)) for _ in parts]
    w_mode = dict(pipeline_mode=pl.Buffered(1)) if tn == n else {}
    in_specs += [pl.BlockSpec((kp, tn), lambda i, j, rb=rb: (rb, j), **w_mode)
                 for _, rb in weights]
    in_specs.append(pl.BlockSpec((tm, tn), lambda i, j: (i, j)))
    out_specs = [pl.BlockSpec((tm, tn), lambda i, j: (i, j))]
    out_shape = [jax.ShapeDtypeStruct((m, n), F32)]
    if emit_w16:
        out_specs += [pl.BlockSpec((kp, tn), lambda i, j: (0, j)) for _ in parts]
        out_shape += [jax.ShapeDtypeStruct((kp, n), BF16) for _ in parts]
    out = pl.pallas_call(
        functools.partial(_linear_res_kernel, n_in=n_in, emit_w16=emit_w16),
        grid=(m // tm, n // tn),
        in_specs=in_specs,
        out_specs=out_specs,
        out_shape=out_shape,
        compiler_params=_params(2, vmem_mib),
        name=name,
    )(*parts, *[w for w, _ in weights], res)
    return out if emit_w16 else out[0]


def _ffn_kernel(x_ref, g_ref, wg_ref, wu_ref, wd_ref, gf_ref, o_ref, *rest, emit_w16):
    xf_ref = rest[-1]
    f = pl.program_id(1)
    last = pl.num_programs(1) - 1

    def hidden_tile(xf):
        wg = wg_ref[...].astype(BF16)
        wu = wu_ref[...].astype(BF16)
        wd = wd_ref[...].astype(BF16)
        if emit_w16:
            for dst, val in zip(rest[:3], (wg, wu, wd)):
                dst[...] = val
        gate = jnp.dot(xf, wg, preferred_element_type=F32)
        up = jnp.dot(xf, wu, preferred_element_type=F32)
        hidden = (gate * _sigmoid(gate)) * up
        return jnp.dot(hidden.astype(BF16), wd, preferred_element_type=F32)

    @pl.when(f == 0)
    def _():
        x = x_ref[...]
        xf = _rms(x, g_ref[...]).astype(BF16)
        xf_ref[...] = xf
        o_ref[...] = x + hidden_tile(xf)

    @pl.when((f > 0) & (f < last))
    def _():
        o_ref[...] += hidden_tile(xf_ref[...])

    @pl.when(f == last)
    def _():
        o_ref[...] = _rms(o_ref[...] + hidden_tile(xf_ref[...]), gf_ref[...])


def _ffn(x, g, w_gate, w_up, w_down, g_final, *, tm, tf, emit_w16=False, vmem_mib, name):
    m, d = x.shape
    dff = w_gate.shape[1]
    assert m % tm == 0 and dff % tf == 0 and dff // tf >= 2 and (not emit_w16 or m == tm)
    up_spec = pl.BlockSpec((d, tf), lambda i, f: (0, f))
    down_spec = pl.BlockSpec((tf, d), lambda i, f: (f, 0))
    out_specs = [pl.BlockSpec((tm, d), lambda i, f: (i, 0))]
    out_shape = [jax.ShapeDtypeStruct((m, d), F32)]
    if emit_w16:
        out_specs += [up_spec, up_spec, down_spec]
        out_shape += [jax.ShapeDtypeStruct(w.shape, BF16) for w in (w_gate, w_up, w_down)]
    out = pl.pallas_call(
        functools.partial(_ffn_kernel, emit_w16=emit_w16),
        grid=(m // tm, dff // tf),
        in_specs=[
            pl.BlockSpec((tm, d), lambda i, f: (i, 0)),
            pl.BlockSpec((1, d), lambda i, f: (0, 0)),
            up_spec,
            up_spec,
            down_spec,
            pl.BlockSpec((1, d), lambda i, f: (0, 0)),
        ],
        out_specs=out_specs,
        out_shape=out_shape,
        scratch_shapes=[pltpu.VMEM((tm, d), BF16)],
        compiler_params=_params(2, vmem_mib),
        name=name,
    )(x, g.reshape(1, d), w_gate, w_up, w_down, g_final.reshape(1, d))
    return out if emit_w16 else out[0]


RG_GATE_LOOKAHEAD = 2


def _rg_gates(xr, wgate_ref, ba_ref, bx_ref, lam_ref, a_ref, b_ref, side=None):
    blocks = [slice(n * RG_BLOCK, (n + 1) * RG_BLOCK) for n in range(RG_BLOCKS)]
    gates = {}

    def issue_gate(n):
        if n < RG_BLOCKS:
            gates[n] = _mm(xr[:, blocks[n]], wgate_ref[n])

    rows, score_phase, value_phase = side if side else (0, None, None)
    per = RG_BLOCKS // rows if rows else RG_BLOCKS
    scores = {}
    for n in range(RG_GATE_LOOKAHEAD):
        issue_gate(n)
    for n, sl in enumerate(blocks):
        if rows and n % per == 0:
            j = n // per
            scores[j] = score_phase(j)
            if j > 0:
                value_phase(j - 1, scores.pop(j - 1))
        xn = xr[:, sl]
        g = gates.pop(n)
        issue_gate(n + RG_GATE_LOOKAHEAD)
        r = _sigmoid(g[:, :RG_BLOCK] + ba_ref[:, sl])
        ig = _sigmoid(g[:, RG_BLOCK:] + bx_ref[:, sl])
        a = jnp.exp(r * (-RG_C * _softplus(-lam_ref[:, sl])))
        a_ref[:, sl] = a
        mult = _sqrt_nonneg(jnp.maximum(1.0 - a * a, 0.0))
        b_ref[:, sl] = mult * (ig * xn)
    if rows:
        value_phase(rows - 1, scores.pop(rows - 1))


def _rglru_seq_kernel(*refs, tl, side_rows):
    (zx_ref, zg_ref, conv0_ref, h0_ref, cw_ref, cb_ref, wgate_ref, ba_ref, bx_ref, lam_ref,
     gout_ref) = refs[:11]
    xe_ref, a_ref, b_ref, h_ref, hc_ref = refs[-5:]
    if side_rows:
        sq_ref, sk_ref, sv_ref, y_ref, hlast_ref, convn_ref, so_ref = refs[11:-5]
    else:
        y_ref, hlast_ref, convn_ref = refs[11:-5]
    t = pl.program_id(1)
    pad = V7X_SUBLANES

    @pl.when(t == 0)
    def _():
        xe_ref[pad - 3:pad, :] = conv0_ref[...]
        hc_ref[...] = h0_ref[...]

    @pl.when(t > 0)
    def _():
        xe_ref[pad - 3:pad, :] = xe_ref[tl + pad - 3:tl + pad, :]

    x = zx_ref[...]
    xe_ref[pad:tl + pad, :] = x
    xr = (xe_ref[pad - 3:tl + pad - 3, :] * cw_ref[0:1, :]
          + xe_ref[pad - 2:tl + pad - 2, :] * cw_ref[1:2, :]
          + xe_ref[pad - 1:tl + pad - 1, :] * cw_ref[2:3, :]
          + x * cw_ref[3:4, :]) + cb_ref[...]
    side = None
    if side_rows:
        own = _xattn_own_mask()
        side = (side_rows,
                lambda j: _xattn_probs(sq_ref, sk_ref, j, own),
                lambda j, p: _xattn_values(sv_ref, so_ref, j, p))

    _rg_gates(xr, wgate_ref, ba_ref, bx_ref, lam_ref, a_ref, b_ref, side=side)

    row = lax.broadcasted_iota(jnp.int32, (V7X_SUBLANES, RG_WIDTH), 0)

    def group(gi, hc):
        r0 = pl.multiple_of(gi * V7X_SUBLANES, V7X_SUBLANES)
        a8 = a_ref[pl.ds(r0, V7X_SUBLANES), :]
        b8 = b_ref[pl.ds(r0, V7X_SUBLANES), :]
        for d in (1, 2, 4):
            keep = row >= d
            b8 = jnp.where(keep, a8 * pltpu.roll(b8, d, axis=0) + b8, b8)
            a8 = jnp.where(keep, a8 * pltpu.roll(a8, d, axis=0), a8)
        h8 = a8 * hc + b8
        h_ref[pl.ds(r0, V7X_SUBLANES), :] = h8
        return h8[V7X_SUBLANES - 1:V7X_SUBLANES, :]

    hc = lax.fori_loop(0, tl // V7X_SUBLANES, group, hc_ref[...], unroll=4)
    hc_ref[...] = hc
    hlast_ref[...] = hc
    convn_ref[...] = xe_ref[tl + pad - 3:tl + pad, :]
    y = h_ref[...] * _gelu_tanh(zg_ref[...])
    y_ref[...] = _rms(y, gout_ref[...]).astype(y_ref.dtype)


def _rglru_seq(z, conv0, h0, cw, cb, wgate, ba, bx, lam, gout, *, batch, seq, tl, name,
               side_xattn=None):
    nt = seq // tl
    assert seq % tl == 0
    w = RG_WIDTH
    row = lambda v: v.reshape(1, w)
    const2 = lambda b, t: (0, 0)
    in_specs = [
        pl.BlockSpec((tl, w), lambda b, t: (b * nt + t, OFF_RGX // w)),
        pl.BlockSpec((tl, w), lambda b, t: (b * nt + t, OFF_RGG // w)),
        pl.BlockSpec((None, CONV_W - 1, w), lambda b, t: (b, 0, 0)),
        pl.BlockSpec((None, 1, w), lambda b, t: (b, 0, 0)),
        pl.BlockSpec((CONV_W, w), const2),
        pl.BlockSpec((1, w), const2),
        pl.BlockSpec((RG_BLOCKS, RG_BLOCK, 2 * RG_BLOCK), lambda b, t: (0, 0, 0)),
        pl.BlockSpec((1, w), const2),
        pl.BlockSpec((1, w), const2),
        pl.BlockSpec((1, w), const2),
        pl.BlockSpec((1, w), const2),
    ]
    out_specs = [
        pl.BlockSpec((tl, w), lambda b, t: (b * nt + t, 0)),
        pl.BlockSpec((None, 1, w), lambda b, t: (b, 0, 0)),
        pl.BlockSpec((None, CONV_W - 1, w), lambda b, t: (b, 0, 0)),
    ]
    out_shape = [
        jax.ShapeDtypeStruct((batch * seq, w), BF16),
        jax.ShapeDtypeStruct((batch, 1, w), F32),
        jax.ShapeDtypeStruct((batch, CONV_W - 1, w), F32),
    ]
    args = [z, z, conv0, h0.reshape(batch, 1, w), cw, row(cb), wgate, row(ba), row(bx), row(lam),
            row(gout)]
    side_rows = 0
    vmem_mib = 32
    if side_xattn is not None:
        xq, ck, cv = side_xattn
        nb = xq.shape[0]
        assert nb % (batch * nt) == 0 and ck.shape == (nb, N_MEM, XA_HEADS, XA_DH)
        side_rows = nb // (batch * nt)
        q_spec = pl.BlockSpec((side_rows, XA_HEADS, XA_DH), lambda b, t: (b * nt + t, 0, 0))
        cache_spec = pl.BlockSpec((side_rows, N_MEM, XA_HEADS, XA_DH),
                                  lambda b, t: (b * nt + t, 0, 0, 0))
        in_specs += [q_spec, cache_spec, cache_spec]
        out_specs.append(q_spec)
        out_shape.append(jax.ShapeDtypeStruct((nb, XA_HEADS, XA_DH), F32))
        args += [xq.reshape(nb, XA_HEADS, XA_DH), ck, cv]
        vmem_mib = 56
    return pl.pallas_call(
        functools.partial(_rglru_seq_kernel, tl=tl, side_rows=side_rows),
        grid=(batch, nt),
        in_specs=in_specs,
        out_specs=out_specs,
        out_shape=out_shape,
        scratch_shapes=[
            pltpu.VMEM((tl + V7X_SUBLANES, w), F32),
            pltpu.VMEM((tl, w), F32),
            pltpu.VMEM((tl, w), F32),
            pltpu.VMEM((tl, w), F32),
            pltpu.VMEM((1, w), F32),
        ],
        compiler_params=_params(2, vmem_mib),
        name=name,
    )(*args)


def _rglru_step_kernel(zx_ref, zg_ref, conv_ref, h0_ref, cw_ref, cb_ref, wgate_ref, ba_ref,
                       bx_ref, lam_ref, gout_ref, y_ref, hn_ref, convn_ref, a_ref, b_ref):
    w = RG_WIDTH
    x = zx_ref[...]
    xr = (conv_ref[:, 0:w] * cw_ref[0:1, :] + conv_ref[:, w:2 * w] * cw_ref[1:2, :]
          + conv_ref[:, 2 * w:3 * w] * cw_ref[2:3, :] + x * cw_ref[3:4, :]) + cb_ref[...]
    _rg_gates(xr, wgate_ref, ba_ref, bx_ref, lam_ref, a_ref, b_ref)
    h = a_ref[...] * h0_ref[...] + b_ref[...]
    hn_ref[...] = h
    convn_ref[:, 0:2 * w] = conv_ref[:, w:3 * w]
    convn_ref[:, 2 * w:3 * w] = x
    y_ref[...] = _rms(h * _gelu_tanh(zg_ref[...]), gout_ref[...]).astype(y_ref.dtype)


def _rglru_step(z, conv, h0, cw, cb, wgate, ba, bx, lam, gout, *, name):
    nb = z.shape[0]
    w = RG_WIDTH
    row = lambda v: v.reshape(1, w)
    c0 = lambda i: (0, 0)
    return pl.pallas_call(
        _rglru_step_kernel,
        grid=(1,),
        in_specs=[
            pl.BlockSpec((nb, w), lambda i: (0, OFF_RGX // w)),
            pl.BlockSpec((nb, w), lambda i: (0, OFF_RGG // w)),
            pl.BlockSpec((nb, (CONV_W - 1) * w), c0),
            pl.BlockSpec((nb, w), c0),
            pl.BlockSpec((CONV_W, w), c0),
            pl.BlockSpec((1, w), c0),
            pl.BlockSpec((RG_BLOCKS, RG_BLOCK, 2 * RG_BLOCK), lambda i: (0, 0, 0)),
            pl.BlockSpec((1, w), c0),
            pl.BlockSpec((1, w), c0),
            pl.BlockSpec((1, w), c0),
            pl.BlockSpec((1, w), c0),
        ],
        out_specs=[
            pl.BlockSpec((nb, w), c0),
            pl.BlockSpec((nb, w), c0),
            pl.BlockSpec((nb, (CONV_W - 1) * w), c0),
        ],
        out_shape=[
            jax.ShapeDtypeStruct((nb, w), BF16),
            jax.ShapeDtypeStruct((nb, w), F32),
            jax.ShapeDtypeStruct((nb, (CONV_W - 1) * w), F32),
        ],
        scratch_shapes=[pltpu.VMEM((nb, w), F32), pltpu.VMEM((nb, w), F32)],
        compiler_params=_params(1, 32),
        name=name,
    )(z, z, conv.reshape(nb, (CONV_W - 1) * w), h0, cw, row(cb), wgate, row(ba), row(bx),
      row(lam), row(gout))


def _mlstm_seq_kernel(bi_ref, bf_ref, q_ref, k_ref, v_ref, o_ref, zg_ref, g_ref,
                      y_ref, c_ref, n_ref, m_ref, zgt_ref, lf_ref, bcum_ref, cs_ref, ns_ref,
                      *, seq, cs):
    h = pl.program_id(1)
    nc = seq // cs
    bi = bi_ref[h]
    bf = bf_ref[h]
    for c in range(nc):
        zgt_ref[c] = zg_ref[c * cs:(c + 1) * cs, :].T
        lf_ref[c:c + 1, :] = zgt_ref[c, pl.ds(h + ML_HEADS, 1), :]
    lf = _log_sigmoid(lf_ref[...] + bf)
    lf_ref[...] = lf
    bcum_ref[...] = _exact_cumsum_lanes(lf)
    cs_ref[...] = jnp.zeros_like(cs_ref)
    ns_ref[...] = jnp.zeros_like(ns_ref)

    t_idx = lax.broadcasted_iota(jnp.int32, (cs, cs), 0)
    s_idx = lax.broadcasted_iota(jnp.int32, (cs, cs), 1)
    causal = s_idx <= t_idx
    lane = lax.broadcasted_iota(jnp.int32, (cs, V7X_LANES), 1)

    def chunk(c, m):
        r0 = pl.multiple_of(c * cs, cs)
        q = q_ref[pl.ds(r0, cs), :]
        k = k_ref[pl.ds(r0, cs), :] * (ML_DK ** -0.5)
        v = v_ref[pl.ds(r0, cs), :]
        zg = zg_ref[pl.ds(r0, cs), :]
        li_col = jnp.sum(jnp.where(lane == h, zg, 0.0), axis=1, keepdims=True) + bi
        li_row = zgt_ref[c, pl.ds(h, 1), :] + bi
        lf_row = lf_ref[pl.ds(c, 1), :]
        bcum_row = bcum_ref[pl.ds(c, 1), :]
        bcum_col = jnp.sum(jnp.where(causal, lf_row, 0.0), axis=1, keepdims=True)
        log_d = jnp.where(causal, bcum_col - bcum_row + li_row, NEG)
        inter = bcum_col + m
        m_t = jnp.maximum(inter, jnp.max(log_d, axis=1, keepdims=True))
        dmat = jnp.exp(log_d - m_t)
        sc = jnp.exp(inter - m_t)
        qb = q.astype(BF16)
        kb = k.astype(BF16)
        vb = v.astype(BF16)
        qk = lax.dot_general(qb, kb, (((1,), (1,)), ((), ())), preferred_element_type=F32) * dmat
        c_old = cs_ref[...]
        n_old = ns_ref[...]
        num = sc * jnp.dot(qb, c_old.astype(BF16), preferred_element_type=F32) + _mm(qk, vb)
        den = sc * jnp.sum(q * n_old, axis=1, keepdims=True) + jnp.sum(qk, axis=1, keepdims=True)
        den = jnp.maximum(jnp.abs(den), jnp.exp(-m_t))
        hh = num / den
        m_new = m_t[cs - 1:cs, :]
        b_last = bcum_col[cs - 1:cs, :]
        w_end = jnp.exp(b_last - bcum_col + li_col - m_new)
        dec = jnp.exp(b_last + m - m_new)
        wk = w_end * k
        cs_ref[...] = dec * c_old + lax.dot_general(
            wk.astype(BF16), vb, (((0,), (0,)), ((), ())), preferred_element_type=F32)
        ns_ref[...] = dec * n_old + jnp.sum(wk, axis=0, keepdims=True)
        y = _rms(hh, g_ref[...]) * _sigmoid(o_ref[pl.ds(r0, cs), :])
        y_ref[pl.ds(r0, cs), :] = y.astype(y_ref.dtype)
        return m_new

    m_fin = lax.fori_loop(0, nc, chunk, jnp.zeros((1, 1), F32))
    c_ref[...] = cs_ref[...]
    n_ref[pl.ds(h, 1), :] = ns_ref[...]
    m_ref[pl.ds(h, 1), :] = jnp.broadcast_to(m_fin, (1, V7X_LANES))


def _mlstm_seq(z, zg, b_i, b_f, g_out, *, batch, seq, cs, name):
    assert seq % cs == 0 and zg.shape[1] == V7X_LANES
    dk, dv, nh = ML_DK, ML_DV, ML_HEADS
    smem = pl.BlockSpec(memory_space=pltpu.SMEM)
    return pl.pallas_call(
        functools.partial(_mlstm_seq_kernel, seq=seq, cs=cs),
        grid=(batch, nh),
        in_specs=[
            smem, smem,
            pl.BlockSpec((seq, dk), lambda b, h: (b, OFF_Q // dk + h)),
            pl.BlockSpec((seq, dk), lambda b, h: (b, OFF_K // dk + h)),
            pl.BlockSpec((seq, dv), lambda b, h: (b, OFF_V // dv + h)),
            pl.BlockSpec((seq, dv), lambda b, h: (b, OFF_O // dv + h)),
            pl.BlockSpec((seq, V7X_LANES), lambda b, h: (b, 0)),
            pl.BlockSpec((1, dv), lambda b, h: (0, h)),
        ],
        out_specs=[
            pl.BlockSpec((seq, dv), lambda b, h: (b, h)),
            pl.BlockSpec((None, None, dk, dv), lambda b, h: (b, h, 0, 0)),
            pl.BlockSpec((None, nh, dk), lambda b, h: (b, 0, 0)),
            pl.BlockSpec((None, nh, V7X_LANES), lambda b, h: (b, 0, 0)),
        ],
        out_shape=[
            jax.ShapeDtypeStruct((batch * seq, nh * dv), BF16),
            jax.ShapeDtypeStruct((batch, nh, dk, dv), F32),
            jax.ShapeDtypeStruct((batch, nh, dk), F32),
            jax.ShapeDtypeStruct((batch, nh, V7X_LANES), F32),
        ],
        scratch_shapes=[
            pltpu.VMEM((seq // cs, V7X_LANES, cs), F32),
            pltpu.VMEM((seq // cs, cs), F32),
            pltpu.VMEM((seq // cs, cs), F32),
            pltpu.VMEM((dk, dv), F32),
            pltpu.VMEM((1, dk), F32),
        ],
        compiler_params=_params(2, 40),
        name=name,
    )(b_i, b_f, z, z, z, z, zg, g_out.reshape(1, nh * dv))


def _mlstm_step_body(in_refs, out_refs, scratch_refs, part, *, bs, parts):
    bi_ref, bf_ref, q_ref, k_ref, v_ref, o_ref, zg_ref, g_ref, c0_ref, n0_ref, m0_ref = in_refs
    y_ref, c_ref, n_ref, m_ref = out_refs
    qc_ref, = scratch_refs
    dk, dv = ML_DK, ML_DV
    nr = bs // parts
    rows = slice(part * nr, (part + 1) * nr)
    eye = (lax.broadcasted_iota(jnp.int32, (dk, dk), 0)
           == lax.broadcasted_iota(jnp.int32, (dk, dk), 1))

    def as_column(row):
        return jnp.sum(jnp.where(eye, jnp.broadcast_to(row, (dk, dk)), 0.0), axis=1, keepdims=True)

    zg = zg_ref[rows, :]
    for h in range(ML_HEADS):
        li = zg[:, h:h + 1] + bi_ref[h]
        lf = _log_sigmoid(zg[:, ML_HEADS + h:ML_HEADS + h + 1] + bf_ref[h])
        m = m0_ref[rows, h:h + 1]
        inter = lf + m
        m_t = jnp.maximum(inter, li)
        dgate = jnp.exp(li - m_t)
        sc = jnp.exp(inter - m_t)
        q = q_ref[rows, h * dk:(h + 1) * dk]
        k = k_ref[rows, h * dk:(h + 1) * dk] * (ML_DK ** -0.5)
        v = v_ref[rows, h * dv:(h + 1) * dv]
        n_old = n0_ref[rows, h, :]
        qk = jnp.sum(q * k, axis=1, keepdims=True) * dgate
        w_end = jnp.exp(li - m_t)
        dec = jnp.exp(inter - m_t)
        wk = w_end * k
        for j in range(nr):
            c_old = c0_ref[part * nr + j, h]
            qc_ref[j:j + 1, :] = jnp.sum(as_column(q[j:j + 1, :]) * c_old, axis=0, keepdims=True)
            c_ref[part * nr + j, h] = (dec[j:j + 1, :] * c_old
                                       + as_column(wk[j:j + 1, :]) * v[j:j + 1, :])
        num = sc * qc_ref[0:nr, :] + qk * v
        den = sc * jnp.sum(q * n_old, axis=1, keepdims=True) + qk
        den = jnp.maximum(jnp.abs(den), jnp.exp(-m_t))
        hh = num / den
        n_ref[rows, h, :] = dec * n_old + wk
        m_ref[rows, h:h + 1] = m_t
        y = (_rms(hh, g_ref[:, h * dv:(h + 1) * dv])
             * _sigmoid(o_ref[rows, h * dv:(h + 1) * dv]))
        y_ref[rows, h * dv:(h + 1) * dv] = y.astype(y_ref.dtype)


def _mlstm_step_job(z, zg, c0, n0, m0, b_i, b_f, g_out, *, bs, parts):
    nb = z.shape[0]
    assert nb % bs == 0 and bs % parts == 0
    dk, dv, nh = ML_DK, ML_DV, ML_HEADS
    smem = pl.BlockSpec(memory_space=pltpu.SMEM)

    def in_specs(blk):
        return [
            smem, smem,
            pl.BlockSpec((bs, nh * dk), lambda *g: (blk(*g), OFF_Q // (nh * dk))),
            pl.BlockSpec((bs, nh * dk), lambda *g: (blk(*g), OFF_K // (nh * dk))),
            pl.BlockSpec((bs, nh * dv), lambda *g: (blk(*g), OFF_V // (nh * dv))),
            pl.BlockSpec((bs, nh * dv), lambda *g: (blk(*g), OFF_O // (nh * dv))),
            pl.BlockSpec((bs, V7X_LANES), lambda *g: (blk(*g), 0)),
            pl.BlockSpec((1, nh * dv), lambda *g: (0, 0)),
            pl.BlockSpec((bs, nh, dk, dv), lambda *g: (blk(*g), 0, 0, 0)),
            pl.BlockSpec((bs, nh, dk), lambda *g: (blk(*g), 0, 0)),
            pl.BlockSpec((bs, nh), lambda *g: (blk(*g), 0)),
        ]

    def out_specs(blk):
        return [
            pl.BlockSpec((bs, nh * dv), lambda *g: (blk(*g), 0)),
            pl.BlockSpec((bs, nh, dk, dv), lambda *g: (blk(*g), 0, 0, 0)),
            pl.BlockSpec((bs, nh, dk), lambda *g: (blk(*g), 0, 0)),
            pl.BlockSpec((bs, nh), lambda *g: (blk(*g), 0)),
        ]

    return SideJob(
        n_blocks=nb // bs,
        parts=parts,
        args=(b_i, b_f, z, z, z, z, zg, g_out.reshape(1, nh * dv), c0, n0, m0),
        in_specs=in_specs,
        out_shape=(
            jax.ShapeDtypeStruct((nb, nh * dv), F32),
            jax.ShapeDtypeStruct((nb, nh, dk, dv), F32),
            jax.ShapeDtypeStruct((nb, nh, dk), F32),
            jax.ShapeDtypeStruct((nb, nh), F32),
        ),
        out_specs=out_specs,
        scratch_shapes=(pltpu.VMEM((bs, dv), F32),),
        body=functools.partial(_mlstm_step_body, bs=bs, parts=parts),
    )


def _run_side_job(job, *, vmem_mib, name):
    n_in, n_out = len(job.args), len(job.out_shape)

    def body(*refs):
        for part in range(job.parts):
            job.body(refs[:n_in], refs[n_in:n_in + n_out], refs[n_in + n_out:], part)

    block_of = lambda i: i
    return pl.pallas_call(
        body,
        grid=(job.n_blocks,),
        in_specs=job.in_specs(block_of),
        out_specs=job.out_specs(block_of),
        out_shape=list(job.out_shape),
        scratch_shapes=list(job.scratch_shapes),
        compiler_params=_params(1, vmem_mib),
        name=name,
    )(*job.args)


def _softmax_rows(s):
    e = jnp.exp(s - jnp.max(s, axis=-1, keepdims=True))
    return e / jnp.sum(e, axis=-1, keepdims=True)


def _xattn_block_kernel(x_ref, g_ref, wq_ref, k_ref, v_ref, wo_ref, o_ref):
    x = x_ref[...]
    xq = jnp.dot(_rms(x, g_ref[...]).astype(BF16), wq_ref[...],
                 preferred_element_type=F32).astype(BF16)
    heads = []
    for h in range(XA_HEADS):
        sl = slice(h * XA_DH, (h + 1) * XA_DH)
        s = lax.dot_general(xq[:, sl], k_ref[:, sl].astype(BF16),
                            (((1,), (1,)), ((), ())), preferred_element_type=F32)
        p = _softmax_rows(s * (XA_DH ** -0.5))
        heads.append(_mm(p, v_ref[:, sl]).astype(BF16))
    o_ref[...] = x + jnp.dot(jnp.concatenate(heads, axis=1), wo_ref[...],
                             preferred_element_type=F32)


def _xattn_block(x, g, wq16, mk, mv, wo16, *, batch, seq, tq, name):
    nt = seq // tq
    assert seq % tq == 0 and wq16.dtype == BF16 and wo16.dtype == BF16
    d = D_MODEL
    resident = dict(pipeline_mode=pl.Buffered(1))
    return pl.pallas_call(
        _xattn_block_kernel,
        grid=(batch, nt),
        in_specs=[
            pl.BlockSpec((tq, d), lambda b, t: (b * nt + t, 0)),
            pl.BlockSpec((1, d), lambda b, t: (0, 0)),
            pl.BlockSpec((d, d), lambda b, t: (0, 0), **resident),
            pl.BlockSpec((N_MEM, d), lambda b, t: (b, 0)),
            pl.BlockSpec((N_MEM, d), lambda b, t: (b, 0)),
            pl.BlockSpec((d, d), lambda b, t: (0, 0), **resident),
        ],
        out_specs=pl.BlockSpec((tq, d), lambda b, t: (b * nt + t, 0)),
        out_shape=jax.ShapeDtypeStruct((batch * seq, d), F32),
        compiler_params=_params(2, 56),
        name=name,
    )(x, g.reshape(1, d), wq16, mk, mv, wo16)


def _xattn_own_mask():
    nrow = N_MEM * XA_HEADS
    col_head = lax.broadcasted_iota(jnp.int32, (V7X_SUBLANES, nrow), 1) & (XA_HEADS - 1)
    row_head = lax.broadcasted_iota(jnp.int32, (V7X_SUBLANES, nrow), 0) & (XA_HEADS - 1)
    return col_head == row_head


def _xattn_probs(q_ref, k_ref, j, own):
    kf = k_ref[j].reshape(N_MEM * XA_HEADS, XA_DH)
    q8 = jnp.concatenate([q_ref[j]] * (V7X_SUBLANES // XA_HEADS), axis=0)
    s = lax.dot_general(q8.astype(BF16), kf.astype(BF16), (((1,), (1,)), ((), ())),
                        preferred_element_type=F32)
    s = jnp.where(own, s * (XA_DH ** -0.5), NEG)
    e = jnp.where(own, jnp.exp(s - jnp.max(s, axis=-1, keepdims=True)), 0.0)
    return e / jnp.sum(e, axis=-1, keepdims=True)


def _xattn_values(v_ref, o_ref, j, p):
    vf = v_ref[j].reshape(N_MEM * XA_HEADS, XA_DH)
    o_ref[j] = _mm(p, vf)[0:XA_HEADS, :]


def _xattn_step_kernel(q_ref, k_ref, v_ref, o_ref, *, sb):
    own = _xattn_own_mask()
    for j in range(sb):
        _xattn_values(v_ref, o_ref, j, _xattn_probs(q_ref, k_ref, j, own))


def _xattn_step(xq, ck, cv, *, sb, name):
    nb = xq.shape[0]
    assert XA_HEADS & (XA_HEADS - 1) == 0
    assert nb % sb == 0 and ck.shape == (nb, N_MEM, XA_HEADS, XA_DH)
    cache_spec = pl.BlockSpec((sb, N_MEM, XA_HEADS, XA_DH), lambda i: (i, 0, 0, 0))
    q_spec = pl.BlockSpec((sb, XA_HEADS, XA_DH), lambda i: (i, 0, 0))
    return pl.pallas_call(
        functools.partial(_xattn_step_kernel, sb=sb),
        grid=(nb // sb,),
        in_specs=[q_spec, cache_spec, cache_spec],
        out_specs=q_spec,
        out_shape=jax.ShapeDtypeStruct((nb, XA_HEADS, XA_DH), F32),
        compiler_params=_params(1, 40),
        name=name,
    )(xq.reshape(nb, XA_HEADS, XA_DH), ck, cv).reshape(nb, XA_HEADS * XA_DH)


def _tiles(rows):
    tm = min(rows, 1024)
    assert rows % tm == 0
    return tm


def kernel(x_prompt, x_sample, mem_prompt, state_rg_h, state_rg_conv, state_ml_C, state_ml_n, state_ml_m, cache_mem_k, cache_mem_v, g_mix, w_in, conv_w, conv_b, w_rg_a, b_rg_a, w_rg_x, b_rg_x, rg_lambda, b_ml_i, b_ml_f, g_rg_out, g_ml_out, w_out, g_xa, g_mem, w_xa_q, w_xa_k, w_xa_v, w_xa_o, g_ffn, w_ffn_gate, w_ffn_up, w_ffn_down, g_final):
    depth = g_mix.shape[0]
    assert depth == 1, "single trunk layer"
    bp, seq, d = x_prompt.shape
    bs_, dec_seq, _ = x_sample.shape
    assert d == D_MODEL and dec_seq == 1
    n_mem = mem_prompt.shape[1]
    assert n_mem == N_MEM
    dff = w_ffn_gate.shape[-1]
    in_w = w_in.shape[-1]
    assert in_w == IN_MAIN + N_GATE

    w_in_t = jnp.swapaxes(w_in, 1, 2).reshape(in_w, d)
    w_gate_pad = jnp.pad(w_in_t[IN_MAIN:], ((0, V7X_LANES - N_GATE), (0, 0)))
    cw = conv_w.reshape(CONV_W, RG_WIDTH)
    wgate = jnp.concatenate([w_rg_a.reshape(RG_BLOCKS, RG_BLOCK, RG_BLOCK),
                             w_rg_x.reshape(RG_BLOCKS, RG_BLOCK, RG_BLOCK)], axis=-1)
    rg_args = (cw, conv_b.reshape(-1), wgate, b_rg_a.reshape(-1), b_rg_x.reshape(-1),
               rg_lambda.reshape(-1), g_rg_out.reshape(-1))
    b_i = b_ml_i.reshape(ML_HEADS)
    b_f = b_ml_f.reshape(ML_HEADS)
    g_ml = g_ml_out.reshape(-1)
    w_out2 = w_out.reshape(d, d)
    w_q = w_xa_q.reshape(d, d)
    w_k = w_xa_k.reshape(d, d)
    w_v = w_xa_v.reshape(d, d)
    w_o = w_xa_o.reshape(d, d)
    w_fg = w_ffn_gate.reshape(d, dff)
    w_fu = w_ffn_up.reshape(d, dff)
    w_fd = w_ffn_down.reshape(dff, d)

    row_tile = _tiles(bp * seq)
    rg_zero_conv = jnp.zeros((bp, CONV_W - 1, RG_WIDTH), F32)
    rg_zero_h = jnp.zeros((bp, RG_WIDTH), F32)

    xs = x_sample.reshape(bs_, d)
    z_s, zg_s, w_in16 = _norm_linear(
        xs, g_mix.reshape(-1), w_in_t, n_out=IN_MAIN, tm=bs_, tn=1024, w_gate=w_gate_pad,
        w_is_nk=True, emit_w16=True, name="in_proj_s")
    y_rg_s, s_h, s_conv = _rglru_step(
        z_s, state_rg_conv.reshape(bs_, CONV_W - 1, RG_WIDTH), state_rg_h.reshape(bs_, RG_WIDTH),
        *rg_args, name="rglru_s")
    mlstm_s_job = _mlstm_step_job(
        z_s, zg_s, state_ml_C.reshape(bs_, ML_HEADS, ML_DK, ML_DV),
        state_ml_n.reshape(bs_, ML_HEADS, ML_DK), state_ml_m.reshape(bs_, ML_HEADS),
        b_i, b_f, g_ml, bs=V7X_SUBLANES, parts=2)
    tp = bp * seq
    xp = x_prompt.reshape(tp, d)
    in_cols, in_rows = IN_MAIN // 1024, tp // row_tile
    if (mlstm_s_job.n_blocks % in_rows == 0
            and mlstm_s_job.n_blocks // in_rows * mlstm_s_job.parts < in_cols):
        z_p, zg_p, y_ml_s, s_c, s_n, s_m = _norm_linear(
            xp, g_mix.reshape(-1), w_in16, n_out=IN_MAIN, tm=row_tile, tn=1024, w_gate=w_gate_pad,
            w_is_nk=True, side=mlstm_s_job, vmem_mib=60, name="in_proj_p")
    else:
        z_p, zg_p = _norm_linear(xp, g_mix.reshape(-1), w_in16, n_out=IN_MAIN, tm=row_tile,
                                 tn=1024, w_gate=w_gate_pad, w_is_nk=True, name="in_proj_p")
        y_ml_s, s_c, s_n, s_m = _run_side_job(mlstm_s_job, vmem_mib=32, name="mlstm_s")
    x1_s, w_out16_rg, w_out16_ml = _linear_residual(
        [y_rg_s, y_ml_s], [(w_out2, 0), (w_out2, 1)], xs, tm=bs_, tn=1024, emit_w16=True,
        name="mix_out_s")
    xq_s, w_q16 = _norm_linear(x1_s, g_xa.reshape(-1), w_q, n_out=d, tm=bs_, tn=1024,
                               emit_w16=True, name="xa_q_s")
    ck = cache_mem_k.reshape(bs_, n_mem, XA_HEADS, XA_DH)
    cv = cache_mem_v.reshape(bs_, n_mem, XA_HEADS, XA_DH)

    rg_tl = min(seq, 256)
    rg_steps = bp * (seq // rg_tl)
    if bs_ % rg_steps == 0 and RG_BLOCKS % (bs_ // rg_steps) == 0:
        y_rg_p, p_h, p_conv, o_s = _rglru_seq(
            z_p, rg_zero_conv, rg_zero_h, *rg_args, batch=bp, seq=seq, tl=rg_tl,
            side_xattn=(xq_s, ck, cv), name="rglru_p")
        o_s = o_s.reshape(bs_, d)
    else:
        y_rg_p, p_h, p_conv = _rglru_seq(z_p, rg_zero_conv, rg_zero_h, *rg_args, batch=bp,
                                         seq=seq, tl=rg_tl, name="rglru_p")
        o_s = _xattn_step(xq_s, ck, cv, sb=2, name="xattn_s")

    x3_s, w_o16 = _linear_residual([o_s], [(w_o, 0)], x1_s, tm=bs_, tn=1024, emit_w16=True,
                                   name="xa_out_s")
    y_s, w_fg16, w_fu16, w_fd16 = _ffn(x3_s, g_ffn.reshape(-1), w_fg, w_fu, w_fd, g_final,
                                       tm=bs_, tf=512, emit_w16=True, vmem_mib=48, name="ffn_s")

    y_ml_p, p_c, p_n, p_m = _mlstm_seq(z_p, zg_p, b_i, b_f, g_ml, batch=bp, seq=seq,
                                       cs=min(seq, ML_CHUNK), name="mlstm_p")
    mem2 = mem_prompt.reshape(bp * n_mem, d)
    tmem = min(bp * n_mem, 256)
    mk, mk_heads = _norm_linear(mem2, g_mem.reshape(-1), w_k, n_out=d, tm=tmem, tn=d,
                                heads_out=(XA_HEADS, XA_DH), name="mem_k")
    mv, mv_heads = _norm_linear(mem2, g_mem.reshape(-1), w_v, n_out=d, tm=tmem, tn=d,
                                heads_out=(XA_HEADS, XA_DH), name="mem_v")
    proj_tile = min(row_tile, 512)
    x1_p = _linear_residual([y_rg_p, y_ml_p], [(w_out16_rg, 0), (w_out16_ml, 0)], xp,
                            tm=proj_tile, tn=d, name="mix_out_p")
    x3_p = _xattn_block(x1_p, g_xa.reshape(-1), w_q16, mk, mv, w_o16, batch=bp, seq=seq,
                        tq=min(seq, proj_tile), name="xattn_p")
    y_p = _ffn(x3_p, g_ffn.reshape(-1), w_fg16, w_fu16, w_fd16, g_final, tm=row_tile, tf=512,
               vmem_mib=60, name="ffn_p")

    return (
        y_p.reshape(bp, seq, d),
        y_s.reshape(bs_, 1, d),
        p_h.reshape(1, bp, RG_WIDTH),
        p_conv.reshape(1, bp, CONV_W - 1, RG_WIDTH),
        p_c.reshape(1, bp, ML_HEADS, ML_DK, ML_DV),
        p_n.reshape(1, bp, ML_HEADS, ML_DK),
        p_m[:, :, 0].reshape(1, bp, ML_HEADS),
        mk_heads.reshape(1, bp, n_mem, XA_HEADS, XA_DH),
        mv_heads.reshape(1, bp, n_mem, XA_HEADS, XA_DH),
        s_h.reshape(1, bs_, RG_WIDTH),
        s_conv.reshape(1, bs_, CONV_W - 1, RG_WIDTH),
        s_c.reshape(1, bs_, ML_HEADS, ML_DK, ML_DV),
        s_n.reshape(1, bs_, ML_HEADS, ML_DK),
        s_m.reshape(1, bs_, ML_HEADS),
    )
```

```python
import functools
from typing import Callable, NamedTuple

import jax
import jax.numpy as jnp
from jax import lax
from jax.experimental import pallas as pl
from jax.experimental.pallas import tpu as pltpu

F32 = jnp.float32
BF16 = jnp.bfloat16

D_MODEL = 2048
RG_WIDTH = D_MODEL // 2
RG_BLOCKS = 8
RG_BLOCK = RG_WIDTH // RG_BLOCKS
CONV_W = 4
RG_C = 8.0
ML_HEADS = 4
ML_WIDTH = D_MODEL - RG_WIDTH
ML_DV = ML_WIDTH // ML_HEADS
ML_DK = ML_DV // 2
N_MEM = 256
XA_HEADS = 4
XA_DH = D_MODEL // XA_HEADS
EPS = 1e-6
NEG = -1e30

OFF_RGX = 0
OFF_RGG = OFF_RGX + RG_WIDTH
OFF_Q = OFF_RGG + RG_WIDTH
OFF_K = OFF_Q + ML_HEADS * ML_DK
OFF_V = OFF_K + ML_HEADS * ML_DK
OFF_O = OFF_V + ML_WIDTH
OFF_I = OFF_O + ML_WIDTH
IN_MAIN = OFF_I
N_GATE = 2 * ML_HEADS

V7X_LANES = 128
V7X_SUBLANES = 8
V7X_VMEM_BYTES = 64 * 2**20

ML_CHUNK = 256


class SideJob(NamedTuple):
    n_blocks: int
    parts: int
    args: tuple
    in_specs: Callable
    out_shape: tuple
    out_specs: Callable
    scratch_shapes: tuple
    body: Callable


def _params(n_axes, vmem_mib):
    assert vmem_mib * 2**20 <= V7X_VMEM_BYTES
    return pltpu.CompilerParams(
        dimension_semantics=("arbitrary",) * n_axes,
        vmem_limit_bytes=vmem_mib * 2**20,
    )


def _rms(x, g):
    ms = jnp.mean(x * x, axis=-1, keepdims=True)
    return x * lax.rsqrt(ms + EPS) * g


def _softplus(u):
    return jnp.maximum(u, 0.0) + jnp.log1p(jnp.exp(-jnp.abs(u)))


def _log_sigmoid(u):
    return -_softplus(-u)


def _sigmoid(u):
    return 0.5 * jnp.tanh(0.5 * u) + 0.5


def _gelu_tanh(x):
    c = 0.7978845608028654
    half_x = 0.5 * x
    return half_x + half_x * jnp.tanh(x * (c + (c * 0.044715) * (x * x)))


def _sqrt_nonneg(v):
    return jnp.where(v > 0.0, v * lax.rsqrt(v), 0.0)


def _mm(a, b):
    return jnp.dot(a.astype(BF16), b.astype(BF16), preferred_element_type=F32)


def _exact_cumsum_lanes(x):
    n = x.shape[-1]
    upper = jnp.where(lax.broadcasted_iota(jnp.int32, (n, n), 0)
                      <= lax.broadcasted_iota(jnp.int32, (n, n), 1), 1.0, 0.0).astype(BF16)
    hi = x.astype(BF16)
    rest = x - hi.astype(F32)
    mid = rest.astype(BF16)
    lo = (rest - mid.astype(F32)).astype(BF16)
    return (jnp.dot(hi, upper, preferred_element_type=F32)
            + jnp.dot(mid, upper, preferred_element_type=F32)
            + jnp.dot(lo, upper, preferred_element_type=F32))


def _dot_w(a, w_ref, w_is_nk):
    w = w_ref[...].astype(BF16)
    if w_is_nk:
        return lax.dot_general(a, w, (((1,), (1,)), ((), ())), preferred_element_type=F32)
    return jnp.dot(a, w, preferred_element_type=F32)


def _norm_linear_kernel(*refs, with_gate, w_is_nk, emit_w16, heads_out, side, side_cols):
    refs = list(refs)
    n_si, n_so, n_ss = ((len(side.args), len(side.out_shape), len(side.scratch_shapes))
                        if side else (0, 0, 0))
    n_in = 3 + with_gate + n_si
    n_out = 1 + bool(heads_out) + with_gate + emit_w16 + n_so
    x_ref, g_ref, w_ref = refs[:3]
    wg_ref = refs[3] if with_gate else None
    side_in = refs[3 + with_gate:n_in]
    outs = refs[n_in:n_in + n_out]
    xn_ref = refs[n_in + n_out]
    side_scratch = refs[n_in + n_out + 1:]
    o_ref = outs[0]
    oh_ref = outs[1] if heads_out else None
    og_ref = outs[1 + bool(heads_out)] if with_gate else None
    w16_ref = outs[n_out - n_so - 1] if emit_w16 else None
    side_out = outs[n_out - n_so:]
    col = pl.program_id(1)

    def column_tile(xn):
        acc = _dot_w(xn, w_ref, w_is_nk)
        o_ref[...] = acc.astype(o_ref.dtype)
        if heads_out:
            oh_ref[...] = acc.reshape(oh_ref.shape)
        if emit_w16:
            w16_ref[...] = w_ref[...].astype(BF16)

    @pl.when(col == 0)
    def _():
        xn = _rms(x_ref[...], g_ref[...]).astype(BF16)
        xn_ref[...] = xn
        if with_gate:
            og_ref[...] = _dot_w(xn, wg_ref, w_is_nk)
        column_tile(xn)

    if side:
        for part in range(side.parts):
            @pl.when((col > 0) & (col <= side_cols) & ((col - 1) % side.parts == part))
            def _(part=part):
                column_tile(xn_ref[...])
                side.body(side_in, side_out, side_scratch, part)

    @pl.when(col > (side_cols if side else 0))
    def _():
        column_tile(xn_ref[...])


def _norm_linear(x, g, w, *, n_out, tm, tn, out_dtype=F32, w_gate=None, w_is_nk=False,
                 emit_w16=False, heads_out=None, side=None, vmem_mib=48, name):
    m, k = x.shape
    k_ax, n_ax = (1, 0) if w_is_nk else (0, 1)
    assert m % tm == 0 and n_out % tn == 0 and w.shape[k_ax] == k and n_out <= w.shape[n_ax]
    assert not emit_w16 or m == tm
    assert heads_out is None or (tn == n_out and heads_out[0] * heads_out[1] == n_out)
    n_rows, n_cols = m // tm, n_out // tn
    with_gate = w_gate is not None
    w_mode = dict(pipeline_mode=pl.Buffered(1)) if tn == n_out else {}
    w_spec = (pl.BlockSpec((tn, k), lambda i, j: (j, 0), **w_mode) if w_is_nk
              else pl.BlockSpec((k, tn), lambda i, j: (0, j), **w_mode))
    in_specs = [
        pl.BlockSpec((tm, k), lambda i, j: (i, 0)),
        pl.BlockSpec((1, k), lambda i, j: (0, 0)),
        w_spec,
    ]
    out_shape = [jax.ShapeDtypeStruct((m, n_out), out_dtype)]
    out_specs = [pl.BlockSpec((tm, tn), lambda i, j: (i, j))]
    scratch_shapes = [pltpu.VMEM((tm, k), BF16)]
    args = [x, g.reshape(1, k), w]
    if heads_out:
        out_shape.append(jax.ShapeDtypeStruct((m, *heads_out), F32))
        out_specs.append(pl.BlockSpec((tm, *heads_out), lambda i, j: (i, 0, 0)))
    if with_gate:
        ng = w_gate.shape[n_ax]
        in_specs.append(pl.BlockSpec(w_gate.shape, lambda i, j: (0, 0)))
        out_shape.append(jax.ShapeDtypeStruct((m, ng), F32))
        out_specs.append(pl.BlockSpec((tm, ng), lambda i, j: (i, 0)))
        args.append(w_gate)
    if emit_w16:
        out_shape.append(jax.ShapeDtypeStruct((n_out, k) if w_is_nk else (k, n_out), BF16))
        out_specs.append(pl.BlockSpec((tn, k), lambda i, j: (j, 0)) if w_is_nk
                         else pl.BlockSpec((k, tn), lambda i, j: (0, j)))
    side_cols = 0
    if side:
        assert side.n_blocks % n_rows == 0
        blocks_per_row = side.n_blocks // n_rows
        side_cols = blocks_per_row * side.parts
        assert 0 < side_cols < n_cols
        block_of = lambda i, j: (i * blocks_per_row
                                 + jnp.clip(j - 1, 0, side_cols - 1) // side.parts)
        in_specs += side.in_specs(block_of)
        out_specs += side.out_specs(block_of)
        out_shape += list(side.out_shape)
        scratch_shapes += list(side.scratch_shapes)
        args += list(side.args)
    out = pl.pallas_call(
        functools.partial(_norm_linear_kernel, with_gate=with_gate, w_is_nk=w_is_nk,
                          emit_w16=emit_w16, heads_out=heads_out, side=side, side_cols=side_cols),
        grid=(n_rows, n_cols),
        in_specs=in_specs,
        out_specs=out_specs,
        out_shape=out_shape,
        scratch_shapes=scratch_shapes,
        compiler_params=_params(2, vmem_mib),
        name=name,
    )(*args)
    return out if len(out) > 1 else out[0]


def _linear_res_kernel(*refs, n_in, emit_w16):
    a_refs = refs[:n_in]
    w_refs = refs[n_in:2 * n_in]
    res_ref = refs[2 * n_in]
    o_ref = refs[2 * n_in + 1]
    acc = res_ref[...]
    for a_ref, w_ref in zip(a_refs, w_refs):
        acc = acc + _mm(a_ref[...], w_ref[...])
    o_ref[...] = acc
    if emit_w16:
        for w_ref, wc_ref in zip(w_refs, refs[2 * n_in + 2:]):
            wc_ref[...] = w_ref[...].astype(BF16)


def _linear_residual(parts, weights, res, *, tm, tn, emit_w16=False, vmem_mib=48, name):
    m, n = res.shape
    kp = parts[0].shape[1]
    assert all(p.shape == (m, kp) for p in parts) and len(weights) == len(parts)
    assert m % tm == 0 and n % tn == 0 and (not emit_w16 or m == tm)
    n_in = len(parts)
    in_specs = [pl.BlockSpec((tm, kp), lambda i, j: (i, 0)) for _ in parts]
    w_mode = dict(pipeline_mode=pl.Buffered(1)) if tn == n else {}
    in_specs += [pl.BlockSpec((kp, tn), lambda i, j, rb=rb: (rb, j), **w_mode)
                 for _, rb in weights]
    in_specs.append(pl.BlockSpec((tm, tn), lambda i, j: (i, j)))
    out_specs = [pl.BlockSpec((tm, tn), lambda i, j: (i, j))]
    out_shape = [jax.ShapeDtypeStruct((m, n), F32)]
    if emit_w16:
        out_specs += [pl.BlockSpec((kp, tn), lambda i, j: (0, j)) for _ in parts]
        out_shape += [jax.ShapeDtypeStruct((kp, n), BF16) for _ in parts]
    out = pl.pallas_call(
        functools.partial(_linear_res_kernel, n_in=n_in, emit_w16=emit_w16),
        grid=(m // tm, n // tn),
        in_specs=in_specs,
        out_specs=out_specs,
        out_shape=out_shape,
        compiler_params=_params(2, vmem_mib),
        name=name,
    )(*parts, *[w for w, _ in weights], res)
    return out if emit_w16 else out[0]


def _ffn_kernel(x_ref, g_ref, wg_ref, wu_ref, wd_ref, gf_ref, o_ref, *rest, emit_w16):
    xf_ref = rest[-1]
    f = pl.program_id(1)
    last = pl.num_programs(1) - 1

    def hidden_tile(xf):
        wg = wg_ref[...].astype(BF16)
        wu = wu_ref[...].astype(BF16)
        wd = wd_ref[...].astype(BF16)
        if emit_w16:
            for dst, val in zip(rest[:3], (wg, wu, wd)):
                dst[...] = val
        gate = jnp.dot(xf, wg, preferred_element_type=F32)
        up = jnp.dot(xf, wu, preferred_element_type=F32)
        hidden = (gate * _sigmoid(gate)) * up
        return jnp.dot(hidden.astype(BF16), wd, preferred_element_type=F32)

    @pl.when(f == 0)
    def _():
        x = x_ref[...]
        xf = _rms(x, g_ref[...]).astype(BF16)
        xf_ref[...] = xf
        o_ref[...] = x + hidden_tile(xf)

    @pl.when((f > 0) & (f < last))
    def _():
        o_ref[...] += hidden_tile(xf_ref[...])

    @pl.when(f == last)
    def _():
        o_ref[...] = _rms(o_ref[...] + hidden_tile(xf_ref[...]), gf_ref[...])


def _ffn(x, g, w_gate, w_up, w_down, g_final, *, tm, tf, emit_w16=False, vmem_mib, name):
    m, d = x.shape
    dff = w_gate.shape[1]
    assert m % tm == 0 and dff % tf == 0 and dff // tf >= 2 and (not emit_w16 or m == tm)
    up_spec = pl.BlockSpec((d, tf), lambda i, f: (0, f))
    down_spec = pl.BlockSpec((tf, d), lambda i, f: (f, 0))
    out_specs = [pl.BlockSpec((tm, d), lambda i, f: (i, 0))]
    out_shape = [jax.ShapeDtypeStruct((m, d), F32)]
    if emit_w16:
        out_specs += [up_spec, up_spec, down_spec]
        out_shape += [jax.ShapeDtypeStruct(w.shape, BF16) for w in (w_gate, w_up, w_down)]
    out = pl.pallas_call(
        functools.partial(_ffn_kernel, emit_w16=emit_w16),
        grid=(m // tm, dff // tf),
        in_specs=[
            pl.BlockSpec((tm, d), lambda i, f: (i, 0)),
            pl.BlockSpec((1, d), lambda i, f: (0, 0)),
            up_spec,
            up_spec,
            down_spec,
            pl.BlockSpec((1, d), lambda i, f: (0, 0)),
        ],
        out_specs=out_specs,
        out_shape=out_shape,
        scratch_shapes=[pltpu.VMEM((tm, d), BF16)],
        compiler_params=_params(2, vmem_mib),
        name=name,
    )(x, g.reshape(1, d), w_gate, w_up, w_down, g_final.reshape(1, d))
    return out if emit_w16 else out[0]


RG_GATE_LOOKAHEAD = 2


def _rg_gates(xr, wgate_ref, ba_ref, bx_ref, lam_ref, a_ref, b_ref, side=None):
    blocks = [slice(n * RG_BLOCK, (n + 1) * RG_BLOCK) for n in range(RG_BLOCKS)]
    gates = {}

    def issue_gate(n):
        if n < RG_BLOCKS:
            gates[n] = _mm(xr[:, blocks[n]], wgate_ref[n])

    rows, score_phase, value_phase = side if side else (0, None, None)
    per = RG_BLOCKS // rows if rows else RG_BLOCKS
    scores = {}
    for n in range(RG_GATE_LOOKAHEAD):
        issue_gate(n)
    for n, sl in enumerate(blocks):
        if rows and n % per == 0:
            j = n // per
            scores[j] = score_phase(j)
            if j > 0:
                value_phase(j - 1, scores.pop(j - 1))
        xn = xr[:, sl]
        g = gates.pop(n)
        issue_gate(n + RG_GATE_LOOKAHEAD)
        r = _sigmoid(g[:, :RG_BLOCK] + ba_ref[:, sl])
        ig = _sigmoid(g[:, RG_BLOCK:] + bx_ref[:, sl])
        a = jnp.exp(r * (-RG_C * _softplus(-lam_ref[:, sl])))
        a_ref[:, sl] = a
        mult = _sqrt_nonneg(jnp.maximum(1.0 - a * a, 0.0))
        b_ref[:, sl] = mult * (ig * xn)
    if rows:
        value_phase(rows - 1, scores.pop(rows - 1))


def _mlstm_chunk(q, k, v, o_gate, li_col, li_row, lf_row, bcum_row, causal, g, c_state, n_state, m):
    cs = q.shape[0]
    bcum_col = jnp.sum(jnp.where(causal, lf_row, 0.0), axis=1, keepdims=True)
    log_d = jnp.where(causal, bcum_col - bcum_row + li_row, NEG)
    inter = bcum_col + m
    m_t = jnp.maximum(inter, jnp.max(log_d, axis=1, keepdims=True))
    dmat = jnp.exp(log_d - m_t)
    sc = jnp.exp(inter - m_t)
    qb = q.astype(BF16)
    kb = k.astype(BF16)
    vb = v.astype(BF16)
    qk = lax.dot_general(qb, kb, (((1,), (1,)), ((), ())), preferred_element_type=F32) * dmat
    num = sc * jnp.dot(qb, c_state.astype(BF16), preferred_element_type=F32) + _mm(qk, vb)
    den = sc * jnp.sum(q * n_state, axis=1, keepdims=True) + jnp.sum(qk, axis=1, keepdims=True)
    den = jnp.maximum(jnp.abs(den), jnp.exp(-m_t))
    hh = num / den
    m_new = m_t[cs - 1:cs, :]
    b_last = bcum_col[cs - 1:cs, :]
    w_end = jnp.exp(b_last - bcum_col + li_col - m_new)
    dec = jnp.exp(b_last + m - m_new)
    wk = w_end * k
    c_new = dec * c_state + lax.dot_general(
        wk.astype(BF16), vb, (((0,), (0,)), ((), ())), preferred_element_type=F32)
    n_new = dec * n_state + jnp.sum(wk, axis=0, keepdims=True)
    y = _rms(hh, g) * _sigmoid(o_gate)
    return y, c_new, n_new, m_new


N_RG_IN = 11
N_ML_IN = 8


def _mixer_seq_kernel(*refs, tl, side_rows):
    (zx_ref, zg_ref, conv0_ref, h0_ref, cw_ref, cb_ref, wgate_ref, ba_ref, bx_ref, lam_ref,
     gout_ref) = refs[:N_RG_IN]
    (bi_ref, bf_ref, zq_ref, zk_ref, zv_ref, zo_ref, gates_ref,
     gml_ref) = refs[N_RG_IN:N_RG_IN + N_ML_IN]
    n_in = N_RG_IN + N_ML_IN + (3 if side_rows else 0)
    if side_rows:
        sq_ref, sk_ref, sv_ref = refs[N_RG_IN + N_ML_IN:n_in]
    y_ref, hlast_ref, convn_ref, yml_ref, cout_ref, nout_ref, mout_ref = refs[n_in:n_in + 7]
    so_ref = refs[n_in + 7] if side_rows else None
    xe_ref, a_ref, b_ref, h_ref, hc_ref, lf_ref, cst_ref, nst_ref, mst_ref = refs[-9:]
    t = pl.program_id(1)
    pad = V7X_SUBLANES

    @pl.when(t == 0)
    def _():
        xe_ref[pad - 3:pad, :] = conv0_ref[...]
        hc_ref[...] = h0_ref[...]
        cst_ref[...] = jnp.zeros_like(cst_ref)
        nst_ref[...] = jnp.zeros_like(nst_ref)
        mst_ref[...] = jnp.zeros_like(mst_ref)
        lf_ref[...] = jnp.zeros_like(lf_ref)

    @pl.when(t > 0)
    def _():
        xe_ref[pad - 3:pad, :] = xe_ref[tl + pad - 3:tl + pad, :]

    x = zx_ref[...]
    xe_ref[pad:tl + pad, :] = x
    xr = (xe_ref[pad - 3:tl + pad - 3, :] * cw_ref[0:1, :]
          + xe_ref[pad - 2:tl + pad - 2, :] * cw_ref[1:2, :]
          + xe_ref[pad - 1:tl + pad - 1, :] * cw_ref[2:3, :]
          + x * cw_ref[3:4, :]) + cb_ref[...]
    side = None
    if side_rows:
        own = _xattn_own_mask()
        side = (side_rows,
                lambda j: _xattn_probs(sq_ref, sk_ref, j, own),
                lambda j, p: _xattn_values(sv_ref, so_ref, j, p))

    _rg_gates(xr, wgate_ref, ba_ref, bx_ref, lam_ref, a_ref, b_ref, side=side)

    gates = gates_ref[...]
    gates_t = gates.T
    for h in range(ML_HEADS):
        lf_ref[h:h + 1, :] = _log_sigmoid(gates_t[ML_HEADS + h:ML_HEADS + h + 1, :] + bf_ref[h])
    lf_rows = lf_ref[...]
    bcum_rows = _exact_cumsum_lanes(lf_rows)
    causal = (lax.broadcasted_iota(jnp.int32, (tl, tl), 1)
              <= lax.broadcasted_iota(jnp.int32, (tl, tl), 0))
    for h in range(ML_HEADS):
        kq = slice(h * ML_DK, (h + 1) * ML_DK)
        vo = slice(h * ML_DV, (h + 1) * ML_DV)
        y_h, c_new, n_new, m_new = _mlstm_chunk(
            zq_ref[:, kq], zk_ref[:, kq] * (ML_DK ** -0.5), zv_ref[:, vo], zo_ref[:, vo],
            gates[:, h:h + 1] + bi_ref[h], gates_t[h:h + 1, :] + bi_ref[h],
            lf_rows[h:h + 1, :], bcum_rows[h:h + 1, :], causal, gml_ref[:, vo],
            cst_ref[h], nst_ref[h:h + 1, :], mst_ref[h:h + 1, 0:1])
        cst_ref[h] = c_new
        nst_ref[h:h + 1, :] = n_new
        mst_ref[h:h + 1, :] = jnp.broadcast_to(m_new, (1, V7X_LANES))
        yml_ref[:, vo] = y_h.astype(yml_ref.dtype)
    cout_ref[...] = cst_ref[...]
    nout_ref[...] = nst_ref[...]
    mout_ref[...] = mst_ref[...]

    row = lax.broadcasted_iota(jnp.int32, (V7X_SUBLANES, RG_WIDTH), 0)

    def group(gi, hc):
        r0 = pl.multiple_of(gi * V7X_SUBLANES, V7X_SUBLANES)
        a8 = a_ref[pl.ds(r0, V7X_SUBLANES), :]
        b8 = b_ref[pl.ds(r0, V7X_SUBLANES), :]
        for d in (1, 2, 4):
            keep = row >= d
            b8 = jnp.where(keep, a8 * pltpu.roll(b8, d, axis=0) + b8, b8)
            a8 = jnp.where(keep, a8 * pltpu.roll(a8, d, axis=0), a8)
        h8 = a8 * hc + b8
        h_ref[pl.ds(r0, V7X_SUBLANES), :] = h8
        return h8[V7X_SUBLANES - 1:V7X_SUBLANES, :]

    hc = lax.fori_loop(0, tl // V7X_SUBLANES, group, hc_ref[...], unroll=4)
    hc_ref[...] = hc
    hlast_ref[...] = hc
    convn_ref[...] = xe_ref[tl + pad - 3:tl + pad, :]
    y = h_ref[...] * _gelu_tanh(zg_ref[...])
    y_ref[...] = _rms(y, gout_ref[...]).astype(y_ref.dtype)


def _mixer_seq(z, zgates, conv0, h0, cw, cb, wgate, ba, bx, lam, gout, b_i, b_f, g_ml, *,
               batch, seq, tl, name, side_xattn=None):
    nt = seq // tl
    assert seq % tl == 0 and zgates.shape[1] == V7X_LANES
    w = RG_WIDTH
    dk, dv, nh = ML_DK, ML_DV, ML_HEADS
    row = lambda v: v.reshape(1, w)
    const2 = lambda b, t: (0, 0)
    smem = pl.BlockSpec(memory_space=pltpu.SMEM)
    tile = lambda width, off: pl.BlockSpec((tl, width), lambda b, t: (b * nt + t, off // width))
    in_specs = [
        tile(w, OFF_RGX),
        tile(w, OFF_RGG),
        pl.BlockSpec((None, CONV_W - 1, w), lambda b, t: (b, 0, 0)),
        pl.BlockSpec((None, 1, w), lambda b, t: (b, 0, 0)),
        pl.BlockSpec((CONV_W, w), const2),
        pl.BlockSpec((1, w), const2),
        pl.BlockSpec((RG_BLOCKS, RG_BLOCK, 2 * RG_BLOCK), lambda b, t: (0, 0, 0)),
        pl.BlockSpec((1, w), const2),
        pl.BlockSpec((1, w), const2),
        pl.BlockSpec((1, w), const2),
        pl.BlockSpec((1, w), const2),
        smem, smem,
        tile(nh * dk, OFF_Q),
        tile(nh * dk, OFF_K),
        tile(nh * dv, OFF_V),
        tile(nh * dv, OFF_O),
        tile(V7X_LANES, 0),
        pl.BlockSpec((1, nh * dv), const2),
    ]
    assert len(in_specs) == N_RG_IN + N_ML_IN
    out_specs = [
        pl.BlockSpec((tl, w), lambda b, t: (b * nt + t, 0)),
        pl.BlockSpec((None, 1, w), lambda b, t: (b, 0, 0)),
        pl.BlockSpec((None, CONV_W - 1, w), lambda b, t: (b, 0, 0)),
        pl.BlockSpec((tl, nh * dv), lambda b, t: (b * nt + t, 0)),
        pl.BlockSpec((None, nh, dk, dv), lambda b, t: (b, 0, 0, 0)),
        pl.BlockSpec((None, nh, dk), lambda b, t: (b, 0, 0)),
        pl.BlockSpec((None, nh, V7X_LANES), lambda b, t: (b, 0, 0)),
    ]
    out_shape = [
        jax.ShapeDtypeStruct((batch * seq, w), BF16),
        jax.ShapeDtypeStruct((batch, 1, w), F32),
        jax.ShapeDtypeStruct((batch, CONV_W - 1, w), F32),
        jax.ShapeDtypeStruct((batch * seq, nh * dv), BF16),
        jax.ShapeDtypeStruct((batch, nh, dk, dv), F32),
        jax.ShapeDtypeStruct((batch, nh, dk), F32),
        jax.ShapeDtypeStruct((batch, nh, V7X_LANES), F32),
    ]
    args = [z, z, conv0, h0.reshape(batch, 1, w), cw, row(cb), wgate, row(ba), row(bx), row(lam),
            row(gout), b_i, b_f, z, z, z, z, zgates, g_ml.reshape(1, nh * dv)]
    side_rows = 0
    vmem_mib = 40
    if side_xattn is not None:
        xq, ck, cv = side_xattn
        nb = xq.shape[0]
        assert nb % (batch * nt) == 0 and ck.shape == (nb, N_MEM, XA_HEADS, XA_DH)
        side_rows = nb // (batch * nt)
        q_spec = pl.BlockSpec((side_rows, XA_HEADS, XA_DH), lambda b, t: (b * nt + t, 0, 0))
        cache_spec = pl.BlockSpec((side_rows, N_MEM, XA_HEADS, XA_DH),
                                  lambda b, t: (b * nt + t, 0, 0, 0))
        in_specs += [q_spec, cache_spec, cache_spec]
        out_specs.append(q_spec)
        out_shape.append(jax.ShapeDtypeStruct((nb, XA_HEADS, XA_DH), F32))
        args += [xq.reshape(nb, XA_HEADS, XA_DH), ck, cv]
        vmem_mib = 60
    return pl.pallas_call(
        functools.partial(_mixer_seq_kernel, tl=tl, side_rows=side_rows),
        grid=(batch, nt),
        in_specs=in_specs,
        out_specs=out_specs,
        out_shape=out_shape,
        scratch_shapes=[
            pltpu.VMEM((tl + V7X_SUBLANES, w), F32),
            pltpu.VMEM((tl, w), F32),
            pltpu.VMEM((tl, w), F32),
            pltpu.VMEM((tl, w), F32),
            pltpu.VMEM((1, w), F32),
            pltpu.VMEM((V7X_SUBLANES, tl), F32),
            pltpu.VMEM((nh, dk, dv), F32),
            pltpu.VMEM((nh, dk), F32),
            pltpu.VMEM((nh, V7X_LANES), F32),
        ],
        compiler_params=_params(2, vmem_mib),
        name=name,
    )(*args)


def _rglru_step_kernel(zx_ref, zg_ref, conv_ref, h0_ref, cw_ref, cb_ref, wgate_ref, ba_ref,
                       bx_ref, lam_ref, gout_ref, y_ref, hn_ref, convn_ref, a_ref, b_ref):
    w = RG_WIDTH
    x = zx_ref[...]
    xr = (conv_ref[:, 0:w] * cw_ref[0:1, :] + conv_ref[:, w:2 * w] * cw_ref[1:2, :]
          + conv_ref[:, 2 * w:3 * w] * cw_ref[2:3, :] + x * cw_ref[3:4, :]) + cb_ref[...]
    _rg_gates(xr, wgate_ref, ba_ref, bx_ref, lam_ref, a_ref, b_ref)
    h = a_ref[...] * h0_ref[...] + b_ref[...]
    hn_ref[...] = h
    convn_ref[:, 0:2 * w] = conv_ref[:, w:3 * w]
    convn_ref[:, 2 * w:3 * w] = x
    y_ref[...] = _rms(h * _gelu_tanh(zg_ref[...]), gout_ref[...]).astype(y_ref.dtype)


def _rglru_step(z, conv, h0, cw, cb, wgate, ba, bx, lam, gout, *, name):
    nb = z.shape[0]
    w = RG_WIDTH
    row = lambda v: v.reshape(1, w)
    c0 = lambda i: (0, 0)
    return pl.pallas_call(
        _rglru_step_kernel,
        grid=(1,),
        in_specs=[
            pl.BlockSpec((nb, w), lambda i: (0, OFF_RGX // w)),
            pl.BlockSpec((nb, w), lambda i: (0, OFF_RGG // w)),
            pl.BlockSpec((nb, (CONV_W - 1) * w), c0),
            pl.BlockSpec((nb, w), c0),
            pl.BlockSpec((CONV_W, w), c0),
            pl.BlockSpec((1, w), c0),
            pl.BlockSpec((RG_BLOCKS, RG_BLOCK, 2 * RG_BLOCK), lambda i: (0, 0, 0)),
            pl.BlockSpec((1, w), c0),
            pl.BlockSpec((1, w), c0),
            pl.BlockSpec((1, w), c0),
            pl.BlockSpec((1, w), c0),
        ],
        out_specs=[
            pl.BlockSpec((nb, w), c0),
            pl.BlockSpec((nb, w), c0),
            pl.BlockSpec((nb, (CONV_W - 1) * w), c0),
        ],
        out_shape=[
            jax.ShapeDtypeStruct((nb, w), BF16),
            jax.ShapeDtypeStruct((nb, w), F32),
            jax.ShapeDtypeStruct((nb, (CONV_W - 1) * w), F32),
        ],
        scratch_shapes=[pltpu.VMEM((nb, w), F32), pltpu.VMEM((nb, w), F32)],
        compiler_params=_params(1, 32),
        name=name,
    )(z, z, conv.reshape(nb, (CONV_W - 1) * w), h0, cw, row(cb), wgate, row(ba), row(bx),
      row(lam), row(gout))


def _mlstm_step_body(in_refs, out_refs, scratch_refs, part, *, bs, parts):
    bi_ref, bf_ref, q_ref, k_ref, v_ref, o_ref, zg_ref, g_ref, c0_ref, n0_ref, m0_ref = in_refs
    y_ref, c_ref, n_ref, m_ref = out_refs
    qc_ref, = scratch_refs
    dk, dv = ML_DK, ML_DV
    nr = bs // parts
    rows = slice(part * nr, (part + 1) * nr)
    eye = (lax.broadcasted_iota(jnp.int32, (dk, dk), 0)
           == lax.broadcasted_iota(jnp.int32, (dk, dk), 1))

    def as_column(row):
        return jnp.sum(jnp.where(eye, jnp.broadcast_to(row, (dk, dk)), 0.0), axis=1, keepdims=True)

    zg = zg_ref[rows, :]
    for h in range(ML_HEADS):
        li = zg[:, h:h + 1] + bi_ref[h]
        lf = _log_sigmoid(zg[:, ML_HEADS + h:ML_HEADS + h + 1] + bf_ref[h])
        m = m0_ref[rows, h:h + 1]
        inter = lf + m
        m_t = jnp.maximum(inter, li)
        dgate = jnp.exp(li - m_t)
        sc = jnp.exp(inter - m_t)
        q = q_ref[rows, h * dk:(h + 1) * dk]
        k = k_ref[rows, h * dk:(h + 1) * dk] * (ML_DK ** -0.5)
        v = v_ref[rows, h * dv:(h + 1) * dv]
        n_old = n0_ref[rows, h, :]
        qk = jnp.sum(q * k, axis=1, keepdims=True) * dgate
        w_end = jnp.exp(li - m_t)
        dec = jnp.exp(inter - m_t)
        wk = w_end * k
        for j in range(nr):
            c_old = c0_ref[part * nr + j, h]
            qc_ref[j:j + 1, :] = jnp.sum(as_column(q[j:j + 1, :]) * c_old, axis=0, keepdims=True)
            c_ref[part * nr + j, h] = (dec[j:j + 1, :] * c_old
                                       + as_column(wk[j:j + 1, :]) * v[j:j + 1, :])
        num = sc * qc_ref[0:nr, :] + qk * v
        den = sc * jnp.sum(q * n_old, axis=1, keepdims=True) + qk
        den = jnp.maximum(jnp.abs(den), jnp.exp(-m_t))
        hh = num / den
        n_ref[rows, h, :] = dec * n_old + wk
        m_ref[rows, h:h + 1] = m_t
        y = (_rms(hh, g_ref[:, h * dv:(h + 1) * dv])
             * _sigmoid(o_ref[rows, h * dv:(h + 1) * dv]))
        y_ref[rows, h * dv:(h + 1) * dv] = y.astype(y_ref.dtype)


def _mlstm_step_job(z, zg, c0, n0, m0, b_i, b_f, g_out, *, bs, parts):
    nb = z.shape[0]
    assert nb % bs == 0 and bs % parts == 0
    dk, dv, nh = ML_DK, ML_DV, ML_HEADS
    smem = pl.BlockSpec(memory_space=pltpu.SMEM)

    def in_specs(blk):
        return [
            smem, smem,
            pl.BlockSpec((bs, nh * dk), lambda *g: (blk(*g), OFF_Q // (nh * dk))),
            pl.BlockSpec((bs, nh * dk), lambda *g: (blk(*g), OFF_K // (nh * dk))),
            pl.BlockSpec((bs, nh * dv), lambda *g: (blk(*g), OFF_V // (nh * dv))),
            pl.BlockSpec((bs, nh * dv), lambda *g: (blk(*g), OFF_O // (nh * dv))),
            pl.BlockSpec((bs, V7X_LANES), lambda *g: (blk(*g), 0)),
            pl.BlockSpec((1, nh * dv), lambda *g: (0, 0)),
            pl.BlockSpec((bs, nh, dk, dv), lambda *g: (blk(*g), 0, 0, 0)),
            pl.BlockSpec((bs, nh, dk), lambda *g: (blk(*g), 0, 0)),
            pl.BlockSpec((bs, nh), lambda *g: (blk(*g), 0)),
        ]

    def out_specs(blk):
        return [
            pl.BlockSpec((bs, nh * dv), lambda *g: (blk(*g), 0)),
            pl.BlockSpec((bs, nh, dk, dv), lambda *g: (blk(*g), 0, 0, 0)),
            pl.BlockSpec((bs, nh, dk), lambda *g: (blk(*g), 0, 0)),
            pl.BlockSpec((bs, nh), lambda *g: (blk(*g), 0)),
        ]

    return SideJob(
        n_blocks=nb // bs,
        parts=parts,
        args=(b_i, b_f, z, z, z, z, zg, g_out.reshape(1, nh * dv), c0, n0, m0),
        in_specs=in_specs,
        out_shape=(
            jax.ShapeDtypeStruct((nb, nh * dv), F32),
            jax.ShapeDtypeStruct((nb, nh, dk, dv), F32),
            jax.ShapeDtypeStruct((nb, nh, dk), F32),
            jax.ShapeDtypeStruct((nb, nh), F32),
        ),
        out_specs=out_specs,
        scratch_shapes=(pltpu.VMEM((bs, dv), F32),),
        body=functools.partial(_mlstm_step_body, bs=bs, parts=parts),
    )


def _run_side_job(job, *, vmem_mib, name):
    n_in, n_out = len(job.args), len(job.out_shape)

    def body(*refs):
        for part in range(job.parts):
            job.body(refs[:n_in], refs[n_in:n_in + n_out], refs[n_in + n_out:], part)

    block_of = lambda i: i
    return pl.pallas_call(
        body,
        grid=(job.n_blocks,),
        in_specs=job.in_specs(block_of),
        out_specs=job.out_specs(block_of),
        out_shape=list(job.out_shape),
        scratch_shapes=list(job.scratch_shapes),
        compiler_params=_params(1, vmem_mib),
        name=name,
    )(*job.args)


def _softmax_rows(s):
    e = jnp.exp(s - jnp.max(s, axis=-1, keepdims=True))
    return e / jnp.sum(e, axis=-1, keepdims=True)


def _xattn_block_kernel(x_ref, g_ref, wq_ref, k_ref, v_ref, wo_ref, o_ref):
    x = x_ref[...]
    xq = jnp.dot(_rms(x, g_ref[...]).astype(BF16), wq_ref[...],
                 preferred_element_type=F32).astype(BF16)
    heads = []
    for h in range(XA_HEADS):
        sl = slice(h * XA_DH, (h + 1) * XA_DH)
        s = lax.dot_general(xq[:, sl], k_ref[:, sl].astype(BF16),
                            (((1,), (1,)), ((), ())), preferred_element_type=F32)
        p = _softmax_rows(s * (XA_DH ** -0.5))
        heads.append(_mm(p, v_ref[:, sl]).astype(BF16))
    o_ref[...] = x + jnp.dot(jnp.concatenate(heads, axis=1), wo_ref[...],
                             preferred_element_type=F32)


def _xattn_block(x, g, wq16, mk, mv, wo16, *, batch, seq, tq, name):
    nt = seq // tq
    assert seq % tq == 0 and wq16.dtype == BF16 and wo16.dtype == BF16
    d = D_MODEL
    resident = dict(pipeline_mode=pl.Buffered(1))
    return pl.pallas_call(
        _xattn_block_kernel,
        grid=(batch, nt),
        in_specs=[
            pl.BlockSpec((tq, d), lambda b, t: (b * nt + t, 0)),
            pl.BlockSpec((1, d), lambda b, t: (0, 0)),
            pl.BlockSpec((d, d), lambda b, t: (0, 0), **resident),
            pl.BlockSpec((N_MEM, d), lambda b, t: (b, 0)),
            pl.BlockSpec((N_MEM, d), lambda b, t: (b, 0)),
            pl.BlockSpec((d, d), lambda b, t: (0, 0), **resident),
        ],
        out_specs=pl.BlockSpec((tq, d), lambda b, t: (b * nt + t, 0)),
        out_shape=jax.ShapeDtypeStruct((batch * seq, d), F32),
        compiler_params=_params(2, 56),
        name=name,
    )(x, g.reshape(1, d), wq16, mk, mv, wo16)


def _xattn_own_mask():
    nrow = N_MEM * XA_HEADS
    col_head = lax.broadcasted_iota(jnp.int32, (V7X_SUBLANES, nrow), 1) & (XA_HEADS - 1)
    row_head = lax.broadcasted_iota(jnp.int32, (V7X_SUBLANES, nrow), 0) & (XA_HEADS - 1)
    return col_head == row_head


def _xattn_probs(q_ref, k_ref, j, own):
    kf = k_ref[j].reshape(N_MEM * XA_HEADS, XA_DH)
    q8 = jnp.concatenate([q_ref[j]] * (V7X_SUBLANES // XA_HEADS), axis=0)
    s = lax.dot_general(q8.astype(BF16), kf.astype(BF16), (((1,), (1,)), ((), ())),
                        preferred_element_type=F32)
    s = jnp.where(own, s * (XA_DH ** -0.5), NEG)
    e = jnp.where(own, jnp.exp(s - jnp.max(s, axis=-1, keepdims=True)), 0.0)
    return e / jnp.sum(e, axis=-1, keepdims=True)


def _xattn_values(v_ref, o_ref, j, p):
    vf = v_ref[j].reshape(N_MEM * XA_HEADS, XA_DH)
    o_ref[j] = _mm(p, vf)[0:XA_HEADS, :]


def _xattn_step_kernel(q_ref, k_ref, v_ref, o_ref, *, sb):
    own = _xattn_own_mask()
    for j in range(sb):
        _xattn_values(v_ref, o_ref, j, _xattn_probs(q_ref, k_ref, j, own))


def _xattn_step(xq, ck, cv, *, sb, name):
    nb = xq.shape[0]
    assert XA_HEADS & (XA_HEADS - 1) == 0
    assert nb % sb == 0 and ck.shape == (nb, N_MEM, XA_HEADS, XA_DH)
    cache_spec = pl.BlockSpec((sb, N_MEM, XA_HEADS, XA_DH), lambda i: (i, 0, 0, 0))
    q_spec = pl.BlockSpec((sb, XA_HEADS, XA_DH), lambda i: (i, 0, 0))
    return pl.pallas_call(
        functools.partial(_xattn_step_kernel, sb=sb),
        grid=(nb // sb,),
        in_specs=[q_spec, cache_spec, cache_spec],
        out_specs=q_spec,
        out_shape=jax.ShapeDtypeStruct((nb, XA_HEADS, XA_DH), F32),
        compiler_params=_params(1, 40),
        name=name,
    )(xq.reshape(nb, XA_HEADS, XA_DH), ck, cv).reshape(nb, XA_HEADS * XA_DH)


def _tiles(rows):
    tm = min(rows, 1024)
    assert rows % tm == 0
    return tm


def kernel(x_prompt, x_sample, mem_prompt, state_rg_h, state_rg_conv, state_ml_C, state_ml_n, state_ml_m, cache_mem_k, cache_mem_v, g_mix, w_in, conv_w, conv_b, w_rg_a, b_rg_a, w_rg_x, b_rg_x, rg_lambda, b_ml_i, b_ml_f, g_rg_out, g_ml_out, w_out, g_xa, g_mem, w_xa_q, w_xa_k, w_xa_v, w_xa_o, g_ffn, w_ffn_gate, w_ffn_up, w_ffn_down, g_final):
    depth = g_mix.shape[0]
    assert depth == 1, "single trunk layer"
    bp, seq, d = x_prompt.shape
    bs_, dec_seq, _ = x_sample.shape
    assert d == D_MODEL and dec_seq == 1
    n_mem = mem_prompt.shape[1]
    assert n_mem == N_MEM
    dff = w_ffn_gate.shape[-1]
    in_w = w_in.shape[-1]
    assert in_w == IN_MAIN + N_GATE

    w_in_t = jnp.swapaxes(w_in, 1, 2).reshape(in_w, d)
    w_gate_pad = jnp.pad(w_in_t[IN_MAIN:], ((0, V7X_LANES - N_GATE), (0, 0)))
    cw = conv_w.reshape(CONV_W, RG_WIDTH)
    wgate = jnp.concatenate([w_rg_a.reshape(RG_BLOCKS, RG_BLOCK, RG_BLOCK),
                             w_rg_x.reshape(RG_BLOCKS, RG_BLOCK, RG_BLOCK)], axis=-1)
    rg_args = (cw, conv_b.reshape(-1), wgate, b_rg_a.reshape(-1), b_rg_x.reshape(-1),
               rg_lambda.reshape(-1), g_rg_out.reshape(-1))
    b_i = b_ml_i.reshape(ML_HEADS)
    b_f = b_ml_f.reshape(ML_HEADS)
    g_ml = g_ml_out.reshape(-1)
    w_out2 = w_out.reshape(d, d)
    w_q = w_xa_q.reshape(d, d)
    w_k = w_xa_k.reshape(d, d)
    w_v = w_xa_v.reshape(d, d)
    w_o = w_xa_o.reshape(d, d)
    w_fg = w_ffn_gate.reshape(d, dff)
    w_fu = w_ffn_up.reshape(d, dff)
    w_fd = w_ffn_down.reshape(dff, d)

    row_tile = _tiles(bp * seq)
    rg_zero_conv = jnp.zeros((bp, CONV_W - 1, RG_WIDTH), F32)
    rg_zero_h = jnp.zeros((bp, RG_WIDTH), F32)

    xs = x_sample.reshape(bs_, d)
    z_s, zg_s, w_in16 = _norm_linear(
        xs, g_mix.reshape(-1), w_in_t, n_out=IN_MAIN, tm=bs_, tn=1024, w_gate=w_gate_pad,
        w_is_nk=True, emit_w16=True, name="in_proj_s")
    y_rg_s, s_h, s_conv = _rglru_step(
        z_s, state_rg_conv.reshape(bs_, CONV_W - 1, RG_WIDTH), state_rg_h.reshape(bs_, RG_WIDTH),
        *rg_args, name="rglru_s")
    mlstm_s_job = _mlstm_step_job(
        z_s, zg_s, state_ml_C.reshape(bs_, ML_HEADS, ML_DK, ML_DV),
        state_ml_n.reshape(bs_, ML_HEADS, ML_DK), state_ml_m.reshape(bs_, ML_HEADS),
        b_i, b_f, g_ml, bs=V7X_SUBLANES, parts=2)
    tp = bp * seq
    xp = x_prompt.reshape(tp, d)
    in_cols, in_rows = IN_MAIN // 1024, tp // row_tile
    if (mlstm_s_job.n_blocks % in_rows == 0
            and mlstm_s_job.n_blocks // in_rows * mlstm_s_job.parts < in_cols):
        z_p, zg_p, y_ml_s, s_c, s_n, s_m = _norm_linear(
            xp, g_mix.reshape(-1), w_in16, n_out=IN_MAIN, tm=row_tile, tn=1024, w_gate=w_gate_pad,
            w_is_nk=True, side=mlstm_s_job, vmem_mib=60, name="in_proj_p")
    else:
        z_p, zg_p = _norm_linear(xp, g_mix.reshape(-1), w_in16, n_out=IN_MAIN, tm=row_tile,
                                 tn=1024, w_gate=w_gate_pad, w_is_nk=True, name="in_proj_p")
        y_ml_s, s_c, s_n, s_m = _run_side_job(mlstm_s_job, vmem_mib=32, name="mlstm_s")
    x1_s, w_out16_rg, w_out16_ml = _linear_residual(
        [y_rg_s, y_ml_s], [(w_out2, 0), (w_out2, 1)], xs, tm=bs_, tn=1024, emit_w16=True,
        name="mix_out_s")
    xq_s, w_q16 = _norm_linear(x1_s, g_xa.reshape(-1), w_q, n_out=d, tm=bs_, tn=1024,
                               emit_w16=True, name="xa_q_s")
    ck = cache_mem_k.reshape(bs_, n_mem, XA_HEADS, XA_DH)
    cv = cache_mem_v.reshape(bs_, n_mem, XA_HEADS, XA_DH)

    mix_tl = min(seq, ML_CHUNK)
    mix_steps = bp * (seq // mix_tl)
    mix_args = (z_p, zg_p, rg_zero_conv, rg_zero_h, *rg_args, b_i, b_f, g_ml)
    if bs_ % mix_steps == 0 and RG_BLOCKS % (bs_ // mix_steps) == 0:
        y_rg_p, p_h, p_conv, y_ml_p, p_c, p_n, p_m, o_s = _mixer_seq(
            *mix_args, batch=bp, seq=seq, tl=mix_tl, side_xattn=(xq_s, ck, cv), name="mixer_p")
        o_s = o_s.reshape(bs_, d)
    else:
        y_rg_p, p_h, p_conv, y_ml_p, p_c, p_n, p_m = _mixer_seq(
            *mix_args, batch=bp, seq=seq, tl=mix_tl, name="mixer_p")
        o_s = _xattn_step(xq_s, ck, cv, sb=2, name="xattn_s")

    x3_s, w_o16 = _linear_residual([o_s], [(w_o, 0)], x1_s, tm=bs_, tn=1024, emit_w16=True,
                                   name="xa_out_s")
    y_s, w_fg16, w_fu16, w_fd16 = _ffn(x3_s, g_ffn.reshape(-1), w_fg, w_fu, w_fd, g_final,
                                       tm=bs_, tf=512, emit_w16=True, vmem_mib=48, name="ffn_s")

    mem2 = mem_prompt.reshape(bp * n_mem, d)
    tmem = min(bp * n_mem, 256)
    mk, mk_heads = _norm_linear(mem2, g_mem.reshape(-1), w_k, n_out=d, tm=tmem, tn=d,
                                heads_out=(XA_HEADS, XA_DH), name="mem_k")
    mv, mv_heads = _norm_linear(mem2, g_mem.reshape(-1), w_v, n_out=d, tm=tmem, tn=d,
                                heads_out=(XA_HEADS, XA_DH), name="mem_v")
    proj_tile = min(row_tile, 512)
    x1_p = _linear_residual([y_rg_p, y_ml_p], [(w_out16_rg, 0), (w_out16_ml, 0)], xp,
                            tm=proj_tile, tn=d, name="mix_out_p")
    x3_p = _xattn_block(x1_p, g_xa.reshape(-1), w_q16, mk, mv, w_o16, batch=bp, seq=seq,
                        tq=min(seq, proj_tile), name="xattn_p")
    y_p = _ffn(x3_p, g_ffn.reshape(-1), w_fg16, w_fu16, w_fd16, g_final, tm=row_tile, tf=512,
               vmem_mib=60, name="ffn_p")

    return (
        y_p.reshape(bp, seq, d),
        y_s.reshape(bs_, 1, d),
        p_h.reshape(1, bp, RG_WIDTH),
        p_conv.reshape(1, bp, CONV_W - 1, RG_WIDTH),
        p_c.reshape(1, bp, ML_HEADS, ML_DK, ML_DV),
        p_n.reshape(1, bp, ML_HEADS, ML_DK),
        p_m[:, :, 0].reshape(1, bp, ML_HEADS),
        mk_heads.reshape(1, bp, n_mem, XA_HEADS, XA_DH),
        mv_heads.reshape(1, bp, n_mem, XA_HEADS, XA_DH),
        s_h.reshape(1, bs_, RG_WIDTH),
        s_conv.reshape(1, bs_, CONV_W - 1, RG_WIDTH),
        s_c.reshape(1, bs_, ML_HEADS, ML_DK, ML_DV),
        s_n.reshape(1, bs_, ML_HEADS, ML_DK),
        s_m.reshape(1, bs_, ML_HEADS),
    )
```

```python
import functools
from typing import Callable, NamedTuple

import jax
import jax.numpy as jnp
from jax import lax
from jax.experimental import pallas as pl
from jax.experimental.pallas import tpu as pltpu

F32 = jnp.float32
BF16 = jnp.bfloat16

D_MODEL = 2048
RG_WIDTH = D_MODEL // 2
RG_BLOCKS = 8
RG_BLOCK = RG_WIDTH // RG_BLOCKS
CONV_W = 4
RG_C = 8.0
ML_HEADS = 4
ML_WIDTH = D_MODEL - RG_WIDTH
ML_DV = ML_WIDTH // ML_HEADS
ML_DK = ML_DV // 2
N_MEM = 256
XA_HEADS = 4
XA_DH = D_MODEL // XA_HEADS
EPS = 1e-6
NEG = -1e30

OFF_RGX = 0
OFF_RGG = OFF_RGX + RG_WIDTH
OFF_Q = OFF_RGG + RG_WIDTH
OFF_K = OFF_Q + ML_HEADS * ML_DK
OFF_V = OFF_K + ML_HEADS * ML_DK
OFF_O = OFF_V + ML_WIDTH
OFF_I = OFF_O + ML_WIDTH
IN_MAIN = OFF_I
N_GATE = 2 * ML_HEADS

V7X_LANES = 128
V7X_SUBLANES = 8
V7X_VMEM_BYTES = 64 * 2**20

ML_CHUNK = 256


class SideJob(NamedTuple):
    n_blocks: int
    parts: int
    args: tuple
    in_specs: Callable
    out_shape: tuple
    out_specs: Callable
    scratch_shapes: tuple
    body: Callable


def _params(n_axes, vmem_mib):
    assert vmem_mib * 2**20 <= V7X_VMEM_BYTES
    return pltpu.CompilerParams(
        dimension_semantics=("arbitrary",) * n_axes,
        vmem_limit_bytes=vmem_mib * 2**20,
    )


def _rms(x, g):
    ms = jnp.mean(x * x, axis=-1, keepdims=True)
    return x * lax.rsqrt(ms + EPS) * g


def _softplus(u):
    return jnp.maximum(u, 0.0) + jnp.log1p(jnp.exp(-jnp.abs(u)))


def _log_sigmoid(u):
    return -_softplus(-u)


def _sigmoid(u):
    return 0.5 * jnp.tanh(0.5 * u) + 0.5


def _gelu_tanh(x):
    c = 0.7978845608028654
    half_x = 0.5 * x
    return half_x + half_x * jnp.tanh(x * (c + (c * 0.044715) * (x * x)))


def _sqrt_nonneg(v):
    return jnp.where(v > 0.0, v * lax.rsqrt(v), 0.0)


def _mm(a, b):
    return jnp.dot(a.astype(BF16), b.astype(BF16), preferred_element_type=F32)


def _exact_cumsum_lanes(x):
    n = x.shape[-1]
    upper = jnp.where(lax.broadcasted_iota(jnp.int32, (n, n), 0)
                      <= lax.broadcasted_iota(jnp.int32, (n, n), 1), 1.0, 0.0).astype(BF16)
    hi = x.astype(BF16)
    rest = x - hi.astype(F32)
    mid = rest.astype(BF16)
    lo = (rest - mid.astype(F32)).astype(BF16)
    return (jnp.dot(hi, upper, preferred_element_type=F32)
            + jnp.dot(mid, upper, preferred_element_type=F32)
            + jnp.dot(lo, upper, preferred_element_type=F32))


def _dot_w(a, w_ref, w_is_nk):
    w = w_ref[...].astype(BF16)
    if w_is_nk:
        return lax.dot_general(a, w, (((1,), (1,)), ((), ())), preferred_element_type=F32)
    return jnp.dot(a, w, preferred_element_type=F32)


def _norm_linear_kernel(*refs, with_gate, w_is_nk, emit_w16, heads_out, side, side_cols):
    refs = list(refs)
    n_si, n_so, n_ss = ((len(side.args), len(side.out_shape), len(side.scratch_shapes))
                        if side else (0, 0, 0))
    n_in = 3 + with_gate + n_si
    n_out = 1 + bool(heads_out) + with_gate + emit_w16 + n_so
    x_ref, g_ref, w_ref = refs[:3]
    wg_ref = refs[3] if with_gate else None
    side_in = refs[3 + with_gate:n_in]
    outs = refs[n_in:n_in + n_out]
    xn_ref = refs[n_in + n_out]
    side_scratch = refs[n_in + n_out + 1:]
    o_ref = outs[0]
    oh_ref = outs[1] if heads_out else None
    og_ref = outs[1 + bool(heads_out)] if with_gate else None
    w16_ref = outs[n_out - n_so - 1] if emit_w16 else None
    side_out = outs[n_out - n_so:]
    col = pl.program_id(1)

    def column_tile(xn):
        acc = _dot_w(xn, w_ref, w_is_nk)
        o_ref[...] = acc.astype(o_ref.dtype)
        if heads_out:
            oh_ref[...] = acc.reshape(oh_ref.shape)
        if emit_w16:
            w16_ref[...] = w_ref[...].astype(BF16)

    @pl.when(col == 0)
    def _():
        xn = _rms(x_ref[...], g_ref[...]).astype(BF16)
        xn_ref[...] = xn
        if with_gate:
            og_ref[...] = _dot_w(xn, wg_ref, w_is_nk)
        column_tile(xn)

    if side:
        for part in range(side.parts):
            @pl.when((col > 0) & (col <= side_cols) & ((col - 1) % side.parts == part))
            def _(part=part):
                column_tile(xn_ref[...])
                side.body(side_in, side_out, side_scratch, part)

    @pl.when(col > (side_cols if side else 0))
    def _():
        column_tile(xn_ref[...])


def _norm_linear(x, g, w, *, n_out, tm, tn, out_dtype=F32, w_gate=None, w_is_nk=False,
                 emit_w16=False, heads_out=None, side=None, vmem_mib=48, name):
    m, k = x.shape
    k_ax, n_ax = (1, 0) if w_is_nk else (0, 1)
    assert m % tm == 0 and n_out % tn == 0 and w.shape[k_ax] == k and n_out <= w.shape[n_ax]
    assert not emit_w16 or m == tm
    assert heads_out is None or (tn == n_out and heads_out[0] * heads_out[1] == n_out)
    n_rows, n_cols = m // tm, n_out // tn
    with_gate = w_gate is not None
    w_mode = dict(pipeline_mode=pl.Buffered(1)) if tn == n_out else {}
    w_spec = (pl.BlockSpec((tn, k), lambda i, j: (j, 0), **w_mode) if w_is_nk
              else pl.BlockSpec((k, tn), lambda i, j: (0, j), **w_mode))
    in_specs = [
        pl.BlockSpec((tm, k), lambda i, j: (i, 0)),
        pl.BlockSpec((1, k), lambda i, j: (0, 0)),
        w_spec,
    ]
    out_shape = [jax.ShapeDtypeStruct((m, n_out), out_dtype)]
    out_specs = [pl.BlockSpec((tm, tn), lambda i, j: (i, j))]
    scratch_shapes = [pltpu.VMEM((tm, k), BF16)]
    args = [x, g.reshape(1, k), w]
    if heads_out:
        out_shape.append(jax.ShapeDtypeStruct((m, *heads_out), F32))
        out_specs.append(pl.BlockSpec((tm, *heads_out), lambda i, j: (i, 0, 0)))
    if with_gate:
        ng = w_gate.shape[n_ax]
        in_specs.append(pl.BlockSpec(w_gate.shape, lambda i, j: (0, 0)))
        out_shape.append(jax.ShapeDtypeStruct((m, ng), F32))
        out_specs.append(pl.BlockSpec((tm, ng), lambda i, j: (i, 0)))
        args.append(w_gate)
    if emit_w16:
        out_shape.append(jax.ShapeDtypeStruct((n_out, k) if w_is_nk else (k, n_out), BF16))
        out_specs.append(pl.BlockSpec((tn, k), lambda i, j: (j, 0)) if w_is_nk
                         else pl.BlockSpec((k, tn), lambda i, j: (0, j)))
    side_cols = 0
    if side:
        assert side.n_blocks % n_rows == 0
        blocks_per_row = side.n_blocks // n_rows
        side_cols = blocks_per_row * side.parts
        assert 0 < side_cols < n_cols
        block_of = lambda i, j: (i * blocks_per_row
                                 + jnp.clip(j - 1, 0, side_cols - 1) // side.parts)
        in_specs += side.in_specs(block_of)
        out_specs += side.out_specs(block_of)
        out_shape += list(side.out_shape)
        scratch_shapes += list(side.scratch_shapes)
        args += list(side.args)
    out = pl.pallas_call(
        functools.partial(_norm_linear_kernel, with_gate=with_gate, w_is_nk=w_is_nk,
                          emit_w16=emit_w16, heads_out=heads_out, side=side, side_cols=side_cols),
        grid=(n_rows, n_cols),
        in_specs=in_specs,
        out_specs=out_specs,
        out_shape=out_shape,
        scratch_shapes=scratch_shapes,
        compiler_params=_params(2, vmem_mib),
        name=name,
    )(*args)
    return out if len(out) > 1 else out[0]


def _linear_res_kernel(*refs, n_in, emit_w16):
    a_refs = refs[:n_in]
    w_refs = refs[n_in:2 * n_in]
    res_ref = refs[2 * n_in]
    o_ref = refs[2 * n_in + 1]
    acc = res_ref[...]
    for a_ref, w_ref in zip(a_refs, w_refs):
        acc = acc + _mm(a_ref[...], w_ref[...])
    o_ref[...] = acc
    if emit_w16:
        for w_ref, wc_ref in zip(w_refs, refs[2 * n_in + 2:]):
            wc_ref[...] = w_ref[...].astype(BF16)


def _linear_residual(parts, weights, res, *, tm, tn, emit_w16=False, vmem_mib=48, name):
    m, n = res.shape
    kp = parts[0].shape[1]
    assert all(p.shape == (m, kp) for p in parts) and len(weights) == len(parts)
    assert m % tm == 0 and n % tn == 0 and (not emit_w16 or m == tm)
    n_in = len(parts)
    in_specs = [pl.BlockSpec((tm, kp), lambda i, j: (i, 0)) for _ in parts]
    w_mode = dict(pipeline_mode=pl.Buffered(1)) if tn == n else {}
    in_specs += [pl.BlockSpec((kp, tn), lambda i, j, rb=rb: (rb, j), **w_mode)
                 for _, rb in weights]
    in_specs.append(pl.BlockSpec((tm, tn), lambda i, j: (i, j)))
    out_specs = [pl.BlockSpec((tm, tn), lambda i, j: (i, j))]
    out_shape = [jax.ShapeDtypeStruct((m, n), F32)]
    if emit_w16:
        out_specs += [pl.BlockSpec((kp, tn), lambda i, j: (0, j)) for _ in parts]
        out_shape += [jax.ShapeDtypeStruct((kp, n), BF16) for _ in parts]
    out = pl.pallas_call(
        functools.partial(_linear_res_kernel, n_in=n_in, emit_w16=emit_w16),
        grid=(m // tm, n // tn),
        in_specs=in_specs,
        out_specs=out_specs,
        out_shape=out_shape,
        compiler_params=_params(2, vmem_mib),
        name=name,
    )(*parts, *[w for w, _ in weights], res)
    return out if emit_w16 else out[0]


FFN_HEAD_SLABS = 8


def _ffn_kernel(*refs, n_groups, emit_w16, head_tile):
    x_refs = refs[:n_groups]
    g_ref, wg_ref, wu_ref, wd_ref, gf_ref = refs[n_groups:n_groups + 5]
    n_in = n_groups + 5 + bool(head_tile)
    head_ref = refs[n_in - 1] if head_tile else None
    o_refs = refs[n_in:n_in + n_groups]
    w16_refs = refs[n_in + n_groups:n_in + n_groups + 3] if emit_w16 else ()
    xf_refs = refs[-n_groups:]
    row = pl.program_id(0)
    f = pl.program_id(1)
    last = pl.num_programs(1) - 1
    computing = (row > 0) if head_tile else True

    def weights():
        wg = wg_ref[...].astype(BF16)
        wu = wu_ref[...].astype(BF16)
        wd = wd_ref[...].astype(BF16)
        for dst, val in zip(w16_refs, (wg, wu, wd)):
            dst[...] = val
        return wg, wu, wd

    def hidden_tile(xf, w):
        wg, wu, wd = w
        gate = jnp.dot(xf, wg, preferred_element_type=F32)
        up = jnp.dot(xf, wu, preferred_element_type=F32)
        hidden = (gate * _sigmoid(gate)) * up
        return jnp.dot(hidden.astype(BF16), wd, preferred_element_type=F32)

    if head_tile:
        slab = head_ref.shape[0]

        @pl.when((row == 0) & (f < FFN_HEAD_SLABS))
        def _():
            o_refs[0][pl.ds(pl.multiple_of(f * slab, slab), slab), :] = head_ref[...]

    @pl.when(computing & (f == 0))
    def _():
        w = weights()
        for x_ref, o_ref, xf_ref in zip(x_refs, o_refs, xf_refs):
            x = x_ref[...]
            xf = _rms(x, g_ref[...]).astype(BF16)
            xf_ref[...] = xf
            o_ref[...] = x + hidden_tile(xf, w)

    @pl.when(computing & (f > 0) & (f < last))
    def _():
        w = weights()
        for o_ref, xf_ref in zip(o_refs, xf_refs):
            o_ref[...] += hidden_tile(xf_ref[...], w)

    @pl.when(computing & (f == last))
    def _():
        w = weights()
        for o_ref, xf_ref in zip(o_refs, xf_refs):
            o_ref[...] = _rms(o_ref[...] + hidden_tile(xf_ref[...], w), gf_ref[...])


def _ffn(groups, g, w_gate, w_up, w_down, g_final, *, n_rows, tf, emit_w16=False, head_tile=None,
         vmem_mib, name):
    d = groups[0][0].shape[1]
    dff = w_gate.shape[1]
    n_groups = len(groups)
    assert dff % tf == 0 and dff // tf >= 2 and (not emit_w16 or n_rows == 1)
    assert head_tile is None or (n_groups == 1 and head_tile.shape == (groups[0][1], d))
    assert all(x.shape[0] >= n_rows * tm for x, tm in groups)
    col = (lambda i, f: jnp.where(i > 0, f, 0)) if head_tile is not None else (lambda i, f: f)
    up_spec = pl.BlockSpec((d, tf), lambda i, f: (0, col(i, f)))
    down_spec = pl.BlockSpec((tf, d), lambda i, f: (col(i, f), 0))
    vec_spec = pl.BlockSpec((1, d), lambda i, f: (0, 0))
    row_specs = [pl.BlockSpec((tm, d), lambda i, f: (i, 0)) for _, tm in groups]
    in_specs = row_specs + [vec_spec, up_spec, up_spec, down_spec, vec_spec]
    args = [x for x, _ in groups] + [g.reshape(1, d), w_gate, w_up, w_down, g_final.reshape(1, d)]
    if head_tile is not None:
        tm0 = groups[0][1]
        assert tm0 % FFN_HEAD_SLABS == 0 and dff // tf >= FFN_HEAD_SLABS
        last_slab = FFN_HEAD_SLABS - 1
        in_specs.append(pl.BlockSpec(
            (tm0 // FFN_HEAD_SLABS, d),
            lambda i, f: (jnp.where(i == 0, jnp.minimum(f, last_slab), last_slab), 0)))
        args.append(head_tile)
    out_specs = list(row_specs)
    out_shape = [jax.ShapeDtypeStruct((n_rows * tm, d), F32) for _, tm in groups]
    if emit_w16:
        out_specs += [pl.BlockSpec((d, tf), lambda i, f: (0, f))] * 2
        out_specs += [pl.BlockSpec((tf, d), lambda i, f: (f, 0))]
        out_shape += [jax.ShapeDtypeStruct(w.shape, BF16) for w in (w_gate, w_up, w_down)]
    out = pl.pallas_call(
        functools.partial(_ffn_kernel, n_groups=n_groups, emit_w16=emit_w16,
                          head_tile=head_tile is not None),
        grid=(n_rows, dff // tf),
        in_specs=in_specs,
        out_specs=out_specs,
        out_shape=out_shape,
        scratch_shapes=[pltpu.VMEM((tm, d), BF16) for _, tm in groups],
        compiler_params=_params(2, vmem_mib),
        name=name,
    )(*args)
    return out if len(out) > 1 else out[0]


RG_GATE_LOOKAHEAD = 2


def _rg_gates(xr, wgate_ref, ba_ref, bx_ref, lam_ref, a_ref, b_ref, side=None):
    blocks = [slice(n * RG_BLOCK, (n + 1) * RG_BLOCK) for n in range(RG_BLOCKS)]
    gates = {}

    def issue_gate(n):
        if n < RG_BLOCKS:
            gates[n] = _mm(xr[:, blocks[n]], wgate_ref[n])

    rows, score_phase, value_phase = side if side else (0, None, None)
    per = RG_BLOCKS // rows if rows else RG_BLOCKS
    scores = {}
    for n in range(RG_GATE_LOOKAHEAD):
        issue_gate(n)
    for n, sl in enumerate(blocks):
        if rows and n % per == 0:
            j = n // per
            scores[j] = score_phase(j)
            if j > 0:
                value_phase(j - 1, scores.pop(j - 1))
        xn = xr[:, sl]
        g = gates.pop(n)
        issue_gate(n + RG_GATE_LOOKAHEAD)
        r = _sigmoid(g[:, :RG_BLOCK] + ba_ref[:, sl])
        ig = _sigmoid(g[:, RG_BLOCK:] + bx_ref[:, sl])
        a = jnp.exp(r * (-RG_C * _softplus(-lam_ref[:, sl])))
        a_ref[:, sl] = a
        mult = _sqrt_nonneg(jnp.maximum(1.0 - a * a, 0.0))
        b_ref[:, sl] = mult * (ig * xn)
    if rows:
        value_phase(rows - 1, scores.pop(rows - 1))


def _mlstm_chunk(q, k, v, o_gate, li_col, li_row, lf_row, bcum_row, causal, g, c_state, n_state, m):
    cs = q.shape[0]
    bcum_col = jnp.sum(jnp.where(causal, lf_row, 0.0), axis=1, keepdims=True)
    log_d = jnp.where(causal, bcum_col - bcum_row + li_row, NEG)
    inter = bcum_col + m
    m_t = jnp.maximum(inter, jnp.max(log_d, axis=1, keepdims=True))
    dmat = jnp.exp(log_d - m_t)
    sc = jnp.exp(inter - m_t)
    qb = q.astype(BF16)
    kb = k.astype(BF16)
    vb = v.astype(BF16)
    qk = lax.dot_general(qb, kb, (((1,), (1,)), ((), ())), preferred_element_type=F32) * dmat
    num = sc * jnp.dot(qb, c_state.astype(BF16), preferred_element_type=F32) + _mm(qk, vb)
    den = sc * jnp.sum(q * n_state, axis=1, keepdims=True) + jnp.sum(qk, axis=1, keepdims=True)
    den = jnp.maximum(jnp.abs(den), jnp.exp(-m_t))
    hh = num / den
    m_new = m_t[cs - 1:cs, :]
    b_last = bcum_col[cs - 1:cs, :]
    w_end = jnp.exp(b_last - bcum_col + li_col - m_new)
    dec = jnp.exp(b_last + m - m_new)
    wk = w_end * k
    c_new = dec * c_state + lax.dot_general(
        wk.astype(BF16), vb, (((0,), (0,)), ((), ())), preferred_element_type=F32)
    n_new = dec * n_state + jnp.sum(wk, axis=0, keepdims=True)
    y = _rms(hh, g) * _sigmoid(o_gate)
    return y, c_new, n_new, m_new


N_RG_IN = 11
N_ML_IN = 8


def _mixer_seq_kernel(*refs, tl, side_rows):
    (zx_ref, zg_ref, conv0_ref, h0_ref, cw_ref, cb_ref, wgate_ref, ba_ref, bx_ref, lam_ref,
     gout_ref) = refs[:N_RG_IN]
    (bi_ref, bf_ref, zq_ref, zk_ref, zv_ref, zo_ref, gates_ref,
     gml_ref) = refs[N_RG_IN:N_RG_IN + N_ML_IN]
    n_in = N_RG_IN + N_ML_IN + (3 if side_rows else 0)
    if side_rows:
        sq_ref, sk_ref, sv_ref = refs[N_RG_IN + N_ML_IN:n_in]
    y_ref, hlast_ref, convn_ref, yml_ref, cout_ref, nout_ref, mout_ref = refs[n_in:n_in + 7]
    so_ref = refs[n_in + 7] if side_rows else None
    xe_ref, a_ref, b_ref, h_ref, hc_ref, lf_ref, cst_ref, nst_ref, mst_ref = refs[-9:]
    t = pl.program_id(1)
    pad = V7X_SUBLANES

    @pl.when(t == 0)
    def _():
        xe_ref[pad - 3:pad, :] = conv0_ref[...]
        hc_ref[...] = h0_ref[...]
        cst_ref[...] = jnp.zeros_like(cst_ref)
        nst_ref[...] = jnp.zeros_like(nst_ref)
        mst_ref[...] = jnp.zeros_like(mst_ref)
        lf_ref[...] = jnp.zeros_like(lf_ref)

    @pl.when(t > 0)
    def _():
        xe_ref[pad - 3:pad, :] = xe_ref[tl + pad - 3:tl + pad, :]

    x = zx_ref[...]
    xe_ref[pad:tl + pad, :] = x
    xr = (xe_ref[pad - 3:tl + pad - 3, :] * cw_ref[0:1, :]
          + xe_ref[pad - 2:tl + pad - 2, :] * cw_ref[1:2, :]
          + xe_ref[pad - 1:tl + pad - 1, :] * cw_ref[2:3, :]
          + x * cw_ref[3:4, :]) + cb_ref[...]
    side = None
    if side_rows:
        own = _xattn_own_mask()
        side = (side_rows,
                lambda j: _xattn_probs(sq_ref, sk_ref, j, own),
                lambda j, p: _xattn_values(sv_ref, so_ref, j, p))

    _rg_gates(xr, wgate_ref, ba_ref, bx_ref, lam_ref, a_ref, b_ref, side=side)

    gates = gates_ref[...]
    gates_t = gates.T
    for h in range(ML_HEADS):
        lf_ref[h:h + 1, :] = _log_sigmoid(gates_t[ML_HEADS + h:ML_HEADS + h + 1, :] + bf_ref[h])
    lf_rows = lf_ref[...]
    bcum_rows = _exact_cumsum_lanes(lf_rows)
    causal = (lax.broadcasted_iota(jnp.int32, (tl, tl), 1)
              <= lax.broadcasted_iota(jnp.int32, (tl, tl), 0))
    for h in range(ML_HEADS):
        kq = slice(h * ML_DK, (h + 1) * ML_DK)
        vo = slice(h * ML_DV, (h + 1) * ML_DV)
        y_h, c_new, n_new, m_new = _mlstm_chunk(
            zq_ref[:, kq], zk_ref[:, kq] * (ML_DK ** -0.5), zv_ref[:, vo], zo_ref[:, vo],
            gates[:, h:h + 1] + bi_ref[h], gates_t[h:h + 1, :] + bi_ref[h],
            lf_rows[h:h + 1, :], bcum_rows[h:h + 1, :], causal, gml_ref[:, vo],
            cst_ref[h], nst_ref[h:h + 1, :], mst_ref[h:h + 1, 0:1])
        cst_ref[h] = c_new
        nst_ref[h:h + 1, :] = n_new
        mst_ref[h:h + 1, :] = jnp.broadcast_to(m_new, (1, V7X_LANES))
        yml_ref[:, vo] = y_h.astype(yml_ref.dtype)
    cout_ref[...] = cst_ref[...]
    nout_ref[...] = nst_ref[...]
    mout_ref[...] = mst_ref[...]

    row = lax.broadcasted_iota(jnp.int32, (V7X_SUBLANES, RG_WIDTH), 0)

    def group(gi, hc):
        r0 = pl.multiple_of(gi * V7X_SUBLANES, V7X_SUBLANES)
        a8 = a_ref[pl.ds(r0, V7X_SUBLANES), :]
        b8 = b_ref[pl.ds(r0, V7X_SUBLANES), :]
        for d in (1, 2, 4):
            keep = row >= d
            b8 = jnp.where(keep, a8 * pltpu.roll(b8, d, axis=0) + b8, b8)
            a8 = jnp.where(keep, a8 * pltpu.roll(a8, d, axis=0), a8)
        h8 = a8 * hc + b8
        h_ref[pl.ds(r0, V7X_SUBLANES), :] = h8
        return h8[V7X_SUBLANES - 1:V7X_SUBLANES, :]

    hc = lax.fori_loop(0, tl // V7X_SUBLANES, group, hc_ref[...], unroll=4)
    hc_ref[...] = hc
    hlast_ref[...] = hc
    convn_ref[...] = xe_ref[tl + pad - 3:tl + pad, :]
    y = h_ref[...] * _gelu_tanh(zg_ref[...])
    y_ref[...] = _rms(y, gout_ref[...]).astype(y_ref.dtype)


def _mixer_seq(z, zgates, conv0, h0, cw, cb, wgate, ba, bx, lam, gout, b_i, b_f, g_ml, *,
               batch, seq, tl, name, side_xattn=None):
    nt = seq // tl
    assert seq % tl == 0 and zgates.shape[1] == V7X_LANES
    w = RG_WIDTH
    dk, dv, nh = ML_DK, ML_DV, ML_HEADS
    row = lambda v: v.reshape(1, w)
    const2 = lambda b, t: (0, 0)
    smem = pl.BlockSpec(memory_space=pltpu.SMEM)
    tile = lambda width, off: pl.BlockSpec((tl, width), lambda b, t: (b * nt + t, off // width))
    in_specs = [
        tile(w, OFF_RGX),
        tile(w, OFF_RGG),
        pl.BlockSpec((None, CONV_W - 1, w), lambda b, t: (b, 0, 0)),
        pl.BlockSpec((None, 1, w), lambda b, t: (b, 0, 0)),
        pl.BlockSpec((CONV_W, w), const2),
        pl.BlockSpec((1, w), const2),
        pl.BlockSpec((RG_BLOCKS, RG_BLOCK, 2 * RG_BLOCK), lambda b, t: (0, 0, 0)),
        pl.BlockSpec((1, w), const2),
        pl.BlockSpec((1, w), const2),
        pl.BlockSpec((1, w), const2),
        pl.BlockSpec((1, w), const2),
        smem, smem,
        tile(nh * dk, OFF_Q),
        tile(nh * dk, OFF_K),
        tile(nh * dv, OFF_V),
        tile(nh * dv, OFF_O),
        tile(V7X_LANES, 0),
        pl.BlockSpec((1, nh * dv), const2),
    ]
    assert len(in_specs) == N_RG_IN + N_ML_IN
    out_specs = [
        pl.BlockSpec((tl, w), lambda b, t: (b * nt + t, 0)),
        pl.BlockSpec((None, 1, w), lambda b, t: (b, 0, 0)),
        pl.BlockSpec((None, CONV_W - 1, w), lambda b, t: (b, 0, 0)),
        pl.BlockSpec((tl, nh * dv), lambda b, t: (b * nt + t, 0)),
        pl.BlockSpec((None, nh, dk, dv), lambda b, t: (b, 0, 0, 0)),
        pl.BlockSpec((None, nh, dk), lambda b, t: (b, 0, 0)),
        pl.BlockSpec((None, nh, V7X_LANES), lambda b, t: (b, 0, 0)),
    ]
    out_shape = [
        jax.ShapeDtypeStruct((batch * seq, w), BF16),
        jax.ShapeDtypeStruct((batch, 1, w), F32),
        jax.ShapeDtypeStruct((batch, CONV_W - 1, w), F32),
        jax.ShapeDtypeStruct((batch * seq, nh * dv), BF16),
        jax.ShapeDtypeStruct((batch, nh, dk, dv), F32),
        jax.ShapeDtypeStruct((batch, nh, dk), F32),
        jax.ShapeDtypeStruct((batch, nh, V7X_LANES), F32),
    ]
    args = [z, z, conv0, h0.reshape(batch, 1, w), cw, row(cb), wgate, row(ba), row(bx), row(lam),
            row(gout), b_i, b_f, z, z, z, z, zgates, g_ml.reshape(1, nh * dv)]
    side_rows = 0
    vmem_mib = 40
    if side_xattn is not None:
        xq, ck, cv = side_xattn
        nb = xq.shape[0]
        assert nb % (batch * nt) == 0 and ck.shape == (nb, N_MEM, XA_HEADS, XA_DH)
        side_rows = nb // (batch * nt)
        q_spec = pl.BlockSpec((side_rows, XA_HEADS, XA_DH), lambda b, t: (b * nt + t, 0, 0))
        cache_spec = pl.BlockSpec((side_rows, N_MEM, XA_HEADS, XA_DH),
                                  lambda b, t: (b * nt + t, 0, 0, 0))
        in_specs += [q_spec, cache_spec, cache_spec]
        out_specs.append(q_spec)
        out_shape.append(jax.ShapeDtypeStruct((nb, XA_HEADS, XA_DH), F32))
        args += [xq.reshape(nb, XA_HEADS, XA_DH), ck, cv]
        vmem_mib = 60
    return pl.pallas_call(
        functools.partial(_mixer_seq_kernel, tl=tl, side_rows=side_rows),
        grid=(batch, nt),
        in_specs=in_specs,
        out_specs=out_specs,
        out_shape=out_shape,
        scratch_shapes=[
            pltpu.VMEM((tl + V7X_SUBLANES, w), F32),
            pltpu.VMEM((tl, w), F32),
            pltpu.VMEM((tl, w), F32),
            pltpu.VMEM((tl, w), F32),
            pltpu.VMEM((1, w), F32),
            pltpu.VMEM((V7X_SUBLANES, tl), F32),
            pltpu.VMEM((nh, dk, dv), F32),
            pltpu.VMEM((nh, dk), F32),
            pltpu.VMEM((nh, V7X_LANES), F32),
        ],
        compiler_params=_params(2, vmem_mib),
        name=name,
    )(*args)


def _rglru_step_kernel(zx_ref, zg_ref, conv_ref, h0_ref, cw_ref, cb_ref, wgate_ref, ba_ref,
                       bx_ref, lam_ref, gout_ref, y_ref, hn_ref, convn_ref, a_ref, b_ref):
    w = RG_WIDTH
    x = zx_ref[...]
    xr = (conv_ref[:, 0:w] * cw_ref[0:1, :] + conv_ref[:, w:2 * w] * cw_ref[1:2, :]
          + conv_ref[:, 2 * w:3 * w] * cw_ref[2:3, :] + x * cw_ref[3:4, :]) + cb_ref[...]
    _rg_gates(xr, wgate_ref, ba_ref, bx_ref, lam_ref, a_ref, b_ref)
    h = a_ref[...] * h0_ref[...] + b_ref[...]
    hn_ref[...] = h
    convn_ref[:, 0:2 * w] = conv_ref[:, w:3 * w]
    convn_ref[:, 2 * w:3 * w] = x
    y_ref[...] = _rms(h * _gelu_tanh(zg_ref[...]), gout_ref[...]).astype(y_ref.dtype)


def _rglru_step(z, conv, h0, cw, cb, wgate, ba, bx, lam, gout, *, name):
    nb = z.shape[0]
    w = RG_WIDTH
    row = lambda v: v.reshape(1, w)
    c0 = lambda i: (0, 0)
    return pl.pallas_call(
        _rglru_step_kernel,
        grid=(1,),
        in_specs=[
            pl.BlockSpec((nb, w), lambda i: (0, OFF_RGX // w)),
            pl.BlockSpec((nb, w), lambda i: (0, OFF_RGG // w)),
            pl.BlockSpec((nb, (CONV_W - 1) * w), c0),
            pl.BlockSpec((nb, w), c0),
            pl.BlockSpec((CONV_W, w), c0),
            pl.BlockSpec((1, w), c0),
            pl.BlockSpec((RG_BLOCKS, RG_BLOCK, 2 * RG_BLOCK), lambda i: (0, 0, 0)),
            pl.BlockSpec((1, w), c0),
            pl.BlockSpec((1, w), c0),
            pl.BlockSpec((1, w), c0),
            pl.BlockSpec((1, w), c0),
        ],
        out_specs=[
            pl.BlockSpec((nb, w), c0),
            pl.BlockSpec((nb, w), c0),
            pl.BlockSpec((nb, (CONV_W - 1) * w), c0),
        ],
        out_shape=[
            jax.ShapeDtypeStruct((nb, w), BF16),
            jax.ShapeDtypeStruct((nb, w), F32),
            jax.ShapeDtypeStruct((nb, (CONV_W - 1) * w), F32),
        ],
        scratch_shapes=[pltpu.VMEM((nb, w), F32), pltpu.VMEM((nb, w), F32)],
        compiler_params=_params(1, 32),
        name=name,
    )(z, z, conv.reshape(nb, (CONV_W - 1) * w), h0, cw, row(cb), wgate, row(ba), row(bx),
      row(lam), row(gout))


def _mlstm_step_body(in_refs, out_refs, scratch_refs, part, *, bs, parts):
    bi_ref, bf_ref, q_ref, k_ref, v_ref, o_ref, zg_ref, g_ref, c0_ref, n0_ref, m0_ref = in_refs
    y_ref, c_ref, n_ref, m_ref = out_refs
    qc_ref, = scratch_refs
    dk, dv = ML_DK, ML_DV
    nr = bs // parts
    rows = slice(part * nr, (part + 1) * nr)
    eye = (lax.broadcasted_iota(jnp.int32, (dk, dk), 0)
           == lax.broadcasted_iota(jnp.int32, (dk, dk), 1))

    def as_column(row):
        return jnp.sum(jnp.where(eye, jnp.broadcast_to(row, (dk, dk)), 0.0), axis=1, keepdims=True)

    zg = zg_ref[rows, :]
    for h in range(ML_HEADS):
        li = zg[:, h:h + 1] + bi_ref[h]
        lf = _log_sigmoid(zg[:, ML_HEADS + h:ML_HEADS + h + 1] + bf_ref[h])
        m = m0_ref[rows, h:h + 1]
        inter = lf + m
        m_t = jnp.maximum(inter, li)
        dgate = jnp.exp(li - m_t)
        sc = jnp.exp(inter - m_t)
        q = q_ref[rows, h * dk:(h + 1) * dk]
        k = k_ref[rows, h * dk:(h + 1) * dk] * (ML_DK ** -0.5)
        v = v_ref[rows, h * dv:(h + 1) * dv]
        n_old = n0_ref[rows, h, :]
        qk = jnp.sum(q * k, axis=1, keepdims=True) * dgate
        w_end = jnp.exp(li - m_t)
        dec = jnp.exp(inter - m_t)
        wk = w_end * k
        for j in range(nr):
            c_old = c0_ref[part * nr + j, h]
            qc_ref[j:j + 1, :] = jnp.sum(as_column(q[j:j + 1, :]) * c_old, axis=0, keepdims=True)
            c_ref[part * nr + j, h] = (dec[j:j + 1, :] * c_old
                                       + as_column(wk[j:j + 1, :]) * v[j:j + 1, :])
        num = sc * qc_ref[0:nr, :] + qk * v
        den = sc * jnp.sum(q * n_old, axis=1, keepdims=True) + qk
        den = jnp.maximum(jnp.abs(den), jnp.exp(-m_t))
        hh = num / den
        n_ref[rows, h, :] = dec * n_old + wk
        m_ref[rows, h:h + 1] = m_t
        y = (_rms(hh, g_ref[:, h * dv:(h + 1) * dv])
             * _sigmoid(o_ref[rows, h * dv:(h + 1) * dv]))
        y_ref[rows, h * dv:(h + 1) * dv] = y.astype(y_ref.dtype)


def _mlstm_step_job(z, zg, c0, n0, m0, b_i, b_f, g_out, *, bs, parts):
    nb = z.shape[0]
    assert nb % bs == 0 and bs % parts == 0
    dk, dv, nh = ML_DK, ML_DV, ML_HEADS
    smem = pl.BlockSpec(memory_space=pltpu.SMEM)

    def in_specs(blk):
        return [
            smem, smem,
            pl.BlockSpec((bs, nh * dk), lambda *g: (blk(*g), OFF_Q // (nh * dk))),
            pl.BlockSpec((bs, nh * dk), lambda *g: (blk(*g), OFF_K // (nh * dk))),
            pl.BlockSpec((bs, nh * dv), lambda *g: (blk(*g), OFF_V // (nh * dv))),
            pl.BlockSpec((bs, nh * dv), lambda *g: (blk(*g), OFF_O // (nh * dv))),
            pl.BlockSpec((bs, V7X_LANES), lambda *g: (blk(*g), 0)),
            pl.BlockSpec((1, nh * dv), lambda *g: (0, 0)),
            pl.BlockSpec((bs, nh, dk, dv), lambda *g: (blk(*g), 0, 0, 0)),
            pl.BlockSpec((bs, nh, dk), lambda *g: (blk(*g), 0, 0)),
            pl.BlockSpec((bs, nh), lambda *g: (blk(*g), 0)),
        ]

    def out_specs(blk):
        return [
            pl.BlockSpec((bs, nh * dv), lambda *g: (blk(*g), 0)),
            pl.BlockSpec((bs, nh, dk, dv), lambda *g: (blk(*g), 0, 0, 0)),
            pl.BlockSpec((bs, nh, dk), lambda *g: (blk(*g), 0, 0)),
            pl.BlockSpec((bs, nh), lambda *g: (blk(*g), 0)),
        ]

    return SideJob(
        n_blocks=nb // bs,
        parts=parts,
        args=(b_i, b_f, z, z, z, z, zg, g_out.reshape(1, nh * dv), c0, n0, m0),
        in_specs=in_specs,
        out_shape=(
            jax.ShapeDtypeStruct((nb, nh * dv), F32),
            jax.ShapeDtypeStruct((nb, nh, dk, dv), F32),
            jax.ShapeDtypeStruct((nb, nh, dk), F32),
            jax.ShapeDtypeStruct((nb, nh), F32),
        ),
        out_specs=out_specs,
        scratch_shapes=(pltpu.VMEM((bs, dv), F32),),
        body=functools.partial(_mlstm_step_body, bs=bs, parts=parts),
    )


def _run_side_job(job, *, vmem_mib, name):
    n_in, n_out = len(job.args), len(job.out_shape)

    def body(*refs):
        for part in range(job.parts):
            job.body(refs[:n_in], refs[n_in:n_in + n_out], refs[n_in + n_out:], part)

    block_of = lambda i: i
    return pl.pallas_call(
        body,
        grid=(job.n_blocks,),
        in_specs=job.in_specs(block_of),
        out_specs=job.out_specs(block_of),
        out_shape=list(job.out_shape),
        scratch_shapes=list(job.scratch_shapes),
        compiler_params=_params(1, vmem_mib),
        name=name,
    )(*job.args)


def _softmax_rows(s):
    e = jnp.exp(s - jnp.max(s, axis=-1, keepdims=True))
    return e / jnp.sum(e, axis=-1, keepdims=True)


def _xattn_block_kernel(x_ref, g_ref, wq_ref, k_ref, v_ref, wo_ref, o_ref):
    x = x_ref[...]
    xq = jnp.dot(_rms(x, g_ref[...]).astype(BF16), wq_ref[...],
                 preferred_element_type=F32).astype(BF16)
    heads = []
    for h in range(XA_HEADS):
        sl = slice(h * XA_DH, (h + 1) * XA_DH)
        s = lax.dot_general(xq[:, sl], k_ref[:, sl].astype(BF16),
                            (((1,), (1,)), ((), ())), preferred_element_type=F32)
        p = _softmax_rows(s * (XA_DH ** -0.5))
        heads.append(_mm(p, v_ref[:, sl]).astype(BF16))
    o_ref[...] = x + jnp.dot(jnp.concatenate(heads, axis=1), wo_ref[...],
                             preferred_element_type=F32)


def _xattn_block(x, g, wq16, mk, mv, wo16, *, batch, seq, tq, name):
    nt = seq // tq
    assert seq % tq == 0 and wq16.dtype == BF16 and wo16.dtype == BF16
    d = D_MODEL
    resident = dict(pipeline_mode=pl.Buffered(1))
    return pl.pallas_call(
        _xattn_block_kernel,
        grid=(batch, nt),
        in_specs=[
            pl.BlockSpec((tq, d), lambda b, t: (b * nt + t, 0)),
            pl.BlockSpec((1, d), lambda b, t: (0, 0)),
            pl.BlockSpec((d, d), lambda b, t: (0, 0), **resident),
            pl.BlockSpec((N_MEM, d), lambda b, t: (b, 0)),
            pl.BlockSpec((N_MEM, d), lambda b, t: (b, 0)),
            pl.BlockSpec((d, d), lambda b, t: (0, 0), **resident),
        ],
        out_specs=pl.BlockSpec((tq, d), lambda b, t: (b * nt + t, 0)),
        out_shape=jax.ShapeDtypeStruct((batch * seq, d), F32),
        compiler_params=_params(2, 56),
        name=name,
    )(x, g.reshape(1, d), wq16, mk, mv, wo16)


def _xattn_own_mask():
    nrow = N_MEM * XA_HEADS
    col_head = lax.broadcasted_iota(jnp.int32, (V7X_SUBLANES, nrow), 1) & (XA_HEADS - 1)
    row_head = lax.broadcasted_iota(jnp.int32, (V7X_SUBLANES, nrow), 0) & (XA_HEADS - 1)
    return col_head == row_head


def _xattn_probs(q_ref, k_ref, j, own):
    kf = k_ref[j].reshape(N_MEM * XA_HEADS, XA_DH)
    q8 = jnp.concatenate([q_ref[j]] * (V7X_SUBLANES // XA_HEADS), axis=0)
    s = lax.dot_general(q8.astype(BF16), kf.astype(BF16), (((1,), (1,)), ((), ())),
                        preferred_element_type=F32)
    s = jnp.where(own, s * (XA_DH ** -0.5), NEG)
    e = jnp.where(own, jnp.exp(s - jnp.max(s, axis=-1, keepdims=True)), 0.0)
    return e / jnp.sum(e, axis=-1, keepdims=True)


def _xattn_values(v_ref, o_ref, j, p):
    vf = v_ref[j].reshape(N_MEM * XA_HEADS, XA_DH)
    o_ref[j] = _mm(p, vf)[0:XA_HEADS, :]


def _xattn_step_kernel(q_ref, k_ref, v_ref, o_ref, *, sb):
    own = _xattn_own_mask()
    for j in range(sb):
        _xattn_values(v_ref, o_ref, j, _xattn_probs(q_ref, k_ref, j, own))


def _xattn_step(xq, ck, cv, *, sb, name):
    nb = xq.shape[0]
    assert XA_HEADS & (XA_HEADS - 1) == 0
    assert nb % sb == 0 and ck.shape == (nb, N_MEM, XA_HEADS, XA_DH)
    cache_spec = pl.BlockSpec((sb, N_MEM, XA_HEADS, XA_DH), lambda i: (i, 0, 0, 0))
    q_spec = pl.BlockSpec((sb, XA_HEADS, XA_DH), lambda i: (i, 0, 0))
    return pl.pallas_call(
        functools.partial(_xattn_step_kernel, sb=sb),
        grid=(nb // sb,),
        in_specs=[q_spec, cache_spec, cache_spec],
        out_specs=q_spec,
        out_shape=jax.ShapeDtypeStruct((nb, XA_HEADS, XA_DH), F32),
        compiler_params=_params(1, 40),
        name=name,
    )(xq.reshape(nb, XA_HEADS, XA_DH), ck, cv).reshape(nb, XA_HEADS * XA_DH)


def _tiles(rows):
    tm = min(rows, 1024)
    assert rows % tm == 0
    return tm


def kernel(x_prompt, x_sample, mem_prompt, state_rg_h, state_rg_conv, state_ml_C, state_ml_n, state_ml_m, cache_mem_k, cache_mem_v, g_mix, w_in, conv_w, conv_b, w_rg_a, b_rg_a, w_rg_x, b_rg_x, rg_lambda, b_ml_i, b_ml_f, g_rg_out, g_ml_out, w_out, g_xa, g_mem, w_xa_q, w_xa_k, w_xa_v, w_xa_o, g_ffn, w_ffn_gate, w_ffn_up, w_ffn_down, g_final):
    depth = g_mix.shape[0]
    assert depth == 1, "single trunk layer"
    bp, seq, d = x_prompt.shape
    bs_, dec_seq, _ = x_sample.shape
    assert d == D_MODEL and dec_seq == 1
    n_mem = mem_prompt.shape[1]
    assert n_mem == N_MEM
    dff = w_ffn_gate.shape[-1]
    in_w = w_in.shape[-1]
    assert in_w == IN_MAIN + N_GATE

    w_in_t = jnp.swapaxes(w_in, 1, 2).reshape(in_w, d)
    w_gate_pad = jnp.pad(w_in_t[IN_MAIN:], ((0, V7X_LANES - N_GATE), (0, 0)))
    cw = conv_w.reshape(CONV_W, RG_WIDTH)
    wgate = jnp.concatenate([w_rg_a.reshape(RG_BLOCKS, RG_BLOCK, RG_BLOCK),
                             w_rg_x.reshape(RG_BLOCKS, RG_BLOCK, RG_BLOCK)], axis=-1)
    rg_args = (cw, conv_b.reshape(-1), wgate, b_rg_a.reshape(-1), b_rg_x.reshape(-1),
               rg_lambda.reshape(-1), g_rg_out.reshape(-1))
    b_i = b_ml_i.reshape(ML_HEADS)
    b_f = b_ml_f.reshape(ML_HEADS)
    g_ml = g_ml_out.reshape(-1)
    w_out2 = w_out.reshape(d, d)
    w_q = w_xa_q.reshape(d, d)
    w_k = w_xa_k.reshape(d, d)
    w_v = w_xa_v.reshape(d, d)
    w_o = w_xa_o.reshape(d, d)
    w_fg = w_ffn_gate.reshape(d, dff)
    w_fu = w_ffn_up.reshape(d, dff)
    w_fd = w_ffn_down.reshape(dff, d)

    row_tile = _tiles(bp * seq)
    rg_zero_conv = jnp.zeros((bp, CONV_W - 1, RG_WIDTH), F32)
    rg_zero_h = jnp.zeros((bp, RG_WIDTH), F32)

    xs = x_sample.reshape(bs_, d)
    z_s, zg_s, w_in16 = _norm_linear(
        xs, g_mix.reshape(-1), w_in_t, n_out=IN_MAIN, tm=bs_, tn=1024, w_gate=w_gate_pad,
        w_is_nk=True, emit_w16=True, name="in_proj_s")
    y_rg_s, s_h, s_conv = _rglru_step(
        z_s, state_rg_conv.reshape(bs_, CONV_W - 1, RG_WIDTH), state_rg_h.reshape(bs_, RG_WIDTH),
        *rg_args, name="rglru_s")
    mlstm_s_job = _mlstm_step_job(
        z_s, zg_s, state_ml_C.reshape(bs_, ML_HEADS, ML_DK, ML_DV),
        state_ml_n.reshape(bs_, ML_HEADS, ML_DK), state_ml_m.reshape(bs_, ML_HEADS),
        b_i, b_f, g_ml, bs=V7X_SUBLANES, parts=2)
    tp = bp * seq
    xp = x_prompt.reshape(tp, d)
    in_cols, in_rows = IN_MAIN // 1024, tp // row_tile
    if (mlstm_s_job.n_blocks % in_rows == 0
            and mlstm_s_job.n_blocks // in_rows * mlstm_s_job.parts < in_cols):
        z_p, zg_p, y_ml_s, s_c, s_n, s_m = _norm_linear(
            xp, g_mix.reshape(-1), w_in16, n_out=IN_MAIN, tm=row_tile, tn=1024, w_gate=w_gate_pad,
            w_is_nk=True, side=mlstm_s_job, vmem_mib=60, name="in_proj_p")
    else:
        z_p, zg_p = _norm_linear(xp, g_mix.reshape(-1), w_in16, n_out=IN_MAIN, tm=row_tile,
                                 tn=1024, w_gate=w_gate_pad, w_is_nk=True, name="in_proj_p")
        y_ml_s, s_c, s_n, s_m = _run_side_job(mlstm_s_job, vmem_mib=32, name="mlstm_s")
    x1_s, w_out16_rg, w_out16_ml = _linear_residual(
        [y_rg_s, y_ml_s], [(w_out2, 0), (w_out2, 1)], xs, tm=bs_, tn=1024, emit_w16=True,
        name="mix_out_s")
    xq_s, w_q16 = _norm_linear(x1_s, g_xa.reshape(-1), w_q, n_out=d, tm=bs_, tn=1024,
                               emit_w16=True, name="xa_q_s")
    ck = cache_mem_k.reshape(bs_, n_mem, XA_HEADS, XA_DH)
    cv = cache_mem_v.reshape(bs_, n_mem, XA_HEADS, XA_DH)

    mix_tl = min(seq, ML_CHUNK)
    mix_steps = bp * (seq // mix_tl)
    mix_args = (z_p, zg_p, rg_zero_conv, rg_zero_h, *rg_args, b_i, b_f, g_ml)
    if bs_ % mix_steps == 0 and RG_BLOCKS % (bs_ // mix_steps) == 0:
        y_rg_p, p_h, p_conv, y_ml_p, p_c, p_n, p_m, o_s = _mixer_seq(
            *mix_args, batch=bp, seq=seq, tl=mix_tl, side_xattn=(xq_s, ck, cv), name="mixer_p")
        o_s = o_s.reshape(bs_, d)
    else:
        y_rg_p, p_h, p_conv, y_ml_p, p_c, p_n, p_m = _mixer_seq(
            *mix_args, batch=bp, seq=seq, tl=mix_tl, name="mixer_p")
        o_s = _xattn_step(xq_s, ck, cv, sb=2, name="xattn_s")

    x3_s, w_o16 = _linear_residual([o_s], [(w_o, 0)], x1_s, tm=bs_, tn=1024, emit_w16=True,
                                   name="xa_out_s")

    mem2 = mem_prompt.reshape(bp * n_mem, d)
    tmem = min(bp * n_mem, 256)
    mk, mk_heads = _norm_linear(mem2, g_mem.reshape(-1), w_k, n_out=d, tm=tmem, tn=d,
                                heads_out=(XA_HEADS, XA_DH), name="mem_k")
    mv, mv_heads = _norm_linear(mem2, g_mem.reshape(-1), w_v, n_out=d, tm=tmem, tn=d,
                                heads_out=(XA_HEADS, XA_DH), name="mem_v")
    proj_tile = min(row_tile, 512)
    x1_p = _linear_residual([y_rg_p, y_ml_p], [(w_out16_rg, 0), (w_out16_ml, 0)], xp,
                            tm=proj_tile, tn=d, name="mix_out_p")
    x3_p = _xattn_block(x1_p, g_xa.reshape(-1), w_q16, mk, mv, w_o16, batch=bp, seq=seq,
                        tq=min(seq, proj_tile), name="xattn_p")
    y_s, y_p0, w_fg16, w_fu16, w_fd16 = _ffn(
        [(x3_s, bs_), (x3_p, row_tile)], g_ffn.reshape(-1), w_fg, w_fu, w_fd, g_final, n_rows=1,
        tf=256, emit_w16=True, vmem_mib=60, name="ffn_s")
    if tp // row_tile > 1:
        y_p = _ffn([(x3_p, row_tile)], g_ffn.reshape(-1), w_fg16, w_fu16, w_fd16, g_final,
                   n_rows=tp // row_tile, tf=512, head_tile=y_p0, vmem_mib=60, name="ffn_p")
    else:
        y_p = y_p0

    return (
        y_p.reshape(bp, seq, d),
        y_s.reshape(bs_, 1, d),
        p_h.reshape(1, bp, RG_WIDTH),
        p_conv.reshape(1, bp, CONV_W - 1, RG_WIDTH),
        p_c.reshape(1, bp, ML_HEADS, ML_DK, ML_DV),
        p_n.reshape(1, bp, ML_HEADS, ML_DK),
        p_m[:, :, 0].reshape(1, bp, ML_HEADS),
        mk_heads.reshape(1, bp, n_mem, XA_HEADS, XA_DH),
        mv_heads.reshape(1, bp, n_mem, XA_HEADS, XA_DH),
        s_h.reshape(1, bs_, RG_WIDTH),
        s_conv.reshape(1, bs_, CONV_W - 1, RG_WIDTH),
        s_c.reshape(1, bs_, ML_HEADS, ML_DK, ML_DV),
        s_n.reshape(1, bs_, ML_HEADS, ML_DK),
        s_m.reshape(1, bs_, ML_HEADS),
    )
```

```python
import functools
from typing import Callable, NamedTuple

import jax
import jax.numpy as jnp
from jax import lax
from jax.experimental import pallas as pl
from jax.experimental.pallas import tpu as pltpu

F32 = jnp.float32
BF16 = jnp.bfloat16

D_MODEL = 2048
RG_WIDTH = D_MODEL // 2
RG_BLOCKS = 8
RG_BLOCK = RG_WIDTH // RG_BLOCKS
CONV_W = 4
RG_C = 8.0
ML_HEADS = 4
ML_WIDTH = D_MODEL - RG_WIDTH
ML_DV = ML_WIDTH // ML_HEADS
ML_DK = ML_DV // 2
N_MEM = 256
XA_HEADS = 4
XA_DH = D_MODEL // XA_HEADS
EPS = 1e-6
NEG = -1e30

OFF_RGX = 0
OFF_RGG = OFF_RGX + RG_WIDTH
OFF_Q = OFF_RGG + RG_WIDTH
OFF_K = OFF_Q + ML_HEADS * ML_DK
OFF_V = OFF_K + ML_HEADS * ML_DK
OFF_O = OFF_V + ML_WIDTH
OFF_I = OFF_O + ML_WIDTH
IN_MAIN = OFF_I
N_GATE = 2 * ML_HEADS

V7X_LANES = 128
V7X_SUBLANES = 8
V7X_VMEM_BYTES = 64 * 2**20

ML_CHUNK = 256


class SideJob(NamedTuple):
    n_blocks: int
    parts: int
    args: tuple
    in_specs: Callable
    out_shape: tuple
    out_specs: Callable
    scratch_shapes: tuple
    body: Callable


def _params(n_axes, vmem_mib):
    assert vmem_mib * 2**20 <= V7X_VMEM_BYTES
    return pltpu.CompilerParams(
        dimension_semantics=("arbitrary",) * n_axes,
        vmem_limit_bytes=vmem_mib * 2**20,
    )


def _rms(x, g):
    ms = jnp.mean(x * x, axis=-1, keepdims=True)
    return x * lax.rsqrt(ms + EPS) * g


def _softplus(u):
    return jnp.maximum(u, 0.0) + jnp.log1p(jnp.exp(-jnp.abs(u)))


def _log_sigmoid(u):
    return -_softplus(-u)


def _sigmoid(u):
    return 0.5 * jnp.tanh(0.5 * u) + 0.5


def _gelu_tanh(x):
    c = 0.7978845608028654
    half_x = 0.5 * x
    return half_x + half_x * jnp.tanh(x * (c + (c * 0.044715) * (x * x)))


def _sqrt_nonneg(v):
    return jnp.where(v > 0.0, v * lax.rsqrt(v), 0.0)


def _mm(a, b):
    return jnp.dot(a.astype(BF16), b.astype(BF16), preferred_element_type=F32)


def _exact_cumsum_lanes(x):
    n = x.shape[-1]
    upper = jnp.where(lax.broadcasted_iota(jnp.int32, (n, n), 0)
                      <= lax.broadcasted_iota(jnp.int32, (n, n), 1), 1.0, 0.0).astype(BF16)
    hi = x.astype(BF16)
    rest = x - hi.astype(F32)
    mid = rest.astype(BF16)
    lo = (rest - mid.astype(F32)).astype(BF16)
    return (jnp.dot(hi, upper, preferred_element_type=F32)
            + jnp.dot(mid, upper, preferred_element_type=F32)
            + jnp.dot(lo, upper, preferred_element_type=F32))


def _dot_w(a, w_ref, w_is_nk):
    w = w_ref[...].astype(BF16)
    if w_is_nk:
        return lax.dot_general(a, w, (((1,), (1,)), ((), ())), preferred_element_type=F32)
    return jnp.dot(a, w, preferred_element_type=F32)


def _norm_linear_kernel(*refs, with_gate, w_is_nk, emit_w16, heads_out, side, side_cols):
    refs = list(refs)
    n_si, n_so, n_ss = ((len(side.args), len(side.out_shape), len(side.scratch_shapes))
                        if side else (0, 0, 0))
    n_in = 3 + with_gate + n_si
    n_out = 1 + bool(heads_out) + with_gate + emit_w16 + n_so
    x_ref, g_ref, w_ref = refs[:3]
    wg_ref = refs[3] if with_gate else None
    side_in = refs[3 + with_gate:n_in]
    outs = refs[n_in:n_in + n_out]
    xn_ref = refs[n_in + n_out]
    side_scratch = refs[n_in + n_out + 1:]
    o_ref = outs[0]
    oh_ref = outs[1] if heads_out else None
    og_ref = outs[1 + bool(heads_out)] if with_gate else None
    w16_ref = outs[n_out - n_so - 1] if emit_w16 else None
    side_out = outs[n_out - n_so:]
    col = pl.program_id(1)

    def column_tile(xn):
        acc = _dot_w(xn, w_ref, w_is_nk)
        o_ref[...] = acc.astype(o_ref.dtype)
        if heads_out:
            oh_ref[...] = acc.reshape(oh_ref.shape)
        if emit_w16:
            w16_ref[...] = w_ref[...].astype(BF16)

    @pl.when(col == 0)
    def _():
        xn = _rms(x_ref[...], g_ref[...]).astype(BF16)
        xn_ref[...] = xn
        if with_gate:
            og_ref[...] = _dot_w(xn, wg_ref, w_is_nk)
        column_tile(xn)

    if side:
        for part in range(side.parts):
            @pl.when((col > 0) & (col <= side_cols) & ((col - 1) % side.parts == part))
            def _(part=part):
                column_tile(xn_ref[...])
                side.body(side_in, side_out, side_scratch, part)

    @pl.when(col > (side_cols if side else 0))
    def _():
        column_tile(xn_ref[...])


def _norm_linear(x, g, w, *, n_out, tm, tn, out_dtype=F32, w_gate=None, w_is_nk=False,
                 emit_w16=False, heads_out=None, side=None, vmem_mib=48, name):
    m, k = x.shape
    k_ax, n_ax = (1, 0) if w_is_nk else (0, 1)
    assert m % tm == 0 and n_out % tn == 0 and w.shape[k_ax] == k and n_out <= w.shape[n_ax]
    assert not emit_w16 or m == tm
    assert heads_out is None or (tn == n_out and heads_out[0] * heads_out[1] == n_out)
    n_rows, n_cols = m // tm, n_out // tn
    with_gate = w_gate is not None
    w_mode = dict(pipeline_mode=pl.Buffered(1)) if tn == n_out else {}
    w_spec = (pl.BlockSpec((tn, k), lambda i, j: (j, 0), **w_mode) if w_is_nk
              else pl.BlockSpec((k, tn), lambda i, j: (0, j), **w_mode))
    in_specs = [
        pl.BlockSpec((tm, k), lambda i, j: (i, 0)),
        pl.BlockSpec((1, k), lambda i, j: (0, 0)),
        w_spec,
    ]
    out_shape = [jax.ShapeDtypeStruct((m, n_out), out_dtype)]
    out_specs = [pl.BlockSpec((tm, tn), lambda i, j: (i, j))]
    scratch_shapes = [pltpu.VMEM((tm, k), BF16)]
    args = [x, g.reshape(1, k), w]
    if heads_out:
        out_shape.append(jax.ShapeDtypeStruct((m, *heads_out), F32))
        out_specs.append(pl.BlockSpec((tm, *heads_out), lambda i, j: (i, 0, 0)))
    if with_gate:
        ng = w_gate.shape[n_ax]
        in_specs.append(pl.BlockSpec(w_gate.shape, lambda i, j: (0, 0)))
        out_shape.append(jax.ShapeDtypeStruct((m, ng), F32))
        out_specs.append(pl.BlockSpec((tm, ng), lambda i, j: (i, 0)))
        args.append(w_gate)
    if emit_w16:
        out_shape.append(jax.ShapeDtypeStruct((n_out, k) if w_is_nk else (k, n_out), BF16))
        out_specs.append(pl.BlockSpec((tn, k), lambda i, j: (j, 0)) if w_is_nk
                         else pl.BlockSpec((k, tn), lambda i, j: (0, j)))
    side_cols = 0
    if side:
        assert side.n_blocks % n_rows == 0
        blocks_per_row = side.n_blocks // n_rows
        side_cols = blocks_per_row * side.parts
        assert 0 < side_cols < n_cols
        block_of = lambda i, j: (i * blocks_per_row
                                 + jnp.clip(j - 1, 0, side_cols - 1) // side.parts)
        in_specs += side.in_specs(block_of)
        out_specs += side.out_specs(block_of)
        out_shape += list(side.out_shape)
        scratch_shapes += list(side.scratch_shapes)
        args += list(side.args)
    out = pl.pallas_call(
        functools.partial(_norm_linear_kernel, with_gate=with_gate, w_is_nk=w_is_nk,
                          emit_w16=emit_w16, heads_out=heads_out, side=side, side_cols=side_cols),
        grid=(n_rows, n_cols),
        in_specs=in_specs,
        out_specs=out_specs,
        out_shape=out_shape,
        scratch_shapes=scratch_shapes,
        compiler_params=_params(2, vmem_mib),
        name=name,
    )(*args)
    return out if len(out) > 1 else out[0]


def _linear_res_kernel(*refs, n_in, emit_w16):
    a_refs = refs[:n_in]
    w_refs = refs[n_in:2 * n_in]
    res_ref = refs[2 * n_in]
    o_ref = refs[2 * n_in + 1]
    acc = res_ref[...]
    for a_ref, w_ref in zip(a_refs, w_refs):
        acc = acc + _mm(a_ref[...], w_ref[...])
    o_ref[...] = acc
    if emit_w16:
        for w_ref, wc_ref in zip(w_refs, refs[2 * n_in + 2:]):
            wc_ref[...] = w_ref[...].astype(BF16)


def _linear_residual(parts, weights, res, *, tm, tn, emit_w16=False, vmem_mib=48, name):
    m, n = res.shape
    kp = parts[0].shape[1]
    assert all(p.shape == (m, kp) for p in parts) and len(weights) == len(parts)
    assert m % tm == 0 and n % tn == 0 and (not emit_w16 or m == tm)
    n_in = len(parts)
    in_specs = [pl.BlockSpec((tm, kp), lambda i, j: (i, 0)) for _ in parts]
    w_mode = dict(pipeline_mode=pl.Buffered(1)) if tn == n else {}
    in_specs += [pl.BlockSpec((kp, tn), lambda i, j, rb=rb: (rb, j), **w_mode)
                 for _, rb in weights]
    in_specs.append(pl.BlockSpec((tm, tn), lambda i, j: (i, j)))
    out_specs = [pl.BlockSpec((tm, tn), lambda i, j: (i, j))]
    out_shape = [jax.ShapeDtypeStruct((m, n), F32)]
    if emit_w16:
        out_specs += [pl.BlockSpec((kp, tn), lambda i, j: (0, j)) for _ in parts]
        out_shape += [jax.ShapeDtypeStruct((kp, n), BF16) for _ in parts]
    out = pl.pallas_call(
        functools.partial(_linear_res_kernel, n_in=n_in, emit_w16=emit_w16),
        grid=(m // tm, n // tn),
        in_specs=in_specs,
        out_specs=out_specs,
        out_shape=out_shape,
        compiler_params=_params(2, vmem_mib),
        name=name,
    )(*parts, *[w for w, _ in weights], res)
    return out if emit_w16 else out[0]


FFN_HEAD_SLABS = 8


def _ffn_kernel(*refs, n_groups, emit_w16, head_tile):
    x_refs = refs[:n_groups]
    g_ref, wg_ref, wu_ref, wd_ref, gf_ref = refs[n_groups:n_groups + 5]
    n_in = n_groups + 5 + bool(head_tile)
    head_ref = refs[n_in - 1] if head_tile else None
    o_refs = refs[n_in:n_in + n_groups]
    w16_refs = refs[n_in + n_groups:n_in + n_groups + 3] if emit_w16 else ()
    xf_refs = refs[-n_groups:]
    row = pl.program_id(0)
    f = pl.program_id(1)
    last = pl.num_programs(1) - 1
    computing = (row > 0) if head_tile else True

    def weights():
        wg = wg_ref[...].astype(BF16)
        wu = wu_ref[...].astype(BF16)
        wd = wd_ref[...].astype(BF16)
        for dst, val in zip(w16_refs, (wg, wu, wd)):
            dst[...] = val
        return wg, wu, wd

    def hidden_tile(xf, w):
        wg, wu, wd = w
        gate = jnp.dot(xf, wg, preferred_element_type=F32)
        up = jnp.dot(xf, wu, preferred_element_type=F32)
        hidden = (gate * _sigmoid(gate)) * up
        return jnp.dot(hidden.astype(BF16), wd, preferred_element_type=F32)

    if head_tile:
        slab = head_ref.shape[0]

        @pl.when((row == 0) & (f < FFN_HEAD_SLABS))
        def _():
            o_refs[0][pl.ds(pl.multiple_of(f * slab, slab), slab), :] = head_ref[...]

    @pl.when(computing & (f == 0))
    def _():
        w = weights()
        for x_ref, o_ref, xf_ref in zip(x_refs, o_refs, xf_refs):
            x = x_ref[...]
            xf = _rms(x, g_ref[...]).astype(BF16)
            xf_ref[...] = xf
            o_ref[...] = x + hidden_tile(xf, w)

    @pl.when(computing & (f > 0) & (f < last))
    def _():
        w = weights()
        for o_ref, xf_ref in zip(o_refs, xf_refs):
            o_ref[...] += hidden_tile(xf_ref[...], w)

    @pl.when(computing & (f == last))
    def _():
        w = weights()
        for o_ref, xf_ref in zip(o_refs, xf_refs):
            o_ref[...] = _rms(o_ref[...] + hidden_tile(xf_ref[...], w), gf_ref[...])


def _ffn(groups, g, w_gate, w_up, w_down, g_final, *, n_rows, tf, emit_w16=False, head_tile=None,
         vmem_mib, name):
    d = groups[0][0].shape[1]
    dff = w_gate.shape[1]
    n_groups = len(groups)
    assert dff % tf == 0 and dff // tf >= 2 and (not emit_w16 or n_rows == 1)
    assert head_tile is None or (n_groups == 1 and head_tile.shape == (groups[0][1], d))
    assert all(x.shape[0] >= n_rows * tm for x, tm in groups)
    col = (lambda i, f: jnp.where(i > 0, f, 0)) if head_tile is not None else (lambda i, f: f)
    up_spec = pl.BlockSpec((d, tf), lambda i, f: (0, col(i, f)))
    down_spec = pl.BlockSpec((tf, d), lambda i, f: (col(i, f), 0))
    vec_spec = pl.BlockSpec((1, d), lambda i, f: (0, 0))
    row_specs = [pl.BlockSpec((tm, d), lambda i, f: (i, 0)) for _, tm in groups]
    in_specs = row_specs + [vec_spec, up_spec, up_spec, down_spec, vec_spec]
    args = [x for x, _ in groups] + [g.reshape(1, d), w_gate, w_up, w_down, g_final.reshape(1, d)]
    if head_tile is not None:
        tm0 = groups[0][1]
        assert tm0 % FFN_HEAD_SLABS == 0 and dff // tf >= FFN_HEAD_SLABS
        last_slab = FFN_HEAD_SLABS - 1
        in_specs.append(pl.BlockSpec(
            (tm0 // FFN_HEAD_SLABS, d),
            lambda i, f: (jnp.where(i == 0, jnp.minimum(f, last_slab), last_slab), 0)))
        args.append(head_tile)
    out_specs = list(row_specs)
    out_shape = [jax.ShapeDtypeStruct((n_rows * tm, d), F32) for _, tm in groups]
    if emit_w16:
        out_specs += [pl.BlockSpec((d, tf), lambda i, f: (0, f))] * 2
        out_specs += [pl.BlockSpec((tf, d), lambda i, f: (f, 0))]
        out_shape += [jax.ShapeDtypeStruct(w.shape, BF16) for w in (w_gate, w_up, w_down)]
    out = pl.pallas_call(
        functools.partial(_ffn_kernel, n_groups=n_groups, emit_w16=emit_w16,
                          head_tile=head_tile is not None),
        grid=(n_rows, dff // tf),
        in_specs=in_specs,
        out_specs=out_specs,
        out_shape=out_shape,
        scratch_shapes=[pltpu.VMEM((tm, d), BF16) for _, tm in groups],
        compiler_params=_params(2, vmem_mib),
        name=name,
    )(*args)
    return out if len(out) > 1 else out[0]


RG_GATE_LOOKAHEAD = 2


def _rg_gates(xr, wgate_ref, ba_ref, bx_ref, lam_ref, a_ref, b_ref, side=None):
    blocks = [slice(n * RG_BLOCK, (n + 1) * RG_BLOCK) for n in range(RG_BLOCKS)]
    gates = {}

    def issue_gate(n):
        if n < RG_BLOCKS:
            gates[n] = _mm(xr[:, blocks[n]], wgate_ref[n])

    rows, score_phase, value_phase = side if side else (0, None, None)
    per = RG_BLOCKS // rows if rows else RG_BLOCKS
    scores = {}
    for n in range(RG_GATE_LOOKAHEAD):
        issue_gate(n)
    for n, sl in enumerate(blocks):
        if rows and n % per == 0:
            j = n // per
            scores[j] = score_phase(j)
            if j > 0:
                value_phase(j - 1, scores.pop(j - 1))
        xn = xr[:, sl]
        g = gates.pop(n)
        issue_gate(n + RG_GATE_LOOKAHEAD)
        r = _sigmoid(g[:, :RG_BLOCK] + ba_ref[:, sl])
        ig = _sigmoid(g[:, RG_BLOCK:] + bx_ref[:, sl])
        a = jnp.exp(r * (-RG_C * _softplus(-lam_ref[:, sl])))
        a_ref[:, sl] = a
        mult = _sqrt_nonneg(jnp.maximum(1.0 - a * a, 0.0))
        b_ref[:, sl] = mult * (ig * xn)
    if rows:
        value_phase(rows - 1, scores.pop(rows - 1))


def _mlstm_chunk(q, k, v, o_gate, li_col, li_row, lf_row, bcum_row, causal, g, c_state, n_state, m):
    cs = q.shape[0]
    bcum_col = jnp.sum(jnp.where(causal, lf_row, 0.0), axis=1, keepdims=True)
    log_d = jnp.where(causal, bcum_col - bcum_row + li_row, NEG)
    inter = bcum_col + m
    m_t = jnp.maximum(inter, jnp.max(log_d, axis=1, keepdims=True))
    dmat = jnp.exp(log_d - m_t)
    sc = jnp.exp(inter - m_t)
    qb = q.astype(BF16)
    kb = k.astype(BF16)
    vb = v.astype(BF16)
    qk = lax.dot_general(qb, kb, (((1,), (1,)), ((), ())), preferred_element_type=F32) * dmat
    num = sc * jnp.dot(qb, c_state.astype(BF16), preferred_element_type=F32) + _mm(qk, vb)
    den = sc * jnp.sum(q * n_state, axis=1, keepdims=True) + jnp.sum(qk, axis=1, keepdims=True)
    den = jnp.maximum(jnp.abs(den), jnp.exp(-m_t))
    hh = num / den
    m_new = m_t[cs - 1:cs, :]
    b_last = bcum_col[cs - 1:cs, :]
    w_end = jnp.exp(b_last - bcum_col + li_col - m_new)
    dec = jnp.exp(b_last + m - m_new)
    wk = w_end * k
    c_new = dec * c_state + lax.dot_general(
        wk.astype(BF16), vb, (((0,), (0,)), ((), ())), preferred_element_type=F32)
    n_new = dec * n_state + jnp.sum(wk, axis=0, keepdims=True)
    y = _rms(hh, g) * _sigmoid(o_gate)
    return y, c_new, n_new, m_new


N_RG_IN = 10
N_ML_IN = 4


def _mixer_seq_kernel(*refs, tl, side_rows):
    (z_ref, conv0_ref, h0_ref, cw_ref, cb_ref, wgate_ref, ba_ref, bx_ref, lam_ref,
     gout_ref) = refs[:N_RG_IN]
    bi_ref, bf_ref, gates_ref, gml_ref = refs[N_RG_IN:N_RG_IN + N_ML_IN]
    n_in = N_RG_IN + N_ML_IN + (3 if side_rows else 0)
    if side_rows:
        sq_ref, sk_ref, sv_ref = refs[N_RG_IN + N_ML_IN:n_in]
    y_ref, hlast_ref, convn_ref, yml_ref, cout_ref, nout_ref, mout_ref = refs[n_in:n_in + 7]
    so_ref = refs[n_in + 7] if side_rows else None
    xe_ref, a_ref, b_ref, h_ref, hc_ref, lf_ref, cst_ref, nst_ref, mst_ref = refs[-9:]
    t = pl.program_id(1)
    pad = V7X_SUBLANES

    @pl.when(t == 0)
    def _():
        xe_ref[pad - 3:pad, :] = conv0_ref[...]
        hc_ref[...] = h0_ref[...]
        cst_ref[...] = jnp.zeros_like(cst_ref)
        nst_ref[...] = jnp.zeros_like(nst_ref)
        mst_ref[...] = jnp.zeros_like(mst_ref)
        lf_ref[...] = jnp.zeros_like(lf_ref)

    @pl.when(t > 0)
    def _():
        xe_ref[pad - 3:pad, :] = xe_ref[tl + pad - 3:tl + pad, :]

    x = z_ref[:, OFF_RGX:OFF_RGX + RG_WIDTH]
    xe_ref[pad:tl + pad, :] = x
    xr = (xe_ref[pad - 3:tl + pad - 3, :] * cw_ref[0:1, :]
          + xe_ref[pad - 2:tl + pad - 2, :] * cw_ref[1:2, :]
          + xe_ref[pad - 1:tl + pad - 1, :] * cw_ref[2:3, :]
          + x * cw_ref[3:4, :]) + cb_ref[...]
    side = None
    if side_rows:
        own = _xattn_own_mask()
        side = (side_rows,
                lambda j: _xattn_probs(sq_ref, sk_ref, j, own),
                lambda j, p: _xattn_values(sv_ref, so_ref, j, p))

    _rg_gates(xr, wgate_ref, ba_ref, bx_ref, lam_ref, a_ref, b_ref, side=side)

    gates = gates_ref[...]
    gates_t = gates.T
    for h in range(ML_HEADS):
        lf_ref[h:h + 1, :] = _log_sigmoid(gates_t[ML_HEADS + h:ML_HEADS + h + 1, :] + bf_ref[h])
    lf_rows = lf_ref[...]
    bcum_rows = _exact_cumsum_lanes(lf_rows)
    causal = (lax.broadcasted_iota(jnp.int32, (tl, tl), 1)
              <= lax.broadcasted_iota(jnp.int32, (tl, tl), 0))
    for h in range(ML_HEADS):
        vo = slice(h * ML_DV, (h + 1) * ML_DV)
        col = lambda off, width: slice(off + h * width, off + (h + 1) * width)
        y_h, c_new, n_new, m_new = _mlstm_chunk(
            z_ref[:, col(OFF_Q, ML_DK)], z_ref[:, col(OFF_K, ML_DK)] * (ML_DK ** -0.5),
            z_ref[:, col(OFF_V, ML_DV)], z_ref[:, col(OFF_O, ML_DV)],
            gates[:, h:h + 1] + bi_ref[h], gates_t[h:h + 1, :] + bi_ref[h],
            lf_rows[h:h + 1, :], bcum_rows[h:h + 1, :], causal, gml_ref[:, vo],
            cst_ref[h], nst_ref[h:h + 1, :], mst_ref[h:h + 1, 0:1])
        cst_ref[h] = c_new
        nst_ref[h:h + 1, :] = n_new
        mst_ref[h:h + 1, :] = jnp.broadcast_to(m_new, (1, V7X_LANES))
        yml_ref[:, vo] = y_h.astype(yml_ref.dtype)
    cout_ref[...] = cst_ref[...]
    nout_ref[...] = nst_ref[...]
    mout_ref[...] = mst_ref[...]

    row = lax.broadcasted_iota(jnp.int32, (V7X_SUBLANES, RG_WIDTH), 0)

    def group(gi, hc):
        r0 = pl.multiple_of(gi * V7X_SUBLANES, V7X_SUBLANES)
        a8 = a_ref[pl.ds(r0, V7X_SUBLANES), :]
        b8 = b_ref[pl.ds(r0, V7X_SUBLANES), :]
        for d in (1, 2, 4):
            keep = row >= d
            b8 = jnp.where(keep, a8 * pltpu.roll(b8, d, axis=0) + b8, b8)
            a8 = jnp.where(keep, a8 * pltpu.roll(a8, d, axis=0), a8)
        h8 = a8 * hc + b8
        h_ref[pl.ds(r0, V7X_SUBLANES), :] = h8
        return h8[V7X_SUBLANES - 1:V7X_SUBLANES, :]

    hc = lax.fori_loop(0, tl // V7X_SUBLANES, group, hc_ref[...], unroll=4)
    hc_ref[...] = hc
    hlast_ref[...] = hc
    convn_ref[...] = xe_ref[tl + pad - 3:tl + pad, :]
    y = h_ref[...] * _gelu_tanh(z_ref[:, OFF_RGG:OFF_RGG + RG_WIDTH])
    y_ref[...] = _rms(y, gout_ref[...]).astype(y_ref.dtype)


def _mixer_seq(z, zgates, conv0, h0, cw, cb, wgate, ba, bx, lam, gout, b_i, b_f, g_ml, *,
               batch, seq, tl, name, side_xattn=None):
    nt = seq // tl
    assert seq % tl == 0 and zgates.shape[1] == V7X_LANES
    w = RG_WIDTH
    dk, dv, nh = ML_DK, ML_DV, ML_HEADS
    row = lambda v: v.reshape(1, w)
    const2 = lambda b, t: (0, 0)
    smem = pl.BlockSpec(memory_space=pltpu.SMEM)
    tile = lambda width: pl.BlockSpec((tl, width), lambda b, t: (b * nt + t, 0))
    in_specs = [
        tile(IN_MAIN),
        pl.BlockSpec((None, CONV_W - 1, w), lambda b, t: (b, 0, 0)),
        pl.BlockSpec((None, 1, w), lambda b, t: (b, 0, 0)),
        pl.BlockSpec((CONV_W, w), const2),
        pl.BlockSpec((1, w), const2),
        pl.BlockSpec((RG_BLOCKS, RG_BLOCK, 2 * RG_BLOCK), lambda b, t: (0, 0, 0)),
        pl.BlockSpec((1, w), const2),
        pl.BlockSpec((1, w), const2),
        pl.BlockSpec((1, w), const2),
        pl.BlockSpec((1, w), const2),
        smem, smem,
        tile(V7X_LANES),
        pl.BlockSpec((1, nh * dv), const2),
    ]
    assert len(in_specs) == N_RG_IN + N_ML_IN and z.shape[1] == IN_MAIN
    out_specs = [
        pl.BlockSpec((tl, w), lambda b, t: (b * nt + t, 0)),
        pl.BlockSpec((None, 1, w), lambda b, t: (b, 0, 0)),
        pl.BlockSpec((None, CONV_W - 1, w), lambda b, t: (b, 0, 0)),
        pl.BlockSpec((tl, nh * dv), lambda b, t: (b * nt + t, 0)),
        pl.BlockSpec((None, nh, dk, dv), lambda b, t: (b, 0, 0, 0)),
        pl.BlockSpec((None, nh, dk), lambda b, t: (b, 0, 0)),
        pl.BlockSpec((None, nh, V7X_LANES), lambda b, t: (b, 0, 0)),
    ]
    out_shape = [
        jax.ShapeDtypeStruct((batch * seq, w), BF16),
        jax.ShapeDtypeStruct((batch, 1, w), F32),
        jax.ShapeDtypeStruct((batch, CONV_W - 1, w), F32),
        jax.ShapeDtypeStruct((batch * seq, nh * dv), BF16),
        jax.ShapeDtypeStruct((batch, nh, dk, dv), F32),
        jax.ShapeDtypeStruct((batch, nh, dk), F32),
        jax.ShapeDtypeStruct((batch, nh, V7X_LANES), F32),
    ]
    args = [z, conv0, h0.reshape(batch, 1, w), cw, row(cb), wgate, row(ba), row(bx), row(lam),
            row(gout), b_i, b_f, zgates, g_ml.reshape(1, nh * dv)]
    side_rows = 0
    vmem_mib = 40
    if side_xattn is not None:
        xq, ck, cv = side_xattn
        nb = xq.shape[0]
        assert nb % (batch * nt) == 0 and ck.shape == (nb, N_MEM, XA_HEADS, XA_DH)
        side_rows = nb // (batch * nt)
        q_spec = pl.BlockSpec((side_rows, XA_HEADS, XA_DH), lambda b, t: (b * nt + t, 0, 0))
        cache_spec = pl.BlockSpec((side_rows, N_MEM, XA_HEADS, XA_DH),
                                  lambda b, t: (b * nt + t, 0, 0, 0))
        in_specs += [q_spec, cache_spec, cache_spec]
        out_specs.append(q_spec)
        out_shape.append(jax.ShapeDtypeStruct((nb, XA_HEADS, XA_DH), F32))
        args += [xq.reshape(nb, XA_HEADS, XA_DH), ck, cv]
        vmem_mib = 60
    return pl.pallas_call(
        functools.partial(_mixer_seq_kernel, tl=tl, side_rows=side_rows),
        grid=(batch, nt),
        in_specs=in_specs,
        out_specs=out_specs,
        out_shape=out_shape,
        scratch_shapes=[
            pltpu.VMEM((tl + V7X_SUBLANES, w), F32),
            pltpu.VMEM((tl, w), F32),
            pltpu.VMEM((tl, w), F32),
            pltpu.VMEM((tl, w), F32),
            pltpu.VMEM((1, w), F32),
            pltpu.VMEM((V7X_SUBLANES, tl), F32),
            pltpu.VMEM((nh, dk, dv), F32),
            pltpu.VMEM((nh, dk), F32),
            pltpu.VMEM((nh, V7X_LANES), F32),
        ],
        compiler_params=_params(2, vmem_mib),
        name=name,
    )(*args)


def _rglru_step_kernel(zx_ref, zg_ref, conv_ref, h0_ref, cw_ref, cb_ref, wgate_ref, ba_ref,
                       bx_ref, lam_ref, gout_ref, y_ref, hn_ref, convn_ref, a_ref, b_ref):
    w = RG_WIDTH
    x = zx_ref[...]
    xr = (conv_ref[:, 0:w] * cw_ref[0:1, :] + conv_ref[:, w:2 * w] * cw_ref[1:2, :]
          + conv_ref[:, 2 * w:3 * w] * cw_ref[2:3, :] + x * cw_ref[3:4, :]) + cb_ref[...]
    _rg_gates(xr, wgate_ref, ba_ref, bx_ref, lam_ref, a_ref, b_ref)
    h = a_ref[...] * h0_ref[...] + b_ref[...]
    hn_ref[...] = h
    convn_ref[:, 0:2 * w] = conv_ref[:, w:3 * w]
    convn_ref[:, 2 * w:3 * w] = x
    y_ref[...] = _rms(h * _gelu_tanh(zg_ref[...]), gout_ref[...]).astype(y_ref.dtype)


def _rglru_step(z, conv, h0, cw, cb, wgate, ba, bx, lam, gout, *, name):
    nb = z.shape[0]
    w = RG_WIDTH
    row = lambda v: v.reshape(1, w)
    c0 = lambda i: (0, 0)
    return pl.pallas_call(
        _rglru_step_kernel,
        grid=(1,),
        in_specs=[
            pl.BlockSpec((nb, w), lambda i: (0, OFF_RGX // w)),
            pl.BlockSpec((nb, w), lambda i: (0, OFF_RGG // w)),
            pl.BlockSpec((nb, (CONV_W - 1) * w), c0),
            pl.BlockSpec((nb, w), c0),
            pl.BlockSpec((CONV_W, w), c0),
            pl.BlockSpec((1, w), c0),
            pl.BlockSpec((RG_BLOCKS, RG_BLOCK, 2 * RG_BLOCK), lambda i: (0, 0, 0)),
            pl.BlockSpec((1, w), c0),
            pl.BlockSpec((1, w), c0),
            pl.BlockSpec((1, w), c0),
            pl.BlockSpec((1, w), c0),
        ],
        out_specs=[
            pl.BlockSpec((nb, w), c0),
            pl.BlockSpec((nb, w), c0),
            pl.BlockSpec((nb, (CONV_W - 1) * w), c0),
        ],
        out_shape=[
            jax.ShapeDtypeStruct((nb, w), BF16),
            jax.ShapeDtypeStruct((nb, w), F32),
            jax.ShapeDtypeStruct((nb, (CONV_W - 1) * w), F32),
        ],
        scratch_shapes=[pltpu.VMEM((nb, w), F32), pltpu.VMEM((nb, w), F32)],
        compiler_params=_params(1, 32),
        name=name,
    )(z, z, conv.reshape(nb, (CONV_W - 1) * w), h0, cw, row(cb), wgate, row(ba), row(bx),
      row(lam), row(gout))


def _mlstm_step_body(in_refs, out_refs, scratch_refs, part, *, bs, parts):
    bi_ref, bf_ref, z_ref, zg_ref, g_ref, c0_ref, n0_ref, m0_ref = in_refs
    y_ref, c_ref, n_ref, m_ref = out_refs
    qc_ref, = scratch_refs
    dk, dv = ML_DK, ML_DV
    nr = bs // parts
    rows = slice(part * nr, (part + 1) * nr)
    eye = (lax.broadcasted_iota(jnp.int32, (dk, dk), 0)
           == lax.broadcasted_iota(jnp.int32, (dk, dk), 1))

    def as_column(row):
        return jnp.sum(jnp.where(eye, jnp.broadcast_to(row, (dk, dk)), 0.0), axis=1, keepdims=True)

    zg = zg_ref[rows, :]
    for h in range(ML_HEADS):
        li = zg[:, h:h + 1] + bi_ref[h]
        lf = _log_sigmoid(zg[:, ML_HEADS + h:ML_HEADS + h + 1] + bf_ref[h])
        m = m0_ref[rows, h:h + 1]
        inter = lf + m
        m_t = jnp.maximum(inter, li)
        dgate = jnp.exp(li - m_t)
        sc = jnp.exp(inter - m_t)
        q = z_ref[rows, OFF_Q + h * dk:OFF_Q + (h + 1) * dk]
        k = z_ref[rows, OFF_K + h * dk:OFF_K + (h + 1) * dk] * (ML_DK ** -0.5)
        v = z_ref[rows, OFF_V + h * dv:OFF_V + (h + 1) * dv]
        n_old = n0_ref[rows, h, :]
        qk = jnp.sum(q * k, axis=1, keepdims=True) * dgate
        w_end = jnp.exp(li - m_t)
        dec = jnp.exp(inter - m_t)
        wk = w_end * k
        for j in range(nr):
            c_old = c0_ref[part * nr + j, h]
            qc_ref[j:j + 1, :] = jnp.sum(as_column(q[j:j + 1, :]) * c_old, axis=0, keepdims=True)
            c_ref[part * nr + j, h] = (dec[j:j + 1, :] * c_old
                                       + as_column(wk[j:j + 1, :]) * v[j:j + 1, :])
        num = sc * qc_ref[0:nr, :] + qk * v
        den = sc * jnp.sum(q * n_old, axis=1, keepdims=True) + qk
        den = jnp.maximum(jnp.abs(den), jnp.exp(-m_t))
        hh = num / den
        n_ref[rows, h, :] = dec * n_old + wk
        m_ref[rows, h:h + 1] = m_t
        y = (_rms(hh, g_ref[:, h * dv:(h + 1) * dv])
             * _sigmoid(z_ref[rows, OFF_O + h * dv:OFF_O + (h + 1) * dv]))
        y_ref[rows, h * dv:(h + 1) * dv] = y.astype(y_ref.dtype)


def _mlstm_step_job(z, zg, c0, n0, m0, b_i, b_f, g_out, *, bs, parts):
    nb = z.shape[0]
    assert nb % bs == 0 and bs % parts == 0
    dk, dv, nh = ML_DK, ML_DV, ML_HEADS
    smem = pl.BlockSpec(memory_space=pltpu.SMEM)

    def in_specs(blk):
        return [
            smem, smem,
            pl.BlockSpec((bs, IN_MAIN), lambda *g: (blk(*g), 0)),
            pl.BlockSpec((bs, V7X_LANES), lambda *g: (blk(*g), 0)),
            pl.BlockSpec((1, nh * dv), lambda *g: (0, 0)),
            pl.BlockSpec((bs, nh, dk, dv), lambda *g: (blk(*g), 0, 0, 0)),
            pl.BlockSpec((bs, nh, dk), lambda *g: (blk(*g), 0, 0)),
            pl.BlockSpec((bs, nh), lambda *g: (blk(*g), 0)),
        ]

    def out_specs(blk):
        return [
            pl.BlockSpec((bs, nh * dv), lambda *g: (blk(*g), 0)),
            pl.BlockSpec((bs, nh, dk, dv), lambda *g: (blk(*g), 0, 0, 0)),
            pl.BlockSpec((bs, nh, dk), lambda *g: (blk(*g), 0, 0)),
            pl.BlockSpec((bs, nh), lambda *g: (blk(*g), 0)),
        ]

    return SideJob(
        n_blocks=nb // bs,
        parts=parts,
        args=(b_i, b_f, z, zg, g_out.reshape(1, nh * dv), c0, n0, m0),
        in_specs=in_specs,
        out_shape=(
            jax.ShapeDtypeStruct((nb, nh * dv), F32),
            jax.ShapeDtypeStruct((nb, nh, dk, dv), F32),
            jax.ShapeDtypeStruct((nb, nh, dk), F32),
            jax.ShapeDtypeStruct((nb, nh), F32),
        ),
        out_specs=out_specs,
        scratch_shapes=(pltpu.VMEM((bs, dv), F32),),
        body=functools.partial(_mlstm_step_body, bs=bs, parts=parts),
    )


def _run_side_job(job, *, vmem_mib, name):
    n_in, n_out = len(job.args), len(job.out_shape)

    def body(*refs):
        for part in range(job.parts):
            job.body(refs[:n_in], refs[n_in:n_in + n_out], refs[n_in + n_out:], part)

    block_of = lambda i: i
    return pl.pallas_call(
        body,
        grid=(job.n_blocks,),
        in_specs=job.in_specs(block_of),
        out_specs=job.out_specs(block_of),
        out_shape=list(job.out_shape),
        scratch_shapes=list(job.scratch_shapes),
        compiler_params=_params(1, vmem_mib),
        name=name,
    )(*job.args)


def _softmax_rows(s):
    e = jnp.exp(s - jnp.max(s, axis=-1, keepdims=True))
    return e / jnp.sum(e, axis=-1, keepdims=True)


def _xattn_block_kernel(*refs, n_mix):
    x_ref = refs[0]
    y_refs = refs[1:1 + n_mix]
    wy_refs = refs[1 + n_mix:1 + 2 * n_mix]
    g_ref, wq_ref, k_ref, v_ref, wo_ref, o_ref = refs[1 + 2 * n_mix:]
    x = x_ref[...]
    for y_ref, wy_ref in zip(y_refs, wy_refs):
        x = x + jnp.dot(y_ref[...], wy_ref[...], preferred_element_type=F32)
    xq = jnp.dot(_rms(x, g_ref[...]).astype(BF16), wq_ref[...],
                 preferred_element_type=F32).astype(BF16)
    heads = []
    for h in range(XA_HEADS):
        sl = slice(h * XA_DH, (h + 1) * XA_DH)
        s = lax.dot_general(xq[:, sl], k_ref[:, sl].astype(BF16),
                            (((1,), (1,)), ((), ())), preferred_element_type=F32)
        p = _softmax_rows(s * (XA_DH ** -0.5))
        heads.append(_mm(p, v_ref[:, sl]).astype(BF16))
    o_ref[...] = x + jnp.dot(jnp.concatenate(heads, axis=1), wo_ref[...],
                             preferred_element_type=F32)


def _xattn_block(x, mix_parts, mix_weights16, g, wq16, mk, mv, wo16, *, batch, seq, tq, name):
    nt = seq // tq
    assert seq % tq == 0 and all(w.dtype == BF16 for w in (*mix_weights16, wq16, wo16))
    assert all(p.dtype == BF16 for p in mix_parts)
    d = D_MODEL
    resident = dict(pipeline_mode=pl.Buffered(1))
    rows = lambda width: pl.BlockSpec((tq, width), lambda b, t: (b * nt + t, 0))
    return pl.pallas_call(
        functools.partial(_xattn_block_kernel, n_mix=len(mix_parts)),
        grid=(batch, nt),
        in_specs=[
            rows(d),
            *[rows(p.shape[1]) for p in mix_parts],
            *[pl.BlockSpec(w.shape, lambda b, t: (0, 0), **resident) for w in mix_weights16],
            pl.BlockSpec((1, d), lambda b, t: (0, 0)),
            pl.BlockSpec((d, d), lambda b, t: (0, 0), **resident),
            pl.BlockSpec((N_MEM, d), lambda b, t: (b, 0)),
            pl.BlockSpec((N_MEM, d), lambda b, t: (b, 0)),
            pl.BlockSpec((d, d), lambda b, t: (0, 0), **resident),
        ],
        out_specs=rows(d),
        out_shape=jax.ShapeDtypeStruct((batch * seq, d), F32),
        compiler_params=_params(2, 60),
        name=name,
    )(x, *mix_parts, *mix_weights16, g.reshape(1, d), wq16, mk, mv, wo16)


def _xattn_own_mask():
    nrow = N_MEM * XA_HEADS
    col_head = lax.broadcasted_iota(jnp.int32, (V7X_SUBLANES, nrow), 1) & (XA_HEADS - 1)
    row_head = lax.broadcasted_iota(jnp.int32, (V7X_SUBLANES, nrow), 0) & (XA_HEADS - 1)
    return col_head == row_head


def _xattn_probs(q_ref, k_ref, j, own):
    kf = k_ref[j].reshape(N_MEM * XA_HEADS, XA_DH)
    q8 = jnp.concatenate([q_ref[j]] * (V7X_SUBLANES // XA_HEADS), axis=0)
    s = lax.dot_general(q8.astype(BF16), kf.astype(BF16), (((1,), (1,)), ((), ())),
                        preferred_element_type=F32)
    s = jnp.where(own, s * (XA_DH ** -0.5), NEG)
    e = jnp.where(own, jnp.exp(s - jnp.max(s, axis=-1, keepdims=True)), 0.0)
    return e / jnp.sum(e, axis=-1, keepdims=True)


def _xattn_values(v_ref, o_ref, j, p):
    vf = v_ref[j].reshape(N_MEM * XA_HEADS, XA_DH)
    o_ref[j] = _mm(p, vf)[0:XA_HEADS, :]


def _xattn_step_kernel(q_ref, k_ref, v_ref, o_ref, *, sb):
    own = _xattn_own_mask()
    for j in range(sb):
        _xattn_values(v_ref, o_ref, j, _xattn_probs(q_ref, k_ref, j, own))


def _xattn_step(xq, ck, cv, *, sb, name):
    nb = xq.shape[0]
    assert XA_HEADS & (XA_HEADS - 1) == 0
    assert nb % sb == 0 and ck.shape == (nb, N_MEM, XA_HEADS, XA_DH)
    cache_spec = pl.BlockSpec((sb, N_MEM, XA_HEADS, XA_DH), lambda i: (i, 0, 0, 0))
    q_spec = pl.BlockSpec((sb, XA_HEADS, XA_DH), lambda i: (i, 0, 0))
    return pl.pallas_call(
        functools.partial(_xattn_step_kernel, sb=sb),
        grid=(nb // sb,),
        in_specs=[q_spec, cache_spec, cache_spec],
        out_specs=q_spec,
        out_shape=jax.ShapeDtypeStruct((nb, XA_HEADS, XA_DH), F32),
        compiler_params=_params(1, 40),
        name=name,
    )(xq.reshape(nb, XA_HEADS, XA_DH), ck, cv).reshape(nb, XA_HEADS * XA_DH)


def _tiles(rows):
    tm = min(rows, 1024)
    assert rows % tm == 0
    return tm


def kernel(x_prompt, x_sample, mem_prompt, state_rg_h, state_rg_conv, state_ml_C, state_ml_n, state_ml_m, cache_mem_k, cache_mem_v, g_mix, w_in, conv_w, conv_b, w_rg_a, b_rg_a, w_rg_x, b_rg_x, rg_lambda, b_ml_i, b_ml_f, g_rg_out, g_ml_out, w_out, g_xa, g_mem, w_xa_q, w_xa_k, w_xa_v, w_xa_o, g_ffn, w_ffn_gate, w_ffn_up, w_ffn_down, g_final):
    depth = g_mix.shape[0]
    assert depth == 1, "single trunk layer"
    bp, seq, d = x_prompt.shape
    bs_, dec_seq, _ = x_sample.shape
    assert d == D_MODEL and dec_seq == 1
    n_mem = mem_prompt.shape[1]
    assert n_mem == N_MEM
    dff = w_ffn_gate.shape[-1]
    in_w = w_in.shape[-1]
    assert in_w == IN_MAIN + N_GATE

    w_in_t = jnp.swapaxes(w_in, 1, 2).reshape(in_w, d)
    w_gate_pad = jnp.pad(w_in_t[IN_MAIN:], ((0, V7X_LANES - N_GATE), (0, 0)))
    cw = conv_w.reshape(CONV_W, RG_WIDTH)
    wgate = jnp.concatenate([w_rg_a.reshape(RG_BLOCKS, RG_BLOCK, RG_BLOCK),
                             w_rg_x.reshape(RG_BLOCKS, RG_BLOCK, RG_BLOCK)], axis=-1)
    rg_args = (cw, conv_b.reshape(-1), wgate, b_rg_a.reshape(-1), b_rg_x.reshape(-1),
               rg_lambda.reshape(-1), g_rg_out.reshape(-1))
    b_i = b_ml_i.reshape(ML_HEADS)
    b_f = b_ml_f.reshape(ML_HEADS)
    g_ml = g_ml_out.reshape(-1)
    w_out2 = w_out.reshape(d, d)
    w_q = w_xa_q.reshape(d, d)
    w_k = w_xa_k.reshape(d, d)
    w_v = w_xa_v.reshape(d, d)
    w_o = w_xa_o.reshape(d, d)
    w_fg = w_ffn_gate.reshape(d, dff)
    w_fu = w_ffn_up.reshape(d, dff)
    w_fd = w_ffn_down.reshape(dff, d)

    row_tile = _tiles(bp * seq)
    rg_zero_conv = jnp.zeros((bp, CONV_W - 1, RG_WIDTH), F32)
    rg_zero_h = jnp.zeros((bp, RG_WIDTH), F32)

    xs = x_sample.reshape(bs_, d)
    z_s, zg_s, w_in16 = _norm_linear(
        xs, g_mix.reshape(-1), w_in_t, n_out=IN_MAIN, tm=bs_, tn=1024, w_gate=w_gate_pad,
        w_is_nk=True, emit_w16=True, name="in_proj_s")
    y_rg_s, s_h, s_conv = _rglru_step(
        z_s, state_rg_conv.reshape(bs_, CONV_W - 1, RG_WIDTH), state_rg_h.reshape(bs_, RG_WIDTH),
        *rg_args, name="rglru_s")
    mlstm_s_job = _mlstm_step_job(
        z_s, zg_s, state_ml_C.reshape(bs_, ML_HEADS, ML_DK, ML_DV),
        state_ml_n.reshape(bs_, ML_HEADS, ML_DK), state_ml_m.reshape(bs_, ML_HEADS),
        b_i, b_f, g_ml, bs=V7X_SUBLANES, parts=2)
    tp = bp * seq
    xp = x_prompt.reshape(tp, d)
    in_cols, in_rows = IN_MAIN // 1024, tp // row_tile
    if (mlstm_s_job.n_blocks % in_rows == 0
            and mlstm_s_job.n_blocks // in_rows * mlstm_s_job.parts < in_cols):
        z_p, zg_p, y_ml_s, s_c, s_n, s_m = _norm_linear(
            xp, g_mix.reshape(-1), w_in16, n_out=IN_MAIN, tm=row_tile, tn=1024, w_gate=w_gate_pad,
            w_is_nk=True, side=mlstm_s_job, vmem_mib=60, name="in_proj_p")
    else:
        z_p, zg_p = _norm_linear(xp, g_mix.reshape(-1), w_in16, n_out=IN_MAIN, tm=row_tile,
                                 tn=1024, w_gate=w_gate_pad, w_is_nk=True, name="in_proj_p")
        y_ml_s, s_c, s_n, s_m = _run_side_job(mlstm_s_job, vmem_mib=32, name="mlstm_s")
    x1_s, w_out16_rg, w_out16_ml = _linear_residual(
        [y_rg_s, y_ml_s], [(w_out2, 0), (w_out2, 1)], xs, tm=bs_, tn=1024, emit_w16=True,
        name="mix_out_s")
    xq_s, w_q16 = _norm_linear(x1_s, g_xa.reshape(-1), w_q, n_out=d, tm=bs_, tn=1024,
                               emit_w16=True, name="xa_q_s")
    ck = cache_mem_k.reshape(bs_, n_mem, XA_HEADS, XA_DH)
    cv = cache_mem_v.reshape(bs_, n_mem, XA_HEADS, XA_DH)

    mix_tl = min(seq, ML_CHUNK)
    mix_steps = bp * (seq // mix_tl)
    mix_args = (z_p, zg_p, rg_zero_conv, rg_zero_h, *rg_args, b_i, b_f, g_ml)
    if bs_ % mix_steps == 0 and RG_BLOCKS % (bs_ // mix_steps) == 0:
        y_rg_p, p_h, p_conv, y_ml_p, p_c, p_n, p_m, o_s = _mixer_seq(
            *mix_args, batch=bp, seq=seq, tl=mix_tl, side_xattn=(xq_s, ck, cv), name="mixer_p")
        o_s = o_s.reshape(bs_, d)
    else:
        y_rg_p, p_h, p_conv, y_ml_p, p_c, p_n, p_m = _mixer_seq(
            *mix_args, batch=bp, seq=seq, tl=mix_tl, name="mixer_p")
        o_s = _xattn_step(xq_s, ck, cv, sb=2, name="xattn_s")

    x3_s, w_o16 = _linear_residual([o_s], [(w_o, 0)], x1_s, tm=bs_, tn=1024, emit_w16=True,
                                   name="xa_out_s")

    mem2 = mem_prompt.reshape(bp * n_mem, d)
    tmem = min(bp * n_mem, 256)
    mk, mk_heads = _norm_linear(mem2, g_mem.reshape(-1), w_k, n_out=d, tm=tmem, tn=d,
                                out_dtype=BF16, heads_out=(XA_HEADS, XA_DH), name="mem_k")
    mv, mv_heads = _norm_linear(mem2, g_mem.reshape(-1), w_v, n_out=d, tm=tmem, tn=d,
                                out_dtype=BF16, heads_out=(XA_HEADS, XA_DH), name="mem_v")
    proj_tile = min(row_tile, 512)
    x3_p = _xattn_block(xp, [y_rg_p, y_ml_p], [w_out16_rg, w_out16_ml], g_xa.reshape(-1), w_q16,
                        mk, mv, w_o16, batch=bp, seq=seq, tq=min(seq, proj_tile), name="xattn_p")
    y_s, y_p0, w_fg16, w_fu16, w_fd16 = _ffn(
        [(x3_s, bs_), (x3_p, row_tile)], g_ffn.reshape(-1), w_fg, w_fu, w_fd, g_final, n_rows=1,
        tf=256, emit_w16=True, vmem_mib=60, name="ffn_s")
    if tp // row_tile > 1:
        y_p = _ffn([(x3_p, row_tile)], g_ffn.reshape(-1), w_fg16, w_fu16, w_fd16, g_final,
                   n_rows=tp // row_tile, tf=512, head_tile=y_p0, vmem_mib=60, name="ffn_p")
    else:
        y_p = y_p0

    return (
        y_p.reshape(bp, seq, d),
        y_s.reshape(bs_, 1, d),
        p_h.reshape(1, bp, RG_WIDTH),
        p_conv.reshape(1, bp, CONV_W - 1, RG_WIDTH),
        p_c.reshape(1, bp, ML_HEADS, ML_DK, ML_DV),
        p_n.reshape(1, bp, ML_HEADS, ML_DK),
        p_m[:, :, 0].reshape(1, bp, ML_HEADS),
        mk_heads.reshape(1, bp, n_mem, XA_HEADS, XA_DH),
        mv_heads.reshape(1, bp, n_mem, XA_HEADS, XA_DH),
        s_h.reshape(1, bs_, RG_WIDTH),
        s_conv.reshape(1, bs_, CONV_W - 1, RG_WIDTH),
        s_c.reshape(1, bs_, ML_HEADS, ML_DK, ML_DV),
        s_n.reshape(1, bs_, ML_HEADS, ML_DK),
        s_m.reshape(1, bs_, ML_HEADS),
    )
```

```python
import functools
from typing import Callable, NamedTuple

import jax
import jax.numpy as jnp
from jax import lax
from jax.experimental import pallas as pl
from jax.experimental.pallas import tpu as pltpu

F32 = jnp.float32
BF16 = jnp.bfloat16

D_MODEL = 2048
RG_WIDTH = D_MODEL // 2
RG_BLOCKS = 8
RG_BLOCK = RG_WIDTH // RG_BLOCKS
CONV_W = 4
RG_C = 8.0
ML_HEADS = 4
ML_WIDTH = D_MODEL - RG_WIDTH
ML_DV = ML_WIDTH // ML_HEADS
ML_DK = ML_DV // 2
N_MEM = 256
XA_HEADS = 4
XA_DH = D_MODEL // XA_HEADS
EPS = 1e-6
NEG = -1e30

OFF_RGX = 0
OFF_RGG = OFF_RGX + RG_WIDTH
OFF_Q = OFF_RGG + RG_WIDTH
OFF_K = OFF_Q + ML_HEADS * ML_DK
OFF_V = OFF_K + ML_HEADS * ML_DK
OFF_O = OFF_V + ML_WIDTH
OFF_I = OFF_O + ML_WIDTH
IN_MAIN = OFF_I
N_GATE = 2 * ML_HEADS

V7X_LANES = 128
V7X_SUBLANES = 8
V7X_VMEM_BYTES = 64 * 2**20

ML_CHUNK = 256


class SideJob(NamedTuple):
    n_blocks: int
    parts: int
    args: tuple
    in_specs: Callable
    out_shape: tuple
    out_specs: Callable
    scratch_shapes: tuple
    body: Callable


def _params(n_axes, vmem_mib):
    assert vmem_mib * 2**20 <= V7X_VMEM_BYTES
    return pltpu.CompilerParams(
        dimension_semantics=("arbitrary",) * n_axes,
        vmem_limit_bytes=vmem_mib * 2**20,
    )


def _rms(x, g):
    ms = jnp.mean(x * x, axis=-1, keepdims=True)
    return x * lax.rsqrt(ms + EPS) * g


def _softplus(u):
    return jnp.maximum(u, 0.0) + jnp.log1p(jnp.exp(-jnp.abs(u)))


def _log_sigmoid(u):
    return -_softplus(-u)


def _sigmoid(u):
    return 0.5 * jnp.tanh(0.5 * u) + 0.5


def _gelu_tanh(x):
    c = 0.7978845608028654
    half_x = 0.5 * x
    return half_x + half_x * jnp.tanh(x * (c + (c * 0.044715) * (x * x)))


def _sqrt_nonneg(v):
    return jnp.where(v > 0.0, v * lax.rsqrt(v), 0.0)


def _mm(a, b):
    return jnp.dot(a.astype(BF16), b.astype(BF16), preferred_element_type=F32)


def _exact_cumsum_lanes(x):
    n = x.shape[-1]
    upper = jnp.where(lax.broadcasted_iota(jnp.int32, (n, n), 0)
                      <= lax.broadcasted_iota(jnp.int32, (n, n), 1), 1.0, 0.0).astype(BF16)
    hi = x.astype(BF16)
    rest = x - hi.astype(F32)
    mid = rest.astype(BF16)
    lo = (rest - mid.astype(F32)).astype(BF16)
    return (jnp.dot(hi, upper, preferred_element_type=F32)
            + jnp.dot(mid, upper, preferred_element_type=F32)
            + jnp.dot(lo, upper, preferred_element_type=F32))


def _dot_w(a, w_ref, w_is_nk):
    w = w_ref[...].astype(BF16)
    if w_is_nk:
        return lax.dot_general(a, w, (((1,), (1,)), ((), ())), preferred_element_type=F32)
    return jnp.dot(a, w, preferred_element_type=F32)


def _norm_linear_kernel(*refs, with_gate, w_is_nk, emit_w16, heads_out, side, side_cols):
    refs = list(refs)
    n_si, n_so, n_ss = ((len(side.args), len(side.out_shape), len(side.scratch_shapes))
                        if side else (0, 0, 0))
    n_in = 3 + with_gate + n_si
    n_out = 1 + bool(heads_out) + with_gate + emit_w16 + n_so
    x_ref, g_ref, w_ref = refs[:3]
    wg_ref = refs[3] if with_gate else None
    side_in = refs[3 + with_gate:n_in]
    outs = refs[n_in:n_in + n_out]
    xn_ref = refs[n_in + n_out]
    side_scratch = refs[n_in + n_out + 1:]
    o_ref = outs[0]
    oh_ref = outs[1] if heads_out else None
    og_ref = outs[1 + bool(heads_out)] if with_gate else None
    w16_ref = outs[n_out - n_so - 1] if emit_w16 else None
    side_out = outs[n_out - n_so:]
    col = pl.program_id(1)

    def column_tile(xn):
        acc = _dot_w(xn, w_ref, w_is_nk)
        o_ref[...] = acc.astype(o_ref.dtype)
        if heads_out:
            oh_ref[...] = acc.reshape(oh_ref.shape)
        if emit_w16:
            w16_ref[...] = w_ref[...].astype(BF16)

    @pl.when(col == 0)
    def _():
        xn = _rms(x_ref[...], g_ref[...]).astype(BF16)
        xn_ref[...] = xn
        if with_gate:
            og_ref[...] = _dot_w(xn, wg_ref, w_is_nk)
        column_tile(xn)

    if side:
        for part in range(side.parts):
            @pl.when((col > 0) & (col <= side_cols) & ((col - 1) % side.parts == part))
            def _(part=part):
                column_tile(xn_ref[...])
                side.body(side_in, side_out, side_scratch, part)

    @pl.when(col > (side_cols if side else 0))
    def _():
        column_tile(xn_ref[...])


def _norm_linear(x, g, w, *, n_out, tm, tn, out_dtype=F32, w_gate=None, w_is_nk=False,
                 emit_w16=False, heads_out=None, side=None, vmem_mib=48, name):
    m, k = x.shape
    k_ax, n_ax = (1, 0) if w_is_nk else (0, 1)
    assert m % tm == 0 and n_out % tn == 0 and w.shape[k_ax] == k and n_out <= w.shape[n_ax]
    assert not emit_w16 or m == tm
    assert heads_out is None or (tn == n_out and heads_out[0] * heads_out[1] == n_out)
    n_rows, n_cols = m // tm, n_out // tn
    with_gate = w_gate is not None
    w_mode = dict(pipeline_mode=pl.Buffered(1)) if tn == n_out else {}
    w_spec = (pl.BlockSpec((tn, k), lambda i, j: (j, 0), **w_mode) if w_is_nk
              else pl.BlockSpec((k, tn), lambda i, j: (0, j), **w_mode))
    in_specs = [
        pl.BlockSpec((tm, k), lambda i, j: (i, 0)),
        pl.BlockSpec((1, k), lambda i, j: (0, 0)),
        w_spec,
    ]
    out_shape = [jax.ShapeDtypeStruct((m, n_out), out_dtype)]
    out_specs = [pl.BlockSpec((tm, tn), lambda i, j: (i, j))]
    scratch_shapes = [pltpu.VMEM((tm, k), BF16)]
    args = [x, g.reshape(1, k), w]
    if heads_out:
        out_shape.append(jax.ShapeDtypeStruct((m, *heads_out), F32))
        out_specs.append(pl.BlockSpec((tm, *heads_out), lambda i, j: (i, 0, 0)))
    if with_gate:
        ng = w_gate.shape[n_ax]
        in_specs.append(pl.BlockSpec(w_gate.shape, lambda i, j: (0, 0)))
        out_shape.append(jax.ShapeDtypeStruct((m, ng), F32))
        out_specs.append(pl.BlockSpec((tm, ng), lambda i, j: (i, 0)))
        args.append(w_gate)
    if emit_w16:
        out_shape.append(jax.ShapeDtypeStruct((n_out, k) if w_is_nk else (k, n_out), BF16))
        out_specs.append(pl.BlockSpec((tn, k), lambda i, j: (j, 0)) if w_is_nk
                         else pl.BlockSpec((k, tn), lambda i, j: (0, j)))
    side_cols = 0
    if side:
        assert side.n_blocks % n_rows == 0
        blocks_per_row = side.n_blocks // n_rows
        side_cols = blocks_per_row * side.parts
        assert 0 < side_cols < n_cols
        block_of = lambda i, j: (i * blocks_per_row
                                 + jnp.clip(j - 1, 0, side_cols - 1) // side.parts)
        in_specs += side.in_specs(block_of)
        out_specs += side.out_specs(block_of)
        out_shape += list(side.out_shape)
        scratch_shapes += list(side.scratch_shapes)
        args += list(side.args)
    out = pl.pallas_call(
        functools.partial(_norm_linear_kernel, with_gate=with_gate, w_is_nk=w_is_nk,
                          emit_w16=emit_w16, heads_out=heads_out, side=side, side_cols=side_cols),
        grid=(n_rows, n_cols),
        in_specs=in_specs,
        out_specs=out_specs,
        out_shape=out_shape,
        scratch_shapes=scratch_shapes,
        compiler_params=_params(2, vmem_mib),
        name=name,
    )(*args)
    return out if len(out) > 1 else out[0]


def _linear_res_kernel(*refs, n_in, emit_w16):
    a_refs = refs[:n_in]
    w_refs = refs[n_in:2 * n_in]
    res_ref = refs[2 * n_in]
    o_ref = refs[2 * n_in + 1]
    acc = res_ref[...]
    for a_ref, w_ref in zip(a_refs, w_refs):
        acc = acc + _mm(a_ref[...], w_ref[...])
    o_ref[...] = acc
    if emit_w16:
        for w_ref, wc_ref in zip(w_refs, refs[2 * n_in + 2:]):
            wc_ref[...] = w_ref[...].astype(BF16)


def _linear_residual(parts, weights, res, *, tm, tn, emit_w16=False, vmem_mib=48, name):
    m, n = res.shape
    kp = parts[0].shape[1]
    assert all(p.shape == (m, kp) for p in parts) and len(weights) == len(parts)
    assert m % tm == 0 and n % tn == 0 and (not emit_w16 or m == tm)
    n_in = len(parts)
    in_specs = [pl.BlockSpec((tm, kp), lambda i, j: (i, 0)) for _ in parts]
    w_mode = dict(pipeline_mode=pl.Buffered(1)) if tn == n else {}
    in_specs += [pl.BlockSpec((kp, tn), lambda i, j, rb=rb: (rb, j), **w_mode)
                 for _, rb in weights]
    in_specs.append(pl.BlockSpec((tm, tn), lambda i, j: (i, j)))
    out_specs = [pl.BlockSpec((tm, tn), lambda i, j: (i, j))]
    out_shape = [jax.ShapeDtypeStruct((m, n), F32)]
    if emit_w16:
        out_specs += [pl.BlockSpec((kp, tn), lambda i, j: (0, j)) for _ in parts]
        out_shape += [jax.ShapeDtypeStruct((kp, n), BF16) for _ in parts]
    out = pl.pallas_call(
        functools.partial(_linear_res_kernel, n_in=n_in, emit_w16=emit_w16),
        grid=(m // tm, n // tn),
        in_specs=in_specs,
        out_specs=out_specs,
        out_shape=out_shape,
        compiler_params=_params(2, vmem_mib),
        name=name,
    )(*parts, *[w for w, _ in weights], res)
    return out if emit_w16 else out[0]


FFN_HEAD_SLABS = 8


def _ffn_kernel(*refs, n_groups, emit_w16, head_tile):
    x_refs = refs[:n_groups]
    g_ref, wg_ref, wu_ref, wd_ref, gf_ref = refs[n_groups:n_groups + 5]
    n_in = n_groups + 5 + bool(head_tile)
    head_ref = refs[n_in - 1] if head_tile else None
    o_refs = refs[n_in:n_in + n_groups]
    w16_refs = refs[n_in + n_groups:n_in + n_groups + 3] if emit_w16 else ()
    xf_refs = refs[-n_groups:]
    row = pl.program_id(0)
    f = pl.program_id(1)
    last = pl.num_programs(1) - 1
    computing = (row > 0) if head_tile else True

    def weights():
        wg = wg_ref[...].astype(BF16)
        wu = wu_ref[...].astype(BF16)
        wd = wd_ref[...].astype(BF16)
        for dst, val in zip(w16_refs, (wg, wu, wd)):
            dst[...] = val
        return wg, wu, wd

    def hidden_tile(xf, w):
        wg, wu, wd = w
        gate = jnp.dot(xf, wg, preferred_element_type=F32)
        up = jnp.dot(xf, wu, preferred_element_type=F32)
        hidden = (gate * _sigmoid(gate)) * up
        return jnp.dot(hidden.astype(BF16), wd, preferred_element_type=F32)

    if head_tile:
        slab = head_ref.shape[0]

        @pl.when((row == 0) & (f < FFN_HEAD_SLABS))
        def _():
            o_refs[0][pl.ds(pl.multiple_of(f * slab, slab), slab), :] = head_ref[...]

    @pl.when(computing & (f == 0))
    def _():
        w = weights()
        for x_ref, o_ref, xf_ref in zip(x_refs, o_refs, xf_refs):
            x = x_ref[...]
            xf = _rms(x, g_ref[...]).astype(BF16)
            xf_ref[...] = xf
            o_ref[...] = x + hidden_tile(xf, w)

    @pl.when(computing & (f > 0) & (f < last))
    def _():
        w = weights()
        for o_ref, xf_ref in zip(o_refs, xf_refs):
            o_ref[...] += hidden_tile(xf_ref[...], w)

    @pl.when(computing & (f == last))
    def _():
        w = weights()
        for o_ref, xf_ref in zip(o_refs, xf_refs):
            o_ref[...] = _rms(o_ref[...] + hidden_tile(xf_ref[...], w), gf_ref[...])


def _ffn(groups, g, w_gate, w_up, w_down, g_final, *, n_rows, tf, emit_w16=False, head_tile=None,
         vmem_mib, name):
    d = groups[0][0].shape[1]
    dff = w_gate.shape[1]
    n_groups = len(groups)
    assert dff % tf == 0 and dff // tf >= 2 and (not emit_w16 or n_rows == 1)
    assert head_tile is None or (n_groups == 1 and head_tile.shape == (groups[0][1], d))
    assert all(x.shape[0] >= n_rows * tm for x, tm in groups)
    col = (lambda i, f: jnp.where(i > 0, f, 0)) if head_tile is not None else (lambda i, f: f)
    up_spec = pl.BlockSpec((d, tf), lambda i, f: (0, col(i, f)))
    down_spec = pl.BlockSpec((tf, d), lambda i, f: (col(i, f), 0))
    vec_spec = pl.BlockSpec((1, d), lambda i, f: (0, 0))
    row_specs = [pl.BlockSpec((tm, d), lambda i, f: (i, 0)) for _, tm in groups]
    in_specs = row_specs + [vec_spec, up_spec, up_spec, down_spec, vec_spec]
    args = [x for x, _ in groups] + [g.reshape(1, d), w_gate, w_up, w_down, g_final.reshape(1, d)]
    if head_tile is not None:
        tm0 = groups[0][1]
        assert tm0 % FFN_HEAD_SLABS == 0 and dff // tf >= FFN_HEAD_SLABS
        last_slab = FFN_HEAD_SLABS - 1
        in_specs.append(pl.BlockSpec(
            (tm0 // FFN_HEAD_SLABS, d),
            lambda i, f: (jnp.where(i == 0, jnp.minimum(f, last_slab), last_slab), 0)))
        args.append(head_tile)
    out_specs = list(row_specs)
    out_shape = [jax.ShapeDtypeStruct((n_rows * tm, d), F32) for _, tm in groups]
    if emit_w16:
        out_specs += [pl.BlockSpec((d, tf), lambda i, f: (0, f))] * 2
        out_specs += [pl.BlockSpec((tf, d), lambda i, f: (f, 0))]
        out_shape += [jax.ShapeDtypeStruct(w.shape, BF16) for w in (w_gate, w_up, w_down)]
    out = pl.pallas_call(
        functools.partial(_ffn_kernel, n_groups=n_groups, emit_w16=emit_w16,
                          head_tile=head_tile is not None),
        grid=(n_rows, dff // tf),
        in_specs=in_specs,
        out_specs=out_specs,
        out_shape=out_shape,
        scratch_shapes=[pltpu.VMEM((tm, d), BF16) for _, tm in groups],
        compiler_params=_params(2, vmem_mib),
        name=name,
    )(*args)
    return out if len(out) > 1 else out[0]


RG_GATE_LOOKAHEAD = 2


def _rg_gates(xr, wgate_ref, ba_ref, bx_ref, lam_ref, a_ref, b_ref, side=None):
    blocks = [slice(n * RG_BLOCK, (n + 1) * RG_BLOCK) for n in range(RG_BLOCKS)]
    gates = {}

    def issue_gate(n):
        if n < RG_BLOCKS:
            gates[n] = _mm(xr[:, blocks[n]], wgate_ref[n])

    rows, score_phase, value_phase = side if side else (0, None, None)
    per = RG_BLOCKS // rows if rows else RG_BLOCKS
    scores = {}
    for n in range(RG_GATE_LOOKAHEAD):
        issue_gate(n)
    for n, sl in enumerate(blocks):
        if rows and n % per == 0:
            j = n // per
            scores[j] = score_phase(j)
            if j > 0:
                value_phase(j - 1, scores.pop(j - 1))
        xn = xr[:, sl]
        g = gates.pop(n)
        issue_gate(n + RG_GATE_LOOKAHEAD)
        r = _sigmoid(g[:, :RG_BLOCK] + ba_ref[:, sl])
        ig = _sigmoid(g[:, RG_BLOCK:] + bx_ref[:, sl])
        a = jnp.exp(r * (-RG_C * _softplus(-lam_ref[:, sl])))
        a_ref[:, sl] = a
        mult = _sqrt_nonneg(jnp.maximum(1.0 - a * a, 0.0))
        b_ref[:, sl] = mult * (ig * xn)
    if rows:
        value_phase(rows - 1, scores.pop(rows - 1))


def _mlstm_chunk(q, k, v, o_gate, li_col, li_row, lf_row, bcum_row, causal, g, c_state, n_state, m):
    cs = q.shape[0]
    bcum_col = jnp.sum(jnp.where(causal, lf_row, 0.0), axis=1, keepdims=True)
    log_d = jnp.where(causal, bcum_col - bcum_row + li_row, NEG)
    inter = bcum_col + m
    m_t = jnp.maximum(inter, jnp.max(log_d, axis=1, keepdims=True))
    dmat = jnp.exp(log_d - m_t)
    sc = jnp.exp(inter - m_t)
    qb = q.astype(BF16)
    kb = k.astype(BF16)
    vb = v.astype(BF16)
    qk = lax.dot_general(qb, kb, (((1,), (1,)), ((), ())), preferred_element_type=F32) * dmat
    num = sc * jnp.dot(qb, c_state.astype(BF16), preferred_element_type=F32) + _mm(qk, vb)
    den = sc * jnp.sum(q * n_state, axis=1, keepdims=True) + jnp.sum(qk, axis=1, keepdims=True)
    den = jnp.maximum(jnp.abs(den), jnp.exp(-m_t))
    hh = num / den
    m_new = m_t[cs - 1:cs, :]
    b_last = bcum_col[cs - 1:cs, :]
    w_end = jnp.exp(b_last - bcum_col + li_col - m_new)
    dec = jnp.exp(b_last + m - m_new)
    wk = w_end * k
    c_new = dec * c_state + lax.dot_general(
        wk.astype(BF16), vb, (((0,), (0,)), ((), ())), preferred_element_type=F32)
    n_new = dec * n_state + jnp.sum(wk, axis=0, keepdims=True)
    y = _rms(hh, g) * _sigmoid(o_gate)
    return y, c_new, n_new, m_new


N_RG_IN = 10
N_ML_IN = 4


def _mixer_seq_kernel(*refs, tl, side_rows):
    (z_ref, conv0_ref, h0_ref, cw_ref, cb_ref, wgate_ref, ba_ref, bx_ref, lam_ref,
     gout_ref) = refs[:N_RG_IN]
    bi_ref, bf_ref, gates_ref, gml_ref = refs[N_RG_IN:N_RG_IN + N_ML_IN]
    n_in = N_RG_IN + N_ML_IN + (3 if side_rows else 0)
    if side_rows:
        sq_ref, sk_ref, sv_ref = refs[N_RG_IN + N_ML_IN:n_in]
    y_ref, hlast_ref, convn_ref, yml_ref, cout_ref, nout_ref, mout_ref = refs[n_in:n_in + 7]
    so_ref = refs[n_in + 7] if side_rows else None
    xe_ref, a_ref, b_ref, h_ref, hc_ref, lf_ref, cst_ref, nst_ref, mst_ref = refs[-9:]
    t = pl.program_id(1)
    pad = V7X_SUBLANES

    @pl.when(t == 0)
    def _():
        xe_ref[pad - 3:pad, :] = conv0_ref[...]
        hc_ref[...] = h0_ref[...]
        cst_ref[...] = jnp.zeros_like(cst_ref)
        nst_ref[...] = jnp.zeros_like(nst_ref)
        mst_ref[...] = jnp.zeros_like(mst_ref)
        lf_ref[...] = jnp.zeros_like(lf_ref)

    @pl.when(t > 0)
    def _():
        xe_ref[pad - 3:pad, :] = xe_ref[tl + pad - 3:tl + pad, :]

    x = z_ref[:, OFF_RGX:OFF_RGX + RG_WIDTH]
    xe_ref[pad:tl + pad, :] = x
    xr = (xe_ref[pad - 3:tl + pad - 3, :] * cw_ref[0:1, :]
          + xe_ref[pad - 2:tl + pad - 2, :] * cw_ref[1:2, :]
          + xe_ref[pad - 1:tl + pad - 1, :] * cw_ref[2:3, :]
          + x * cw_ref[3:4, :]) + cb_ref[...]
    side = None
    if side_rows:
        own = _xattn_own_mask()
        side = (side_rows,
                lambda j: _xattn_probs(sq_ref, sk_ref, j, own),
                lambda j, p: _xattn_values(sv_ref, so_ref, j, p))

    _rg_gates(xr, wgate_ref, ba_ref, bx_ref, lam_ref, a_ref, b_ref, side=side)

    gates = gates_ref[...]
    gates_t = gates.T
    for h in range(ML_HEADS):
        lf_ref[h:h + 1, :] = _log_sigmoid(gates_t[ML_HEADS + h:ML_HEADS + h + 1, :] + bf_ref[h])
    lf_rows = lf_ref[...]
    bcum_rows = _exact_cumsum_lanes(lf_rows)
    causal = (lax.broadcasted_iota(jnp.int32, (tl, tl), 1)
              <= lax.broadcasted_iota(jnp.int32, (tl, tl), 0))
    for h in range(ML_HEADS):
        vo = slice(h * ML_DV, (h + 1) * ML_DV)
        col = lambda off, width: slice(off + h * width, off + (h + 1) * width)
        y_h, c_new, n_new, m_new = _mlstm_chunk(
            z_ref[:, col(OFF_Q, ML_DK)], z_ref[:, col(OFF_K, ML_DK)] * (ML_DK ** -0.5),
            z_ref[:, col(OFF_V, ML_DV)], z_ref[:, col(OFF_O, ML_DV)],
            gates[:, h:h + 1] + bi_ref[h], gates_t[h:h + 1, :] + bi_ref[h],
            lf_rows[h:h + 1, :], bcum_rows[h:h + 1, :], causal, gml_ref[:, vo],
            cst_ref[h], nst_ref[h:h + 1, :], mst_ref[h:h + 1, 0:1])
        cst_ref[h] = c_new
        nst_ref[h:h + 1, :] = n_new
        mst_ref[h:h + 1, :] = jnp.broadcast_to(m_new, (1, V7X_LANES))
        yml_ref[:, vo] = y_h.astype(yml_ref.dtype)
    cout_ref[...] = cst_ref[...]
    nout_ref[...] = nst_ref[...]
    mout_ref[...] = mst_ref[...]

    row = lax.broadcasted_iota(jnp.int32, (V7X_SUBLANES, RG_WIDTH), 0)

    def group(gi, hc):
        r0 = pl.multiple_of(gi * V7X_SUBLANES, V7X_SUBLANES)
        a8 = a_ref[pl.ds(r0, V7X_SUBLANES), :]
        b8 = b_ref[pl.ds(r0, V7X_SUBLANES), :]
        for d in (1, 2, 4):
            keep = row >= d
            b8 = jnp.where(keep, a8 * pltpu.roll(b8, d, axis=0) + b8, b8)
            a8 = jnp.where(keep, a8 * pltpu.roll(a8, d, axis=0), a8)
        h8 = a8 * hc + b8
        h_ref[pl.ds(r0, V7X_SUBLANES), :] = h8
        return h8[V7X_SUBLANES - 1:V7X_SUBLANES, :]

    hc = lax.fori_loop(0, tl // V7X_SUBLANES, group, hc_ref[...], unroll=4)
    hc_ref[...] = hc
    hlast_ref[...] = hc
    convn_ref[...] = xe_ref[tl + pad - 3:tl + pad, :]
    y = h_ref[...] * _gelu_tanh(z_ref[:, OFF_RGG:OFF_RGG + RG_WIDTH])
    y_ref[...] = _rms(y, gout_ref[...]).astype(y_ref.dtype)


def _mixer_seq(z, zgates, conv0, h0, cw, cb, wgate, ba, bx, lam, gout, b_i, b_f, g_ml, *,
               batch, seq, tl, name, side_xattn=None):
    nt = seq // tl
    assert seq % tl == 0 and zgates.shape[1] == V7X_LANES
    w = RG_WIDTH
    dk, dv, nh = ML_DK, ML_DV, ML_HEADS
    row = lambda v: v.reshape(1, w)
    const2 = lambda b, t: (0, 0)
    smem = pl.BlockSpec(memory_space=pltpu.SMEM)
    tile = lambda width: pl.BlockSpec((tl, width), lambda b, t: (b * nt + t, 0))
    in_specs = [
        tile(IN_MAIN),
        pl.BlockSpec((None, CONV_W - 1, w), lambda b, t: (b, 0, 0)),
        pl.BlockSpec((None, 1, w), lambda b, t: (b, 0, 0)),
        pl.BlockSpec((CONV_W, w), const2),
        pl.BlockSpec((1, w), const2),
        pl.BlockSpec((RG_BLOCKS, RG_BLOCK, 2 * RG_BLOCK), lambda b, t: (0, 0, 0)),
        pl.BlockSpec((1, w), const2),
        pl.BlockSpec((1, w), const2),
        pl.BlockSpec((1, w), const2),
        pl.BlockSpec((1, w), const2),
        smem, smem,
        tile(V7X_LANES),
        pl.BlockSpec((1, nh * dv), const2),
    ]
    assert len(in_specs) == N_RG_IN + N_ML_IN and z.shape[1] == IN_MAIN
    out_specs = [
        pl.BlockSpec((tl, w), lambda b, t: (b * nt + t, 0)),
        pl.BlockSpec((None, 1, w), lambda b, t: (b, 0, 0)),
        pl.BlockSpec((None, CONV_W - 1, w), lambda b, t: (b, 0, 0)),
        pl.BlockSpec((tl, nh * dv), lambda b, t: (b * nt + t, 0)),
        pl.BlockSpec((None, nh, dk, dv), lambda b, t: (b, 0, 0, 0)),
        pl.BlockSpec((None, nh, dk), lambda b, t: (b, 0, 0)),
        pl.BlockSpec((None, nh, V7X_LANES), lambda b, t: (b, 0, 0)),
    ]
    out_shape = [
        jax.ShapeDtypeStruct((batch * seq, w), BF16),
        jax.ShapeDtypeStruct((batch, 1, w), F32),
        jax.ShapeDtypeStruct((batch, CONV_W - 1, w), F32),
        jax.ShapeDtypeStruct((batch * seq, nh * dv), BF16),
        jax.ShapeDtypeStruct((batch, nh, dk, dv), F32),
        jax.ShapeDtypeStruct((batch, nh, dk), F32),
        jax.ShapeDtypeStruct((batch, nh, V7X_LANES), F32),
    ]
    args = [z, conv0, h0.reshape(batch, 1, w), cw, row(cb), wgate, row(ba), row(bx), row(lam),
            row(gout), b_i, b_f, zgates, g_ml.reshape(1, nh * dv)]
    side_rows = 0
    vmem_mib = 40
    if side_xattn is not None:
        xq, ck, cv = side_xattn
        nb = xq.shape[0]
        assert nb % (batch * nt) == 0 and ck.shape == (nb, N_MEM, XA_HEADS, XA_DH)
        side_rows = nb // (batch * nt)
        q_spec = pl.BlockSpec((side_rows, XA_HEADS, XA_DH), lambda b, t: (b * nt + t, 0, 0))
        cache_spec = pl.BlockSpec((side_rows, N_MEM, XA_HEADS, XA_DH),
                                  lambda b, t: (b * nt + t, 0, 0, 0))
        in_specs += [q_spec, cache_spec, cache_spec]
        out_specs.append(q_spec)
        out_shape.append(jax.ShapeDtypeStruct((nb, XA_HEADS, XA_DH), F32))
        args += [xq.reshape(nb, XA_HEADS, XA_DH), ck, cv]
        vmem_mib = 60
    return pl.pallas_call(
        functools.partial(_mixer_seq_kernel, tl=tl, side_rows=side_rows),
        grid=(batch, nt),
        in_specs=in_specs,
        out_specs=out_specs,
        out_shape=out_shape,
        scratch_shapes=[
            pltpu.VMEM((tl + V7X_SUBLANES, w), F32),
            pltpu.VMEM((tl, w), F32),
            pltpu.VMEM((tl, w), F32),
            pltpu.VMEM((tl, w), F32),
            pltpu.VMEM((1, w), F32),
            pltpu.VMEM((V7X_SUBLANES, tl), F32),
            pltpu.VMEM((nh, dk, dv), F32),
            pltpu.VMEM((nh, dk), F32),
            pltpu.VMEM((nh, V7X_LANES), F32),
        ],
        compiler_params=_params(2, vmem_mib),
        name=name,
    )(*args)


def _rglru_step_kernel(zx_ref, zg_ref, conv_ref, h0_ref, cw_ref, cb_ref, wgate_ref, ba_ref,
                       bx_ref, lam_ref, gout_ref, y_ref, hn_ref, convn_ref, a_ref, b_ref):
    w = RG_WIDTH
    x = zx_ref[...]
    xr = (conv_ref[:, 0:w] * cw_ref[0:1, :] + conv_ref[:, w:2 * w] * cw_ref[1:2, :]
          + conv_ref[:, 2 * w:3 * w] * cw_ref[2:3, :] + x * cw_ref[3:4, :]) + cb_ref[...]
    _rg_gates(xr, wgate_ref, ba_ref, bx_ref, lam_ref, a_ref, b_ref)
    h = a_ref[...] * h0_ref[...] + b_ref[...]
    hn_ref[...] = h
    convn_ref[:, 0:2 * w] = conv_ref[:, w:3 * w]
    convn_ref[:, 2 * w:3 * w] = x
    y_ref[...] = _rms(h * _gelu_tanh(zg_ref[...]), gout_ref[...]).astype(y_ref.dtype)


def _rglru_step(z, conv, h0, cw, cb, wgate, ba, bx, lam, gout, *, name):
    nb = z.shape[0]
    w = RG_WIDTH
    row = lambda v: v.reshape(1, w)
    c0 = lambda i: (0, 0)
    return pl.pallas_call(
        _rglru_step_kernel,
        grid=(1,),
        in_specs=[
            pl.BlockSpec((nb, w), lambda i: (0, OFF_RGX // w)),
            pl.BlockSpec((nb, w), lambda i: (0, OFF_RGG // w)),
            pl.BlockSpec((nb, (CONV_W - 1) * w), c0),
            pl.BlockSpec((nb, w), c0),
            pl.BlockSpec((CONV_W, w), c0),
            pl.BlockSpec((1, w), c0),
            pl.BlockSpec((RG_BLOCKS, RG_BLOCK, 2 * RG_BLOCK), lambda i: (0, 0, 0)),
            pl.BlockSpec((1, w), c0),
            pl.BlockSpec((1, w), c0),
            pl.BlockSpec((1, w), c0),
            pl.BlockSpec((1, w), c0),
        ],
        out_specs=[
            pl.BlockSpec((nb, w), c0),
            pl.BlockSpec((nb, w), c0),
            pl.BlockSpec((nb, (CONV_W - 1) * w), c0),
        ],
        out_shape=[
            jax.ShapeDtypeStruct((nb, w), BF16),
            jax.ShapeDtypeStruct((nb, w), F32),
            jax.ShapeDtypeStruct((nb, (CONV_W - 1) * w), F32),
        ],
        scratch_shapes=[pltpu.VMEM((nb, w), F32), pltpu.VMEM((nb, w), F32)],
        compiler_params=_params(1, 32),
        name=name,
    )(z, z, conv.reshape(nb, (CONV_W - 1) * w), h0, cw, row(cb), wgate, row(ba), row(bx),
      row(lam), row(gout))


def _mlstm_step_body(in_refs, out_refs, scratch_refs, part, *, bs, parts):
    bi_ref, bf_ref, z_ref, zg_ref, g_ref, c0_ref, n0_ref, m0_ref = in_refs
    y_ref, c_ref, n_ref, m_ref = out_refs
    qc_ref, = scratch_refs
    dk, dv = ML_DK, ML_DV
    nr = bs // parts
    rows = slice(part * nr, (part + 1) * nr)
    eye = (lax.broadcasted_iota(jnp.int32, (dk, dk), 0)
           == lax.broadcasted_iota(jnp.int32, (dk, dk), 1))

    def as_column(row):
        return jnp.sum(jnp.where(eye, jnp.broadcast_to(row, (dk, dk)), 0.0), axis=1, keepdims=True)

    zg = zg_ref[rows, :]
    for h in range(ML_HEADS):
        li = zg[:, h:h + 1] + bi_ref[h]
        lf = _log_sigmoid(zg[:, ML_HEADS + h:ML_HEADS + h + 1] + bf_ref[h])
        m = m0_ref[rows, h:h + 1]
        inter = lf + m
        m_t = jnp.maximum(inter, li)
        dgate = jnp.exp(li - m_t)
        sc = jnp.exp(inter - m_t)
        q = z_ref[rows, OFF_Q + h * dk:OFF_Q + (h + 1) * dk]
        k = z_ref[rows, OFF_K + h * dk:OFF_K + (h + 1) * dk] * (ML_DK ** -0.5)
        v = z_ref[rows, OFF_V + h * dv:OFF_V + (h + 1) * dv]
        n_old = n0_ref[rows, h, :]
        qk = jnp.sum(q * k, axis=1, keepdims=True) * dgate
        w_end = jnp.exp(li - m_t)
        dec = jnp.exp(inter - m_t)
        wk = w_end * k
        for j in range(nr):
            c_old = c0_ref[part * nr + j, h]
            qc_ref[j:j + 1, :] = jnp.sum(as_column(q[j:j + 1, :]) * c_old, axis=0, keepdims=True)
            c_ref[part * nr + j, h] = (dec[j:j + 1, :] * c_old
                                       + as_column(wk[j:j + 1, :]) * v[j:j + 1, :])
        num = sc * qc_ref[0:nr, :] + qk * v
        den = sc * jnp.sum(q * n_old, axis=1, keepdims=True) + qk
        den = jnp.maximum(jnp.abs(den), jnp.exp(-m_t))
        hh = num / den
        n_ref[rows, h, :] = dec * n_old + wk
        m_ref[rows, h:h + 1] = m_t
        y = (_rms(hh, g_ref[:, h * dv:(h + 1) * dv])
             * _sigmoid(z_ref[rows, OFF_O + h * dv:OFF_O + (h + 1) * dv]))
        y_ref[rows, h * dv:(h + 1) * dv] = y.astype(y_ref.dtype)


def _mlstm_step_job(z, zg, c0, n0, m0, b_i, b_f, g_out, *, bs, parts):
    nb = z.shape[0]
    assert nb % bs == 0 and bs % parts == 0
    dk, dv, nh = ML_DK, ML_DV, ML_HEADS
    smem = pl.BlockSpec(memory_space=pltpu.SMEM)

    def in_specs(blk):
        return [
            smem, smem,
            pl.BlockSpec((bs, IN_MAIN), lambda *g: (blk(*g), 0)),
            pl.BlockSpec((bs, V7X_LANES), lambda *g: (blk(*g), 0)),
            pl.BlockSpec((1, nh * dv), lambda *g: (0, 0)),
            pl.BlockSpec((bs, nh, dk, dv), lambda *g: (blk(*g), 0, 0, 0)),
            pl.BlockSpec((bs, nh, dk), lambda *g: (blk(*g), 0, 0)),
            pl.BlockSpec((bs, nh), lambda *g: (blk(*g), 0)),
        ]

    def out_specs(blk):
        return [
            pl.BlockSpec((bs, nh * dv), lambda *g: (blk(*g), 0)),
            pl.BlockSpec((bs, nh, dk, dv), lambda *g: (blk(*g), 0, 0, 0)),
            pl.BlockSpec((bs, nh, dk), lambda *g: (blk(*g), 0, 0)),
            pl.BlockSpec((bs, nh), lambda *g: (blk(*g), 0)),
        ]

    return SideJob(
        n_blocks=nb // bs,
        parts=parts,
        args=(b_i, b_f, z, zg, g_out.reshape(1, nh * dv), c0, n0, m0),
        in_specs=in_specs,
        out_shape=(
            jax.ShapeDtypeStruct((nb, nh * dv), F32),
            jax.ShapeDtypeStruct((nb, nh, dk, dv), F32),
            jax.ShapeDtypeStruct((nb, nh, dk), F32),
            jax.ShapeDtypeStruct((nb, nh), F32),
        ),
        out_specs=out_specs,
        scratch_shapes=(pltpu.VMEM((bs, dv), F32),),
        body=functools.partial(_mlstm_step_body, bs=bs, parts=parts),
    )


def _run_side_job(job, *, vmem_mib, name):
    n_in, n_out = len(job.args), len(job.out_shape)

    def body(*refs):
        for part in range(job.parts):
            job.body(refs[:n_in], refs[n_in:n_in + n_out], refs[n_in + n_out:], part)

    block_of = lambda i: i
    return pl.pallas_call(
        body,
        grid=(job.n_blocks,),
        in_specs=job.in_specs(block_of),
        out_specs=job.out_specs(block_of),
        out_shape=list(job.out_shape),
        scratch_shapes=list(job.scratch_shapes),
        compiler_params=_params(1, vmem_mib),
        name=name,
    )(*job.args)


def _softmax_rows(s):
    e = jnp.exp(s - jnp.max(s, axis=-1, keepdims=True))
    return e / jnp.sum(e, axis=-1, keepdims=True)


def _xattn_block_kernel(*refs, n_mix):
    x_ref = refs[0]
    y_refs = refs[1:1 + n_mix]
    wy_refs = refs[1 + n_mix:1 + 2 * n_mix]
    g_ref, wq_ref, k_ref, v_ref, wo_ref, o_ref = refs[1 + 2 * n_mix:]
    x = x_ref[...]
    for y_ref, wy_ref in zip(y_refs, wy_refs):
        x = x + jnp.dot(y_ref[...], wy_ref[...], preferred_element_type=F32)
    xq = jnp.dot(_rms(x, g_ref[...]).astype(BF16), wq_ref[...],
                 preferred_element_type=F32).astype(BF16)
    heads = []
    for h in range(XA_HEADS):
        sl = slice(h * XA_DH, (h + 1) * XA_DH)
        s = lax.dot_general(xq[:, sl], k_ref[:, sl].astype(BF16),
                            (((1,), (1,)), ((), ())), preferred_element_type=F32)
        p = _softmax_rows(s * (XA_DH ** -0.5))
        heads.append(_mm(p, v_ref[:, sl]).astype(BF16))
    o_ref[...] = x + jnp.dot(jnp.concatenate(heads, axis=1), wo_ref[...],
                             preferred_element_type=F32)


def _xattn_block(x, mix_parts, mix_weights16, g, wq16, mk, mv, wo16, *, batch, seq, tq, name):
    nt = seq // tq
    assert seq % tq == 0 and all(w.dtype == BF16 for w in (*mix_weights16, wq16, wo16))
    assert all(p.dtype == BF16 for p in mix_parts)
    d = D_MODEL
    resident = dict(pipeline_mode=pl.Buffered(1))
    rows = lambda width: pl.BlockSpec((tq, width), lambda b, t: (b * nt + t, 0))
    return pl.pallas_call(
        functools.partial(_xattn_block_kernel, n_mix=len(mix_parts)),
        grid=(batch, nt),
        in_specs=[
            rows(d),
            *[rows(p.shape[1]) for p in mix_parts],
            *[pl.BlockSpec(w.shape, lambda b, t: (0, 0), **resident) for w in mix_weights16],
            pl.BlockSpec((1, d), lambda b, t: (0, 0)),
            pl.BlockSpec((d, d), lambda b, t: (0, 0), **resident),
            pl.BlockSpec((N_MEM, d), lambda b, t: (b, 0)),
            pl.BlockSpec((N_MEM, d), lambda b, t: (b, 0)),
            pl.BlockSpec((d, d), lambda b, t: (0, 0), **resident),
        ],
        out_specs=rows(d),
        out_shape=jax.ShapeDtypeStruct((batch * seq, d), F32),
        compiler_params=_params(2, 60),
        name=name,
    )(x, *mix_parts, *mix_weights16, g.reshape(1, d), wq16, mk, mv, wo16)


def _xattn_own_mask():
    nrow = N_MEM * XA_HEADS
    col_head = lax.broadcasted_iota(jnp.int32, (V7X_SUBLANES, nrow), 1) & (XA_HEADS - 1)
    row_head = lax.broadcasted_iota(jnp.int32, (V7X_SUBLANES, nrow), 0) & (XA_HEADS - 1)
    return col_head == row_head


def _xattn_probs(q_ref, k_ref, j, own):
    kf = k_ref[j].reshape(N_MEM * XA_HEADS, XA_DH)
    q8 = jnp.concatenate([q_ref[j]] * (V7X_SUBLANES // XA_HEADS), axis=0)
    s = lax.dot_general(q8.astype(BF16), kf.astype(BF16), (((1,), (1,)), ((), ())),
                        preferred_element_type=F32)
    s = jnp.where(own, s * (XA_DH ** -0.5), NEG)
    e = jnp.where(own, jnp.exp(s - jnp.max(s, axis=-1, keepdims=True)), 0.0)
    return e / jnp.sum(e, axis=-1, keepdims=True)


def _xattn_values(v_ref, o_ref, j, p):
    vf = v_ref[j].reshape(N_MEM * XA_HEADS, XA_DH)
    o_ref[j] = _mm(p, vf)[0:XA_HEADS, :]


def _xattn_step_kernel(q_ref, k_ref, v_ref, o_ref, *, sb):
    own = _xattn_own_mask()
    for j in range(sb):
        _xattn_values(v_ref, o_ref, j, _xattn_probs(q_ref, k_ref, j, own))


def _xattn_step(xq, ck, cv, *, sb, name):
    nb = xq.shape[0]
    assert XA_HEADS & (XA_HEADS - 1) == 0
    assert nb % sb == 0 and ck.shape == (nb, N_MEM, XA_HEADS, XA_DH)
    cache_spec = pl.BlockSpec((sb, N_MEM, XA_HEADS, XA_DH), lambda i: (i, 0, 0, 0))
    q_spec = pl.BlockSpec((sb, XA_HEADS, XA_DH), lambda i: (i, 0, 0))
    return pl.pallas_call(
        functools.partial(_xattn_step_kernel, sb=sb),
        grid=(nb // sb,),
        in_specs=[q_spec, cache_spec, cache_spec],
        out_specs=q_spec,
        out_shape=jax.ShapeDtypeStruct((nb, XA_HEADS, XA_DH), F32),
        compiler_params=_params(1, 40),
        name=name,
    )(xq.reshape(nb, XA_HEADS, XA_DH), ck, cv).reshape(nb, XA_HEADS * XA_DH)


class Tiles(NamedTuple):
    rows: int
    attn_rows: int
    mem_rows: int
    cols: int
    ffn_cols16: int
    ffn_cols32: int
    step_block: int
    step_parts: int
    xattn_rows: int


def _plan_tiles(prompt_rows, mem_rows):
    rows = min(prompt_rows, 1024)
    assert prompt_rows % rows == 0
    return Tiles(rows=rows, attn_rows=min(rows, 512), mem_rows=min(mem_rows, 512), cols=1024,
                 ffn_cols16=512, ffn_cols32=256, step_block=V7X_SUBLANES, step_parts=2,
                 xattn_rows=2)


def kernel(x_prompt, x_sample, mem_prompt, state_rg_h, state_rg_conv, state_ml_C, state_ml_n, state_ml_m, cache_mem_k, cache_mem_v, g_mix, w_in, conv_w, conv_b, w_rg_a, b_rg_a, w_rg_x, b_rg_x, rg_lambda, b_ml_i, b_ml_f, g_rg_out, g_ml_out, w_out, g_xa, g_mem, w_xa_q, w_xa_k, w_xa_v, w_xa_o, g_ffn, w_ffn_gate, w_ffn_up, w_ffn_down, g_final):
    depth = g_mix.shape[0]
    assert depth == 1, "single trunk layer"
    bp, seq, d = x_prompt.shape
    bs_, dec_seq, _ = x_sample.shape
    assert d == D_MODEL and dec_seq == 1
    n_mem = mem_prompt.shape[1]
    assert n_mem == N_MEM
    dff = w_ffn_gate.shape[-1]
    in_w = w_in.shape[-1]
    assert in_w == IN_MAIN + N_GATE

    w_in_t = jnp.swapaxes(w_in, 1, 2).reshape(in_w, d)
    w_gate_pad = jnp.pad(w_in_t[IN_MAIN:], ((0, V7X_LANES - N_GATE), (0, 0)))
    cw = conv_w.reshape(CONV_W, RG_WIDTH)
    wgate = jnp.concatenate([w_rg_a.reshape(RG_BLOCKS, RG_BLOCK, RG_BLOCK),
                             w_rg_x.reshape(RG_BLOCKS, RG_BLOCK, RG_BLOCK)], axis=-1)
    rg_args = (cw, conv_b.reshape(-1), wgate, b_rg_a.reshape(-1), b_rg_x.reshape(-1),
               rg_lambda.reshape(-1), g_rg_out.reshape(-1))
    b_i = b_ml_i.reshape(ML_HEADS)
    b_f = b_ml_f.reshape(ML_HEADS)
    g_ml = g_ml_out.reshape(-1)
    w_out2 = w_out.reshape(d, d)
    w_q = w_xa_q.reshape(d, d)
    w_k = w_xa_k.reshape(d, d)
    w_v = w_xa_v.reshape(d, d)
    w_o = w_xa_o.reshape(d, d)
    w_fg = w_ffn_gate.reshape(d, dff)
    w_fu = w_ffn_up.reshape(d, dff)
    w_fd = w_ffn_down.reshape(dff, d)

    tiles = _plan_tiles(bp * seq, bp * n_mem)
    row_tile = tiles.rows
    rg_zero_conv = jnp.zeros((bp, CONV_W - 1, RG_WIDTH), F32)
    rg_zero_h = jnp.zeros((bp, RG_WIDTH), F32)

    xs = x_sample.reshape(bs_, d)
    z_s, zg_s, w_in16 = _norm_linear(
        xs, g_mix.reshape(-1), w_in_t, n_out=IN_MAIN, tm=bs_, tn=tiles.cols, w_gate=w_gate_pad,
        w_is_nk=True, emit_w16=True, name="in_proj_s")
    y_rg_s, s_h, s_conv = _rglru_step(
        z_s, state_rg_conv.reshape(bs_, CONV_W - 1, RG_WIDTH), state_rg_h.reshape(bs_, RG_WIDTH),
        *rg_args, name="rglru_s")
    mlstm_s_job = _mlstm_step_job(
        z_s, zg_s, state_ml_C.reshape(bs_, ML_HEADS, ML_DK, ML_DV),
        state_ml_n.reshape(bs_, ML_HEADS, ML_DK), state_ml_m.reshape(bs_, ML_HEADS),
        b_i, b_f, g_ml, bs=tiles.step_block, parts=tiles.step_parts)
    tp = bp * seq
    xp = x_prompt.reshape(tp, d)
    in_cols, in_rows = IN_MAIN // tiles.cols, tp // row_tile
    if (mlstm_s_job.n_blocks % in_rows == 0
            and mlstm_s_job.n_blocks // in_rows * mlstm_s_job.parts < in_cols):
        z_p, zg_p, y_ml_s, s_c, s_n, s_m = _norm_linear(
            xp, g_mix.reshape(-1), w_in16, n_out=IN_MAIN, tm=row_tile, tn=tiles.cols,
            w_gate=w_gate_pad, w_is_nk=True, side=mlstm_s_job, vmem_mib=60, name="in_proj_p")
    else:
        z_p, zg_p = _norm_linear(xp, g_mix.reshape(-1), w_in16, n_out=IN_MAIN, tm=row_tile,
                                 tn=tiles.cols, w_gate=w_gate_pad, w_is_nk=True, name="in_proj_p")
        y_ml_s, s_c, s_n, s_m = _run_side_job(mlstm_s_job, vmem_mib=32, name="mlstm_s")
    x1_s, w_out16_rg, w_out16_ml = _linear_residual(
        [y_rg_s, y_ml_s], [(w_out2, 0), (w_out2, 1)], xs, tm=bs_, tn=tiles.cols, emit_w16=True,
        name="mix_out_s")
    xq_s, w_q16 = _norm_linear(x1_s, g_xa.reshape(-1), w_q, n_out=d, tm=bs_, tn=tiles.cols,
                               emit_w16=True, name="xa_q_s")
    ck = cache_mem_k.reshape(bs_, n_mem, XA_HEADS, XA_DH)
    cv = cache_mem_v.reshape(bs_, n_mem, XA_HEADS, XA_DH)

    mix_tl = min(seq, ML_CHUNK)
    mix_steps = bp * (seq // mix_tl)
    mix_args = (z_p, zg_p, rg_zero_conv, rg_zero_h, *rg_args, b_i, b_f, g_ml)
    if bs_ % mix_steps == 0 and RG_BLOCKS % (bs_ // mix_steps) == 0:
        y_rg_p, p_h, p_conv, y_ml_p, p_c, p_n, p_m, o_s = _mixer_seq(
            *mix_args, batch=bp, seq=seq, tl=mix_tl, side_xattn=(xq_s, ck, cv), name="mixer_p")
        o_s = o_s.reshape(bs_, d)
    else:
        y_rg_p, p_h, p_conv, y_ml_p, p_c, p_n, p_m = _mixer_seq(
            *mix_args, batch=bp, seq=seq, tl=mix_tl, name="mixer_p")
        o_s = _xattn_step(xq_s, ck, cv, sb=tiles.xattn_rows, name="xattn_s")

    x3_s, w_o16 = _linear_residual([o_s], [(w_o, 0)], x1_s, tm=bs_, tn=tiles.cols, emit_w16=True,
                                   name="xa_out_s")

    mem2 = mem_prompt.reshape(bp * n_mem, d)
    tmem = tiles.mem_rows
    mk, mk_heads = _norm_linear(mem2, g_mem.reshape(-1), w_k, n_out=d, tm=tmem, tn=d,
                                out_dtype=BF16, heads_out=(XA_HEADS, XA_DH), name="mem_k")
    mv, mv_heads = _norm_linear(mem2, g_mem.reshape(-1), w_v, n_out=d, tm=tmem, tn=d,
                                out_dtype=BF16, heads_out=(XA_HEADS, XA_DH), name="mem_v")
    x3_p = _xattn_block(xp, [y_rg_p, y_ml_p], [w_out16_rg, w_out16_ml], g_xa.reshape(-1), w_q16,
                        mk, mv, w_o16, batch=bp, seq=seq, tq=min(seq, tiles.attn_rows), name="xattn_p")
    y_s, y_p0, w_fg16, w_fu16, w_fd16 = _ffn(
        [(x3_s, bs_), (x3_p, row_tile)], g_ffn.reshape(-1), w_fg, w_fu, w_fd, g_final, n_rows=1,
        tf=tiles.ffn_cols32, emit_w16=True, vmem_mib=60, name="ffn_s")
    if tp // row_tile > 1:
        y_p = _ffn([(x3_p, row_tile)], g_ffn.reshape(-1), w_fg16, w_fu16, w_fd16, g_final,
                   n_rows=tp // row_tile, tf=tiles.ffn_cols16, head_tile=y_p0, vmem_mib=60, name="ffn_p")
    else:
        y_p = y_p0

    return (
        y_p.reshape(bp, seq, d),
        y_s.reshape(bs_, 1, d),
        p_h.reshape(1, bp, RG_WIDTH),
        p_conv.reshape(1, bp, CONV_W - 1, RG_WIDTH),
        p_c.reshape(1, bp, ML_HEADS, ML_DK, ML_DV),
        p_n.reshape(1, bp, ML_HEADS, ML_DK),
        p_m[:, :, 0].reshape(1, bp, ML_HEADS),
        mk_heads.reshape(1, bp, n_mem, XA_HEADS, XA_DH),
        mv_heads.reshape(1, bp, n_mem, XA_HEADS, XA_DH),
        s_h.reshape(1, bs_, RG_WIDTH),
        s_conv.reshape(1, bs_, CONV_W - 1, RG_WIDTH),
        s_c.reshape(1, bs_, ML_HEADS, ML_DK, ML_DV),
        s_n.reshape(1, bs_, ML_HEADS, ML_DK),
        s_m.reshape(1, bs_, ML_HEADS),
    )
```

```python
import functools
from typing import Callable, NamedTuple

import jax
import jax.numpy as jnp
from jax import lax
from jax.experimental import pallas as pl
from jax.experimental.pallas import tpu as pltpu

F32 = jnp.float32
BF16 = jnp.bfloat16

D_MODEL = 2048
RG_WIDTH = D_MODEL // 2
RG_BLOCKS = 8
RG_BLOCK = RG_WIDTH // RG_BLOCKS
CONV_W = 4
RG_C = 8.0
ML_HEADS = 4
ML_WIDTH = D_MODEL - RG_WIDTH
ML_DV = ML_WIDTH // ML_HEADS
ML_DK = ML_DV // 2
N_MEM = 256
XA_HEADS = 4
XA_DH = D_MODEL // XA_HEADS
EPS = 1e-6
NEG = -1e30

OFF_RGX = 0
OFF_RGG = OFF_RGX + RG_WIDTH
OFF_Q = OFF_RGG + RG_WIDTH
OFF_K = OFF_Q + ML_HEADS * ML_DK
OFF_V = OFF_K + ML_HEADS * ML_DK
OFF_O = OFF_V + ML_WIDTH
OFF_I = OFF_O + ML_WIDTH
IN_MAIN = OFF_I
N_GATE = 2 * ML_HEADS

V7X_LANES = 128
V7X_SUBLANES = 8
V7X_VMEM_BYTES = 64 * 2**20

ML_CHUNK = 256


class SideJob(NamedTuple):
    n_blocks: int
    parts: int
    args: tuple
    in_specs: Callable
    out_shape: tuple
    out_specs: Callable
    scratch_shapes: tuple
    body: Callable


def _params(n_axes, vmem_mib):
    assert vmem_mib * 2**20 <= V7X_VMEM_BYTES
    return pltpu.CompilerParams(
        dimension_semantics=("arbitrary",) * n_axes,
        vmem_limit_bytes=vmem_mib * 2**20,
    )


def _rms(x, g):
    ms = jnp.mean(x * x, axis=-1, keepdims=True)
    return x * lax.rsqrt(ms + EPS) * g


def _softplus(u):
    return jnp.maximum(u, 0.0) + jnp.log1p(jnp.exp(-jnp.abs(u)))


def _log_sigmoid(u):
    return -_softplus(-u)


def _sigmoid(u):
    return 0.5 * jnp.tanh(0.5 * u) + 0.5


def _gelu_tanh(x):
    c = 0.7978845608028654
    half_x = 0.5 * x
    return half_x + half_x * jnp.tanh(x * (c + (c * 0.044715) * (x * x)))


def _sqrt_nonneg(v):
    return jnp.where(v > 0.0, v * lax.rsqrt(v), 0.0)


def _mm(a, b):
    return jnp.dot(a.astype(BF16), b.astype(BF16), preferred_element_type=F32)


def _exact_cumsum_lanes(x):
    n = x.shape[-1]
    upper = jnp.where(lax.broadcasted_iota(jnp.int32, (n, n), 0)
                      <= lax.broadcasted_iota(jnp.int32, (n, n), 1), 1.0, 0.0).astype(BF16)
    hi = x.astype(BF16)
    rest = x - hi.astype(F32)
    mid = rest.astype(BF16)
    lo = (rest - mid.astype(F32)).astype(BF16)
    return (jnp.dot(hi, upper, preferred_element_type=F32)
            + jnp.dot(mid, upper, preferred_element_type=F32)
            + jnp.dot(lo, upper, preferred_element_type=F32))


def _dot_w(a, w_ref, w_is_nk, rows=None):
    w = (w_ref[...] if rows is None else w_ref[rows, :]).astype(BF16)
    if w_is_nk:
        return lax.dot_general(a, w, (((1,), (1,)), ((), ())), preferred_element_type=F32)
    return jnp.dot(a, w, preferred_element_type=F32)


def _norm_linear_kernel(*refs, with_gate, w_is_nk, emit_w16, heads_out, side, side_cols, w_rows):
    refs = list(refs)
    n_si, n_so, n_ss = ((len(side.args), len(side.out_shape), len(side.scratch_shapes))
                        if side else (0, 0, 0))
    n_in = 3 + with_gate + n_si
    n_out = 1 + bool(heads_out) + with_gate + emit_w16 + n_so
    x_ref, g_ref, w_ref = refs[:3]
    wg_ref = refs[3] if with_gate else None
    side_in = refs[3 + with_gate:n_in]
    outs = refs[n_in:n_in + n_out]
    xn_ref = refs[n_in + n_out]
    side_scratch = refs[n_in + n_out + 1:]
    o_ref = outs[0]
    oh_ref = outs[1] if heads_out else None
    og_ref = outs[1 + bool(heads_out)] if with_gate else None
    w16_ref = outs[n_out - n_so - 1] if emit_w16 else None
    side_out = outs[n_out - n_so:]
    col = pl.program_id(1)

    def column_tile(xn):
        rows = pl.ds(pl.multiple_of(col * w_rows, w_rows), w_rows) if w_rows else None
        acc = _dot_w(xn, w_ref, w_is_nk, rows)
        o_ref[...] = acc.astype(o_ref.dtype)
        if heads_out:
            oh_ref[...] = acc.reshape(oh_ref.shape)
        if emit_w16:
            w16_ref[...] = w_ref[...].astype(BF16)

    @pl.when(col == 0)
    def _():
        xn = _rms(x_ref[...], g_ref[...]).astype(BF16)
        xn_ref[...] = xn
        if with_gate:
            og_ref[...] = _dot_w(xn, wg_ref, w_is_nk)
        column_tile(xn)

    if side:
        for part in range(side.parts):
            @pl.when((col > 0) & (col <= side_cols) & ((col - 1) % side.parts == part))
            def _(part=part):
                column_tile(xn_ref[...])
                side.body(side_in, side_out, side_scratch, part)

    @pl.when(col > (side_cols if side else 0))
    def _():
        column_tile(xn_ref[...])


def _norm_linear(x, g, w, *, n_out, tm, tn, out_dtype=F32, w_gate=None, w_is_nk=False,
                 emit_w16=False, heads_out=None, side=None, w_resident=False, vmem_mib=48, name):
    m, k = x.shape
    k_ax, n_ax = (1, 0) if w_is_nk else (0, 1)
    assert m % tm == 0 and n_out % tn == 0 and w.shape[k_ax] == k and n_out <= w.shape[n_ax]
    assert not emit_w16 or m == tm
    assert heads_out is None or (tn == n_out and heads_out[0] * heads_out[1] == n_out)
    n_rows, n_cols = m // tm, n_out // tn
    with_gate = w_gate is not None
    w_mode = dict(pipeline_mode=pl.Buffered(1)) if tn == n_out or w_resident else {}
    if w_resident:
        assert w_is_nk and not emit_w16 and w.shape[0] == n_out
        w_spec = pl.BlockSpec((n_out, k), lambda i, j: (0, 0), **w_mode)
    else:
        w_spec = (pl.BlockSpec((tn, k), lambda i, j: (j, 0), **w_mode) if w_is_nk
                  else pl.BlockSpec((k, tn), lambda i, j: (0, j), **w_mode))
    in_specs = [
        pl.BlockSpec((tm, k), lambda i, j: (i, 0)),
        pl.BlockSpec((1, k), lambda i, j: (0, 0)),
        w_spec,
    ]
    out_shape = [jax.ShapeDtypeStruct((m, n_out), out_dtype)]
    out_specs = [pl.BlockSpec((tm, tn), lambda i, j: (i, j))]
    scratch_shapes = [pltpu.VMEM((tm, k), BF16)]
    args = [x, g.reshape(1, k), w]
    if heads_out:
        out_shape.append(jax.ShapeDtypeStruct((m, *heads_out), F32))
        out_specs.append(pl.BlockSpec((tm, *heads_out), lambda i, j: (i, 0, 0)))
    if with_gate:
        ng = w_gate.shape[n_ax]
        in_specs.append(pl.BlockSpec(w_gate.shape, lambda i, j: (0, 0)))
        out_shape.append(jax.ShapeDtypeStruct((m, ng), F32))
        out_specs.append(pl.BlockSpec((tm, ng), lambda i, j: (i, 0)))
        args.append(w_gate)
    if emit_w16:
        out_shape.append(jax.ShapeDtypeStruct((n_out, k) if w_is_nk else (k, n_out), BF16))
        out_specs.append(pl.BlockSpec((tn, k), lambda i, j: (j, 0)) if w_is_nk
                         else pl.BlockSpec((k, tn), lambda i, j: (0, j)))
    side_cols = 0
    if side:
        assert side.n_blocks % n_rows == 0
        blocks_per_row = side.n_blocks // n_rows
        side_cols = blocks_per_row * side.parts
        assert 0 < side_cols < n_cols
        block_of = lambda i, j: (i * blocks_per_row
                                 + jnp.clip(j - 1, 0, side_cols - 1) // side.parts)
        in_specs += side.in_specs(block_of)
        out_specs += side.out_specs(block_of)
        out_shape += list(side.out_shape)
        scratch_shapes += list(side.scratch_shapes)
        args += list(side.args)
    out = pl.pallas_call(
        functools.partial(_norm_linear_kernel, with_gate=with_gate, w_is_nk=w_is_nk,
                          emit_w16=emit_w16, heads_out=heads_out, side=side, side_cols=side_cols,
                          w_rows=tn if w_resident and tn != n_out else 0),
        grid=(n_rows, n_cols),
        in_specs=in_specs,
        out_specs=out_specs,
        out_shape=out_shape,
        scratch_shapes=scratch_shapes,
        compiler_params=_params(2, vmem_mib),
        name=name,
    )(*args)
    return out if len(out) > 1 else out[0]


def _linear_res_kernel(*refs, n_in, emit_w16):
    a_refs = refs[:n_in]
    w_refs = refs[n_in:2 * n_in]
    res_ref = refs[2 * n_in]
    o_ref = refs[2 * n_in + 1]
    acc = res_ref[...]
    for a_ref, w_ref in zip(a_refs, w_refs):
        acc = acc + _mm(a_ref[...], w_ref[...])
    o_ref[...] = acc
    if emit_w16:
        for w_ref, wc_ref in zip(w_refs, refs[2 * n_in + 2:]):
            wc_ref[...] = w_ref[...].astype(BF16)


def _linear_residual(parts, weights, res, *, tm, tn, emit_w16=False, vmem_mib=48, name):
    m, n = res.shape
    kp = parts[0].shape[1]
    assert all(p.shape == (m, kp) for p in parts) and len(weights) == len(parts)
    assert m % tm == 0 and n % tn == 0 and (not emit_w16 or m == tm)
    n_in = len(parts)
    in_specs = [pl.BlockSpec((tm, kp), lambda i, j: (i, 0)) for _ in parts]
    w_mode = dict(pipeline_mode=pl.Buffered(1)) if tn == n else {}
    in_specs += [pl.BlockSpec((kp, tn), lambda i, j, rb=rb: (rb, j), **w_mode)
                 for _, rb in weights]
    in_specs.append(pl.BlockSpec((tm, tn), lambda i, j: (i, j)))
    out_specs = [pl.BlockSpec((tm, tn), lambda i, j: (i, j))]
    out_shape = [jax.ShapeDtypeStruct((m, n), F32)]
    if emit_w16:
        out_specs += [pl.BlockSpec((kp, tn), lambda i, j: (0, j)) for _ in parts]
        out_shape += [jax.ShapeDtypeStruct((kp, n), BF16) for _ in parts]
    out = pl.pallas_call(
        functools.partial(_linear_res_kernel, n_in=n_in, emit_w16=emit_w16),
        grid=(m // tm, n // tn),
        in_specs=in_specs,
        out_specs=out_specs,
        out_shape=out_shape,
        compiler_params=_params(2, vmem_mib),
        name=name,
    )(*parts, *[w for w, _ in weights], res)
    return out if emit_w16 else out[0]


FFN_HEAD_SLABS = 8


def _ffn_kernel(*refs, n_groups, emit_w16, head_tile):
    x_refs = refs[:n_groups]
    g_ref, wg_ref, wu_ref, wd_ref, gf_ref = refs[n_groups:n_groups + 5]
    n_in = n_groups + 5 + bool(head_tile)
    head_ref = refs[n_in - 1] if head_tile else None
    o_refs = refs[n_in:n_in + n_groups]
    w16_refs = refs[n_in + n_groups:n_in + n_groups + 3] if emit_w16 else ()
    xf_refs = refs[-n_groups:]
    row = pl.program_id(0)
    f = pl.program_id(1)
    last = pl.num_programs(1) - 1
    computing = (row > 0) if head_tile else True

    def weights():
        wg = wg_ref[...].astype(BF16)
        wu = wu_ref[...].astype(BF16)
        wd = wd_ref[...].astype(BF16)
        for dst, val in zip(w16_refs, (wg, wu, wd)):
            dst[...] = val
        return wg, wu, wd

    def hidden_tile(xf, w):
        wg, wu, wd = w
        gate = jnp.dot(xf, wg, preferred_element_type=F32)
        up = jnp.dot(xf, wu, preferred_element_type=F32)
        hidden = (gate * _sigmoid(gate)) * up
        return jnp.dot(hidden.astype(BF16), wd, preferred_element_type=F32)

    if head_tile:
        slab = head_ref.shape[0]

        @pl.when((row == 0) & (f < FFN_HEAD_SLABS))
        def _():
            o_refs[0][pl.ds(pl.multiple_of(f * slab, slab), slab), :] = head_ref[...]

    @pl.when(computing & (f == 0))
    def _():
        w = weights()
        for x_ref, o_ref, xf_ref in zip(x_refs, o_refs, xf_refs):
            x = x_ref[...]
            xf = _rms(x, g_ref[...]).astype(BF16)
            xf_ref[...] = xf
            o_ref[...] = x + hidden_tile(xf, w)

    @pl.when(computing & (f > 0) & (f < last))
    def _():
        w = weights()
        for o_ref, xf_ref in zip(o_refs, xf_refs):
            o_ref[...] += hidden_tile(xf_ref[...], w)

    @pl.when(computing & (f == last))
    def _():
        w = weights()
        for o_ref, xf_ref in zip(o_refs, xf_refs):
            o_ref[...] = _rms(o_ref[...] + hidden_tile(xf_ref[...], w), gf_ref[...])


def _ffn(groups, g, w_gate, w_up, w_down, g_final, *, n_rows, tf, emit_w16=False, head_tile=None,
         vmem_mib, name):
    d = groups[0][0].shape[1]
    dff = w_gate.shape[1]
    n_groups = len(groups)
    assert dff % tf == 0 and dff // tf >= 2 and (not emit_w16 or n_rows == 1)
    assert head_tile is None or (n_groups == 1 and head_tile.shape == (groups[0][1], d))
    assert all(x.shape[0] >= n_rows * tm for x, tm in groups)
    col = (lambda i, f: jnp.where(i > 0, f, 0)) if head_tile is not None else (lambda i, f: f)
    up_spec = pl.BlockSpec((d, tf), lambda i, f: (0, col(i, f)))
    down_spec = pl.BlockSpec((tf, d), lambda i, f: (col(i, f), 0))
    vec_spec = pl.BlockSpec((1, d), lambda i, f: (0, 0))
    row_specs = [pl.BlockSpec((tm, d), lambda i, f: (i, 0)) for _, tm in groups]
    in_specs = row_specs + [vec_spec, up_spec, up_spec, down_spec, vec_spec]
    args = [x for x, _ in groups] + [g.reshape(1, d), w_gate, w_up, w_down, g_final.reshape(1, d)]
    if head_tile is not None:
        tm0 = groups[0][1]
        assert tm0 % FFN_HEAD_SLABS == 0 and dff // tf >= FFN_HEAD_SLABS
        last_slab = FFN_HEAD_SLABS - 1
        in_specs.append(pl.BlockSpec(
            (tm0 // FFN_HEAD_SLABS, d),
            lambda i, f: (jnp.where(i == 0, jnp.minimum(f, last_slab), last_slab), 0)))
        args.append(head_tile)
    out_specs = list(row_specs)
    out_shape = [jax.ShapeDtypeStruct((n_rows * tm, d), F32) for _, tm in groups]
    if emit_w16:
        out_specs += [pl.BlockSpec((d, tf), lambda i, f: (0, f))] * 2
        out_specs += [pl.BlockSpec((tf, d), lambda i, f: (f, 0))]
        out_shape += [jax.ShapeDtypeStruct(w.shape, BF16) for w in (w_gate, w_up, w_down)]
    out = pl.pallas_call(
        functools.partial(_ffn_kernel, n_groups=n_groups, emit_w16=emit_w16,
                          head_tile=head_tile is not None),
        grid=(n_rows, dff // tf),
        in_specs=in_specs,
        out_specs=out_specs,
        out_shape=out_shape,
        scratch_shapes=[pltpu.VMEM((tm, d), BF16) for _, tm in groups],
        compiler_params=_params(2, vmem_mib),
        name=name,
    )(*args)
    return out if len(out) > 1 else out[0]


RG_GATE_LOOKAHEAD = 2


def _rg_gates(xr, wgate_ref, ba_ref, bx_ref, lam_ref, a_ref, b_ref, side=None):
    blocks = [slice(n * RG_BLOCK, (n + 1) * RG_BLOCK) for n in range(RG_BLOCKS)]
    gates = {}

    def issue_gate(n):
        if n < RG_BLOCKS:
            gates[n] = _mm(xr[:, blocks[n]], wgate_ref[n])

    rows, score_phase, value_phase = side if side else (0, None, None)
    per = RG_BLOCKS // rows if rows else RG_BLOCKS
    scores = {}
    for n in range(RG_GATE_LOOKAHEAD):
        issue_gate(n)
    for n, sl in enumerate(blocks):
        if rows and n % per == 0:
            j = n // per
            scores[j] = score_phase(j)
            if j > 0:
                value_phase(j - 1, scores.pop(j - 1))
        xn = xr[:, sl]
        g = gates.pop(n)
        issue_gate(n + RG_GATE_LOOKAHEAD)
        r = _sigmoid(g[:, :RG_BLOCK] + ba_ref[:, sl])
        ig = _sigmoid(g[:, RG_BLOCK:] + bx_ref[:, sl])
        a = jnp.exp(r * (-RG_C * _softplus(-lam_ref[:, sl])))
        a_ref[:, sl] = a
        mult = _sqrt_nonneg(jnp.maximum(1.0 - a * a, 0.0))
        b_ref[:, sl] = mult * (ig * xn)
    if rows:
        value_phase(rows - 1, scores.pop(rows - 1))


def _mlstm_chunk(q, k, v, o_gate, li_col, li_row, lf_row, bcum_row, causal, g, c_state, n_state, m):
    cs = q.shape[0]
    bcum_col = jnp.sum(jnp.where(causal, lf_row, 0.0), axis=1, keepdims=True)
    log_d = jnp.where(causal, bcum_col - bcum_row + li_row, NEG)
    inter = bcum_col + m
    m_t = jnp.maximum(inter, jnp.max(log_d, axis=1, keepdims=True))
    dmat = jnp.exp(log_d - m_t)
    sc = jnp.exp(inter - m_t)
    qb = q.astype(BF16)
    kb = k.astype(BF16)
    vb = v.astype(BF16)
    qk = lax.dot_general(qb, kb, (((1,), (1,)), ((), ())), preferred_element_type=F32) * dmat
    num = sc * jnp.dot(qb, c_state.astype(BF16), preferred_element_type=F32) + _mm(qk, vb)
    den = sc * jnp.sum(q * n_state, axis=1, keepdims=True) + jnp.sum(qk, axis=1, keepdims=True)
    den = jnp.maximum(jnp.abs(den), jnp.exp(-m_t))
    hh = num / den
    m_new = m_t[cs - 1:cs, :]
    b_last = bcum_col[cs - 1:cs, :]
    w_end = jnp.exp(b_last - bcum_col + li_col - m_new)
    dec = jnp.exp(b_last + m - m_new)
    wk = w_end * k
    c_new = dec * c_state + lax.dot_general(
        wk.astype(BF16), vb, (((0,), (0,)), ((), ())), preferred_element_type=F32)
    n_new = dec * n_state + jnp.sum(wk, axis=0, keepdims=True)
    y = _rms(hh, g) * _sigmoid(o_gate)
    return y, c_new, n_new, m_new


N_RG_IN = 10
N_ML_IN = 4


def _mixer_seq_kernel(*refs, tl, side_rows):
    (z_ref, conv0_ref, h0_ref, cw_ref, cb_ref, wgate_ref, ba_ref, bx_ref, lam_ref,
     gout_ref) = refs[:N_RG_IN]
    bi_ref, bf_ref, gates_ref, gml_ref = refs[N_RG_IN:N_RG_IN + N_ML_IN]
    n_in = N_RG_IN + N_ML_IN + (3 if side_rows else 0)
    if side_rows:
        sq_ref, sk_ref, sv_ref = refs[N_RG_IN + N_ML_IN:n_in]
    y_ref, hlast_ref, convn_ref, yml_ref, cout_ref, nout_ref, mout_ref = refs[n_in:n_in + 7]
    so_ref = refs[n_in + 7] if side_rows else None
    xe_ref, a_ref, b_ref, h_ref, hc_ref, lf_ref, cst_ref, nst_ref, mst_ref = refs[-9:]
    t = pl.program_id(1)
    pad = V7X_SUBLANES

    @pl.when(t == 0)
    def _():
        xe_ref[pad - 3:pad, :] = conv0_ref[...]
        hc_ref[...] = h0_ref[...]
        cst_ref[...] = jnp.zeros_like(cst_ref)
        nst_ref[...] = jnp.zeros_like(nst_ref)
        mst_ref[...] = jnp.zeros_like(mst_ref)
        lf_ref[...] = jnp.zeros_like(lf_ref)

    @pl.when(t > 0)
    def _():
        xe_ref[pad - 3:pad, :] = xe_ref[tl + pad - 3:tl + pad, :]

    x = z_ref[:, OFF_RGX:OFF_RGX + RG_WIDTH]
    xe_ref[pad:tl + pad, :] = x
    xr = (xe_ref[pad - 3:tl + pad - 3, :] * cw_ref[0:1, :]
          + xe_ref[pad - 2:tl + pad - 2, :] * cw_ref[1:2, :]
          + xe_ref[pad - 1:tl + pad - 1, :] * cw_ref[2:3, :]
          + x * cw_ref[3:4, :]) + cb_ref[...]
    side = None
    if side_rows:
        own = _xattn_own_mask()
        side = (side_rows,
                lambda j: _xattn_probs(sq_ref, sk_ref, j, own),
                lambda j, p: _xattn_values(sv_ref, so_ref, j, p))

    _rg_gates(xr, wgate_ref, ba_ref, bx_ref, lam_ref, a_ref, b_ref, side=side)

    gates = gates_ref[...]
    gates_t = gates.T
    for h in range(ML_HEADS):
        lf_ref[h:h + 1, :] = _log_sigmoid(gates_t[ML_HEADS + h:ML_HEADS + h + 1, :] + bf_ref[h])
    lf_rows = lf_ref[...]
    bcum_rows = _exact_cumsum_lanes(lf_rows)
    causal = (lax.broadcasted_iota(jnp.int32, (tl, tl), 1)
              <= lax.broadcasted_iota(jnp.int32, (tl, tl), 0))
    for h in range(ML_HEADS):
        vo = slice(h * ML_DV, (h + 1) * ML_DV)
        col = lambda off, width: slice(off + h * width, off + (h + 1) * width)
        y_h, c_new, n_new, m_new = _mlstm_chunk(
            z_ref[:, col(OFF_Q, ML_DK)], z_ref[:, col(OFF_K, ML_DK)] * (ML_DK ** -0.5),
            z_ref[:, col(OFF_V, ML_DV)], z_ref[:, col(OFF_O, ML_DV)],
            gates[:, h:h + 1] + bi_ref[h], gates_t[h:h + 1, :] + bi_ref[h],
            lf_rows[h:h + 1, :], bcum_rows[h:h + 1, :], causal, gml_ref[:, vo],
            cst_ref[h], nst_ref[h:h + 1, :], mst_ref[h:h + 1, 0:1])
        cst_ref[h] = c_new
        nst_ref[h:h + 1, :] = n_new
        mst_ref[h:h + 1, :] = jnp.broadcast_to(m_new, (1, V7X_LANES))
        yml_ref[:, vo] = y_h.astype(yml_ref.dtype)
    cout_ref[...] = cst_ref[...]
    nout_ref[...] = nst_ref[...]
    mout_ref[...] = mst_ref[...]

    row = lax.broadcasted_iota(jnp.int32, (V7X_SUBLANES, RG_WIDTH), 0)

    def group(gi, hc):
        r0 = pl.multiple_of(gi * V7X_SUBLANES, V7X_SUBLANES)
        a8 = a_ref[pl.ds(r0, V7X_SUBLANES), :]
        b8 = b_ref[pl.ds(r0, V7X_SUBLANES), :]
        for d in (1, 2, 4):
            keep = row >= d
            b8 = jnp.where(keep, a8 * pltpu.roll(b8, d, axis=0) + b8, b8)
            a8 = jnp.where(keep, a8 * pltpu.roll(a8, d, axis=0), a8)
        h8 = a8 * hc + b8
        h_ref[pl.ds(r0, V7X_SUBLANES), :] = h8
        return h8[V7X_SUBLANES - 1:V7X_SUBLANES, :]

    hc = lax.fori_loop(0, tl // V7X_SUBLANES, group, hc_ref[...], unroll=4)
    hc_ref[...] = hc
    hlast_ref[...] = hc
    convn_ref[...] = xe_ref[tl + pad - 3:tl + pad, :]
    y = h_ref[...] * _gelu_tanh(z_ref[:, OFF_RGG:OFF_RGG + RG_WIDTH])
    y_ref[...] = _rms(y, gout_ref[...]).astype(y_ref.dtype)


def _mixer_seq(z, zgates, conv0, h0, cw, cb, wgate, ba, bx, lam, gout, b_i, b_f, g_ml, *,
               batch, seq, tl, name, side_xattn=None):
    nt = seq // tl
    assert seq % tl == 0 and zgates.shape[1] == V7X_LANES
    w = RG_WIDTH
    dk, dv, nh = ML_DK, ML_DV, ML_HEADS
    row = lambda v: v.reshape(1, w)
    const2 = lambda b, t: (0, 0)
    smem = pl.BlockSpec(memory_space=pltpu.SMEM)
    tile = lambda width: pl.BlockSpec((tl, width), lambda b, t: (b * nt + t, 0))
    in_specs = [
        tile(IN_MAIN),
        pl.BlockSpec((None, CONV_W - 1, w), lambda b, t: (b, 0, 0)),
        pl.BlockSpec((None, 1, w), lambda b, t: (b, 0, 0)),
        pl.BlockSpec((CONV_W, w), const2),
        pl.BlockSpec((1, w), const2),
        pl.BlockSpec((RG_BLOCKS, RG_BLOCK, 2 * RG_BLOCK), lambda b, t: (0, 0, 0)),
        pl.BlockSpec((1, w), const2),
        pl.BlockSpec((1, w), const2),
        pl.BlockSpec((1, w), const2),
        pl.BlockSpec((1, w), const2),
        smem, smem,
        tile(V7X_LANES),
        pl.BlockSpec((1, nh * dv), const2),
    ]
    assert len(in_specs) == N_RG_IN + N_ML_IN and z.shape[1] == IN_MAIN
    out_specs = [
        pl.BlockSpec((tl, w), lambda b, t: (b * nt + t, 0)),
        pl.BlockSpec((None, 1, w), lambda b, t: (b, 0, 0)),
        pl.BlockSpec((None, CONV_W - 1, w), lambda b, t: (b, 0, 0)),
        pl.BlockSpec((tl, nh * dv), lambda b, t: (b * nt + t, 0)),
        pl.BlockSpec((None, nh, dk, dv), lambda b, t: (b, 0, 0, 0)),
        pl.BlockSpec((None, nh, dk), lambda b, t: (b, 0, 0)),
        pl.BlockSpec((None, nh, V7X_LANES), lambda b, t: (b, 0, 0)),
    ]
    out_shape = [
        jax.ShapeDtypeStruct((batch * seq, w), BF16),
        jax.ShapeDtypeStruct((batch, 1, w), F32),
        jax.ShapeDtypeStruct((batch, CONV_W - 1, w), F32),
        jax.ShapeDtypeStruct((batch * seq, nh * dv), BF16),
        jax.ShapeDtypeStruct((batch, nh, dk, dv), F32),
        jax.ShapeDtypeStruct((batch, nh, dk), F32),
        jax.ShapeDtypeStruct((batch, nh, V7X_LANES), F32),
    ]
    args = [z, conv0, h0.reshape(batch, 1, w), cw, row(cb), wgate, row(ba), row(bx), row(lam),
            row(gout), b_i, b_f, zgates, g_ml.reshape(1, nh * dv)]
    side_rows = 0
    vmem_mib = 40
    if side_xattn is not None:
        xq, ck, cv = side_xattn
        nb = xq.shape[0]
        assert nb % (batch * nt) == 0 and ck.shape == (nb, N_MEM, XA_HEADS, XA_DH)
        side_rows = nb // (batch * nt)
        q_spec = pl.BlockSpec((side_rows, XA_HEADS, XA_DH), lambda b, t: (b * nt + t, 0, 0))
        cache_spec = pl.BlockSpec((side_rows, N_MEM, XA_HEADS, XA_DH),
                                  lambda b, t: (b * nt + t, 0, 0, 0))
        in_specs += [q_spec, cache_spec, cache_spec]
        out_specs.append(q_spec)
        out_shape.append(jax.ShapeDtypeStruct((nb, XA_HEADS, XA_DH), F32))
        args += [xq.reshape(nb, XA_HEADS, XA_DH), ck, cv]
        vmem_mib = 60
    return pl.pallas_call(
        functools.partial(_mixer_seq_kernel, tl=tl, side_rows=side_rows),
        grid=(batch, nt),
        in_specs=in_specs,
        out_specs=out_specs,
        out_shape=out_shape,
        scratch_shapes=[
            pltpu.VMEM((tl + V7X_SUBLANES, w), F32),
            pltpu.VMEM((tl, w), F32),
            pltpu.VMEM((tl, w), F32),
            pltpu.VMEM((tl, w), F32),
            pltpu.VMEM((1, w), F32),
            pltpu.VMEM((V7X_SUBLANES, tl), F32),
            pltpu.VMEM((nh, dk, dv), F32),
            pltpu.VMEM((nh, dk), F32),
            pltpu.VMEM((nh, V7X_LANES), F32),
        ],
        compiler_params=_params(2, vmem_mib),
        name=name,
    )(*args)


def _rglru_step_kernel(zx_ref, zg_ref, conv_ref, h0_ref, cw_ref, cb_ref, wgate_ref, ba_ref,
                       bx_ref, lam_ref, gout_ref, y_ref, hn_ref, convn_ref, a_ref, b_ref):
    w = RG_WIDTH
    x = zx_ref[...]
    xr = (conv_ref[0] * cw_ref[0:1, :] + conv_ref[1] * cw_ref[1:2, :]
          + conv_ref[2] * cw_ref[2:3, :] + x * cw_ref[3:4, :]) + cb_ref[...]
    _rg_gates(xr, wgate_ref, ba_ref, bx_ref, lam_ref, a_ref, b_ref)
    h = a_ref[...] * h0_ref[...] + b_ref[...]
    hn_ref[...] = h
    convn_ref[0] = conv_ref[1]
    convn_ref[1] = conv_ref[2]
    convn_ref[2] = x
    y_ref[...] = _rms(h * _gelu_tanh(zg_ref[...]), gout_ref[...]).astype(y_ref.dtype)


def _rglru_step(z, conv, h0, cw, cb, wgate, ba, bx, lam, gout, *, name):
    nb = z.shape[0]
    w = RG_WIDTH
    row = lambda v: v.reshape(1, w)
    c0 = lambda i: (0, 0)
    return pl.pallas_call(
        _rglru_step_kernel,
        grid=(1,),
        in_specs=[
            pl.BlockSpec((nb, w), lambda i: (0, OFF_RGX // w)),
            pl.BlockSpec((nb, w), lambda i: (0, OFF_RGG // w)),
            pl.BlockSpec((CONV_W - 1, nb, w), lambda i: (0, 0, 0)),
            pl.BlockSpec((nb, w), c0),
            pl.BlockSpec((CONV_W, w), c0),
            pl.BlockSpec((1, w), c0),
            pl.BlockSpec((RG_BLOCKS, RG_BLOCK, 2 * RG_BLOCK), lambda i: (0, 0, 0)),
            pl.BlockSpec((1, w), c0),
            pl.BlockSpec((1, w), c0),
            pl.BlockSpec((1, w), c0),
            pl.BlockSpec((1, w), c0),
        ],
        out_specs=[
            pl.BlockSpec((nb, w), c0),
            pl.BlockSpec((nb, w), c0),
            pl.BlockSpec((CONV_W - 1, nb, w), lambda i: (0, 0, 0)),
        ],
        out_shape=[
            jax.ShapeDtypeStruct((nb, w), BF16),
            jax.ShapeDtypeStruct((nb, w), F32),
            jax.ShapeDtypeStruct((CONV_W - 1, nb, w), F32),
        ],
        scratch_shapes=[pltpu.VMEM((nb, w), F32), pltpu.VMEM((nb, w), F32)],
        compiler_params=_params(1, 32),
        name=name,
    )(z, z, conv, h0, cw, row(cb), wgate, row(ba), row(bx), row(lam), row(gout))


def _mlstm_step_body(in_refs, out_refs, scratch_refs, part, *, bs, parts):
    bi_ref, bf_ref, z_ref, zg_ref, g_ref, c0_ref, n0_ref, m0_ref = in_refs
    y_ref, c_ref, n_ref, m_ref = out_refs
    qc_ref, = scratch_refs
    dk, dv = ML_DK, ML_DV
    nr = bs // parts
    rows = slice(part * nr, (part + 1) * nr)
    eye = (lax.broadcasted_iota(jnp.int32, (dk, dk), 0)
           == lax.broadcasted_iota(jnp.int32, (dk, dk), 1))

    def as_column(row):
        return jnp.sum(jnp.where(eye, jnp.broadcast_to(row, (dk, dk)), 0.0), axis=1, keepdims=True)

    zg = zg_ref[rows, :]
    for h in range(ML_HEADS):
        li = zg[:, h:h + 1] + bi_ref[h]
        lf = _log_sigmoid(zg[:, ML_HEADS + h:ML_HEADS + h + 1] + bf_ref[h])
        m = m0_ref[rows, h:h + 1]
        inter = lf + m
        m_t = jnp.maximum(inter, li)
        dgate = jnp.exp(li - m_t)
        sc = jnp.exp(inter - m_t)
        q = z_ref[rows, OFF_Q + h * dk:OFF_Q + (h + 1) * dk]
        k = z_ref[rows, OFF_K + h * dk:OFF_K + (h + 1) * dk] * (ML_DK ** -0.5)
        v = z_ref[rows, OFF_V + h * dv:OFF_V + (h + 1) * dv]
        n_old = n0_ref[rows, h, :]
        qk = jnp.sum(q * k, axis=1, keepdims=True) * dgate
        w_end = jnp.exp(li - m_t)
        dec = jnp.exp(inter - m_t)
        wk = w_end * k
        for j in range(nr):
            c_old = c0_ref[part * nr + j, h]
            qc_ref[j:j + 1, :] = jnp.sum(as_column(q[j:j + 1, :]) * c_old, axis=0, keepdims=True)
            c_ref[part * nr + j, h] = (dec[j:j + 1, :] * c_old
                                       + as_column(wk[j:j + 1, :]) * v[j:j + 1, :])
        num = sc * qc_ref[0:nr, :] + qk * v
        den = sc * jnp.sum(q * n_old, axis=1, keepdims=True) + qk
        den = jnp.maximum(jnp.abs(den), jnp.exp(-m_t))
        hh = num / den
        n_ref[rows, h, :] = dec * n_old + wk
        m_ref[rows, h:h + 1] = m_t
        y = (_rms(hh, g_ref[:, h * dv:(h + 1) * dv])
             * _sigmoid(z_ref[rows, OFF_O + h * dv:OFF_O + (h + 1) * dv]))
        y_ref[rows, h * dv:(h + 1) * dv] = y.astype(y_ref.dtype)


def _mlstm_step_job(z, zg, c0, n0, m0, b_i, b_f, g_out, *, bs, parts):
    nb = z.shape[0]
    assert nb % bs == 0 and bs % parts == 0
    dk, dv, nh = ML_DK, ML_DV, ML_HEADS
    smem = pl.BlockSpec(memory_space=pltpu.SMEM)

    def in_specs(blk):
        return [
            smem, smem,
            pl.BlockSpec((bs, IN_MAIN), lambda *g: (blk(*g), 0)),
            pl.BlockSpec((bs, V7X_LANES), lambda *g: (blk(*g), 0)),
            pl.BlockSpec((1, nh * dv), lambda *g: (0, 0)),
            pl.BlockSpec((bs, nh, dk, dv), lambda *g: (blk(*g), 0, 0, 0)),
            pl.BlockSpec((bs, nh, dk), lambda *g: (blk(*g), 0, 0)),
            pl.BlockSpec((bs, nh), lambda *g: (blk(*g), 0)),
        ]

    def out_specs(blk):
        return [
            pl.BlockSpec((bs, nh * dv), lambda *g: (blk(*g), 0)),
            pl.BlockSpec((bs, nh, dk, dv), lambda *g: (blk(*g), 0, 0, 0)),
            pl.BlockSpec((bs, nh, dk), lambda *g: (blk(*g), 0, 0)),
            pl.BlockSpec((bs, nh), lambda *g: (blk(*g), 0)),
        ]

    return SideJob(
        n_blocks=nb // bs,
        parts=parts,
        args=(b_i, b_f, z, zg, g_out.reshape(1, nh * dv), c0, n0, m0),
        in_specs=in_specs,
        out_shape=(
            jax.ShapeDtypeStruct((nb, nh * dv), F32),
            jax.ShapeDtypeStruct((nb, nh, dk, dv), F32),
            jax.ShapeDtypeStruct((nb, nh, dk), F32),
            jax.ShapeDtypeStruct((nb, nh), F32),
        ),
        out_specs=out_specs,
        scratch_shapes=(pltpu.VMEM((bs, dv), F32),),
        body=functools.partial(_mlstm_step_body, bs=bs, parts=parts),
    )


def _run_side_job(job, *, vmem_mib, name):
    n_in, n_out = len(job.args), len(job.out_shape)

    def body(*refs):
        for part in range(job.parts):
            job.body(refs[:n_in], refs[n_in:n_in + n_out], refs[n_in + n_out:], part)

    block_of = lambda i: i
    return pl.pallas_call(
        body,
        grid=(job.n_blocks,),
        in_specs=job.in_specs(block_of),
        out_specs=job.out_specs(block_of),
        out_shape=list(job.out_shape),
        scratch_shapes=list(job.scratch_shapes),
        compiler_params=_params(1, vmem_mib),
        name=name,
    )(*job.args)


def _softmax_rows(s):
    e = jnp.exp(s - jnp.max(s, axis=-1, keepdims=True))
    return e / jnp.sum(e, axis=-1, keepdims=True)


def _xattn_block_kernel(*refs, n_mix):
    x_ref = refs[0]
    y_refs = refs[1:1 + n_mix]
    wy_refs = refs[1 + n_mix:1 + 2 * n_mix]
    g_ref, wq_ref, k_ref, v_ref, wo_ref, o_ref = refs[1 + 2 * n_mix:]
    x = x_ref[...]
    for y_ref, wy_ref in zip(y_refs, wy_refs):
        x = x + jnp.dot(y_ref[...], wy_ref[...], preferred_element_type=F32)
    xq = jnp.dot(_rms(x, g_ref[...]).astype(BF16), wq_ref[...],
                 preferred_element_type=F32).astype(BF16)
    heads = []
    for h in range(XA_HEADS):
        sl = slice(h * XA_DH, (h + 1) * XA_DH)
        s = lax.dot_general(xq[:, sl], k_ref[:, sl].astype(BF16),
                            (((1,), (1,)), ((), ())), preferred_element_type=F32)
        p = _softmax_rows(s * (XA_DH ** -0.5))
        heads.append(_mm(p, v_ref[:, sl]).astype(BF16))
    o_ref[...] = x + jnp.dot(jnp.concatenate(heads, axis=1), wo_ref[...],
                             preferred_element_type=F32)


def _xattn_block(x, mix_parts, mix_weights16, g, wq16, mk, mv, wo16, *, batch, seq, tq, name):
    nt = seq // tq
    assert seq % tq == 0 and all(w.dtype == BF16 for w in (*mix_weights16, wq16, wo16))
    assert all(p.dtype == BF16 for p in mix_parts)
    d = D_MODEL
    resident = dict(pipeline_mode=pl.Buffered(1))
    rows = lambda width: pl.BlockSpec((tq, width), lambda b, t: (b * nt + t, 0))
    return pl.pallas_call(
        functools.partial(_xattn_block_kernel, n_mix=len(mix_parts)),
        grid=(batch, nt),
        in_specs=[
            rows(d),
            *[rows(p.shape[1]) for p in mix_parts],
            *[pl.BlockSpec(w.shape, lambda b, t: (0, 0), **resident) for w in mix_weights16],
            pl.BlockSpec((1, d), lambda b, t: (0, 0)),
            pl.BlockSpec((d, d), lambda b, t: (0, 0), **resident),
            pl.BlockSpec((N_MEM, d), lambda b, t: (b, 0)),
            pl.BlockSpec((N_MEM, d), lambda b, t: (b, 0)),
            pl.BlockSpec((d, d), lambda b, t: (0, 0), **resident),
        ],
        out_specs=rows(d),
        out_shape=jax.ShapeDtypeStruct((batch * seq, d), F32),
        compiler_params=_params(2, 60),
        name=name,
    )(x, *mix_parts, *mix_weights16, g.reshape(1, d), wq16, mk, mv, wo16)


def _xattn_own_mask():
    nrow = N_MEM * XA_HEADS
    col_head = lax.broadcasted_iota(jnp.int32, (V7X_SUBLANES, nrow), 1) & (XA_HEADS - 1)
    row_head = lax.broadcasted_iota(jnp.int32, (V7X_SUBLANES, nrow), 0) & (XA_HEADS - 1)
    return col_head == row_head


def _xattn_probs(q_ref, k_ref, j, own):
    kf = k_ref[j].reshape(N_MEM * XA_HEADS, XA_DH)
    q8 = jnp.concatenate([q_ref[j]] * (V7X_SUBLANES // XA_HEADS), axis=0)
    s = lax.dot_general(q8.astype(BF16), kf.astype(BF16), (((1,), (1,)), ((), ())),
                        preferred_element_type=F32)
    s = jnp.where(own, s * (XA_DH ** -0.5), NEG)
    e = jnp.where(own, jnp.exp(s - jnp.max(s, axis=-1, keepdims=True)), 0.0)
    return e / jnp.sum(e, axis=-1, keepdims=True)


def _xattn_values(v_ref, o_ref, j, p):
    vf = v_ref[j].reshape(N_MEM * XA_HEADS, XA_DH)
    o_ref[j] = _mm(p, vf)[0:XA_HEADS, :]


def _xattn_step_kernel(q_ref, k_ref, v_ref, o_ref, *, sb):
    own = _xattn_own_mask()
    for j in range(sb):
        _xattn_values(v_ref, o_ref, j, _xattn_probs(q_ref, k_ref, j, own))


def _xattn_step(xq, ck, cv, *, sb, name):
    nb = xq.shape[0]
    assert XA_HEADS & (XA_HEADS - 1) == 0
    assert nb % sb == 0 and ck.shape == (nb, N_MEM, XA_HEADS, XA_DH)
    cache_spec = pl.BlockSpec((sb, N_MEM, XA_HEADS, XA_DH), lambda i: (i, 0, 0, 0))
    q_spec = pl.BlockSpec((sb, XA_HEADS, XA_DH), lambda i: (i, 0, 0))
    return pl.pallas_call(
        functools.partial(_xattn_step_kernel, sb=sb),
        grid=(nb // sb,),
        in_specs=[q_spec, cache_spec, cache_spec],
        out_specs=q_spec,
        out_shape=jax.ShapeDtypeStruct((nb, XA_HEADS, XA_DH), F32),
        compiler_params=_params(1, 40),
        name=name,
    )(xq.reshape(nb, XA_HEADS, XA_DH), ck, cv).reshape(nb, XA_HEADS * XA_DH)


class Tiles(NamedTuple):
    rows: int
    in_rows: int
    in_cols: int
    attn_rows: int
    mem_rows: int
    cols: int
    ffn_cols16: int
    ffn_cols32: int
    step_block: int
    step_parts: int
    xattn_rows: int


def _plan_tiles(prompt_rows, mem_rows):
    rows = min(prompt_rows, 1024)
    assert prompt_rows % rows == 0
    return Tiles(rows=rows, in_rows=min(rows, 512), in_cols=IN_MAIN // 4,
                 attn_rows=min(rows, 512), mem_rows=min(mem_rows, 512), cols=1024,
                 ffn_cols16=512, ffn_cols32=256, step_block=V7X_SUBLANES, step_parts=2,
                 xattn_rows=2)


def kernel(x_prompt, x_sample, mem_prompt, state_rg_h, state_rg_conv, state_ml_C, state_ml_n, state_ml_m, cache_mem_k, cache_mem_v, g_mix, w_in, conv_w, conv_b, w_rg_a, b_rg_a, w_rg_x, b_rg_x, rg_lambda, b_ml_i, b_ml_f, g_rg_out, g_ml_out, w_out, g_xa, g_mem, w_xa_q, w_xa_k, w_xa_v, w_xa_o, g_ffn, w_ffn_gate, w_ffn_up, w_ffn_down, g_final):
    depth = g_mix.shape[0]
    assert depth == 1, "single trunk layer"
    bp, seq, d = x_prompt.shape
    bs_, dec_seq, _ = x_sample.shape
    assert d == D_MODEL and dec_seq == 1
    n_mem = mem_prompt.shape[1]
    assert n_mem == N_MEM
    dff = w_ffn_gate.shape[-1]
    in_w = w_in.shape[-1]
    assert in_w == IN_MAIN + N_GATE

    w_in_t = jnp.swapaxes(w_in, 1, 2).reshape(in_w, d)
    w_gate_pad = jnp.pad(w_in_t[IN_MAIN:], ((0, V7X_LANES - N_GATE), (0, 0)))
    cw = conv_w.reshape(CONV_W, RG_WIDTH)
    wgate = jnp.concatenate([w_rg_a.reshape(RG_BLOCKS, RG_BLOCK, RG_BLOCK),
                             w_rg_x.reshape(RG_BLOCKS, RG_BLOCK, RG_BLOCK)], axis=-1)
    rg_args = (cw, conv_b.reshape(-1), wgate, b_rg_a.reshape(-1), b_rg_x.reshape(-1),
               rg_lambda.reshape(-1), g_rg_out.reshape(-1))
    b_i = b_ml_i.reshape(ML_HEADS)
    b_f = b_ml_f.reshape(ML_HEADS)
    g_ml = g_ml_out.reshape(-1)
    w_out2 = w_out.reshape(d, d)
    w_q = w_xa_q.reshape(d, d)
    w_k = w_xa_k.reshape(d, d)
    w_v = w_xa_v.reshape(d, d)
    w_o = w_xa_o.reshape(d, d)
    w_fg = w_ffn_gate.reshape(d, dff)
    w_fu = w_ffn_up.reshape(d, dff)
    w_fd = w_ffn_down.reshape(dff, d)

    tiles = _plan_tiles(bp * seq, bp * n_mem)
    row_tile = tiles.rows
    rg_zero_conv = jnp.zeros((bp, CONV_W - 1, RG_WIDTH), F32)
    rg_zero_h = jnp.zeros((bp, RG_WIDTH), F32)

    xs = x_sample.reshape(bs_, d)
    z_s, zg_s, w_in16 = _norm_linear(
        xs, g_mix.reshape(-1), w_in_t, n_out=IN_MAIN, tm=bs_, tn=tiles.cols, w_gate=w_gate_pad,
        w_is_nk=True, emit_w16=True, name="in_proj_s")
    conv_taps = jnp.swapaxes(state_rg_conv.reshape(bs_, CONV_W - 1, RG_WIDTH), 0, 1)
    y_rg_s, s_h, s_conv_taps = _rglru_step(z_s, conv_taps, state_rg_h.reshape(bs_, RG_WIDTH),
                                           *rg_args, name="rglru_s")
    s_conv = jnp.swapaxes(s_conv_taps, 0, 1)
    mlstm_s_job = _mlstm_step_job(
        z_s, zg_s, state_ml_C.reshape(bs_, ML_HEADS, ML_DK, ML_DV),
        state_ml_n.reshape(bs_, ML_HEADS, ML_DK), state_ml_m.reshape(bs_, ML_HEADS),
        b_i, b_f, g_ml, bs=tiles.step_block, parts=tiles.step_parts)
    tp = bp * seq
    xp = x_prompt.reshape(tp, d)
    in_cols, in_rows = IN_MAIN // tiles.in_cols, tp // tiles.in_rows
    in_proj = functools.partial(
        _norm_linear, xp, g_mix.reshape(-1), w_in16, n_out=IN_MAIN, tm=tiles.in_rows,
        tn=tiles.in_cols, w_gate=w_gate_pad, w_is_nk=True, w_resident=True, vmem_mib=60,
        name="in_proj_p")
    if (mlstm_s_job.n_blocks % in_rows == 0
            and mlstm_s_job.n_blocks // in_rows * mlstm_s_job.parts < in_cols):
        z_p, zg_p, y_ml_s, s_c, s_n, s_m = in_proj(side=mlstm_s_job)
    else:
        z_p, zg_p = in_proj()
        y_ml_s, s_c, s_n, s_m = _run_side_job(mlstm_s_job, vmem_mib=32, name="mlstm_s")
    x1_s, w_out16_rg, w_out16_ml = _linear_residual(
        [y_rg_s, y_ml_s], [(w_out2, 0), (w_out2, 1)], xs, tm=bs_, tn=tiles.cols, emit_w16=True,
        name="mix_out_s")
    xq_s, w_q16 = _norm_linear(x1_s, g_xa.reshape(-1), w_q, n_out=d, tm=bs_, tn=tiles.cols,
                               emit_w16=True, name="xa_q_s")
    ck = cache_mem_k.reshape(bs_, n_mem, XA_HEADS, XA_DH)
    cv = cache_mem_v.reshape(bs_, n_mem, XA_HEADS, XA_DH)

    mix_tl = min(seq, ML_CHUNK)
    mix_steps = bp * (seq // mix_tl)
    mix_args = (z_p, zg_p, rg_zero_conv, rg_zero_h, *rg_args, b_i, b_f, g_ml)
    if bs_ % mix_steps == 0 and RG_BLOCKS % (bs_ // mix_steps) == 0:
        y_rg_p, p_h, p_conv, y_ml_p, p_c, p_n, p_m, o_s = _mixer_seq(
            *mix_args, batch=bp, seq=seq, tl=mix_tl, side_xattn=(xq_s, ck, cv), name="mixer_p")
        o_s = o_s.reshape(bs_, d)
    else:
        y_rg_p, p_h, p_conv, y_ml_p, p_c, p_n, p_m = _mixer_seq(
            *mix_args, batch=bp, seq=seq, tl=mix_tl, name="mixer_p")
        o_s = _xattn_step(xq_s, ck, cv, sb=tiles.xattn_rows, name="xattn_s")

    x3_s, w_o16 = _linear_residual([o_s], [(w_o, 0)], x1_s, tm=bs_, tn=tiles.cols, emit_w16=True,
                                   name="xa_out_s")

    mem2 = mem_prompt.reshape(bp * n_mem, d)
    tmem = tiles.mem_rows
    mk, mk_heads = _norm_linear(mem2, g_mem.reshape(-1), w_k, n_out=d, tm=tmem, tn=d,
                                out_dtype=BF16, heads_out=(XA_HEADS, XA_DH), name="mem_k")
    mv, mv_heads = _norm_linear(mem2, g_mem.reshape(-1), w_v, n_out=d, tm=tmem, tn=d,
                                out_dtype=BF16, heads_out=(XA_HEADS, XA_DH), name="mem_v")
    x3_p = _xattn_block(xp, [y_rg_p, y_ml_p], [w_out16_rg, w_out16_ml], g_xa.reshape(-1), w_q16,
                        mk, mv, w_o16, batch=bp, seq=seq, tq=min(seq, tiles.attn_rows), name="xattn_p")
    y_s, y_p0, w_fg16, w_fu16, w_fd16 = _ffn(
        [(x3_s, bs_), (x3_p, row_tile)], g_ffn.reshape(-1), w_fg, w_fu, w_fd, g_final, n_rows=1,
        tf=tiles.ffn_cols32, emit_w16=True, vmem_mib=60, name="ffn_s")
    if tp // row_tile > 1:
        y_p = _ffn([(x3_p, row_tile)], g_ffn.reshape(-1), w_fg16, w_fu16, w_fd16, g_final,
                   n_rows=tp // row_tile, tf=tiles.ffn_cols16, head_tile=y_p0, vmem_mib=60, name="ffn_p")
    else:
        y_p = y_p0

    return (
        y_p.reshape(bp, seq, d),
        y_s.reshape(bs_, 1, d),
        p_h.reshape(1, bp, RG_WIDTH),
        p_conv.reshape(1, bp, CONV_W - 1, RG_WIDTH),
        p_c.reshape(1, bp, ML_HEADS, ML_DK, ML_DV),
        p_n.reshape(1, bp, ML_HEADS, ML_DK),
        p_m[:, :, 0].reshape(1, bp, ML_HEADS),
        mk_heads.reshape(1, bp, n_mem, XA_HEADS, XA_DH),
        mv_heads.reshape(1, bp, n_mem, XA_HEADS, XA_DH),
        s_h.reshape(1, bs_, RG_WIDTH),
        s_conv.reshape(1, bs_, CONV_W - 1, RG_WIDTH),
        s_c.reshape(1, bs_, ML_HEADS, ML_DK, ML_DV),
        s_n.reshape(1, bs_, ML_HEADS, ML_DK),
        s_m.reshape(1, bs_, ML_HEADS),
    )
```

```python
import functools
from typing import Callable, NamedTuple

import jax
import jax.numpy as jnp
from jax import lax
from jax.experimental import pallas as pl
from jax.experimental.pallas import tpu as pltpu

F32 = jnp.float32
BF16 = jnp.bfloat16

D_MODEL = 2048
RG_WIDTH = D_MODEL // 2
RG_BLOCKS = 8
RG_BLOCK = RG_WIDTH // RG_BLOCKS
CONV_W = 4
RG_C = 8.0
ML_HEADS = 4
ML_WIDTH = D_MODEL - RG_WIDTH
ML_DV = ML_WIDTH // ML_HEADS
ML_DK = ML_DV // 2
N_MEM = 256
XA_HEADS = 4
XA_DH = D_MODEL // XA_HEADS
EPS = 1e-6
NEG = -1e30

OFF_RGX = 0
OFF_RGG = OFF_RGX + RG_WIDTH
OFF_Q = OFF_RGG + RG_WIDTH
OFF_K = OFF_Q + ML_HEADS * ML_DK
OFF_V = OFF_K + ML_HEADS * ML_DK
OFF_O = OFF_V + ML_WIDTH
OFF_I = OFF_O + ML_WIDTH
IN_MAIN = OFF_I
N_GATE = 2 * ML_HEADS

V7X_LANES = 128
V7X_SUBLANES = 8
V7X_VMEM_BYTES = 64 * 2**20

ML_CHUNK = 256


class SideJob(NamedTuple):
    n_blocks: int
    parts: int
    args: tuple
    in_specs: Callable
    out_shape: tuple
    out_specs: Callable
    scratch_shapes: tuple
    body: Callable


def _params(n_axes, vmem_mib):
    assert vmem_mib * 2**20 <= V7X_VMEM_BYTES
    return pltpu.CompilerParams(
        dimension_semantics=("arbitrary",) * n_axes,
        vmem_limit_bytes=vmem_mib * 2**20,
    )


def _rms(x, g):
    ms = jnp.mean(x * x, axis=-1, keepdims=True)
    return x * lax.rsqrt(ms + EPS) * g


def _softplus(u):
    return jnp.maximum(u, 0.0) + jnp.log1p(jnp.exp(-jnp.abs(u)))


def _log_sigmoid(u):
    return -_softplus(-u)


def _sigmoid(u):
    return 0.5 * jnp.tanh(0.5 * u) + 0.5


def _gelu_tanh(x):
    c = 0.7978845608028654
    half_x = 0.5 * x
    return half_x + half_x * jnp.tanh(x * (c + (c * 0.044715) * (x * x)))


def _sqrt_nonneg(v):
    return jnp.where(v > 0.0, v * lax.rsqrt(v), 0.0)


def _mm(a, b):
    return jnp.dot(a.astype(BF16), b.astype(BF16), preferred_element_type=F32)


def _exact_cumsum_lanes(x):
    n = x.shape[-1]
    upper = jnp.where(lax.broadcasted_iota(jnp.int32, (n, n), 0)
                      <= lax.broadcasted_iota(jnp.int32, (n, n), 1), 1.0, 0.0).astype(BF16)
    hi = x.astype(BF16)
    rest = x - hi.astype(F32)
    mid = rest.astype(BF16)
    lo = (rest - mid.astype(F32)).astype(BF16)
    return (jnp.dot(hi, upper, preferred_element_type=F32)
            + jnp.dot(mid, upper, preferred_element_type=F32)
            + jnp.dot(lo, upper, preferred_element_type=F32))


def _dot_w(a, w_ref, w_is_nk):
    w = w_ref[...].astype(BF16)
    if w_is_nk:
        return lax.dot_general(a, w, (((1,), (1,)), ((), ())), preferred_element_type=F32)
    return jnp.dot(a, w, preferred_element_type=F32)


def _norm_linear_kernel(*refs, x_parts, with_gate, w_is_nk, emit_w16, heads_out, side, side_cols):
    x_refs, refs = list(refs[:x_parts]), list(refs[x_parts - 1:])
    n_si, n_so, n_ss = ((len(side.args), len(side.out_shape), len(side.scratch_shapes))
                        if side else (0, 0, 0))
    n_in = 3 + with_gate + n_si
    n_out = 1 + bool(heads_out) + with_gate + emit_w16 + n_so
    g_ref, w_ref = refs[1:3]
    wg_ref = refs[3] if with_gate else None
    side_in = refs[3 + with_gate:n_in]
    outs = refs[n_in:n_in + n_out]
    xn_ref = refs[n_in + n_out]
    side_scratch = refs[n_in + n_out + 1:]
    o_ref = outs[0]
    oh_ref = outs[1] if heads_out else None
    og_ref = outs[1 + bool(heads_out)] if with_gate else None
    w16_ref = outs[n_out - n_so - 1] if emit_w16 else None
    side_out = outs[n_out - n_so:]
    col = pl.program_id(1)

    def column_tile(xn):
        acc = _dot_w(xn, w_ref, w_is_nk)
        o_ref[...] = acc.astype(o_ref.dtype)
        if heads_out:
            oh_ref[...] = acc.reshape(oh_ref.shape)
        if emit_w16:
            w16_ref[...] = w_ref[...].astype(BF16)

    @pl.when(col == 0)
    def _():
        if x_parts == 1:
            xn = _rms(x_refs[0][...], g_ref[...]).astype(BF16)
            xn_ref[...] = xn
        else:
            part = xn_ref.shape[0] // x_parts
            for q, x_ref in enumerate(x_refs):
                xn_ref[q * part:(q + 1) * part, :] = _rms(x_ref[...], g_ref[...]).astype(BF16)
            xn = xn_ref[...]
        if with_gate:
            og_ref[...] = _dot_w(xn, wg_ref, w_is_nk)
        column_tile(xn)

    if side:
        for part in range(side.parts):
            @pl.when((col > 0) & (col <= side_cols) & ((col - 1) % side.parts == part))
            def _(part=part):
                column_tile(xn_ref[...])
                side.body(side_in, side_out, side_scratch, part)

    @pl.when(col > (side_cols if side else 0))
    def _():
        column_tile(xn_ref[...])


def _norm_linear(x, g, w, *, n_out, tm, tn, out_dtype=F32, w_gate=None, w_is_nk=False,
                 emit_w16=False, heads_out=None, side=None, x_parts=1, vmem_mib=48, name):
    m, k = x.shape
    k_ax, n_ax = (1, 0) if w_is_nk else (0, 1)
    assert m % tm == 0 and n_out % tn == 0 and w.shape[k_ax] == k and n_out <= w.shape[n_ax]
    assert not emit_w16 or m == tm
    assert heads_out is None or (tn == n_out and heads_out[0] * heads_out[1] == n_out)
    n_rows, n_cols = m // tm, n_out // tn
    with_gate = w_gate is not None
    w_mode = dict(pipeline_mode=pl.Buffered(1)) if tn == n_out else {}
    w_spec = (pl.BlockSpec((tn, k), lambda i, j: (j, 0), **w_mode) if w_is_nk
              else pl.BlockSpec((k, tn), lambda i, j: (0, j), **w_mode))
    assert tm % x_parts == 0 and x_parts <= n_cols
    if x_parts == 1:
        x_specs = [pl.BlockSpec((tm, k), lambda i, j: (i, 0))]
    else:
        x_specs = [
            pl.BlockSpec((tm // x_parts, k),
                         lambda i, j, q=q: (jnp.minimum(i + (j >= n_cols - (x_parts - 1) + q),
                                                        n_rows - 1) * x_parts + q, 0))
            for q in range(x_parts)]
    in_specs = [
        *x_specs,
        pl.BlockSpec((1, k), lambda i, j: (0, 0)),
        w_spec,
    ]
    out_shape = [jax.ShapeDtypeStruct((m, n_out), out_dtype)]
    out_specs = [pl.BlockSpec((tm, tn), lambda i, j: (i, j))]
    scratch_shapes = [pltpu.VMEM((tm, k), BF16)]
    args = [*([x] * x_parts), g.reshape(1, k), w]
    if heads_out:
        out_shape.append(jax.ShapeDtypeStruct((m, *heads_out), F32))
        out_specs.append(pl.BlockSpec((tm, *heads_out), lambda i, j: (i, 0, 0)))
    if with_gate:
        ng = w_gate.shape[n_ax]
        in_specs.append(pl.BlockSpec(w_gate.shape, lambda i, j: (0, 0)))
        out_shape.append(jax.ShapeDtypeStruct((m, ng), F32))
        out_specs.append(pl.BlockSpec((tm, ng), lambda i, j: (i, 0)))
        args.append(w_gate)
    if emit_w16:
        out_shape.append(jax.ShapeDtypeStruct((n_out, k) if w_is_nk else (k, n_out), BF16))
        out_specs.append(pl.BlockSpec((tn, k), lambda i, j: (j, 0)) if w_is_nk
                         else pl.BlockSpec((k, tn), lambda i, j: (0, j)))
    side_cols = 0
    if side:
        assert side.n_blocks % n_rows == 0
        blocks_per_row = side.n_blocks // n_rows
        side_cols = blocks_per_row * side.parts
        assert 0 < side_cols < n_cols
        block_of = lambda i, j: (i * blocks_per_row
                                 + jnp.clip(j - 1, 0, side_cols - 1) // side.parts)
        in_specs += side.in_specs(block_of)
        out_specs += side.out_specs(block_of)
        out_shape += list(side.out_shape)
        scratch_shapes += list(side.scratch_shapes)
        args += list(side.args)
    out = pl.pallas_call(
        functools.partial(_norm_linear_kernel, x_parts=x_parts, with_gate=with_gate, w_is_nk=w_is_nk,
                          emit_w16=emit_w16, heads_out=heads_out, side=side, side_cols=side_cols),
        grid=(n_rows, n_cols),
        in_specs=in_specs,
        out_specs=out_specs,
        out_shape=out_shape,
        scratch_shapes=scratch_shapes,
        compiler_params=_params(2, vmem_mib),
        name=name,
    )(*args)
    return out if len(out) > 1 else out[0]


def _linear_res_kernel(*refs, n_in, emit_w16):
    a_refs = refs[:n_in]
    w_refs = refs[n_in:2 * n_in]
    res_ref = refs[2 * n_in]
    o_ref = refs[2 * n_in + 1]
    acc = res_ref[...]
    for a_ref, w_ref in zip(a_refs, w_refs):
        acc = acc + _mm(a_ref[...], w_ref[...])
    o_ref[...] = acc
    if emit_w16:
        for w_ref, wc_ref in zip(w_refs, refs[2 * n_in + 2:]):
            wc_ref[...] = w_ref[...].astype(BF16)


def _linear_residual(parts, weights, res, *, tm, tn, emit_w16=False, vmem_mib=48, name):
    m, n = res.shape
    kp = parts[0].shape[1]
    assert all(p.shape == (m, kp) for p in parts) and len(weights) == len(parts)
    assert m % tm == 0 and n % tn == 0 and (not emit_w16 or m == tm)
    n_in = len(parts)
    in_specs = [pl.BlockSpec((tm, kp), lambda i, j: (i, 0)) for _ in parts]
    w_mode = dict(pipeline_mode=pl.Buffered(1)) if tn == n else {}
    in_specs += [pl.BlockSpec((kp, tn), lambda i, j, rb=rb: (rb, j), **w_mode)
                 for _, rb in weights]
    in_specs.append(pl.BlockSpec((tm, tn), lambda i, j: (i, j)))
    out_specs = [pl.BlockSpec((tm, tn), lambda i, j: (i, j))]
    out_shape = [jax.ShapeDtypeStruct((m, n), F32)]
    if emit_w16:
        out_specs += [pl.BlockSpec((kp, tn), lambda i, j: (0, j)) for _ in parts]
        out_shape += [jax.ShapeDtypeStruct((kp, n), BF16) for _ in parts]
    out = pl.pallas_call(
        functools.partial(_linear_res_kernel, n_in=n_in, emit_w16=emit_w16),
        grid=(m // tm, n // tn),
        in_specs=in_specs,
        out_specs=out_specs,
        out_shape=out_shape,
        compiler_params=_params(2, vmem_mib),
        name=name,
    )(*parts, *[w for w, _ in weights], res)
    return out if emit_w16 else out[0]


FFN_HEAD_SLABS = 8


def _ffn_kernel(*refs, n_groups, emit_w16, head_tile):
    x_refs = refs[:n_groups]
    g_ref, wg_ref, wu_ref, wd_ref, gf_ref = refs[n_groups:n_groups + 5]
    n_in = n_groups + 5 + bool(head_tile)
    head_ref = refs[n_in - 1] if head_tile else None
    o_refs = refs[n_in:n_in + n_groups]
    w16_refs = refs[n_in + n_groups:n_in + n_groups + 3] if emit_w16 else ()
    xf_refs = refs[-n_groups:]
    row = pl.program_id(0)
    f = pl.program_id(1)
    last = pl.num_programs(1) - 1
    computing = (row > 0) if head_tile else True

    def weights():
        wg = wg_ref[...].astype(BF16)
        wu = wu_ref[...].astype(BF16)
        wd = wd_ref[...].astype(BF16)
        for dst, val in zip(w16_refs, (wg, wu, wd)):
            dst[...] = val
        return wg, wu, wd

    def hidden_tile(xf, w):
        wg, wu, wd = w
        gate = jnp.dot(xf, wg, preferred_element_type=F32)
        up = jnp.dot(xf, wu, preferred_element_type=F32)
        hidden = (gate * _sigmoid(gate)) * up
        return jnp.dot(hidden.astype(BF16), wd, preferred_element_type=F32)

    if head_tile:
        slab = head_ref.shape[0]

        @pl.when((row == 0) & (f < FFN_HEAD_SLABS))
        def _():
            o_refs[0][pl.ds(pl.multiple_of(f * slab, slab), slab), :] = head_ref[...]

    @pl.when(computing & (f == 0))
    def _():
        w = weights()
        for x_ref, o_ref, xf_ref in zip(x_refs, o_refs, xf_refs):
            x = x_ref[...]
            xf = _rms(x, g_ref[...]).astype(BF16)
            xf_ref[...] = xf
            o_ref[...] = x + hidden_tile(xf, w)

    @pl.when(computing & (f > 0) & (f < last))
    def _():
        w = weights()
        for o_ref, xf_ref in zip(o_refs, xf_refs):
            o_ref[...] += hidden_tile(xf_ref[...], w)

    @pl.when(computing & (f == last))
    def _():
        w = weights()
        for o_ref, xf_ref in zip(o_refs, xf_refs):
            o_ref[...] = _rms(o_ref[...] + hidden_tile(xf_ref[...], w), gf_ref[...])


def _ffn(groups, g, w_gate, w_up, w_down, g_final, *, n_rows, tf, emit_w16=False, head_tile=None,
         vmem_mib, name):
    d = groups[0][0].shape[1]
    dff = w_gate.shape[1]
    n_groups = len(groups)
    assert dff % tf == 0 and dff // tf >= 2 and (not emit_w16 or n_rows == 1)
    assert head_tile is None or (n_groups == 1 and head_tile.shape == (groups[0][1], d))
    assert all(x.shape[0] >= n_rows * tm for x, tm in groups)
    col = (lambda i, f: jnp.where(i > 0, f, 0)) if head_tile is not None else (lambda i, f: f)
    up_spec = pl.BlockSpec((d, tf), lambda i, f: (0, col(i, f)))
    down_spec = pl.BlockSpec((tf, d), lambda i, f: (col(i, f), 0))
    vec_spec = pl.BlockSpec((1, d), lambda i, f: (0, 0))
    row_specs = [pl.BlockSpec((tm, d), lambda i, f: (i, 0)) for _, tm in groups]
    in_specs = row_specs + [vec_spec, up_spec, up_spec, down_spec, vec_spec]
    args = [x for x, _ in groups] + [g.reshape(1, d), w_gate, w_up, w_down, g_final.reshape(1, d)]
    if head_tile is not None:
        tm0 = groups[0][1]
        assert tm0 % FFN_HEAD_SLABS == 0 and dff // tf >= FFN_HEAD_SLABS
        last_slab = FFN_HEAD_SLABS - 1
        in_specs.append(pl.BlockSpec(
            (tm0 // FFN_HEAD_SLABS, d),
            lambda i, f: (jnp.where(i == 0, jnp.minimum(f, last_slab), last_slab), 0)))
        args.append(head_tile)
    out_specs = list(row_specs)
    out_shape = [jax.ShapeDtypeStruct((n_rows * tm, d), F32) for _, tm in groups]
    if emit_w16:
        out_specs += [pl.BlockSpec((d, tf), lambda i, f: (0, f))] * 2
        out_specs += [pl.BlockSpec((tf, d), lambda i, f: (f, 0))]
        out_shape += [jax.ShapeDtypeStruct(w.shape, BF16) for w in (w_gate, w_up, w_down)]
    out = pl.pallas_call(
        functools.partial(_ffn_kernel, n_groups=n_groups, emit_w16=emit_w16,
                          head_tile=head_tile is not None),
        grid=(n_rows, dff // tf),
        in_specs=in_specs,
        out_specs=out_specs,
        out_shape=out_shape,
        scratch_shapes=[pltpu.VMEM((tm, d), BF16) for _, tm in groups],
        compiler_params=_params(2, vmem_mib),
        name=name,
    )(*args)
    return out if len(out) > 1 else out[0]


RG_GATE_LOOKAHEAD = 2


def _rg_gates(xr, wgate_ref, ba_ref, bx_ref, lam_ref, a_ref, b_ref, side=None):
    blocks = [slice(n * RG_BLOCK, (n + 1) * RG_BLOCK) for n in range(RG_BLOCKS)]
    gates = {}

    def issue_gate(n):
        if n < RG_BLOCKS:
            gates[n] = _mm(xr[:, blocks[n]], wgate_ref[n])

    rows, score_phase, value_phase = side if side else (0, None, None)
    per = RG_BLOCKS // rows if rows else RG_BLOCKS
    scores = {}
    for n in range(RG_GATE_LOOKAHEAD):
        issue_gate(n)
    for n, sl in enumerate(blocks):
        if rows and n % per == 0:
            j = n // per
            scores[j] = score_phase(j)
            if j > 0:
                value_phase(j - 1, scores.pop(j - 1))
        xn = xr[:, sl]
        g = gates.pop(n)
        issue_gate(n + RG_GATE_LOOKAHEAD)
        r = _sigmoid(g[:, :RG_BLOCK] + ba_ref[:, sl])
        ig = _sigmoid(g[:, RG_BLOCK:] + bx_ref[:, sl])
        a = jnp.exp(r * (-RG_C * _softplus(-lam_ref[:, sl])))
        a_ref[:, sl] = a
        mult = _sqrt_nonneg(jnp.maximum(1.0 - a * a, 0.0))
        b_ref[:, sl] = mult * (ig * xn)
    if rows:
        value_phase(rows - 1, scores.pop(rows - 1))


def _mlstm_chunk(q, k, v, o_gate, li_col, li_row, lf_row, bcum_row, causal, g, c_state, n_state, m):
    cs = q.shape[0]
    bcum_col = jnp.sum(jnp.where(causal, lf_row, 0.0), axis=1, keepdims=True)
    log_d = jnp.where(causal, bcum_col - bcum_row + li_row, NEG)
    inter = bcum_col + m
    m_t = jnp.maximum(inter, jnp.max(log_d, axis=1, keepdims=True))
    dmat = jnp.exp(log_d - m_t)
    sc = jnp.exp(inter - m_t)
    qb = q.astype(BF16)
    kb = k.astype(BF16)
    vb = v.astype(BF16)
    qk = lax.dot_general(qb, kb, (((1,), (1,)), ((), ())), preferred_element_type=F32) * dmat
    num = sc * jnp.dot(qb, c_state.astype(BF16), preferred_element_type=F32) + _mm(qk, vb)
    den = sc * jnp.sum(q * n_state, axis=1, keepdims=True) + jnp.sum(qk, axis=1, keepdims=True)
    den = jnp.maximum(jnp.abs(den), jnp.exp(-m_t))
    hh = num / den
    m_new = m_t[cs - 1:cs, :]
    b_last = bcum_col[cs - 1:cs, :]
    w_end = jnp.exp(b_last - bcum_col + li_col - m_new)
    dec = jnp.exp(b_last + m - m_new)
    wk = w_end * k
    c_new = dec * c_state + lax.dot_general(
        wk.astype(BF16), vb, (((0,), (0,)), ((), ())), preferred_element_type=F32)
    n_new = dec * n_state + jnp.sum(wk, axis=0, keepdims=True)
    y = _rms(hh, g) * _sigmoid(o_gate)
    return y, c_new, n_new, m_new


N_RG_IN = 10
N_ML_IN = 4


def _mixer_seq_kernel(*refs, tl, side_rows):
    (z_ref, conv0_ref, h0_ref, cw_ref, cb_ref, wgate_ref, ba_ref, bx_ref, lam_ref,
     gout_ref) = refs[:N_RG_IN]
    bi_ref, bf_ref, gates_ref, gml_ref = refs[N_RG_IN:N_RG_IN + N_ML_IN]
    n_in = N_RG_IN + N_ML_IN + (3 if side_rows else 0)
    if side_rows:
        sq_ref, sk_ref, sv_ref = refs[N_RG_IN + N_ML_IN:n_in]
    y_ref, hlast_ref, convn_ref, yml_ref, cout_ref, nout_ref, mout_ref = refs[n_in:n_in + 7]
    so_ref = refs[n_in + 7] if side_rows else None
    xe_ref, a_ref, b_ref, h_ref, hc_ref, lf_ref, cst_ref, nst_ref, mst_ref = refs[-9:]
    t = pl.program_id(1)
    pad = V7X_SUBLANES

    @pl.when(t == 0)
    def _():
        xe_ref[pad - 3:pad, :] = conv0_ref[...]
        hc_ref[...] = h0_ref[...]
        cst_ref[...] = jnp.zeros_like(cst_ref)
        nst_ref[...] = jnp.zeros_like(nst_ref)
        mst_ref[...] = jnp.zeros_like(mst_ref)
        lf_ref[...] = jnp.zeros_like(lf_ref)

    @pl.when(t > 0)
    def _():
        xe_ref[pad - 3:pad, :] = xe_ref[tl + pad - 3:tl + pad, :]

    x = z_ref[:, OFF_RGX:OFF_RGX + RG_WIDTH]
    xe_ref[pad:tl + pad, :] = x
    xr = (xe_ref[pad - 3:tl + pad - 3, :] * cw_ref[0:1, :]
          + xe_ref[pad - 2:tl + pad - 2, :] * cw_ref[1:2, :]
          + xe_ref[pad - 1:tl + pad - 1, :] * cw_ref[2:3, :]
          + x * cw_ref[3:4, :]) + cb_ref[...]
    side = None
    if side_rows:
        own = _xattn_own_mask()
        side = (side_rows,
                lambda j: _xattn_probs(sq_ref, sk_ref, j, own),
                lambda j, p: _xattn_values(sv_ref, so_ref, j, p))

    _rg_gates(xr, wgate_ref, ba_ref, bx_ref, lam_ref, a_ref, b_ref, side=side)

    gates = gates_ref[...]
    gates_t = gates.T
    for h in range(ML_HEADS):
        lf_ref[h:h + 1, :] = _log_sigmoid(gates_t[ML_HEADS + h:ML_HEADS + h + 1, :] + bf_ref[h])
    lf_rows = lf_ref[...]
    bcum_rows = _exact_cumsum_lanes(lf_rows)
    causal = (lax.broadcasted_iota(jnp.int32, (tl, tl), 1)
              <= lax.broadcasted_iota(jnp.int32, (tl, tl), 0))
    for h in range(ML_HEADS):
        vo = slice(h * ML_DV, (h + 1) * ML_DV)
        col = lambda off, width: slice(off + h * width, off + (h + 1) * width)
        y_h, c_new, n_new, m_new = _mlstm_chunk(
            z_ref[:, col(OFF_Q, ML_DK)], z_ref[:, col(OFF_K, ML_DK)] * (ML_DK ** -0.5),
            z_ref[:, col(OFF_V, ML_DV)], z_ref[:, col(OFF_O, ML_DV)],
            gates[:, h:h + 1] + bi_ref[h], gates_t[h:h + 1, :] + bi_ref[h],
            lf_rows[h:h + 1, :], bcum_rows[h:h + 1, :], causal, gml_ref[:, vo],
            cst_ref[h], nst_ref[h:h + 1, :], mst_ref[h:h + 1, 0:1])
        cst_ref[h] = c_new
        nst_ref[h:h + 1, :] = n_new
        mst_ref[h:h + 1, :] = jnp.broadcast_to(m_new, (1, V7X_LANES))
        yml_ref[:, vo] = y_h.astype(yml_ref.dtype)
    cout_ref[...] = cst_ref[...]
    nout_ref[...] = nst_ref[...]
    mout_ref[...] = mst_ref[...]

    row = lax.broadcasted_iota(jnp.int32, (V7X_SUBLANES, RG_WIDTH), 0)

    def group(gi, hc):
        r0 = pl.multiple_of(gi * V7X_SUBLANES, V7X_SUBLANES)
        a8 = a_ref[pl.ds(r0, V7X_SUBLANES), :]
        b8 = b_ref[pl.ds(r0, V7X_SUBLANES), :]
        for d in (1, 2, 4):
            keep = row >= d
            b8 = jnp.where(keep, a8 * pltpu.roll(b8, d, axis=0) + b8, b8)
            a8 = jnp.where(keep, a8 * pltpu.roll(a8, d, axis=0), a8)
        h8 = a8 * hc + b8
        h_ref[pl.ds(r0, V7X_SUBLANES), :] = h8
        return h8[V7X_SUBLANES - 1:V7X_SUBLANES, :]

    hc = lax.fori_loop(0, tl // V7X_SUBLANES, group, hc_ref[...], unroll=4)
    hc_ref[...] = hc
    hlast_ref[...] = hc
    convn_ref[...] = xe_ref[tl + pad - 3:tl + pad, :]
    y = h_ref[...] * _gelu_tanh(z_ref[:, OFF_RGG:OFF_RGG + RG_WIDTH])
    y_ref[...] = _rms(y, gout_ref[...]).astype(y_ref.dtype)


def _mixer_seq(z, zgates, conv0, h0, cw, cb, wgate, ba, bx, lam, gout, b_i, b_f, g_ml, *,
               batch, seq, tl, name, side_xattn=None):
    nt = seq // tl
    assert seq % tl == 0 and zgates.shape[1] == V7X_LANES
    w = RG_WIDTH
    dk, dv, nh = ML_DK, ML_DV, ML_HEADS
    row = lambda v: v.reshape(1, w)
    const2 = lambda b, t: (0, 0)
    smem = pl.BlockSpec(memory_space=pltpu.SMEM)
    tile = lambda width: pl.BlockSpec((tl, width), lambda b, t: (b * nt + t, 0))
    in_specs = [
        tile(IN_MAIN),
        pl.BlockSpec((None, CONV_W - 1, w), lambda b, t: (b, 0, 0)),
        pl.BlockSpec((None, 1, w), lambda b, t: (b, 0, 0)),
        pl.BlockSpec((CONV_W, w), const2),
        pl.BlockSpec((1, w), const2),
        pl.BlockSpec((RG_BLOCKS, RG_BLOCK, 2 * RG_BLOCK), lambda b, t: (0, 0, 0)),
        pl.BlockSpec((1, w), const2),
        pl.BlockSpec((1, w), const2),
        pl.BlockSpec((1, w), const2),
        pl.BlockSpec((1, w), const2),
        smem, smem,
        tile(V7X_LANES),
        pl.BlockSpec((1, nh * dv), const2),
    ]
    assert len(in_specs) == N_RG_IN + N_ML_IN and z.shape[1] == IN_MAIN
    out_specs = [
        pl.BlockSpec((tl, w), lambda b, t: (b * nt + t, 0)),
        pl.BlockSpec((None, 1, w), lambda b, t: (b, 0, 0)),
        pl.BlockSpec((None, CONV_W - 1, w), lambda b, t: (b, 0, 0)),
        pl.BlockSpec((tl, nh * dv), lambda b, t: (b * nt + t, 0)),
        pl.BlockSpec((None, nh, dk, dv), lambda b, t: (b, 0, 0, 0)),
        pl.BlockSpec((None, nh, dk), lambda b, t: (b, 0, 0)),
        pl.BlockSpec((None, nh, V7X_LANES), lambda b, t: (b, 0, 0)),
    ]
    out_shape = [
        jax.ShapeDtypeStruct((batch * seq, w), BF16),
        jax.ShapeDtypeStruct((batch, 1, w), F32),
        jax.ShapeDtypeStruct((batch, CONV_W - 1, w), F32),
        jax.ShapeDtypeStruct((batch * seq, nh * dv), BF16),
        jax.ShapeDtypeStruct((batch, nh, dk, dv), F32),
        jax.ShapeDtypeStruct((batch, nh, dk), F32),
        jax.ShapeDtypeStruct((batch, nh, V7X_LANES), F32),
    ]
    args = [z, conv0, h0.reshape(batch, 1, w), cw, row(cb), wgate, row(ba), row(bx), row(lam),
            row(gout), b_i, b_f, zgates, g_ml.reshape(1, nh * dv)]
    side_rows = 0
    vmem_mib = 40
    if side_xattn is not None:
        xq, ck, cv = side_xattn
        nb = xq.shape[0]
        assert nb % (batch * nt) == 0 and ck.shape == (nb, N_MEM, XA_HEADS, XA_DH)
        side_rows = nb // (batch * nt)
        q_spec = pl.BlockSpec((side_rows, XA_HEADS, XA_DH), lambda b, t: (b * nt + t, 0, 0))
        cache_spec = pl.BlockSpec((side_rows, N_MEM, XA_HEADS, XA_DH),
                                  lambda b, t: (b * nt + t, 0, 0, 0))
        in_specs += [q_spec, cache_spec, cache_spec]
        out_specs.append(q_spec)
        out_shape.append(jax.ShapeDtypeStruct((nb, XA_HEADS, XA_DH), F32))
        args += [xq.reshape(nb, XA_HEADS, XA_DH), ck, cv]
        vmem_mib = 60
    return pl.pallas_call(
        functools.partial(_mixer_seq_kernel, tl=tl, side_rows=side_rows),
        grid=(batch, nt),
        in_specs=in_specs,
        out_specs=out_specs,
        out_shape=out_shape,
        scratch_shapes=[
            pltpu.VMEM((tl + V7X_SUBLANES, w), F32),
            pltpu.VMEM((tl, w), F32),
            pltpu.VMEM((tl, w), F32),
            pltpu.VMEM((tl, w), F32),
            pltpu.VMEM((1, w), F32),
            pltpu.VMEM((V7X_SUBLANES, tl), F32),
            pltpu.VMEM((nh, dk, dv), F32),
            pltpu.VMEM((nh, dk), F32),
            pltpu.VMEM((nh, V7X_LANES), F32),
        ],
        compiler_params=_params(2, vmem_mib),
        name=name,
    )(*args)


def _rglru_step_kernel(zx_ref, zg_ref, conv_ref, h0_ref, cw_ref, cb_ref, wgate_ref, ba_ref,
                       bx_ref, lam_ref, gout_ref, y_ref, hn_ref, convn_ref, a_ref, b_ref):
    w = RG_WIDTH
    x = zx_ref[...]
    xr = (conv_ref[0] * cw_ref[0:1, :] + conv_ref[1] * cw_ref[1:2, :]
          + conv_ref[2] * cw_ref[2:3, :] + x * cw_ref[3:4, :]) + cb_ref[...]
    _rg_gates(xr, wgate_ref, ba_ref, bx_ref, lam_ref, a_ref, b_ref)
    h = a_ref[...] * h0_ref[...] + b_ref[...]
    hn_ref[...] = h
    convn_ref[0] = conv_ref[1]
    convn_ref[1] = conv_ref[2]
    convn_ref[2] = x
    y_ref[...] = _rms(h * _gelu_tanh(zg_ref[...]), gout_ref[...]).astype(y_ref.dtype)


def _rglru_step(z, conv, h0, cw, cb, wgate, ba, bx, lam, gout, *, name):
    nb = z.shape[0]
    w = RG_WIDTH
    row = lambda v: v.reshape(1, w)
    c0 = lambda i: (0, 0)
    return pl.pallas_call(
        _rglru_step_kernel,
        grid=(1,),
        in_specs=[
            pl.BlockSpec((nb, w), lambda i: (0, OFF_RGX // w)),
            pl.BlockSpec((nb, w), lambda i: (0, OFF_RGG // w)),
            pl.BlockSpec((CONV_W - 1, nb, w), lambda i: (0, 0, 0)),
            pl.BlockSpec((nb, w), c0),
            pl.BlockSpec((CONV_W, w), c0),
            pl.BlockSpec((1, w), c0),
            pl.BlockSpec((RG_BLOCKS, RG_BLOCK, 2 * RG_BLOCK), lambda i: (0, 0, 0)),
            pl.BlockSpec((1, w), c0),
            pl.BlockSpec((1, w), c0),
            pl.BlockSpec((1, w), c0),
            pl.BlockSpec((1, w), c0),
        ],
        out_specs=[
            pl.BlockSpec((nb, w), c0),
            pl.BlockSpec((nb, w), c0),
            pl.BlockSpec((CONV_W - 1, nb, w), lambda i: (0, 0, 0)),
        ],
        out_shape=[
            jax.ShapeDtypeStruct((nb, w), BF16),
            jax.ShapeDtypeStruct((nb, w), F32),
            jax.ShapeDtypeStruct((CONV_W - 1, nb, w), F32),
        ],
        scratch_shapes=[pltpu.VMEM((nb, w), F32), pltpu.VMEM((nb, w), F32)],
        compiler_params=_params(1, 32),
        name=name,
    )(z, z, conv, h0, cw, row(cb), wgate, row(ba), row(bx), row(lam), row(gout))


def _mlstm_step_body(in_refs, out_refs, scratch_refs, part, *, bs, parts):
    bi_ref, bf_ref, z_ref, zg_ref, g_ref, c0_ref, n0_ref, m0_ref = in_refs
    y_ref, c_ref, n_ref, m_ref = out_refs
    qc_ref, = scratch_refs
    dk, dv = ML_DK, ML_DV
    nr = bs // parts
    rows = slice(part * nr, (part + 1) * nr)
    eye = (lax.broadcasted_iota(jnp.int32, (dk, dk), 0)
           == lax.broadcasted_iota(jnp.int32, (dk, dk), 1))

    def as_column(row):
        return jnp.sum(jnp.where(eye, jnp.broadcast_to(row, (dk, dk)), 0.0), axis=1, keepdims=True)

    zg = zg_ref[rows, :]
    for h in range(ML_HEADS):
        li = zg[:, h:h + 1] + bi_ref[h]
        lf = _log_sigmoid(zg[:, ML_HEADS + h:ML_HEADS + h + 1] + bf_ref[h])
        m = m0_ref[rows, h:h + 1]
        inter = lf + m
        m_t = jnp.maximum(inter, li)
        dgate = jnp.exp(li - m_t)
        sc = jnp.exp(inter - m_t)
        q = z_ref[rows, OFF_Q + h * dk:OFF_Q + (h + 1) * dk]
        k = z_ref[rows, OFF_K + h * dk:OFF_K + (h + 1) * dk] * (ML_DK ** -0.5)
        v = z_ref[rows, OFF_V + h * dv:OFF_V + (h + 1) * dv]
        n_old = n0_ref[rows, h, :]
        qk = jnp.sum(q * k, axis=1, keepdims=True) * dgate
        w_end = jnp.exp(li - m_t)
        dec = jnp.exp(inter - m_t)
        wk = w_end * k
        for j in range(nr):
            c_old = c0_ref[part * nr + j, h]
            qc_ref[j:j + 1, :] = jnp.sum(as_column(q[j:j + 1, :]) * c_old, axis=0, keepdims=True)
            c_ref[part * nr + j, h] = (dec[j:j + 1, :] * c_old
                                       + as_column(wk[j:j + 1, :]) * v[j:j + 1, :])
        num = sc * qc_ref[0:nr, :] + qk * v
        den = sc * jnp.sum(q * n_old, axis=1, keepdims=True) + qk
        den = jnp.maximum(jnp.abs(den), jnp.exp(-m_t))
        hh = num / den
        n_ref[rows, h, :] = dec * n_old + wk
        m_ref[rows, h:h + 1] = m_t
        y = (_rms(hh, g_ref[:, h * dv:(h + 1) * dv])
             * _sigmoid(z_ref[rows, OFF_O + h * dv:OFF_O + (h + 1) * dv]))
        y_ref[rows, h * dv:(h + 1) * dv] = y.astype(y_ref.dtype)


def _mlstm_step_job(z, zg, c0, n0, m0, b_i, b_f, g_out, *, bs, parts):
    nb = z.shape[0]
    assert nb % bs == 0 and bs % parts == 0
    dk, dv, nh = ML_DK, ML_DV, ML_HEADS
    smem = pl.BlockSpec(memory_space=pltpu.SMEM)

    def in_specs(blk):
        return [
            smem, smem,
            pl.BlockSpec((bs, IN_MAIN), lambda *g: (blk(*g), 0)),
            pl.BlockSpec((bs, V7X_LANES), lambda *g: (blk(*g), 0)),
            pl.BlockSpec((1, nh * dv), lambda *g: (0, 0)),
            pl.BlockSpec((bs, nh, dk, dv), lambda *g: (blk(*g), 0, 0, 0)),
            pl.BlockSpec((bs, nh, dk), lambda *g: (blk(*g), 0, 0)),
            pl.BlockSpec((bs, nh), lambda *g: (blk(*g), 0)),
        ]

    def out_specs(blk):
        return [
            pl.BlockSpec((bs, nh * dv), lambda *g: (blk(*g), 0)),
            pl.BlockSpec((bs, nh, dk, dv), lambda *g: (blk(*g), 0, 0, 0)),
            pl.BlockSpec((bs, nh, dk), lambda *g: (blk(*g), 0, 0)),
            pl.BlockSpec((bs, nh), lambda *g: (blk(*g), 0)),
        ]

    return SideJob(
        n_blocks=nb // bs,
        parts=parts,
        args=(b_i, b_f, z, zg, g_out.reshape(1, nh * dv), c0, n0, m0),
        in_specs=in_specs,
        out_shape=(
            jax.ShapeDtypeStruct((nb, nh * dv), F32),
            jax.ShapeDtypeStruct((nb, nh, dk, dv), F32),
            jax.ShapeDtypeStruct((nb, nh, dk), F32),
            jax.ShapeDtypeStruct((nb, nh), F32),
        ),
        out_specs=out_specs,
        scratch_shapes=(pltpu.VMEM((bs, dv), F32),),
        body=functools.partial(_mlstm_step_body, bs=bs, parts=parts),
    )


def _run_side_job(job, *, vmem_mib, name):
    n_in, n_out = len(job.args), len(job.out_shape)

    def body(*refs):
        for part in range(job.parts):
            job.body(refs[:n_in], refs[n_in:n_in + n_out], refs[n_in + n_out:], part)

    block_of = lambda i: i
    return pl.pallas_call(
        body,
        grid=(job.n_blocks,),
        in_specs=job.in_specs(block_of),
        out_specs=job.out_specs(block_of),
        out_shape=list(job.out_shape),
        scratch_shapes=list(job.scratch_shapes),
        compiler_params=_params(1, vmem_mib),
        name=name,
    )(*job.args)


def _softmax_rows(s):
    e = jnp.exp(s - jnp.max(s, axis=-1, keepdims=True))
    return e / jnp.sum(e, axis=-1, keepdims=True)


def _xattn_block_kernel(*refs, n_mix):
    x_ref = refs[0]
    y_refs = refs[1:1 + n_mix]
    wy_refs = refs[1 + n_mix:1 + 2 * n_mix]
    g_ref, wq_ref, k_ref, v_ref, wo_ref, o_ref = refs[1 + 2 * n_mix:]
    x = x_ref[...]
    for y_ref, wy_ref in zip(y_refs, wy_refs):
        x = x + jnp.dot(y_ref[...], wy_ref[...], preferred_element_type=F32)
    xq = jnp.dot(_rms(x, g_ref[...]).astype(BF16), wq_ref[...],
                 preferred_element_type=F32).astype(BF16)
    heads = []
    for h in range(XA_HEADS):
        sl = slice(h * XA_DH, (h + 1) * XA_DH)
        s = lax.dot_general(xq[:, sl], k_ref[:, sl].astype(BF16),
                            (((1,), (1,)), ((), ())), preferred_element_type=F32)
        p = _softmax_rows(s * (XA_DH ** -0.5))
        heads.append(_mm(p, v_ref[:, sl]).astype(BF16))
    o_ref[...] = x + jnp.dot(jnp.concatenate(heads, axis=1), wo_ref[...],
                             preferred_element_type=F32)


def _xattn_block(x, mix_parts, mix_weights16, g, wq16, mk, mv, wo16, *, batch, seq, tq, name):
    nt = seq // tq
    assert seq % tq == 0 and all(w.dtype == BF16 for w in (*mix_weights16, wq16, wo16))
    assert all(p.dtype == BF16 for p in mix_parts)
    d = D_MODEL
    resident = dict(pipeline_mode=pl.Buffered(1))
    rows = lambda width: pl.BlockSpec((tq, width), lambda b, t: (b * nt + t, 0))
    return pl.pallas_call(
        functools.partial(_xattn_block_kernel, n_mix=len(mix_parts)),
        grid=(batch, nt),
        in_specs=[
            rows(d),
            *[rows(p.shape[1]) for p in mix_parts],
            *[pl.BlockSpec(w.shape, lambda b, t: (0, 0), **resident) for w in mix_weights16],
            pl.BlockSpec((1, d), lambda b, t: (0, 0)),
            pl.BlockSpec((d, d), lambda b, t: (0, 0), **resident),
            pl.BlockSpec((N_MEM, d), lambda b, t: (b, 0)),
            pl.BlockSpec((N_MEM, d), lambda b, t: (b, 0)),
            pl.BlockSpec((d, d), lambda b, t: (0, 0), **resident),
        ],
        out_specs=rows(d),
        out_shape=jax.ShapeDtypeStruct((batch * seq, d), F32),
        compiler_params=_params(2, 60),
        name=name,
    )(x, *mix_parts, *mix_weights16, g.reshape(1, d), wq16, mk, mv, wo16)


def _xattn_own_mask():
    nrow = N_MEM * XA_HEADS
    col_head = lax.broadcasted_iota(jnp.int32, (V7X_SUBLANES, nrow), 1) & (XA_HEADS - 1)
    row_head = lax.broadcasted_iota(jnp.int32, (V7X_SUBLANES, nrow), 0) & (XA_HEADS - 1)
    return col_head == row_head


def _xattn_probs(q_ref, k_ref, j, own):
    kf = k_ref[j].reshape(N_MEM * XA_HEADS, XA_DH)
    q8 = jnp.concatenate([q_ref[j]] * (V7X_SUBLANES // XA_HEADS), axis=0)
    s = lax.dot_general(q8.astype(BF16), kf.astype(BF16), (((1,), (1,)), ((), ())),
                        preferred_element_type=F32)
    s = jnp.where(own, s * (XA_DH ** -0.5), NEG)
    e = jnp.where(own, jnp.exp(s - jnp.max(s, axis=-1, keepdims=True)), 0.0)
    return e / jnp.sum(e, axis=-1, keepdims=True)


def _xattn_values(v_ref, o_ref, j, p):
    vf = v_ref[j].reshape(N_MEM * XA_HEADS, XA_DH)
    o_ref[j] = _mm(p, vf)[0:XA_HEADS, :]


def _xattn_step_kernel(q_ref, k_ref, v_ref, o_ref, *, sb):
    own = _xattn_own_mask()
    for j in range(sb):
        _xattn_values(v_ref, o_ref, j, _xattn_probs(q_ref, k_ref, j, own))


def _xattn_step(xq, ck, cv, *, sb, name):
    nb = xq.shape[0]
    assert XA_HEADS & (XA_HEADS - 1) == 0
    assert nb % sb == 0 and ck.shape == (nb, N_MEM, XA_HEADS, XA_DH)
    cache_spec = pl.BlockSpec((sb, N_MEM, XA_HEADS, XA_DH), lambda i: (i, 0, 0, 0))
    q_spec = pl.BlockSpec((sb, XA_HEADS, XA_DH), lambda i: (i, 0, 0))
    return pl.pallas_call(
        functools.partial(_xattn_step_kernel, sb=sb),
        grid=(nb // sb,),
        in_specs=[q_spec, cache_spec, cache_spec],
        out_specs=q_spec,
        out_shape=jax.ShapeDtypeStruct((nb, XA_HEADS, XA_DH), F32),
        compiler_params=_params(1, 40),
        name=name,
    )(xq.reshape(nb, XA_HEADS, XA_DH), ck, cv).reshape(nb, XA_HEADS * XA_DH)


class Tiles(NamedTuple):
    rows: int
    attn_rows: int
    mem_rows: int
    cols: int
    ffn_cols16: int
    ffn_cols32: int
    step_block: int
    step_parts: int
    xattn_rows: int


def _plan_tiles(prompt_rows, mem_rows):
    rows = min(prompt_rows, 1024)
    assert prompt_rows % rows == 0
    return Tiles(rows=rows, attn_rows=min(rows, 512), mem_rows=min(mem_rows, 512), cols=1024,
                 ffn_cols16=512, ffn_cols32=256, step_block=V7X_SUBLANES, step_parts=2,
                 xattn_rows=2)


def kernel(x_prompt, x_sample, mem_prompt, state_rg_h, state_rg_conv, state_ml_C, state_ml_n, state_ml_m, cache_mem_k, cache_mem_v, g_mix, w_in, conv_w, conv_b, w_rg_a, b_rg_a, w_rg_x, b_rg_x, rg_lambda, b_ml_i, b_ml_f, g_rg_out, g_ml_out, w_out, g_xa, g_mem, w_xa_q, w_xa_k, w_xa_v, w_xa_o, g_ffn, w_ffn_gate, w_ffn_up, w_ffn_down, g_final):
    depth = g_mix.shape[0]
    assert depth == 1, "single trunk layer"
    bp, seq, d = x_prompt.shape
    bs_, dec_seq, _ = x_sample.shape
    assert d == D_MODEL and dec_seq == 1
    n_mem = mem_prompt.shape[1]
    assert n_mem == N_MEM
    dff = w_ffn_gate.shape[-1]
    in_w = w_in.shape[-1]
    assert in_w == IN_MAIN + N_GATE

    w_in_t = jnp.swapaxes(w_in, 1, 2).reshape(in_w, d)
    w_gate_pad = jnp.pad(w_in_t[IN_MAIN:], ((0, V7X_LANES - N_GATE), (0, 0)))
    cw = conv_w.reshape(CONV_W, RG_WIDTH)
    wgate = jnp.concatenate([w_rg_a.reshape(RG_BLOCKS, RG_BLOCK, RG_BLOCK),
                             w_rg_x.reshape(RG_BLOCKS, RG_BLOCK, RG_BLOCK)], axis=-1)
    rg_args = (cw, conv_b.reshape(-1), wgate, b_rg_a.reshape(-1), b_rg_x.reshape(-1),
               rg_lambda.reshape(-1), g_rg_out.reshape(-1))
    b_i = b_ml_i.reshape(ML_HEADS)
    b_f = b_ml_f.reshape(ML_HEADS)
    g_ml = g_ml_out.reshape(-1)
    w_out2 = w_out.reshape(d, d)
    w_q = w_xa_q.reshape(d, d)
    w_k = w_xa_k.reshape(d, d)
    w_v = w_xa_v.reshape(d, d)
    w_o = w_xa_o.reshape(d, d)
    w_fg = w_ffn_gate.reshape(d, dff)
    w_fu = w_ffn_up.reshape(d, dff)
    w_fd = w_ffn_down.reshape(dff, d)

    tiles = _plan_tiles(bp * seq, bp * n_mem)
    row_tile = tiles.rows
    rg_zero_conv = jnp.zeros((bp, CONV_W - 1, RG_WIDTH), F32)
    rg_zero_h = jnp.zeros((bp, RG_WIDTH), F32)

    xs = x_sample.reshape(bs_, d)
    z_s, zg_s, w_in16 = _norm_linear(
        xs, g_mix.reshape(-1), w_in_t, n_out=IN_MAIN, tm=bs_, tn=tiles.cols, w_gate=w_gate_pad,
        w_is_nk=True, emit_w16=True, name="in_proj_s")
    conv_taps = jnp.swapaxes(state_rg_conv.reshape(bs_, CONV_W - 1, RG_WIDTH), 0, 1)
    y_rg_s, s_h, s_conv_taps = _rglru_step(z_s, conv_taps, state_rg_h.reshape(bs_, RG_WIDTH),
                                           *rg_args, name="rglru_s")
    s_conv = jnp.swapaxes(s_conv_taps, 0, 1)
    mlstm_s_job = _mlstm_step_job(
        z_s, zg_s, state_ml_C.reshape(bs_, ML_HEADS, ML_DK, ML_DV),
        state_ml_n.reshape(bs_, ML_HEADS, ML_DK), state_ml_m.reshape(bs_, ML_HEADS),
        b_i, b_f, g_ml, bs=tiles.step_block, parts=tiles.step_parts)
    tp = bp * seq
    xp = x_prompt.reshape(tp, d)
    in_cols, in_rows = IN_MAIN // tiles.cols, tp // row_tile
    if (mlstm_s_job.n_blocks % in_rows == 0
            and mlstm_s_job.n_blocks // in_rows * mlstm_s_job.parts < in_cols):
        z_p, zg_p, y_ml_s, s_c, s_n, s_m = _norm_linear(
            xp, g_mix.reshape(-1), w_in16, n_out=IN_MAIN, tm=row_tile, tn=tiles.cols,
            w_gate=w_gate_pad, w_is_nk=True, side=mlstm_s_job, x_parts=4, vmem_mib=60,
            name="in_proj_p")
    else:
        z_p, zg_p = _norm_linear(xp, g_mix.reshape(-1), w_in16, n_out=IN_MAIN, tm=row_tile,
                                 tn=tiles.cols, w_gate=w_gate_pad, w_is_nk=True, name="in_proj_p")
        y_ml_s, s_c, s_n, s_m = _run_side_job(mlstm_s_job, vmem_mib=32, name="mlstm_s")
    x1_s, w_out16_rg, w_out16_ml = _linear_residual(
        [y_rg_s, y_ml_s], [(w_out2, 0), (w_out2, 1)], xs, tm=bs_, tn=tiles.cols, emit_w16=True,
        name="mix_out_s")
    xq_s, w_q16 = _norm_linear(x1_s, g_xa.reshape(-1), w_q, n_out=d, tm=bs_, tn=tiles.cols,
                               emit_w16=True, name="xa_q_s")
    ck = cache_mem_k.reshape(bs_, n_mem, XA_HEADS, XA_DH)
    cv = cache_mem_v.reshape(bs_, n_mem, XA_HEADS, XA_DH)

    mix_tl = min(seq, ML_CHUNK)
    mix_steps = bp * (seq // mix_tl)
    mix_args = (z_p, zg_p, rg_zero_conv, rg_zero_h, *rg_args, b_i, b_f, g_ml)
    if bs_ % mix_steps == 0 and RG_BLOCKS % (bs_ // mix_steps) == 0:
        y_rg_p, p_h, p_conv, y_ml_p, p_c, p_n, p_m, o_s = _mixer_seq(
            *mix_args, batch=bp, seq=seq, tl=mix_tl, side_xattn=(xq_s, ck, cv), name="mixer_p")
        o_s = o_s.reshape(bs_, d)
    else:
        y_rg_p, p_h, p_conv, y_ml_p, p_c, p_n, p_m = _mixer_seq(
            *mix_args, batch=bp, seq=seq, tl=mix_tl, name="mixer_p")
        o_s = _xattn_step(xq_s, ck, cv, sb=tiles.xattn_rows, name="xattn_s")

    x3_s, w_o16 = _linear_residual([o_s], [(w_o, 0)], x1_s, tm=bs_, tn=tiles.cols, emit_w16=True,
                                   name="xa_out_s")

    mem2 = mem_prompt.reshape(bp * n_mem, d)
    tmem = tiles.mem_rows
    mk, mk_heads = _norm_linear(mem2, g_mem.reshape(-1), w_k, n_out=d, tm=tmem, tn=d,
                                out_dtype=BF16, heads_out=(XA_HEADS, XA_DH), name="mem_k")
    mv, mv_heads = _norm_linear(mem2, g_mem.reshape(-1), w_v, n_out=d, tm=tmem, tn=d,
                                out_dtype=BF16, heads_out=(XA_HEADS, XA_DH), name="mem_v")
    x3_p = _xattn_block(xp, [y_rg_p, y_ml_p], [w_out16_rg, w_out16_ml], g_xa.reshape(-1), w_q16,
                        mk, mv, w_o16, batch=bp, seq=seq, tq=min(seq, tiles.attn_rows), name="xattn_p")
    y_s, y_p0, w_fg16, w_fu16, w_fd16 = _ffn(
        [(x3_s, bs_), (x3_p, row_tile)], g_ffn.reshape(-1), w_fg, w_fu, w_fd, g_final, n_rows=1,
        tf=tiles.ffn_cols32, emit_w16=True, vmem_mib=60, name="ffn_s")
    if tp // row_tile > 1:
        y_p = _ffn([(x3_p, row_tile)], g_ffn.reshape(-1), w_fg16, w_fu16, w_fd16, g_final,
                   n_rows=tp // row_tile, tf=tiles.ffn_cols16, head_tile=y_p0, vmem_mib=60, name="ffn_p")
    else:
        y_p = y_p0

    return (
        y_p.reshape(bp, seq, d),
        y_s.reshape(bs_, 1, d),
        p_h.reshape(1, bp, RG_WIDTH),
        p_conv.reshape(1, bp, CONV_W - 1, RG_WIDTH),
        p_c.reshape(1, bp, ML_HEADS, ML_DK, ML_DV),
        p_n.reshape(1, bp, ML_HEADS, ML_DK),
        p_m[:, :, 0].reshape(1, bp, ML_HEADS),
        mk_heads.reshape(1, bp, n_mem, XA_HEADS, XA_DH),
        mv_heads.reshape(1, bp, n_mem, XA_HEADS, XA_DH),
        s_h.reshape(1, bs_, RG_WIDTH),
        s_conv.reshape(1, bs_, CONV_W - 1, RG_WIDTH),
        s_c.reshape(1, bs_, ML_HEADS, ML_DK, ML_DV),
        s_n.reshape(1, bs_, ML_HEADS, ML_DK),
        s_m.reshape(1, bs_, ML_HEADS),
    )
```

```python
import functools
from typing import Callable, NamedTuple

import jax
import jax.numpy as jnp
from jax import lax
from jax.experimental import pallas as pl
from jax.experimental.pallas import tpu as pltpu

F32 = jnp.float32
BF16 = jnp.bfloat16

D_MODEL = 2048
RG_WIDTH = D_MODEL // 2
RG_BLOCKS = 8
RG_BLOCK = RG_WIDTH // RG_BLOCKS
CONV_W = 4
RG_C = 8.0
ML_HEADS = 4
ML_WIDTH = D_MODEL - RG_WIDTH
ML_DV = ML_WIDTH // ML_HEADS
ML_DK = ML_DV // 2
N_MEM = 256
XA_HEADS = 4
XA_DH = D_MODEL // XA_HEADS
EPS = 1e-6
NEG = -1e30

OFF_RGX = 0
OFF_RGG = OFF_RGX + RG_WIDTH
OFF_Q = OFF_RGG + RG_WIDTH
OFF_K = OFF_Q + ML_HEADS * ML_DK
OFF_V = OFF_K + ML_HEADS * ML_DK
OFF_O = OFF_V + ML_WIDTH
OFF_I = OFF_O + ML_WIDTH
IN_MAIN = OFF_I
N_GATE = 2 * ML_HEADS

V7X_LANES = 128
V7X_SUBLANES = 8
V7X_VMEM_BYTES = 64 * 2**20

ML_CHUNK = 256


class SideJob(NamedTuple):
    n_blocks: int
    parts: int
    args: tuple
    in_specs: Callable
    out_shape: tuple
    out_specs: Callable
    scratch_shapes: tuple
    body: Callable


def _params(n_axes, vmem_mib):
    assert vmem_mib * 2**20 <= V7X_VMEM_BYTES
    return pltpu.CompilerParams(
        dimension_semantics=("arbitrary",) * n_axes,
        vmem_limit_bytes=vmem_mib * 2**20,
    )


def _rms(x, g):
    ms = jnp.mean(x * x, axis=-1, keepdims=True)
    return x * lax.rsqrt(ms + EPS) * g


def _softplus(u):
    return jnp.maximum(u, 0.0) + jnp.log1p(jnp.exp(-jnp.abs(u)))


def _log_sigmoid(u):
    return -_softplus(-u)


def _sigmoid(u):
    return 0.5 * jnp.tanh(0.5 * u) + 0.5


def _gelu_tanh(x):
    c = 0.7978845608028654
    half_x = 0.5 * x
    return half_x + half_x * jnp.tanh(x * (c + (c * 0.044715) * (x * x)))


def _sqrt_nonneg(v):
    return jnp.where(v > 0.0, v * lax.rsqrt(v), 0.0)


def _mm(a, b):
    return jnp.dot(a.astype(BF16), b.astype(BF16), preferred_element_type=F32)


def _exact_cumsum_lanes(x):
    n = x.shape[-1]
    upper = jnp.where(lax.broadcasted_iota(jnp.int32, (n, n), 0)
                      <= lax.broadcasted_iota(jnp.int32, (n, n), 1), 1.0, 0.0).astype(BF16)
    hi = x.astype(BF16)
    rest = x - hi.astype(F32)
    mid = rest.astype(BF16)
    lo = (rest - mid.astype(F32)).astype(BF16)
    return (jnp.dot(hi, upper, preferred_element_type=F32)
            + jnp.dot(mid, upper, preferred_element_type=F32)
            + jnp.dot(lo, upper, preferred_element_type=F32))


def _dot_w(a, w_ref, w_is_nk):
    w = w_ref[...].astype(BF16)
    if w_is_nk:
        return lax.dot_general(a, w, (((1,), (1,)), ((), ())), preferred_element_type=F32)
    return jnp.dot(a, w, preferred_element_type=F32)


def _norm_linear_kernel(*refs, x_parts, with_gate, w_is_nk, emit_w16, heads_out, side, side_cols):
    x_refs, refs = list(refs[:x_parts]), list(refs[x_parts - 1:])
    n_si, n_so, n_ss = ((len(side.args), len(side.out_shape), len(side.scratch_shapes))
                        if side else (0, 0, 0))
    n_in = 3 + with_gate + n_si
    n_out = 1 + bool(heads_out) + with_gate + emit_w16 + n_so
    g_ref, w_ref = refs[1:3]
    wg_ref = refs[3] if with_gate else None
    side_in = refs[3 + with_gate:n_in]
    outs = refs[n_in:n_in + n_out]
    xn_ref = refs[n_in + n_out]
    side_scratch = refs[n_in + n_out + 1:]
    o_ref = outs[0]
    oh_ref = outs[1] if heads_out else None
    og_ref = outs[1 + bool(heads_out)] if with_gate else None
    w16_ref = outs[n_out - n_so - 1] if emit_w16 else None
    side_out = outs[n_out - n_so:]
    col = pl.program_id(1)

    def column_tile(xn):
        acc = _dot_w(xn, w_ref, w_is_nk)
        o_ref[...] = acc.astype(o_ref.dtype)
        if heads_out:
            oh_ref[...] = acc.reshape(oh_ref.shape)
        if emit_w16:
            w16_ref[...] = w_ref[...].astype(BF16)

    @pl.when(col == 0)
    def _():
        if x_parts == 1:
            xn = _rms(x_refs[0][...], g_ref[...]).astype(BF16)
            xn_ref[...] = xn
        else:
            part = xn_ref.shape[0] // x_parts
            for q, x_ref in enumerate(x_refs):
                xn_ref[q * part:(q + 1) * part, :] = _rms(x_ref[...], g_ref[...]).astype(BF16)
            xn = xn_ref[...]
        if with_gate:
            og_ref[...] = _dot_w(xn, wg_ref, w_is_nk)
        column_tile(xn)

    if side:
        for part in range(side.parts):
            @pl.when((col > 0) & (col <= side_cols) & ((col - 1) % side.parts == part))
            def _(part=part):
                column_tile(xn_ref[...])
                side.body(side_in, side_out, side_scratch, part)

    @pl.when(col > (side_cols if side else 0))
    def _():
        column_tile(xn_ref[...])


def _norm_linear(x, g, w, *, n_out, tm, tn, out_dtype=F32, w_gate=None, w_is_nk=False,
                 emit_w16=False, heads_out=None, side=None, x_parts=1, vmem_mib=48, name):
    m, k = x.shape
    k_ax, n_ax = (1, 0) if w_is_nk else (0, 1)
    assert m % tm == 0 and n_out % tn == 0 and w.shape[k_ax] == k and n_out <= w.shape[n_ax]
    assert not emit_w16 or m == tm
    assert heads_out is None or (tn == n_out and heads_out[0] * heads_out[1] == n_out)
    n_rows, n_cols = m // tm, n_out // tn
    with_gate = w_gate is not None
    w_mode = dict(pipeline_mode=pl.Buffered(1)) if tn == n_out else {}
    w_spec = (pl.BlockSpec((tn, k), lambda i, j: (j, 0), **w_mode) if w_is_nk
              else pl.BlockSpec((k, tn), lambda i, j: (0, j), **w_mode))
    assert tm % x_parts == 0 and x_parts <= n_cols
    if x_parts == 1:
        x_specs = [pl.BlockSpec((tm, k), lambda i, j: (i, 0))]
    else:
        x_specs = [
            pl.BlockSpec((tm // x_parts, k),
                         lambda i, j, q=q: (jnp.minimum(i + (j >= n_cols - (x_parts - 1) + q),
                                                        n_rows - 1) * x_parts + q, 0))
            for q in range(x_parts)]
    in_specs = [
        *x_specs,
        pl.BlockSpec((1, k), lambda i, j: (0, 0)),
        w_spec,
    ]
    out_shape = [jax.ShapeDtypeStruct((m, n_out), out_dtype)]
    out_specs = [pl.BlockSpec((tm, tn), lambda i, j: (i, j))]
    scratch_shapes = [pltpu.VMEM((tm, k), BF16)]
    args = [*([x] * x_parts), g.reshape(1, k), w]
    if heads_out:
        out_shape.append(jax.ShapeDtypeStruct((m, *heads_out), F32))
        out_specs.append(pl.BlockSpec((tm, *heads_out), lambda i, j: (i, 0, 0)))
    if with_gate:
        ng = w_gate.shape[n_ax]
        in_specs.append(pl.BlockSpec(w_gate.shape, lambda i, j: (0, 0)))
        out_shape.append(jax.ShapeDtypeStruct((m, ng), F32))
        out_specs.append(pl.BlockSpec((tm, ng), lambda i, j: (i, 0)))
        args.append(w_gate)
    if emit_w16:
        out_shape.append(jax.ShapeDtypeStruct((n_out, k) if w_is_nk else (k, n_out), BF16))
        out_specs.append(pl.BlockSpec((tn, k), lambda i, j: (j, 0)) if w_is_nk
                         else pl.BlockSpec((k, tn), lambda i, j: (0, j)))
    side_cols = 0
    if side:
        assert side.n_blocks % n_rows == 0
        blocks_per_row = side.n_blocks // n_rows
        side_cols = blocks_per_row * side.parts
        assert 0 < side_cols < n_cols
        block_of = lambda i, j: (i * blocks_per_row
                                 + jnp.clip(j - 1, 0, side_cols - 1) // side.parts)
        in_specs += side.in_specs(block_of)
        out_specs += side.out_specs(block_of)
        out_shape += list(side.out_shape)
        scratch_shapes += list(side.scratch_shapes)
        args += list(side.args)
    out = pl.pallas_call(
        functools.partial(_norm_linear_kernel, x_parts=x_parts, with_gate=with_gate, w_is_nk=w_is_nk,
                          emit_w16=emit_w16, heads_out=heads_out, side=side, side_cols=side_cols),
        grid=(n_rows, n_cols),
        in_specs=in_specs,
        out_specs=out_specs,
        out_shape=out_shape,
        scratch_shapes=scratch_shapes,
        compiler_params=_params(2, vmem_mib),
        name=name,
    )(*args)
    return out if len(out) > 1 else out[0]


def _linear_res_kernel(*refs, n_in, emit_w16):
    a_refs = refs[:n_in]
    w_refs = refs[n_in:2 * n_in]
    res_ref = refs[2 * n_in]
    o_ref = refs[2 * n_in + 1]
    acc = res_ref[...]
    for a_ref, w_ref in zip(a_refs, w_refs):
        acc = acc + _mm(a_ref[...], w_ref[...])
    o_ref[...] = acc
    if emit_w16:
        for w_ref, wc_ref in zip(w_refs, refs[2 * n_in + 2:]):
            wc_ref[...] = w_ref[...].astype(BF16)


def _linear_residual(parts, weights, res, *, tm, tn, emit_w16=False, vmem_mib=48, name):
    m, n = res.shape
    kp = parts[0].shape[1]
    assert all(p.shape == (m, kp) for p in parts) and len(weights) == len(parts)
    assert m % tm == 0 and n % tn == 0 and (not emit_w16 or m == tm)
    n_in = len(parts)
    in_specs = [pl.BlockSpec((tm, kp), lambda i, j: (i, 0)) for _ in parts]
    w_mode = dict(pipeline_mode=pl.Buffered(1)) if tn == n else {}
    in_specs += [pl.BlockSpec((kp, tn), lambda i, j, rb=rb: (rb, j), **w_mode)
                 for _, rb in weights]
    in_specs.append(pl.BlockSpec((tm, tn), lambda i, j: (i, j)))
    out_specs = [pl.BlockSpec((tm, tn), lambda i, j: (i, j))]
    out_shape = [jax.ShapeDtypeStruct((m, n), F32)]
    if emit_w16:
        out_specs += [pl.BlockSpec((kp, tn), lambda i, j: (0, j)) for _ in parts]
        out_shape += [jax.ShapeDtypeStruct((kp, n), BF16) for _ in parts]
    out = pl.pallas_call(
        functools.partial(_linear_res_kernel, n_in=n_in, emit_w16=emit_w16),
        grid=(m // tm, n // tn),
        in_specs=in_specs,
        out_specs=out_specs,
        out_shape=out_shape,
        compiler_params=_params(2, vmem_mib),
        name=name,
    )(*parts, *[w for w, _ in weights], res)
    return out if emit_w16 else out[0]


FFN_HEAD_SLABS = 8


def _ffn_kernel(*refs, n_groups, x_parts, emit_w16, head_tile):
    x_part_refs, refs = refs[:x_parts], refs[x_parts - 1:]
    x_refs = refs[:n_groups]
    g_ref, wg_ref, wu_ref, wd_ref, gf_ref = refs[n_groups:n_groups + 5]
    n_in = n_groups + 5 + bool(head_tile)
    head_ref = refs[n_in - 1] if head_tile else None
    o_refs = refs[n_in:n_in + n_groups]
    w16_refs = refs[n_in + n_groups:n_in + n_groups + 3] if emit_w16 else ()
    xf_refs = refs[-n_groups:]
    row = pl.program_id(0)
    f = pl.program_id(1)
    last = pl.num_programs(1) - 1
    computing = (row > 0) if head_tile else True

    def weights():
        wg = wg_ref[...].astype(BF16)
        wu = wu_ref[...].astype(BF16)
        wd = wd_ref[...].astype(BF16)
        for dst, val in zip(w16_refs, (wg, wu, wd)):
            dst[...] = val
        return wg, wu, wd

    def hidden_tile(xf, w):
        wg, wu, wd = w
        gate = jnp.dot(xf, wg, preferred_element_type=F32)
        up = jnp.dot(xf, wu, preferred_element_type=F32)
        hidden = (gate * _sigmoid(gate)) * up
        return jnp.dot(hidden.astype(BF16), wd, preferred_element_type=F32)

    if head_tile:
        slab = head_ref.shape[0]

        @pl.when((row == 0) & (f < FFN_HEAD_SLABS))
        def _():
            o_refs[0][pl.ds(pl.multiple_of(f * slab, slab), slab), :] = head_ref[...]

    @pl.when(computing & (f == 0))
    def _():
        w = weights()
        if x_parts > 1:
            part = xf_refs[0].shape[0] // x_parts
            for q, x_ref in enumerate(x_part_refs):
                x = x_ref[...]
                xf_refs[0][q * part:(q + 1) * part, :] = _rms(x, g_ref[...]).astype(BF16)
                o_refs[0][q * part:(q + 1) * part, :] = x
            o_refs[0][...] += hidden_tile(xf_refs[0][...], w)
        else:
            for x_ref, o_ref, xf_ref in zip(x_refs, o_refs, xf_refs):
                x = x_ref[...]
                xf = _rms(x, g_ref[...]).astype(BF16)
                xf_ref[...] = xf
                o_ref[...] = x + hidden_tile(xf, w)

    @pl.when(computing & (f > 0) & (f < last))
    def _():
        w = weights()
        for o_ref, xf_ref in zip(o_refs, xf_refs):
            o_ref[...] += hidden_tile(xf_ref[...], w)

    @pl.when(computing & (f == last))
    def _():
        w = weights()
        for o_ref, xf_ref in zip(o_refs, xf_refs):
            o_ref[...] = _rms(o_ref[...] + hidden_tile(xf_ref[...], w), gf_ref[...])


def _ffn(groups, g, w_gate, w_up, w_down, g_final, *, n_rows, tf, emit_w16=False, head_tile=None,
         x_parts=1, vmem_mib, name):
    d = groups[0][0].shape[1]
    dff = w_gate.shape[1]
    n_groups = len(groups)
    assert dff % tf == 0 and dff // tf >= 2 and (not emit_w16 or n_rows == 1)
    assert head_tile is None or (n_groups == 1 and head_tile.shape == (groups[0][1], d))
    assert all(x.shape[0] >= n_rows * tm for x, tm in groups)
    col = (lambda i, f: jnp.where(i > 0, f, 0)) if head_tile is not None else (lambda i, f: f)
    up_spec = pl.BlockSpec((d, tf), lambda i, f: (0, col(i, f)))
    down_spec = pl.BlockSpec((tf, d), lambda i, f: (col(i, f), 0))
    vec_spec = pl.BlockSpec((1, d), lambda i, f: (0, 0))
    row_specs = [pl.BlockSpec((tm, d), lambda i, f: (i, 0)) for _, tm in groups]
    x_specs, n_f = row_specs, dff // tf
    if x_parts > 1:
        tm0 = groups[0][1]
        assert n_groups == 1 and tm0 % x_parts == 0 and x_parts <= n_f
        x_specs = [
            pl.BlockSpec((tm0 // x_parts, d),
                         lambda i, f, q=q: (jnp.minimum(i + (f >= n_f - (x_parts - 1) + q),
                                                        n_rows - 1) * x_parts + q, 0))
            for q in range(x_parts)]
    in_specs = x_specs + [vec_spec, up_spec, up_spec, down_spec, vec_spec]
    args = ([x for x, _ in groups] * x_parts
            + [g.reshape(1, d), w_gate, w_up, w_down, g_final.reshape(1, d)])
    if head_tile is not None:
        tm0 = groups[0][1]
        assert tm0 % FFN_HEAD_SLABS == 0 and dff // tf >= FFN_HEAD_SLABS
        last_slab = FFN_HEAD_SLABS - 1
        in_specs.append(pl.BlockSpec(
            (tm0 // FFN_HEAD_SLABS, d),
            lambda i, f: (jnp.where(i == 0, jnp.minimum(f, last_slab), last_slab), 0)))
        args.append(head_tile)
    out_specs = list(row_specs)
    out_shape = [jax.ShapeDtypeStruct((n_rows * tm, d), F32) for _, tm in groups]
    if emit_w16:
        out_specs += [pl.BlockSpec((d, tf), lambda i, f: (0, f))] * 2
        out_specs += [pl.BlockSpec((tf, d), lambda i, f: (f, 0))]
        out_shape += [jax.ShapeDtypeStruct(w.shape, BF16) for w in (w_gate, w_up, w_down)]
    out = pl.pallas_call(
        functools.partial(_ffn_kernel, n_groups=n_groups, x_parts=x_parts, emit_w16=emit_w16,
                          head_tile=head_tile is not None),
        grid=(n_rows, dff // tf),
        in_specs=in_specs,
        out_specs=out_specs,
        out_shape=out_shape,
        scratch_shapes=[pltpu.VMEM((tm, d), BF16) for _, tm in groups],
        compiler_params=_params(2, vmem_mib),
        name=name,
    )(*args)
    return out if len(out) > 1 else out[0]


RG_GATE_LOOKAHEAD = 2


def _rg_gates(xr, wgate_ref, ba_ref, bx_ref, lam_ref, a_ref, b_ref, side=None):
    blocks = [slice(n * RG_BLOCK, (n + 1) * RG_BLOCK) for n in range(RG_BLOCKS)]
    gates = {}

    def issue_gate(n):
        if n < RG_BLOCKS:
            gates[n] = _mm(xr[:, blocks[n]], wgate_ref[n])

    rows, score_phase, value_phase = side if side else (0, None, None)
    per = RG_BLOCKS // rows if rows else RG_BLOCKS
    scores = {}
    for n in range(RG_GATE_LOOKAHEAD):
        issue_gate(n)
    for n, sl in enumerate(blocks):
        if rows and n % per == 0:
            j = n // per
            scores[j] = score_phase(j)
            if j > 0:
                value_phase(j - 1, scores.pop(j - 1))
        xn = xr[:, sl]
        g = gates.pop(n)
        issue_gate(n + RG_GATE_LOOKAHEAD)
        r = _sigmoid(g[:, :RG_BLOCK] + ba_ref[:, sl])
        ig = _sigmoid(g[:, RG_BLOCK:] + bx_ref[:, sl])
        a = jnp.exp(r * (-RG_C * _softplus(-lam_ref[:, sl])))
        a_ref[:, sl] = a
        mult = _sqrt_nonneg(jnp.maximum(1.0 - a * a, 0.0))
        b_ref[:, sl] = mult * (ig * xn)
    if rows:
        value_phase(rows - 1, scores.pop(rows - 1))


def _mlstm_chunk(q, k, v, o_gate, li_col, li_row, lf_row, bcum_row, causal, g, c_state, n_state, m):
    cs = q.shape[0]
    bcum_col = jnp.sum(jnp.where(causal, lf_row, 0.0), axis=1, keepdims=True)
    log_d = jnp.where(causal, bcum_col - bcum_row + li_row, NEG)
    inter = bcum_col + m
    m_t = jnp.maximum(inter, jnp.max(log_d, axis=1, keepdims=True))
    dmat = jnp.exp(log_d - m_t)
    sc = jnp.exp(inter - m_t)
    qb = q.astype(BF16)
    kb = k.astype(BF16)
    vb = v.astype(BF16)
    qk = lax.dot_general(qb, kb, (((1,), (1,)), ((), ())), preferred_element_type=F32) * dmat
    num = sc * jnp.dot(qb, c_state.astype(BF16), preferred_element_type=F32) + _mm(qk, vb)
    den = sc * jnp.sum(q * n_state, axis=1, keepdims=True) + jnp.sum(qk, axis=1, keepdims=True)
    den = jnp.maximum(jnp.abs(den), jnp.exp(-m_t))
    hh = num / den
    m_new = m_t[cs - 1:cs, :]
    b_last = bcum_col[cs - 1:cs, :]
    w_end = jnp.exp(b_last - bcum_col + li_col - m_new)
    dec = jnp.exp(b_last + m - m_new)
    wk = w_end * k
    c_new = dec * c_state + lax.dot_general(
        wk.astype(BF16), vb, (((0,), (0,)), ((), ())), preferred_element_type=F32)
    n_new = dec * n_state + jnp.sum(wk, axis=0, keepdims=True)
    y = _rms(hh, g) * _sigmoid(o_gate)
    return y, c_new, n_new, m_new


N_RG_IN = 10
N_ML_IN = 4


def _mixer_seq_kernel(*refs, tl, side_rows):
    (z_ref, conv0_ref, h0_ref, cw_ref, cb_ref, wgate_ref, ba_ref, bx_ref, lam_ref,
     gout_ref) = refs[:N_RG_IN]
    bi_ref, bf_ref, gates_ref, gml_ref = refs[N_RG_IN:N_RG_IN + N_ML_IN]
    n_in = N_RG_IN + N_ML_IN + (3 if side_rows else 0)
    if side_rows:
        sq_ref, sk_ref, sv_ref = refs[N_RG_IN + N_ML_IN:n_in]
    y_ref, hlast_ref, convn_ref, yml_ref, cout_ref, nout_ref, mout_ref = refs[n_in:n_in + 7]
    so_ref = refs[n_in + 7] if side_rows else None
    xe_ref, a_ref, b_ref, h_ref, hc_ref, lf_ref, cst_ref, nst_ref, mst_ref = refs[-9:]
    t = pl.program_id(1)
    pad = V7X_SUBLANES

    @pl.when(t == 0)
    def _():
        xe_ref[pad - 3:pad, :] = conv0_ref[...]
        hc_ref[...] = h0_ref[...]
        cst_ref[...] = jnp.zeros_like(cst_ref)
        nst_ref[...] = jnp.zeros_like(nst_ref)
        mst_ref[...] = jnp.zeros_like(mst_ref)
        lf_ref[...] = jnp.zeros_like(lf_ref)

    @pl.when(t > 0)
    def _():
        xe_ref[pad - 3:pad, :] = xe_ref[tl + pad - 3:tl + pad, :]

    x = z_ref[:, OFF_RGX:OFF_RGX + RG_WIDTH]
    xe_ref[pad:tl + pad, :] = x
    xr = (xe_ref[pad - 3:tl + pad - 3, :] * cw_ref[0:1, :]
          + xe_ref[pad - 2:tl + pad - 2, :] * cw_ref[1:2, :]
          + xe_ref[pad - 1:tl + pad - 1, :] * cw_ref[2:3, :]
          + x * cw_ref[3:4, :]) + cb_ref[...]
    side = None
    if side_rows:
        own = _xattn_own_mask()
        side = (side_rows,
                lambda j: _xattn_probs(sq_ref, sk_ref, j, own),
                lambda j, p: _xattn_values(sv_ref, so_ref, j, p))

    _rg_gates(xr, wgate_ref, ba_ref, bx_ref, lam_ref, a_ref, b_ref, side=side)

    gates = gates_ref[...]
    gates_t = gates.T
    for h in range(ML_HEADS):
        lf_ref[h:h + 1, :] = _log_sigmoid(gates_t[ML_HEADS + h:ML_HEADS + h + 1, :] + bf_ref[h])
    lf_rows = lf_ref[...]
    bcum_rows = _exact_cumsum_lanes(lf_rows)
    causal = (lax.broadcasted_iota(jnp.int32, (tl, tl), 1)
              <= lax.broadcasted_iota(jnp.int32, (tl, tl), 0))
    for h in range(ML_HEADS):
        vo = slice(h * ML_DV, (h + 1) * ML_DV)
        col = lambda off, width: slice(off + h * width, off + (h + 1) * width)
        y_h, c_new, n_new, m_new = _mlstm_chunk(
            z_ref[:, col(OFF_Q, ML_DK)], z_ref[:, col(OFF_K, ML_DK)] * (ML_DK ** -0.5),
            z_ref[:, col(OFF_V, ML_DV)], z_ref[:, col(OFF_O, ML_DV)],
            gates[:, h:h + 1] + bi_ref[h], gates_t[h:h + 1, :] + bi_ref[h],
            lf_rows[h:h + 1, :], bcum_rows[h:h + 1, :], causal, gml_ref[:, vo],
            cst_ref[h], nst_ref[h:h + 1, :], mst_ref[h:h + 1, 0:1])
        cst_ref[h] = c_new
        nst_ref[h:h + 1, :] = n_new
        mst_ref[h:h + 1, :] = jnp.broadcast_to(m_new, (1, V7X_LANES))
        yml_ref[:, vo] = y_h.astype(yml_ref.dtype)
    cout_ref[...] = cst_ref[...]
    nout_ref[...] = nst_ref[...]
    mout_ref[...] = mst_ref[...]

    row = lax.broadcasted_iota(jnp.int32, (V7X_SUBLANES, RG_WIDTH), 0)

    def group(gi, hc):
        r0 = pl.multiple_of(gi * V7X_SUBLANES, V7X_SUBLANES)
        a8 = a_ref[pl.ds(r0, V7X_SUBLANES), :]
        b8 = b_ref[pl.ds(r0, V7X_SUBLANES), :]
        for d in (1, 2, 4):
            keep = row >= d
            b8 = jnp.where(keep, a8 * pltpu.roll(b8, d, axis=0) + b8, b8)
            a8 = jnp.where(keep, a8 * pltpu.roll(a8, d, axis=0), a8)
        h8 = a8 * hc + b8
        h_ref[pl.ds(r0, V7X_SUBLANES), :] = h8
        return h8[V7X_SUBLANES - 1:V7X_SUBLANES, :]

    hc = lax.fori_loop(0, tl // V7X_SUBLANES, group, hc_ref[...], unroll=4)
    hc_ref[...] = hc
    hlast_ref[...] = hc
    convn_ref[...] = xe_ref[tl + pad - 3:tl + pad, :]
    y = h_ref[...] * _gelu_tanh(z_ref[:, OFF_RGG:OFF_RGG + RG_WIDTH])
    y_ref[...] = _rms(y, gout_ref[...]).astype(y_ref.dtype)


def _mixer_seq(z, zgates, conv0, h0, cw, cb, wgate, ba, bx, lam, gout, b_i, b_f, g_ml, *,
               batch, seq, tl, name, side_xattn=None):
    nt = seq // tl
    assert seq % tl == 0 and zgates.shape[1] == V7X_LANES
    w = RG_WIDTH
    dk, dv, nh = ML_DK, ML_DV, ML_HEADS
    row = lambda v: v.reshape(1, w)
    const2 = lambda b, t: (0, 0)
    smem = pl.BlockSpec(memory_space=pltpu.SMEM)
    tile = lambda width: pl.BlockSpec((tl, width), lambda b, t: (b * nt + t, 0))
    in_specs = [
        tile(IN_MAIN),
        pl.BlockSpec((None, CONV_W - 1, w), lambda b, t: (b, 0, 0)),
        pl.BlockSpec((None, 1, w), lambda b, t: (b, 0, 0)),
        pl.BlockSpec((CONV_W, w), const2),
        pl.BlockSpec((1, w), const2),
        pl.BlockSpec((RG_BLOCKS, RG_BLOCK, 2 * RG_BLOCK), lambda b, t: (0, 0, 0)),
        pl.BlockSpec((1, w), const2),
        pl.BlockSpec((1, w), const2),
        pl.BlockSpec((1, w), const2),
        pl.BlockSpec((1, w), const2),
        smem, smem,
        tile(V7X_LANES),
        pl.BlockSpec((1, nh * dv), const2),
    ]
    assert len(in_specs) == N_RG_IN + N_ML_IN and z.shape[1] == IN_MAIN
    out_specs = [
        pl.BlockSpec((tl, w), lambda b, t: (b * nt + t, 0)),
        pl.BlockSpec((None, 1, w), lambda b, t: (b, 0, 0)),
        pl.BlockSpec((None, CONV_W - 1, w), lambda b, t: (b, 0, 0)),
        pl.BlockSpec((tl, nh * dv), lambda b, t: (b * nt + t, 0)),
        pl.BlockSpec((None, nh, dk, dv), lambda b, t: (b, 0, 0, 0)),
        pl.BlockSpec((None, nh, dk), lambda b, t: (b, 0, 0)),
        pl.BlockSpec((None, nh, V7X_LANES), lambda b, t: (b, 0, 0)),
    ]
    out_shape = [
        jax.ShapeDtypeStruct((batch * seq, w), BF16),
        jax.ShapeDtypeStruct((batch, 1, w), F32),
        jax.ShapeDtypeStruct((batch, CONV_W - 1, w), F32),
        jax.ShapeDtypeStruct((batch * seq, nh * dv), BF16),
        jax.ShapeDtypeStruct((batch, nh, dk, dv), F32),
        jax.ShapeDtypeStruct((batch, nh, dk), F32),
        jax.ShapeDtypeStruct((batch, nh, V7X_LANES), F32),
    ]
    args = [z, conv0, h0.reshape(batch, 1, w), cw, row(cb), wgate, row(ba), row(bx), row(lam),
            row(gout), b_i, b_f, zgates, g_ml.reshape(1, nh * dv)]
    side_rows = 0
    vmem_mib = 40
    if side_xattn is not None:
        xq, ck, cv = side_xattn
        nb = xq.shape[0]
        assert nb % (batch * nt) == 0 and ck.shape == (nb, N_MEM, XA_HEADS, XA_DH)
        side_rows = nb // (batch * nt)
        q_spec = pl.BlockSpec((side_rows, XA_HEADS, XA_DH), lambda b, t: (b * nt + t, 0, 0))
        cache_spec = pl.BlockSpec((side_rows, N_MEM, XA_HEADS, XA_DH),
                                  lambda b, t: (b * nt + t, 0, 0, 0))
        in_specs += [q_spec, cache_spec, cache_spec]
        out_specs.append(q_spec)
        out_shape.append(jax.ShapeDtypeStruct((nb, XA_HEADS, XA_DH), F32))
        args += [xq.reshape(nb, XA_HEADS, XA_DH), ck, cv]
        vmem_mib = 60
    return pl.pallas_call(
        functools.partial(_mixer_seq_kernel, tl=tl, side_rows=side_rows),
        grid=(batch, nt),
        in_specs=in_specs,
        out_specs=out_specs,
        out_shape=out_shape,
        scratch_shapes=[
            pltpu.VMEM((tl + V7X_SUBLANES, w), F32),
            pltpu.VMEM((tl, w), F32),
            pltpu.VMEM((tl, w), F32),
            pltpu.VMEM((tl, w), F32),
            pltpu.VMEM((1, w), F32),
            pltpu.VMEM((V7X_SUBLANES, tl), F32),
            pltpu.VMEM((nh, dk, dv), F32),
            pltpu.VMEM((nh, dk), F32),
            pltpu.VMEM((nh, V7X_LANES), F32),
        ],
        compiler_params=_params(2, vmem_mib),
        name=name,
    )(*args)


def _rglru_step_kernel(zx_ref, zg_ref, conv_ref, h0_ref, cw_ref, cb_ref, wgate_ref, ba_ref,
                       bx_ref, lam_ref, gout_ref, y_ref, hn_ref, convn_ref, a_ref, b_ref):
    w = RG_WIDTH
    x = zx_ref[...]
    xr = (conv_ref[0] * cw_ref[0:1, :] + conv_ref[1] * cw_ref[1:2, :]
          + conv_ref[2] * cw_ref[2:3, :] + x * cw_ref[3:4, :]) + cb_ref[...]
    _rg_gates(xr, wgate_ref, ba_ref, bx_ref, lam_ref, a_ref, b_ref)
    h = a_ref[...] * h0_ref[...] + b_ref[...]
    hn_ref[...] = h
    convn_ref[0] = conv_ref[1]
    convn_ref[1] = conv_ref[2]
    convn_ref[2] = x
    y_ref[...] = _rms(h * _gelu_tanh(zg_ref[...]), gout_ref[...]).astype(y_ref.dtype)


def _rglru_step(z, conv, h0, cw, cb, wgate, ba, bx, lam, gout, *, name):
    nb = z.shape[0]
    w = RG_WIDTH
    row = lambda v: v.reshape(1, w)
    c0 = lambda i: (0, 0)
    return pl.pallas_call(
        _rglru_step_kernel,
        grid=(1,),
        in_specs=[
            pl.BlockSpec((nb, w), lambda i: (0, OFF_RGX // w)),
            pl.BlockSpec((nb, w), lambda i: (0, OFF_RGG // w)),
            pl.BlockSpec((CONV_W - 1, nb, w), lambda i: (0, 0, 0)),
            pl.BlockSpec((nb, w), c0),
            pl.BlockSpec((CONV_W, w), c0),
            pl.BlockSpec((1, w), c0),
            pl.BlockSpec((RG_BLOCKS, RG_BLOCK, 2 * RG_BLOCK), lambda i: (0, 0, 0)),
            pl.BlockSpec((1, w), c0),
            pl.BlockSpec((1, w), c0),
            pl.BlockSpec((1, w), c0),
            pl.BlockSpec((1, w), c0),
        ],
        out_specs=[
            pl.BlockSpec((nb, w), c0),
            pl.BlockSpec((nb, w), c0),
            pl.BlockSpec((CONV_W - 1, nb, w), lambda i: (0, 0, 0)),
        ],
        out_shape=[
            jax.ShapeDtypeStruct((nb, w), BF16),
            jax.ShapeDtypeStruct((nb, w), F32),
            jax.ShapeDtypeStruct((CONV_W - 1, nb, w), F32),
        ],
        scratch_shapes=[pltpu.VMEM((nb, w), F32), pltpu.VMEM((nb, w), F32)],
        compiler_params=_params(1, 32),
        name=name,
    )(z, z, conv, h0, cw, row(cb), wgate, row(ba), row(bx), row(lam), row(gout))


def _mlstm_step_body(in_refs, out_refs, scratch_refs, part, *, bs, parts):
    bi_ref, bf_ref, z_ref, zg_ref, g_ref, c0_ref, n0_ref, m0_ref = in_refs
    y_ref, c_ref, n_ref, m_ref = out_refs
    qc_ref, = scratch_refs
    dk, dv = ML_DK, ML_DV
    nr = bs // parts
    rows = slice(part * nr, (part + 1) * nr)
    eye = (lax.broadcasted_iota(jnp.int32, (dk, dk), 0)
           == lax.broadcasted_iota(jnp.int32, (dk, dk), 1))

    def as_column(row):
        return jnp.sum(jnp.where(eye, jnp.broadcast_to(row, (dk, dk)), 0.0), axis=1, keepdims=True)

    zg = zg_ref[rows, :]
    for h in range(ML_HEADS):
        li = zg[:, h:h + 1] + bi_ref[h]
        lf = _log_sigmoid(zg[:, ML_HEADS + h:ML_HEADS + h + 1] + bf_ref[h])
        m = m0_ref[rows, h:h + 1]
        inter = lf + m
        m_t = jnp.maximum(inter, li)
        dgate = jnp.exp(li - m_t)
        sc = jnp.exp(inter - m_t)
        q = z_ref[rows, OFF_Q + h * dk:OFF_Q + (h + 1) * dk]
        k = z_ref[rows, OFF_K + h * dk:OFF_K + (h + 1) * dk] * (ML_DK ** -0.5)
        v = z_ref[rows, OFF_V + h * dv:OFF_V + (h + 1) * dv]
        n_old = n0_ref[rows, h, :]
        qk = jnp.sum(q * k, axis=1, keepdims=True) * dgate
        w_end = jnp.exp(li - m_t)
        dec = jnp.exp(inter - m_t)
        wk = w_end * k
        for j in range(nr):
            c_old = c0_ref[part * nr + j, h]
            qc_ref[j:j + 1, :] = jnp.sum(as_column(q[j:j + 1, :]) * c_old, axis=0, keepdims=True)
            c_ref[part * nr + j, h] = (dec[j:j + 1, :] * c_old
                                       + as_column(wk[j:j + 1, :]) * v[j:j + 1, :])
        num = sc * qc_ref[0:nr, :] + qk * v
        den = sc * jnp.sum(q * n_old, axis=1, keepdims=True) + qk
        den = jnp.maximum(jnp.abs(den), jnp.exp(-m_t))
        hh = num / den
        n_ref[rows, h, :] = dec * n_old + wk
        m_ref[rows, h:h + 1] = m_t
        y = (_rms(hh, g_ref[:, h * dv:(h + 1) * dv])
             * _sigmoid(z_ref[rows, OFF_O + h * dv:OFF_O + (h + 1) * dv]))
        y_ref[rows, h * dv:(h + 1) * dv] = y.astype(y_ref.dtype)


def _mlstm_step_job(z, zg, c0, n0, m0, b_i, b_f, g_out, *, bs, parts):
    nb = z.shape[0]
    assert nb % bs == 0 and bs % parts == 0
    dk, dv, nh = ML_DK, ML_DV, ML_HEADS
    smem = pl.BlockSpec(memory_space=pltpu.SMEM)

    def in_specs(blk):
        return [
            smem, smem,
            pl.BlockSpec((bs, IN_MAIN), lambda *g: (blk(*g), 0)),
            pl.BlockSpec((bs, V7X_LANES), lambda *g: (blk(*g), 0)),
            pl.BlockSpec((1, nh * dv), lambda *g: (0, 0)),
            pl.BlockSpec((bs, nh, dk, dv), lambda *g: (blk(*g), 0, 0, 0)),
            pl.BlockSpec((bs, nh, dk), lambda *g: (blk(*g), 0, 0)),
            pl.BlockSpec((bs, nh), lambda *g: (blk(*g), 0)),
        ]

    def out_specs(blk):
        return [
            pl.BlockSpec((bs, nh * dv), lambda *g: (blk(*g), 0)),
            pl.BlockSpec((bs, nh, dk, dv), lambda *g: (blk(*g), 0, 0, 0)),
            pl.BlockSpec((bs, nh, dk), lambda *g: (blk(*g), 0, 0)),
            pl.BlockSpec((bs, nh), lambda *g: (blk(*g), 0)),
        ]

    return SideJob(
        n_blocks=nb // bs,
        parts=parts,
        args=(b_i, b_f, z, zg, g_out.reshape(1, nh * dv), c0, n0, m0),
        in_specs=in_specs,
        out_shape=(
            jax.ShapeDtypeStruct((nb, nh * dv), F32),
            jax.ShapeDtypeStruct((nb, nh, dk, dv), F32),
            jax.ShapeDtypeStruct((nb, nh, dk), F32),
            jax.ShapeDtypeStruct((nb, nh), F32),
        ),
        out_specs=out_specs,
        scratch_shapes=(pltpu.VMEM((bs, dv), F32),),
        body=functools.partial(_mlstm_step_body, bs=bs, parts=parts),
    )


def _run_side_job(job, *, vmem_mib, name):
    n_in, n_out = len(job.args), len(job.out_shape)

    def body(*refs):
        for part in range(job.parts):
            job.body(refs[:n_in], refs[n_in:n_in + n_out], refs[n_in + n_out:], part)

    block_of = lambda i: i
    return pl.pallas_call(
        body,
        grid=(job.n_blocks,),
        in_specs=job.in_specs(block_of),
        out_specs=job.out_specs(block_of),
        out_shape=list(job.out_shape),
        scratch_shapes=list(job.scratch_shapes),
        compiler_params=_params(1, vmem_mib),
        name=name,
    )(*job.args)


def _softmax_rows(s):
    e = jnp.exp(s - jnp.max(s, axis=-1, keepdims=True))
    return e / jnp.sum(e, axis=-1, keepdims=True)


def _xattn_block_kernel(*refs, n_mix):
    x_ref = refs[0]
    y_refs = refs[1:1 + n_mix]
    wy_refs = refs[1 + n_mix:1 + 2 * n_mix]
    g_ref, wq_ref, k_ref, v_ref, wo_ref, o_ref = refs[1 + 2 * n_mix:]
    x = x_ref[...]
    for y_ref, wy_ref in zip(y_refs, wy_refs):
        x = x + jnp.dot(y_ref[...], wy_ref[...], preferred_element_type=F32)
    xq = jnp.dot(_rms(x, g_ref[...]).astype(BF16), wq_ref[...],
                 preferred_element_type=F32).astype(BF16)
    heads = []
    for h in range(XA_HEADS):
        sl = slice(h * XA_DH, (h + 1) * XA_DH)
        s = lax.dot_general(xq[:, sl], k_ref[:, sl].astype(BF16),
                            (((1,), (1,)), ((), ())), preferred_element_type=F32)
        p = _softmax_rows(s * (XA_DH ** -0.5))
        heads.append(_mm(p, v_ref[:, sl]).astype(BF16))
    o_ref[...] = x + jnp.dot(jnp.concatenate(heads, axis=1), wo_ref[...],
                             preferred_element_type=F32)


def _xattn_block(x, mix_parts, mix_weights16, g, wq16, mk, mv, wo16, *, batch, seq, tq, name):
    nt = seq // tq
    assert seq % tq == 0 and all(w.dtype == BF16 for w in (*mix_weights16, wq16, wo16))
    assert all(p.dtype == BF16 for p in mix_parts)
    d = D_MODEL
    resident = dict(pipeline_mode=pl.Buffered(1))
    rows = lambda width: pl.BlockSpec((tq, width), lambda b, t: (b * nt + t, 0))
    return pl.pallas_call(
        functools.partial(_xattn_block_kernel, n_mix=len(mix_parts)),
        grid=(batch, nt),
        in_specs=[
            rows(d),
            *[rows(p.shape[1]) for p in mix_parts],
            *[pl.BlockSpec(w.shape, lambda b, t: (0, 0), **resident) for w in mix_weights16],
            pl.BlockSpec((1, d), lambda b, t: (0, 0)),
            pl.BlockSpec((d, d), lambda b, t: (0, 0), **resident),
            pl.BlockSpec((N_MEM, d), lambda b, t: (b, 0)),
            pl.BlockSpec((N_MEM, d), lambda b, t: (b, 0)),
            pl.BlockSpec((d, d), lambda b, t: (0, 0), **resident),
        ],
        out_specs=rows(d),
        out_shape=jax.ShapeDtypeStruct((batch * seq, d), F32),
        compiler_params=_params(2, 60),
        name=name,
    )(x, *mix_parts, *mix_weights16, g.reshape(1, d), wq16, mk, mv, wo16)


def _xattn_own_mask():
    nrow = N_MEM * XA_HEADS
    col_head = lax.broadcasted_iota(jnp.int32, (V7X_SUBLANES, nrow), 1) & (XA_HEADS - 1)
    row_head = lax.broadcasted_iota(jnp.int32, (V7X_SUBLANES, nrow), 0) & (XA_HEADS - 1)
    return col_head == row_head


def _xattn_probs(q_ref, k_ref, j, own):
    kf = k_ref[j].reshape(N_MEM * XA_HEADS, XA_DH)
    q8 = jnp.concatenate([q_ref[j]] * (V7X_SUBLANES // XA_HEADS), axis=0)
    s = lax.dot_general(q8.astype(BF16), kf.astype(BF16), (((1,), (1,)), ((), ())),
                        preferred_element_type=F32)
    s = jnp.where(own, s * (XA_DH ** -0.5), NEG)
    e = jnp.where(own, jnp.exp(s - jnp.max(s, axis=-1, keepdims=True)), 0.0)
    return e / jnp.sum(e, axis=-1, keepdims=True)


def _xattn_values(v_ref, o_ref, j, p):
    vf = v_ref[j].reshape(N_MEM * XA_HEADS, XA_DH)
    o_ref[j] = _mm(p, vf)[0:XA_HEADS, :]


def _xattn_step_kernel(q_ref, k_ref, v_ref, o_ref, *, sb):
    own = _xattn_own_mask()
    for j in range(sb):
        _xattn_values(v_ref, o_ref, j, _xattn_probs(q_ref, k_ref, j, own))


def _xattn_step(xq, ck, cv, *, sb, name):
    nb = xq.shape[0]
    assert XA_HEADS & (XA_HEADS - 1) == 0
    assert nb % sb == 0 and ck.shape == (nb, N_MEM, XA_HEADS, XA_DH)
    cache_spec = pl.BlockSpec((sb, N_MEM, XA_HEADS, XA_DH), lambda i: (i, 0, 0, 0))
    q_spec = pl.BlockSpec((sb, XA_HEADS, XA_DH), lambda i: (i, 0, 0))
    return pl.pallas_call(
        functools.partial(_xattn_step_kernel, sb=sb),
        grid=(nb // sb,),
        in_specs=[q_spec, cache_spec, cache_spec],
        out_specs=q_spec,
        out_shape=jax.ShapeDtypeStruct((nb, XA_HEADS, XA_DH), F32),
        compiler_params=_params(1, 40),
        name=name,
    )(xq.reshape(nb, XA_HEADS, XA_DH), ck, cv).reshape(nb, XA_HEADS * XA_DH)


class Tiles(NamedTuple):
    rows: int
    attn_rows: int
    mem_rows: int
    cols: int
    ffn_cols16: int
    ffn_cols32: int
    step_block: int
    step_parts: int
    xattn_rows: int


def _plan_tiles(prompt_rows, mem_rows):
    rows = min(prompt_rows, 1024)
    assert prompt_rows % rows == 0
    return Tiles(rows=rows, attn_rows=min(rows, 512), mem_rows=min(mem_rows, 512), cols=1024,
                 ffn_cols16=512, ffn_cols32=256, step_block=V7X_SUBLANES, step_parts=2,
                 xattn_rows=2)


def kernel(x_prompt, x_sample, mem_prompt, state_rg_h, state_rg_conv, state_ml_C, state_ml_n, state_ml_m, cache_mem_k, cache_mem_v, g_mix, w_in, conv_w, conv_b, w_rg_a, b_rg_a, w_rg_x, b_rg_x, rg_lambda, b_ml_i, b_ml_f, g_rg_out, g_ml_out, w_out, g_xa, g_mem, w_xa_q, w_xa_k, w_xa_v, w_xa_o, g_ffn, w_ffn_gate, w_ffn_up, w_ffn_down, g_final):
    depth = g_mix.shape[0]
    assert depth == 1, "single trunk layer"
    bp, seq, d = x_prompt.shape
    bs_, dec_seq, _ = x_sample.shape
    assert d == D_MODEL and dec_seq == 1
    n_mem = mem_prompt.shape[1]
    assert n_mem == N_MEM
    dff = w_ffn_gate.shape[-1]
    in_w = w_in.shape[-1]
    assert in_w == IN_MAIN + N_GATE

    w_in_t = jnp.swapaxes(w_in, 1, 2).reshape(in_w, d)
    w_gate_pad = jnp.pad(w_in_t[IN_MAIN:], ((0, V7X_LANES - N_GATE), (0, 0)))
    cw = conv_w.reshape(CONV_W, RG_WIDTH)
    wgate = jnp.concatenate([w_rg_a.reshape(RG_BLOCKS, RG_BLOCK, RG_BLOCK),
                             w_rg_x.reshape(RG_BLOCKS, RG_BLOCK, RG_BLOCK)], axis=-1)
    rg_args = (cw, conv_b.reshape(-1), wgate, b_rg_a.reshape(-1), b_rg_x.reshape(-1),
               rg_lambda.reshape(-1), g_rg_out.reshape(-1))
    b_i = b_ml_i.reshape(ML_HEADS)
    b_f = b_ml_f.reshape(ML_HEADS)
    g_ml = g_ml_out.reshape(-1)
    w_out2 = w_out.reshape(d, d)
    w_q = w_xa_q.reshape(d, d)
    w_k = w_xa_k.reshape(d, d)
    w_v = w_xa_v.reshape(d, d)
    w_o = w_xa_o.reshape(d, d)
    w_fg = w_ffn_gate.reshape(d, dff)
    w_fu = w_ffn_up.reshape(d, dff)
    w_fd = w_ffn_down.reshape(dff, d)

    tiles = _plan_tiles(bp * seq, bp * n_mem)
    row_tile = tiles.rows
    rg_zero_conv = jnp.zeros((bp, CONV_W - 1, RG_WIDTH), F32)
    rg_zero_h = jnp.zeros((bp, RG_WIDTH), F32)

    xs = x_sample.reshape(bs_, d)
    z_s, zg_s, w_in16 = _norm_linear(
        xs, g_mix.reshape(-1), w_in_t, n_out=IN_MAIN, tm=bs_, tn=tiles.cols, w_gate=w_gate_pad,
        w_is_nk=True, emit_w16=True, name="in_proj_s")
    conv_taps = jnp.swapaxes(state_rg_conv.reshape(bs_, CONV_W - 1, RG_WIDTH), 0, 1)
    y_rg_s, s_h, s_conv_taps = _rglru_step(z_s, conv_taps, state_rg_h.reshape(bs_, RG_WIDTH),
                                           *rg_args, name="rglru_s")
    s_conv = jnp.swapaxes(s_conv_taps, 0, 1)
    mlstm_s_job = _mlstm_step_job(
        z_s, zg_s, state_ml_C.reshape(bs_, ML_HEADS, ML_DK, ML_DV),
        state_ml_n.reshape(bs_, ML_HEADS, ML_DK), state_ml_m.reshape(bs_, ML_HEADS),
        b_i, b_f, g_ml, bs=tiles.step_block, parts=tiles.step_parts)
    tp = bp * seq
    xp = x_prompt.reshape(tp, d)
    in_cols, in_rows = IN_MAIN // tiles.cols, tp // row_tile
    if (mlstm_s_job.n_blocks % in_rows == 0
            and mlstm_s_job.n_blocks // in_rows * mlstm_s_job.parts < in_cols):
        z_p, zg_p, y_ml_s, s_c, s_n, s_m = _norm_linear(
            xp, g_mix.reshape(-1), w_in16, n_out=IN_MAIN, tm=row_tile, tn=tiles.cols,
            w_gate=w_gate_pad, w_is_nk=True, side=mlstm_s_job, x_parts=4, vmem_mib=60,
            name="in_proj_p")
    else:
        z_p, zg_p = _norm_linear(xp, g_mix.reshape(-1), w_in16, n_out=IN_MAIN, tm=row_tile,
                                 tn=tiles.cols, w_gate=w_gate_pad, w_is_nk=True, name="in_proj_p")
        y_ml_s, s_c, s_n, s_m = _run_side_job(mlstm_s_job, vmem_mib=32, name="mlstm_s")
    x1_s, w_out16_rg, w_out16_ml = _linear_residual(
        [y_rg_s, y_ml_s], [(w_out2, 0), (w_out2, 1)], xs, tm=bs_, tn=tiles.cols, emit_w16=True,
        name="mix_out_s")
    xq_s, w_q16 = _norm_linear(x1_s, g_xa.reshape(-1), w_q, n_out=d, tm=bs_, tn=tiles.cols,
                               emit_w16=True, name="xa_q_s")
    ck = cache_mem_k.reshape(bs_, n_mem, XA_HEADS, XA_DH)
    cv = cache_mem_v.reshape(bs_, n_mem, XA_HEADS, XA_DH)

    mix_tl = min(seq, ML_CHUNK)
    mix_steps = bp * (seq // mix_tl)
    mix_args = (z_p, zg_p, rg_zero_conv, rg_zero_h, *rg_args, b_i, b_f, g_ml)
    if bs_ % mix_steps == 0 and RG_BLOCKS % (bs_ // mix_steps) == 0:
        y_rg_p, p_h, p_conv, y_ml_p, p_c, p_n, p_m, o_s = _mixer_seq(
            *mix_args, batch=bp, seq=seq, tl=mix_tl, side_xattn=(xq_s, ck, cv), name="mixer_p")
        o_s = o_s.reshape(bs_, d)
    else:
        y_rg_p, p_h, p_conv, y_ml_p, p_c, p_n, p_m = _mixer_seq(
            *mix_args, batch=bp, seq=seq, tl=mix_tl, name="mixer_p")
        o_s = _xattn_step(xq_s, ck, cv, sb=tiles.xattn_rows, name="xattn_s")

    x3_s, w_o16 = _linear_residual([o_s], [(w_o, 0)], x1_s, tm=bs_, tn=tiles.cols, emit_w16=True,
                                   name="xa_out_s")

    mem2 = mem_prompt.reshape(bp * n_mem, d)
    tmem = tiles.mem_rows
    mk, mk_heads = _norm_linear(mem2, g_mem.reshape(-1), w_k, n_out=d, tm=tmem, tn=d,
                                out_dtype=BF16, heads_out=(XA_HEADS, XA_DH), name="mem_k")
    mv, mv_heads = _norm_linear(mem2, g_mem.reshape(-1), w_v, n_out=d, tm=tmem, tn=d,
                                out_dtype=BF16, heads_out=(XA_HEADS, XA_DH), name="mem_v")
    x3_p = _xattn_block(xp, [y_rg_p, y_ml_p], [w_out16_rg, w_out16_ml], g_xa.reshape(-1), w_q16,
                        mk, mv, w_o16, batch=bp, seq=seq, tq=min(seq, tiles.attn_rows), name="xattn_p")
    y_s, y_p0, w_fg16, w_fu16, w_fd16 = _ffn(
        [(x3_s, bs_), (x3_p, row_tile)], g_ffn.reshape(-1), w_fg, w_fu, w_fd, g_final, n_rows=1,
        tf=tiles.ffn_cols32, emit_w16=True, vmem_mib=60, name="ffn_s")
    if tp // row_tile > 1:
        y_p = _ffn([(x3_p, row_tile)], g_ffn.reshape(-1), w_fg16, w_fu16, w_fd16, g_final,
                   n_rows=tp // row_tile, tf=tiles.ffn_cols16, head_tile=y_p0, x_parts=4,
                   vmem_mib=60, name="ffn_p")
    else:
        y_p = y_p0

    return (
        y_p.reshape(bp, seq, d),
        y_s.reshape(bs_, 1, d),
        p_h.reshape(1, bp, RG_WIDTH),
        p_conv.reshape(1, bp, CONV_W - 1, RG_WIDTH),
        p_c.reshape(1, bp, ML_HEADS, ML_DK, ML_DV),
        p_n.reshape(1, bp, ML_HEADS, ML_DK),
        p_m[:, :, 0].reshape(1, bp, ML_HEADS),
        mk_heads.reshape(1, bp, n_mem, XA_HEADS, XA_DH),
        mv_heads.reshape(1, bp, n_mem, XA_HEADS, XA_DH),
        s_h.reshape(1, bs_, RG_WIDTH),
        s_conv.reshape(1, bs_, CONV_W - 1, RG_WIDTH),
        s_c.reshape(1, bs_, ML_HEADS, ML_DK, ML_DV),
        s_n.reshape(1, bs_, ML_HEADS, ML_DK),
        s_m.reshape(1, bs_, ML_HEADS),
    )
```

```python
import functools
from typing import Callable, NamedTuple

import jax
import jax.numpy as jnp
from jax import lax
from jax.experimental import pallas as pl
from jax.experimental.pallas import tpu as pltpu

F32 = jnp.float32
BF16 = jnp.bfloat16

D_MODEL = 2048
RG_WIDTH = D_MODEL // 2
RG_BLOCKS = 8
RG_BLOCK = RG_WIDTH // RG_BLOCKS
CONV_W = 4
RG_C = 8.0
ML_HEADS = 4
ML_WIDTH = D_MODEL - RG_WIDTH
ML_DV = ML_WIDTH // ML_HEADS
ML_DK = ML_DV // 2
N_MEM = 256
XA_HEADS = 4
XA_DH = D_MODEL // XA_HEADS
EPS = 1e-6
NEG = -1e30

OFF_RGX = 0
OFF_RGG = OFF_RGX + RG_WIDTH
OFF_Q = OFF_RGG + RG_WIDTH
OFF_K = OFF_Q + ML_HEADS * ML_DK
OFF_V = OFF_K + ML_HEADS * ML_DK
OFF_O = OFF_V + ML_WIDTH
OFF_I = OFF_O + ML_WIDTH
IN_MAIN = OFF_I
N_GATE = 2 * ML_HEADS

V7X_LANES = 128
V7X_SUBLANES = 8
V7X_VMEM_BYTES = 64 * 2**20

ML_CHUNK = 256


class SideJob(NamedTuple):
    n_blocks: int
    parts: int
    args: tuple
    in_specs: Callable
    out_shape: tuple
    out_specs: Callable
    scratch_shapes: tuple
    body: Callable


def _params(n_axes, vmem_mib):
    assert vmem_mib * 2**20 <= V7X_VMEM_BYTES
    return pltpu.CompilerParams(
        dimension_semantics=("arbitrary",) * n_axes,
        vmem_limit_bytes=vmem_mib * 2**20,
    )


def _rms(x, g):
    ms = jnp.mean(x * x, axis=-1, keepdims=True)
    return x * lax.rsqrt(ms + EPS) * g


def _softplus(u):
    return jnp.maximum(u, 0.0) + jnp.log1p(jnp.exp(-jnp.abs(u)))


def _log_sigmoid(u):
    return -_softplus(-u)


def _sigmoid(u):
    return 0.5 * jnp.tanh(0.5 * u) + 0.5


def _gelu_tanh(x):
    c = 0.7978845608028654
    half_x = 0.5 * x
    return half_x + half_x * jnp.tanh(x * (c + (c * 0.044715) * (x * x)))


def _sqrt_nonneg(v):
    return jnp.where(v > 0.0, v * lax.rsqrt(v), 0.0)


def _mm(a, b):
    return jnp.dot(a.astype(BF16), b.astype(BF16), preferred_element_type=F32)


def _exact_cumsum_lanes(x):
    n = x.shape[-1]
    upper = jnp.where(lax.broadcasted_iota(jnp.int32, (n, n), 0)
                      <= lax.broadcasted_iota(jnp.int32, (n, n), 1), 1.0, 0.0).astype(BF16)
    hi = x.astype(BF16)
    rest = x - hi.astype(F32)
    mid = rest.astype(BF16)
    lo = (rest - mid.astype(F32)).astype(BF16)
    return (jnp.dot(hi, upper, preferred_element_type=F32)
            + jnp.dot(mid, upper, preferred_element_type=F32)
            + jnp.dot(lo, upper, preferred_element_type=F32))


def _dot_w(a, w_ref, w_is_nk):
    w = w_ref[...].astype(BF16)
    if w_is_nk:
        return lax.dot_general(a, w, (((1,), (1,)), ((), ())), preferred_element_type=F32)
    return jnp.dot(a, w, preferred_element_type=F32)


def _norm_linear_kernel(*refs, x_parts, with_gate, w_is_nk, emit_w16, heads_out, side, side_cols):
    x_refs, refs = list(refs[:x_parts]), list(refs[x_parts - 1:])
    n_si, n_so, n_ss = ((len(side.args), len(side.out_shape), len(side.scratch_shapes))
                        if side else (0, 0, 0))
    n_in = 3 + with_gate + n_si
    n_out = 1 + bool(heads_out) + with_gate + emit_w16 + n_so
    g_ref, w_ref = refs[1:3]
    wg_ref = refs[3] if with_gate else None
    side_in = refs[3 + with_gate:n_in]
    outs = refs[n_in:n_in + n_out]
    xn_ref = refs[n_in + n_out]
    side_scratch = refs[n_in + n_out + 1:]
    o_ref = outs[0]
    oh_ref = outs[1] if heads_out else None
    og_ref = outs[1 + bool(heads_out)] if with_gate else None
    w16_ref = outs[n_out - n_so - 1] if emit_w16 else None
    side_out = outs[n_out - n_so:]
    col = pl.program_id(1)

    def column_tile(xn):
        acc = _dot_w(xn, w_ref, w_is_nk)
        o_ref[...] = acc.astype(o_ref.dtype)
        if heads_out:
            oh_ref[...] = acc.reshape(oh_ref.shape)
        if emit_w16:
            w16_ref[...] = w_ref[...].astype(BF16)

    @pl.when(col == 0)
    def _():
        if x_parts == 1:
            xn = _rms(x_refs[0][...], g_ref[...]).astype(BF16)
            xn_ref[...] = xn
        else:
            part = xn_ref.shape[0] // x_parts
            for q, x_ref in enumerate(x_refs):
                xn_ref[q * part:(q + 1) * part, :] = _rms(x_ref[...], g_ref[...]).astype(BF16)
            xn = xn_ref[...]
        if with_gate:
            og_ref[...] = _dot_w(xn, wg_ref, w_is_nk)
        column_tile(xn)

    if side:
        for part in range(side.parts):
            @pl.when((col > 0) & (col <= side_cols) & ((col - 1) % side.parts == part))
            def _(part=part):
                column_tile(xn_ref[...])
                side.body(side_in, side_out, side_scratch, part)

    @pl.when(col > (side_cols if side else 0))
    def _():
        column_tile(xn_ref[...])


def _norm_linear(x, g, w, *, n_out, tm, tn, out_dtype=F32, w_gate=None, w_is_nk=False,
                 emit_w16=False, heads_out=None, side=None, x_parts=1, vmem_mib=48, name):
    m, k = x.shape
    k_ax, n_ax = (1, 0) if w_is_nk else (0, 1)
    assert m % tm == 0 and n_out % tn == 0 and w.shape[k_ax] == k and n_out <= w.shape[n_ax]
    assert not emit_w16 or m == tm
    assert heads_out is None or (tn == n_out and heads_out[0] * heads_out[1] == n_out)
    n_rows, n_cols = m // tm, n_out // tn
    with_gate = w_gate is not None
    w_mode = dict(pipeline_mode=pl.Buffered(1)) if tn == n_out else {}
    w_spec = (pl.BlockSpec((tn, k), lambda i, j: (j, 0), **w_mode) if w_is_nk
              else pl.BlockSpec((k, tn), lambda i, j: (0, j), **w_mode))
    assert tm % x_parts == 0 and x_parts <= n_cols
    if x_parts == 1:
        x_specs = [pl.BlockSpec((tm, k), lambda i, j: (i, 0))]
    else:
        x_specs = [
            pl.BlockSpec((tm // x_parts, k),
                         lambda i, j, q=q: (jnp.minimum(i + (j >= n_cols - (x_parts - 1) + q),
                                                        n_rows - 1) * x_parts + q, 0))
            for q in range(x_parts)]
    in_specs = [
        *x_specs,
        pl.BlockSpec((1, k), lambda i, j: (0, 0)),
        w_spec,
    ]
    out_shape = [jax.ShapeDtypeStruct((m, n_out), out_dtype)]
    out_specs = [pl.BlockSpec((tm, tn), lambda i, j: (i, j))]
    scratch_shapes = [pltpu.VMEM((tm, k), BF16)]
    args = [*([x] * x_parts), g.reshape(1, k), w]
    if heads_out:
        out_shape.append(jax.ShapeDtypeStruct((m, *heads_out), F32))
        out_specs.append(pl.BlockSpec((tm, *heads_out), lambda i, j: (i, 0, 0)))
    if with_gate:
        ng = w_gate.shape[n_ax]
        in_specs.append(pl.BlockSpec(w_gate.shape, lambda i, j: (0, 0)))
        out_shape.append(jax.ShapeDtypeStruct((m, ng), F32))
        out_specs.append(pl.BlockSpec((tm, ng), lambda i, j: (i, 0)))
        args.append(w_gate)
    if emit_w16:
        out_shape.append(jax.ShapeDtypeStruct((n_out, k) if w_is_nk else (k, n_out), BF16))
        out_specs.append(pl.BlockSpec((tn, k), lambda i, j: (j, 0)) if w_is_nk
                         else pl.BlockSpec((k, tn), lambda i, j: (0, j)))
    side_cols = 0
    if side:
        assert side.n_blocks % n_rows == 0
        blocks_per_row = side.n_blocks // n_rows
        side_cols = blocks_per_row * side.parts
        assert 0 < side_cols < n_cols
        block_of = lambda i, j: (i * blocks_per_row
                                 + jnp.clip(j - 1, 0, side_cols - 1) // side.parts)
        block_in = lambda i, j: jnp.where(
            j > side_cols, jnp.minimum((i + 1) * blocks_per_row, side.n_blocks - 1), block_of(i, j))
        in_specs += side.in_specs(block_in)
        out_specs += side.out_specs(block_of)
        out_shape += list(side.out_shape)
        scratch_shapes += list(side.scratch_shapes)
        args += list(side.args)
    out = pl.pallas_call(
        functools.partial(_norm_linear_kernel, x_parts=x_parts, with_gate=with_gate, w_is_nk=w_is_nk,
                          emit_w16=emit_w16, heads_out=heads_out, side=side, side_cols=side_cols),
        grid=(n_rows, n_cols),
        in_specs=in_specs,
        out_specs=out_specs,
        out_shape=out_shape,
        scratch_shapes=scratch_shapes,
        compiler_params=_params(2, vmem_mib),
        name=name,
    )(*args)
    return out if len(out) > 1 else out[0]


def _linear_res_kernel(*refs, n_in, emit_w16):
    a_refs = refs[:n_in]
    w_refs = refs[n_in:2 * n_in]
    res_ref = refs[2 * n_in]
    o_ref = refs[2 * n_in + 1]
    acc = res_ref[...]
    for a_ref, w_ref in zip(a_refs, w_refs):
        acc = acc + _mm(a_ref[...], w_ref[...])
    o_ref[...] = acc
    if emit_w16:
        for w_ref, wc_ref in zip(w_refs, refs[2 * n_in + 2:]):
            wc_ref[...] = w_ref[...].astype(BF16)


def _linear_residual(parts, weights, res, *, tm, tn, emit_w16=False, vmem_mib=48, name):
    m, n = res.shape
    kp = parts[0].shape[1]
    assert all(p.shape == (m, kp) for p in parts) and len(weights) == len(parts)
    assert m % tm == 0 and n % tn == 0 and (not emit_w16 or m == tm)
    n_in = len(parts)
    in_specs = [pl.BlockSpec((tm, kp), lambda i, j: (i, 0)) for _ in parts]
    w_mode = dict(pipeline_mode=pl.Buffered(1)) if tn == n else {}
    in_specs += [pl.BlockSpec((kp, tn), lambda i, j, rb=rb: (rb, j), **w_mode)
                 for _, rb in weights]
    in_specs.append(pl.BlockSpec((tm, tn), lambda i, j: (i, j)))
    out_specs = [pl.BlockSpec((tm, tn), lambda i, j: (i, j))]
    out_shape = [jax.ShapeDtypeStruct((m, n), F32)]
    if emit_w16:
        out_specs += [pl.BlockSpec((kp, tn), lambda i, j: (0, j)) for _ in parts]
        out_shape += [jax.ShapeDtypeStruct((kp, n), BF16) for _ in parts]
    out = pl.pallas_call(
        functools.partial(_linear_res_kernel, n_in=n_in, emit_w16=emit_w16),
        grid=(m // tm, n // tn),
        in_specs=in_specs,
        out_specs=out_specs,
        out_shape=out_shape,
        compiler_params=_params(2, vmem_mib),
        name=name,
    )(*parts, *[w for w, _ in weights], res)
    return out if emit_w16 else out[0]


FFN_HEAD_SLABS = 8


def _ffn_kernel(*refs, n_groups, x_parts, emit_w16, head_tile):
    x_part_refs, refs = refs[:x_parts], refs[x_parts - 1:]
    x_refs = refs[:n_groups]
    g_ref, wg_ref, wu_ref, wd_ref, gf_ref = refs[n_groups:n_groups + 5]
    n_in = n_groups + 5 + bool(head_tile)
    head_ref = refs[n_in - 1] if head_tile else None
    o_refs = refs[n_in:n_in + n_groups]
    w16_refs = refs[n_in + n_groups:n_in + n_groups + 3] if emit_w16 else ()
    xf_refs = refs[-n_groups:]
    row = pl.program_id(0)
    f = pl.program_id(1)
    last = pl.num_programs(1) - 1
    computing = (row > 0) if head_tile else True

    def weights():
        wg = wg_ref[...].astype(BF16)
        wu = wu_ref[...].astype(BF16)
        wd = wd_ref[...].astype(BF16)
        for dst, val in zip(w16_refs, (wg, wu, wd)):
            dst[...] = val
        return wg, wu, wd

    def hidden_tile(xf, w):
        wg, wu, wd = w
        gate = jnp.dot(xf, wg, preferred_element_type=F32)
        up = jnp.dot(xf, wu, preferred_element_type=F32)
        hidden = (gate * _sigmoid(gate)) * up
        return jnp.dot(hidden.astype(BF16), wd, preferred_element_type=F32)

    if head_tile:
        slab = head_ref.shape[0]

        @pl.when((row == 0) & (f < FFN_HEAD_SLABS))
        def _():
            o_refs[0][pl.ds(pl.multiple_of(f * slab, slab), slab), :] = head_ref[...]

    @pl.when(computing & (f == 0))
    def _():
        w = weights()
        if x_parts > 1:
            part = xf_refs[0].shape[0] // x_parts
            for q, x_ref in enumerate(x_part_refs):
                x = x_ref[...]
                xf_refs[0][q * part:(q + 1) * part, :] = _rms(x, g_ref[...]).astype(BF16)
                o_refs[0][q * part:(q + 1) * part, :] = x
            o_refs[0][...] += hidden_tile(xf_refs[0][...], w)
        else:
            for x_ref, o_ref, xf_ref in zip(x_refs, o_refs, xf_refs):
                x = x_ref[...]
                xf = _rms(x, g_ref[...]).astype(BF16)
                xf_ref[...] = xf
                o_ref[...] = x + hidden_tile(xf, w)

    @pl.when(computing & (f > 0) & (f < last))
    def _():
        w = weights()
        for o_ref, xf_ref in zip(o_refs, xf_refs):
            o_ref[...] += hidden_tile(xf_ref[...], w)

    @pl.when(computing & (f == last))
    def _():
        w = weights()
        for o_ref, xf_ref in zip(o_refs, xf_refs):
            o_ref[...] = _rms(o_ref[...] + hidden_tile(xf_ref[...], w), gf_ref[...])


def _ffn(groups, g, w_gate, w_up, w_down, g_final, *, n_rows, tf, emit_w16=False, head_tile=None,
         x_parts=1, vmem_mib, name):
    d = groups[0][0].shape[1]
    dff = w_gate.shape[1]
    n_groups = len(groups)
    assert dff % tf == 0 and dff // tf >= 2 and (not emit_w16 or n_rows == 1)
    assert head_tile is None or (n_groups == 1 and head_tile.shape == (groups[0][1], d))
    assert all(x.shape[0] >= n_rows * tm for x, tm in groups)
    col = (lambda i, f: jnp.where(i > 0, f, 0)) if head_tile is not None else (lambda i, f: f)
    up_spec = pl.BlockSpec((d, tf), lambda i, f: (0, col(i, f)))
    down_spec = pl.BlockSpec((tf, d), lambda i, f: (col(i, f), 0))
    vec_spec = pl.BlockSpec((1, d), lambda i, f: (0, 0))
    row_specs = [pl.BlockSpec((tm, d), lambda i, f: (i, 0)) for _, tm in groups]
    x_specs, n_f = row_specs, dff // tf
    if x_parts > 1:
        tm0 = groups[0][1]
        assert n_groups == 1 and tm0 % x_parts == 0 and x_parts <= n_f
        x_specs = [
            pl.BlockSpec((tm0 // x_parts, d),
                         lambda i, f, q=q: (jnp.minimum(i + (f >= n_f - (x_parts - 1) + q),
                                                        n_rows - 1) * x_parts + q, 0))
            for q in range(x_parts)]
    in_specs = x_specs + [vec_spec, up_spec, up_spec, down_spec, vec_spec]
    args = ([x for x, _ in groups] * x_parts
            + [g.reshape(1, d), w_gate, w_up, w_down, g_final.reshape(1, d)])
    if head_tile is not None:
        tm0 = groups[0][1]
        assert tm0 % FFN_HEAD_SLABS == 0 and dff // tf >= FFN_HEAD_SLABS
        last_slab = FFN_HEAD_SLABS - 1
        in_specs.append(pl.BlockSpec(
            (tm0 // FFN_HEAD_SLABS, d),
            lambda i, f: (jnp.where(i == 0, jnp.minimum(f, last_slab), last_slab), 0)))
        args.append(head_tile)
    out_specs = list(row_specs)
    out_shape = [jax.ShapeDtypeStruct((n_rows * tm, d), F32) for _, tm in groups]
    if emit_w16:
        out_specs += [pl.BlockSpec((d, tf), lambda i, f: (0, f))] * 2
        out_specs += [pl.BlockSpec((tf, d), lambda i, f: (f, 0))]
        out_shape += [jax.ShapeDtypeStruct(w.shape, BF16) for w in (w_gate, w_up, w_down)]
    out = pl.pallas_call(
        functools.partial(_ffn_kernel, n_groups=n_groups, x_parts=x_parts, emit_w16=emit_w16,
                          head_tile=head_tile is not None),
        grid=(n_rows, dff // tf),
        in_specs=in_specs,
        out_specs=out_specs,
        out_shape=out_shape,
        scratch_shapes=[pltpu.VMEM((tm, d), BF16) for _, tm in groups],
        compiler_params=_params(2, vmem_mib),
        name=name,
    )(*args)
    return out if len(out) > 1 else out[0]


RG_GATE_LOOKAHEAD = 2


def _rg_gates(xr, wgate_ref, ba_ref, bx_ref, lam_ref, a_ref, b_ref, side=None):
    blocks = [slice(n * RG_BLOCK, (n + 1) * RG_BLOCK) for n in range(RG_BLOCKS)]
    gates = {}

    def issue_gate(n):
        if n < RG_BLOCKS:
            gates[n] = _mm(xr[:, blocks[n]], wgate_ref[n])

    rows, score_phase, value_phase = side if side else (0, None, None)
    per = RG_BLOCKS // rows if rows else RG_BLOCKS
    scores = {}
    for n in range(RG_GATE_LOOKAHEAD):
        issue_gate(n)
    for n, sl in enumerate(blocks):
        if rows and n % per == 0:
            j = n // per
            scores[j] = score_phase(j)
            if j > 0:
                value_phase(j - 1, scores.pop(j - 1))
        xn = xr[:, sl]
        g = gates.pop(n)
        issue_gate(n + RG_GATE_LOOKAHEAD)
        r = _sigmoid(g[:, :RG_BLOCK] + ba_ref[:, sl])
        ig = _sigmoid(g[:, RG_BLOCK:] + bx_ref[:, sl])
        a = jnp.exp(r * (-RG_C * _softplus(-lam_ref[:, sl])))
        a_ref[:, sl] = a
        mult = _sqrt_nonneg(jnp.maximum(1.0 - a * a, 0.0))
        b_ref[:, sl] = mult * (ig * xn)
    if rows:
        value_phase(rows - 1, scores.pop(rows - 1))


def _mlstm_chunk(q, k, v, o_gate, li_col, li_row, lf_row, bcum_row, causal, g, c_state, n_state, m):
    cs = q.shape[0]
    bcum_col = jnp.sum(jnp.where(causal, lf_row, 0.0), axis=1, keepdims=True)
    log_d = jnp.where(causal, bcum_col - bcum_row + li_row, NEG)
    inter = bcum_col + m
    m_t = jnp.maximum(inter, jnp.max(log_d, axis=1, keepdims=True))
    dmat = jnp.exp(log_d - m_t)
    sc = jnp.exp(inter - m_t)
    qb = q.astype(BF16)
    kb = k.astype(BF16)
    vb = v.astype(BF16)
    qk = lax.dot_general(qb, kb, (((1,), (1,)), ((), ())), preferred_element_type=F32) * dmat
    num = sc * jnp.dot(qb, c_state.astype(BF16), preferred_element_type=F32) + _mm(qk, vb)
    den = sc * jnp.sum(q * n_state, axis=1, keepdims=True) + jnp.sum(qk, axis=1, keepdims=True)
    den = jnp.maximum(jnp.abs(den), jnp.exp(-m_t))
    hh = num / den
    m_new = m_t[cs - 1:cs, :]
    b_last = bcum_col[cs - 1:cs, :]
    w_end = jnp.exp(b_last - bcum_col + li_col - m_new)
    dec = jnp.exp(b_last + m - m_new)
    wk = w_end * k
    c_new = dec * c_state + lax.dot_general(
        wk.astype(BF16), vb, (((0,), (0,)), ((), ())), preferred_element_type=F32)
    n_new = dec * n_state + jnp.sum(wk, axis=0, keepdims=True)
    y = _rms(hh, g) * _sigmoid(o_gate)
    return y, c_new, n_new, m_new


N_RG_IN = 10
N_ML_IN = 4


def _mixer_seq_kernel(*refs, tl, side_rows):
    (z_ref, conv0_ref, h0_ref, cw_ref, cb_ref, wgate_ref, ba_ref, bx_ref, lam_ref,
     gout_ref) = refs[:N_RG_IN]
    bi_ref, bf_ref, gates_ref, gml_ref = refs[N_RG_IN:N_RG_IN + N_ML_IN]
    n_in = N_RG_IN + N_ML_IN + (3 if side_rows else 0)
    if side_rows:
        sq_ref, sk_ref, sv_ref = refs[N_RG_IN + N_ML_IN:n_in]
    y_ref, hlast_ref, convn_ref, yml_ref, cout_ref, nout_ref, mout_ref = refs[n_in:n_in + 7]
    so_ref = refs[n_in + 7] if side_rows else None
    xe_ref, a_ref, b_ref, h_ref, hc_ref, lf_ref, cst_ref, nst_ref, mst_ref = refs[-9:]
    t = pl.program_id(1)
    pad = V7X_SUBLANES

    @pl.when(t == 0)
    def _():
        xe_ref[pad - 3:pad, :] = conv0_ref[...]
        hc_ref[...] = h0_ref[...]
        cst_ref[...] = jnp.zeros_like(cst_ref)
        nst_ref[...] = jnp.zeros_like(nst_ref)
        mst_ref[...] = jnp.zeros_like(mst_ref)
        lf_ref[...] = jnp.zeros_like(lf_ref)

    @pl.when(t > 0)
    def _():
        xe_ref[pad - 3:pad, :] = xe_ref[tl + pad - 3:tl + pad, :]

    x = z_ref[:, OFF_RGX:OFF_RGX + RG_WIDTH]
    xe_ref[pad:tl + pad, :] = x
    xr = (xe_ref[pad - 3:tl + pad - 3, :] * cw_ref[0:1, :]
          + xe_ref[pad - 2:tl + pad - 2, :] * cw_ref[1:2, :]
          + xe_ref[pad - 1:tl + pad - 1, :] * cw_ref[2:3, :]
          + x * cw_ref[3:4, :]) + cb_ref[...]
    side = None
    if side_rows:
        own = _xattn_own_mask()
        side = (side_rows,
                lambda j: _xattn_probs(sq_ref, sk_ref, j, own),
                lambda j, p: _xattn_values(sv_ref, so_ref, j, p))

    _rg_gates(xr, wgate_ref, ba_ref, bx_ref, lam_ref, a_ref, b_ref, side=side)

    gates = gates_ref[...]
    gates_t = gates.T
    for h in range(ML_HEADS):
        lf_ref[h:h + 1, :] = _log_sigmoid(gates_t[ML_HEADS + h:ML_HEADS + h + 1, :] + bf_ref[h])
    lf_rows = lf_ref[...]
    bcum_rows = _exact_cumsum_lanes(lf_rows)
    causal = (lax.broadcasted_iota(jnp.int32, (tl, tl), 1)
              <= lax.broadcasted_iota(jnp.int32, (tl, tl), 0))
    for h in range(ML_HEADS):
        vo = slice(h * ML_DV, (h + 1) * ML_DV)
        col = lambda off, width: slice(off + h * width, off + (h + 1) * width)
        y_h, c_new, n_new, m_new = _mlstm_chunk(
            z_ref[:, col(OFF_Q, ML_DK)], z_ref[:, col(OFF_K, ML_DK)] * (ML_DK ** -0.5),
            z_ref[:, col(OFF_V, ML_DV)], z_ref[:, col(OFF_O, ML_DV)],
            gates[:, h:h + 1] + bi_ref[h], gates_t[h:h + 1, :] + bi_ref[h],
            lf_rows[h:h + 1, :], bcum_rows[h:h + 1, :], causal, gml_ref[:, vo],
            cst_ref[h], nst_ref[h:h + 1, :], mst_ref[h:h + 1, 0:1])
        cst_ref[h] = c_new
        nst_ref[h:h + 1, :] = n_new
        mst_ref[h:h + 1, :] = jnp.broadcast_to(m_new, (1, V7X_LANES))
        yml_ref[:, vo] = y_h.astype(yml_ref.dtype)
    cout_ref[...] = cst_ref[...]
    nout_ref[...] = nst_ref[...]
    mout_ref[...] = mst_ref[...]

    row = lax.broadcasted_iota(jnp.int32, (V7X_SUBLANES, RG_WIDTH), 0)

    def group(gi, hc):
        r0 = pl.multiple_of(gi * V7X_SUBLANES, V7X_SUBLANES)
        a8 = a_ref[pl.ds(r0, V7X_SUBLANES), :]
        b8 = b_ref[pl.ds(r0, V7X_SUBLANES), :]
        for d in (1, 2, 4):
            keep = row >= d
            b8 = jnp.where(keep, a8 * pltpu.roll(b8, d, axis=0) + b8, b8)
            a8 = jnp.where(keep, a8 * pltpu.roll(a8, d, axis=0), a8)
        h8 = a8 * hc + b8
        h_ref[pl.ds(r0, V7X_SUBLANES), :] = h8
        return h8[V7X_SUBLANES - 1:V7X_SUBLANES, :]

    hc = lax.fori_loop(0, tl // V7X_SUBLANES, group, hc_ref[...], unroll=4)
    hc_ref[...] = hc
    hlast_ref[...] = hc
    convn_ref[...] = xe_ref[tl + pad - 3:tl + pad, :]
    y = h_ref[...] * _gelu_tanh(z_ref[:, OFF_RGG:OFF_RGG + RG_WIDTH])
    y_ref[...] = _rms(y, gout_ref[...]).astype(y_ref.dtype)


def _mixer_seq(z, zgates, conv0, h0, cw, cb, wgate, ba, bx, lam, gout, b_i, b_f, g_ml, *,
               batch, seq, tl, name, side_xattn=None):
    nt = seq // tl
    assert seq % tl == 0 and zgates.shape[1] == V7X_LANES
    w = RG_WIDTH
    dk, dv, nh = ML_DK, ML_DV, ML_HEADS
    row = lambda v: v.reshape(1, w)
    const2 = lambda b, t: (0, 0)
    smem = pl.BlockSpec(memory_space=pltpu.SMEM)
    tile = lambda width: pl.BlockSpec((tl, width), lambda b, t: (b * nt + t, 0))
    in_specs = [
        tile(IN_MAIN),
        pl.BlockSpec((None, CONV_W - 1, w), lambda b, t: (b, 0, 0)),
        pl.BlockSpec((None, 1, w), lambda b, t: (b, 0, 0)),
        pl.BlockSpec((CONV_W, w), const2),
        pl.BlockSpec((1, w), const2),
        pl.BlockSpec((RG_BLOCKS, RG_BLOCK, 2 * RG_BLOCK), lambda b, t: (0, 0, 0)),
        pl.BlockSpec((1, w), const2),
        pl.BlockSpec((1, w), const2),
        pl.BlockSpec((1, w), const2),
        pl.BlockSpec((1, w), const2),
        smem, smem,
        tile(V7X_LANES),
        pl.BlockSpec((1, nh * dv), const2),
    ]
    assert len(in_specs) == N_RG_IN + N_ML_IN and z.shape[1] == IN_MAIN
    out_specs = [
        pl.BlockSpec((tl, w), lambda b, t: (b * nt + t, 0)),
        pl.BlockSpec((None, 1, w), lambda b, t: (b, 0, 0)),
        pl.BlockSpec((None, CONV_W - 1, w), lambda b, t: (b, 0, 0)),
        pl.BlockSpec((tl, nh * dv), lambda b, t: (b * nt + t, 0)),
        pl.BlockSpec((None, nh, dk, dv), lambda b, t: (b, 0, 0, 0)),
        pl.BlockSpec((None, nh, dk), lambda b, t: (b, 0, 0)),
        pl.BlockSpec((None, nh, V7X_LANES), lambda b, t: (b, 0, 0)),
    ]
    out_shape = [
        jax.ShapeDtypeStruct((batch * seq, w), BF16),
        jax.ShapeDtypeStruct((batch, 1, w), F32),
        jax.ShapeDtypeStruct((batch, CONV_W - 1, w), F32),
        jax.ShapeDtypeStruct((batch * seq, nh * dv), BF16),
        jax.ShapeDtypeStruct((batch, nh, dk, dv), F32),
        jax.ShapeDtypeStruct((batch, nh, dk), F32),
        jax.ShapeDtypeStruct((batch, nh, V7X_LANES), F32),
    ]
    args = [z, conv0, h0.reshape(batch, 1, w), cw, row(cb), wgate, row(ba), row(bx), row(lam),
            row(gout), b_i, b_f, zgates, g_ml.reshape(1, nh * dv)]
    side_rows = 0
    vmem_mib = 40
    if side_xattn is not None:
        xq, ck, cv = side_xattn
        nb = xq.shape[0]
        assert nb % (batch * nt) == 0 and ck.shape == (nb, N_MEM, XA_HEADS, XA_DH)
        side_rows = nb // (batch * nt)
        q_spec = pl.BlockSpec((side_rows, XA_HEADS, XA_DH), lambda b, t: (b * nt + t, 0, 0))
        cache_spec = pl.BlockSpec((side_rows, N_MEM, XA_HEADS, XA_DH),
                                  lambda b, t: (b * nt + t, 0, 0, 0))
        in_specs += [q_spec, cache_spec, cache_spec]
        out_specs.append(q_spec)
        out_shape.append(jax.ShapeDtypeStruct((nb, XA_HEADS, XA_DH), F32))
        args += [xq.reshape(nb, XA_HEADS, XA_DH), ck, cv]
        vmem_mib = 60
    return pl.pallas_call(
        functools.partial(_mixer_seq_kernel, tl=tl, side_rows=side_rows),
        grid=(batch, nt),
        in_specs=in_specs,
        out_specs=out_specs,
        out_shape=out_shape,
        scratch_shapes=[
            pltpu.VMEM((tl + V7X_SUBLANES, w), F32),
            pltpu.VMEM((tl, w), F32),
            pltpu.VMEM((tl, w), F32),
            pltpu.VMEM((tl, w), F32),
            pltpu.VMEM((1, w), F32),
            pltpu.VMEM((V7X_SUBLANES, tl), F32),
            pltpu.VMEM((nh, dk, dv), F32),
            pltpu.VMEM((nh, dk), F32),
            pltpu.VMEM((nh, V7X_LANES), F32),
        ],
        compiler_params=_params(2, vmem_mib),
        name=name,
    )(*args)


def _rglru_step_kernel(zx_ref, zg_ref, conv_ref, h0_ref, cw_ref, cb_ref, wgate_ref, ba_ref,
                       bx_ref, lam_ref, gout_ref, y_ref, hn_ref, convn_ref, a_ref, b_ref):
    w = RG_WIDTH
    x = zx_ref[...]
    xr = (conv_ref[0] * cw_ref[0:1, :] + conv_ref[1] * cw_ref[1:2, :]
          + conv_ref[2] * cw_ref[2:3, :] + x * cw_ref[3:4, :]) + cb_ref[...]
    _rg_gates(xr, wgate_ref, ba_ref, bx_ref, lam_ref, a_ref, b_ref)
    h = a_ref[...] * h0_ref[...] + b_ref[...]
    hn_ref[...] = h
    convn_ref[0] = conv_ref[1]
    convn_ref[1] = conv_ref[2]
    convn_ref[2] = x
    y_ref[...] = _rms(h * _gelu_tanh(zg_ref[...]), gout_ref[...]).astype(y_ref.dtype)


def _rglru_step(z, conv, h0, cw, cb, wgate, ba, bx, lam, gout, *, name):
    nb = z.shape[0]
    w = RG_WIDTH
    row = lambda v: v.reshape(1, w)
    c0 = lambda i: (0, 0)
    return pl.pallas_call(
        _rglru_step_kernel,
        grid=(1,),
        in_specs=[
            pl.BlockSpec((nb, w), lambda i: (0, OFF_RGX // w)),
            pl.BlockSpec((nb, w), lambda i: (0, OFF_RGG // w)),
            pl.BlockSpec((CONV_W - 1, nb, w), lambda i: (0, 0, 0)),
            pl.BlockSpec((nb, w), c0),
            pl.BlockSpec((CONV_W, w), c0),
            pl.BlockSpec((1, w), c0),
            pl.BlockSpec((RG_BLOCKS, RG_BLOCK, 2 * RG_BLOCK), lambda i: (0, 0, 0)),
            pl.BlockSpec((1, w), c0),
            pl.BlockSpec((1, w), c0),
            pl.BlockSpec((1, w), c0),
            pl.BlockSpec((1, w), c0),
        ],
        out_specs=[
            pl.BlockSpec((nb, w), c0),
            pl.BlockSpec((nb, w), c0),
            pl.BlockSpec((CONV_W - 1, nb, w), lambda i: (0, 0, 0)),
        ],
        out_shape=[
            jax.ShapeDtypeStruct((nb, w), BF16),
            jax.ShapeDtypeStruct((nb, w), F32),
            jax.ShapeDtypeStruct((CONV_W - 1, nb, w), F32),
        ],
        scratch_shapes=[pltpu.VMEM((nb, w), F32), pltpu.VMEM((nb, w), F32)],
        compiler_params=_params(1, 32),
        name=name,
    )(z, z, conv, h0, cw, row(cb), wgate, row(ba), row(bx), row(lam), row(gout))


def _mlstm_step_body(in_refs, out_refs, scratch_refs, part, *, bs, parts):
    bi_ref, bf_ref, z_ref, zg_ref, g_ref, c0_ref, n0_ref, m0_ref = in_refs
    y_ref, c_ref, n_ref, m_ref = out_refs
    qc_ref, = scratch_refs
    dk, dv = ML_DK, ML_DV
    nr = bs // parts
    rows = slice(part * nr, (part + 1) * nr)
    eye = (lax.broadcasted_iota(jnp.int32, (dk, dk), 0)
           == lax.broadcasted_iota(jnp.int32, (dk, dk), 1))

    def as_column(row):
        return jnp.sum(jnp.where(eye, jnp.broadcast_to(row, (dk, dk)), 0.0), axis=1, keepdims=True)

    zg = zg_ref[rows, :]
    for h in range(ML_HEADS):
        li = zg[:, h:h + 1] + bi_ref[h]
        lf = _log_sigmoid(zg[:, ML_HEADS + h:ML_HEADS + h + 1] + bf_ref[h])
        m = m0_ref[rows, h:h + 1]
        inter = lf + m
        m_t = jnp.maximum(inter, li)
        dgate = jnp.exp(li - m_t)
        sc = jnp.exp(inter - m_t)
        q = z_ref[rows, OFF_Q + h * dk:OFF_Q + (h + 1) * dk]
        k = z_ref[rows, OFF_K + h * dk:OFF_K + (h + 1) * dk] * (ML_DK ** -0.5)
        v = z_ref[rows, OFF_V + h * dv:OFF_V + (h + 1) * dv]
        n_old = n0_ref[rows, h, :]
        qk = jnp.sum(q * k, axis=1, keepdims=True) * dgate
        w_end = jnp.exp(li - m_t)
        dec = jnp.exp(inter - m_t)
        wk = w_end * k
        for j in range(nr):
            c_old = c0_ref[part * nr + j, h]
            qc_ref[j:j + 1, :] = jnp.sum(as_column(q[j:j + 1, :]) * c_old, axis=0, keepdims=True)
            c_ref[part * nr + j, h] = (dec[j:j + 1, :] * c_old
                                       + as_column(wk[j:j + 1, :]) * v[j:j + 1, :])
        num = sc * qc_ref[0:nr, :] + qk * v
        den = sc * jnp.sum(q * n_old, axis=1, keepdims=True) + qk
        den = jnp.maximum(jnp.abs(den), jnp.exp(-m_t))
        hh = num / den
        n_ref[rows, h, :] = dec * n_old + wk
        m_ref[rows, h:h + 1] = m_t
        y = (_rms(hh, g_ref[:, h * dv:(h + 1) * dv])
             * _sigmoid(z_ref[rows, OFF_O + h * dv:OFF_O + (h + 1) * dv]))
        y_ref[rows, h * dv:(h + 1) * dv] = y.astype(y_ref.dtype)


def _mlstm_step_job(z, zg, c0, n0, m0, b_i, b_f, g_out, *, bs, parts):
    nb = z.shape[0]
    assert nb % bs == 0 and bs % parts == 0
    dk, dv, nh = ML_DK, ML_DV, ML_HEADS
    smem = pl.BlockSpec(memory_space=pltpu.SMEM)

    def in_specs(blk):
        return [
            smem, smem,
            pl.BlockSpec((bs, IN_MAIN), lambda *g: (blk(*g), 0)),
            pl.BlockSpec((bs, V7X_LANES), lambda *g: (blk(*g), 0)),
            pl.BlockSpec((1, nh * dv), lambda *g: (0, 0)),
            pl.BlockSpec((bs, nh, dk, dv), lambda *g: (blk(*g), 0, 0, 0)),
            pl.BlockSpec((bs, nh, dk), lambda *g: (blk(*g), 0, 0)),
            pl.BlockSpec((bs, nh), lambda *g: (blk(*g), 0)),
        ]

    def out_specs(blk):
        return [
            pl.BlockSpec((bs, nh * dv), lambda *g: (blk(*g), 0)),
            pl.BlockSpec((bs, nh, dk, dv), lambda *g: (blk(*g), 0, 0, 0)),
            pl.BlockSpec((bs, nh, dk), lambda *g: (blk(*g), 0, 0)),
            pl.BlockSpec((bs, nh), lambda *g: (blk(*g), 0)),
        ]

    return SideJob(
        n_blocks=nb // bs,
        parts=parts,
        args=(b_i, b_f, z, zg, g_out.reshape(1, nh * dv), c0, n0, m0),
        in_specs=in_specs,
        out_shape=(
            jax.ShapeDtypeStruct((nb, nh * dv), F32),
            jax.ShapeDtypeStruct((nb, nh, dk, dv), F32),
            jax.ShapeDtypeStruct((nb, nh, dk), F32),
            jax.ShapeDtypeStruct((nb, nh), F32),
        ),
        out_specs=out_specs,
        scratch_shapes=(pltpu.VMEM((bs, dv), F32),),
        body=functools.partial(_mlstm_step_body, bs=bs, parts=parts),
    )


def _run_side_job(job, *, vmem_mib, name):
    n_in, n_out = len(job.args), len(job.out_shape)

    def body(*refs):
        for part in range(job.parts):
            job.body(refs[:n_in], refs[n_in:n_in + n_out], refs[n_in + n_out:], part)

    block_of = lambda i: i
    return pl.pallas_call(
        body,
        grid=(job.n_blocks,),
        in_specs=job.in_specs(block_of),
        out_specs=job.out_specs(block_of),
        out_shape=list(job.out_shape),
        scratch_shapes=list(job.scratch_shapes),
        compiler_params=_params(1, vmem_mib),
        name=name,
    )(*job.args)


def _softmax_rows(s):
    e = jnp.exp(s - jnp.max(s, axis=-1, keepdims=True))
    return e / jnp.sum(e, axis=-1, keepdims=True)


def _xattn_block_kernel(*refs, n_mix):
    x_ref = refs[0]
    y_refs = refs[1:1 + n_mix]
    wy_refs = refs[1 + n_mix:1 + 2 * n_mix]
    g_ref, wq_ref, k_ref, v_ref, wo_ref, o_ref = refs[1 + 2 * n_mix:]
    x = x_ref[...]
    for y_ref, wy_ref in zip(y_refs, wy_refs):
        x = x + jnp.dot(y_ref[...], wy_ref[...], preferred_element_type=F32)
    xq = jnp.dot(_rms(x, g_ref[...]).astype(BF16), wq_ref[...],
                 preferred_element_type=F32).astype(BF16)
    heads = []
    for h in range(XA_HEADS):
        sl = slice(h * XA_DH, (h + 1) * XA_DH)
        s = lax.dot_general(xq[:, sl], k_ref[:, sl].astype(BF16),
                            (((1,), (1,)), ((), ())), preferred_element_type=F32)
        p = _softmax_rows(s * (XA_DH ** -0.5))
        heads.append(_mm(p, v_ref[:, sl]).astype(BF16))
    o_ref[...] = x + jnp.dot(jnp.concatenate(heads, axis=1), wo_ref[...],
                             preferred_element_type=F32)


def _xattn_block(x, mix_parts, mix_weights16, g, wq16, mk, mv, wo16, *, batch, seq, tq, name):
    nt = seq // tq
    assert seq % tq == 0 and all(w.dtype == BF16 for w in (*mix_weights16, wq16, wo16))
    assert all(p.dtype == BF16 for p in mix_parts)
    d = D_MODEL
    resident = dict(pipeline_mode=pl.Buffered(1))
    rows = lambda width: pl.BlockSpec((tq, width), lambda b, t: (b * nt + t, 0))
    return pl.pallas_call(
        functools.partial(_xattn_block_kernel, n_mix=len(mix_parts)),
        grid=(batch, nt),
        in_specs=[
            rows(d),
            *[rows(p.shape[1]) for p in mix_parts],
            *[pl.BlockSpec(w.shape, lambda b, t: (0, 0), **resident) for w in mix_weights16],
            pl.BlockSpec((1, d), lambda b, t: (0, 0)),
            pl.BlockSpec((d, d), lambda b, t: (0, 0), **resident),
            pl.BlockSpec((N_MEM, d), lambda b, t: (b, 0)),
            pl.BlockSpec((N_MEM, d), lambda b, t: (b, 0)),
            pl.BlockSpec((d, d), lambda b, t: (0, 0), **resident),
        ],
        out_specs=rows(d),
        out_shape=jax.ShapeDtypeStruct((batch * seq, d), F32),
        compiler_params=_params(2, 60),
        name=name,
    )(x, *mix_parts, *mix_weights16, g.reshape(1, d), wq16, mk, mv, wo16)


def _xattn_own_mask():
    nrow = N_MEM * XA_HEADS
    col_head = lax.broadcasted_iota(jnp.int32, (V7X_SUBLANES, nrow), 1) & (XA_HEADS - 1)
    row_head = lax.broadcasted_iota(jnp.int32, (V7X_SUBLANES, nrow), 0) & (XA_HEADS - 1)
    return col_head == row_head


def _xattn_probs(q_ref, k_ref, j, own):
    kf = k_ref[j].reshape(N_MEM * XA_HEADS, XA_DH)
    q8 = jnp.concatenate([q_ref[j]] * (V7X_SUBLANES // XA_HEADS), axis=0)
    s = lax.dot_general(q8.astype(BF16), kf.astype(BF16), (((1,), (1,)), ((), ())),
                        preferred_element_type=F32)
    s = jnp.where(own, s * (XA_DH ** -0.5), NEG)
    e = jnp.where(own, jnp.exp(s - jnp.max(s, axis=-1, keepdims=True)), 0.0)
    return e / jnp.sum(e, axis=-1, keepdims=True)


def _xattn_values(v_ref, o_ref, j, p):
    vf = v_ref[j].reshape(N_MEM * XA_HEADS, XA_DH)
    o_ref[j] = _mm(p, vf)[0:XA_HEADS, :]


def _xattn_step_kernel(q_ref, k_ref, v_ref, o_ref, *, sb):
    own = _xattn_own_mask()
    for j in range(sb):
        _xattn_values(v_ref, o_ref, j, _xattn_probs(q_ref, k_ref, j, own))


def _xattn_step(xq, ck, cv, *, sb, name):
    nb = xq.shape[0]
    assert XA_HEADS & (XA_HEADS - 1) == 0
    assert nb % sb == 0 and ck.shape == (nb, N_MEM, XA_HEADS, XA_DH)
    cache_spec = pl.BlockSpec((sb, N_MEM, XA_HEADS, XA_DH), lambda i: (i, 0, 0, 0))
    q_spec = pl.BlockSpec((sb, XA_HEADS, XA_DH), lambda i: (i, 0, 0))
    return pl.pallas_call(
        functools.partial(_xattn_step_kernel, sb=sb),
        grid=(nb // sb,),
        in_specs=[q_spec, cache_spec, cache_spec],
        out_specs=q_spec,
        out_shape=jax.ShapeDtypeStruct((nb, XA_HEADS, XA_DH), F32),
        compiler_params=_params(1, 40),
        name=name,
    )(xq.reshape(nb, XA_HEADS, XA_DH), ck, cv).reshape(nb, XA_HEADS * XA_DH)


class Tiles(NamedTuple):
    rows: int
    attn_rows: int
    mem_rows: int
    cols: int
    ffn_cols16: int
    ffn_cols32: int
    step_block: int
    step_parts: int
    xattn_rows: int


def _plan_tiles(prompt_rows, mem_rows):
    rows = min(prompt_rows, 1024)
    assert prompt_rows % rows == 0
    return Tiles(rows=rows, attn_rows=min(rows, 512), mem_rows=min(mem_rows, 512), cols=1024,
                 ffn_cols16=512, ffn_cols32=256, step_block=V7X_SUBLANES, step_parts=2,
                 xattn_rows=2)


def kernel(x_prompt, x_sample, mem_prompt, state_rg_h, state_rg_conv, state_ml_C, state_ml_n, state_ml_m, cache_mem_k, cache_mem_v, g_mix, w_in, conv_w, conv_b, w_rg_a, b_rg_a, w_rg_x, b_rg_x, rg_lambda, b_ml_i, b_ml_f, g_rg_out, g_ml_out, w_out, g_xa, g_mem, w_xa_q, w_xa_k, w_xa_v, w_xa_o, g_ffn, w_ffn_gate, w_ffn_up, w_ffn_down, g_final):
    depth = g_mix.shape[0]
    assert depth == 1, "single trunk layer"
    bp, seq, d = x_prompt.shape
    bs_, dec_seq, _ = x_sample.shape
    assert d == D_MODEL and dec_seq == 1
    n_mem = mem_prompt.shape[1]
    assert n_mem == N_MEM
    dff = w_ffn_gate.shape[-1]
    in_w = w_in.shape[-1]
    assert in_w == IN_MAIN + N_GATE

    w_in_t = jnp.swapaxes(w_in, 1, 2).reshape(in_w, d)
    w_gate_pad = jnp.pad(w_in_t[IN_MAIN:], ((0, V7X_LANES - N_GATE), (0, 0)))
    cw = conv_w.reshape(CONV_W, RG_WIDTH)
    wgate = jnp.concatenate([w_rg_a.reshape(RG_BLOCKS, RG_BLOCK, RG_BLOCK),
                             w_rg_x.reshape(RG_BLOCKS, RG_BLOCK, RG_BLOCK)], axis=-1)
    rg_args = (cw, conv_b.reshape(-1), wgate, b_rg_a.reshape(-1), b_rg_x.reshape(-1),
               rg_lambda.reshape(-1), g_rg_out.reshape(-1))
    b_i = b_ml_i.reshape(ML_HEADS)
    b_f = b_ml_f.reshape(ML_HEADS)
    g_ml = g_ml_out.reshape(-1)
    w_out2 = w_out.reshape(d, d)
    w_q = w_xa_q.reshape(d, d)
    w_k = w_xa_k.reshape(d, d)
    w_v = w_xa_v.reshape(d, d)
    w_o = w_xa_o.reshape(d, d)
    w_fg = w_ffn_gate.reshape(d, dff)
    w_fu = w_ffn_up.reshape(d, dff)
    w_fd = w_ffn_down.reshape(dff, d)

    tiles = _plan_tiles(bp * seq, bp * n_mem)
    row_tile = tiles.rows
    rg_zero_conv = jnp.zeros((bp, CONV_W - 1, RG_WIDTH), F32)
    rg_zero_h = jnp.zeros((bp, RG_WIDTH), F32)

    xs = x_sample.reshape(bs_, d)
    z_s, zg_s, w_in16 = _norm_linear(
        xs, g_mix.reshape(-1), w_in_t, n_out=IN_MAIN, tm=bs_, tn=tiles.cols, w_gate=w_gate_pad,
        w_is_nk=True, emit_w16=True, name="in_proj_s")
    conv_taps = jnp.swapaxes(state_rg_conv.reshape(bs_, CONV_W - 1, RG_WIDTH), 0, 1)
    y_rg_s, s_h, s_conv_taps = _rglru_step(z_s, conv_taps, state_rg_h.reshape(bs_, RG_WIDTH),
                                           *rg_args, name="rglru_s")
    s_conv = jnp.swapaxes(s_conv_taps, 0, 1)
    mlstm_s_job = _mlstm_step_job(
        z_s, zg_s, state_ml_C.reshape(bs_, ML_HEADS, ML_DK, ML_DV),
        state_ml_n.reshape(bs_, ML_HEADS, ML_DK), state_ml_m.reshape(bs_, ML_HEADS),
        b_i, b_f, g_ml, bs=tiles.step_block, parts=tiles.step_parts)
    tp = bp * seq
    xp = x_prompt.reshape(tp, d)
    in_cols, in_rows = IN_MAIN // tiles.cols, tp // row_tile
    if (mlstm_s_job.n_blocks % in_rows == 0
            and mlstm_s_job.n_blocks // in_rows * mlstm_s_job.parts < in_cols):
        z_p, zg_p, y_ml_s, s_c, s_n, s_m = _norm_linear(
            xp, g_mix.reshape(-1), w_in16, n_out=IN_MAIN, tm=row_tile, tn=tiles.cols,
            w_gate=w_gate_pad, w_is_nk=True, side=mlstm_s_job, x_parts=4, vmem_mib=60,
            name="in_proj_p")
    else:
        z_p, zg_p = _norm_linear(xp, g_mix.reshape(-1), w_in16, n_out=IN_MAIN, tm=row_tile,
                                 tn=tiles.cols, w_gate=w_gate_pad, w_is_nk=True, name="in_proj_p")
        y_ml_s, s_c, s_n, s_m = _run_side_job(mlstm_s_job, vmem_mib=32, name="mlstm_s")
    x1_s, w_out16_rg, w_out16_ml = _linear_residual(
        [y_rg_s, y_ml_s], [(w_out2, 0), (w_out2, 1)], xs, tm=bs_, tn=tiles.cols, emit_w16=True,
        name="mix_out_s")
    xq_s, w_q16 = _norm_linear(x1_s, g_xa.reshape(-1), w_q, n_out=d, tm=bs_, tn=tiles.cols,
                               emit_w16=True, name="xa_q_s")
    ck = cache_mem_k.reshape(bs_, n_mem, XA_HEADS, XA_DH)
    cv = cache_mem_v.reshape(bs_, n_mem, XA_HEADS, XA_DH)

    mix_tl = min(seq, ML_CHUNK)
    mix_steps = bp * (seq // mix_tl)
    mix_args = (z_p, zg_p, rg_zero_conv, rg_zero_h, *rg_args, b_i, b_f, g_ml)
    if bs_ % mix_steps == 0 and RG_BLOCKS % (bs_ // mix_steps) == 0:
        y_rg_p, p_h, p_conv, y_ml_p, p_c, p_n, p_m, o_s = _mixer_seq(
            *mix_args, batch=bp, seq=seq, tl=mix_tl, side_xattn=(xq_s, ck, cv), name="mixer_p")
        o_s = o_s.reshape(bs_, d)
    else:
        y_rg_p, p_h, p_conv, y_ml_p, p_c, p_n, p_m = _mixer_seq(
            *mix_args, batch=bp, seq=seq, tl=mix_tl, name="mixer_p")
        o_s = _xattn_step(xq_s, ck, cv, sb=tiles.xattn_rows, name="xattn_s")

    x3_s, w_o16 = _linear_residual([o_s], [(w_o, 0)], x1_s, tm=bs_, tn=tiles.cols, emit_w16=True,
                                   name="xa_out_s")

    mem2 = mem_prompt.reshape(bp * n_mem, d)
    tmem = tiles.mem_rows
    mk, mk_heads = _norm_linear(mem2, g_mem.reshape(-1), w_k, n_out=d, tm=tmem, tn=d,
                                out_dtype=BF16, heads_out=(XA_HEADS, XA_DH), name="mem_k")
    mv, mv_heads = _norm_linear(mem2, g_mem.reshape(-1), w_v, n_out=d, tm=tmem, tn=d,
                                out_dtype=BF16, heads_out=(XA_HEADS, XA_DH), name="mem_v")
    x3_p = _xattn_block(xp, [y_rg_p, y_ml_p], [w_out16_rg, w_out16_ml], g_xa.reshape(-1), w_q16,
                        mk, mv, w_o16, batch=bp, seq=seq, tq=min(seq, tiles.attn_rows), name="xattn_p")
    y_s, y_p0, w_fg16, w_fu16, w_fd16 = _ffn(
        [(x3_s, bs_), (x3_p, row_tile)], g_ffn.reshape(-1), w_fg, w_fu, w_fd, g_final, n_rows=1,
        tf=tiles.ffn_cols32, emit_w16=True, vmem_mib=60, name="ffn_s")
    if tp // row_tile > 1:
        y_p = _ffn([(x3_p, row_tile)], g_ffn.reshape(-1), w_fg16, w_fu16, w_fd16, g_final,
                   n_rows=tp // row_tile, tf=tiles.ffn_cols16, head_tile=y_p0, x_parts=4,
                   vmem_mib=60, name="ffn_p")
    else:
        y_p = y_p0

    return (
        y_p.reshape(bp, seq, d),
        y_s.reshape(bs_, 1, d),
        p_h.reshape(1, bp, RG_WIDTH),
        p_conv.reshape(1, bp, CONV_W - 1, RG_WIDTH),
        p_c.reshape(1, bp, ML_HEADS, ML_DK, ML_DV),
        p_n.reshape(1, bp, ML_HEADS, ML_DK),
        p_m[:, :, 0].reshape(1, bp, ML_HEADS),
        mk_heads.reshape(1, bp, n_mem, XA_HEADS, XA_DH),
        mv_heads.reshape(1, bp, n_mem, XA_HEADS, XA_DH),
        s_h.reshape(1, bs_, RG_WIDTH),
        s_conv.reshape(1, bs_, CONV_W - 1, RG_WIDTH),
        s_c.reshape(1, bs_, ML_HEADS, ML_DK, ML_DV),
        s_n.reshape(1, bs_, ML_HEADS, ML_DK),
        s_m.reshape(1, bs_, ML_HEADS),
    )
```

```python
import functools
from typing import Callable, NamedTuple

import jax
import jax.numpy as jnp
from jax import lax
from jax.experimental import pallas as pl
from jax.experimental.pallas import tpu as pltpu

F32 = jnp.float32
BF16 = jnp.bfloat16

D_MODEL = 2048
RG_WIDTH = D_MODEL // 2
RG_BLOCKS = 8
RG_BLOCK = RG_WIDTH // RG_BLOCKS
CONV_W = 4
RG_C = 8.0
ML_HEADS = 4
ML_WIDTH = D_MODEL - RG_WIDTH
ML_DV = ML_WIDTH // ML_HEADS
ML_DK = ML_DV // 2
N_MEM = 256
XA_HEADS = 4
XA_DH = D_MODEL // XA_HEADS
EPS = 1e-6
NEG = -1e30

OFF_RGX = 0
OFF_RGG = OFF_RGX + RG_WIDTH
OFF_Q = OFF_RGG + RG_WIDTH
OFF_K = OFF_Q + ML_HEADS * ML_DK
OFF_V = OFF_K + ML_HEADS * ML_DK
OFF_O = OFF_V + ML_WIDTH
OFF_I = OFF_O + ML_WIDTH
IN_MAIN = OFF_I
N_GATE = 2 * ML_HEADS

V7X_LANES = 128
V7X_SUBLANES = 8
V7X_VMEM_BYTES = 64 * 2**20

ML_CHUNK = 256


class SideJob(NamedTuple):
    n_blocks: int
    parts: int
    args: tuple
    in_specs: Callable
    out_shape: tuple
    out_specs: Callable
    scratch_shapes: tuple
    body: Callable


def _params(n_axes, vmem_mib):
    assert vmem_mib * 2**20 <= V7X_VMEM_BYTES
    return pltpu.CompilerParams(
        dimension_semantics=("arbitrary",) * n_axes,
        vmem_limit_bytes=vmem_mib * 2**20,
    )


def _rms(x, g):
    ms = jnp.mean(x * x, axis=-1, keepdims=True)
    return x * lax.rsqrt(ms + EPS) * g


def _softplus(u):
    return jnp.maximum(u, 0.0) + jnp.log1p(jnp.exp(-jnp.abs(u)))


def _log_sigmoid(u):
    return -_softplus(-u)


def _sigmoid(u):
    return 0.5 * jnp.tanh(0.5 * u) + 0.5


def _gelu_tanh(x):
    c = 0.7978845608028654
    half_x = 0.5 * x
    return half_x + half_x * jnp.tanh(x * (c + (c * 0.044715) * (x * x)))


def _sqrt_nonneg(v):
    return jnp.where(v > 0.0, v * lax.rsqrt(v), 0.0)


def _mm(a, b):
    return jnp.dot(a.astype(BF16), b.astype(BF16), preferred_element_type=F32)


def _exact_cumsum_lanes(x):
    n = x.shape[-1]
    upper = jnp.where(lax.broadcasted_iota(jnp.int32, (n, n), 0)
                      <= lax.broadcasted_iota(jnp.int32, (n, n), 1), 1.0, 0.0).astype(BF16)
    hi = x.astype(BF16)
    rest = x - hi.astype(F32)
    mid = rest.astype(BF16)
    lo = (rest - mid.astype(F32)).astype(BF16)
    return (jnp.dot(hi, upper, preferred_element_type=F32)
            + jnp.dot(mid, upper, preferred_element_type=F32)
            + jnp.dot(lo, upper, preferred_element_type=F32))


def _dot_w(a, w_ref, w_is_nk):
    w = w_ref[...].astype(BF16)
    if w_is_nk:
        return lax.dot_general(a, w, (((1,), (1,)), ((), ())), preferred_element_type=F32)
    return jnp.dot(a, w, preferred_element_type=F32)


def _norm_linear_kernel(*refs, x_parts, with_gate, w_is_nk, emit_w16, heads_out, side, side_cols):
    x_refs, refs = list(refs[:x_parts]), list(refs[x_parts - 1:])
    n_si, n_so, n_ss = ((len(side.args), len(side.out_shape), len(side.scratch_shapes))
                        if side else (0, 0, 0))
    n_in = 3 + with_gate + n_si
    n_out = 1 + bool(heads_out) + with_gate + emit_w16 + n_so
    g_ref, w_ref = refs[1:3]
    wg_ref = refs[3] if with_gate else None
    side_in = refs[3 + with_gate:n_in]
    outs = refs[n_in:n_in + n_out]
    xn_ref = refs[n_in + n_out]
    side_scratch = refs[n_in + n_out + 1:]
    o_ref = outs[0]
    oh_ref = outs[1] if heads_out else None
    og_ref = outs[1 + bool(heads_out)] if with_gate else None
    w16_ref = outs[n_out - n_so - 1] if emit_w16 else None
    side_out = outs[n_out - n_so:]
    col = pl.program_id(1)

    def column_tile(xn):
        acc = _dot_w(xn, w_ref, w_is_nk)
        o_ref[...] = acc.astype(o_ref.dtype)
        if heads_out:
            oh_ref[...] = acc.reshape(oh_ref.shape)
        if emit_w16:
            w16_ref[...] = w_ref[...].astype(BF16)

    @pl.when(col == 0)
    def _():
        if x_parts == 1:
            xn = _rms(x_refs[0][...], g_ref[...]).astype(BF16)
            xn_ref[...] = xn
        else:
            part = xn_ref.shape[0] // x_parts
            for q, x_ref in enumerate(x_refs):
                xn_ref[q * part:(q + 1) * part, :] = _rms(x_ref[...], g_ref[...]).astype(BF16)
            xn = xn_ref[...]
        if with_gate:
            og_ref[...] = _dot_w(xn, wg_ref, w_is_nk)
        column_tile(xn)

    if side:
        for part in range(side.parts):
            @pl.when((col > 0) & (col <= side_cols) & ((col - 1) % side.parts == part))
            def _(part=part):
                column_tile(xn_ref[...])
                side.body(side_in, side_out, side_scratch, part)

    @pl.when(col > (side_cols if side else 0))
    def _():
        column_tile(xn_ref[...])


def _norm_linear(x, g, w, *, n_out, tm, tn, out_dtype=F32, w_gate=None, w_is_nk=False,
                 emit_w16=False, heads_out=None, side=None, x_parts=1, vmem_mib=48, name):
    m, k = x.shape
    k_ax, n_ax = (1, 0) if w_is_nk else (0, 1)
    assert m % tm == 0 and n_out % tn == 0 and w.shape[k_ax] == k and n_out <= w.shape[n_ax]
    assert not emit_w16 or m == tm
    assert heads_out is None or (tn == n_out and heads_out[0] * heads_out[1] == n_out)
    n_rows, n_cols = m // tm, n_out // tn
    with_gate = w_gate is not None
    w_mode = dict(pipeline_mode=pl.Buffered(1)) if tn == n_out else {}
    w_spec = (pl.BlockSpec((tn, k), lambda i, j: (j, 0), **w_mode) if w_is_nk
              else pl.BlockSpec((k, tn), lambda i, j: (0, j), **w_mode))
    assert tm % x_parts == 0 and x_parts <= n_cols
    if x_parts == 1:
        x_specs = [pl.BlockSpec((tm, k), lambda i, j: (i, 0))]
    else:
        x_specs = [
            pl.BlockSpec((tm // x_parts, k),
                         lambda i, j, q=q: (jnp.minimum(i + (j >= n_cols - (x_parts - 1) + q),
                                                        n_rows - 1) * x_parts + q, 0))
            for q in range(x_parts)]
    in_specs = [
        *x_specs,
        pl.BlockSpec((1, k), lambda i, j: (0, 0)),
        w_spec,
    ]
    out_shape = [jax.ShapeDtypeStruct((m, n_out), out_dtype)]
    out_specs = [pl.BlockSpec((tm, tn), lambda i, j: (i, j))]
    scratch_shapes = [pltpu.VMEM((tm, k), BF16)]
    args = [*([x] * x_parts), g.reshape(1, k), w]
    if heads_out:
        out_shape.append(jax.ShapeDtypeStruct((m, *heads_out), F32))
        out_specs.append(pl.BlockSpec((tm, *heads_out), lambda i, j: (i, 0, 0)))
    if with_gate:
        ng = w_gate.shape[n_ax]
        in_specs.append(pl.BlockSpec(w_gate.shape, lambda i, j: (0, 0)))
        out_shape.append(jax.ShapeDtypeStruct((m, ng), F32))
        out_specs.append(pl.BlockSpec((tm, ng), lambda i, j: (i, 0)))
        args.append(w_gate)
    if emit_w16:
        out_shape.append(jax.ShapeDtypeStruct((n_out, k) if w_is_nk else (k, n_out), BF16))
        out_specs.append(pl.BlockSpec((tn, k), lambda i, j: (j, 0)) if w_is_nk
                         else pl.BlockSpec((k, tn), lambda i, j: (0, j)))
    side_cols = 0
    if side:
        assert side.n_blocks % n_rows == 0
        blocks_per_row = side.n_blocks // n_rows
        side_cols = blocks_per_row * side.parts
        assert 0 < side_cols < n_cols
        block_of = lambda i, j: (i * blocks_per_row
                                 + jnp.clip(j - 1, 0, side_cols - 1) // side.parts)
        in_specs += side.in_specs(block_of)
        out_specs += side.out_specs(block_of)
        out_shape += list(side.out_shape)
        scratch_shapes += list(side.scratch_shapes)
        args += list(side.args)
    out = pl.pallas_call(
        functools.partial(_norm_linear_kernel, x_parts=x_parts, with_gate=with_gate, w_is_nk=w_is_nk,
                          emit_w16=emit_w16, heads_out=heads_out, side=side, side_cols=side_cols),
        grid=(n_rows, n_cols),
        in_specs=in_specs,
        out_specs=out_specs,
        out_shape=out_shape,
        scratch_shapes=scratch_shapes,
        compiler_params=_params(2, vmem_mib),
        name=name,
    )(*args)
    return out if len(out) > 1 else out[0]


def _linear_res_kernel(*refs, n_in, emit_w16):
    a_refs = refs[:n_in]
    w_refs = refs[n_in:2 * n_in]
    res_ref = refs[2 * n_in]
    o_ref = refs[2 * n_in + 1]
    acc = res_ref[...]
    for a_ref, w_ref in zip(a_refs, w_refs):
        acc = acc + _mm(a_ref[...], w_ref[...])
    o_ref[...] = acc
    if emit_w16:
        for w_ref, wc_ref in zip(w_refs, refs[2 * n_in + 2:]):
            wc_ref[...] = w_ref[...].astype(BF16)


def _linear_residual(parts, weights, res, *, tm, tn, emit_w16=False, vmem_mib=48, name):
    m, n = res.shape
    kp = parts[0].shape[1]
    assert all(p.shape == (m, kp) for p in parts) and len(weights) == len(parts)
    assert m % tm == 0 and n % tn == 0 and (not emit_w16 or m == tm)
    n_in = len(parts)
    in_specs = [pl.BlockSpec((tm, kp), lambda i, j: (i, 0)) for _ in parts]
    w_mode = dict(pipeline_mode=pl.Buffered(1)) if tn == n else {}
    in_specs += [pl.BlockSpec((kp, tn), lambda i, j, rb=rb: (rb, j), **w_mode)
                 for _, rb in weights]
    in_specs.append(pl.BlockSpec((tm, tn), lambda i, j: (i, j)))
    out_specs = [pl.BlockSpec((tm, tn), lambda i, j: (i, j))]
    out_shape = [jax.ShapeDtypeStruct((m, n), F32)]
    if emit_w16:
        out_specs += [pl.BlockSpec((kp, tn), lambda i, j: (0, j)) for _ in parts]
        out_shape += [jax.ShapeDtypeStruct((kp, n), BF16) for _ in parts]
    out = pl.pallas_call(
        functools.partial(_linear_res_kernel, n_in=n_in, emit_w16=emit_w16),
        grid=(m // tm, n // tn),
        in_specs=in_specs,
        out_specs=out_specs,
        out_shape=out_shape,
        compiler_params=_params(2, vmem_mib),
        name=name,
    )(*parts, *[w for w, _ in weights], res)
    return out if emit_w16 else out[0]


FFN_HEAD_SLABS = 8


def _ffn_kernel(*refs, n_groups, x_parts, emit_w16, head_tile):
    x_part_refs, refs = refs[:x_parts], refs[x_parts - 1:]
    x_refs = refs[:n_groups]
    g_ref, wg_ref, wu_ref, wd_ref, gf_ref = refs[n_groups:n_groups + 5]
    n_in = n_groups + 5 + bool(head_tile)
    head_ref = refs[n_in - 1] if head_tile else None
    o_refs = refs[n_in:n_in + n_groups]
    w16_refs = refs[n_in + n_groups:n_in + n_groups + 3] if emit_w16 else ()
    xf_refs = refs[-n_groups:]
    row = pl.program_id(0)
    f = pl.program_id(1)
    last = pl.num_programs(1) - 1
    computing = (row > 0) if head_tile else True

    def weights():
        wg = wg_ref[...].astype(BF16)
        wu = wu_ref[...].astype(BF16)
        wd = wd_ref[...].astype(BF16)
        for dst, val in zip(w16_refs, (wg, wu, wd)):
            dst[...] = val
        return wg, wu, wd

    def hidden_tile(xf, w):
        wg, wu, wd = w
        gate = jnp.dot(xf, wg, preferred_element_type=F32)
        up = jnp.dot(xf, wu, preferred_element_type=F32)
        hidden = (gate * _sigmoid(gate)) * up
        return jnp.dot(hidden.astype(BF16), wd, preferred_element_type=F32)

    if head_tile:
        slab = head_ref.shape[0]

        @pl.when((row == 0) & (f < FFN_HEAD_SLABS))
        def _():
            o_refs[0][pl.ds(pl.multiple_of(f * slab, slab), slab), :] = head_ref[...]

    @pl.when(computing & (f == 0))
    def _():
        w = weights()
        if x_parts > 1:
            part = xf_refs[0].shape[0] // x_parts
            for q, x_ref in enumerate(x_part_refs):
                x = x_ref[...]
                xf_refs[0][q * part:(q + 1) * part, :] = _rms(x, g_ref[...]).astype(BF16)
                o_refs[0][q * part:(q + 1) * part, :] = x
            o_refs[0][...] += hidden_tile(xf_refs[0][...], w)
        else:
            for x_ref, o_ref, xf_ref in zip(x_refs, o_refs, xf_refs):
                x = x_ref[...]
                xf = _rms(x, g_ref[...]).astype(BF16)
                xf_ref[...] = xf
                o_ref[...] = x + hidden_tile(xf, w)

    @pl.when(computing & (f > 0) & (f < last))
    def _():
        w = weights()
        for o_ref, xf_ref in zip(o_refs, xf_refs):
            o_ref[...] += hidden_tile(xf_ref[...], w)

    @pl.when(computing & (f == last))
    def _():
        w = weights()
        for o_ref, xf_ref in zip(o_refs, xf_refs):
            o_ref[...] = _rms(o_ref[...] + hidden_tile(xf_ref[...], w), gf_ref[...])


def _ffn(groups, g, w_gate, w_up, w_down, g_final, *, n_rows, tf, emit_w16=False, head_tile=None,
         x_parts=1, vmem_mib, name):
    d = groups[0][0].shape[1]
    dff = w_gate.shape[1]
    n_groups = len(groups)
    assert dff % tf == 0 and dff // tf >= 2 and (not emit_w16 or n_rows == 1)
    assert head_tile is None or (n_groups == 1 and head_tile.shape == (groups[0][1], d))
    assert all(x.shape[0] >= n_rows * tm for x, tm in groups)
    col = (lambda i, f: jnp.where(i > 0, f, 0)) if head_tile is not None else (lambda i, f: f)
    up_spec = pl.BlockSpec((d, tf), lambda i, f: (0, col(i, f)))
    down_spec = pl.BlockSpec((tf, d), lambda i, f: (col(i, f), 0))
    vec_spec = pl.BlockSpec((1, d), lambda i, f: (0, 0))
    row_specs = [pl.BlockSpec((tm, d), lambda i, f: (i, 0)) for _, tm in groups]
    x_specs, n_f = row_specs, dff // tf
    if x_parts > 1:
        tm0 = groups[0][1]
        assert n_groups == 1 and tm0 % x_parts == 0 and x_parts <= n_f
        x_specs = [
            pl.BlockSpec((tm0 // x_parts, d),
                         lambda i, f, q=q: (jnp.minimum(i + (f >= n_f - (x_parts - 1) + q),
                                                        n_rows - 1) * x_parts + q, 0))
            for q in range(x_parts)]
    in_specs = x_specs + [vec_spec, up_spec, up_spec, down_spec, vec_spec]
    args = ([x for x, _ in groups] * x_parts
            + [g.reshape(1, d), w_gate, w_up, w_down, g_final.reshape(1, d)])
    if head_tile is not None:
        tm0 = groups[0][1]
        assert tm0 % FFN_HEAD_SLABS == 0 and dff // tf >= FFN_HEAD_SLABS
        last_slab = FFN_HEAD_SLABS - 1
        in_specs.append(pl.BlockSpec(
            (tm0 // FFN_HEAD_SLABS, d),
            lambda i, f: (jnp.where(i == 0, jnp.minimum(f, last_slab), last_slab), 0)))
        args.append(head_tile)
    out_specs = list(row_specs)
    out_shape = [jax.ShapeDtypeStruct((n_rows * tm, d), F32) for _, tm in groups]
    if emit_w16:
        out_specs += [pl.BlockSpec((d, tf), lambda i, f: (0, f))] * 2
        out_specs += [pl.BlockSpec((tf, d), lambda i, f: (f, 0))]
        out_shape += [jax.ShapeDtypeStruct(w.shape, BF16) for w in (w_gate, w_up, w_down)]
    out = pl.pallas_call(
        functools.partial(_ffn_kernel, n_groups=n_groups, x_parts=x_parts, emit_w16=emit_w16,
                          head_tile=head_tile is not None),
        grid=(n_rows, dff // tf),
        in_specs=in_specs,
        out_specs=out_specs,
        out_shape=out_shape,
        scratch_shapes=[pltpu.VMEM((tm, d), BF16) for _, tm in groups],
        compiler_params=_params(2, vmem_mib),
        name=name,
    )(*args)
    return out if len(out) > 1 else out[0]


RG_GATE_LOOKAHEAD = 2


def _rg_gates(xr, wgate_ref, ba_ref, bx_ref, lam_ref, a_ref, b_ref, side=None):
    blocks = [slice(n * RG_BLOCK, (n + 1) * RG_BLOCK) for n in range(RG_BLOCKS)]
    gates = {}

    def issue_gate(n):
        if n < RG_BLOCKS:
            gates[n] = _mm(xr[:, blocks[n]], wgate_ref[n])

    rows, score_phase, value_phase = side if side else (0, None, None)
    per = RG_BLOCKS // rows if rows else RG_BLOCKS
    scores = {}
    for n in range(RG_GATE_LOOKAHEAD):
        issue_gate(n)
    for n, sl in enumerate(blocks):
        if rows and n % per == 0:
            j = n // per
            scores[j] = score_phase(j)
            if j > 0:
                value_phase(j - 1, scores.pop(j - 1))
        xn = xr[:, sl]
        g = gates.pop(n)
        issue_gate(n + RG_GATE_LOOKAHEAD)
        r = _sigmoid(g[:, :RG_BLOCK] + ba_ref[:, sl])
        ig = _sigmoid(g[:, RG_BLOCK:] + bx_ref[:, sl])
        a = jnp.exp(r * (-RG_C * _softplus(-lam_ref[:, sl])))
        a_ref[:, sl] = a
        mult = _sqrt_nonneg(jnp.maximum(1.0 - a * a, 0.0))
        b_ref[:, sl] = mult * (ig * xn)
    if rows:
        value_phase(rows - 1, scores.pop(rows - 1))


def _mlstm_chunk(q, k, v, o_gate, li_col, li_row, lf_row, bcum_row, causal, g, c_state, n_state, m):
    cs = q.shape[0]
    bcum_col = jnp.sum(jnp.where(causal, lf_row, 0.0), axis=1, keepdims=True)
    log_d = jnp.where(causal, bcum_col - bcum_row + li_row, NEG)
    inter = bcum_col + m
    m_t = jnp.maximum(inter, jnp.max(log_d, axis=1, keepdims=True))
    dmat = jnp.exp(log_d - m_t)
    sc = jnp.exp(inter - m_t)
    qb = q.astype(BF16)
    kb = k.astype(BF16)
    vb = v.astype(BF16)
    qk = lax.dot_general(qb, kb, (((1,), (1,)), ((), ())), preferred_element_type=F32) * dmat
    num = sc * jnp.dot(qb, c_state.astype(BF16), preferred_element_type=F32) + _mm(qk, vb)
    den = sc * jnp.sum(q * n_state, axis=1, keepdims=True) + jnp.sum(qk, axis=1, keepdims=True)
    den = jnp.maximum(jnp.abs(den), jnp.exp(-m_t))
    hh = num / den
    m_new = m_t[cs - 1:cs, :]
    b_last = bcum_col[cs - 1:cs, :]
    w_end = jnp.exp(b_last - bcum_col + li_col - m_new)
    dec = jnp.exp(b_last + m - m_new)
    wk = w_end * k
    c_new = dec * c_state + lax.dot_general(
        wk.astype(BF16), vb, (((0,), (0,)), ((), ())), preferred_element_type=F32)
    n_new = dec * n_state + jnp.sum(wk, axis=0, keepdims=True)
    y = _rms(hh, g) * _sigmoid(o_gate)
    return y, c_new, n_new, m_new


N_RG_IN = 10
N_ML_IN = 4


def _mixer_seq_kernel(*refs, tl, side_rows):
    (z_ref, conv0_ref, h0_ref, cw_ref, cb_ref, wgate_ref, ba_ref, bx_ref, lam_ref,
     gout_ref) = refs[:N_RG_IN]
    bi_ref, bf_ref, gates_ref, gml_ref = refs[N_RG_IN:N_RG_IN + N_ML_IN]
    n_in = N_RG_IN + N_ML_IN + (3 if side_rows else 0)
    if side_rows:
        sq_ref, sk_ref, sv_ref = refs[N_RG_IN + N_ML_IN:n_in]
    y_ref, hlast_ref, convn_ref, yml_ref, cout_ref, nout_ref, mout_ref = refs[n_in:n_in + 7]
    so_ref = refs[n_in + 7] if side_rows else None
    xe_ref, a_ref, b_ref, h_ref, hc_ref, lf_ref, cst_ref, nst_ref, mst_ref = refs[-9:]
    t = pl.program_id(1)
    pad = V7X_SUBLANES

    @pl.when(t == 0)
    def _():
        xe_ref[pad - 3:pad, :] = conv0_ref[...]
        hc_ref[...] = h0_ref[...]
        cst_ref[...] = jnp.zeros_like(cst_ref)
        nst_ref[...] = jnp.zeros_like(nst_ref)
        mst_ref[...] = jnp.zeros_like(mst_ref)
        lf_ref[...] = jnp.zeros_like(lf_ref)

    @pl.when(t > 0)
    def _():
        xe_ref[pad - 3:pad, :] = xe_ref[tl + pad - 3:tl + pad, :]

    x = z_ref[:, OFF_RGX:OFF_RGX + RG_WIDTH]
    xe_ref[pad:tl + pad, :] = x
    xr = (xe_ref[pad - 3:tl + pad - 3, :] * cw_ref[0:1, :]
          + xe_ref[pad - 2:tl + pad - 2, :] * cw_ref[1:2, :]
          + xe_ref[pad - 1:tl + pad - 1, :] * cw_ref[2:3, :]
          + x * cw_ref[3:4, :]) + cb_ref[...]
    side = None
    if side_rows:
        own = _xattn_own_mask()
        side = (side_rows,
                lambda j: _xattn_probs(sq_ref, sk_ref, j, own),
                lambda j, p: _xattn_values(sv_ref, so_ref, j, p))

    _rg_gates(xr, wgate_ref, ba_ref, bx_ref, lam_ref, a_ref, b_ref, side=side)

    gates = gates_ref[...]
    gates_t = gates.T
    for h in range(ML_HEADS):
        lf_ref[h:h + 1, :] = _log_sigmoid(gates_t[ML_HEADS + h:ML_HEADS + h + 1, :] + bf_ref[h])
    lf_rows = lf_ref[...]
    bcum_rows = _exact_cumsum_lanes(lf_rows)
    causal = (lax.broadcasted_iota(jnp.int32, (tl, tl), 1)
              <= lax.broadcasted_iota(jnp.int32, (tl, tl), 0))
    for h in range(ML_HEADS):
        vo = slice(h * ML_DV, (h + 1) * ML_DV)
        col = lambda off, width: slice(off + h * width, off + (h + 1) * width)
        y_h, c_new, n_new, m_new = _mlstm_chunk(
            z_ref[:, col(OFF_Q, ML_DK)], z_ref[:, col(OFF_K, ML_DK)] * (ML_DK ** -0.5),
            z_ref[:, col(OFF_V, ML_DV)], z_ref[:, col(OFF_O, ML_DV)],
            gates[:, h:h + 1] + bi_ref[h], gates_t[h:h + 1, :] + bi_ref[h],
            lf_rows[h:h + 1, :], bcum_rows[h:h + 1, :], causal, gml_ref[:, vo],
            cst_ref[h], nst_ref[h:h + 1, :], mst_ref[h:h + 1, 0:1])
        cst_ref[h] = c_new
        nst_ref[h:h + 1, :] = n_new
        mst_ref[h:h + 1, :] = jnp.broadcast_to(m_new, (1, V7X_LANES))
        yml_ref[:, vo] = y_h.astype(yml_ref.dtype)
    cout_ref[...] = cst_ref[...]
    nout_ref[...] = nst_ref[...]
    mout_ref[...] = mst_ref[...]

    row = lax.broadcasted_iota(jnp.int32, (V7X_SUBLANES, RG_WIDTH), 0)

    def group(gi, hc):
        r0 = pl.multiple_of(gi * V7X_SUBLANES, V7X_SUBLANES)
        a8 = a_ref[pl.ds(r0, V7X_SUBLANES), :]
        b8 = b_ref[pl.ds(r0, V7X_SUBLANES), :]
        for d in (1, 2, 4):
            keep = row >= d
            b8 = jnp.where(keep, a8 * pltpu.roll(b8, d, axis=0) + b8, b8)
            a8 = jnp.where(keep, a8 * pltpu.roll(a8, d, axis=0), a8)
        h8 = a8 * hc + b8
        h_ref[pl.ds(r0, V7X_SUBLANES), :] = h8
        return h8[V7X_SUBLANES - 1:V7X_SUBLANES, :]

    hc = lax.fori_loop(0, tl // V7X_SUBLANES, group, hc_ref[...], unroll=4)
    hc_ref[...] = hc
    hlast_ref[...] = hc
    convn_ref[...] = xe_ref[tl + pad - 3:tl + pad, :]
    y = h_ref[...] * _gelu_tanh(z_ref[:, OFF_RGG:OFF_RGG + RG_WIDTH])
    y_ref[...] = _rms(y, gout_ref[...]).astype(y_ref.dtype)


def _mixer_seq(z, zgates, conv0, h0, cw, cb, wgate, ba, bx, lam, gout, b_i, b_f, g_ml, *,
               batch, seq, tl, name, side_xattn=None):
    nt = seq // tl
    assert seq % tl == 0 and zgates.shape[1] == V7X_LANES
    w = RG_WIDTH
    dk, dv, nh = ML_DK, ML_DV, ML_HEADS
    row = lambda v: v.reshape(1, w)
    const2 = lambda b, t: (0, 0)
    smem = pl.BlockSpec(memory_space=pltpu.SMEM)
    tile = lambda width: pl.BlockSpec((tl, width), lambda b, t: (b * nt + t, 0))
    in_specs = [
        tile(IN_MAIN),
        pl.BlockSpec((None, CONV_W - 1, w), lambda b, t: (b, 0, 0)),
        pl.BlockSpec((None, 1, w), lambda b, t: (b, 0, 0)),
        pl.BlockSpec((CONV_W, w), const2),
        pl.BlockSpec((1, w), const2),
        pl.BlockSpec((RG_BLOCKS, RG_BLOCK, 2 * RG_BLOCK), lambda b, t: (0, 0, 0)),
        pl.BlockSpec((1, w), const2),
        pl.BlockSpec((1, w), const2),
        pl.BlockSpec((1, w), const2),
        pl.BlockSpec((1, w), const2),
        smem, smem,
        tile(V7X_LANES),
        pl.BlockSpec((1, nh * dv), const2),
    ]
    assert len(in_specs) == N_RG_IN + N_ML_IN and z.shape[1] == IN_MAIN
    out_specs = [
        pl.BlockSpec((tl, w), lambda b, t: (b * nt + t, 0)),
        pl.BlockSpec((None, 1, w), lambda b, t: (b, 0, 0)),
        pl.BlockSpec((None, CONV_W - 1, w), lambda b, t: (b, 0, 0)),
        pl.BlockSpec((tl, nh * dv), lambda b, t: (b * nt + t, 0)),
        pl.BlockSpec((None, nh, dk, dv), lambda b, t: (b, 0, 0, 0)),
        pl.BlockSpec((None, nh, dk), lambda b, t: (b, 0, 0)),
        pl.BlockSpec((None, nh, V7X_LANES), lambda b, t: (b, 0, 0)),
    ]
    out_shape = [
        jax.ShapeDtypeStruct((batch * seq, w), BF16),
        jax.ShapeDtypeStruct((batch, 1, w), F32),
        jax.ShapeDtypeStruct((batch, CONV_W - 1, w), F32),
        jax.ShapeDtypeStruct((batch * seq, nh * dv), BF16),
        jax.ShapeDtypeStruct((batch, nh, dk, dv), F32),
        jax.ShapeDtypeStruct((batch, nh, dk), F32),
        jax.ShapeDtypeStruct((batch, nh, V7X_LANES), F32),
    ]
    args = [z, conv0, h0.reshape(batch, 1, w), cw, row(cb), wgate, row(ba), row(bx), row(lam),
            row(gout), b_i, b_f, zgates, g_ml.reshape(1, nh * dv)]
    side_rows = 0
    vmem_mib = 40
    if side_xattn is not None:
        xq, ck, cv = side_xattn
        nb = xq.shape[0]
        assert nb % (batch * nt) == 0 and ck.shape == (nb, N_MEM, XA_HEADS, XA_DH)
        side_rows = nb // (batch * nt)
        q_spec = pl.BlockSpec((side_rows, XA_HEADS, XA_DH), lambda b, t: (b * nt + t, 0, 0))
        cache_spec = pl.BlockSpec((side_rows, N_MEM, XA_HEADS, XA_DH),
                                  lambda b, t: (b * nt + t, 0, 0, 0))
        in_specs += [q_spec, cache_spec, cache_spec]
        out_specs.append(q_spec)
        out_shape.append(jax.ShapeDtypeStruct((nb, XA_HEADS, XA_DH), F32))
        args += [xq.reshape(nb, XA_HEADS, XA_DH), ck, cv]
        vmem_mib = 60
    return pl.pallas_call(
        functools.partial(_mixer_seq_kernel, tl=tl, side_rows=side_rows),
        grid=(batch, nt),
        in_specs=in_specs,
        out_specs=out_specs,
        out_shape=out_shape,
        scratch_shapes=[
            pltpu.VMEM((tl + V7X_SUBLANES, w), F32),
            pltpu.VMEM((tl, w), F32),
            pltpu.VMEM((tl, w), F32),
            pltpu.VMEM((tl, w), F32),
            pltpu.VMEM((1, w), F32),
            pltpu.VMEM((V7X_SUBLANES, tl), F32),
            pltpu.VMEM((nh, dk, dv), F32),
            pltpu.VMEM((nh, dk), F32),
            pltpu.VMEM((nh, V7X_LANES), F32),
        ],
        compiler_params=_params(2, vmem_mib),
        name=name,
    )(*args)


def _rglru_step_kernel(zx_ref, zg_ref, conv_ref, h0_ref, cw_ref, cb_ref, wgate_ref, ba_ref,
                       bx_ref, lam_ref, gout_ref, y_ref, hn_ref, convn_ref, a_ref, b_ref):
    w = RG_WIDTH
    x = zx_ref[...]
    xr = (conv_ref[0] * cw_ref[0:1, :] + conv_ref[1] * cw_ref[1:2, :]
          + conv_ref[2] * cw_ref[2:3, :] + x * cw_ref[3:4, :]) + cb_ref[...]
    _rg_gates(xr, wgate_ref, ba_ref, bx_ref, lam_ref, a_ref, b_ref)
    h = a_ref[...] * h0_ref[...] + b_ref[...]
    hn_ref[...] = h
    convn_ref[0] = conv_ref[1]
    convn_ref[1] = conv_ref[2]
    convn_ref[2] = x
    y_ref[...] = _rms(h * _gelu_tanh(zg_ref[...]), gout_ref[...]).astype(y_ref.dtype)


def _rglru_step(z, conv, h0, cw, cb, wgate, ba, bx, lam, gout, *, name):
    nb = z.shape[0]
    w = RG_WIDTH
    row = lambda v: v.reshape(1, w)
    c0 = lambda i: (0, 0)
    return pl.pallas_call(
        _rglru_step_kernel,
        grid=(1,),
        in_specs=[
            pl.BlockSpec((nb, w), lambda i: (0, OFF_RGX // w)),
            pl.BlockSpec((nb, w), lambda i: (0, OFF_RGG // w)),
            pl.BlockSpec((CONV_W - 1, nb, w), lambda i: (0, 0, 0)),
            pl.BlockSpec((nb, w), c0),
            pl.BlockSpec((CONV_W, w), c0),
            pl.BlockSpec((1, w), c0),
            pl.BlockSpec((RG_BLOCKS, RG_BLOCK, 2 * RG_BLOCK), lambda i: (0, 0, 0)),
            pl.BlockSpec((1, w), c0),
            pl.BlockSpec((1, w), c0),
            pl.BlockSpec((1, w), c0),
            pl.BlockSpec((1, w), c0),
        ],
        out_specs=[
            pl.BlockSpec((nb, w), c0),
            pl.BlockSpec((nb, w), c0),
            pl.BlockSpec((CONV_W - 1, nb, w), lambda i: (0, 0, 0)),
        ],
        out_shape=[
            jax.ShapeDtypeStruct((nb, w), BF16),
            jax.ShapeDtypeStruct((nb, w), F32),
            jax.ShapeDtypeStruct((CONV_W - 1, nb, w), F32),
        ],
        scratch_shapes=[pltpu.VMEM((nb, w), F32), pltpu.VMEM((nb, w), F32)],
        compiler_params=_params(1, 32),
        name=name,
    )(z, z, conv, h0, cw, row(cb), wgate, row(ba), row(bx), row(lam), row(gout))


def _mlstm_step_body(in_refs, out_refs, scratch_refs, part, *, bs, parts):
    bi_ref, bf_ref, z_ref, zg_ref, g_ref, c0_ref, n0_ref, m0_ref = in_refs
    y_ref, c_ref, n_ref, m_ref = out_refs
    qc_ref, = scratch_refs
    dk, dv = ML_DK, ML_DV
    nr = bs // parts
    rows = slice(part * nr, (part + 1) * nr)
    eye = (lax.broadcasted_iota(jnp.int32, (dk, dk), 0)
           == lax.broadcasted_iota(jnp.int32, (dk, dk), 1))

    def as_column(row):
        return jnp.sum(jnp.where(eye, jnp.broadcast_to(row, (dk, dk)), 0.0), axis=1, keepdims=True)

    zg = zg_ref[rows, :]
    for h in range(ML_HEADS):
        li = zg[:, h:h + 1] + bi_ref[h]
        lf = _log_sigmoid(zg[:, ML_HEADS + h:ML_HEADS + h + 1] + bf_ref[h])
        m = m0_ref[rows, h:h + 1]
        inter = lf + m
        m_t = jnp.maximum(inter, li)
        dgate = jnp.exp(li - m_t)
        sc = jnp.exp(inter - m_t)
        q = z_ref[rows, OFF_Q + h * dk:OFF_Q + (h + 1) * dk]
        k = z_ref[rows, OFF_K + h * dk:OFF_K + (h + 1) * dk] * (ML_DK ** -0.5)
        v = z_ref[rows, OFF_V + h * dv:OFF_V + (h + 1) * dv]
        n_old = n0_ref[rows, h, :]
        qk = jnp.sum(q * k, axis=1, keepdims=True) * dgate
        w_end = jnp.exp(li - m_t)
        dec = jnp.exp(inter - m_t)
        wk = w_end * k
        for j in range(nr):
            c_old = c0_ref[part * nr + j, h]
            qc_ref[j:j + 1, :] = jnp.sum(as_column(q[j:j + 1, :]) * c_old, axis=0, keepdims=True)
            c_ref[part * nr + j, h] = (dec[j:j + 1, :] * c_old
                                       + as_column(wk[j:j + 1, :]) * v[j:j + 1, :])
        num = sc * qc_ref[0:nr, :] + qk * v
        den = sc * jnp.sum(q * n_old, axis=1, keepdims=True) + qk
        den = jnp.maximum(jnp.abs(den), jnp.exp(-m_t))
        hh = num / den
        n_ref[rows, h, :] = dec * n_old + wk
        m_ref[rows, h:h + 1] = m_t
        y = (_rms(hh, g_ref[:, h * dv:(h + 1) * dv])
             * _sigmoid(z_ref[rows, OFF_O + h * dv:OFF_O + (h + 1) * dv]))
        y_ref[rows, h * dv:(h + 1) * dv] = y.astype(y_ref.dtype)


def _mlstm_step_job(z, zg, c0, n0, m0, b_i, b_f, g_out, *, bs, parts):
    nb = z.shape[0]
    assert nb % bs == 0 and bs % parts == 0
    dk, dv, nh = ML_DK, ML_DV, ML_HEADS
    smem = pl.BlockSpec(memory_space=pltpu.SMEM)

    def in_specs(blk):
        return [
            smem, smem,
            pl.BlockSpec((bs, IN_MAIN), lambda *g: (blk(*g), 0)),
            pl.BlockSpec((bs, V7X_LANES), lambda *g: (blk(*g), 0)),
            pl.BlockSpec((1, nh * dv), lambda *g: (0, 0)),
            pl.BlockSpec((bs, nh, dk, dv), lambda *g: (blk(*g), 0, 0, 0)),
            pl.BlockSpec((bs, nh, dk), lambda *g: (blk(*g), 0, 0)),
            pl.BlockSpec((bs, nh), lambda *g: (blk(*g), 0)),
        ]

    def out_specs(blk):
        return [
            pl.BlockSpec((bs, nh * dv), lambda *g: (blk(*g), 0)),
            pl.BlockSpec((bs, nh, dk, dv), lambda *g: (blk(*g), 0, 0, 0)),
            pl.BlockSpec((bs, nh, dk), lambda *g: (blk(*g), 0, 0)),
            pl.BlockSpec((bs, nh), lambda *g: (blk(*g), 0)),
        ]

    return SideJob(
        n_blocks=nb // bs,
        parts=parts,
        args=(b_i, b_f, z, zg, g_out.reshape(1, nh * dv), c0, n0, m0),
        in_specs=in_specs,
        out_shape=(
            jax.ShapeDtypeStruct((nb, nh * dv), F32),
            jax.ShapeDtypeStruct((nb, nh, dk, dv), F32),
            jax.ShapeDtypeStruct((nb, nh, dk), F32),
            jax.ShapeDtypeStruct((nb, nh), F32),
        ),
        out_specs=out_specs,
        scratch_shapes=(pltpu.VMEM((bs, dv), F32),),
        body=functools.partial(_mlstm_step_body, bs=bs, parts=parts),
    )


def _run_side_job(job, *, vmem_mib, name):
    n_in, n_out = len(job.args), len(job.out_shape)

    def body(*refs):
        for part in range(job.parts):
            job.body(refs[:n_in], refs[n_in:n_in + n_out], refs[n_in + n_out:], part)

    block_of = lambda i: i
    return pl.pallas_call(
        body,
        grid=(job.n_blocks,),
        in_specs=job.in_specs(block_of),
        out_specs=job.out_specs(block_of),
        out_shape=list(job.out_shape),
        scratch_shapes=list(job.scratch_shapes),
        compiler_params=_params(1, vmem_mib),
        name=name,
    )(*job.args)


def _softmax_rows(s):
    e = jnp.exp(s - jnp.max(s, axis=-1, keepdims=True))
    return e / jnp.sum(e, axis=-1, keepdims=True)


def _xattn_block_kernel(*refs, n_mix):
    x_ref = refs[0]
    y_refs = refs[1:1 + n_mix]
    wy_refs = refs[1 + n_mix:1 + 2 * n_mix]
    g_ref, wq_hbm, k_ref, v_ref, wo_hbm, o_ref, wq_ref, wo_ref, sem = refs[1 + 2 * n_mix:]
    first = (pl.program_id(0) == 0) & (pl.program_id(1) == 0)
    wq_copy = pltpu.make_async_copy(wq_hbm, wq_ref, sem.at[0])
    wo_copy = pltpu.make_async_copy(wo_hbm, wo_ref, sem.at[1])

    @pl.when(first)
    def _():
        wq_copy.start()
        wo_copy.start()

    x = x_ref[...]
    for y_ref, wy_ref in zip(y_refs, wy_refs):
        x = x + jnp.dot(y_ref[...], wy_ref[...], preferred_element_type=F32)

    @pl.when(first)
    def _():
        wq_copy.wait()
        wo_copy.wait()

    xq = jnp.dot(_rms(x, g_ref[...]).astype(BF16), wq_ref[...],
                 preferred_element_type=F32).astype(BF16)
    heads = []
    for h in range(XA_HEADS):
        sl = slice(h * XA_DH, (h + 1) * XA_DH)
        s = lax.dot_general(xq[:, sl], k_ref[:, sl].astype(BF16),
                            (((1,), (1,)), ((), ())), preferred_element_type=F32)
        p = _softmax_rows(s * (XA_DH ** -0.5))
        heads.append(_mm(p, v_ref[:, sl]).astype(BF16))
    o_ref[...] = x + jnp.dot(jnp.concatenate(heads, axis=1), wo_ref[...],
                             preferred_element_type=F32)


def _xattn_block(x, mix_parts, mix_weights16, g, wq16, mk, mv, wo16, *, batch, seq, tq, name):
    nt = seq // tq
    assert seq % tq == 0 and all(w.dtype == BF16 for w in (*mix_weights16, wq16, wo16))
    assert all(p.dtype == BF16 for p in mix_parts)
    d = D_MODEL
    resident = dict(pipeline_mode=pl.Buffered(1))
    rows = lambda width: pl.BlockSpec((tq, width), lambda b, t: (b * nt + t, 0))
    return pl.pallas_call(
        functools.partial(_xattn_block_kernel, n_mix=len(mix_parts)),
        grid=(batch, nt),
        in_specs=[
            rows(d),
            *[rows(p.shape[1]) for p in mix_parts],
            *[pl.BlockSpec(w.shape, lambda b, t: (0, 0), **resident) for w in mix_weights16],
            pl.BlockSpec((1, d), lambda b, t: (0, 0)),
            pl.BlockSpec(memory_space=pl.ANY),
            pl.BlockSpec((N_MEM, d), lambda b, t: (b, 0)),
            pl.BlockSpec((N_MEM, d), lambda b, t: (b, 0)),
            pl.BlockSpec(memory_space=pl.ANY),
        ],
        out_specs=rows(d),
        out_shape=jax.ShapeDtypeStruct((batch * seq, d), F32),
        scratch_shapes=[pltpu.VMEM((d, d), BF16), pltpu.VMEM((d, d), BF16),
                        pltpu.SemaphoreType.DMA((2,))],
        compiler_params=_params(2, 60),
        name=name,
    )(x, *mix_parts, *mix_weights16, g.reshape(1, d), wq16, mk, mv, wo16)


def _xattn_own_mask():
    nrow = N_MEM * XA_HEADS
    col_head = lax.broadcasted_iota(jnp.int32, (V7X_SUBLANES, nrow), 1) & (XA_HEADS - 1)
    row_head = lax.broadcasted_iota(jnp.int32, (V7X_SUBLANES, nrow), 0) & (XA_HEADS - 1)
    return col_head == row_head


def _xattn_probs(q_ref, k_ref, j, own):
    kf = k_ref[j].reshape(N_MEM * XA_HEADS, XA_DH)
    q8 = jnp.concatenate([q_ref[j]] * (V7X_SUBLANES // XA_HEADS), axis=0)
    s = lax.dot_general(q8.astype(BF16), kf.astype(BF16), (((1,), (1,)), ((), ())),
                        preferred_element_type=F32)
    s = jnp.where(own, s * (XA_DH ** -0.5), NEG)
    e = jnp.where(own, jnp.exp(s - jnp.max(s, axis=-1, keepdims=True)), 0.0)
    return e / jnp.sum(e, axis=-1, keepdims=True)


def _xattn_values(v_ref, o_ref, j, p):
    vf = v_ref[j].reshape(N_MEM * XA_HEADS, XA_DH)
    o_ref[j] = _mm(p, vf)[0:XA_HEADS, :]


def _xattn_step_kernel(q_ref, k_ref, v_ref, o_ref, *, sb):
    own = _xattn_own_mask()
    for j in range(sb):
        _xattn_values(v_ref, o_ref, j, _xattn_probs(q_ref, k_ref, j, own))


def _xattn_step(xq, ck, cv, *, sb, name):
    nb = xq.shape[0]
    assert XA_HEADS & (XA_HEADS - 1) == 0
    assert nb % sb == 0 and ck.shape == (nb, N_MEM, XA_HEADS, XA_DH)
    cache_spec = pl.BlockSpec((sb, N_MEM, XA_HEADS, XA_DH), lambda i: (i, 0, 0, 0))
    q_spec = pl.BlockSpec((sb, XA_HEADS, XA_DH), lambda i: (i, 0, 0))
    return pl.pallas_call(
        functools.partial(_xattn_step_kernel, sb=sb),
        grid=(nb // sb,),
        in_specs=[q_spec, cache_spec, cache_spec],
        out_specs=q_spec,
        out_shape=jax.ShapeDtypeStruct((nb, XA_HEADS, XA_DH), F32),
        compiler_params=_params(1, 40),
        name=name,
    )(xq.reshape(nb, XA_HEADS, XA_DH), ck, cv).reshape(nb, XA_HEADS * XA_DH)


class Tiles(NamedTuple):
    rows: int
    attn_rows: int
    mem_rows: int
    cols: int
    ffn_cols16: int
    ffn_cols32: int
    step_block: int
    step_parts: int
    xattn_rows: int


def _plan_tiles(prompt_rows, mem_rows):
    rows = min(prompt_rows, 1024)
    assert prompt_rows % rows == 0
    return Tiles(rows=rows, attn_rows=min(rows, 512), mem_rows=min(mem_rows, 512), cols=1024,
                 ffn_cols16=512, ffn_cols32=256, step_block=V7X_SUBLANES, step_parts=2,
                 xattn_rows=2)


def kernel(x_prompt, x_sample, mem_prompt, state_rg_h, state_rg_conv, state_ml_C, state_ml_n, state_ml_m, cache_mem_k, cache_mem_v, g_mix, w_in, conv_w, conv_b, w_rg_a, b_rg_a, w_rg_x, b_rg_x, rg_lambda, b_ml_i, b_ml_f, g_rg_out, g_ml_out, w_out, g_xa, g_mem, w_xa_q, w_xa_k, w_xa_v, w_xa_o, g_ffn, w_ffn_gate, w_ffn_up, w_ffn_down, g_final):
    depth = g_mix.shape[0]
    assert depth == 1, "single trunk layer"
    bp, seq, d = x_prompt.shape
    bs_, dec_seq, _ = x_sample.shape
    assert d == D_MODEL and dec_seq == 1
    n_mem = mem_prompt.shape[1]
    assert n_mem == N_MEM
    dff = w_ffn_gate.shape[-1]
    in_w = w_in.shape[-1]
    assert in_w == IN_MAIN + N_GATE

    w_in_t = jnp.swapaxes(w_in, 1, 2).reshape(in_w, d)
    w_gate_pad = jnp.pad(w_in_t[IN_MAIN:], ((0, V7X_LANES - N_GATE), (0, 0)))
    cw = conv_w.reshape(CONV_W, RG_WIDTH)
    wgate = jnp.concatenate([w_rg_a.reshape(RG_BLOCKS, RG_BLOCK, RG_BLOCK),
                             w_rg_x.reshape(RG_BLOCKS, RG_BLOCK, RG_BLOCK)], axis=-1)
    rg_args = (cw, conv_b.reshape(-1), wgate, b_rg_a.reshape(-1), b_rg_x.reshape(-1),
               rg_lambda.reshape(-1), g_rg_out.reshape(-1))
    b_i = b_ml_i.reshape(ML_HEADS)
    b_f = b_ml_f.reshape(ML_HEADS)
    g_ml = g_ml_out.reshape(-1)
    w_out2 = w_out.reshape(d, d)
    w_q = w_xa_q.reshape(d, d)
    w_k = w_xa_k.reshape(d, d)
    w_v = w_xa_v.reshape(d, d)
    w_o = w_xa_o.reshape(d, d)
    w_fg = w_ffn_gate.reshape(d, dff)
    w_fu = w_ffn_up.reshape(d, dff)
    w_fd = w_ffn_down.reshape(dff, d)

    tiles = _plan_tiles(bp * seq, bp * n_mem)
    row_tile = tiles.rows
    rg_zero_conv = jnp.zeros((bp, CONV_W - 1, RG_WIDTH), F32)
    rg_zero_h = jnp.zeros((bp, RG_WIDTH), F32)

    xs = x_sample.reshape(bs_, d)
    z_s, zg_s, w_in16 = _norm_linear(
        xs, g_mix.reshape(-1), w_in_t, n_out=IN_MAIN, tm=bs_, tn=tiles.cols, w_gate=w_gate_pad,
        w_is_nk=True, emit_w16=True, name="in_proj_s")
    conv_taps = jnp.swapaxes(state_rg_conv.reshape(bs_, CONV_W - 1, RG_WIDTH), 0, 1)
    y_rg_s, s_h, s_conv_taps = _rglru_step(z_s, conv_taps, state_rg_h.reshape(bs_, RG_WIDTH),
                                           *rg_args, name="rglru_s")
    s_conv = jnp.swapaxes(s_conv_taps, 0, 1)
    mlstm_s_job = _mlstm_step_job(
        z_s, zg_s, state_ml_C.reshape(bs_, ML_HEADS, ML_DK, ML_DV),
        state_ml_n.reshape(bs_, ML_HEADS, ML_DK), state_ml_m.reshape(bs_, ML_HEADS),
        b_i, b_f, g_ml, bs=tiles.step_block, parts=tiles.step_parts)
    tp = bp * seq
    xp = x_prompt.reshape(tp, d)
    in_cols, in_rows = IN_MAIN // tiles.cols, tp // row_tile
    if (mlstm_s_job.n_blocks % in_rows == 0
            and mlstm_s_job.n_blocks // in_rows * mlstm_s_job.parts < in_cols):
        z_p, zg_p, y_ml_s, s_c, s_n, s_m = _norm_linear(
            xp, g_mix.reshape(-1), w_in16, n_out=IN_MAIN, tm=row_tile, tn=tiles.cols,
            w_gate=w_gate_pad, w_is_nk=True, side=mlstm_s_job, x_parts=4, vmem_mib=60,
            name="in_proj_p")
    else:
        z_p, zg_p = _norm_linear(xp, g_mix.reshape(-1), w_in16, n_out=IN_MAIN, tm=row_tile,
                                 tn=tiles.cols, w_gate=w_gate_pad, w_is_nk=True, name="in_proj_p")
        y_ml_s, s_c, s_n, s_m = _run_side_job(mlstm_s_job, vmem_mib=32, name="mlstm_s")
    x1_s, w_out16_rg, w_out16_ml = _linear_residual(
        [y_rg_s, y_ml_s], [(w_out2, 0), (w_out2, 1)], xs, tm=bs_, tn=tiles.cols, emit_w16=True,
        name="mix_out_s")
    xq_s, w_q16 = _norm_linear(x1_s, g_xa.reshape(-1), w_q, n_out=d, tm=bs_, tn=tiles.cols,
                               emit_w16=True, name="xa_q_s")
    ck = cache_mem_k.reshape(bs_, n_mem, XA_HEADS, XA_DH)
    cv = cache_mem_v.reshape(bs_, n_mem, XA_HEADS, XA_DH)

    mix_tl = min(seq, ML_CHUNK)
    mix_steps = bp * (seq // mix_tl)
    mix_args = (z_p, zg_p, rg_zero_conv, rg_zero_h, *rg_args, b_i, b_f, g_ml)
    if bs_ % mix_steps == 0 and RG_BLOCKS % (bs_ // mix_steps) == 0:
        y_rg_p, p_h, p_conv, y_ml_p, p_c, p_n, p_m, o_s = _mixer_seq(
            *mix_args, batch=bp, seq=seq, tl=mix_tl, side_xattn=(xq_s, ck, cv), name="mixer_p")
        o_s = o_s.reshape(bs_, d)
    else:
        y_rg_p, p_h, p_conv, y_ml_p, p_c, p_n, p_m = _mixer_seq(
            *mix_args, batch=bp, seq=seq, tl=mix_tl, name="mixer_p")
        o_s = _xattn_step(xq_s, ck, cv, sb=tiles.xattn_rows, name="xattn_s")

    x3_s, w_o16 = _linear_residual([o_s], [(w_o, 0)], x1_s, tm=bs_, tn=tiles.cols, emit_w16=True,
                                   name="xa_out_s")

    mem2 = mem_prompt.reshape(bp * n_mem, d)
    tmem = tiles.mem_rows
    mk, mk_heads = _norm_linear(mem2, g_mem.reshape(-1), w_k, n_out=d, tm=tmem, tn=d,
                                out_dtype=BF16, heads_out=(XA_HEADS, XA_DH), name="mem_k")
    mv, mv_heads = _norm_linear(mem2, g_mem.reshape(-1), w_v, n_out=d, tm=tmem, tn=d,
                                out_dtype=BF16, heads_out=(XA_HEADS, XA_DH), name="mem_v")
    x3_p = _xattn_block(xp, [y_rg_p, y_ml_p], [w_out16_rg, w_out16_ml], g_xa.reshape(-1), w_q16,
                        mk, mv, w_o16, batch=bp, seq=seq, tq=min(seq, tiles.attn_rows), name="xattn_p")
    y_s, y_p0, w_fg16, w_fu16, w_fd16 = _ffn(
        [(x3_s, bs_), (x3_p, row_tile)], g_ffn.reshape(-1), w_fg, w_fu, w_fd, g_final, n_rows=1,
        tf=tiles.ffn_cols32, emit_w16=True, vmem_mib=60, name="ffn_s")
    if tp // row_tile > 1:
        y_p = _ffn([(x3_p, row_tile)], g_ffn.reshape(-1), w_fg16, w_fu16, w_fd16, g_final,
                   n_rows=tp // row_tile, tf=tiles.ffn_cols16, head_tile=y_p0, x_parts=4,
                   vmem_mib=60, name="ffn_p")
    else:
        y_p = y_p0

    return (
        y_p.reshape(bp, seq, d),
        y_s.reshape(bs_, 1, d),
        p_h.reshape(1, bp, RG_WIDTH),
        p_conv.reshape(1, bp, CONV_W - 1, RG_WIDTH),
        p_c.reshape(1, bp, ML_HEADS, ML_DK, ML_DV),
        p_n.reshape(1, bp, ML_HEADS, ML_DK),
        p_m[:, :, 0].reshape(1, bp, ML_HEADS),
        mk_heads.reshape(1, bp, n_mem, XA_HEADS, XA_DH),
        mv_heads.reshape(1, bp, n_mem, XA_HEADS, XA_DH),
        s_h.reshape(1, bs_, RG_WIDTH),
        s_conv.reshape(1, bs_, CONV_W - 1, RG_WIDTH),
        s_c.reshape(1, bs_, ML_HEADS, ML_DK, ML_DV),
        s_n.reshape(1, bs_, ML_HEADS, ML_DK),
        s_m.reshape(1, bs_, ML_HEADS),
    )
```
